```python
import math
import jax
import jax.numpy as jnp
from jax import lax
import numpy as np

D_MODEL = 1024
BATCH = 8
SEQ = 4096
DEPTH = 4

DN_HEADS = 8
DN_HEAD_DIM = 128
DN_WIDTH = DN_HEADS * DN_HEAD_DIM
DN_CONV = 5
DN_CHUNK = 64
SW_HEADS = 16
SW_KV_HEADS = 4
SW_HEAD_DIM = 64
SW_WINDOW = 128
SW_BLOCK = 128
ROPE_THETA = 10000.0
FFN_HIDDEN = ((8 * D_MODEL // 3 + 255) // 256) * 256
DN_ALPHA = (2.0 * DEPTH) ** 0.25
DN_BETA = (8.0 * DEPTH) ** -0.25
LN_EPS = 1e-5
RMS_EPS = 1e-6

COL_SIZES = (
    3 * DN_WIDTH,
    DN_WIDTH,
    2 * DN_HEADS,
    2 * DN_HEADS,
    SW_HEADS * SW_HEAD_DIM,
    SW_KV_HEADS * SW_HEAD_DIM,
    SW_KV_HEADS * SW_HEAD_DIM,
    2 * D_MODEL,
)
IN_COLS = sum(COL_SIZES)

kernel_name = "hybrid_deltanet_swa_deepnorm_encoder"


def layer_norm(x, g, b):
    xf = x.astype(jnp.float32)
    mu = jnp.mean(xf, axis=-1, keepdims=True)
    var = jnp.mean(jnp.square(xf - mu), axis=-1, keepdims=True)
    y = (xf - mu) * lax.rsqrt(var + LN_EPS) * g.astype(jnp.float32) + b.astype(jnp.float32)
    return y.astype(x.dtype)


def l2_normalize(x):
    xf = x.astype(jnp.float32)
    return xf * lax.rsqrt(jnp.sum(xf * xf, axis=-1, keepdims=True) + RMS_EPS)


def depthwise_conv_centred(x, w):
    c = x.shape[-1]
    half = DN_CONV // 2
    return lax.conv_general_dilated(
        x, w[:, None, :].astype(x.dtype), window_strides=(1,),
        padding=[(half, half)], dimension_numbers=("NWC", "WIO", "NWC"),
        feature_group_count=c)


def rope_tables(seq_len):
    half = SW_HEAD_DIM // 2
    inv_freq = ROPE_THETA ** (-jnp.arange(half, dtype=jnp.float32) / half)
    pos = jnp.arange(seq_len, dtype=jnp.float32)
    ang = pos[:, None] * inv_freq[None, :]
    return jnp.cos(ang), jnp.sin(ang)


def apply_rope(x, cos, sin):
    x1, x2 = jnp.split(x, 2, axis=-1)
    c = cos[None, :, None, :].astype(x.dtype)
    s = sin[None, :, None, :].astype(x.dtype)
    return jnp.concatenate([x1 * c - x2 * s, x2 * c + x1 * s], axis=-1)


def chunk_gated_delta_rule(q, k, v, g, beta):
    b_, t_, h_, dk = q.shape
    dv = v.shape[-1]
    n_chunks = t_ // DN_CHUNK
    f32 = jnp.float32

    def chunks(a):
        a = a.astype(f32).reshape((b_, n_chunks, DN_CHUNK, h_) + a.shape[3:])
        return jnp.moveaxis(a, 3, 1)

    q, k, v, beta = chunks(q), chunks(k), chunks(v), chunks(beta)
    g = jnp.cumsum(chunks(g), axis=-1)
    idx = jnp.arange(DN_CHUNK)
    lower_incl = idx[:, None] >= idx[None, :]
    strict = idx[:, None] > idx[None, :]
    decay = jnp.exp(jnp.where(lower_incl, g[..., :, None] - g[..., None, :], -jnp.inf))

    kk = jnp.einsum("bhnid,bhnjd->bhnij", k, k)
    eye = jnp.eye(DN_CHUNK, dtype=f32)
    lmat = jnp.where(strict, beta[..., :, None] * kk * decay, 0.0) + eye
    t_inv = lax.linalg.triangular_solve(
        lmat, jnp.broadcast_to(eye, lmat.shape), left_side=True, lower=True,
        unit_diagonal=True)
    u = jnp.einsum("bhnij,bhnjd->bhnid", t_inv, v * beta[..., None])
    w = jnp.einsum("bhnij,bhnjd->bhnid", t_inv, k * (beta * jnp.exp(g))[..., None])
    qk = jnp.einsum("bhnid,bhnjd->bhnij", q, k) * decay
    q_dec = q * jnp.exp(g)[..., None]
    g_last = g[..., -1]
    k_dec = k * jnp.exp(g_last[..., None] - g)[..., None]

    xs = tuple(jnp.moveaxis(a, 2, 0) for a in (u, w, qk, q_dec, k_dec, g_last))

    def step(state, inp):
        u_n, w_n, qk_n, qd_n, kd_n, gl_n = inp
        v_new = u_n - jnp.einsum("bhcd,bhde->bhce", w_n, state)
        o_n = (jnp.einsum("bhcd,bhde->bhce", qd_n, state)
               + jnp.einsum("bhij,bhje->bhie", qk_n, v_new))
        state = (state * jnp.exp(gl_n)[..., None, None]
                 + jnp.einsum("bhcd,bhce->bhde", kd_n, v_new))
        return state, o_n

    s0 = jnp.zeros((b_, h_, dk, dv), f32)
    _, o = lax.scan(step, s0, xs)
    return jnp.transpose(o, (1, 0, 3, 2, 4)).reshape(b_, t_, h_, dv)


def windowed_gqa_sink(q, k, v, sinks):
    b_, t_, hq, d = q.shape
    hkv = k.shape[2]
    grp = hq // hkv
    nb = t_ // SW_BLOCK
    pad = ((0, 0), (SW_BLOCK, SW_BLOCK), (0, 0), (0, 0))

    def bands(a):
        ab = jnp.pad(a, pad).reshape(b_, nb + 2, SW_BLOCK, hkv, d)
        return jnp.concatenate([ab[:, :-2], ab[:, 1:-1], ab[:, 2:]], axis=2)

    kb, vb = bands(k), bands(v)
    qb = q.reshape(b_, nb, SW_BLOCK, hkv, grp, d)
    s = jnp.einsum("bnqhgd,bnkhd->bnhgqk", qb, kb).astype(jnp.float32) * (d ** -0.5)
    blk = jnp.arange(nb)[:, None, None] * SW_BLOCK
    qpos = blk + jnp.arange(SW_BLOCK)[None, :, None]
    kpos = blk - SW_BLOCK + jnp.arange(3 * SW_BLOCK)[None, None, :]
    valid = (jnp.abs(qpos - kpos) <= SW_WINDOW) & (kpos >= 0) & (kpos < t_)
    s = jnp.where(valid[None, :, None, None], s, -jnp.inf)
    sink = jnp.broadcast_to(
        sinks.astype(jnp.float32).reshape(hkv, grp)[None, None, :, :, None, None],
        s.shape[:-1] + (1,))
    p = jax.nn.softmax(jnp.concatenate([s, sink], axis=-1), axis=-1)[..., :-1]
    o = jnp.einsum("bnhgqk,bnkhd->bnqhgd", p.astype(v.dtype), vb)
    return o.reshape(b_, t_, hq * d)


def token_mixing(x, w_in, conv_w, a_log, dt_bias, dn_norm_w, sinks,
                 w_branch_a, w_branch_b, w_out, cos, sin):
    b_, t_, _ = x.shape
    proj = jnp.einsum("btd,de->bte", x, w_in)
    offs = [sum(COL_SIZES[:i + 1]) for i in range(len(COL_SIZES) - 1)]
    qkv_dn, z, b_dn, a_dn, q_sw, k_sw, v_sw, gates = jnp.split(proj, offs, axis=-1)

    qkv_dn = jax.nn.silu(depthwise_conv_centred(qkv_dn, conv_w))
    q_dn, k_dn, v_dn = [a.reshape(b_, t_, DN_HEADS, DN_HEAD_DIM)
                        for a in jnp.split(qkv_dn, 3, axis=-1)]
    q_dn = l2_normalize(q_dn) * (DN_HEAD_DIM ** -0.5)
    k_dn = l2_normalize(k_dn)
    beta = jax.nn.sigmoid(b_dn.astype(jnp.float32)).reshape(b_, t_, 2, DN_HEADS)
    g = -jnp.exp(a_log.astype(jnp.float32)) * jax.nn.softplus(
        a_dn.astype(jnp.float32).reshape(b_, t_, 2, DN_HEADS) + dt_bias.astype(jnp.float32))
    o_fwd = chunk_gated_delta_rule(q_dn, k_dn, v_dn, g[:, :, 0], beta[:, :, 0])
    flip = lambda a: jnp.flip(a, axis=1)
    o_bwd = flip(chunk_gated_delta_rule(flip(q_dn), flip(k_dn), flip(v_dn),
                                        flip(g[:, :, 1]), flip(beta[:, :, 1])))
    o_dn = o_fwd + o_bwd
    o_dn = (o_dn * lax.rsqrt(jnp.mean(o_dn * o_dn, axis=-1, keepdims=True) + RMS_EPS)
            * dn_norm_w.astype(jnp.float32)
            * jax.nn.silu(z.astype(jnp.float32)).reshape(b_, t_, DN_HEADS, DN_HEAD_DIM))
    o_dn = o_dn.reshape(b_, t_, DN_WIDTH).astype(x.dtype)

    q_sw = apply_rope(q_sw.reshape(b_, t_, SW_HEADS, SW_HEAD_DIM), cos, sin)
    k_sw = apply_rope(k_sw.reshape(b_, t_, SW_KV_HEADS, SW_HEAD_DIM), cos, sin)
    v_sw = v_sw.reshape(b_, t_, SW_KV_HEADS, SW_HEAD_DIM)
    o_sw = windowed_gqa_sink(q_sw, k_sw, v_sw, sinks)

    gate_a, gate_b = jnp.split(jax.nn.sigmoid(gates), 2, axis=-1)
    merged = (gate_a * jnp.einsum("bte,ed->btd", o_dn, w_branch_a)
              + gate_b * jnp.einsum("bte,ed->btd", o_sw, w_branch_b))
    return jnp.einsum("btd,de->bte", merged, w_out)


def swiglu_ffn(x, w_gate_up, w_down):
    gu = jnp.einsum("btd,df->btf", x, w_gate_up)
    gate, up = jnp.split(gu, 2, axis=-1)
    return jnp.einsum("btf,fd->btd", jax.nn.silu(gate) * up, w_down)


def _fwd_setup_inputs(seed: int = 0) -> dict:
    key = jax.random.key(seed)
    ks = jax.random.split(key, 20)
    f32 = jnp.float32
    nrm = lambda k, shape, scale: jax.random.normal(k, shape, f32) * scale
    x = jax.random.normal(ks[0], (BATCH, SEQ, D_MODEL), f32)
    w_in = nrm(ks[1], (DEPTH, D_MODEL, IN_COLS), D_MODEL ** -0.5)
    conv_w = nrm(ks[2], (DEPTH, DN_CONV, 3 * DN_WIDTH), DN_CONV ** -0.5)
    a_log = jnp.log(jax.random.uniform(ks[3], (DEPTH, 2, DN_HEADS), f32, 1.0, 16.0))
    dt = jnp.exp(jax.random.uniform(ks[4], (DEPTH, 2, DN_HEADS), f32,
                                    math.log(1e-3), math.log(1e-1)))
    dt_bias = dt + jnp.log(-jnp.expm1(-dt))
    dn_norm_w = 1.0 + nrm(ks[5], (DEPTH, DN_HEAD_DIM), 0.02)
    sinks = nrm(ks[6], (DEPTH, SW_HEADS), 0.5)
    w_branch_a = nrm(ks[7], (DEPTH, DN_WIDTH, D_MODEL), DN_WIDTH ** -0.5)
    w_branch_b = nrm(ks[8], (DEPTH, SW_HEADS * SW_HEAD_DIM, D_MODEL),
                     (SW_HEADS * SW_HEAD_DIM) ** -0.5)
    w_out = nrm(ks[9], (DEPTH, D_MODEL, D_MODEL), D_MODEL ** -0.5 * DN_BETA)
    ln1_g = 1.0 + nrm(ks[10], (DEPTH, D_MODEL), 0.02)
    ln1_b = nrm(ks[11], (DEPTH, D_MODEL), 0.02)
    w_gate_up = nrm(ks[12], (DEPTH, D_MODEL, 2 * FFN_HIDDEN), D_MODEL ** -0.5)
    w_down = nrm(ks[13], (DEPTH, FFN_HIDDEN, D_MODEL), FFN_HIDDEN ** -0.5 * DN_BETA)
    ln2_g = 1.0 + nrm(ks[14], (DEPTH, D_MODEL), 0.02)
    ln2_b = nrm(ks[15], (DEPTH, D_MODEL), 0.02)
    return {"x": x, "w_in": w_in, "conv_w": conv_w, "a_log": a_log,
            "dt_bias": dt_bias, "dn_norm_w": dn_norm_w, "sinks": sinks,
            "w_branch_a": w_branch_a, "w_branch_b": w_branch_b, "w_out": w_out,
            "ln1_g": ln1_g, "ln1_b": ln1_b, "w_gate_up": w_gate_up,
            "w_down": w_down, "ln2_g": ln2_g, "ln2_b": ln2_b}


def _fwd_reference(x, w_in, conv_w, a_log, dt_bias, dn_norm_w, sinks, w_branch_a,
              w_branch_b, w_out, ln1_g, ln1_b, w_gate_up, w_down, ln2_g, ln2_b):
    cos, sin = rope_tables(x.shape[1])
    for l in range(DEPTH):
        mix = token_mixing(x, w_in[l], conv_w[l], a_log[l], dt_bias[l], dn_norm_w[l],
                           sinks[l], w_branch_a[l], w_branch_b[l], w_out[l], cos, sin)
        x = layer_norm(DN_ALPHA * x + mix, ln1_g[l], ln1_b[l])
        x = layer_norm(DN_ALPHA * x + swiglu_ffn(x, w_gate_up[l], w_down[l]), ln2_g[l], ln2_b[l])
    return x


import jax as _jax
import jax.numpy as _jnp

TWIN_FORMAT = 'train_step'
FWD_PARAMS = ['x', 'w_in', 'conv_w', 'a_log', 'dt_bias', 'dn_norm_w', 'sinks', 'w_branch_a', 'w_branch_b', 'w_out', 'ln1_g', 'ln1_b', 'w_gate_up', 'w_down', 'ln2_g', 'ln2_b']
TWIN_WEIGHTS = ['w_in', 'conv_w', 'a_log', 'dt_bias', 'dn_norm_w', 'sinks', 'w_branch_a', 'w_branch_b', 'w_out', 'ln1_g', 'ln1_b', 'w_gate_up', 'w_down', 'ln2_g', 'ln2_b']
TWIN_DIFF_INPUT = 'x'
TWIN_INPUTS = ['x', 'w_in', 'conv_w', 'a_log', 'dt_bias', 'dn_norm_w', 'sinks', 'w_branch_a', 'w_branch_b', 'w_out', 'ln1_g', 'ln1_b', 'w_gate_up', 'w_down', 'ln2_g', 'ln2_b', 'loss_target', 'm_w_in', 'm_conv_w', 'm_a_log', 'm_dt_bias', 'm_dn_norm_w', 'm_sinks', 'm_w_branch_a', 'm_w_branch_b', 'm_w_out', 'm_ln1_g', 'm_ln1_b', 'm_w_gate_up', 'm_w_down', 'm_ln2_g', 'm_ln2_b', 'v_w_in', 'v_conv_w', 'v_a_log', 'v_dt_bias', 'v_dn_norm_w', 'v_sinks', 'v_w_branch_a', 'v_w_branch_b', 'v_w_out', 'v_ln1_g', 'v_ln1_b', 'v_w_gate_up', 'v_w_down', 'v_ln2_g', 'v_ln2_b']
TWIN_OUTPUTS = ['loss', 'grad_x', 'grad_w_in', 'grad_conv_w', 'grad_a_log', 'grad_dt_bias', 'grad_dn_norm_w', 'grad_sinks', 'grad_w_branch_a', 'grad_w_branch_b', 'grad_w_out', 'grad_ln1_g', 'grad_ln1_b', 'grad_w_gate_up', 'grad_w_down', 'grad_ln2_g', 'grad_ln2_b', 'delta_w_in', 'delta_conv_w', 'delta_a_log', 'delta_dt_bias', 'delta_dn_norm_w', 'delta_sinks', 'delta_w_branch_a', 'delta_w_branch_b', 'delta_w_out', 'delta_ln1_g', 'delta_ln1_b', 'delta_w_gate_up', 'delta_w_down', 'delta_ln2_g', 'delta_ln2_b', 'new_m_w_in', 'new_m_conv_w', 'new_m_a_log', 'new_m_dt_bias', 'new_m_dn_norm_w', 'new_m_sinks', 'new_m_w_branch_a', 'new_m_w_branch_b', 'new_m_w_out', 'new_m_ln1_g', 'new_m_ln1_b', 'new_m_w_gate_up', 'new_m_w_down', 'new_m_ln2_g', 'new_m_ln2_b', 'new_v_w_in', 'new_v_conv_w', 'new_v_a_log', 'new_v_dt_bias', 'new_v_dn_norm_w', 'new_v_sinks', 'new_v_w_branch_a', 'new_v_w_branch_b', 'new_v_w_out', 'new_v_ln1_g', 'new_v_ln1_b', 'new_v_w_gate_up', 'new_v_w_down', 'new_v_ln2_g', 'new_v_ln2_b']
TWIN_LEAF_KINDS = {'loss': 'loss', 'grad_x': 'grad_x', 'grad_w_in': 'grad_w', 'grad_conv_w': 'grad_w', 'grad_a_log': 'grad_w', 'grad_dt_bias': 'grad_w', 'grad_dn_norm_w': 'grad_w', 'grad_sinks': 'grad_w', 'grad_w_branch_a': 'grad_w', 'grad_w_branch_b': 'grad_w', 'grad_w_out': 'grad_w', 'grad_ln1_g': 'grad_w', 'grad_ln1_b': 'grad_w', 'grad_w_gate_up': 'grad_w', 'grad_w_down': 'grad_w', 'grad_ln2_g': 'grad_w', 'grad_ln2_b': 'grad_w', 'delta_w_in': 'delta_w', 'delta_conv_w': 'delta_w', 'delta_a_log': 'delta_w', 'delta_dt_bias': 'delta_w', 'delta_dn_norm_w': 'delta_w', 'delta_sinks': 'delta_w', 'delta_w_branch_a': 'delta_w', 'delta_w_branch_b': 'delta_w', 'delta_w_out': 'delta_w', 'delta_ln1_g': 'delta_w', 'delta_ln1_b': 'delta_w', 'delta_w_gate_up': 'delta_w', 'delta_w_down': 'delta_w', 'delta_ln2_g': 'delta_w', 'delta_ln2_b': 'delta_w', 'new_m_w_in': 'new_m', 'new_m_conv_w': 'new_m', 'new_m_a_log': 'new_m', 'new_m_dt_bias': 'new_m', 'new_m_dn_norm_w': 'new_m', 'new_m_sinks': 'new_m', 'new_m_w_branch_a': 'new_m', 'new_m_w_branch_b': 'new_m', 'new_m_w_out': 'new_m', 'new_m_ln1_g': 'new_m', 'new_m_ln1_b': 'new_m', 'new_m_w_gate_up': 'new_m', 'new_m_w_down': 'new_m', 'new_m_ln2_g': 'new_m', 'new_m_ln2_b': 'new_m', 'new_v_w_in': 'new_v', 'new_v_conv_w': 'new_v', 'new_v_a_log': 'new_v', 'new_v_dt_bias': 'new_v', 'new_v_dn_norm_w': 'new_v', 'new_v_sinks': 'new_v', 'new_v_w_branch_a': 'new_v', 'new_v_w_branch_b': 'new_v', 'new_v_w_out': 'new_v', 'new_v_ln1_g': 'new_v', 'new_v_ln1_b': 'new_v', 'new_v_w_gate_up': 'new_v', 'new_v_w_down': 'new_v', 'new_v_ln2_g': 'new_v', 'new_v_ln2_b': 'new_v'}


def _forward(args):
    return _fwd_reference(*[args[k] for k in FWD_PARAMS])


def _output_shape():
    def fwd():
        inp = _fwd_setup_inputs(0)
        return _fwd_reference(*[inp[k] for k in FWD_PARAMS])
    out = _jax.eval_shape(fwd)
    return out.shape, out.dtype

N_MICROBATCH = 1
ADAM_LR = 0.001
ADAM_B1 = 0.9
ADAM_B2 = 0.999
ADAM_EPS = 1e-08
ADAM_WD = 0.01
ADAM_STEP = 10
PER_EXAMPLE_BATCH_AXIS = {'x': 0, 'loss_target': 0}
SHARED_INPUTS = []
_WEIGHT_DTYPES = {'w_in': _jnp.float32, 'conv_w': _jnp.float32, 'a_log': _jnp.float32, 'dt_bias': _jnp.float32, 'dn_norm_w': _jnp.float32, 'sinks': _jnp.float32, 'w_branch_a': _jnp.float32, 'w_branch_b': _jnp.float32, 'w_out': _jnp.float32, 'ln1_g': _jnp.float32, 'ln1_b': _jnp.float32, 'w_gate_up': _jnp.float32, 'w_down': _jnp.float32, 'ln2_g': _jnp.float32, 'ln2_b': _jnp.float32}
MOMENT_SCALE = {'w_in': 8.986173e-03, 'conv_w': 1.099976e-02, 'a_log': 3.351904e-02, 'dt_bias': 3.306855e-02, 'dn_norm_w': 4.592956e-02, 'sinks': 1.136121e-04, 'w_branch_a': 1.681422e-02, 'w_branch_b': 3.366571e-03, 'w_out': 4.072385e-02, 'ln1_g': 1.163568e+00, 'ln1_b': 5.563163e-01, 'w_gate_up': 1.646203e-02, 'w_down': 6.391515e-02, 'ln2_g': 1.611109e+01, 'ln2_b': 1.120423e+00}


def _to_microbatches(a, axis):
    t = _jnp.moveaxis(a, axis, 0)
    t = t.reshape((N_MICROBATCH, t.shape[0] // N_MICROBATCH) + t.shape[1:])
    return _jnp.moveaxis(t, 1, axis + 1)


def setup_inputs(seed: int = 0) -> dict:
    inp = _fwd_setup_inputs(seed)
    key = _jax.random.fold_in(_jax.random.key(seed), 7919)
    shape, _ = _output_shape()
    out = dict(inp)
    out["loss_target"] = _jax.random.normal(_jax.random.fold_in(key, 0), shape, _jnp.float32)
    for i, name in enumerate(TWIN_WEIGHTS):
        w = inp[name].astype(_jnp.float32)
        if MOMENT_SCALE is None:
            s = _jnp.sqrt(_jnp.mean(_jnp.square(w)) + 1e-30)
        else:
            s = MOMENT_SCALE[name]
        km, kv = _jax.random.split(_jax.random.fold_in(key, i + 1))
        out[name] = w
        out["m_" + name] = s * _jax.random.normal(km, w.shape, _jnp.float32)
        out["v_" + name] = (s * s) * _jax.random.uniform(kv, w.shape, _jnp.float32, 0.5, 1.5)
    if N_MICROBATCH > 1:
        for name, axis in PER_EXAMPLE_BATCH_AXIS.items():
            out[name] = _to_microbatches(out[name], axis)
    return {'x': out['x'], 'w_in': out['w_in'], 'conv_w': out['conv_w'], 'a_log': out['a_log'], 'dt_bias': out['dt_bias'], 'dn_norm_w': out['dn_norm_w'], 'sinks': out['sinks'], 'w_branch_a': out['w_branch_a'], 'w_branch_b': out['w_branch_b'], 'w_out': out['w_out'], 'ln1_g': out['ln1_g'], 'ln1_b': out['ln1_b'], 'w_gate_up': out['w_gate_up'], 'w_down': out['w_down'], 'ln2_g': out['ln2_g'], 'ln2_b': out['ln2_b'], 'loss_target': out['loss_target'], 'm_w_in': out['m_w_in'], 'm_conv_w': out['m_conv_w'], 'm_a_log': out['m_a_log'], 'm_dt_bias': out['m_dt_bias'], 'm_dn_norm_w': out['m_dn_norm_w'], 'm_sinks': out['m_sinks'], 'm_w_branch_a': out['m_w_branch_a'], 'm_w_branch_b': out['m_w_branch_b'], 'm_w_out': out['m_w_out'], 'm_ln1_g': out['m_ln1_g'], 'm_ln1_b': out['m_ln1_b'], 'm_w_gate_up': out['m_w_gate_up'], 'm_w_down': out['m_w_down'], 'm_ln2_g': out['m_ln2_g'], 'm_ln2_b': out['m_ln2_b'], 'v_w_in': out['v_w_in'], 'v_conv_w': out['v_conv_w'], 'v_a_log': out['v_a_log'], 'v_dt_bias': out['v_dt_bias'], 'v_dn_norm_w': out['v_dn_norm_w'], 'v_sinks': out['v_sinks'], 'v_w_branch_a': out['v_w_branch_a'], 'v_w_branch_b': out['v_w_branch_b'], 'v_w_out': out['v_w_out'], 'v_ln1_g': out['v_ln1_g'], 'v_ln1_b': out['v_ln1_b'], 'v_w_gate_up': out['v_w_gate_up'], 'v_w_down': out['v_w_down'], 'v_ln2_g': out['v_ln2_g'], 'v_ln2_b': out['v_ln2_b']}


def _loss(weights, diff, rest, loss_target):
    with _jax.named_scope("forward"):
        args = {**rest, TWIN_DIFF_INPUT: diff, **{k: w.astype(_WEIGHT_DTYPES[k]) for k, w in weights.items()}}
        y = _forward(args)
    with _jax.named_scope("loss_head"):
        err = _jnp.square(y.astype(_jnp.float32) - loss_target)
        return 0.5 * _jnp.sum(_jnp.mean(err, axis=-1)) if err.ndim else 0.5 * err


def _adamw(w, g, m, v):
    m = ADAM_B1 * m + (1.0 - ADAM_B1) * g
    v = ADAM_B2 * v + (1.0 - ADAM_B2) * _jnp.square(g)
    m_hat = m / (1.0 - ADAM_B1 ** ADAM_STEP)
    v_hat = v / (1.0 - ADAM_B2 ** ADAM_STEP)
    delta = -ADAM_LR * (m_hat / (_jnp.sqrt(v_hat) + ADAM_EPS) + ADAM_WD * w)
    return delta, m, v


def reference(x, w_in, conv_w, a_log, dt_bias, dn_norm_w, sinks, w_branch_a, w_branch_b, w_out, ln1_g, ln1_b, w_gate_up, w_down, ln2_g, ln2_b, loss_target, m_w_in, m_conv_w, m_a_log, m_dt_bias, m_dn_norm_w, m_sinks, m_w_branch_a, m_w_branch_b, m_w_out, m_ln1_g, m_ln1_b, m_w_gate_up, m_w_down, m_ln2_g, m_ln2_b, v_w_in, v_conv_w, v_a_log, v_dt_bias, v_dn_norm_w, v_sinks, v_w_branch_a, v_w_branch_b, v_w_out, v_ln1_g, v_ln1_b, v_w_gate_up, v_w_down, v_ln2_g, v_ln2_b):
    given = dict(x=x, w_in=w_in, conv_w=conv_w, a_log=a_log, dt_bias=dt_bias, dn_norm_w=dn_norm_w, sinks=sinks, w_branch_a=w_branch_a, w_branch_b=w_branch_b, w_out=w_out, ln1_g=ln1_g, ln1_b=ln1_b, w_gate_up=w_gate_up, w_down=w_down, ln2_g=ln2_g, ln2_b=ln2_b, loss_target=loss_target, m_w_in=m_w_in, m_conv_w=m_conv_w, m_a_log=m_a_log, m_dt_bias=m_dt_bias, m_dn_norm_w=m_dn_norm_w, m_sinks=m_sinks, m_w_branch_a=m_w_branch_a, m_w_branch_b=m_w_branch_b, m_w_out=m_w_out, m_ln1_g=m_ln1_g, m_ln1_b=m_ln1_b, m_w_gate_up=m_w_gate_up, m_w_down=m_w_down, m_ln2_g=m_ln2_g, m_ln2_b=m_ln2_b, v_w_in=v_w_in, v_conv_w=v_conv_w, v_a_log=v_a_log, v_dt_bias=v_dt_bias, v_dn_norm_w=v_dn_norm_w, v_sinks=v_sinks, v_w_branch_a=v_w_branch_a, v_w_branch_b=v_w_branch_b, v_w_out=v_w_out, v_ln1_g=v_ln1_g, v_ln1_b=v_ln1_b, v_w_gate_up=v_w_gate_up, v_w_down=v_w_down, v_ln2_g=v_ln2_g, v_ln2_b=v_ln2_b)
    weights = {n: given[n] for n in TWIN_WEIGHTS}
    shared = {n: given[n] for n in SHARED_INPUTS}
    per_example = {n: given[n] for n in ['x']}
    grad_fn = _jax.value_and_grad(_loss, argnums=(0, 1))

    def one_microbatch(ex, loss_target):
        ex = dict(ex)
        diff = ex.pop(TWIN_DIFF_INPUT)
        return grad_fn(weights, diff, {**shared, **ex}, loss_target)

    if N_MICROBATCH == 1:
        loss, (grad_w, grad_x) = one_microbatch(per_example, given["loss_target"])
    else:
        def body(carry, xs):
            loss_sum, grad_sum = carry
            l_k, (gw_k, gx_k) = one_microbatch(xs[0], xs[1])
            with _jax.named_scope("update"):
                return (loss_sum + l_k, _jax.tree.map(_jnp.add, grad_sum, gw_k)), gx_k

        init = (_jnp.zeros((), _jnp.float32), _jax.tree.map(_jnp.zeros_like, weights))
        (loss, grad_w), grad_x = _jax.lax.scan(body, init, (per_example, given["loss_target"]))
    with _jax.named_scope("update"):
        delta_w, new_m, new_v = {}, {}, {}
        for n in TWIN_WEIGHTS:
            delta_w[n], new_m[n], new_v[n] = _adamw(weights[n], grad_w[n], given["m_" + n], given["v_" + n])
    return (loss, grad_x, *[grad_w[n] for n in TWIN_WEIGHTS], *[delta_w[n] for n in TWIN_WEIGHTS],
            *[new_m[n] for n in TWIN_WEIGHTS], *[new_v[n] for n in TWIN_WEIGHTS])
```

```python
import functools

import jax
import jax.numpy as jnp
from jax import lax
from jax.experimental import pallas as pl
from jax.experimental.pallas import tpu as pltpu

F32 = jnp.float32
BF16 = jnp.bfloat16
_MXU = BF16

D = 1024
DEPTH = 4
DN_HEADS = 8
DN_DIM = 128
DN_CONV = 5
CHUNK = 64
SW_HEADS = 16
SW_KV = 4
SW_DIM = 64
SW_GRP = SW_HEADS // SW_KV
SW_BLOCK = 128
ROPE_THETA = 10000.0
FFN = 2816
ALPHA = (2.0 * DEPTH) ** 0.25
LN_EPS = 1e-5
RMS_EPS = 1e-6
IN_COLS = 7712
O_Z, O_QS, O_KS, O_VS, O_GA, O_GB, N_MAIN = 3072, 4096, 5120, 5376, 5632, 6656, 7680
R_BG = 4096
N_CHIPS = 4
IN_SHARD = IN_COLS // N_CHIPS
IN_PAD = 2048
ADAM_LR, ADAM_B1, ADAM_B2, ADAM_EPS, ADAM_WD, ADAM_STEP = 0.001, 0.9, 0.999, 1e-08, 0.01, 10
VMEM_LIMIT = 52 * 1024 * 1024
MESH = pl.DeviceIdType.MESH
ANY = pl.BlockSpec(memory_space=pl.ANY)


def _params(n_grid, **kw):
    return pltpu.CompilerParams(dimension_semantics=("arbitrary",) * n_grid, vmem_limit_bytes=VMEM_LIMIT, **kw)


def _full(a):
    nd = a.ndim
    return pl.BlockSpec(a.shape, lambda *_, nd=nd: (0,) * nd)


def _raw_dot(a, b, ca, cb):
    return lax.dot_general(a.astype(_MXU), b.astype(_MXU), (((ca,), (cb,)), ((), ())), preferred_element_type=F32)


@jax.custom_vjp
def _nn(a, b):
    return _raw_dot(a, b, 1, 0)


@jax.custom_vjp
def _nt(a, b):
    return _raw_dot(a, b, 1, 1)


@jax.custom_vjp
def _tn(a, b):
    return _raw_dot(a, b, 0, 0)


_nn.defvjp(lambda a, b: (_nn(a, b), (a, b)), lambda r, g: (_nt(g, r[1]), _tn(r[0], g)))
_nt.defvjp(lambda a, b: (_nt(a, b), (a, b)), lambda r, g: (_nn(g, r[1]), _tn(g, r[0])))
_tn.defvjp(lambda a, b: (_tn(a, b), (a, b)), lambda r, g: (_nt(r[1], g), _nn(r[0], g)))


def _hdot(a, b, ca=1, cb=0):
    return lax.dot_general(a, b, (((ca,), (cb,)), ((), ())), precision=lax.Precision.HIGHEST,
                           preferred_element_type=F32)


def _inv_impl(a):
    n = a.shape[0]
    eye = (lax.broadcasted_iota(jnp.int32, (n, n), 0) == lax.broadcasted_iota(jnp.int32, (n, n), 1)).astype(F32)
    p = -a
    t = eye + p
    steps = max(1, (n - 1).bit_length()) - 1
    for _ in range(steps):
        p = _hdot(p, p)
        t = t + _hdot(t, p)
    return t


@jax.custom_vjp
def _inv(a):
    return _inv_impl(a)


def _inv_fwd(a):
    t = _inv_impl(a)
    return t, t


def _inv_bwd(t, g):
    return (-_hdot(_hdot(t, g, 0, 0), t, 1, 1),)


_inv.defvjp(_inv_fwd, _inv_bwd)


def _tile(n, cap):
    if n <= cap:
        return n
    best = [t for t in range(128, cap + 1, 128) if n % t == 0]
    assert best, (n, cap)
    return best[-1]


def _mm(a, b, *, name, ta=False, tb=False, add=None, tm=1024, tn=512, tk=1024):
    if ta:
        k_, m_ = a.shape
    else:
        m_, k_ = a.shape
    n_ = b.shape[0] if tb else b.shape[1]
    tm, tn, tk = _tile(m_, tm), _tile(n_, tn), _tile(k_, tk)
    nk = k_ // tk
    has_add = add is not None

    def body(*refs):
        if has_add:
            a_ref, b_ref, add_ref, o_ref, acc = refs
        else:
            a_ref, b_ref, o_ref, acc = refs
        k = pl.program_id(2)

        @pl.when(k == 0)
        def _():
            acc[...] = jnp.zeros_like(acc)

        acc[...] += _raw_dot(a_ref[...], b_ref[...], 0 if ta else 1, 1 if tb else 0)

        @pl.when(k == nk - 1)
        def _():
            if has_add:
                o_ref[...] = acc[...] + add_ref[...]
            else:
                o_ref[...] = acc[...]

    a_spec = pl.BlockSpec((tk, tm), lambda i, j, k: (k, i)) if ta else pl.BlockSpec((tm, tk), lambda i, j, k: (i, k))
    b_spec = pl.BlockSpec((tn, tk), lambda i, j, k: (j, k)) if tb else pl.BlockSpec((tk, tn), lambda i, j, k: (k, j))
    o_spec = pl.BlockSpec((tm, tn), lambda i, j, k: (i, j))
    in_specs = [a_spec, b_spec] + ([o_spec] if has_add else [])
    args = (a, b) + ((add,) if has_add else ())
    return pl.pallas_call(
        body, name=name, grid=(m_ // tm, n_ // tn, nk), in_specs=in_specs, out_specs=o_spec,
        out_shape=jax.ShapeDtypeStruct((m_, n_), F32), scratch_shapes=[pltpu.VMEM((tm, tn), F32)],
        compiler_params=_params(3))(*args)


def _row_spec(tb, w, c0, percol):
    return pl.BlockSpec((tb, w), lambda i, j, c0=c0, pc=percol: (i, c0 + (j if pc else 0)))


def _stage_fwd(f, name, rows, params, out_widths, tb, ncol=1):
    t_ = rows[0][0].shape[0]
    nr, npar = len(rows), len(params)

    def body(*refs):
        res = f(*[r[...] for r in refs[:nr + npar]])
        for o_ref, val in zip(refs[nr + npar:], res):
            o_ref[...] = val

    return pl.pallas_call(
        body, name=name, grid=(t_ // tb, ncol),
        in_specs=[_row_spec(tb, w, c0, pc) for (_, w, c0, pc) in rows] + [_full(p) for p in params],
        out_specs=[pl.BlockSpec((tb, w), lambda i, j: (i, j)) for w in out_widths],
        out_shape=[jax.ShapeDtypeStruct((t_, w * ncol), F32) for w in out_widths],
        compiler_params=_params(2))(*[r[0] for r in rows], *params)


def _stage_bwd(f, name, rows, params, douts, tb, ncol=1, cat=None):
    t_ = rows[0][0].shape[0]
    nr, npar, nd = len(rows), len(params), len(douts)
    cat = cat if cat is not None else [[r] for r in range(nr)]
    assert ncol == 1 or all(len(g) == 1 and rows[g[0]][3] for g in cat)

    def body(*refs):
        ins = [r[...] for r in refs[:nr + npar]]
        dvals = tuple(r[...] for r in refs[nr + npar:nr + npar + nd])
        out_refs = refs[nr + npar + nd:]
        _, vjp = jax.vjp(f, *ins)
        grads = vjp(dvals)
        for o_ref, grp in zip(out_refs[:len(cat)], cat):
            o_ref[...] = grads[grp[0]] if len(grp) == 1 else jnp.concatenate([grads[r] for r in grp], axis=-1)
        first = jnp.logical_and(pl.program_id(0) == 0, pl.program_id(1) == 0)
        for p_ref, gp in zip(out_refs[len(cat):], grads[nr:]):
            @pl.when(first)
            def _(p_ref=p_ref):
                p_ref[...] = jnp.zeros_like(p_ref)
            p_ref[...] += gp

    gw = [sum(rows[r][1] for r in grp) for grp in cat]
    return pl.pallas_call(
        body, name=name, grid=(t_ // tb, ncol),
        in_specs=[_row_spec(tb, w, c0, pc) for (_, w, c0, pc) in rows] + [_full(p) for p in params]
        + [pl.BlockSpec((tb, d.shape[1] // ncol), lambda i, j: (i, j)) for d in douts],
        out_specs=[pl.BlockSpec((tb, w), lambda i, j: (i, j)) for w in gw] + [_full(p) for p in params],
        out_shape=[jax.ShapeDtypeStruct((t_, w * ncol), F32) for w in gw]
        + [jax.ShapeDtypeStruct(p.shape, F32) for p in params],
        compiler_params=_params(2))(*[r[0] for r in rows], *params, *douts)


def _ln_f(x, y, g, b):
    u = ALPHA * x + y
    c = u - jnp.mean(u, axis=-1, keepdims=True)
    var = jnp.mean(c * c, axis=-1, keepdims=True)
    return (c * lax.rsqrt(var + LN_EPS) * g + b,)


def _swiglu_f(gate, up):
    return (jax.nn.silu(gate) * up,)


def _merge_f(ya, yb, ga, gb):
    return (jax.nn.sigmoid(ga) * ya + jax.nn.sigmoid(gb) * yb,)


def _gnorm_f(of, ob, z, w):
    o = of + ob
    return (o * lax.rsqrt(jnp.mean(o * o, axis=-1, keepdims=True) + RMS_EPS) * w * jax.nn.silu(z),)


def _bg_f(x, arow, dtrow):
    lane = lax.broadcasted_iota(jnp.int32, x.shape, 1)
    beta = jax.nn.sigmoid(x)
    g = -jnp.exp(arow) * jax.nn.softplus(x + dtrow)
    return (jnp.where(lane < 16, beta, jnp.where(lane < 32, g, 0.0)),)


PREP_ROWS = 512
PAD = 8


def _prep_f(part, w, *wins):
    xc = wins[0] * w[0:1, :]
    for k in range(1, DN_CONV):
        xc = xc + wins[k] * w[k:k + 1, :]
    a = jax.nn.silu(xc)
    nrm = a * lax.rsqrt(jnp.sum(a * a, axis=-1, keepdims=True) + RMS_EPS)
    return jnp.where(part == 0, nrm * (DN_DIM ** -0.5), jnp.where(part == 1, nrm, a))


def _windows(pad_ref, r0, rows):
    return [pad_ref[PAD + r0 - 2 + k:PAD + r0 - 2 + k + rows, :] for k in range(DN_CONV)]


def _prep_fwd(pm, conv):
    t_ = pm.shape[0]
    rows = min(PREP_ROWS, t_)

    def body(x_ref, w_ref, o_ref, pad_ref):
        part = pl.program_id(0) // DN_HEADS
        pad_ref[0:PAD, :] = jnp.zeros((PAD, DN_DIM), F32)
        pad_ref[PAD + t_:2 * PAD + t_, :] = jnp.zeros((PAD, DN_DIM), F32)
        pad_ref[PAD:PAD + t_, :] = x_ref[...]
        w = w_ref[...]
        for r in range(t_ // rows):
            o_ref[r * rows:(r + 1) * rows, :] = _prep_f(part, w, *_windows(pad_ref, r * rows, rows))

    ncb = 3 * DN_HEADS
    return pl.pallas_call(
        body, name="prep_fwd", grid=(ncb,),
        in_specs=[pl.BlockSpec((t_, DN_DIM), lambda j: (0, j)), pl.BlockSpec((DN_CONV, DN_DIM), lambda j: (0, j))],
        out_specs=pl.BlockSpec((t_, DN_DIM), lambda j: (0, j)),
        out_shape=jax.ShapeDtypeStruct((t_, ncb * DN_DIM), F32),
        scratch_shapes=[pltpu.VMEM((t_ + 2 * PAD, DN_DIM), F32)],
        compiler_params=_params(1))(pm, conv)


def _prep_bwd(pm, conv, dout):
    t_ = pm.shape[0]
    rows = min(PREP_ROWS, t_)

    def body(x_ref, w_ref, d_ref, dx_ref, dw_ref, pad_ref, dpad_ref):
        part = pl.program_id(0) // DN_HEADS
        pad_ref[0:PAD, :] = jnp.zeros((PAD, DN_DIM), F32)
        pad_ref[PAD + t_:2 * PAD + t_, :] = jnp.zeros((PAD, DN_DIM), F32)
        pad_ref[PAD:PAD + t_, :] = x_ref[...]
        dpad_ref[...] = jnp.zeros_like(dpad_ref)
        w = w_ref[...]
        dw = jnp.zeros((DN_CONV, DN_DIM), F32)
        for r in range(t_ // rows):
            r0 = r * rows
            _, vjp = jax.vjp(functools.partial(_prep_f, part), w, *_windows(pad_ref, r0, rows))
            grads = vjp(d_ref[r0:r0 + rows, :])
            dw = dw + grads[0]
            for k in range(DN_CONV):
                lo = PAD + r0 - 2 + k
                dpad_ref[lo:lo + rows, :] += grads[1 + k]
        dx_ref[...] = dpad_ref[PAD:PAD + t_, :]
        dw_ref[...] = dw

    ncb = 3 * DN_HEADS
    col = pl.BlockSpec((t_, DN_DIM), lambda j: (0, j))
    wsp = pl.BlockSpec((DN_CONV, DN_DIM), lambda j: (0, j))
    return pl.pallas_call(
        body, name="prep_bwd", grid=(ncb,), in_specs=[col, wsp, col], out_specs=[col, wsp],
        out_shape=[jax.ShapeDtypeStruct((t_, ncb * DN_DIM), F32), jax.ShapeDtypeStruct((DN_CONV, ncb * DN_DIM), F32)],
        scratch_shapes=[pltpu.VMEM((t_ + 2 * PAD, DN_DIM), F32), pltpu.VMEM((t_ + 2 * PAD, DN_DIM), F32)],
        compiler_params=_params(1))(pm, conv, dout)


def _dn_chunk(q, k, v, grow, brow, sgn):
    c = q.shape[0]
    i = lax.broadcasted_iota(jnp.int32, (c, c), 0)
    j = lax.broadcasted_iota(jnp.int32, (c, c), 1)
    dlt = (i - j).astype(F32) * sgn
    incl, strict, eye = dlt >= 0, dlt > 0, i == j
    gc = jnp.sum(jnp.where(incl, grow, 0.0), axis=1, keepdims=True)
    gr = jnp.sum(jnp.where(eye, gc, 0.0), axis=0, keepdims=True)
    bc = jnp.sum(jnp.where(eye, brow, 0.0), axis=1, keepdims=True)
    glast = jnp.sum(grow, axis=1, keepdims=True)
    decay = jnp.exp(jnp.where(incl, gc - gr, -1e30))
    amat = jnp.where(strict, bc * _nt(k, k) * decay, 0.0)
    tinv = _inv(amat)
    egc = jnp.exp(gc)
    u = _nn(tinv, v * bc)
    w = _nn(tinv, k * (bc * egc))
    qk = _nt(q, k) * decay
    return u, w, qk, q * egc, k * jnp.exp(glast - gc)


def _dn_step(u, w, qk, qd, kd, grow, s):
    gl = jnp.exp(jnp.sum(grow, axis=1, keepdims=True))
    vnew = u - _nn(w, s)
    o = _nn(qd, s) + _nn(qk, vnew)
    return o, s * gl + _tn(kd, vnew)


def _hs(h):
    return slice(h * DN_DIM, (h + 1) * DN_DIM)


_DIR_SGN = (1.0, -1.0)
_A_OUT = 5


def _dn_a_fwd(qkv, grows, brows):
    t_ = qkv.shape[0]
    nch = t_ // CHUNK

    def body(q_ref, k_ref, v_ref, g_ref, b_ref, *outs):
        for d in range(2):
            u_ref, w_ref, qk_ref, qd_ref, kd_ref = outs[d * _A_OUT:(d + 1) * _A_OUT]
            for h in range(DN_HEADS):
                res = _dn_chunk(q_ref[:, _hs(h)], k_ref[:, _hs(h)], v_ref[:, _hs(h)],
                                g_ref[d, 0, h:h + 1, :], b_ref[d, 0, h:h + 1, :], _DIR_SGN[d])
                u_ref[:, _hs(h)], w_ref[:, _hs(h)], qk_ref[0, h], qd_ref[:, _hs(h)], kd_ref[:, _hs(h)] = res

    rspec = pl.BlockSpec((2, 1, DN_HEADS, CHUNK), lambda c: (0, c, 0, 0))
    big = pl.BlockSpec((CHUNK, D), lambda c: (c, 0))
    qks = pl.BlockSpec((1, DN_HEADS, CHUNK, CHUNK), lambda c: (c, 0, 0, 0))
    bigs = jax.ShapeDtypeStruct((t_, D), F32)
    qksh = jax.ShapeDtypeStruct((nch, DN_HEADS, CHUNK, CHUNK), F32)
    return pl.pallas_call(
        body, name="dn_a_fwd", grid=(nch,),
        in_specs=[pl.BlockSpec((CHUNK, D), lambda c, p=p: (c, p)) for p in range(3)] + [rspec, rspec],
        out_specs=[big, big, qks, big, big] * 2, out_shape=[bigs, bigs, qksh, bigs, bigs] * 2,
        compiler_params=_params(1))(qkv, qkv, qkv, grows, brows)


def _dn_a_bwd(qkv, grows, brows, dres, dg_b):
    t_ = qkv.shape[0]
    nch = t_ // CHUNK

    def body(q_ref, k_ref, v_ref, g_ref, b_ref, *rest):
        dins, dgb_ref, (dqkv_ref, dg_ref, db_ref) = rest[:2 * _A_OUT], rest[2 * _A_OUT], rest[2 * _A_OUT + 1:]
        for h in range(DN_HEADS):
            dq = dk = dv = None
            for d in range(2):
                du_ref, dw_ref, dqk_ref, dqd_ref, dkd_ref = dins[d * _A_OUT:(d + 1) * _A_OUT]
                _, vjp = jax.vjp(functools.partial(_dn_chunk, sgn=_DIR_SGN[d]),
                                 q_ref[:, _hs(h)], k_ref[:, _hs(h)], v_ref[:, _hs(h)],
                                 g_ref[d, 0, h:h + 1, :], b_ref[d, 0, h:h + 1, :])
                gq, gk, gv, gg, gb = vjp((du_ref[:, _hs(h)], dw_ref[:, _hs(h)], dqk_ref[0, h],
                                          dqd_ref[:, _hs(h)], dkd_ref[:, _hs(h)]))
                dq, dk, dv = (gq, gk, gv) if d == 0 else (dq + gq, dk + gk, dv + gv)
                dg_ref[d, 0, h:h + 1, :] = gg + dgb_ref[d, 0, h:h + 1, :]
                db_ref[d, 0, h:h + 1, :] = gb
            dqkv_ref[:, _hs(h)] = dq
            dqkv_ref[:, _hs(DN_HEADS + h)] = dk
            dqkv_ref[:, _hs(2 * DN_HEADS + h)] = dv

    rspec = pl.BlockSpec((2, 1, DN_HEADS, CHUNK), lambda c: (0, c, 0, 0))
    big = pl.BlockSpec((CHUNK, D), lambda c: (c, 0))
    qks = pl.BlockSpec((1, DN_HEADS, CHUNK, CHUNK), lambda c: (c, 0, 0, 0))
    rsh = jax.ShapeDtypeStruct(grows.shape, F32)
    return pl.pallas_call(
        body, name="dn_a_bwd", grid=(nch,),
        in_specs=[pl.BlockSpec((CHUNK, D), lambda c, p=p: (c, p)) for p in range(3)] + [rspec, rspec]
        + [big, big, qks, big, big] * 2 + [rspec],
        out_specs=[pl.BlockSpec((CHUNK, 3 * D), lambda c: (c, 0)), rspec, rspec],
        out_shape=[jax.ShapeDtypeStruct((t_, 3 * D), F32), rsh, rsh],
        compiler_params=_params(1))(qkv, qkv, qkv, grows, brows, *dres, dg_b)


def _dir_specs(nch):
    def cidx(d):
        return (lambda n: n) if d == 0 else (lambda n: nch - 1 - n)
    out = []
    for d in range(2):
        ci = cidx(d)
        big = pl.BlockSpec((CHUNK, D), lambda n, ci=ci: (ci(n), 0))
        qks = pl.BlockSpec((1, DN_HEADS, CHUNK, CHUNK), lambda n, ci=ci: (ci(n), 0, 0, 0))
        row = pl.BlockSpec((1, 1, DN_HEADS, CHUNK), lambda n, ci=ci, d=d: (d, ci(n), 0, 0))
        st = pl.BlockSpec((1, DN_HEADS, DN_DIM, DN_DIM), lambda n, ci=ci: (ci(n), 0, 0, 0))
        out.append(dict(big=big, qk=qks, row=row, st=st))
    return out


def _dn_b_fwd(ares, grows):
    t_ = ares[0].shape[0]
    nch = t_ // CHUNK
    sp = _dir_specs(nch)

    def body(*refs):
        ins, outs, s_ref = refs[:12], refs[12:16], refs[16]

        @pl.when(pl.program_id(0) == 0)
        def _():
            s_ref[...] = jnp.zeros_like(s_ref)

        for d in range(2):
            u_ref, w_ref, qk_ref, qd_ref, kd_ref, g_ref = ins[d * 6:(d + 1) * 6]
            o_ref, st_ref = outs[d], outs[2 + d]
            for h in range(DN_HEADS):
                s = s_ref[d * DN_HEADS + h]
                st_ref[0, h] = s
                o, s2 = _dn_step(u_ref[:, _hs(h)], w_ref[:, _hs(h)], qk_ref[0, h], qd_ref[:, _hs(h)],
                                 kd_ref[:, _hs(h)], g_ref[0, 0, h:h + 1, :], s)
                o_ref[:, _hs(h)] = o
                s_ref[d * DN_HEADS + h] = s2

    in_specs, args = [], []
    for d in range(2):
        in_specs += [sp[d]["big"], sp[d]["big"], sp[d]["qk"], sp[d]["big"], sp[d]["big"], sp[d]["row"]]
        args += list(ares[d * _A_OUT:(d + 1) * _A_OUT]) + [grows]
    stsh = jax.ShapeDtypeStruct((nch, DN_HEADS, DN_DIM, DN_DIM), F32)
    osh = jax.ShapeDtypeStruct((t_, D), F32)
    return pl.pallas_call(
        body, name="dn_b_fwd", grid=(nch,), in_specs=in_specs,
        out_specs=[sp[0]["big"], sp[1]["big"], sp[0]["st"], sp[1]["st"]], out_shape=[osh, osh, stsh, stsh],
        scratch_shapes=[pltpu.VMEM((2 * DN_HEADS, DN_DIM, DN_DIM), F32)],
        compiler_params=_params(1))(*args)


def _dn_b_bwd(ares, grows, st_f, st_b, do):
    t_ = ares[0].shape[0]
    nch = t_ // CHUNK
    sp = _dir_specs(nch)
    rsp = [sp[1], sp[0]]

    def body(*refs):
        ins, outs, ds_ref = refs[:16], refs[16:28], refs[28]

        @pl.when(pl.program_id(0) == 0)
        def _():
            ds_ref[...] = jnp.zeros_like(ds_ref)

        for d in range(2):
            u_ref, w_ref, qk_ref, qd_ref, kd_ref, g_ref, st_ref, do_ref = ins[d * 8:(d + 1) * 8]
            du_ref, dw_ref, dqk_ref, dqd_ref, dkd_ref, dg_ref = outs[d * 6:(d + 1) * 6]
            for h in range(DN_HEADS):
                _, vjp = jax.vjp(_dn_step, u_ref[:, _hs(h)], w_ref[:, _hs(h)], qk_ref[0, h], qd_ref[:, _hs(h)],
                                 kd_ref[:, _hs(h)], g_ref[0, 0, h:h + 1, :], st_ref[0, h])
                gu, gw, gqk, gqd, gkd, gg, gs = vjp((do_ref[:, _hs(h)], ds_ref[d * DN_HEADS + h]))
                du_ref[:, _hs(h)], dw_ref[:, _hs(h)], dqk_ref[0, h] = gu, gw, gqk
                dqd_ref[:, _hs(h)], dkd_ref[:, _hs(h)] = gqd, gkd
                dg_ref[0, 0, h:h + 1, :] = gg
                ds_ref[d * DN_HEADS + h] = gs

    in_specs, args, out_specs, out_shape = [], [], [], []
    big_sh = jax.ShapeDtypeStruct((t_, D), F32)
    qk_sh = jax.ShapeDtypeStruct((nch, DN_HEADS, CHUNK, CHUNK), F32)
    row_sh = jax.ShapeDtypeStruct((1, nch, DN_HEADS, CHUNK), F32)
    for d in range(2):
        s = rsp[d]
        row0 = pl.BlockSpec((1, 1, DN_HEADS, CHUNK), lambda m, d=d: (0, (nch - 1 - m) if d == 0 else m, 0, 0))
        rowd = pl.BlockSpec((1, 1, DN_HEADS, CHUNK), lambda m, d=d: (d, (nch - 1 - m) if d == 0 else m, 0, 0))
        in_specs += [s["big"], s["big"], s["qk"], s["big"], s["big"], rowd, s["st"], s["big"]]
        args += list(ares[d * _A_OUT:(d + 1) * _A_OUT]) + [grows, (st_f, st_b)[d], do]
        out_specs += [s["big"], s["big"], s["qk"], s["big"], s["big"], row0]
        out_shape += [big_sh, big_sh, qk_sh, big_sh, big_sh, row_sh]
    res = pl.pallas_call(
        body, name="dn_b_bwd", grid=(nch,), in_specs=in_specs, out_specs=out_specs, out_shape=out_shape,
        scratch_shapes=[pltpu.VMEM((2 * DN_HEADS, DN_DIM, DN_DIM), F32)],
        compiler_params=_params(1))(*args)
    dares = list(res[0:5]) + list(res[6:11])
    return dares, jnp.concatenate([res[5], res[11]], axis=0)


def _rope(x, c, s):
    half = SW_DIM // 2
    return x * c + jnp.concatenate([-x[:, half:], x[:, :half]], axis=-1) * s


def _attn_f(blk, t_, cq, sq, ck, sk, q, kp, ko, kn, vp, vo, vn, sinks):
    kall = jnp.concatenate([kp, ko, kn], axis=0)
    vall = jnp.concatenate([vp, vo, vn], axis=0)
    nq, nk = SW_GRP * SW_BLOCK, 3 * SW_BLOCK
    qpos = lax.broadcasted_iota(jnp.int32, (nq, nk), 0) % SW_BLOCK
    krel = lax.broadcasted_iota(jnp.int32, (nq, nk), 1) - SW_BLOCK
    kglob = krel + blk * SW_BLOCK
    valid = (jnp.abs(qpos - krel) <= SW_BLOCK) & (kglob >= 0) & (kglob < t_)
    outs = []
    for kvh in range(SW_KV):
        kh = _rope(kall[:, kvh * SW_DIM:(kvh + 1) * SW_DIM], ck, sk)
        vh = vall[:, kvh * SW_DIM:(kvh + 1) * SW_DIM]
        heads = [kvh * SW_GRP + g for g in range(SW_GRP)]
        qg = jnp.concatenate([_rope(q[:, h * SW_DIM:(h + 1) * SW_DIM], cq, sq) for h in heads], axis=0)
        s = jnp.where(valid, _nt(qg, kh) * (SW_DIM ** -0.5), -1e30)
        snk = jnp.concatenate([jnp.broadcast_to(sinks[:, h:h + 1], (SW_BLOCK, 1)) for h in heads], axis=0)
        m = lax.stop_gradient(jnp.maximum(jnp.max(s, axis=-1, keepdims=True), snk))
        e = jnp.exp(s - m)
        p = e / (jnp.sum(e, axis=-1, keepdims=True) + jnp.exp(snk - m))
        og = _nn(p, vh)
        outs += [og[g * SW_BLOCK:(g + 1) * SW_BLOCK] for g in range(SW_GRP)]
    return jnp.concatenate(outs, axis=-1)


def _attn_specs(nb):
    prv = lambda i: jnp.maximum(i - 1, 0)
    nxt = lambda i: jnp.minimum(i + 1, nb - 1)
    rows = [lambda i: i, prv, lambda i: i, nxt]
    tab = [pl.BlockSpec((SW_BLOCK, SW_DIM), lambda i, r=r: (r(i), 0)) for r in rows]
    qs = pl.BlockSpec((SW_BLOCK, SW_HEADS * SW_DIM), lambda i: (i, O_QS // (SW_HEADS * SW_DIM)))
    kw = SW_KV * SW_DIM
    ks = [pl.BlockSpec((SW_BLOCK, kw), lambda i, r=r: (r(i), O_KS // kw)) for r in rows[1:]]
    vs = [pl.BlockSpec((SW_BLOCK, kw), lambda i, r=r: (r(i), O_VS // kw)) for r in rows[1:]]
    return tab, qs, ks, vs


def _attn_tables(refs):
    cq, cp, co, cn, sq, sp_, so, sn = [r[...] for r in refs]
    return cq, sq, jnp.concatenate([cp, co, cn], axis=0), jnp.concatenate([sp_, so, sn], axis=0)


def _attn_fwd(pm, cos, sin, sinks):
    t_ = pm.shape[0]
    nb = t_ // SW_BLOCK
    tab, qs, ks, vs = _attn_specs(nb)

    def body(*refs):
        tabs = _attn_tables(refs[:8])
        vals = [r[...] for r in refs[8:16]]
        refs[16][...] = _attn_f(pl.program_id(0), t_, *tabs, *vals)

    return pl.pallas_call(
        body, name="attn_fwd", grid=(nb,), in_specs=tab + tab + [qs] + ks + vs + [_full(sinks)],
        out_specs=pl.BlockSpec((SW_BLOCK, D), lambda i: (i, 0)), out_shape=jax.ShapeDtypeStruct((t_, D), F32),
        compiler_params=_params(1))(*([cos] * 4), *([sin] * 4), pm, pm, pm, pm, pm, pm, pm, sinks)


def _attn_bwd(pm, cos, sin, sinks, do):
    t_ = pm.shape[0]
    nb = t_ // SW_BLOCK
    tab, qs, ks, vs = _attn_specs(nb)
    kw = SW_KV * SW_DIM

    def body(*refs):
        tabs = _attn_tables(refs[:8])
        vals = [r[...] for r in refs[8:16]]
        do_ref, outs = refs[16], refs[17:]
        _, vjp = jax.vjp(functools.partial(_attn_f, pl.program_id(0), t_, *tabs), *vals)
        grads = vjp(do_ref[...])
        for o_ref, g in zip(outs[:7], grads[:7]):
            o_ref[...] = g

        @pl.when(pl.program_id(0) == 0)
        def _():
            outs[7][...] = jnp.zeros_like(outs[7])
        outs[7][...] += grads[7]

    own = lambda w: pl.BlockSpec((SW_BLOCK, w), lambda i: (i, 0))
    return pl.pallas_call(
        body, name="attn_bwd", grid=(nb,),
        in_specs=tab + tab + [qs] + ks + vs + [_full(sinks), own(D)],
        out_specs=[own(D)] + [own(kw)] * 6 + [_full(sinks)],
        out_shape=[jax.ShapeDtypeStruct((t_, D), F32)] + [jax.ShapeDtypeStruct((t_, kw), F32)] * 6
        + [jax.ShapeDtypeStruct(sinks.shape, F32)],
        compiler_params=_params(1))(*([cos] * 4), *([sin] * 4), pm, pm, pm, pm, pm, pm, pm, sinks, do)


def _band_sum(parts):
    dp, do, dn = parts
    t_, kw = do.shape
    nb = t_ // SW_BLOCK

    def body(p_ref, o_ref, n_ref, out_ref):
        j = pl.program_id(0)
        acc = o_ref[...]
        acc = acc + jnp.where(j + 1 < nb, p_ref[...], 0.0)
        out_ref[...] = acc + jnp.where(j > 0, n_ref[...], 0.0)

    return pl.pallas_call(
        body, name="band_sum", grid=(nb,),
        in_specs=[pl.BlockSpec((SW_BLOCK, kw), lambda j: (jnp.minimum(j + 1, nb - 1), 0)),
                  pl.BlockSpec((SW_BLOCK, kw), lambda j: (j, 0)),
                  pl.BlockSpec((SW_BLOCK, kw), lambda j: (jnp.maximum(j - 1, 0), 0))],
        out_specs=pl.BlockSpec((SW_BLOCK, kw), lambda j: (j, 0)), out_shape=jax.ShapeDtypeStruct((t_, kw), F32),
        compiler_params=_params(1))(dp, do, dn)


def _loss_head(y, target, tb=256):
    t_ = y.shape[0]

    def body(y_ref, t_ref, dy_ref, acc_ref):
        err = y_ref[...] - t_ref[...]
        dy_ref[...] = err * (1.0 / D)
        sq = (err * err).reshape(tb // 8, 8, D).sum(axis=0)
        part = sq[:, 0:128]
        for c in range(1, D // 128):
            part = part + sq[:, c * 128:(c + 1) * 128]

        @pl.when(pl.program_id(0) == 0)
        def _():
            acc_ref[...] = jnp.zeros_like(acc_ref)
        acc_ref[...] += part

    row = pl.BlockSpec((tb, D), lambda i: (i, 0))
    return pl.pallas_call(
        body, name="loss_head", grid=(t_ // tb,), in_specs=[row, row],
        out_specs=[row, pl.BlockSpec((8, 128), lambda i: (0, 0))],
        out_shape=[jax.ShapeDtypeStruct((t_, D), F32), jax.ShapeDtypeStruct((8, 128), F32)],
        compiler_params=_params(1))(y, target)


def _adamw(name, w, g, m, v):
    shape = w.shape
    cols = shape[-1]
    rows = w.size // cols
    w2, g2, m2, v2 = [a.reshape(rows, cols) for a in (w, g, m, v)]
    tb = rows
    while tb * cols * 4 > (1 << 20) and tb % 16 == 0:
        tb //= 2
    bc1 = 1.0 - ADAM_B1 ** ADAM_STEP
    bc2 = 1.0 - ADAM_B2 ** ADAM_STEP

    def body(w_ref, g_ref, m_ref, v_ref, d_ref, nm_ref, nv_ref):
        gv = g_ref[...]
        nm = ADAM_B1 * m_ref[...] + (1.0 - ADAM_B1) * gv
        nv = ADAM_B2 * v_ref[...] + (1.0 - ADAM_B2) * (gv * gv)
        d_ref[...] = -ADAM_LR * ((nm / bc1) / (jnp.sqrt(nv / bc2) + ADAM_EPS) + ADAM_WD * w_ref[...])
        nm_ref[...] = nm
        nv_ref[...] = nv

    spec = pl.BlockSpec((tb, cols), lambda i: (i, 0))
    sh = jax.ShapeDtypeStruct((rows, cols), F32)
    outs = pl.pallas_call(body, name=name, grid=(rows // tb,), in_specs=[spec] * 4, out_specs=[spec] * 3,
                          out_shape=[sh] * 3, compiler_params=_params(1))(w2, g2, m2, v2)
    return [o.reshape(shape) for o in outs]


def _to_rows(bg, col0):
    t_ = bg.shape[0]
    a = bg[:, col0:col0 + 2 * DN_HEADS].reshape(t_ // CHUNK, CHUNK, 2, DN_HEADS)
    return jnp.transpose(a, (2, 0, 3, 1))


def _from_rows(db, dg):
    nch = db.shape[1]
    back = lambda a: jnp.transpose(a, (1, 3, 0, 2)).reshape(nch * CHUNK, 2 * DN_HEADS)
    return jnp.pad(jnp.concatenate([back(db), back(dg)], axis=1), ((0, 0), (0, 128 - 4 * DN_HEADS)))


def _layer_fwd(x, w, cos, sin):
    tb = min(256, x.shape[0])
    pm = _mm(x, w["in_main"], name="mm_in")
    pbg = _mm(x, w["in_bg"], name="mm_in_bg")
    qkv = _prep_fwd(pm, w["conv"])
    bg, = _stage_fwd(_bg_f, "bg_fwd", [(pbg, 128, 0, False)], [w["arow"], w["dtrow"]], [128], tb)
    brows, grows = _to_rows(bg, 0), _to_rows(bg, 2 * DN_HEADS)
    ares = _dn_a_fwd(qkv, grows, brows)
    o_f, o_b, st_f, st_b = _dn_b_fwd(ares, grows)
    odn, = _stage_fwd(_gnorm_f, "gnorm_fwd", [(o_f, 128, 0, True), (o_b, 128, 0, True), (pm, 128, O_Z // 128, True)],
                      [w["gnw"]], [128], tb, ncol=DN_HEADS)
    osw = _attn_fwd(pm, cos, sin, w["sinks"])
    ya = _mm(odn, w["a"], name="mm_a")
    yb = _mm(osw, w["b"], name="mm_b")
    gates = [(pm, 512, O_GA // 512, True), (pm, 512, O_GB // 512, True)]
    merged, = _stage_fwd(_merge_f, "merge_fwd", [(ya, 512, 0, True), (yb, 512, 0, True)] + gates, [], [512], tb, ncol=2)
    mix = _mm(merged, w["o"], name="mm_o")
    x1, = _stage_fwd(_ln_f, "ln1_fwd", [(x, D, 0, False), (mix, D, 0, False)], [w["ln1g"], w["ln1b"]], [D], tb)
    gu = _mm(x1, w["gu"], name="mm_gu")
    hid, = _stage_fwd(_swiglu_f, "swiglu_fwd", [(gu, 256, 0, True), (gu, 256, FFN // 256, True)], [], [256], tb,
                      ncol=FFN // 256)
    ffn = _mm(hid, w["d"], name="mm_d")
    x2, = _stage_fwd(_ln_f, "ln2_fwd", [(x1, D, 0, False), (ffn, D, 0, False)], [w["ln2g"], w["ln2b"]], [D], tb)
    res = dict(x=x, pm=pm, pbg=pbg, qkv=qkv, grows=grows, brows=brows, ares=ares, o_f=o_f, o_b=o_b, st_f=st_f,
               st_b=st_b, odn=odn, osw=osw, ya=ya, yb=yb, merged=merged, mix=mix, x1=x1, gu=gu, hid=hid, ffn=ffn)
    return x2, res


def _layer_bwd(dx2, r, w, cos, sin):
    tb = min(256, dx2.shape[0])
    sb = min(128, dx2.shape[0])
    g = {}
    dx1a, dffn, g["ln2g"], g["ln2b"] = _stage_bwd(
        _ln_f, "ln2_bwd", [(r["x1"], D, 0, False), (r["ffn"], D, 0, False)], [w["ln2g"], w["ln2b"]], [dx2], tb)
    dhid = _mm(dffn, w["d"], tb=True, name="mm_d_dx")
    g["d"] = _mm(r["hid"], dffn, ta=True, name="mm_d_dw")
    dgu, = _stage_bwd(_swiglu_f, "swiglu_bwd", [(r["gu"], FFN, 0, False), (r["gu"], FFN, 1, False)], [], [dhid], sb,
                      cat=[[0, 1]])
    dx1 = _mm(dgu, w["gu"], tb=True, add=dx1a, name="mm_gu_dx")
    g["gu"] = _mm(r["x1"], dgu, ta=True, name="mm_gu_dw")
    dxa, dmix, g["ln1g"], g["ln1b"] = _stage_bwd(
        _ln_f, "ln1_bwd", [(r["x"], D, 0, False), (r["mix"], D, 0, False)], [w["ln1g"], w["ln1b"]], [dx1], tb)
    dmerged = _mm(dmix, w["o"], tb=True, name="mm_o_dx")
    g["o"] = _mm(r["merged"], dmix, ta=True, name="mm_o_dw")
    pm = r["pm"]
    dya, dyb, dgates = _merge_bwd(r, pm, dmerged, tb)
    dodn = _mm(dya, w["a"], tb=True, name="mm_a_dx")
    g["a"] = _mm(r["odn"], dya, ta=True, name="mm_a_dw")
    dosw = _mm(dyb, w["b"], tb=True, name="mm_b_dx")
    g["b"] = _mm(r["osw"], dyb, ta=True, name="mm_b_dw")
    ab = _attn_bwd(pm, cos, sin, w["sinks"], dosw)
    dq_sw, g["sinks"] = ab[0], ab[7]
    dk_sw = _band_sum(ab[1:4])
    dv_sw = _band_sum(ab[4:7])
    dof, _, dz, g["gnw"] = _stage_bwd(
        _gnorm_f, "gnorm_bwd", [(r["o_f"], 128, 0, True), (r["o_b"], 128, 0, True), (pm, 128, O_Z // 128, True)],
        [w["gnw"]], [dodn], tb, ncol=DN_HEADS)
    dares, dg_b = _dn_b_bwd(r["ares"], r["grows"], r["st_f"], r["st_b"], dof)
    dqkv, dgrows, dbrows = _dn_a_bwd(r["qkv"], r["grows"], r["brows"], dares, dg_b)
    dbg = _from_rows(dbrows, dgrows)
    dpbg, g["arow"], g["dtrow"] = _stage_bwd(_bg_f, "bg_bwd", [(r["pbg"], 128, 0, False)], [w["arow"], w["dtrow"]],
                                             [dbg], tb)
    dpre, g["conv"] = _prep_bwd(pm, w["conv"], dqkv)
    dpm = jnp.concatenate([dpre, dz, dq_sw, dk_sw, dv_sw, dgates], axis=1)
    dx = _mm(dpm, w["in_main"], tb=True, add=dxa, name="mm_in_dx")
    dx = _mm(dpbg, w["in_bg"], tb=True, add=dx, name="mm_in_bg_dx")
    g["in_main"] = _mm(r["x"], dpm, ta=True, name="mm_in_dw")
    g["in_bg"] = _mm(r["x"], dpbg, ta=True, name="mm_in_bg_dw")
    return dx, g


def _merge_bwd(r, pm, dmerged, tb):
    gates = [(pm, 512, O_GA // 512, True), (pm, 512, O_GB // 512, True)]
    dya, dyb, dga, dgb = _stage_bwd(_merge_f, "merge_bwd", [(r["ya"], 512, 0, True), (r["yb"], 512, 0, True)] + gates,
                                    [], [dmerged], tb, ncol=2)
    return dya, dyb, jnp.concatenate([dga, dgb], axis=1)


def _place():
    return lax.axis_index("x"), lax.axis_index("y"), lax.axis_index("c")


def _other_chips(x, y):
    return [(1 - x, y), (x, 1 - y), (1 - x, 1 - y)]


def _gather_weights(locs, kinds):
    n = len(locs)

    def dst(ref, kind, q, width):
        return ref.at[:, q] if kind == "row" else ref.at[:, :, pl.ds(pl.multiple_of(q * width, 128), width)]

    def body(*refs):
        loc_refs, out_refs = refs[:n], refs[n:2 * n]
        send_sems, recv_sems, local_sems = refs[2 * n:]
        x, y, c = _place()
        myq = 2 * x + y
        chips = _other_chips(x, y)
        started = []
        for t in range(n):
            width = locs[t].shape[-1]
            mine = pltpu.make_async_copy(loc_refs[t], dst(out_refs[t], kinds[t], myq, width), local_sems.at[t])
            mine.start()
            started.append(mine)
            for j, (cx, cy) in enumerate(chips):
                cp = pltpu.make_async_remote_copy(
                    src_ref=loc_refs[t], dst_ref=dst(out_refs[t], kinds[t], myq, width),
                    send_sem=send_sems.at[3 * t + j], recv_sem=recv_sems.at[3 * t + j],
                    device_id=(cx, cy, c), device_id_type=MESH)
                cp.start()
        for t in range(n):
            width = locs[t].shape[-1]
            for j, (cx, cy) in enumerate(chips):
                cp = pltpu.make_async_remote_copy(
                    src_ref=loc_refs[t], dst_ref=dst(out_refs[t], kinds[t], 2 * cx + cy, width),
                    send_sem=send_sems.at[3 * t + j], recv_sem=recv_sems.at[3 * t + j],
                    device_id=(cx, cy, c), device_id_type=MESH)
                cp.wait_recv()
                cp.wait_send()
        for mine in started:
            mine.wait()

    out_shape = []
    for a, kind in zip(locs, kinds):
        l_, r_, c_ = a.shape
        out_shape.append(jax.ShapeDtypeStruct((l_, N_CHIPS, r_, c_) if kind == "row" else (l_, r_, N_CHIPS * c_), a.dtype))
    return pl.pallas_call(
        body, name="gather_weights", in_specs=[ANY] * n, out_specs=[ANY] * n, out_shape=out_shape,
        scratch_shapes=[pltpu.SemaphoreType.DMA((3 * n,)), pltpu.SemaphoreType.DMA((3 * n,)),
                        pltpu.SemaphoreType.DMA((n,))],
    )(*locs)


RS_CHUNKS = 4


def _piece(ref, kind, q, hf, pr, pc):
    if kind == "row":
        return ref.at[pl.ds((2 * q + hf) * pr, pr), :]
    return ref.at[pl.ds(hf * pr, pr), pl.ds(pl.multiple_of(q * pc, 128), pc)]


def _rs_to_sibling(ts, meta):
    n = len(ts)

    def body(*refs):
        t_refs, r1_refs, send_sems, recv_sems = refs[:n], refs[n:2 * n], refs[2 * n], refs[2 * n + 1]
        x, y, c = _place()
        sib = (x, y, 1 - c)
        for t, (kind, pr, pc) in enumerate(meta):
            for q in range(N_CHIPS):
                pltpu.make_async_remote_copy(
                    src_ref=_piece(t_refs[t], kind, q, 1 - c, pr, pc), dst_ref=r1_refs[t].at[q],
                    send_sem=send_sems.at[t], recv_sem=recv_sems.at[t], device_id=sib, device_id_type=MESH).start()
        for t in range(n):
            allq = pltpu.make_async_remote_copy(
                src_ref=r1_refs[t], dst_ref=r1_refs[t], send_sem=send_sems.at[t], recv_sem=recv_sems.at[t],
                device_id=sib, device_id_type=MESH)
            allq.wait_send()
            allq.wait_recv()

    return pl.pallas_call(
        body, name="rs_to_sibling", in_specs=[ANY] * n, out_specs=[ANY] * n,
        out_shape=[jax.ShapeDtypeStruct((N_CHIPS, pr, pc), F32) for (_, pr, pc) in meta],
        scratch_shapes=[pltpu.SemaphoreType.DMA((n,)), pltpu.SemaphoreType.DMA((n,))],
    )(*ts)


def _rs_add_sibling(ts, r1s, meta, c):
    n = len(ts)
    in_specs, out_specs, out_shape = [], [], []
    for kind, pr, pc in meta:
        rs = pr // RS_CHUNKS
        if kind == "row":
            in_specs.append(pl.BlockSpec((rs, pc), lambda q, r, c_ref: ((2 * q + c_ref[0]) * RS_CHUNKS + r, 0)))
        else:
            in_specs.append(pl.BlockSpec((rs, pc), lambda q, r, c_ref: (c_ref[0] * RS_CHUNKS + r, q)))
    for kind, pr, pc in meta:
        sp = pl.BlockSpec((None, pr // RS_CHUNKS, pc), lambda q, r, c_ref: (q, r, 0))
        in_specs.append(sp)
        out_specs.append(sp)
        out_shape.append(jax.ShapeDtypeStruct((N_CHIPS, pr, pc), F32))

    def body(c_ref, *refs):
        for t in range(n):
            refs[2 * n + t][...] = refs[t][...] + refs[n + t][...]

    return pl.pallas_call(
        body, name="rs_add_sibling", out_shape=out_shape,
        grid_spec=pltpu.PrefetchScalarGridSpec(num_scalar_prefetch=1, grid=(N_CHIPS, RS_CHUNKS), in_specs=in_specs,
                                               out_specs=out_specs),
        compiler_params=_params(2))(c.reshape(1).astype(jnp.int32), *ts, *r1s)


def _rs_to_chips(ps, meta):
    n = len(ps)

    def body(*refs):
        p_refs, r2_refs, send_sems, recv_sems = refs[:n], refs[n:2 * n], refs[2 * n], refs[2 * n + 1]
        x, y, c = _place()
        for t in range(n):
            for j, (cx, cy) in enumerate(_other_chips(x, y)):
                pltpu.make_async_remote_copy(
                    src_ref=p_refs[t].at[2 * cx + cy], dst_ref=r2_refs[t].at[j],
                    send_sem=send_sems.at[t], recv_sem=recv_sems.at[t], device_id=(cx, cy, c),
                    device_id_type=MESH).start()
        for t in range(n):
            allj = pltpu.make_async_remote_copy(
                src_ref=r2_refs[t], dst_ref=r2_refs[t], send_sem=send_sems.at[t], recv_sem=recv_sems.at[t],
                device_id=(x, y, c), device_id_type=MESH)
            allj.wait_send()
            allj.wait_recv()

    return pl.pallas_call(
        body, name="rs_to_chips", in_specs=[ANY] * n, out_specs=[ANY] * n,
        out_shape=[jax.ShapeDtypeStruct((N_CHIPS - 1, pr, pc), F32) for (_, pr, pc) in meta],
        scratch_shapes=[pltpu.SemaphoreType.DMA((n,)), pltpu.SemaphoreType.DMA((n,))],
    )(*ps)


def _rs_add_chips(ps, r2s, meta, myq):
    n = len(ps)
    in_specs, out_specs, out_shape = [], [], []
    for _, pr, pc in meta:
        in_specs.append(pl.BlockSpec((None, pr // RS_CHUNKS, pc), lambda r, q_ref: (q_ref[0], r, 0)))
    for _, pr, pc in meta:
        in_specs.append(pl.BlockSpec((N_CHIPS - 1, pr // RS_CHUNKS, pc), lambda r, q_ref: (0, r, 0)))
        out_specs.append(pl.BlockSpec((pr // RS_CHUNKS, pc), lambda r, q_ref: (r, 0)))
        out_shape.append(jax.ShapeDtypeStruct((pr, pc), F32))

    def body(q_ref, *refs):
        for t in range(n):
            r2 = refs[n + t]
            refs[2 * n + t][...] = ((refs[t][...] + r2[0]) + r2[1]) + r2[2]

    return pl.pallas_call(
        body, name="rs_add_chips", out_shape=out_shape,
        grid_spec=pltpu.PrefetchScalarGridSpec(num_scalar_prefetch=1, grid=(RS_CHUNKS,), in_specs=in_specs,
                                               out_specs=out_specs),
        compiler_params=_params(1))(myq.reshape(1).astype(jnp.int32), *ps, *r2s)


def _rs_share_halves(fs, meta):
    n = len(fs)

    def body(*refs):
        f_refs, g_refs = refs[:n], refs[n:2 * n]
        send_sems, recv_sems, local_sems = refs[2 * n:]
        x, y, c = _place()
        sib = (x, y, 1 - c)
        local = []
        for t in range(n):
            cp = pltpu.make_async_copy(f_refs[t], g_refs[t].at[c], local_sems.at[t])
            cp.start()
            local.append(cp)
            pltpu.make_async_remote_copy(
                src_ref=f_refs[t], dst_ref=g_refs[t].at[c], send_sem=send_sems.at[t], recv_sem=recv_sems.at[t],
                device_id=sib, device_id_type=MESH).start()
        for t in range(n):
            cp = pltpu.make_async_remote_copy(
                src_ref=f_refs[t], dst_ref=g_refs[t].at[1 - c], send_sem=send_sems.at[t], recv_sem=recv_sems.at[t],
                device_id=sib, device_id_type=MESH)
            cp.wait_send()
            cp.wait_recv()
            local[t].wait()

    return pl.pallas_call(
        body, name="rs_share_halves", in_specs=[ANY] * n, out_specs=[ANY] * n,
        out_shape=[jax.ShapeDtypeStruct((2, pr, pc), F32) for (_, pr, pc) in meta],
        scratch_shapes=[pltpu.SemaphoreType.DMA((n,)), pltpu.SemaphoreType.DMA((n,)), pltpu.SemaphoreType.DMA((n,))],
    )(*fs)


def _reduce_scatter(ts, meta, c, myq):
    r1s = _rs_to_sibling(ts, meta)
    ps = _rs_add_sibling(ts, r1s, meta, c)
    r2s = _rs_to_chips(ps, meta)
    fs = _rs_add_chips(ps, r2s, meta, myq)
    return _rs_share_halves(fs, meta)


def _allreduce_small(buf):
    rows = buf.shape[0]
    ndev = 8

    def body(b_ref, o_ref, slots, send_sems, recv_sems):
        x, y, c = _place()
        me = 4 * x + 2 * y + c
        slots[me] = b_ref[...]
        for k in range(1, ndev):
            kx, ky, kc = (k >> 2) & 1, (k >> 1) & 1, k & 1
            peer = (x ^ kx, y ^ ky, c ^ kc)
            pltpu.make_async_remote_copy(
                src_ref=b_ref, dst_ref=slots.at[me], send_sem=send_sems.at[k - 1], recv_sem=recv_sems.at[k - 1],
                device_id=peer, device_id_type=MESH).start()
        for k in range(1, ndev):
            kx, ky, kc = (k >> 2) & 1, (k >> 1) & 1, k & 1
            cp = pltpu.make_async_remote_copy(
                src_ref=b_ref, dst_ref=slots.at[me ^ k], send_sem=send_sems.at[k - 1], recv_sem=recv_sems.at[k - 1],
                device_id=(x ^ kx, y ^ ky, c ^ kc), device_id_type=MESH)
            cp.wait_send()
            cp.wait_recv()
        acc = slots[0]
        for s in range(1, ndev):
            acc = acc + slots[s]
        o_ref[...] = acc

    vm = pl.BlockSpec(memory_space=pltpu.VMEM)
    return pl.pallas_call(
        body, name="allreduce_small", in_specs=[vm], out_specs=vm, out_shape=jax.ShapeDtypeStruct((rows, 128), F32),
        scratch_shapes=[pltpu.VMEM((ndev, rows, 128), F32), pltpu.SemaphoreType.DMA((ndev - 1,)),
                        pltpu.SemaphoreType.DMA((ndev - 1,))],
        compiler_params=pltpu.CompilerParams(vmem_limit_bytes=VMEM_LIMIT))(buf)


RS_META = [("col", D // 2, IN_PAD), ("row", D // 8, D), ("row", D // 8, D), ("row", D // 8, D),
           ("col", D // 2, 2 * FFN // N_CHIPS), ("row", FFN // 8, D)]
SMALL_ROWS = 156


def _rope_tables(t_):
    half = SW_DIM // 2
    inv_freq = ROPE_THETA ** (-jnp.arange(half, dtype=F32) / half)
    ang = jnp.arange(t_, dtype=F32)[:, None] * inv_freq[None, :]
    return jnp.concatenate([jnp.cos(ang)] * 2, axis=1), jnp.concatenate([jnp.sin(ang)] * 2, axis=1)


def _lane_row(v16):
    return jnp.pad(v16.reshape(1, 2 * DN_HEADS), ((0, 0), (2 * DN_HEADS, 128 - 4 * DN_HEADS)))


def _pack_small(g):
    pad16 = jnp.pad(g["sinks"], ((0, 0), (0, 128 - SW_HEADS)))
    return jnp.concatenate([g["conv"].reshape(-1, 128), g["ln1g"].reshape(-1, 128), g["ln1b"].reshape(-1, 128),
                            g["ln2g"].reshape(-1, 128), g["ln2b"].reshape(-1, 128), g["gnw"], g["arow"], g["dtrow"],
                            pad16], axis=0)


def _unpack_small(buf):
    nconv = DN_CONV * 3 * D // 128
    o = nconv
    out = dict(conv=buf[:o].reshape(DN_CONV, 3 * D))
    for name in ("ln1g", "ln1b", "ln2g", "ln2b"):
        out[name] = buf[o:o + 8].reshape(D)
        o += 8
    out["gnw"] = buf[o]
    out["a_log"] = buf[o + 1, 2 * DN_HEADS:4 * DN_HEADS].reshape(2, DN_HEADS)
    out["dt_bias"] = buf[o + 2, 2 * DN_HEADS:4 * DN_HEADS].reshape(2, DN_HEADS)
    out["sinks"] = buf[o + 3, :SW_HEADS]
    return out


def kernel(x, w_in, conv_w, a_log, dt_bias, dn_norm_w, sinks, w_branch_a, w_branch_b, w_out, ln1_g, ln1_b, w_gate_up, w_down, ln2_g, ln2_b, loss_target, m_w_in, m_conv_w, m_a_log, m_dt_bias, m_dn_norm_w, m_sinks, m_w_branch_a, m_w_branch_b, m_w_out, m_ln1_g, m_ln1_b, m_w_gate_up, m_w_down, m_ln2_g, m_ln2_b, v_w_in, v_conv_w, v_a_log, v_dt_bias, v_dn_norm_w, v_sinks, v_w_branch_a, v_w_branch_b, v_w_out, v_ln1_g, v_ln1_b, v_w_gate_up, v_w_down, v_ln2_g, v_ln2_b):
    xi, yi, ci = _place()
    myq = 2 * xi + yi
    t_ = x.shape[1]
    cos, sin = _rope_tables(t_)

    loc_in = jnp.pad(w_in.astype(BF16), ((0, 0), (0, 0), (0, IN_PAD - IN_SHARD)))
    full_in, full_a, full_b, full_o, full_gu, full_d, full_conv = _gather_weights(
        [loc_in, w_branch_a.astype(BF16), w_branch_b.astype(BF16), w_out.astype(BF16), w_gate_up.astype(BF16),
         w_down.astype(BF16), conv_w], ["col", "row", "row", "row", "col", "row", "col"])
    layers = []
    for l in range(DEPTH):
        orig = jnp.concatenate([full_in[l, :, q * IN_PAD:q * IN_PAD + IN_SHARD] for q in range(N_CHIPS)], axis=1)
        layers.append(dict(
            in_main=jnp.concatenate([orig[:, :R_BG], orig[:, R_BG + 4 * DN_HEADS:]], axis=1),
            in_bg=jnp.pad(orig[:, R_BG:R_BG + 4 * DN_HEADS], ((0, 0), (0, 128 - 4 * DN_HEADS))),
            conv=full_conv[l], arow=_lane_row(a_log[l]), dtrow=_lane_row(dt_bias[l]), gnw=dn_norm_w[l][None],
            sinks=sinks[l][None], a=full_a[l].reshape(D, D), b=full_b[l].reshape(D, D), o=full_o[l].reshape(D, D),
            ln1g=ln1_g[l][None], ln1b=ln1_b[l][None], gu=full_gu[l], d=full_d[l].reshape(FFN, D),
            ln2g=ln2_g[l][None], ln2b=ln2_b[l][None]))

    h = x[0]
    residuals = []
    for l in range(DEPTH):
        h, res = _layer_fwd(h, layers[l], cos, sin)
        residuals.append(res)
    dh, sq = _loss_head(h, loss_target[0])
    loss = lax.psum((0.5 / D) * jnp.sum(sq), ("x", "y", "c"))

    big = [None] * DEPTH
    small = [None] * DEPTH
    for l in reversed(range(DEPTH)):
        dh, g = _layer_bwd(dh, residuals[l], layers[l], cos, sin)
        g_orig = jnp.concatenate([g["in_main"][:, :R_BG], g["in_bg"][:, :4 * DN_HEADS], g["in_main"][:, R_BG:]], axis=1)
        g_in = jnp.concatenate(
            [jnp.pad(g_orig[:, q * IN_SHARD:(q + 1) * IN_SHARD], ((0, 0), (0, IN_PAD - IN_SHARD)))
             for q in range(N_CHIPS)], axis=1)
        big[l] = _reduce_scatter([g_in, g["a"], g["b"], g["o"], g["gu"], g["d"]], RS_META, ci, myq)
        small[l] = _pack_small(g)
    tot = _allreduce_small(jnp.concatenate(small, axis=0))
    sm = [_unpack_small(tot[l * SMALL_ROWS:(l + 1) * SMALL_ROWS]) for l in range(DEPTH)]
    stack = lambda name: jnp.stack([s[name] for s in sm], axis=0)

    grads = dict(
        w_in=jnp.stack([big[l][0].reshape(D, IN_PAD)[:, :IN_SHARD] for l in range(DEPTH)]),
        conv_w=lax.dynamic_slice_in_dim(stack("conv"), myq * (3 * D // N_CHIPS), 3 * D // N_CHIPS, axis=2),
        a_log=stack("a_log"), dt_bias=stack("dt_bias"), dn_norm_w=stack("gnw"), sinks=stack("sinks"),
        w_branch_a=jnp.stack([big[l][1].reshape(D // N_CHIPS, D) for l in range(DEPTH)]),
        w_branch_b=jnp.stack([big[l][2].reshape(D // N_CHIPS, D) for l in range(DEPTH)]),
        w_out=jnp.stack([big[l][3].reshape(D // N_CHIPS, D) for l in range(DEPTH)]),
        ln1_g=stack("ln1g"), ln1_b=stack("ln1b"),
        w_gate_up=jnp.stack([big[l][4].reshape(D, 2 * FFN // N_CHIPS) for l in range(DEPTH)]),
        w_down=jnp.stack([big[l][5].reshape(FFN // N_CHIPS, D) for l in range(DEPTH)]),
        ln2_g=stack("ln2g"), ln2_b=stack("ln2b"))
    weights = dict(w_in=w_in, conv_w=conv_w, a_log=a_log, dt_bias=dt_bias, dn_norm_w=dn_norm_w, sinks=sinks,
                   w_branch_a=w_branch_a, w_branch_b=w_branch_b, w_out=w_out, ln1_g=ln1_g, ln1_b=ln1_b,
                   w_gate_up=w_gate_up, w_down=w_down, ln2_g=ln2_g, ln2_b=ln2_b)
    ms = dict(w_in=m_w_in, conv_w=m_conv_w, a_log=m_a_log, dt_bias=m_dt_bias, dn_norm_w=m_dn_norm_w, sinks=m_sinks,
              w_branch_a=m_w_branch_a, w_branch_b=m_w_branch_b, w_out=m_w_out, ln1_g=m_ln1_g, ln1_b=m_ln1_b,
              w_gate_up=m_w_gate_up, w_down=m_w_down, ln2_g=m_ln2_g, ln2_b=m_ln2_b)
    vs = dict(w_in=v_w_in, conv_w=v_conv_w, a_log=v_a_log, dt_bias=v_dt_bias, dn_norm_w=v_dn_norm_w, sinks=v_sinks,
              w_branch_a=v_w_branch_a, w_branch_b=v_w_branch_b, w_out=v_w_out, ln1_g=v_ln1_g, ln1_b=v_ln1_b,
              w_gate_up=v_w_gate_up, w_down=v_w_down, ln2_g=v_ln2_g, ln2_b=v_ln2_b)
    names = list(weights)
    upd = {n: _adamw("adamw_" + n, weights[n], grads[n], ms[n], vs[n]) for n in names}
    return (loss, dh[None], *[grads[n] for n in names], *[upd[n][0] for n in names], *[upd[n][1] for n in names],
            *[upd[n][2] for n in names])
```

```python
import functools

import jax
import jax.numpy as jnp
from jax import lax
from jax.experimental import pallas as pl
from jax.experimental.pallas import tpu as pltpu

F32 = jnp.float32
BF16 = jnp.bfloat16
_MXU = BF16

D = 1024
DEPTH = 4
DN_HEADS = 8
DN_DIM = 128
DN_CONV = 5
CHUNK = 64
SW_HEADS = 16
SW_KV = 4
SW_DIM = 64
SW_GRP = SW_HEADS // SW_KV
SW_BLOCK = 128
ROPE_THETA = 10000.0
FFN = 2816
ALPHA = (2.0 * DEPTH) ** 0.25
LN_EPS = 1e-5
RMS_EPS = 1e-6
IN_COLS = 7712
O_Z, O_QS, O_KS, O_VS, O_GA, O_GB, N_MAIN = 3072, 4096, 5120, 5376, 5632, 6656, 7680
R_BG = 4096
N_CHIPS = 4
IN_SHARD = IN_COLS // N_CHIPS
IN_PAD = 2048
ADAM_LR, ADAM_B1, ADAM_B2, ADAM_EPS, ADAM_WD, ADAM_STEP = 0.001, 0.9, 0.999, 1e-08, 0.01, 10
VMEM_LIMIT = 52 * 1024 * 1024
MESH = pl.DeviceIdType.MESH
ANY = pl.BlockSpec(memory_space=pl.ANY)


def _params(n_grid, **kw):
    return pltpu.CompilerParams(dimension_semantics=("arbitrary",) * n_grid, vmem_limit_bytes=VMEM_LIMIT, **kw)


def _full(a):
    nd = a.ndim
    return pl.BlockSpec(a.shape, lambda *_, nd=nd: (0,) * nd)


def _raw_dot(a, b, ca, cb):
    return lax.dot_general(a.astype(_MXU), b.astype(_MXU), (((ca,), (cb,)), ((), ())), preferred_element_type=F32)


@jax.custom_vjp
def _nn(a, b):
    return _raw_dot(a, b, 1, 0)


@jax.custom_vjp
def _nt(a, b):
    return _raw_dot(a, b, 1, 1)


@jax.custom_vjp
def _tn(a, b):
    return _raw_dot(a, b, 0, 0)


_nn.defvjp(lambda a, b: (_nn(a, b), (a, b)), lambda r, g: (_nt(g, r[1]), _tn(r[0], g)))
_nt.defvjp(lambda a, b: (_nt(a, b), (a, b)), lambda r, g: (_nn(g, r[1]), _tn(g, r[0])))
_tn.defvjp(lambda a, b: (_tn(a, b), (a, b)), lambda r, g: (_nt(r[1], g), _nn(r[0], g)))


def _hdot(a, b, ca=1, cb=0):
    ah, bh = a.astype(BF16), b.astype(BF16)
    al, bl = (a - ah.astype(F32)).astype(BF16), (b - bh.astype(F32)).astype(BF16)
    dot = lambda u, v: lax.dot_general(u, v, (((ca,), (cb,)), ((), ())), preferred_element_type=F32)
    return dot(ah, bh) + (dot(ah, bl) + dot(al, bh))


def _inv_impl(mats):
    n = mats[0].shape[0]
    eye = (lax.broadcasted_iota(jnp.int32, (n, n), 0) == lax.broadcasted_iota(jnp.int32, (n, n), 1)).astype(F32)
    ps = [-a for a in mats]
    ts = [eye + p for p in ps]
    for _ in range(max(1, (n - 1).bit_length()) - 1):
        ps = [_hdot(p, p) for p in ps]
        ts = [t + _hdot(t, p) for t, p in zip(ts, ps)]
    return tuple(ts)


@jax.custom_vjp
def _inv(mats):
    return _inv_impl(mats)


def _inv_fwd(mats):
    ts = _inv_impl(mats)
    return ts, ts


def _inv_bwd(ts, gs):
    xs = [_hdot(t, g, 0, 0) for t, g in zip(ts, gs)]
    return (tuple(-_hdot(x, t, 1, 1) for x, t in zip(xs, ts)),)


_inv.defvjp(_inv_fwd, _inv_bwd)


def _tile(n, cap):
    if n <= cap:
        return n
    best = [t for t in range(128, cap + 1, 128) if n % t == 0]
    assert best, (n, cap)
    return best[-1]


def _mm(a, b, *, name, ta=False, tb=False, add=None, tm=1024, tn=512, tk=1024):
    if ta:
        k_, m_ = a.shape
    else:
        m_, k_ = a.shape
    n_ = b.shape[0] if tb else b.shape[1]
    tm, tn, tk = _tile(m_, tm), _tile(n_, tn), _tile(k_, tk)
    nk = k_ // tk
    has_add = add is not None

    def body(*refs):
        if has_add:
            a_ref, b_ref, add_ref, o_ref, acc = refs
        else:
            a_ref, b_ref, o_ref, acc = refs
        k = pl.program_id(2)

        @pl.when(k == 0)
        def _():
            acc[...] = jnp.zeros_like(acc)

        acc[...] += _raw_dot(a_ref[...], b_ref[...], 0 if ta else 1, 1 if tb else 0)

        @pl.when(k == nk - 1)
        def _():
            if has_add:
                o_ref[...] = acc[...] + add_ref[...]
            else:
                o_ref[...] = acc[...]

    a_spec = pl.BlockSpec((tk, tm), lambda i, j, k: (k, i)) if ta else pl.BlockSpec((tm, tk), lambda i, j, k: (i, k))
    b_spec = pl.BlockSpec((tn, tk), lambda i, j, k: (j, k)) if tb else pl.BlockSpec((tk, tn), lambda i, j, k: (k, j))
    o_spec = pl.BlockSpec((tm, tn), lambda i, j, k: (i, j))
    in_specs = [a_spec, b_spec] + ([o_spec] if has_add else [])
    args = (a, b) + ((add,) if has_add else ())
    return pl.pallas_call(
        body, name=name, grid=(m_ // tm, n_ // tn, nk), in_specs=in_specs, out_specs=o_spec,
        out_shape=jax.ShapeDtypeStruct((m_, n_), F32), scratch_shapes=[pltpu.VMEM((tm, tn), F32)],
        compiler_params=_params(3))(*args)


def _row_spec(tb, w, c0, percol):
    return pl.BlockSpec((tb, w), lambda i, j, c0=c0, pc=percol: (i, c0 + (j if pc else 0)))


def _stage_fwd(f, name, rows, params, out_widths, tb, ncol=1):
    t_ = rows[0][0].shape[0]
    nr, npar = len(rows), len(params)

    def body(*refs):
        res = f(*[r[...] for r in refs[:nr + npar]])
        for o_ref, val in zip(refs[nr + npar:], res):
            o_ref[...] = val

    return pl.pallas_call(
        body, name=name, grid=(t_ // tb, ncol),
        in_specs=[_row_spec(tb, w, c0, pc) for (_, w, c0, pc) in rows] + [_full(p) for p in params],
        out_specs=[pl.BlockSpec((tb, w), lambda i, j: (i, j)) for w in out_widths],
        out_shape=[jax.ShapeDtypeStruct((t_, w * ncol), F32) for w in out_widths],
        compiler_params=_params(2))(*[r[0] for r in rows], *params)


def _stage_bwd(f, name, rows, params, douts, tb, ncol=1, cat=None):
    t_ = rows[0][0].shape[0]
    nr, npar, nd = len(rows), len(params), len(douts)
    cat = cat if cat is not None else [[r] for r in range(nr)]
    assert ncol == 1 or all(len(g) == 1 and rows[g[0]][3] for g in cat)

    def body(*refs):
        ins = [r[...] for r in refs[:nr + npar]]
        dvals = tuple(r[...] for r in refs[nr + npar:nr + npar + nd])
        out_refs = refs[nr + npar + nd:]
        _, vjp = jax.vjp(f, *ins)
        grads = vjp(dvals)
        for o_ref, grp in zip(out_refs[:len(cat)], cat):
            o_ref[...] = grads[grp[0]] if len(grp) == 1 else jnp.concatenate([grads[r] for r in grp], axis=-1)
        first = jnp.logical_and(pl.program_id(0) == 0, pl.program_id(1) == 0)
        for p_ref, gp in zip(out_refs[len(cat):], grads[nr:]):
            @pl.when(first)
            def _(p_ref=p_ref):
                p_ref[...] = jnp.zeros_like(p_ref)
            p_ref[...] += gp

    gw = [sum(rows[r][1] for r in grp) for grp in cat]
    return pl.pallas_call(
        body, name=name, grid=(t_ // tb, ncol),
        in_specs=[_row_spec(tb, w, c0, pc) for (_, w, c0, pc) in rows] + [_full(p) for p in params]
        + [pl.BlockSpec((tb, d.shape[1] // ncol), lambda i, j: (i, j)) for d in douts],
        out_specs=[pl.BlockSpec((tb, w), lambda i, j: (i, j)) for w in gw] + [_full(p) for p in params],
        out_shape=[jax.ShapeDtypeStruct((t_, w * ncol), F32) for w in gw]
        + [jax.ShapeDtypeStruct(p.shape, F32) for p in params],
        compiler_params=_params(2))(*[r[0] for r in rows], *params, *douts)


def _ln_f(x, y, g, b):
    u = ALPHA * x + y
    c = u - jnp.mean(u, axis=-1, keepdims=True)
    var = jnp.mean(c * c, axis=-1, keepdims=True)
    return (c * lax.rsqrt(var + LN_EPS) * g + b,)


def _swiglu_f(gate, up):
    return (jax.nn.silu(gate) * up,)


def _merge_f(ya, yb, ga, gb):
    return (jax.nn.sigmoid(ga) * ya + jax.nn.sigmoid(gb) * yb,)


def _gnorm_f(of, ob, z, w):
    o = of + ob
    return (o * lax.rsqrt(jnp.mean(o * o, axis=-1, keepdims=True) + RMS_EPS) * w * jax.nn.silu(z),)


def _bg_f(x, arow, dtrow):
    lane = lax.broadcasted_iota(jnp.int32, x.shape, 1)
    beta = jax.nn.sigmoid(x)
    g = -jnp.exp(arow) * jax.nn.softplus(x + dtrow)
    return (jnp.where(lane < 16, beta, jnp.where(lane < 32, g, 0.0)),)


PREP_ROWS = 512
PAD = 8


def _prep_f(part, w, *wins):
    xc = wins[0] * w[0:1, :]
    for k in range(1, DN_CONV):
        xc = xc + wins[k] * w[k:k + 1, :]
    a = jax.nn.silu(xc)
    nrm = a * lax.rsqrt(jnp.sum(a * a, axis=-1, keepdims=True) + RMS_EPS)
    return jnp.where(part == 0, nrm * (DN_DIM ** -0.5), jnp.where(part == 1, nrm, a))


def _windows(pad_ref, r0, rows):
    return [pad_ref[PAD + r0 - 2 + k:PAD + r0 - 2 + k + rows, :] for k in range(DN_CONV)]


def _prep_fwd(pm, conv):
    t_ = pm.shape[0]
    rows = min(PREP_ROWS, t_)

    def body(x_ref, w_ref, o_ref, pad_ref):
        part = pl.program_id(0) // DN_HEADS
        pad_ref[0:PAD, :] = jnp.zeros((PAD, DN_DIM), F32)
        pad_ref[PAD + t_:2 * PAD + t_, :] = jnp.zeros((PAD, DN_DIM), F32)
        pad_ref[PAD:PAD + t_, :] = x_ref[...]
        w = w_ref[...]
        for r in range(t_ // rows):
            o_ref[r * rows:(r + 1) * rows, :] = _prep_f(part, w, *_windows(pad_ref, r * rows, rows))

    ncb = 3 * DN_HEADS
    return pl.pallas_call(
        body, name="prep_fwd", grid=(ncb,),
        in_specs=[pl.BlockSpec((t_, DN_DIM), lambda j: (0, j)), pl.BlockSpec((DN_CONV, DN_DIM), lambda j: (0, j))],
        out_specs=pl.BlockSpec((t_, DN_DIM), lambda j: (0, j)),
        out_shape=jax.ShapeDtypeStruct((t_, ncb * DN_DIM), F32),
        scratch_shapes=[pltpu.VMEM((t_ + 2 * PAD, DN_DIM), F32)],
        compiler_params=_params(1))(pm, conv)


def _prep_bwd(pm, conv, dout):
    t_ = pm.shape[0]
    rows = min(PREP_ROWS, t_)

    def body(x_ref, w_ref, d_ref, dx_ref, dw_ref, pad_ref, dpad_ref):
        part = pl.program_id(0) // DN_HEADS
        pad_ref[0:PAD, :] = jnp.zeros((PAD, DN_DIM), F32)
        pad_ref[PAD + t_:2 * PAD + t_, :] = jnp.zeros((PAD, DN_DIM), F32)
        pad_ref[PAD:PAD + t_, :] = x_ref[...]
        dpad_ref[...] = jnp.zeros_like(dpad_ref)
        w = w_ref[...]
        dw = jnp.zeros((DN_CONV, DN_DIM), F32)
        for r in range(t_ // rows):
            r0 = r * rows
            _, vjp = jax.vjp(functools.partial(_prep_f, part), w, *_windows(pad_ref, r0, rows))
            grads = vjp(d_ref[r0:r0 + rows, :])
            dw = dw + grads[0]
            for k in range(DN_CONV):
                lo = PAD + r0 - 2 + k
                dpad_ref[lo:lo + rows, :] += grads[1 + k]
        dx_ref[...] = dpad_ref[PAD:PAD + t_, :]
        dw_ref[...] = dw

    ncb = 3 * DN_HEADS
    col = pl.BlockSpec((t_, DN_DIM), lambda j: (0, j))
    wsp = pl.BlockSpec((DN_CONV, DN_DIM), lambda j: (0, j))
    return pl.pallas_call(
        body, name="prep_bwd", grid=(ncb,), in_specs=[col, wsp, col], out_specs=[col, wsp],
        out_shape=[jax.ShapeDtypeStruct((t_, ncb * DN_DIM), F32), jax.ShapeDtypeStruct((DN_CONV, ncb * DN_DIM), F32)],
        scratch_shapes=[pltpu.VMEM((t_ + 2 * PAD, DN_DIM), F32), pltpu.VMEM((t_ + 2 * PAD, DN_DIM), F32)],
        compiler_params=_params(1))(pm, conv, dout)


def _dn_chunk(sgns, qs, ks, vs, grows, brows):
    c = qs[0].shape[0]
    i = lax.broadcasted_iota(jnp.int32, (c, c), 0)
    j = lax.broadcasted_iota(jnp.int32, (c, c), 1)
    eye = i == j
    incl = {s: (i - j) * int(s) >= 0 for s in set(sgns)}
    strict = {s: (i - j) * int(s) > 0 for s in set(sgns)}
    gcs = [jnp.sum(jnp.where(incl[s], g, 0.0), axis=1, keepdims=True) for s, g in zip(sgns, grows)]
    grs = [jnp.sum(jnp.where(eye, gc, 0.0), axis=0, keepdims=True) for gc in gcs]
    bcs = [jnp.sum(jnp.where(eye, b, 0.0), axis=1, keepdims=True) for b in brows]
    gls = [jnp.sum(g, axis=1, keepdims=True) for g in grows]
    decs = [jnp.exp(jnp.where(incl[s], gc - gr, -1e30)) for s, gc, gr in zip(sgns, gcs, grs)]
    kks = [_nt(k, k) for k in ks]
    tinvs = _inv(tuple(jnp.where(strict[s], bc * kk * dec, 0.0) for s, bc, kk, dec in zip(sgns, bcs, kks, decs)))
    egcs = [jnp.exp(gc) for gc in gcs]
    us = [_nn(t, v * bc) for t, v, bc in zip(tinvs, vs, bcs)]
    ws = [_nn(t, k * (bc * egc)) for t, k, bc, egc in zip(tinvs, ks, bcs, egcs)]
    qks = [_nt(q, k) * dec for q, k, dec in zip(qs, ks, decs)]
    qds = [q * egc for q, egc in zip(qs, egcs)]
    kds = [k * jnp.exp(gl - gc) for k, gl, gc in zip(ks, gls, gcs)]
    return tuple(us), tuple(ws), tuple(qks), tuple(qds), tuple(kds)


def _dn_step(us, ws, qks, qds, kds, grows, ss):
    gls = [jnp.exp(jnp.sum(g, axis=1, keepdims=True)) for g in grows]
    wss = [_nn(w, s) for w, s in zip(ws, ss)]
    qss = [_nn(qd, s) for qd, s in zip(qds, ss)]
    vns = [u - x for u, x in zip(us, wss)]
    os_ = [a + _nn(qk, vn) for a, qk, vn in zip(qss, qks, vns)]
    s2s = [s * gl + _tn(kd, vn) for s, gl, kd, vn in zip(ss, gls, kds, vns)]
    return tuple(os_), tuple(s2s)


def _hs(h):
    return slice(h * DN_DIM, (h + 1) * DN_DIM)


_DIR_SGN = (1, -1)
_A_OUT = 5
_PROBLEMS = [(d, h) for d in range(2) for h in range(DN_HEADS)]
_SGNS = [_DIR_SGN[d] for d, _ in _PROBLEMS]


def _chunk_inputs(q_ref, k_ref, v_ref, g_ref, b_ref):
    heads = lambda ref: tuple(ref[:, _hs(h)] for _, h in _PROBLEMS)
    rows = lambda ref: tuple(ref[d, 0, h:h + 1, :] for d, h in _PROBLEMS)
    return heads(q_ref), heads(k_ref), heads(v_ref), rows(g_ref), rows(b_ref)


def _dn_a_fwd(qkv, grows, brows):
    t_ = qkv.shape[0]
    nch = t_ // CHUNK

    def body(q_ref, k_ref, v_ref, g_ref, b_ref, *outs):
        us, ws, qks, qds, kds = _dn_chunk(_SGNS, *_chunk_inputs(q_ref, k_ref, v_ref, g_ref, b_ref))
        for p, (d, h) in enumerate(_PROBLEMS):
            u_ref, w_ref, qk_ref, qd_ref, kd_ref = outs[d * _A_OUT:(d + 1) * _A_OUT]
            u_ref[:, _hs(h)], w_ref[:, _hs(h)], qk_ref[0, h] = us[p], ws[p], qks[p]
            qd_ref[:, _hs(h)], kd_ref[:, _hs(h)] = qds[p], kds[p]

    rspec = pl.BlockSpec((2, 1, DN_HEADS, CHUNK), lambda c: (0, c, 0, 0))
    big = pl.BlockSpec((CHUNK, D), lambda c: (c, 0))
    qks = pl.BlockSpec((1, DN_HEADS, CHUNK, CHUNK), lambda c: (c, 0, 0, 0))
    bigs = jax.ShapeDtypeStruct((t_, D), F32)
    qksh = jax.ShapeDtypeStruct((nch, DN_HEADS, CHUNK, CHUNK), F32)
    return pl.pallas_call(
        body, name="dn_a_fwd", grid=(nch,),
        in_specs=[pl.BlockSpec((CHUNK, D), lambda c, p=p: (c, p)) for p in range(3)] + [rspec, rspec],
        out_specs=[big, big, qks, big, big] * 2, out_shape=[bigs, bigs, qksh, bigs, bigs] * 2,
        compiler_params=_params(1))(qkv, qkv, qkv, grows, brows)


def _dn_a_bwd(qkv, grows, brows, dres, dg_b):
    t_ = qkv.shape[0]
    nch = t_ // CHUNK

    def body(q_ref, k_ref, v_ref, g_ref, b_ref, *rest):
        dins, dgb_ref, (dqkv_ref, dg_ref, db_ref) = rest[:2 * _A_OUT], rest[2 * _A_OUT], rest[2 * _A_OUT + 1:]
        _, vjp = jax.vjp(functools.partial(_dn_chunk, _SGNS), *_chunk_inputs(q_ref, k_ref, v_ref, g_ref, b_ref))
        cots = []
        for o in range(_A_OUT):
            cots.append(tuple(dins[d * _A_OUT + o][0, h] if o == 2 else dins[d * _A_OUT + o][:, _hs(h)]
                              for d, h in _PROBLEMS))
        gq, gk, gv, gg, gb = vjp(tuple(cots))
        for p, (d, h) in enumerate(_PROBLEMS):
            dg_ref[d, 0, h:h + 1, :] = gg[p] + dgb_ref[d, 0, h:h + 1, :]
            db_ref[d, 0, h:h + 1, :] = gb[p]
        for h in range(DN_HEADS):
            dqkv_ref[:, _hs(h)] = gq[h] + gq[DN_HEADS + h]
            dqkv_ref[:, _hs(DN_HEADS + h)] = gk[h] + gk[DN_HEADS + h]
            dqkv_ref[:, _hs(2 * DN_HEADS + h)] = gv[h] + gv[DN_HEADS + h]

    rspec = pl.BlockSpec((2, 1, DN_HEADS, CHUNK), lambda c: (0, c, 0, 0))
    big = pl.BlockSpec((CHUNK, D), lambda c: (c, 0))
    qks = pl.BlockSpec((1, DN_HEADS, CHUNK, CHUNK), lambda c: (c, 0, 0, 0))
    rsh = jax.ShapeDtypeStruct(grows.shape, F32)
    return pl.pallas_call(
        body, name="dn_a_bwd", grid=(nch,),
        in_specs=[pl.BlockSpec((CHUNK, D), lambda c, p=p: (c, p)) for p in range(3)] + [rspec, rspec]
        + [big, big, qks, big, big] * 2 + [rspec],
        out_specs=[pl.BlockSpec((CHUNK, 3 * D), lambda c: (c, 0)), rspec, rspec],
        out_shape=[jax.ShapeDtypeStruct((t_, 3 * D), F32), rsh, rsh],
        compiler_params=_params(1))(qkv, qkv, qkv, grows, brows, *dres, dg_b)


def _dir_specs(nch):
    def cidx(d):
        return (lambda n: n) if d == 0 else (lambda n: nch - 1 - n)
    out = []
    for d in range(2):
        ci = cidx(d)
        big = pl.BlockSpec((CHUNK, D), lambda n, ci=ci: (ci(n), 0))
        qks = pl.BlockSpec((1, DN_HEADS, CHUNK, CHUNK), lambda n, ci=ci: (ci(n), 0, 0, 0))
        row = pl.BlockSpec((1, 1, DN_HEADS, CHUNK), lambda n, ci=ci, d=d: (d, ci(n), 0, 0))
        st = pl.BlockSpec((1, DN_HEADS, DN_DIM, DN_DIM), lambda n, ci=ci: (ci(n), 0, 0, 0))
        out.append(dict(big=big, qk=qks, row=row, st=st))
    return out


def _step_inputs(ins, per_dir):
    def pick(o):
        if o == 2:
            return tuple(ins[d * per_dir + o][0, h] for d, h in _PROBLEMS)
        if o == 5:
            return tuple(ins[d * per_dir + o][0, 0, h:h + 1, :] for d, h in _PROBLEMS)
        return tuple(ins[d * per_dir + o][:, _hs(h)] for d, h in _PROBLEMS)
    return [pick(o) for o in range(6)]


def _dn_b_fwd(ares, grows):
    t_ = ares[0].shape[0]
    nch = t_ // CHUNK
    sp = _dir_specs(nch)

    def body(*refs):
        ins, outs, s_ref = refs[:12], refs[12:16], refs[16]

        @pl.when(pl.program_id(0) == 0)
        def _():
            s_ref[...] = jnp.zeros_like(s_ref)

        ss = tuple(s_ref[p] for p in range(len(_PROBLEMS)))
        os_, s2s = _dn_step(*_step_inputs(ins, 6), ss)
        for p, (d, h) in enumerate(_PROBLEMS):
            outs[2 + d][0, h] = ss[p]
            outs[d][:, _hs(h)] = os_[p]
            s_ref[p] = s2s[p]

    in_specs, args = [], []
    for d in range(2):
        in_specs += [sp[d]["big"], sp[d]["big"], sp[d]["qk"], sp[d]["big"], sp[d]["big"], sp[d]["row"]]
        args += list(ares[d * _A_OUT:(d + 1) * _A_OUT]) + [grows]
    stsh = jax.ShapeDtypeStruct((nch, DN_HEADS, DN_DIM, DN_DIM), F32)
    osh = jax.ShapeDtypeStruct((t_, D), F32)
    return pl.pallas_call(
        body, name="dn_b_fwd", grid=(nch,), in_specs=in_specs,
        out_specs=[sp[0]["big"], sp[1]["big"], sp[0]["st"], sp[1]["st"]], out_shape=[osh, osh, stsh, stsh],
        scratch_shapes=[pltpu.VMEM((2 * DN_HEADS, DN_DIM, DN_DIM), F32)],
        compiler_params=_params(1))(*args)


def _dn_b_bwd(ares, grows, st_f, st_b, do):
    t_ = ares[0].shape[0]
    nch = t_ // CHUNK
    sp = _dir_specs(nch)
    rsp = [sp[1], sp[0]]

    def body(*refs):
        ins, outs, ds_ref = refs[:16], refs[16:28], refs[28]

        @pl.when(pl.program_id(0) == 0)
        def _():
            ds_ref[...] = jnp.zeros_like(ds_ref)

        ss = tuple(ins[d * 8 + 6][0, h] for d, h in _PROBLEMS)
        _, vjp = jax.vjp(_dn_step, *_step_inputs(ins, 8), ss)
        dos = tuple(ins[d * 8 + 7][:, _hs(h)] for d, h in _PROBLEMS)
        grads = vjp((dos, tuple(ds_ref[p] for p in range(len(_PROBLEMS)))))
        for p, (d, h) in enumerate(_PROBLEMS):
            du_ref, dw_ref, dqk_ref, dqd_ref, dkd_ref, dg_ref = outs[d * 6:(d + 1) * 6]
            du_ref[:, _hs(h)], dw_ref[:, _hs(h)], dqk_ref[0, h] = grads[0][p], grads[1][p], grads[2][p]
            dqd_ref[:, _hs(h)], dkd_ref[:, _hs(h)] = grads[3][p], grads[4][p]
            dg_ref[0, 0, h:h + 1, :] = grads[5][p]
            ds_ref[p] = grads[6][p]

    in_specs, args, out_specs, out_shape = [], [], [], []
    big_sh = jax.ShapeDtypeStruct((t_, D), F32)
    qk_sh = jax.ShapeDtypeStruct((nch, DN_HEADS, CHUNK, CHUNK), F32)
    row_sh = jax.ShapeDtypeStruct((1, nch, DN_HEADS, CHUNK), F32)
    for d in range(2):
        s = rsp[d]
        row0 = pl.BlockSpec((1, 1, DN_HEADS, CHUNK), lambda m, d=d: (0, (nch - 1 - m) if d == 0 else m, 0, 0))
        rowd = pl.BlockSpec((1, 1, DN_HEADS, CHUNK), lambda m, d=d: (d, (nch - 1 - m) if d == 0 else m, 0, 0))
        in_specs += [s["big"], s["big"], s["qk"], s["big"], s["big"], rowd, s["st"], s["big"]]
        args += list(ares[d * _A_OUT:(d + 1) * _A_OUT]) + [grows, (st_f, st_b)[d], do]
        out_specs += [s["big"], s["big"], s["qk"], s["big"], s["big"], row0]
        out_shape += [big_sh, big_sh, qk_sh, big_sh, big_sh, row_sh]
    res = pl.pallas_call(
        body, name="dn_b_bwd", grid=(nch,), in_specs=in_specs, out_specs=out_specs, out_shape=out_shape,
        scratch_shapes=[pltpu.VMEM((2 * DN_HEADS, DN_DIM, DN_DIM), F32)],
        compiler_params=_params(1))(*args)
    dares = list(res[0:5]) + list(res[6:11])
    return dares, jnp.concatenate([res[5], res[11]], axis=0)


def _rope(x, c, s):
    half = SW_DIM // 2
    return x * c + jnp.concatenate([-x[:, half:], x[:, :half]], axis=-1) * s


def _attn_f(blk, t_, cq, sq, ck, sk, q, kp, ko, kn, vp, vo, vn, sinks):
    kall = jnp.concatenate([kp, ko, kn], axis=0)
    vall = jnp.concatenate([vp, vo, vn], axis=0)
    nq, nk = SW_GRP * SW_BLOCK, 3 * SW_BLOCK
    qpos = lax.broadcasted_iota(jnp.int32, (nq, nk), 0) % SW_BLOCK
    krel = lax.broadcasted_iota(jnp.int32, (nq, nk), 1) - SW_BLOCK
    kglob = krel + blk * SW_BLOCK
    valid = (jnp.abs(qpos - krel) <= SW_BLOCK) & (kglob >= 0) & (kglob < t_)
    kvs = range(SW_KV)
    heads = [[kvh * SW_GRP + g for g in range(SW_GRP)] for kvh in kvs]
    khs = [_rope(kall[:, kvh * SW_DIM:(kvh + 1) * SW_DIM], ck, sk) for kvh in kvs]
    vhs = [vall[:, kvh * SW_DIM:(kvh + 1) * SW_DIM] for kvh in kvs]
    qgs = [jnp.concatenate([_rope(q[:, h * SW_DIM:(h + 1) * SW_DIM], cq, sq) for h in hs], axis=0) for hs in heads]
    ss = [jnp.where(valid, _nt(qg, kh) * (SW_DIM ** -0.5), -1e30) for qg, kh in zip(qgs, khs)]
    snks = [jnp.concatenate([jnp.broadcast_to(sinks[:, h:h + 1], (SW_BLOCK, 1)) for h in hs], axis=0) for hs in heads]
    ms = [lax.stop_gradient(jnp.maximum(jnp.max(s, axis=-1, keepdims=True), snk)) for s, snk in zip(ss, snks)]
    es = [jnp.exp(s - m) for s, m in zip(ss, ms)]
    ps = [e / (jnp.sum(e, axis=-1, keepdims=True) + jnp.exp(snk - m)) for e, snk, m in zip(es, snks, ms)]
    ogs = [_nn(p, vh) for p, vh in zip(ps, vhs)]
    return jnp.concatenate([og[g * SW_BLOCK:(g + 1) * SW_BLOCK] for og in ogs for g in range(SW_GRP)], axis=-1)


def _attn_specs(nb):
    prv = lambda i: jnp.maximum(i - 1, 0)
    nxt = lambda i: jnp.minimum(i + 1, nb - 1)
    rows = [lambda i: i, prv, lambda i: i, nxt]
    tab = [pl.BlockSpec((SW_BLOCK, SW_DIM), lambda i, r=r: (r(i), 0)) for r in rows]
    qs = pl.BlockSpec((SW_BLOCK, SW_HEADS * SW_DIM), lambda i: (i, O_QS // (SW_HEADS * SW_DIM)))
    kw = SW_KV * SW_DIM
    ks = [pl.BlockSpec((SW_BLOCK, kw), lambda i, r=r: (r(i), O_KS // kw)) for r in rows[1:]]
    vs = [pl.BlockSpec((SW_BLOCK, kw), lambda i, r=r: (r(i), O_VS // kw)) for r in rows[1:]]
    return tab, qs, ks, vs


def _attn_tables(refs):
    cq, cp, co, cn, sq, sp_, so, sn = [r[...] for r in refs]
    return cq, sq, jnp.concatenate([cp, co, cn], axis=0), jnp.concatenate([sp_, so, sn], axis=0)


def _attn_fwd(pm, cos, sin, sinks):
    t_ = pm.shape[0]
    nb = t_ // SW_BLOCK
    tab, qs, ks, vs = _attn_specs(nb)

    def body(*refs):
        tabs = _attn_tables(refs[:8])
        vals = [r[...] for r in refs[8:16]]
        refs[16][...] = _attn_f(pl.program_id(0), t_, *tabs, *vals)

    return pl.pallas_call(
        body, name="attn_fwd", grid=(nb,), in_specs=tab + tab + [qs] + ks + vs + [_full(sinks)],
        out_specs=pl.BlockSpec((SW_BLOCK, D), lambda i: (i, 0)), out_shape=jax.ShapeDtypeStruct((t_, D), F32),
        compiler_params=_params(1))(*([cos] * 4), *([sin] * 4), pm, pm, pm, pm, pm, pm, pm, sinks)


def _attn_bwd(pm, cos, sin, sinks, do):
    t_ = pm.shape[0]
    nb = t_ // SW_BLOCK
    tab, qs, ks, vs = _attn_specs(nb)
    kw = SW_KV * SW_DIM

    def body(*refs):
        tabs = _attn_tables(refs[:8])
        vals = [r[...] for r in refs[8:16]]
        do_ref, outs = refs[16], refs[17:]
        _, vjp = jax.vjp(functools.partial(_attn_f, pl.program_id(0), t_, *tabs), *vals)
        grads = vjp(do_ref[...])
        for o_ref, g in zip(outs[:7], grads[:7]):
            o_ref[...] = g

        @pl.when(pl.program_id(0) == 0)
        def _():
            outs[7][...] = jnp.zeros_like(outs[7])
        outs[7][...] += grads[7]

    own = lambda w: pl.BlockSpec((SW_BLOCK, w), lambda i: (i, 0))
    return pl.pallas_call(
        body, name="attn_bwd", grid=(nb,),
        in_specs=tab + tab + [qs] + ks + vs + [_full(sinks), own(D)],
        out_specs=[own(D)] + [own(kw)] * 6 + [_full(sinks)],
        out_shape=[jax.ShapeDtypeStruct((t_, D), F32)] + [jax.ShapeDtypeStruct((t_, kw), F32)] * 6
        + [jax.ShapeDtypeStruct(sinks.shape, F32)],
        compiler_params=_params(1))(*([cos] * 4), *([sin] * 4), pm, pm, pm, pm, pm, pm, pm, sinks, do)


def _band_sum(parts):
    dp, do, dn = parts
    t_, kw = do.shape
    nb = t_ // SW_BLOCK

    def body(p_ref, o_ref, n_ref, out_ref):
        j = pl.program_id(0)
        acc = o_ref[...]
        acc = acc + jnp.where(j + 1 < nb, p_ref[...], 0.0)
        out_ref[...] = acc + jnp.where(j > 0, n_ref[...], 0.0)

    return pl.pallas_call(
        body, name="band_sum", grid=(nb,),
        in_specs=[pl.BlockSpec((SW_BLOCK, kw), lambda j: (jnp.minimum(j + 1, nb - 1), 0)),
                  pl.BlockSpec((SW_BLOCK, kw), lambda j: (j, 0)),
                  pl.BlockSpec((SW_BLOCK, kw), lambda j: (jnp.maximum(j - 1, 0), 0))],
        out_specs=pl.BlockSpec((SW_BLOCK, kw), lambda j: (j, 0)), out_shape=jax.ShapeDtypeStruct((t_, kw), F32),
        compiler_params=_params(1))(dp, do, dn)


def _loss_head(y, target, tb=256):
    t_ = y.shape[0]

    def body(y_ref, t_ref, dy_ref, acc_ref):
        err = y_ref[...] - t_ref[...]
        dy_ref[...] = err * (1.0 / D)
        sq = (err * err).reshape(tb // 8, 8, D).sum(axis=0)
        part = sq[:, 0:128]
        for c in range(1, D // 128):
            part = part + sq[:, c * 128:(c + 1) * 128]

        @pl.when(pl.program_id(0) == 0)
        def _():
            acc_ref[...] = jnp.zeros_like(acc_ref)
        acc_ref[...] += part

    row = pl.BlockSpec((tb, D), lambda i: (i, 0))
    return pl.pallas_call(
        body, name="loss_head", grid=(t_ // tb,), in_specs=[row, row],
        out_specs=[row, pl.BlockSpec((8, 128), lambda i: (0, 0))],
        out_shape=[jax.ShapeDtypeStruct((t_, D), F32), jax.ShapeDtypeStruct((8, 128), F32)],
        compiler_params=_params(1))(y, target)


def _adamw(name, w, g, m, v):
    shape = w.shape
    cols = shape[-1]
    rows = w.size // cols
    w2, g2, m2, v2 = [a.reshape(rows, cols) for a in (w, g, m, v)]
    tb = rows
    while tb * cols * 4 > (1 << 20) and tb % 16 == 0:
        tb //= 2
    bc1 = 1.0 - ADAM_B1 ** ADAM_STEP
    bc2 = 1.0 - ADAM_B2 ** ADAM_STEP

    def body(w_ref, g_ref, m_ref, v_ref, d_ref, nm_ref, nv_ref):
        gv = g_ref[...]
        nm = ADAM_B1 * m_ref[...] + (1.0 - ADAM_B1) * gv
        nv = ADAM_B2 * v_ref[...] + (1.0 - ADAM_B2) * (gv * gv)
        d_ref[...] = -ADAM_LR * ((nm / bc1) / (jnp.sqrt(nv / bc2) + ADAM_EPS) + ADAM_WD * w_ref[...])
        nm_ref[...] = nm
        nv_ref[...] = nv

    spec = pl.BlockSpec((tb, cols), lambda i: (i, 0))
    sh = jax.ShapeDtypeStruct((rows, cols), F32)
    outs = pl.pallas_call(body, name=name, grid=(rows // tb,), in_specs=[spec] * 4, out_specs=[spec] * 3,
                          out_shape=[sh] * 3, compiler_params=_params(1))(w2, g2, m2, v2)
    return [o.reshape(shape) for o in outs]


def _to_rows(bg, col0):
    t_ = bg.shape[0]
    a = bg[:, col0:col0 + 2 * DN_HEADS].reshape(t_ // CHUNK, CHUNK, 2, DN_HEADS)
    return jnp.transpose(a, (2, 0, 3, 1))


def _from_rows(db, dg):
    nch = db.shape[1]
    back = lambda a: jnp.transpose(a, (1, 3, 0, 2)).reshape(nch * CHUNK, 2 * DN_HEADS)
    return jnp.pad(jnp.concatenate([back(db), back(dg)], axis=1), ((0, 0), (0, 128 - 4 * DN_HEADS)))


def _layer_fwd(x, w, cos, sin):
    tb = min(256, x.shape[0])
    nb = min(1024, x.shape[0])
    pm = _mm(x, w["in_main"], name="mm_in")
    pbg = _mm(x, w["in_bg"], name="mm_in_bg")
    qkv = _prep_fwd(pm, w["conv"])
    bg, = _stage_fwd(_bg_f, "bg_fwd", [(pbg, 128, 0, False)], [w["arow"], w["dtrow"]], [128], nb)
    brows, grows = _to_rows(bg, 0), _to_rows(bg, 2 * DN_HEADS)
    ares = _dn_a_fwd(qkv, grows, brows)
    o_f, o_b, st_f, st_b = _dn_b_fwd(ares, grows)
    odn, = _stage_fwd(_gnorm_f, "gnorm_fwd", [(o_f, 128, 0, True), (o_b, 128, 0, True), (pm, 128, O_Z // 128, True)],
                      [w["gnw"]], [128], nb, ncol=DN_HEADS)
    osw = _attn_fwd(pm, cos, sin, w["sinks"])
    ya = _mm(odn, w["a"], name="mm_a")
    yb = _mm(osw, w["b"], name="mm_b")
    gates = [(pm, 512, O_GA // 512, True), (pm, 512, O_GB // 512, True)]
    merged, = _stage_fwd(_merge_f, "merge_fwd", [(ya, 512, 0, True), (yb, 512, 0, True)] + gates, [], [512], tb, ncol=2)
    mix = _mm(merged, w["o"], name="mm_o")
    x1, = _stage_fwd(_ln_f, "ln1_fwd", [(x, D, 0, False), (mix, D, 0, False)], [w["ln1g"], w["ln1b"]], [D], tb)
    gu = _mm(x1, w["gu"], name="mm_gu")
    hid, = _stage_fwd(_swiglu_f, "swiglu_fwd", [(gu, FFN // 2, 0, True), (gu, FFN // 2, 2, True)], [], [FFN // 2], tb,
                      ncol=2)
    ffn = _mm(hid, w["d"], name="mm_d", tk=FFN // 2)
    x2, = _stage_fwd(_ln_f, "ln2_fwd", [(x1, D, 0, False), (ffn, D, 0, False)], [w["ln2g"], w["ln2b"]], [D], tb)
    res = dict(x=x, pm=pm, pbg=pbg, qkv=qkv, grows=grows, brows=brows, ares=ares, o_f=o_f, o_b=o_b, st_f=st_f,
               st_b=st_b, odn=odn, osw=osw, ya=ya, yb=yb, merged=merged, mix=mix, x1=x1, gu=gu, hid=hid, ffn=ffn)
    return x2, res


def _layer_bwd(dx2, r, w, cos, sin):
    tb = min(256, dx2.shape[0])
    sb = min(128, dx2.shape[0])
    g = {}
    dx1a, dffn, g["ln2g"], g["ln2b"] = _stage_bwd(
        _ln_f, "ln2_bwd", [(r["x1"], D, 0, False), (r["ffn"], D, 0, False)], [w["ln2g"], w["ln2b"]], [dx2], tb)
    dhid = _mm(dffn, w["d"], tb=True, name="mm_d_dx", tn=FFN // 2)
    g["d"] = _mm(r["hid"], dffn, ta=True, name="mm_d_dw", tm=FFN // 2)
    dgu, = _stage_bwd(_swiglu_f, "swiglu_bwd", [(r["gu"], FFN, 0, False), (r["gu"], FFN, 1, False)], [], [dhid], sb,
                      cat=[[0, 1]])
    dx1 = _mm(dgu, w["gu"], tb=True, add=dx1a, name="mm_gu_dx")
    g["gu"] = _mm(r["x1"], dgu, ta=True, name="mm_gu_dw")
    dxa, dmix, g["ln1g"], g["ln1b"] = _stage_bwd(
        _ln_f, "ln1_bwd", [(r["x"], D, 0, False), (r["mix"], D, 0, False)], [w["ln1g"], w["ln1b"]], [dx1], tb)
    dmerged = _mm(dmix, w["o"], tb=True, name="mm_o_dx")
    g["o"] = _mm(r["merged"], dmix, ta=True, name="mm_o_dw")
    pm = r["pm"]
    dya, dyb, dgates = _merge_bwd(r, pm, dmerged, tb)
    dodn = _mm(dya, w["a"], tb=True, name="mm_a_dx")
    g["a"] = _mm(r["odn"], dya, ta=True, name="mm_a_dw")
    dosw = _mm(dyb, w["b"], tb=True, name="mm_b_dx")
    g["b"] = _mm(r["osw"], dyb, ta=True, name="mm_b_dw")
    ab = _attn_bwd(pm, cos, sin, w["sinks"], dosw)
    dq_sw, g["sinks"] = ab[0], ab[7]
    dk_sw = _band_sum(ab[1:4])
    dv_sw = _band_sum(ab[4:7])
    dof, _, dz, g["gnw"] = _stage_bwd(
        _gnorm_f, "gnorm_bwd", [(r["o_f"], 128, 0, True), (r["o_b"], 128, 0, True), (pm, 128, O_Z // 128, True)],
        [w["gnw"]], [dodn], min(1024, dx2.shape[0]), ncol=DN_HEADS)
    dares, dg_b = _dn_b_bwd(r["ares"], r["grows"], r["st_f"], r["st_b"], dof)
    dqkv, dgrows, dbrows = _dn_a_bwd(r["qkv"], r["grows"], r["brows"], dares, dg_b)
    dbg = _from_rows(dbrows, dgrows)
    dpbg, g["arow"], g["dtrow"] = _stage_bwd(_bg_f, "bg_bwd", [(r["pbg"], 128, 0, False)], [w["arow"], w["dtrow"]],
                                             [dbg], min(1024, dx2.shape[0]))
    dpre, g["conv"] = _prep_bwd(pm, w["conv"], dqkv)
    dpm = jnp.concatenate([dpre, dz, dq_sw, dk_sw, dv_sw, dgates], axis=1)
    dx = _mm(dpm, w["in_main"], tb=True, add=dxa, name="mm_in_dx")
    dx = _mm(dpbg, w["in_bg"], tb=True, add=dx, name="mm_in_bg_dx")
    g["in_main"] = _mm(r["x"], dpm, ta=True, name="mm_in_dw")
    g["in_bg"] = _mm(r["x"], dpbg, ta=True, name="mm_in_bg_dw")
    return dx, g


def _merge_bwd(r, pm, dmerged, tb):
    gates = [(pm, 512, O_GA // 512, True), (pm, 512, O_GB // 512, True)]
    dya, dyb, dga, dgb = _stage_bwd(_merge_f, "merge_bwd", [(r["ya"], 512, 0, True), (r["yb"], 512, 0, True)] + gates,
                                    [], [dmerged], tb, ncol=2)
    return dya, dyb, jnp.concatenate([dga, dgb], axis=1)


def _place():
    return lax.axis_index("x"), lax.axis_index("y"), lax.axis_index("c")


def _other_chips(x, y):
    return [(1 - x, y), (x, 1 - y), (1 - x, 1 - y)]


def _gather_weights(locs, kinds):
    n = len(locs)

    def dst(ref, kind, q, width):
        return ref.at[:, q] if kind == "row" else ref.at[:, :, pl.ds(pl.multiple_of(q * width, 128), width)]

    def body(*refs):
        loc_refs, out_refs = refs[:n], refs[n:2 * n]
        send_sems, recv_sems, local_sems = refs[2 * n:]
        x, y, c = _place()
        myq = 2 * x + y
        chips = _other_chips(x, y)
        started = []
        for t in range(n):
            width = locs[t].shape[-1]
            mine = pltpu.make_async_copy(loc_refs[t], dst(out_refs[t], kinds[t], myq, width), local_sems.at[t])
            mine.start()
            started.append(mine)
            for j, (cx, cy) in enumerate(chips):
                cp = pltpu.make_async_remote_copy(
                    src_ref=loc_refs[t], dst_ref=dst(out_refs[t], kinds[t], myq, width),
                    send_sem=send_sems.at[3 * t + j], recv_sem=recv_sems.at[3 * t + j],
                    device_id=(cx, cy, c), device_id_type=MESH)
                cp.start()
        for t in range(n):
            width = locs[t].shape[-1]
            for j, (cx, cy) in enumerate(chips):
                cp = pltpu.make_async_remote_copy(
                    src_ref=loc_refs[t], dst_ref=dst(out_refs[t], kinds[t], 2 * cx + cy, width),
                    send_sem=send_sems.at[3 * t + j], recv_sem=recv_sems.at[3 * t + j],
                    device_id=(cx, cy, c), device_id_type=MESH)
                cp.wait_recv()
                cp.wait_send()
        for mine in started:
            mine.wait()

    out_shape = []
    for a, kind in zip(locs, kinds):
        l_, r_, c_ = a.shape
        out_shape.append(jax.ShapeDtypeStruct((l_, N_CHIPS, r_, c_) if kind == "row" else (l_, r_, N_CHIPS * c_), a.dtype))
    return pl.pallas_call(
        body, name="gather_weights", in_specs=[ANY] * n, out_specs=[ANY] * n, out_shape=out_shape,
        scratch_shapes=[pltpu.SemaphoreType.DMA((3 * n,)), pltpu.SemaphoreType.DMA((3 * n,)),
                        pltpu.SemaphoreType.DMA((n,))],
    )(*locs)


RS_CHUNKS = 2


def _piece(ref, kind, q, hf, pr, pc):
    if kind == "row":
        return ref.at[pl.ds((2 * q + hf) * pr, pr), :]
    return ref.at[pl.ds(hf * pr, pr), pl.ds(pl.multiple_of(q * pc, 128), pc)]


def _rs_to_sibling(ts, meta):
    n = len(ts)

    def body(*refs):
        t_refs, r1_refs, send_sems, recv_sems = refs[:n], refs[n:2 * n], refs[2 * n], refs[2 * n + 1]
        x, y, c = _place()
        sib = (x, y, 1 - c)
        for t, (kind, pr, pc) in enumerate(meta):
            for q in range(N_CHIPS):
                pltpu.make_async_remote_copy(
                    src_ref=_piece(t_refs[t], kind, q, 1 - c, pr, pc), dst_ref=r1_refs[t].at[q],
                    send_sem=send_sems.at[t], recv_sem=recv_sems.at[t], device_id=sib, device_id_type=MESH).start()
        for t in range(n):
            allq = pltpu.make_async_remote_copy(
                src_ref=r1_refs[t], dst_ref=r1_refs[t], send_sem=send_sems.at[t], recv_sem=recv_sems.at[t],
                device_id=sib, device_id_type=MESH)
            allq.wait_send()
            allq.wait_recv()

    return pl.pallas_call(
        body, name="rs_to_sibling", in_specs=[ANY] * n, out_specs=[ANY] * n,
        out_shape=[jax.ShapeDtypeStruct((N_CHIPS, pr, pc), F32) for (_, pr, pc) in meta],
        scratch_shapes=[pltpu.SemaphoreType.DMA((n,)), pltpu.SemaphoreType.DMA((n,))],
    )(*ts)


def _rs_add_sibling(ts, r1s, meta, c):
    n = len(ts)
    in_specs, out_specs, out_shape = [], [], []
    for kind, pr, pc in meta:
        rs = pr // RS_CHUNKS
        if kind == "row":
            in_specs.append(pl.BlockSpec((rs, pc), lambda q, r, c_ref: ((2 * q + c_ref[0]) * RS_CHUNKS + r, 0)))
        else:
            in_specs.append(pl.BlockSpec((rs, pc), lambda q, r, c_ref: (c_ref[0] * RS_CHUNKS + r, q)))
    for kind, pr, pc in meta:
        sp = pl.BlockSpec((None, pr // RS_CHUNKS, pc), lambda q, r, c_ref: (q, r, 0))
        in_specs.append(sp)
        out_specs.append(sp)
        out_shape.append(jax.ShapeDtypeStruct((N_CHIPS, pr, pc), BF16))

    def body(c_ref, *refs):
        for t in range(n):
            refs[2 * n + t][...] = (refs[t][...] + refs[n + t][...]).astype(BF16)

    return pl.pallas_call(
        body, name="rs_add_sibling", out_shape=out_shape,
        grid_spec=pltpu.PrefetchScalarGridSpec(num_scalar_prefetch=1, grid=(N_CHIPS, RS_CHUNKS), in_specs=in_specs,
                                               out_specs=out_specs),
        compiler_params=_params(2))(c.reshape(1).astype(jnp.int32), *ts, *r1s)


def _rs_to_chips(ps, meta):
    n = len(ps)

    def body(*refs):
        p_refs, r2_refs, send_sems, recv_sems = refs[:n], refs[n:2 * n], refs[2 * n], refs[2 * n + 1]
        x, y, c = _place()
        for t in range(n):
            for j, (cx, cy) in enumerate(_other_chips(x, y)):
                pltpu.make_async_remote_copy(
                    src_ref=p_refs[t].at[2 * cx + cy], dst_ref=r2_refs[t].at[j],
                    send_sem=send_sems.at[t], recv_sem=recv_sems.at[t], device_id=(cx, cy, c),
                    device_id_type=MESH).start()
        for t in range(n):
            allj = pltpu.make_async_remote_copy(
                src_ref=r2_refs[t], dst_ref=r2_refs[t], send_sem=send_sems.at[t], recv_sem=recv_sems.at[t],
                device_id=(x, y, c), device_id_type=MESH)
            allj.wait_send()
            allj.wait_recv()

    return pl.pallas_call(
        body, name="rs_to_chips", in_specs=[ANY] * n, out_specs=[ANY] * n,
        out_shape=[jax.ShapeDtypeStruct((N_CHIPS - 1, pr, pc), p.dtype) for p, (_, pr, pc) in zip(ps, meta)],
        scratch_shapes=[pltpu.SemaphoreType.DMA((n,)), pltpu.SemaphoreType.DMA((n,))],
    )(*ps)


def _rs_add_chips(ps, r2s, meta, myq, c):
    n = len(ps)
    in_specs, out_specs, out_shape = [], [], []
    for _, pr, pc in meta:
        in_specs.append(pl.BlockSpec((None, pr // RS_CHUNKS, pc), lambda r, q_ref, c_ref: (q_ref[0], r, 0)))
    for _, pr, pc in meta:
        in_specs.append(pl.BlockSpec((N_CHIPS - 1, pr // RS_CHUNKS, pc), lambda r, q_ref, c_ref: (0, r, 0)))
        out_specs.append(pl.BlockSpec((None, pr // RS_CHUNKS, pc), lambda r, q_ref, c_ref: (c_ref[0], r, 0)))
        out_shape.append(jax.ShapeDtypeStruct((2, pr, pc), F32))

    def body(q_ref, c_ref, *refs):
        for t in range(n):
            r2 = refs[n + t]
            own = refs[t][...].astype(F32)
            refs[2 * n + t][...] = ((own + r2[0].astype(F32)) + r2[1].astype(F32)) + r2[2].astype(F32)

    return pl.pallas_call(
        body, name="rs_add_chips", out_shape=out_shape,
        grid_spec=pltpu.PrefetchScalarGridSpec(num_scalar_prefetch=2, grid=(RS_CHUNKS,), in_specs=in_specs,
                                               out_specs=out_specs),
        compiler_params=_params(1))(myq.reshape(1).astype(jnp.int32), c.reshape(1).astype(jnp.int32), *ps, *r2s)


def _rs_share_halves(gs):
    n = len(gs)

    def body(*refs):
        g_refs, send_sems, recv_sems = refs[n:2 * n], refs[2 * n], refs[2 * n + 1]
        x, y, c = _place()
        sib = (x, y, 1 - c)
        for t in range(n):
            pltpu.make_async_remote_copy(
                src_ref=g_refs[t].at[c], dst_ref=g_refs[t].at[c], send_sem=send_sems.at[t], recv_sem=recv_sems.at[t],
                device_id=sib, device_id_type=MESH).start()
        for t in range(n):
            cp = pltpu.make_async_remote_copy(
                src_ref=g_refs[t].at[c], dst_ref=g_refs[t].at[1 - c], send_sem=send_sems.at[t],
                recv_sem=recv_sems.at[t], device_id=sib, device_id_type=MESH)
            cp.wait_send()
            cp.wait_recv()

    return pl.pallas_call(
        body, name="rs_share_halves", in_specs=[ANY] * n, out_specs=[ANY] * n,
        out_shape=[jax.ShapeDtypeStruct(g.shape, g.dtype) for g in gs], input_output_aliases={t: t for t in range(n)},
        scratch_shapes=[pltpu.SemaphoreType.DMA((n,)), pltpu.SemaphoreType.DMA((n,))],
    )(*gs)


def _reduce_scatter(ts, meta, c, myq):
    r1s = _rs_to_sibling(ts, meta)
    ps = _rs_add_sibling(ts, r1s, meta, c)
    r2s = _rs_to_chips(ps, meta)
    return _rs_share_halves(_rs_add_chips(ps, r2s, meta, myq, c))


def _allreduce_small(buf):
    rows = buf.shape[0]
    ndev = 8

    def body(b_ref, o_ref, slots, send_sems, recv_sems):
        x, y, c = _place()
        me = 4 * x + 2 * y + c
        slots[me] = b_ref[...]
        for k in range(1, ndev):
            kx, ky, kc = (k >> 2) & 1, (k >> 1) & 1, k & 1
            peer = (x ^ kx, y ^ ky, c ^ kc)
            pltpu.make_async_remote_copy(
                src_ref=b_ref, dst_ref=slots.at[me], send_sem=send_sems.at[k - 1], recv_sem=recv_sems.at[k - 1],
                device_id=peer, device_id_type=MESH).start()
        for k in range(1, ndev):
            kx, ky, kc = (k >> 2) & 1, (k >> 1) & 1, k & 1
            cp = pltpu.make_async_remote_copy(
                src_ref=b_ref, dst_ref=slots.at[me ^ k], send_sem=send_sems.at[k - 1], recv_sem=recv_sems.at[k - 1],
                device_id=(x ^ kx, y ^ ky, c ^ kc), device_id_type=MESH)
            cp.wait_send()
            cp.wait_recv()
        acc = slots[0]
        for s in range(1, ndev):
            acc = acc + slots[s]
        o_ref[...] = acc

    vm = pl.BlockSpec(memory_space=pltpu.VMEM)
    return pl.pallas_call(
        body, name="allreduce_small", in_specs=[vm], out_specs=vm, out_shape=jax.ShapeDtypeStruct((rows, 128), F32),
        scratch_shapes=[pltpu.VMEM((ndev, rows, 128), F32), pltpu.SemaphoreType.DMA((ndev - 1,)),
                        pltpu.SemaphoreType.DMA((ndev - 1,))],
        compiler_params=pltpu.CompilerParams(vmem_limit_bytes=VMEM_LIMIT))(buf)


RS_META = [("col", D // 2, IN_PAD), ("row", D // 8, D), ("row", D // 8, D), ("row", D // 8, D),
           ("col", D // 2, 2 * FFN // N_CHIPS), ("row", FFN // 8, D)]
SMALL_ROWS = 156


def _rope_tables(t_):
    half = SW_DIM // 2
    inv_freq = ROPE_THETA ** (-jnp.arange(half, dtype=F32) / half)
    ang = jnp.arange(t_, dtype=F32)[:, None] * inv_freq[None, :]
    return jnp.concatenate([jnp.cos(ang)] * 2, axis=1), jnp.concatenate([jnp.sin(ang)] * 2, axis=1)


def _lane_row(v16):
    return jnp.pad(v16.reshape(1, 2 * DN_HEADS), ((0, 0), (2 * DN_HEADS, 128 - 4 * DN_HEADS)))


def _pack_small(g):
    pad16 = jnp.pad(g["sinks"], ((0, 0), (0, 128 - SW_HEADS)))
    return jnp.concatenate([g["conv"].reshape(-1, 128), g["ln1g"].reshape(-1, 128), g["ln1b"].reshape(-1, 128),
                            g["ln2g"].reshape(-1, 128), g["ln2b"].reshape(-1, 128), g["gnw"], g["arow"], g["dtrow"],
                            pad16], axis=0)


def _unpack_small(buf):
    nconv = DN_CONV * 3 * D // 128
    o = nconv
    out = dict(conv=buf[:o].reshape(DN_CONV, 3 * D))
    for name in ("ln1g", "ln1b", "ln2g", "ln2b"):
        out[name] = buf[o:o + 8].reshape(D)
        o += 8
    out["gnw"] = buf[o]
    out["a_log"] = buf[o + 1, 2 * DN_HEADS:4 * DN_HEADS].reshape(2, DN_HEADS)
    out["dt_bias"] = buf[o + 2, 2 * DN_HEADS:4 * DN_HEADS].reshape(2, DN_HEADS)
    out["sinks"] = buf[o + 3, :SW_HEADS]
    return out


def kernel(x, w_in, conv_w, a_log, dt_bias, dn_norm_w, sinks, w_branch_a, w_branch_b, w_out, ln1_g, ln1_b, w_gate_up, w_down, ln2_g, ln2_b, loss_target, m_w_in, m_conv_w, m_a_log, m_dt_bias, m_dn_norm_w, m_sinks, m_w_branch_a, m_w_branch_b, m_w_out, m_ln1_g, m_ln1_b, m_w_gate_up, m_w_down, m_ln2_g, m_ln2_b, v_w_in, v_conv_w, v_a_log, v_dt_bias, v_dn_norm_w, v_sinks, v_w_branch_a, v_w_branch_b, v_w_out, v_ln1_g, v_ln1_b, v_w_gate_up, v_w_down, v_ln2_g, v_ln2_b):
    xi, yi, ci = _place()
    myq = 2 * xi + yi
    t_ = x.shape[1]
    cos, sin = _rope_tables(t_)

    loc_in = jnp.pad(w_in.astype(BF16), ((0, 0), (0, 0), (0, IN_PAD - IN_SHARD)))
    full_in, full_a, full_b, full_o, full_gu, full_d, full_conv = _gather_weights(
        [loc_in, w_branch_a.astype(BF16), w_branch_b.astype(BF16), w_out.astype(BF16), w_gate_up.astype(BF16),
         w_down.astype(BF16), conv_w], ["col", "row", "row", "row", "col", "row", "col"])
    layers = []
    for l in range(DEPTH):
        orig = jnp.concatenate([full_in[l, :, q * IN_PAD:q * IN_PAD + IN_SHARD] for q in range(N_CHIPS)], axis=1)
        layers.append(dict(
            in_main=jnp.concatenate([orig[:, :R_BG], orig[:, R_BG + 4 * DN_HEADS:]], axis=1),
            in_bg=jnp.pad(orig[:, R_BG:R_BG + 4 * DN_HEADS], ((0, 0), (0, 128 - 4 * DN_HEADS))),
            conv=full_conv[l], arow=_lane_row(a_log[l]), dtrow=_lane_row(dt_bias[l]), gnw=dn_norm_w[l][None],
            sinks=sinks[l][None], a=full_a[l].reshape(D, D), b=full_b[l].reshape(D, D), o=full_o[l].reshape(D, D),
            ln1g=ln1_g[l][None], ln1b=ln1_b[l][None], gu=full_gu[l], d=full_d[l].reshape(FFN, D),
            ln2g=ln2_g[l][None], ln2b=ln2_b[l][None]))

    h = x[0]
    residuals = []
    for l in range(DEPTH):
        h, res = _layer_fwd(h, layers[l], cos, sin)
        residuals.append(res)
    dh, sq = _loss_head(h, loss_target[0])
    loss = lax.psum((0.5 / D) * jnp.sum(sq), ("x", "y", "c"))

    big = [None] * DEPTH
    small = [None] * DEPTH
    for l in reversed(range(DEPTH)):
        dh, g = _layer_bwd(dh, residuals[l], layers[l], cos, sin)
        g_orig = jnp.concatenate([g["in_main"][:, :R_BG], g["in_bg"][:, :4 * DN_HEADS], g["in_main"][:, R_BG:]], axis=1)
        g_in = jnp.concatenate(
            [jnp.pad(g_orig[:, q * IN_SHARD:(q + 1) * IN_SHARD], ((0, 0), (0, IN_PAD - IN_SHARD)))
             for q in range(N_CHIPS)], axis=1)
        big[l] = _reduce_scatter([g_in, g["a"], g["b"], g["o"], g["gu"], g["d"]], RS_META, ci, myq)
        small[l] = _pack_small(g)
    tot = _allreduce_small(jnp.concatenate(small, axis=0))
    sm = [_unpack_small(tot[l * SMALL_ROWS:(l + 1) * SMALL_ROWS]) for l in range(DEPTH)]
    stack = lambda name: jnp.stack([s[name] for s in sm], axis=0)

    grads = dict(
        w_in=jnp.stack([big[l][0].reshape(D, IN_PAD)[:, :IN_SHARD] for l in range(DEPTH)]),
        conv_w=lax.dynamic_slice_in_dim(stack("conv"), myq * (3 * D // N_CHIPS), 3 * D // N_CHIPS, axis=2),
        a_log=stack("a_log"), dt_bias=stack("dt_bias"), dn_norm_w=stack("gnw"), sinks=stack("sinks"),
        w_branch_a=jnp.stack([big[l][1].reshape(D // N_CHIPS, D) for l in range(DEPTH)]),
        w_branch_b=jnp.stack([big[l][2].reshape(D // N_CHIPS, D) for l in range(DEPTH)]),
        w_out=jnp.stack([big[l][3].reshape(D // N_CHIPS, D) for l in range(DEPTH)]),
        ln1_g=stack("ln1g"), ln1_b=stack("ln1b"),
        w_gate_up=jnp.stack([big[l][4].reshape(D, 2 * FFN // N_CHIPS) for l in range(DEPTH)]),
        w_down=jnp.stack([big[l][5].reshape(FFN // N_CHIPS, D) for l in range(DEPTH)]),
        ln2_g=stack("ln2g"), ln2_b=stack("ln2b"))
    weights = dict(w_in=w_in, conv_w=conv_w, a_log=a_log, dt_bias=dt_bias, dn_norm_w=dn_norm_w, sinks=sinks,
                   w_branch_a=w_branch_a, w_branch_b=w_branch_b, w_out=w_out, ln1_g=ln1_g, ln1_b=ln1_b,
                   w_gate_up=w_gate_up, w_down=w_down, ln2_g=ln2_g, ln2_b=ln2_b)
    ms = dict(w_in=m_w_in, conv_w=m_conv_w, a_log=m_a_log, dt_bias=m_dt_bias, dn_norm_w=m_dn_norm_w, sinks=m_sinks,
              w_branch_a=m_w_branch_a, w_branch_b=m_w_branch_b, w_out=m_w_out, ln1_g=m_ln1_g, ln1_b=m_ln1_b,
              w_gate_up=m_w_gate_up, w_down=m_w_down, ln2_g=m_ln2_g, ln2_b=m_ln2_b)
    vs = dict(w_in=v_w_in, conv_w=v_conv_w, a_log=v_a_log, dt_bias=v_dt_bias, dn_norm_w=v_dn_norm_w, sinks=v_sinks,
              w_branch_a=v_w_branch_a, w_branch_b=v_w_branch_b, w_out=v_w_out, ln1_g=v_ln1_g, ln1_b=v_ln1_b,
              w_gate_up=v_w_gate_up, w_down=v_w_down, ln2_g=v_ln2_g, ln2_b=v_ln2_b)
    names = list(weights)
    upd = {n: _adamw("adamw_" + n, weights[n], grads[n], ms[n], vs[n]) for n in names}
    return (loss, dh[None], *[grads[n] for n in names], *[upd[n][0] for n in names], *[upd[n][1] for n in names],
            *[upd[n][2] for n in names])
```

```python
import functools

import jax
import jax.numpy as jnp
from jax import lax
from jax.experimental import pallas as pl
from jax.experimental.pallas import tpu as pltpu

F32 = jnp.float32
BF16 = jnp.bfloat16
_MXU = BF16

D = 1024
DEPTH = 4
DN_HEADS = 8
DN_DIM = 128
DN_CONV = 5
CHUNK = 64
SW_HEADS = 16
SW_KV = 4
SW_DIM = 64
SW_GRP = SW_HEADS // SW_KV
SW_BLOCK = 128
ROPE_THETA = 10000.0
FFN = 2816
ALPHA = (2.0 * DEPTH) ** 0.25
LN_EPS = 1e-5
RMS_EPS = 1e-6
IN_COLS = 7712
O_Z, O_GA, O_GB, O_QS, O_KS, O_VS, N_MAIN = 3072, 4096, 5120, 6144, 7168, 7424, 7680
R_BG, R_SW, R_GATES = 4096, 4128, 5664
N_CHIPS = 4
IN_SHARD = IN_COLS // N_CHIPS
IN_PAD = 2048
ADAM_LR, ADAM_B1, ADAM_B2, ADAM_EPS, ADAM_WD, ADAM_STEP = 0.001, 0.9, 0.999, 1e-08, 0.01, 10
VMEM_LIMIT = 52 * 1024 * 1024
MESH = pl.DeviceIdType.MESH
ANY = pl.BlockSpec(memory_space=pl.ANY)


def _params(n_grid, **kw):
    return pltpu.CompilerParams(dimension_semantics=("arbitrary",) * n_grid, vmem_limit_bytes=VMEM_LIMIT, **kw)


def _full(a):
    nd = a.ndim
    return pl.BlockSpec(a.shape, lambda *_, nd=nd: (0,) * nd)


def _raw_dot(a, b, ca, cb):
    return lax.dot_general(a.astype(_MXU), b.astype(_MXU), (((ca,), (cb,)), ((), ())), preferred_element_type=F32)


@jax.custom_vjp
def _nn(a, b):
    return _raw_dot(a, b, 1, 0)


@jax.custom_vjp
def _nt(a, b):
    return _raw_dot(a, b, 1, 1)


@jax.custom_vjp
def _tn(a, b):
    return _raw_dot(a, b, 0, 0)


_nn.defvjp(lambda a, b: (_nn(a, b), (a, b)), lambda r, g: (_nt(g, r[1]), _tn(r[0], g)))
_nt.defvjp(lambda a, b: (_nt(a, b), (a, b)), lambda r, g: (_nn(g, r[1]), _tn(g, r[0])))
_tn.defvjp(lambda a, b: (_tn(a, b), (a, b)), lambda r, g: (_nt(r[1], g), _nn(r[0], g)))


def _hdot(a, b, ca=1, cb=0):
    ah, bh = a.astype(BF16), b.astype(BF16)
    al, bl = (a - ah.astype(F32)).astype(BF16), (b - bh.astype(F32)).astype(BF16)
    dot = lambda u, v: lax.dot_general(u, v, (((ca,), (cb,)), ((), ())), preferred_element_type=F32)
    return dot(ah, bh) + (dot(ah, bl) + dot(al, bh))


def _inv_impl(mats):
    n = mats[0].shape[0]
    eye = (lax.broadcasted_iota(jnp.int32, (n, n), 0) == lax.broadcasted_iota(jnp.int32, (n, n), 1)).astype(F32)
    ps = [-a for a in mats]
    ts = [eye + p for p in ps]
    for _ in range(max(1, (n - 1).bit_length()) - 1):
        ps = [_hdot(p, p) for p in ps]
        ts = [t + _hdot(t, p) for t, p in zip(ts, ps)]
    return tuple(ts)


@jax.custom_vjp
def _inv(mats):
    return _inv_impl(mats)


def _inv_fwd(mats):
    ts = _inv_impl(mats)
    return ts, ts


def _inv_bwd(ts, gs):
    xs = [_hdot(t, g, 0, 0) for t, g in zip(ts, gs)]
    return (tuple(-_hdot(x, t, 1, 1) for x, t in zip(xs, ts)),)


_inv.defvjp(_inv_fwd, _inv_bwd)


def _tile(n, cap):
    if n <= cap:
        return n
    best = [t for t in range(128, cap + 1, 128) if n % t == 0]
    assert best, (n, cap)
    return best[-1]


def _mm(a, b, *, name, ta=False, tb=False, add=None, tm=1024, tn=1024, tk=1024):
    if ta:
        k_, m_ = a.shape
    else:
        m_, k_ = a.shape
    n_ = b.shape[0] if tb else b.shape[1]
    tm, tn, tk = _tile(m_, tm), _tile(n_, tn), _tile(k_, tk)
    nk = k_ // tk
    has_add = add is not None

    def body(*refs):
        a_ref, b_ref = refs[:2]
        add_ref = refs[2] if has_add else None
        o_ref = refs[3] if has_add else refs[2]
        part = _raw_dot(a_ref[...], b_ref[...], 0 if ta else 1, 1 if tb else 0)
        if nk == 1:
            o_ref[...] = part + add_ref[...] if has_add else part
            return
        acc = refs[-1]
        k = pl.program_id(2)

        @pl.when(k == 0)
        def _():
            acc[...] = part

        @pl.when(jnp.logical_and(k > 0, k < nk - 1))
        def _():
            acc[...] += part

        @pl.when(k == nk - 1)
        def _():
            o_ref[...] = acc[...] + part + add_ref[...] if has_add else acc[...] + part

    a_spec = pl.BlockSpec((tk, tm), lambda i, j, k: (k, i)) if ta else pl.BlockSpec((tm, tk), lambda i, j, k: (i, k))
    b_spec = pl.BlockSpec((tn, tk), lambda i, j, k: (j, k)) if tb else pl.BlockSpec((tk, tn), lambda i, j, k: (k, j))
    o_spec = pl.BlockSpec((tm, tn), lambda i, j, k: (i, j))
    in_specs = [a_spec, b_spec] + ([o_spec] if has_add else [])
    args = (a, b) + ((add,) if has_add else ())
    return pl.pallas_call(
        body, name=name, grid=(m_ // tm, n_ // tn, nk), in_specs=in_specs, out_specs=o_spec,
        out_shape=jax.ShapeDtypeStruct((m_, n_), F32),
        scratch_shapes=[pltpu.VMEM((tm, tn), F32)] if nk > 1 else [],
        compiler_params=_params(3))(*args)


def _row_spec(tb, w, c0, percol):
    return pl.BlockSpec((tb, w), lambda i, j, c0=c0, pc=percol: (i, c0 + (j if pc else 0)))


def _stage_fwd(f, name, rows, params, outs, tb, ncol=1):
    t_ = rows[0][0].shape[0]
    nr, npar = len(rows), len(params)

    def body(*refs):
        res = f(*[r[...].astype(F32) for r in refs[:nr + npar]])
        for o_ref, val in zip(refs[nr + npar:], res):
            o_ref[...] = val.astype(o_ref.dtype)

    return pl.pallas_call(
        body, name=name, grid=(t_ // tb, ncol),
        in_specs=[_row_spec(tb, w, c0, pc) for (_, w, c0, pc) in rows] + [_full(p) for p in params],
        out_specs=[pl.BlockSpec((tb, w), lambda i, j: (i, j)) for w, _ in outs],
        out_shape=[jax.ShapeDtypeStruct((t_, w * ncol), dt) for w, dt in outs],
        compiler_params=_params(2))(*[r[0] for r in rows], *params)


def _into(dest, tb, width, t_, ncol, dtype):
    if dest is None:
        return pl.BlockSpec((tb, width), lambda i, j: (i, j)), jax.ShapeDtypeStruct((t_, width * ncol), dtype), None
    buf, total, c0 = dest
    return (pl.BlockSpec((tb, width), lambda i, j, c0=c0: (i, c0 + j)), jax.ShapeDtypeStruct((t_, total), dtype), buf)


def _stage_bwd(f, name, rows, params, douts, tb, ncol=1, cat=None, dtypes=None, dest=None):
    t_ = rows[0][0].shape[0]
    nr, npar, nd = len(rows), len(params), len(douts)
    cat = cat if cat is not None else [[r] for r in range(nr)]
    dtypes = dtypes if dtypes is not None else [F32] * len(cat)
    dest = dest or {}
    assert ncol == 1 or all(len(g) == 1 and rows[g[0]][3] for g in cat)
    nin = nr + npar + nd

    def body(*refs):
        ins = [r[...].astype(F32) for r in refs[:nr + npar]]
        dvals = tuple(r[...].astype(F32) for r in refs[nr + npar:nin])
        out_refs = refs[nin + len(aliased):]
        _, vjp = jax.vjp(f, *ins)
        grads = vjp(dvals)
        for o_ref, grp in zip(out_refs[:len(cat)], cat):
            val = grads[grp[0]] if len(grp) == 1 else jnp.concatenate([grads[r] for r in grp], axis=-1)
            o_ref[...] = val.astype(o_ref.dtype)
        first = jnp.logical_and(pl.program_id(0) == 0, pl.program_id(1) == 0)
        for p_ref, gp in zip(out_refs[len(cat):], grads[nr:]):
            @pl.when(first)
            def _(p_ref=p_ref):
                p_ref[...] = jnp.zeros_like(p_ref)
            p_ref[...] += gp

    gw = [sum(rows[r][1] for r in grp) for grp in cat]
    out_specs, out_shape, aliased, aliases = [], [], [], {}
    for gi, (w, dt) in enumerate(zip(gw, dtypes)):
        spec, shape, buf = _into(dest.get(gi), tb, w, t_, ncol, dt)
        out_specs.append(spec)
        out_shape.append(shape)
        if buf is not None:
            aliases[nin + len(aliased)] = gi
            aliased.append(buf)
    return pl.pallas_call(
        body, name=name, grid=(t_ // tb, ncol),
        in_specs=[_row_spec(tb, w, c0, pc) for (_, w, c0, pc) in rows] + [_full(p) for p in params]
        + [pl.BlockSpec((tb, d.shape[1] // ncol), lambda i, j: (i, j)) for d in douts] + [ANY] * len(aliased),
        out_specs=out_specs + [_full(p) for p in params],
        out_shape=out_shape + [jax.ShapeDtypeStruct(p.shape, F32) for p in params],
        input_output_aliases=aliases,
        compiler_params=_params(2))(*[r[0] for r in rows], *params, *douts, *aliased)


def _ln_f(x, y, g, b):
    u = ALPHA * x + y
    c = u - jnp.mean(u, axis=-1, keepdims=True)
    var = jnp.mean(c * c, axis=-1, keepdims=True)
    return (c * lax.rsqrt(var + LN_EPS) * g + b,)


def _ln_f2(x, y, g, b):
    out, = _ln_f(x, y, g, b)
    return out, out


def _swiglu_f(gate, up):
    return (jax.nn.silu(gate) * up,)


def _merge_f(ya, yb, ga, gb):
    return (jax.nn.sigmoid(ga) * ya + jax.nn.sigmoid(gb) * yb,)


def _gnorm_f(of, ob, z, w):
    o = of + ob
    return (o * lax.rsqrt(jnp.mean(o * o, axis=-1, keepdims=True) + RMS_EPS) * w * jax.nn.silu(z),)


def _bg_f(x, arow, dtrow):
    lane = lax.broadcasted_iota(jnp.int32, x.shape, 1)
    beta = jax.nn.sigmoid(x)
    g = -jnp.exp(arow) * jax.nn.softplus(x + dtrow)
    return (jnp.where(lane < 16, beta, jnp.where(lane < 32, g, 0.0)),)


PREP_ROWS = 512
PAD = 8


def _prep_f(part, w, *wins):
    xc = wins[0] * w[0:1, :]
    for k in range(1, DN_CONV):
        xc = xc + wins[k] * w[k:k + 1, :]
    a = jax.nn.silu(xc)
    nrm = a * lax.rsqrt(jnp.sum(a * a, axis=-1, keepdims=True) + RMS_EPS)
    return jnp.where(part == 0, nrm * (DN_DIM ** -0.5), jnp.where(part == 1, nrm, a))


def _windows(pad_ref, r0, rows):
    return [pad_ref[PAD + r0 - 2 + k:PAD + r0 - 2 + k + rows, :] for k in range(DN_CONV)]


def _prep_fwd(pm, conv):
    t_ = pm.shape[0]
    rows = min(PREP_ROWS, t_)

    def body(x_ref, w_ref, o_ref, pad_ref):
        part = pl.program_id(0) // DN_HEADS
        pad_ref[0:PAD, :] = jnp.zeros((PAD, DN_DIM), F32)
        pad_ref[PAD + t_:2 * PAD + t_, :] = jnp.zeros((PAD, DN_DIM), F32)
        pad_ref[PAD:PAD + t_, :] = x_ref[...]
        w = w_ref[...]
        for r in range(t_ // rows):
            o_ref[r * rows:(r + 1) * rows, :] = _prep_f(part, w, *_windows(pad_ref, r * rows, rows))

    ncb = 3 * DN_HEADS
    return pl.pallas_call(
        body, name="prep_fwd", grid=(ncb,),
        in_specs=[pl.BlockSpec((t_, DN_DIM), lambda j: (0, j)), pl.BlockSpec((DN_CONV, DN_DIM), lambda j: (0, j))],
        out_specs=pl.BlockSpec((t_, DN_DIM), lambda j: (0, j)),
        out_shape=jax.ShapeDtypeStruct((t_, ncb * DN_DIM), F32),
        scratch_shapes=[pltpu.VMEM((t_ + 2 * PAD, DN_DIM), F32)],
        compiler_params=_params(1))(pm, conv)


def _prep_bwd(pm, conv, dout, dpm):
    t_ = pm.shape[0]
    rows = min(PREP_ROWS, t_)

    def body(x_ref, w_ref, d_ref, _, dx_ref, dw_ref, pad_ref, dpad_ref):
        part = pl.program_id(0) // DN_HEADS
        pad_ref[0:PAD, :] = jnp.zeros((PAD, DN_DIM), F32)
        pad_ref[PAD + t_:2 * PAD + t_, :] = jnp.zeros((PAD, DN_DIM), F32)
        pad_ref[PAD:PAD + t_, :] = x_ref[...]
        dpad_ref[...] = jnp.zeros_like(dpad_ref)
        w = w_ref[...]
        dw = jnp.zeros((DN_CONV, DN_DIM), F32)
        for r in range(t_ // rows):
            r0 = r * rows
            _, vjp = jax.vjp(functools.partial(_prep_f, part), w, *_windows(pad_ref, r0, rows))
            grads = vjp(d_ref[r0:r0 + rows, :])
            dw = dw + grads[0]
            for k in range(DN_CONV):
                lo = PAD + r0 - 2 + k
                dpad_ref[lo:lo + rows, :] += grads[1 + k]
        dx_ref[...] = dpad_ref[PAD:PAD + t_, :].astype(dx_ref.dtype)
        dw_ref[...] = dw

    ncb = 3 * DN_HEADS
    col = pl.BlockSpec((t_, DN_DIM), lambda j: (0, j))
    wsp = pl.BlockSpec((DN_CONV, DN_DIM), lambda j: (0, j))
    return pl.pallas_call(
        body, name="prep_bwd", grid=(ncb,), in_specs=[col, wsp, col, ANY], out_specs=[col, wsp],
        out_shape=[jax.ShapeDtypeStruct(dpm.shape, dpm.dtype), jax.ShapeDtypeStruct((DN_CONV, ncb * DN_DIM), F32)],
        scratch_shapes=[pltpu.VMEM((t_ + 2 * PAD, DN_DIM), F32), pltpu.VMEM((t_ + 2 * PAD, DN_DIM), F32)],
        input_output_aliases={3: 0}, compiler_params=_params(1))(pm, conv, dout, dpm)


def _dn_chunk(sgns, qs, ks, vs, grows, brows):
    c = qs[0].shape[0]
    i = lax.broadcasted_iota(jnp.int32, (c, c), 0)
    j = lax.broadcasted_iota(jnp.int32, (c, c), 1)
    eye = i == j
    incl = {s: (i - j) * int(s) >= 0 for s in set(sgns)}
    strict = {s: (i - j) * int(s) > 0 for s in set(sgns)}
    gcs = [jnp.sum(jnp.where(incl[s], g, 0.0), axis=1, keepdims=True) for s, g in zip(sgns, grows)]
    grs = [jnp.sum(jnp.where(eye, gc, 0.0), axis=0, keepdims=True) for gc in gcs]
    bcs = [jnp.sum(jnp.where(eye, b, 0.0), axis=1, keepdims=True) for b in brows]
    gls = [jnp.sum(g, axis=1, keepdims=True) for g in grows]
    decs = [jnp.exp(jnp.where(incl[s], gc - gr, -1e30)) for s, gc, gr in zip(sgns, gcs, grs)]
    kks = [_nt(k, k) for k in ks]
    tinvs = _inv(tuple(jnp.where(strict[s], bc * kk * dec, 0.0) for s, bc, kk, dec in zip(sgns, bcs, kks, decs)))
    egcs = [jnp.exp(gc) for gc in gcs]
    us = [_nn(t, v * bc) for t, v, bc in zip(tinvs, vs, bcs)]
    ws = [_nn(t, k * (bc * egc)) for t, k, bc, egc in zip(tinvs, ks, bcs, egcs)]
    qks = [_nt(q, k) * dec for q, k, dec in zip(qs, ks, decs)]
    qds = [q * egc for q, egc in zip(qs, egcs)]
    kds = [k * jnp.exp(gl - gc) for k, gl, gc in zip(ks, gls, gcs)]
    return tuple(us), tuple(ws), tuple(qks), tuple(qds), tuple(kds)


def _dn_step(us, ws, qks, qds, kds, grows, ss):
    gls = [jnp.exp(jnp.sum(g, axis=1, keepdims=True)) for g in grows]
    wss = [_nn(w, s) for w, s in zip(ws, ss)]
    qss = [_nn(qd, s) for qd, s in zip(qds, ss)]
    vns = [u - x for u, x in zip(us, wss)]
    os_ = [a + _nn(qk, vn) for a, qk, vn in zip(qss, qks, vns)]
    s2s = [s * gl + _tn(kd, vn) for s, gl, kd, vn in zip(ss, gls, kds, vns)]
    return tuple(os_), tuple(s2s)


def _hs(h):
    return slice(h * DN_DIM, (h + 1) * DN_DIM)


_DIR_SGN = (1, -1)
_A_OUT = 5
_PROBLEMS = [(d, h) for d in range(2) for h in range(DN_HEADS)]
_SGNS = [_DIR_SGN[d] for d, _ in _PROBLEMS]


def _chunk_inputs(q_ref, k_ref, v_ref, g_ref, b_ref):
    heads = lambda ref: tuple(ref[:, _hs(h)].astype(F32) for _, h in _PROBLEMS)
    rows = lambda ref: tuple(ref[d, 0, h:h + 1, :] for d, h in _PROBLEMS)
    return heads(q_ref), heads(k_ref), heads(v_ref), rows(g_ref), rows(b_ref)


def _dn_a_fwd(qkv, grows, brows):
    t_ = qkv.shape[0]
    nch = t_ // CHUNK

    def body(q_ref, k_ref, v_ref, g_ref, b_ref, *outs):
        us, ws, qks, qds, kds = _dn_chunk(_SGNS, *_chunk_inputs(q_ref, k_ref, v_ref, g_ref, b_ref))
        for p, (d, h) in enumerate(_PROBLEMS):
            u_ref, w_ref, qk_ref, qd_ref, kd_ref = outs[d * _A_OUT:(d + 1) * _A_OUT]
            u_ref[:, _hs(h)], w_ref[:, _hs(h)], qk_ref[0, h] = us[p], ws[p].astype(_MXU), qks[p].astype(_MXU)
            qd_ref[:, _hs(h)], kd_ref[:, _hs(h)] = qds[p].astype(_MXU), kds[p].astype(_MXU)

    rspec = pl.BlockSpec((2, 1, DN_HEADS, CHUNK), lambda c: (0, c, 0, 0))
    big = pl.BlockSpec((CHUNK, D), lambda c: (c, 0))
    qks = pl.BlockSpec((1, DN_HEADS, CHUNK, CHUNK), lambda c: (c, 0, 0, 0))
    bigs = lambda dt: jax.ShapeDtypeStruct((t_, D), dt)
    qksh = jax.ShapeDtypeStruct((nch, DN_HEADS, CHUNK, CHUNK), _MXU)
    return pl.pallas_call(
        body, name="dn_a_fwd", grid=(nch,),
        in_specs=[pl.BlockSpec((CHUNK, D), lambda c, p=p: (c, p)) for p in range(3)] + [rspec, rspec],
        out_specs=[big, big, qks, big, big] * 2, out_shape=[bigs(F32), bigs(_MXU), qksh, bigs(_MXU), bigs(_MXU)] * 2,
        compiler_params=_params(1))(qkv, qkv, qkv, grows, brows)


def _dn_a_bwd(qkv, grows, brows, dres, dg_b):
    t_ = qkv.shape[0]
    nch = t_ // CHUNK

    def body(q_ref, k_ref, v_ref, g_ref, b_ref, *rest):
        dins, dgb_ref, (dqkv_ref, dg_ref, db_ref) = rest[:2 * _A_OUT], rest[2 * _A_OUT], rest[2 * _A_OUT + 1:]
        _, vjp = jax.vjp(functools.partial(_dn_chunk, _SGNS), *_chunk_inputs(q_ref, k_ref, v_ref, g_ref, b_ref))
        cots = []
        for o in range(_A_OUT):
            cots.append(tuple(dins[d * _A_OUT + o][0, h] if o == 2 else dins[d * _A_OUT + o][:, _hs(h)]
                              for d, h in _PROBLEMS))
        gq, gk, gv, gg, gb = vjp(tuple(cots))
        for p, (d, h) in enumerate(_PROBLEMS):
            dg_ref[d, 0, h:h + 1, :] = gg[p] + dgb_ref[d, 0, h:h + 1, :]
            db_ref[d, 0, h:h + 1, :] = gb[p]
        for h in range(DN_HEADS):
            dqkv_ref[:, _hs(h)] = gq[h] + gq[DN_HEADS + h]
            dqkv_ref[:, _hs(DN_HEADS + h)] = gk[h] + gk[DN_HEADS + h]
            dqkv_ref[:, _hs(2 * DN_HEADS + h)] = gv[h] + gv[DN_HEADS + h]

    rspec = pl.BlockSpec((2, 1, DN_HEADS, CHUNK), lambda c: (0, c, 0, 0))
    big = pl.BlockSpec((CHUNK, D), lambda c: (c, 0))
    qks = pl.BlockSpec((1, DN_HEADS, CHUNK, CHUNK), lambda c: (c, 0, 0, 0))
    rsh = jax.ShapeDtypeStruct(grows.shape, F32)
    return pl.pallas_call(
        body, name="dn_a_bwd", grid=(nch,),
        in_specs=[pl.BlockSpec((CHUNK, D), lambda c, p=p: (c, p)) for p in range(3)] + [rspec, rspec]
        + [big, big, qks, big, big] * 2 + [rspec],
        out_specs=[pl.BlockSpec((CHUNK, 3 * D), lambda c: (c, 0)), rspec, rspec],
        out_shape=[jax.ShapeDtypeStruct((t_, 3 * D), F32), rsh, rsh],
        compiler_params=_params(1))(qkv, qkv, qkv, grows, brows, *dres, dg_b)


def _dir_specs(nch):
    def cidx(d):
        return (lambda n: n) if d == 0 else (lambda n: nch - 1 - n)
    out = []
    for d in range(2):
        ci = cidx(d)
        big = pl.BlockSpec((CHUNK, D), lambda n, ci=ci: (ci(n), 0))
        qks = pl.BlockSpec((1, DN_HEADS, CHUNK, CHUNK), lambda n, ci=ci: (ci(n), 0, 0, 0))
        row = pl.BlockSpec((1, 1, DN_HEADS, CHUNK), lambda n, ci=ci, d=d: (d, ci(n), 0, 0))
        st = pl.BlockSpec((1, DN_HEADS, DN_DIM, DN_DIM), lambda n, ci=ci: (ci(n), 0, 0, 0))
        out.append(dict(big=big, qk=qks, row=row, st=st))
    return out


def _step_inputs(ins, per_dir):
    def pick(o):
        if o == 2:
            return tuple(ins[d * per_dir + o][0, h].astype(F32) for d, h in _PROBLEMS)
        if o == 5:
            return tuple(ins[d * per_dir + o][0, 0, h:h + 1, :] for d, h in _PROBLEMS)
        return tuple(ins[d * per_dir + o][:, _hs(h)].astype(F32) for d, h in _PROBLEMS)
    return [pick(o) for o in range(6)]


def _dn_b_fwd(ares, grows):
    t_ = ares[0].shape[0]
    nch = t_ // CHUNK
    sp = _dir_specs(nch)

    def body(*refs):
        ins, outs, s_ref = refs[:12], refs[12:16], refs[16]

        @pl.when(pl.program_id(0) == 0)
        def _():
            s_ref[...] = jnp.zeros_like(s_ref)

        ss = tuple(s_ref[p] for p in range(len(_PROBLEMS)))
        os_, s2s = _dn_step(*_step_inputs(ins, 6), ss)
        for p, (d, h) in enumerate(_PROBLEMS):
            outs[2 + d][0, h] = ss[p]
            outs[d][:, _hs(h)] = os_[p]
            s_ref[p] = s2s[p]

    in_specs, args = [], []
    for d in range(2):
        in_specs += [sp[d]["big"], sp[d]["big"], sp[d]["qk"], sp[d]["big"], sp[d]["big"], sp[d]["row"]]
        args += list(ares[d * _A_OUT:(d + 1) * _A_OUT]) + [grows]
    stsh = jax.ShapeDtypeStruct((nch, DN_HEADS, DN_DIM, DN_DIM), F32)
    osh = jax.ShapeDtypeStruct((t_, D), F32)
    return pl.pallas_call(
        body, name="dn_b_fwd", grid=(nch,), in_specs=in_specs,
        out_specs=[sp[0]["big"], sp[1]["big"], sp[0]["st"], sp[1]["st"]], out_shape=[osh, osh, stsh, stsh],
        scratch_shapes=[pltpu.VMEM((2 * DN_HEADS, DN_DIM, DN_DIM), F32)],
        compiler_params=_params(1))(*args)


def _dn_b_bwd(ares, grows, st_f, st_b, do):
    t_ = ares[0].shape[0]
    nch = t_ // CHUNK
    sp = _dir_specs(nch)
    rsp = [sp[1], sp[0]]

    def body(*refs):
        ins, outs, ds_ref = refs[:16], refs[16:28], refs[28]

        @pl.when(pl.program_id(0) == 0)
        def _():
            ds_ref[...] = jnp.zeros_like(ds_ref)

        ss = tuple(ins[d * 8 + 6][0, h] for d, h in _PROBLEMS)
        _, vjp = jax.vjp(_dn_step, *_step_inputs(ins, 8), ss)
        dos = tuple(ins[d * 8 + 7][:, _hs(h)] for d, h in _PROBLEMS)
        grads = vjp((dos, tuple(ds_ref[p] for p in range(len(_PROBLEMS)))))
        for p, (d, h) in enumerate(_PROBLEMS):
            du_ref, dw_ref, dqk_ref, dqd_ref, dkd_ref, dg_ref = outs[d * 6:(d + 1) * 6]
            du_ref[:, _hs(h)], dw_ref[:, _hs(h)], dqk_ref[0, h] = grads[0][p], grads[1][p], grads[2][p]
            dqd_ref[:, _hs(h)], dkd_ref[:, _hs(h)] = grads[3][p], grads[4][p]
            dg_ref[0, 0, h:h + 1, :] = grads[5][p]
            ds_ref[p] = grads[6][p]

    in_specs, args, out_specs, out_shape = [], [], [], []
    big_sh = jax.ShapeDtypeStruct((t_, D), F32)
    qk_sh = jax.ShapeDtypeStruct((nch, DN_HEADS, CHUNK, CHUNK), F32)
    row_sh = jax.ShapeDtypeStruct((1, nch, DN_HEADS, CHUNK), F32)
    for d in range(2):
        s = rsp[d]
        row0 = pl.BlockSpec((1, 1, DN_HEADS, CHUNK), lambda m, d=d: (0, (nch - 1 - m) if d == 0 else m, 0, 0))
        rowd = pl.BlockSpec((1, 1, DN_HEADS, CHUNK), lambda m, d=d: (d, (nch - 1 - m) if d == 0 else m, 0, 0))
        in_specs += [s["big"], s["big"], s["qk"], s["big"], s["big"], rowd, s["st"], s["big"]]
        args += list(ares[d * _A_OUT:(d + 1) * _A_OUT]) + [grows, (st_f, st_b)[d], do]
        out_specs += [s["big"], s["big"], s["qk"], s["big"], s["big"], row0]
        out_shape += [big_sh, big_sh, qk_sh, big_sh, big_sh, row_sh]
    res = pl.pallas_call(
        body, name="dn_b_bwd", grid=(nch,), in_specs=in_specs, out_specs=out_specs, out_shape=out_shape,
        scratch_shapes=[pltpu.VMEM((2 * DN_HEADS, DN_DIM, DN_DIM), F32)],
        compiler_params=_params(1))(*args)
    dares = list(res[0:5]) + list(res[6:11])
    return dares, jnp.concatenate([res[5], res[11]], axis=0)


def _rope(x, c, s):
    half = SW_DIM // 2
    return x * c + jnp.concatenate([-x[:, half:], x[:, :half]], axis=-1) * s


def _attn_f(blk, t_, cq, sq, ck, sk, q, kp, ko, kn, vp, vo, vn, sinks):
    kall = jnp.concatenate([kp, ko, kn], axis=0)
    vall = jnp.concatenate([vp, vo, vn], axis=0)
    nq, nk = SW_GRP * SW_BLOCK, 3 * SW_BLOCK
    qpos = lax.broadcasted_iota(jnp.int32, (nq, nk), 0) % SW_BLOCK
    krel = lax.broadcasted_iota(jnp.int32, (nq, nk), 1) - SW_BLOCK
    kglob = krel + blk * SW_BLOCK
    valid = (jnp.abs(qpos - krel) <= SW_BLOCK) & (kglob >= 0) & (kglob < t_)
    kvs = range(SW_KV)
    heads = [[kvh * SW_GRP + g for g in range(SW_GRP)] for kvh in kvs]
    khs = [_rope(kall[:, kvh * SW_DIM:(kvh + 1) * SW_DIM], ck, sk) for kvh in kvs]
    vhs = [vall[:, kvh * SW_DIM:(kvh + 1) * SW_DIM] for kvh in kvs]
    qgs = [jnp.concatenate([_rope(q[:, h * SW_DIM:(h + 1) * SW_DIM], cq, sq) for h in hs], axis=0) for hs in heads]
    ss = [jnp.where(valid, _nt(qg, kh) * (SW_DIM ** -0.5), -1e30) for qg, kh in zip(qgs, khs)]
    snks = [jnp.concatenate([jnp.broadcast_to(sinks[:, h:h + 1], (SW_BLOCK, 1)) for h in hs], axis=0) for hs in heads]
    ms = [lax.stop_gradient(jnp.maximum(jnp.max(s, axis=-1, keepdims=True), snk)) for s, snk in zip(ss, snks)]
    es = [jnp.exp(s - m) for s, m in zip(ss, ms)]
    ps = [e / (jnp.sum(e, axis=-1, keepdims=True) + jnp.exp(snk - m)) for e, snk, m in zip(es, snks, ms)]
    ogs = [_nn(p, vh) for p, vh in zip(ps, vhs)]
    return jnp.concatenate([og[g * SW_BLOCK:(g + 1) * SW_BLOCK] for og in ogs for g in range(SW_GRP)], axis=-1)


def _attn_specs(nb):
    prv = lambda i: jnp.maximum(i - 1, 0)
    nxt = lambda i: jnp.minimum(i + 1, nb - 1)
    rows = [lambda i: i, prv, lambda i: i, nxt]
    tab = [pl.BlockSpec((SW_BLOCK, SW_DIM), lambda i, r=r: (r(i), 0)) for r in rows]
    qs = pl.BlockSpec((SW_BLOCK, SW_HEADS * SW_DIM), lambda i: (i, O_QS // (SW_HEADS * SW_DIM)))
    kw = SW_KV * SW_DIM
    ks = [pl.BlockSpec((SW_BLOCK, kw), lambda i, r=r: (r(i), O_KS // kw)) for r in rows[1:]]
    vs = [pl.BlockSpec((SW_BLOCK, kw), lambda i, r=r: (r(i), O_VS // kw)) for r in rows[1:]]
    return tab, qs, ks, vs


def _attn_tables(refs):
    cq, cp, co, cn, sq, sp_, so, sn = [r[...] for r in refs]
    return cq, sq, jnp.concatenate([cp, co, cn], axis=0), jnp.concatenate([sp_, so, sn], axis=0)


def _attn_fwd(pm, cos, sin, sinks):
    t_ = pm.shape[0]
    nb = t_ // SW_BLOCK
    tab, qs, ks, vs = _attn_specs(nb)

    def body(*refs):
        tabs = _attn_tables(refs[:8])
        vals = [r[...] for r in refs[8:16]]
        refs[16][...] = _attn_f(pl.program_id(0), t_, *tabs, *vals).astype(refs[16].dtype)

    return pl.pallas_call(
        body, name="attn_fwd", grid=(nb,), in_specs=tab + tab + [qs] + ks + vs + [_full(sinks)],
        out_specs=pl.BlockSpec((SW_BLOCK, D), lambda i: (i, 0)), out_shape=jax.ShapeDtypeStruct((t_, D), _MXU),
        compiler_params=_params(1))(*([cos] * 4), *([sin] * 4), pm, pm, pm, pm, pm, pm, pm, sinks)


def _attn_bwd(pm, cos, sin, sinks, do, dpm):
    t_ = pm.shape[0]
    nb = t_ // SW_BLOCK
    tab, qs, ks, vs = _attn_specs(nb)
    kw = SW_KV * SW_DIM

    def body(*refs):
        tabs = _attn_tables(refs[:8])
        vals = [r[...] for r in refs[8:16]]
        do_ref, outs = refs[16], refs[18:]
        _, vjp = jax.vjp(functools.partial(_attn_f, pl.program_id(0), t_, *tabs), *vals)
        grads = vjp(do_ref[...])
        for o_ref, g in zip(outs[:7], grads[:7]):
            o_ref[...] = g.astype(o_ref.dtype)

        @pl.when(pl.program_id(0) == 0)
        def _():
            outs[7][...] = jnp.zeros_like(outs[7])
        outs[7][...] += grads[7]

    own = lambda w: pl.BlockSpec((SW_BLOCK, w), lambda i: (i, 0))
    return pl.pallas_call(
        body, name="attn_bwd", grid=(nb,),
        in_specs=tab + tab + [qs] + ks + vs + [_full(sinks), own(D), ANY],
        out_specs=[pl.BlockSpec((SW_BLOCK, D), lambda i: (i, O_QS // D))] + [own(kw)] * 6 + [_full(sinks)],
        out_shape=[jax.ShapeDtypeStruct(dpm.shape, dpm.dtype)] + [jax.ShapeDtypeStruct((t_, kw), F32)] * 6
        + [jax.ShapeDtypeStruct(sinks.shape, F32)],
        input_output_aliases={17: 0},
        compiler_params=_params(1))(*([cos] * 4), *([sin] * 4), pm, pm, pm, pm, pm, pm, pm, sinks, do, dpm)


def _band_sum(kparts, vparts, dpm):
    t_, kw = kparts[1].shape
    nb = t_ // SW_BLOCK

    def body(kp, ko, kn, vp, vo, vn, _, out_ref):
        j = pl.program_id(0)
        band = lambda p, o, n: o[...] + jnp.where(j + 1 < nb, p[...], 0.0) + jnp.where(j > 0, n[...], 0.0)
        out_ref[...] = jnp.concatenate([band(kp, ko, kn), band(vp, vo, vn)], axis=-1).astype(out_ref.dtype)

    specs = [pl.BlockSpec((SW_BLOCK, kw), lambda j: (jnp.minimum(j + 1, nb - 1), 0)),
             pl.BlockSpec((SW_BLOCK, kw), lambda j: (j, 0)),
             pl.BlockSpec((SW_BLOCK, kw), lambda j: (jnp.maximum(j - 1, 0), 0))]
    return pl.pallas_call(
        body, name="band_sum", grid=(nb,), in_specs=specs * 2 + [ANY],
        out_specs=pl.BlockSpec((SW_BLOCK, 2 * kw), lambda j: (j, O_KS // (2 * kw))),
        out_shape=jax.ShapeDtypeStruct(dpm.shape, dpm.dtype), input_output_aliases={6: 0},
        compiler_params=_params(1))(*kparts, *vparts, dpm)


def _loss_head(y, target, tb=256):
    t_ = y.shape[0]

    def body(y_ref, t_ref, dy_ref, acc_ref):
        err = y_ref[...] - t_ref[...]
        dy_ref[...] = err * (1.0 / D)
        sq = (err * err).reshape(tb // 8, 8, D).sum(axis=0)
        part = sq[:, 0:128]
        for c in range(1, D // 128):
            part = part + sq[:, c * 128:(c + 1) * 128]

        @pl.when(pl.program_id(0) == 0)
        def _():
            acc_ref[...] = jnp.zeros_like(acc_ref)
        acc_ref[...] += part

    row = pl.BlockSpec((tb, D), lambda i: (i, 0))
    return pl.pallas_call(
        body, name="loss_head", grid=(t_ // tb,), in_specs=[row, row],
        out_specs=[row, pl.BlockSpec((8, 128), lambda i: (0, 0))],
        out_shape=[jax.ShapeDtypeStruct((t_, D), F32), jax.ShapeDtypeStruct((8, 128), F32)],
        compiler_params=_params(1))(y, target)


def _adamw(name, w, g, m, v):
    shape = w.shape
    cols = shape[-1]
    rows = w.size // cols
    w2, g2, m2, v2 = [a.reshape(rows, cols) for a in (w, g, m, v)]
    tb = rows
    while tb * cols * 4 > (1 << 20) and tb % 16 == 0:
        tb //= 2
    bc1 = 1.0 - ADAM_B1 ** ADAM_STEP
    bc2 = 1.0 - ADAM_B2 ** ADAM_STEP

    def body(w_ref, g_ref, m_ref, v_ref, d_ref, nm_ref, nv_ref):
        gv = g_ref[...]
        nm = ADAM_B1 * m_ref[...] + (1.0 - ADAM_B1) * gv
        nv = ADAM_B2 * v_ref[...] + (1.0 - ADAM_B2) * (gv * gv)
        d_ref[...] = -ADAM_LR * ((nm / bc1) / (jnp.sqrt(nv / bc2) + ADAM_EPS) + ADAM_WD * w_ref[...])
        nm_ref[...] = nm
        nv_ref[...] = nv

    spec = pl.BlockSpec((tb, cols), lambda i: (i, 0))
    sh = jax.ShapeDtypeStruct((rows, cols), F32)
    outs = pl.pallas_call(body, name=name, grid=(rows // tb,), in_specs=[spec] * 4, out_specs=[spec] * 3,
                          out_shape=[sh] * 3, compiler_params=_params(1))(w2, g2, m2, v2)
    return [o.reshape(shape) for o in outs]


def _to_rows(bg, col0):
    t_ = bg.shape[0]
    a = bg[:, col0:col0 + 2 * DN_HEADS].reshape(t_ // CHUNK, CHUNK, 2, DN_HEADS)
    return jnp.transpose(a, (2, 0, 3, 1))


def _from_rows(db, dg):
    nch = db.shape[1]
    back = lambda a: jnp.transpose(a, (1, 3, 0, 2)).reshape(nch * CHUNK, 2 * DN_HEADS)
    return jnp.pad(jnp.concatenate([back(db), back(dg)], axis=1), ((0, 0), (0, 128 - 4 * DN_HEADS)))


def _layer_fwd(x, xm, w, cos, sin):
    t_ = x.shape[0]
    tb, nb = min(256, t_), min(1024, t_)
    pm = _mm(xm, w["in_main"], name="mm_in", tn=1536)
    pbg = _mm(xm, w["in_bg"], name="mm_in_bg")
    qkv = _prep_fwd(pm, w["conv"])
    bg, = _stage_fwd(_bg_f, "bg_fwd", [(pbg, 128, 0, False)], [w["arow"], w["dtrow"]], [(128, F32)], nb)
    brows, grows = _to_rows(bg, 0), _to_rows(bg, 2 * DN_HEADS)
    ares = _dn_a_fwd(qkv, grows, brows)
    o_f, o_b, st_f, st_b = _dn_b_fwd(ares, grows)
    odn, = _stage_fwd(_gnorm_f, "gnorm_fwd", [(o_f, 128, 0, True), (o_b, 128, 0, True), (pm, 128, O_Z // 128, True)],
                      [w["gnw"]], [(128, _MXU)], nb, ncol=DN_HEADS)
    osw = _attn_fwd(pm, cos, sin, w["sinks"])
    ya = _mm(odn, w["a"], name="mm_a")
    yb = _mm(osw, w["b"], name="mm_b")
    gates = [(pm, 512, O_GA // 512, True), (pm, 512, O_GB // 512, True)]
    merged, = _stage_fwd(_merge_f, "merge_fwd", [(ya, 512, 0, True), (yb, 512, 0, True)] + gates, [], [(512, _MXU)],
                         tb, ncol=2)
    mix = _mm(merged, w["o"], name="mm_o")
    x1, x1m = _stage_fwd(_ln_f2, "ln1_fwd", [(x, D, 0, False), (mix, D, 0, False)], [w["ln1g"], w["ln1b"]],
                         [(D, F32), (D, _MXU)], tb)
    gu = _mm(x1m, w["gu"], name="mm_gu", tn=FFN // 2)
    hid, = _stage_fwd(_swiglu_f, "swiglu_fwd", [(gu, FFN // 2, 0, True), (gu, FFN // 2, 2, True)], [],
                      [(FFN // 2, _MXU)], tb, ncol=2)
    ffn = _mm(hid, w["d"], name="mm_d", tk=FFN // 2)
    x2, x2m = _stage_fwd(_ln_f2, "ln2_fwd", [(x1, D, 0, False), (ffn, D, 0, False)], [w["ln2g"], w["ln2b"]],
                         [(D, F32), (D, _MXU)], tb)
    res = dict(x=x, xm=xm, pm=pm, pbg=pbg, qkv=qkv, grows=grows, brows=brows, ares=ares, o_f=o_f, o_b=o_b, st_f=st_f,
               st_b=st_b, odn=odn, osw=osw, ya=ya, yb=yb, merged=merged, mix=mix, x1=x1, x1m=x1m, gu=gu, hid=hid,
               ffn=ffn)
    return x2, x2m, res


def _layer_bwd(dx2, r, w, cos, sin):
    t_ = dx2.shape[0]
    tb, sb, nb = min(256, t_), min(128, t_), min(1024, t_)
    pm = r["pm"]
    g = {}
    dx1a, dffn, g["ln2g"], g["ln2b"] = _stage_bwd(
        _ln_f, "ln2_bwd", [(r["x1"], D, 0, False), (r["ffn"], D, 0, False)], [w["ln2g"], w["ln2b"]], [dx2], tb,
        dtypes=[F32, _MXU])
    dhid = _mm(dffn, w["d"], tb=True, name="mm_d_dx", tn=FFN // 2)
    g["d"] = _mm(r["hid"], dffn, ta=True, name="mm_d_dw", tm=FFN // 2)
    dgu, = _stage_bwd(_swiglu_f, "swiglu_bwd", [(r["gu"], FFN, 0, False), (r["gu"], FFN, 1, False)], [], [dhid], sb,
                      cat=[[0, 1]], dtypes=[_MXU])
    dx1 = _mm(dgu, w["gu"], tb=True, add=dx1a, name="mm_gu_dx", tk=FFN // 2)
    g["gu"] = _mm(r["x1m"], dgu, ta=True, name="mm_gu_dw", tn=FFN // 2)
    dxa, dmix, g["ln1g"], g["ln1b"] = _stage_bwd(
        _ln_f, "ln1_bwd", [(r["x"], D, 0, False), (r["mix"], D, 0, False)], [w["ln1g"], w["ln1b"]], [dx1], tb,
        dtypes=[F32, _MXU])
    dmerged = _mm(dmix, w["o"], tb=True, name="mm_o_dx")
    g["o"] = _mm(r["merged"], dmix, ta=True, name="mm_o_dw")
    dya, dyb, dpm = _stage_bwd(
        _merge_f, "merge_bwd", [(r["ya"], D, 0, False), (r["yb"], D, 0, False), (pm, D, O_GA // D, False),
                                (pm, D, O_GB // D, False)], [], [dmerged], tb, cat=[[0], [1], [2, 3]],
        dtypes=[_MXU, _MXU, _MXU], dest={2: (None, N_MAIN, O_GA // (2 * D))})
    dodn = _mm(dya, w["a"], tb=True, name="mm_a_dx")
    g["a"] = _mm(r["odn"], dya, ta=True, name="mm_a_dw")
    dosw = _mm(dyb, w["b"], tb=True, name="mm_b_dx")
    g["b"] = _mm(r["osw"], dyb, ta=True, name="mm_b_dw")
    ab = _attn_bwd(pm, cos, sin, w["sinks"], dosw, dpm)
    dpm, g["sinks"] = ab[0], ab[7]
    dpm = _band_sum(ab[1:4], ab[4:7], dpm)
    dof, _, dpm, g["gnw"] = _stage_bwd(
        _gnorm_f, "gnorm_bwd", [(r["o_f"], 128, 0, True), (r["o_b"], 128, 0, True), (pm, 128, O_Z // 128, True)],
        [w["gnw"]], [dodn], nb, ncol=DN_HEADS, dtypes=[F32, F32, _MXU], dest={2: (dpm, N_MAIN, O_Z // 128)})
    dares, dg_b = _dn_b_bwd(r["ares"], r["grows"], r["st_f"], r["st_b"], dof)
    dqkv, dgrows, dbrows = _dn_a_bwd(r["qkv"], r["grows"], r["brows"], dares, dg_b)
    dbg = _from_rows(dbrows, dgrows)
    dpbg, g["arow"], g["dtrow"] = _stage_bwd(_bg_f, "bg_bwd", [(r["pbg"], 128, 0, False)], [w["arow"], w["dtrow"]],
                                             [dbg], nb)
    dpm, g["conv"] = _prep_bwd(pm, w["conv"], dqkv, dpm)
    dx = _mm(dpm, w["in_main"], tb=True, add=dxa, name="mm_in_dx", tk=1920)
    dx = _mm(dpbg, w["in_bg"], tb=True, add=dx, name="mm_in_bg_dx")
    g["in_main"] = _mm(r["xm"], dpm, ta=True, name="mm_in_dw", tn=1536)
    g["in_bg"] = _mm(r["xm"], dpbg, ta=True, name="mm_in_bg_dw")
    return dx, g


def _place():
    return lax.axis_index("x"), lax.axis_index("y"), lax.axis_index("c")


def _other_chips(x, y):
    return [(1 - x, y), (x, 1 - y), (1 - x, 1 - y)]


def _gather_weights(locs, kinds):
    n = len(locs)
    half = locs[0].shape[0] // 2

    def slot(ref, kind, q, width, lo=None):
        lsl = slice(None) if lo is None else pl.ds(lo, half)
        return ref.at[lsl, q] if kind == "row" else ref.at[lsl, :, pl.ds(pl.multiple_of(q * width, 128), width)]

    def body(*refs):
        loc_refs, out_refs = refs[:n], refs[n:2 * n]
        own_send, own_recv, ici_send, ici_recv, fwd_send, fwd_recv = refs[2 * n:]
        x, y, c = _place()
        myq = 2 * x + y
        sib = (x, y, 1 - c)
        chips = _other_chips(x, y)
        mine, theirs = c * half, (1 - c) * half
        width = [a.shape[-1] for a in locs]

        def copy(src, dst, ssem, rsem, dev):
            return pltpu.make_async_remote_copy(src_ref=src, dst_ref=dst, send_sem=ssem, recv_sem=rsem,
                                                device_id=dev, device_id_type=MESH)

        for t in range(n):
            copy(loc_refs[t], slot(out_refs[t], kinds[t], myq, width[t]), own_send.at[t], own_recv.at[t], sib).start()
            for j, (cx, cy) in enumerate(chips):
                copy(loc_refs[t].at[pl.ds(mine, half)], slot(out_refs[t], kinds[t], myq, width[t], mine),
                     ici_send.at[3 * t + j], ici_recv.at[3 * t + j], (cx, cy, c)).start()
        for t in range(n):
            for j, (cx, cy) in enumerate(chips):
                got = slot(out_refs[t], kinds[t], 2 * cx + cy, width[t], mine)
                copy(got, got, ici_send.at[3 * t + j], ici_recv.at[3 * t + j], (cx, cy, c)).wait_recv()
                copy(got, got, fwd_send.at[3 * t + j], fwd_recv.at[3 * t + j], sib).start()
        for t in range(n):
            whole = slot(out_refs[t], kinds[t], myq, width[t])
            copy(loc_refs[t], whole, own_send.at[t], own_recv.at[t], sib).wait_recv()
            copy(loc_refs[t], whole, own_send.at[t], own_recv.at[t], sib).wait_send()
            for j, (cx, cy) in enumerate(chips):
                relayed = slot(out_refs[t], kinds[t], 2 * cx + cy, width[t], theirs)
                copy(relayed, relayed, fwd_send.at[3 * t + j], fwd_recv.at[3 * t + j], sib).wait_recv()
                copy(relayed, relayed, fwd_send.at[3 * t + j], fwd_recv.at[3 * t + j], sib).wait_send()
                copy(relayed, relayed, ici_send.at[3 * t + j], ici_recv.at[3 * t + j], sib).wait_send()

    out_shape = []
    for a, kind in zip(locs, kinds):
        l_, r_, c_ = a.shape
        out_shape.append(jax.ShapeDtypeStruct((l_, N_CHIPS, r_, c_) if kind == "row" else (l_, r_, N_CHIPS * c_), a.dtype))
    dma = pltpu.SemaphoreType.DMA
    return pl.pallas_call(
        body, name="gather_weights", in_specs=[ANY] * n, out_specs=[ANY] * n, out_shape=out_shape,
        scratch_shapes=[dma((n,)), dma((n,)), dma((3 * n,)), dma((3 * n,)), dma((3 * n,)), dma((3 * n,))],
    )(*locs)


RS_CHUNKS = 2


def _piece(ref, kind, q, hf, pr, pc):
    if kind == "row":
        return ref.at[pl.ds((2 * q + hf) * pr, pr), :]
    return ref.at[pl.ds(hf * pr, pr), pl.ds(pl.multiple_of(q * pc, 128), pc)]


def _rs_to_sibling(ts, meta):
    n = len(ts)

    def body(*refs):
        t_refs, r1_refs, send_sems, recv_sems = refs[:n], refs[n:2 * n], refs[2 * n], refs[2 * n + 1]
        x, y, c = _place()
        sib = (x, y, 1 - c)
        for t, (kind, pr, pc) in enumerate(meta):
            for q in range(N_CHIPS):
                pltpu.make_async_remote_copy(
                    src_ref=_piece(t_refs[t], kind, q, 1 - c, pr, pc), dst_ref=r1_refs[t].at[q],
                    send_sem=send_sems.at[t], recv_sem=recv_sems.at[t], device_id=sib, device_id_type=MESH).start()
        for t in range(n):
            allq = pltpu.make_async_remote_copy(
                src_ref=r1_refs[t], dst_ref=r1_refs[t], send_sem=send_sems.at[t], recv_sem=recv_sems.at[t],
                device_id=sib, device_id_type=MESH)
            allq.wait_send()
            allq.wait_recv()

    return pl.pallas_call(
        body, name="rs_to_sibling", in_specs=[ANY] * n, out_specs=[ANY] * n,
        out_shape=[jax.ShapeDtypeStruct((N_CHIPS, pr, pc), F32) for (_, pr, pc) in meta],
        scratch_shapes=[pltpu.SemaphoreType.DMA((n,)), pltpu.SemaphoreType.DMA((n,))],
    )(*ts)


def _rs_add_sibling(ts, r1s, meta, c):
    n = len(ts)
    in_specs, out_specs, out_shape = [], [], []
    for kind, pr, pc in meta:
        rs = pr // RS_CHUNKS
        if kind == "row":
            in_specs.append(pl.BlockSpec((rs, pc), lambda q, r, c_ref: ((2 * q + c_ref[0]) * RS_CHUNKS + r, 0)))
        else:
            in_specs.append(pl.BlockSpec((rs, pc), lambda q, r, c_ref: (c_ref[0] * RS_CHUNKS + r, q)))
    for kind, pr, pc in meta:
        sp = pl.BlockSpec((None, pr // RS_CHUNKS, pc), lambda q, r, c_ref: (q, r, 0))
        in_specs.append(sp)
        out_specs.append(sp)
        out_shape.append(jax.ShapeDtypeStruct((N_CHIPS, pr, pc), BF16))

    def body(c_ref, *refs):
        for t in range(n):
            refs[2 * n + t][...] = (refs[t][...] + refs[n + t][...]).astype(BF16)

    return pl.pallas_call(
        body, name="rs_add_sibling", out_shape=out_shape,
        grid_spec=pltpu.PrefetchScalarGridSpec(num_scalar_prefetch=1, grid=(N_CHIPS, RS_CHUNKS), in_specs=in_specs,
                                               out_specs=out_specs),
        compiler_params=_params(2))(c.reshape(1).astype(jnp.int32), *ts, *r1s)


def _rs_to_chips(ps, meta):
    n = len(ps)

    def body(*refs):
        p_refs, r2_refs, send_sems, recv_sems = refs[:n], refs[n:2 * n], refs[2 * n], refs[2 * n + 1]
        x, y, c = _place()
        for t in range(n):
            for j, (cx, cy) in enumerate(_other_chips(x, y)):
                pltpu.make_async_remote_copy(
                    src_ref=p_refs[t].at[2 * cx + cy], dst_ref=r2_refs[t].at[j],
                    send_sem=send_sems.at[t], recv_sem=recv_sems.at[t], device_id=(cx, cy, c),
                    device_id_type=MESH).start()
        for t in range(n):
            allj = pltpu.make_async_remote_copy(
                src_ref=r2_refs[t], dst_ref=r2_refs[t], send_sem=send_sems.at[t], recv_sem=recv_sems.at[t],
                device_id=(x, y, c), device_id_type=MESH)
            allj.wait_send()
            allj.wait_recv()

    return pl.pallas_call(
        body, name="rs_to_chips", in_specs=[ANY] * n, out_specs=[ANY] * n,
        out_shape=[jax.ShapeDtypeStruct((N_CHIPS - 1, pr, pc), p.dtype) for p, (_, pr, pc) in zip(ps, meta)],
        scratch_shapes=[pltpu.SemaphoreType.DMA((n,)), pltpu.SemaphoreType.DMA((n,))],
    )(*ps)


def _rs_add_chips(ps, r2s, meta, myq, c):
    n = len(ps)
    in_specs, out_specs, out_shape = [], [], []
    for _, pr, pc in meta:
        in_specs.append(pl.BlockSpec((None, pr // RS_CHUNKS, pc), lambda r, q_ref, c_ref: (q_ref[0], r, 0)))
    for _, pr, pc in meta:
        in_specs.append(pl.BlockSpec((N_CHIPS - 1, pr // RS_CHUNKS, pc), lambda r, q_ref, c_ref: (0, r, 0)))
        out_specs.append(pl.BlockSpec((None, pr // RS_CHUNKS, pc), lambda r, q_ref, c_ref: (c_ref[0], r, 0)))
        out_shape.append(jax.ShapeDtypeStruct((2, pr, pc), F32))

    def body(q_ref, c_ref, *refs):
        for t in range(n):
            r2 = refs[n + t]
            own = refs[t][...].astype(F32)
            refs[2 * n + t][...] = ((own + r2[0].astype(F32)) + r2[1].astype(F32)) + r2[2].astype(F32)

    return pl.pallas_call(
        body, name="rs_add_chips", out_shape=out_shape,
        grid_spec=pltpu.PrefetchScalarGridSpec(num_scalar_prefetch=2, grid=(RS_CHUNKS,), in_specs=in_specs,
                                               out_specs=out_specs),
        compiler_params=_params(1))(myq.reshape(1).astype(jnp.int32), c.reshape(1).astype(jnp.int32), *ps, *r2s)


def _rs_share_halves(gs):
    n = len(gs)

    def body(*refs):
        g_refs, send_sems, recv_sems = refs[n:2 * n], refs[2 * n], refs[2 * n + 1]
        x, y, c = _place()
        sib = (x, y, 1 - c)
        for t in range(n):
            pltpu.make_async_remote_copy(
                src_ref=g_refs[t].at[c], dst_ref=g_refs[t].at[c], send_sem=send_sems.at[t], recv_sem=recv_sems.at[t],
                device_id=sib, device_id_type=MESH).start()
        for t in range(n):
            cp = pltpu.make_async_remote_copy(
                src_ref=g_refs[t].at[c], dst_ref=g_refs[t].at[1 - c], send_sem=send_sems.at[t],
                recv_sem=recv_sems.at[t], device_id=sib, device_id_type=MESH)
            cp.wait_send()
            cp.wait_recv()

    return pl.pallas_call(
        body, name="rs_share_halves", in_specs=[ANY] * n, out_specs=[ANY] * n,
        out_shape=[jax.ShapeDtypeStruct(g.shape, g.dtype) for g in gs], input_output_aliases={t: t for t in range(n)},
        scratch_shapes=[pltpu.SemaphoreType.DMA((n,)), pltpu.SemaphoreType.DMA((n,))],
    )(*gs)


def _reduce_scatter(ts, meta, c, myq):
    r1s = _rs_to_sibling(ts, meta)
    ps = _rs_add_sibling(ts, r1s, meta, c)
    r2s = _rs_to_chips(ps, meta)
    return _rs_share_halves(_rs_add_chips(ps, r2s, meta, myq, c))


def _allreduce_small(buf):
    rows = buf.shape[0]
    ndev = 8

    def body(b_ref, o_ref, slots, send_sems, recv_sems):
        x, y, c = _place()
        me = 4 * x + 2 * y + c
        slots[me] = b_ref[...]
        for k in range(1, ndev):
            kx, ky, kc = (k >> 2) & 1, (k >> 1) & 1, k & 1
            peer = (x ^ kx, y ^ ky, c ^ kc)
            pltpu.make_async_remote_copy(
                src_ref=b_ref, dst_ref=slots.at[me], send_sem=send_sems.at[k - 1], recv_sem=recv_sems.at[k - 1],
                device_id=peer, device_id_type=MESH).start()
        for k in range(1, ndev):
            kx, ky, kc = (k >> 2) & 1, (k >> 1) & 1, k & 1
            cp = pltpu.make_async_remote_copy(
                src_ref=b_ref, dst_ref=slots.at[me ^ k], send_sem=send_sems.at[k - 1], recv_sem=recv_sems.at[k - 1],
                device_id=(x ^ kx, y ^ ky, c ^ kc), device_id_type=MESH)
            cp.wait_send()
            cp.wait_recv()
        acc = slots[0]
        for s in range(1, ndev):
            acc = acc + slots[s]
        o_ref[...] = acc

    vm = pl.BlockSpec(memory_space=pltpu.VMEM)
    return pl.pallas_call(
        body, name="allreduce_small", in_specs=[vm], out_specs=vm, out_shape=jax.ShapeDtypeStruct((rows, 128), F32),
        scratch_shapes=[pltpu.VMEM((ndev, rows, 128), F32), pltpu.SemaphoreType.DMA((ndev - 1,)),
                        pltpu.SemaphoreType.DMA((ndev - 1,))],
        compiler_params=pltpu.CompilerParams(vmem_limit_bytes=VMEM_LIMIT))(buf)


RS_META = [("col", D // 2, IN_PAD), ("row", D // 8, D), ("row", D // 8, D), ("row", D // 8, D),
           ("col", D // 2, 2 * FFN // N_CHIPS), ("row", FFN // 8, D)]
SMALL_ROWS = 156


def _rope_tables(t_):
    half = SW_DIM // 2
    inv_freq = ROPE_THETA ** (-jnp.arange(half, dtype=F32) / half)
    ang = jnp.arange(t_, dtype=F32)[:, None] * inv_freq[None, :]
    return jnp.concatenate([jnp.cos(ang)] * 2, axis=1), jnp.concatenate([jnp.sin(ang)] * 2, axis=1)


def _lane_row(v16):
    return jnp.pad(v16.reshape(1, 2 * DN_HEADS), ((0, 0), (2 * DN_HEADS, 128 - 4 * DN_HEADS)))


def _pack_small(g):
    pad16 = jnp.pad(g["sinks"], ((0, 0), (0, 128 - SW_HEADS)))
    return jnp.concatenate([g["conv"].reshape(-1, 128), g["ln1g"].reshape(-1, 128), g["ln1b"].reshape(-1, 128),
                            g["ln2g"].reshape(-1, 128), g["ln2b"].reshape(-1, 128), g["gnw"], g["arow"], g["dtrow"],
                            pad16], axis=0)


def _unpack_small(buf):
    nconv = DN_CONV * 3 * D // 128
    o = nconv
    out = dict(conv=buf[:o].reshape(DN_CONV, 3 * D))
    for name in ("ln1g", "ln1b", "ln2g", "ln2b"):
        out[name] = buf[o:o + 8].reshape(D)
        o += 8
    out["gnw"] = buf[o]
    out["a_log"] = buf[o + 1, 2 * DN_HEADS:4 * DN_HEADS].reshape(2, DN_HEADS)
    out["dt_bias"] = buf[o + 2, 2 * DN_HEADS:4 * DN_HEADS].reshape(2, DN_HEADS)
    out["sinks"] = buf[o + 3, :SW_HEADS]
    return out


def kernel(x, w_in, conv_w, a_log, dt_bias, dn_norm_w, sinks, w_branch_a, w_branch_b, w_out, ln1_g, ln1_b, w_gate_up, w_down, ln2_g, ln2_b, loss_target, m_w_in, m_conv_w, m_a_log, m_dt_bias, m_dn_norm_w, m_sinks, m_w_branch_a, m_w_branch_b, m_w_out, m_ln1_g, m_ln1_b, m_w_gate_up, m_w_down, m_ln2_g, m_ln2_b, v_w_in, v_conv_w, v_a_log, v_dt_bias, v_dn_norm_w, v_sinks, v_w_branch_a, v_w_branch_b, v_w_out, v_ln1_g, v_ln1_b, v_w_gate_up, v_w_down, v_ln2_g, v_ln2_b):
    xi, yi, ci = _place()
    myq = 2 * xi + yi
    t_ = x.shape[1]
    cos, sin = _rope_tables(t_)

    loc_in = jnp.pad(w_in.astype(BF16), ((0, 0), (0, 0), (0, IN_PAD - IN_SHARD)))
    full_in, full_a, full_b, full_o, full_gu, full_d, full_conv = _gather_weights(
        [loc_in, w_branch_a.astype(BF16), w_branch_b.astype(BF16), w_out.astype(BF16), w_gate_up.astype(BF16),
         w_down.astype(BF16), conv_w], ["col", "row", "row", "row", "col", "row", "col"])
    layers = []
    for l in range(DEPTH):
        orig = jnp.concatenate([full_in[l, :, q * IN_PAD:q * IN_PAD + IN_SHARD] for q in range(N_CHIPS)], axis=1)
        layers.append(dict(
            in_main=jnp.concatenate([orig[:, :R_BG], orig[:, R_GATES:], orig[:, R_SW:R_GATES]], axis=1),
            in_bg=jnp.pad(orig[:, R_BG:R_BG + 4 * DN_HEADS], ((0, 0), (0, 128 - 4 * DN_HEADS))),
            conv=full_conv[l], arow=_lane_row(a_log[l]), dtrow=_lane_row(dt_bias[l]), gnw=dn_norm_w[l][None],
            sinks=sinks[l][None], a=full_a[l].reshape(D, D), b=full_b[l].reshape(D, D), o=full_o[l].reshape(D, D),
            ln1g=ln1_g[l][None], ln1b=ln1_b[l][None], gu=full_gu[l], d=full_d[l].reshape(FFN, D),
            ln2g=ln2_g[l][None], ln2b=ln2_b[l][None]))

    h = x[0]
    hm = h.astype(_MXU)
    residuals = []
    for l in range(DEPTH):
        h, hm, res = _layer_fwd(h, hm, layers[l], cos, sin)
        residuals.append(res)
    dh, sq = _loss_head(h, loss_target[0])
    loss = lax.psum((0.5 / D) * jnp.sum(sq), ("x", "y", "c"))

    big = [None] * DEPTH
    small = [None] * DEPTH
    for l in reversed(range(DEPTH)):
        dh, g = _layer_bwd(dh, residuals[l], layers[l], cos, sin)
        g_orig = jnp.concatenate([g["in_main"][:, :R_BG], g["in_bg"][:, :4 * DN_HEADS], g["in_main"][:, O_QS:],
                                  g["in_main"][:, O_GA:O_QS]], axis=1)
        zeros = jnp.zeros((D, IN_PAD - IN_SHARD), F32)
        g_in = jnp.concatenate(
            [p for q in range(N_CHIPS) for p in (g_orig[:, q * IN_SHARD:(q + 1) * IN_SHARD], zeros)], axis=1)
        big[l] = _reduce_scatter([g_in, g["a"], g["b"], g["o"], g["gu"], g["d"]], RS_META, ci, myq)
        small[l] = _pack_small(g)
    tot = _allreduce_small(jnp.concatenate(small, axis=0))
    sm = [_unpack_small(tot[l * SMALL_ROWS:(l + 1) * SMALL_ROWS]) for l in range(DEPTH)]
    stack = lambda name: jnp.stack([s[name] for s in sm], axis=0)

    grads = dict(
        w_in=jnp.stack([big[l][0].reshape(D, IN_PAD)[:, :IN_SHARD] for l in range(DEPTH)]),
        conv_w=lax.dynamic_slice_in_dim(stack("conv"), myq * (3 * D // N_CHIPS), 3 * D // N_CHIPS, axis=2),
        a_log=stack("a_log"), dt_bias=stack("dt_bias"), dn_norm_w=stack("gnw"), sinks=stack("sinks"),
        w_branch_a=jnp.stack([big[l][1].reshape(D // N_CHIPS, D) for l in range(DEPTH)]),
        w_branch_b=jnp.stack([big[l][2].reshape(D // N_CHIPS, D) for l in range(DEPTH)]),
        w_out=jnp.stack([big[l][3].reshape(D // N_CHIPS, D) for l in range(DEPTH)]),
        ln1_g=stack("ln1g"), ln1_b=stack("ln1b"),
        w_gate_up=jnp.stack([big[l][4].reshape(D, 2 * FFN // N_CHIPS) for l in range(DEPTH)]),
        w_down=jnp.stack([big[l][5].reshape(FFN // N_CHIPS, D) for l in range(DEPTH)]),
        ln2_g=stack("ln2g"), ln2_b=stack("ln2b"))
    weights = dict(w_in=w_in, conv_w=conv_w, a_log=a_log, dt_bias=dt_bias, dn_norm_w=dn_norm_w, sinks=sinks,
                   w_branch_a=w_branch_a, w_branch_b=w_branch_b, w_out=w_out, ln1_g=ln1_g, ln1_b=ln1_b,
                   w_gate_up=w_gate_up, w_down=w_down, ln2_g=ln2_g, ln2_b=ln2_b)
    ms = dict(w_in=m_w_in, conv_w=m_conv_w, a_log=m_a_log, dt_bias=m_dt_bias, dn_norm_w=m_dn_norm_w, sinks=m_sinks,
              w_branch_a=m_w_branch_a, w_branch_b=m_w_branch_b, w_out=m_w_out, ln1_g=m_ln1_g, ln1_b=m_ln1_b,
              w_gate_up=m_w_gate_up, w_down=m_w_down, ln2_g=m_ln2_g, ln2_b=m_ln2_b)
    vs = dict(w_in=v_w_in, conv_w=v_conv_w, a_log=v_a_log, dt_bias=v_dt_bias, dn_norm_w=v_dn_norm_w, sinks=v_sinks,
              w_branch_a=v_w_branch_a, w_branch_b=v_w_branch_b, w_out=v_w_out, ln1_g=v_ln1_g, ln1_b=v_ln1_b,
              w_gate_up=v_w_gate_up, w_down=v_w_down, ln2_g=v_ln2_g, ln2_b=v_ln2_b)
    names = list(weights)
    upd = {n: _adamw("adamw_" + n, weights[n], grads[n], ms[n], vs[n]) for n in names}
    return (loss, dh[None], *[grads[n] for n in names], *[upd[n][0] for n in names], *[upd[n][1] for n in names],
            *[upd[n][2] for n in names])
```

```python
import functools

import jax
import jax.numpy as jnp
from jax import lax
from jax.experimental import pallas as pl
from jax.experimental.pallas import tpu as pltpu

F32 = jnp.float32
BF16 = jnp.bfloat16
_MXU = BF16

D = 1024
DEPTH = 4
DN_HEADS = 8
DN_DIM = 128
DN_CONV = 5
CHUNK = 64
SW_HEADS = 16
SW_KV = 4
SW_DIM = 64
SW_GRP = SW_HEADS // SW_KV
SW_BLOCK = 128
ROPE_THETA = 10000.0
FFN = 2816
ALPHA = (2.0 * DEPTH) ** 0.25
LN_EPS = 1e-5
RMS_EPS = 1e-6
IN_COLS = 7712
O_Z, O_GA, O_GB, O_QS, O_KS, O_VS, N_MAIN = 3072, 4096, 5120, 6144, 7168, 7424, 7680
R_BG, R_SW, R_GATES = 4096, 4128, 5664
N_CHIPS = 4
IN_SHARD = IN_COLS // N_CHIPS
IN_PAD = 2048
ADAM_LR, ADAM_B1, ADAM_B2, ADAM_EPS, ADAM_WD, ADAM_STEP = 0.001, 0.9, 0.999, 1e-08, 0.01, 10
VMEM_LIMIT = 52 * 1024 * 1024
MESH = pl.DeviceIdType.MESH
ANY = pl.BlockSpec(memory_space=pl.ANY)


def _params(n_grid, **kw):
    return pltpu.CompilerParams(dimension_semantics=("arbitrary",) * n_grid, vmem_limit_bytes=VMEM_LIMIT, **kw)


def _full(a):
    nd = a.ndim
    return pl.BlockSpec(a.shape, lambda *_, nd=nd: (0,) * nd)


def _raw_dot(a, b, ca, cb):
    return lax.dot_general(a.astype(_MXU), b.astype(_MXU), (((ca,), (cb,)), ((), ())), preferred_element_type=F32)


@jax.custom_vjp
def _nn(a, b):
    return _raw_dot(a, b, 1, 0)


@jax.custom_vjp
def _nt(a, b):
    return _raw_dot(a, b, 1, 1)


@jax.custom_vjp
def _tn(a, b):
    return _raw_dot(a, b, 0, 0)


_nn.defvjp(lambda a, b: (_nn(a, b), (a, b)), lambda r, g: (_nt(g, r[1]), _tn(r[0], g)))
_nt.defvjp(lambda a, b: (_nt(a, b), (a, b)), lambda r, g: (_nn(g, r[1]), _tn(g, r[0])))
_tn.defvjp(lambda a, b: (_tn(a, b), (a, b)), lambda r, g: (_nt(r[1], g), _nn(r[0], g)))


def _hdot(a, b, ca=1, cb=0):
    ah, bh = a.astype(BF16), b.astype(BF16)
    al, bl = (a - ah.astype(F32)).astype(BF16), (b - bh.astype(F32)).astype(BF16)
    dot = lambda u, v: lax.dot_general(u, v, (((ca,), (cb,)), ((), ())), preferred_element_type=F32)
    return dot(ah, bh) + (dot(ah, bl) + dot(al, bh))


def _inv_impl(mats):
    n = mats[0].shape[0]
    eye = (lax.broadcasted_iota(jnp.int32, (n, n), 0) == lax.broadcasted_iota(jnp.int32, (n, n), 1)).astype(F32)
    ps = [-a for a in mats]
    ts = [eye + p for p in ps]
    for _ in range(max(1, (n - 1).bit_length()) - 1):
        ps = [_hdot(p, p) for p in ps]
        ts = [t + _hdot(t, p) for t, p in zip(ts, ps)]
    return tuple(ts)


@jax.custom_vjp
def _inv(mats):
    return _inv_impl(mats)


def _inv_fwd(mats):
    ts = _inv_impl(mats)
    return ts, ts


def _inv_bwd(ts, gs):
    xs = [_hdot(t, g, 0, 0) for t, g in zip(ts, gs)]
    return (tuple(-_hdot(x, t, 1, 1) for x, t in zip(xs, ts)),)


_inv.defvjp(_inv_fwd, _inv_bwd)


def _tile(n, cap):
    if n <= cap:
        return n
    best = [t for t in range(128, cap + 1, 128) if n % t == 0]
    assert best, (n, cap)
    return best[-1]


def _mm(a, b, *, name, ta=False, tb=False, add=None, tm=1024, tn=1024, tk=1024):
    if ta:
        k_, m_ = a.shape
    else:
        m_, k_ = a.shape
    n_ = b.shape[0] if tb else b.shape[1]
    tm, tn, tk = _tile(m_, tm), _tile(n_, tn), _tile(k_, tk)
    nk = k_ // tk
    has_add = add is not None

    def body(*refs):
        a_ref, b_ref = refs[:2]
        add_ref = refs[2] if has_add else None
        o_ref = refs[3] if has_add else refs[2]
        part = _raw_dot(a_ref[...], b_ref[...], 0 if ta else 1, 1 if tb else 0)
        if nk == 1:
            o_ref[...] = part + add_ref[...] if has_add else part
            return
        acc = refs[-1]
        k = pl.program_id(2)

        @pl.when(k == 0)
        def _():
            acc[...] = part

        @pl.when(jnp.logical_and(k > 0, k < nk - 1))
        def _():
            acc[...] += part

        @pl.when(k == nk - 1)
        def _():
            o_ref[...] = acc[...] + part + add_ref[...] if has_add else acc[...] + part

    a_spec = pl.BlockSpec((tk, tm), lambda i, j, k: (k, i)) if ta else pl.BlockSpec((tm, tk), lambda i, j, k: (i, k))
    b_spec = pl.BlockSpec((tn, tk), lambda i, j, k: (j, k)) if tb else pl.BlockSpec((tk, tn), lambda i, j, k: (k, j))
    o_spec = pl.BlockSpec((tm, tn), lambda i, j, k: (i, j))
    in_specs = [a_spec, b_spec] + ([o_spec] if has_add else [])
    args = (a, b) + ((add,) if has_add else ())
    return pl.pallas_call(
        body, name=name, grid=(m_ // tm, n_ // tn, nk), in_specs=in_specs, out_specs=o_spec,
        out_shape=jax.ShapeDtypeStruct((m_, n_), F32),
        scratch_shapes=[pltpu.VMEM((tm, tn), F32)] if nk > 1 else [],
        compiler_params=_params(3))(*args)


def _row_spec(tb, w, c0, percol):
    return pl.BlockSpec((tb, w), lambda i, j, c0=c0, pc=percol: (i, c0 + (j if pc else 0)))


def _stage_fwd(f, name, rows, params, outs, tb, ncol=1):
    t_ = rows[0][0].shape[0]
    nr, npar = len(rows), len(params)

    def body(*refs):
        res = f(*[r[...].astype(F32) for r in refs[:nr + npar]])
        for o_ref, val in zip(refs[nr + npar:], res):
            o_ref[...] = val.astype(o_ref.dtype)

    return pl.pallas_call(
        body, name=name, grid=(t_ // tb, ncol),
        in_specs=[_row_spec(tb, w, c0, pc) for (_, w, c0, pc) in rows] + [_full(p) for p in params],
        out_specs=[pl.BlockSpec((tb, w), lambda i, j: (i, j)) for w, _ in outs],
        out_shape=[jax.ShapeDtypeStruct((t_, w * ncol), dt) for w, dt in outs],
        compiler_params=_params(2))(*[r[0] for r in rows], *params)


def _into(dest, tb, width, t_, ncol, dtype):
    if dest is None:
        return pl.BlockSpec((tb, width), lambda i, j: (i, j)), jax.ShapeDtypeStruct((t_, width * ncol), dtype), None
    buf, total, c0 = dest
    return (pl.BlockSpec((tb, width), lambda i, j, c0=c0: (i, c0 + j)), jax.ShapeDtypeStruct((t_, total), dtype), buf)


def _stage_bwd(f, name, rows, params, douts, tb, ncol=1, cat=None, dtypes=None, dest=None):
    t_ = rows[0][0].shape[0]
    nr, npar, nd = len(rows), len(params), len(douts)
    cat = cat if cat is not None else [[r] for r in range(nr)]
    dtypes = dtypes if dtypes is not None else [F32] * len(cat)
    dest = dest or {}
    assert ncol == 1 or all(len(g) == 1 and rows[g[0]][3] for g in cat)
    nin = nr + npar + nd

    def body(*refs):
        ins = [r[...].astype(F32) for r in refs[:nr + npar]]
        dvals = tuple(r[...].astype(F32) for r in refs[nr + npar:nin])
        out_refs = refs[nin + len(aliased):]
        _, vjp = jax.vjp(f, *ins)
        grads = vjp(dvals)
        for o_ref, grp in zip(out_refs[:len(cat)], cat):
            val = grads[grp[0]] if len(grp) == 1 else jnp.concatenate([grads[r] for r in grp], axis=-1)
            o_ref[...] = val.astype(o_ref.dtype)
        first = jnp.logical_and(pl.program_id(0) == 0, pl.program_id(1) == 0)
        for p_ref, gp in zip(out_refs[len(cat):], grads[nr:]):
            @pl.when(first)
            def _(p_ref=p_ref):
                p_ref[...] = jnp.zeros_like(p_ref)
            p_ref[...] += gp

    gw = [sum(rows[r][1] for r in grp) for grp in cat]
    out_specs, out_shape, aliased, aliases = [], [], [], {}
    for gi, (w, dt) in enumerate(zip(gw, dtypes)):
        spec, shape, buf = _into(dest.get(gi), tb, w, t_, ncol, dt)
        out_specs.append(spec)
        out_shape.append(shape)
        if buf is not None:
            aliases[nin + len(aliased)] = gi
            aliased.append(buf)
    return pl.pallas_call(
        body, name=name, grid=(t_ // tb, ncol),
        in_specs=[_row_spec(tb, w, c0, pc) for (_, w, c0, pc) in rows] + [_full(p) for p in params]
        + [pl.BlockSpec((tb, d.shape[1] // ncol), lambda i, j: (i, j)) for d in douts] + [ANY] * len(aliased),
        out_specs=out_specs + [_full(p) for p in params],
        out_shape=out_shape + [jax.ShapeDtypeStruct(p.shape, F32) for p in params],
        input_output_aliases=aliases,
        compiler_params=_params(2))(*[r[0] for r in rows], *params, *douts, *aliased)


def _ln_f(x, y, g, b):
    u = ALPHA * x + y
    c = u - jnp.mean(u, axis=-1, keepdims=True)
    var = jnp.mean(c * c, axis=-1, keepdims=True)
    return (c * lax.rsqrt(var + LN_EPS) * g + b,)


def _ln_f2(x, y, g, b):
    out, = _ln_f(x, y, g, b)
    return out, out


def _swiglu_f(gate, up):
    return (jax.nn.silu(gate) * up,)


def _merge_f(ya, yb, ga, gb):
    return (jax.nn.sigmoid(ga) * ya + jax.nn.sigmoid(gb) * yb,)


def _gnorm_f(of, ob, z, w):
    o = of + ob
    return (o * lax.rsqrt(jnp.mean(o * o, axis=-1, keepdims=True) + RMS_EPS) * w * jax.nn.silu(z),)


def _bg_f(x, arow, dtrow):
    lane = lax.broadcasted_iota(jnp.int32, x.shape, 1)
    beta = jax.nn.sigmoid(x)
    g = -jnp.exp(arow) * jax.nn.softplus(x + dtrow)
    return (jnp.where(lane < 16, beta, jnp.where(lane < 32, g, 0.0)),)


PREP_ROWS = 512
PAD = 8


def _prep_f(part, w, *wins):
    xc = wins[0] * w[0:1, :]
    for k in range(1, DN_CONV):
        xc = xc + wins[k] * w[k:k + 1, :]
    a = jax.nn.silu(xc)
    nrm = a * lax.rsqrt(jnp.sum(a * a, axis=-1, keepdims=True) + RMS_EPS)
    return jnp.where(part == 0, nrm * (DN_DIM ** -0.5), jnp.where(part == 1, nrm, a))


def _windows(pad_ref, r0, rows):
    return [pad_ref[PAD + r0 - 2 + k:PAD + r0 - 2 + k + rows, :] for k in range(DN_CONV)]


def _prep_fwd(pm, conv):
    t_ = pm.shape[0]
    rows = min(PREP_ROWS, t_)

    def body(x_ref, w_ref, o_ref, pad_ref):
        part = pl.program_id(0) // DN_HEADS
        pad_ref[0:PAD, :] = jnp.zeros((PAD, DN_DIM), F32)
        pad_ref[PAD + t_:2 * PAD + t_, :] = jnp.zeros((PAD, DN_DIM), F32)
        pad_ref[PAD:PAD + t_, :] = x_ref[...]
        w = w_ref[...]
        for r in range(t_ // rows):
            o_ref[r * rows:(r + 1) * rows, :] = _prep_f(part, w, *_windows(pad_ref, r * rows, rows))

    ncb = 3 * DN_HEADS
    return pl.pallas_call(
        body, name="prep_fwd", grid=(ncb,),
        in_specs=[pl.BlockSpec((t_, DN_DIM), lambda j: (0, j)), pl.BlockSpec((DN_CONV, DN_DIM), lambda j: (0, j))],
        out_specs=pl.BlockSpec((t_, DN_DIM), lambda j: (0, j)),
        out_shape=jax.ShapeDtypeStruct((t_, ncb * DN_DIM), F32),
        scratch_shapes=[pltpu.VMEM((t_ + 2 * PAD, DN_DIM), F32)],
        compiler_params=_params(1))(pm, conv)


def _prep_bwd(pm, conv, dout, dpm):
    t_ = pm.shape[0]
    rows = min(PREP_ROWS, t_)

    def body(x_ref, w_ref, d_ref, _, dx_ref, dw_ref, pad_ref, dpad_ref):
        part = pl.program_id(0) // DN_HEADS
        pad_ref[0:PAD, :] = jnp.zeros((PAD, DN_DIM), F32)
        pad_ref[PAD + t_:2 * PAD + t_, :] = jnp.zeros((PAD, DN_DIM), F32)
        pad_ref[PAD:PAD + t_, :] = x_ref[...]
        dpad_ref[...] = jnp.zeros_like(dpad_ref)
        w = w_ref[...]
        dw = jnp.zeros((DN_CONV, DN_DIM), F32)
        for r in range(t_ // rows):
            r0 = r * rows
            _, vjp = jax.vjp(functools.partial(_prep_f, part), w, *_windows(pad_ref, r0, rows))
            grads = vjp(d_ref[r0:r0 + rows, :])
            dw = dw + grads[0]
            for k in range(DN_CONV):
                lo = PAD + r0 - 2 + k
                dpad_ref[lo:lo + rows, :] += grads[1 + k]
        dx_ref[...] = dpad_ref[PAD:PAD + t_, :].astype(dx_ref.dtype)
        dw_ref[...] = dw

    ncb = 3 * DN_HEADS
    col = pl.BlockSpec((t_, DN_DIM), lambda j: (0, j))
    wsp = pl.BlockSpec((DN_CONV, DN_DIM), lambda j: (0, j))
    return pl.pallas_call(
        body, name="prep_bwd", grid=(ncb,), in_specs=[col, wsp, col, ANY], out_specs=[col, wsp],
        out_shape=[jax.ShapeDtypeStruct(dpm.shape, dpm.dtype), jax.ShapeDtypeStruct((DN_CONV, ncb * DN_DIM), F32)],
        scratch_shapes=[pltpu.VMEM((t_ + 2 * PAD, DN_DIM), F32), pltpu.VMEM((t_ + 2 * PAD, DN_DIM), F32)],
        input_output_aliases={3: 0}, compiler_params=_params(1))(pm, conv, dout, dpm)


def _dn_chunk(sgns, qs, ks, vs, grows, brows):
    c = qs[0].shape[0]
    i = lax.broadcasted_iota(jnp.int32, (c, c), 0)
    j = lax.broadcasted_iota(jnp.int32, (c, c), 1)
    eye = i == j
    incl = {s: (i - j) * int(s) >= 0 for s in set(sgns)}
    strict = {s: (i - j) * int(s) > 0 for s in set(sgns)}
    gcs = [jnp.sum(jnp.where(incl[s], g, 0.0), axis=1, keepdims=True) for s, g in zip(sgns, grows)]
    grs = [jnp.sum(jnp.where(eye, gc, 0.0), axis=0, keepdims=True) for gc in gcs]
    bcs = [jnp.sum(jnp.where(eye, b, 0.0), axis=1, keepdims=True) for b in brows]
    gls = [jnp.sum(g, axis=1, keepdims=True) for g in grows]
    decs = [jnp.exp(jnp.where(incl[s], gc - gr, -1e30)) for s, gc, gr in zip(sgns, gcs, grs)]
    kks = [_nt(k, k) for k in ks]
    tinvs = _inv(tuple(jnp.where(strict[s], bc * kk * dec, 0.0) for s, bc, kk, dec in zip(sgns, bcs, kks, decs)))
    egcs = [jnp.exp(gc) for gc in gcs]
    us = [_nn(t, v * bc) for t, v, bc in zip(tinvs, vs, bcs)]
    ws = [_nn(t, k * (bc * egc)) for t, k, bc, egc in zip(tinvs, ks, bcs, egcs)]
    qks = [_nt(q, k) * dec for q, k, dec in zip(qs, ks, decs)]
    qds = [q * egc for q, egc in zip(qs, egcs)]
    kds = [k * jnp.exp(gl - gc) for k, gl, gc in zip(ks, gls, gcs)]
    return tuple(us), tuple(ws), tuple(qks), tuple(qds), tuple(kds)


def _dn_step(us, ws, qks, qds, kds, grows, ss):
    gls = [jnp.exp(jnp.sum(g, axis=1, keepdims=True)) for g in grows]
    wss = [_nn(w, s) for w, s in zip(ws, ss)]
    qss = [_nn(qd, s) for qd, s in zip(qds, ss)]
    vns = [u - x for u, x in zip(us, wss)]
    os_ = [a + _nn(qk, vn) for a, qk, vn in zip(qss, qks, vns)]
    s2s = [s * gl + _tn(kd, vn) for s, gl, kd, vn in zip(ss, gls, kds, vns)]
    return tuple(os_), tuple(s2s)


def _hs(h):
    return slice(h * DN_DIM, (h + 1) * DN_DIM)


_DIR_SGN = (1, -1)
_A_OUT = 5
_PROBLEMS = [(d, h) for d in range(2) for h in range(DN_HEADS)]
_SGNS = [_DIR_SGN[d] for d, _ in _PROBLEMS]


def _chunk_inputs(q_ref, k_ref, v_ref, g_ref, b_ref):
    heads = lambda ref: tuple(ref[:, _hs(h)].astype(F32) for _, h in _PROBLEMS)
    rows = lambda ref: tuple(ref[d, 0, h:h + 1, :] for d, h in _PROBLEMS)
    return heads(q_ref), heads(k_ref), heads(v_ref), rows(g_ref), rows(b_ref)


def _dn_a_fwd(qkv, grows, brows):
    t_ = qkv.shape[0]
    nch = t_ // CHUNK

    def body(q_ref, k_ref, v_ref, g_ref, b_ref, *outs):
        us, ws, qks, qds, kds = _dn_chunk(_SGNS, *_chunk_inputs(q_ref, k_ref, v_ref, g_ref, b_ref))
        for p, (d, h) in enumerate(_PROBLEMS):
            u_ref, w_ref, qk_ref, qd_ref, kd_ref = outs[d * _A_OUT:(d + 1) * _A_OUT]
            u_ref[:, _hs(h)], w_ref[:, _hs(h)], qk_ref[0, h] = us[p], ws[p].astype(_MXU), qks[p].astype(_MXU)
            qd_ref[:, _hs(h)], kd_ref[:, _hs(h)] = qds[p].astype(_MXU), kds[p].astype(_MXU)

    rspec = pl.BlockSpec((2, 1, DN_HEADS, CHUNK), lambda c: (0, c, 0, 0))
    big = pl.BlockSpec((CHUNK, D), lambda c: (c, 0))
    qks = pl.BlockSpec((1, DN_HEADS, CHUNK, CHUNK), lambda c: (c, 0, 0, 0))
    bigs = lambda dt: jax.ShapeDtypeStruct((t_, D), dt)
    qksh = jax.ShapeDtypeStruct((nch, DN_HEADS, CHUNK, CHUNK), _MXU)
    return pl.pallas_call(
        body, name="dn_a_fwd", grid=(nch,),
        in_specs=[pl.BlockSpec((CHUNK, D), lambda c, p=p: (c, p)) for p in range(3)] + [rspec, rspec],
        out_specs=[big, big, qks, big, big] * 2, out_shape=[bigs(F32), bigs(_MXU), qksh, bigs(_MXU), bigs(_MXU)] * 2,
        compiler_params=_params(1))(qkv, qkv, qkv, grows, brows)


def _dn_a_bwd(qkv, grows, brows, dres, dg_b):
    t_ = qkv.shape[0]
    nch = t_ // CHUNK

    def body(q_ref, k_ref, v_ref, g_ref, b_ref, *rest):
        dins, dgb_ref, (dqkv_ref, dg_ref, db_ref) = rest[:2 * _A_OUT], rest[2 * _A_OUT], rest[2 * _A_OUT + 1:]
        _, vjp = jax.vjp(functools.partial(_dn_chunk, _SGNS), *_chunk_inputs(q_ref, k_ref, v_ref, g_ref, b_ref))
        cots = []
        for o in range(_A_OUT):
            cots.append(tuple(dins[d * _A_OUT + o][0, h] if o == 2 else dins[d * _A_OUT + o][:, _hs(h)]
                              for d, h in _PROBLEMS))
        gq, gk, gv, gg, gb = vjp(tuple(cots))
        for p, (d, h) in enumerate(_PROBLEMS):
            dg_ref[d, 0, h:h + 1, :] = gg[p] + dgb_ref[d, 0, h:h + 1, :]
            db_ref[d, 0, h:h + 1, :] = gb[p]
        for h in range(DN_HEADS):
            dqkv_ref[:, _hs(h)] = gq[h] + gq[DN_HEADS + h]
            dqkv_ref[:, _hs(DN_HEADS + h)] = gk[h] + gk[DN_HEADS + h]
            dqkv_ref[:, _hs(2 * DN_HEADS + h)] = gv[h] + gv[DN_HEADS + h]

    rspec = pl.BlockSpec((2, 1, DN_HEADS, CHUNK), lambda c: (0, c, 0, 0))
    big = pl.BlockSpec((CHUNK, D), lambda c: (c, 0))
    qks = pl.BlockSpec((1, DN_HEADS, CHUNK, CHUNK), lambda c: (c, 0, 0, 0))
    rsh = jax.ShapeDtypeStruct(grows.shape, F32)
    return pl.pallas_call(
        body, name="dn_a_bwd", grid=(nch,),
        in_specs=[pl.BlockSpec((CHUNK, D), lambda c, p=p: (c, p)) for p in range(3)] + [rspec, rspec]
        + [big, big, qks, big, big] * 2 + [rspec],
        out_specs=[pl.BlockSpec((CHUNK, 3 * D), lambda c: (c, 0)), rspec, rspec],
        out_shape=[jax.ShapeDtypeStruct((t_, 3 * D), F32), rsh, rsh],
        compiler_params=_params(1))(qkv, qkv, qkv, grows, brows, *dres, dg_b)


def _dir_specs(nch):
    def cidx(d):
        return (lambda n: n) if d == 0 else (lambda n: nch - 1 - n)
    out = []
    for d in range(2):
        ci = cidx(d)
        big = pl.BlockSpec((CHUNK, D), lambda n, ci=ci: (ci(n), 0))
        qks = pl.BlockSpec((1, DN_HEADS, CHUNK, CHUNK), lambda n, ci=ci: (ci(n), 0, 0, 0))
        row = pl.BlockSpec((1, 1, DN_HEADS, CHUNK), lambda n, ci=ci, d=d: (d, ci(n), 0, 0))
        st = pl.BlockSpec((1, DN_HEADS, DN_DIM, DN_DIM), lambda n, ci=ci: (ci(n), 0, 0, 0))
        out.append(dict(big=big, qk=qks, row=row, st=st))
    return out


def _step_inputs(ins, per_dir):
    def pick(o):
        if o == 2:
            return tuple(ins[d * per_dir + o][0, h].astype(F32) for d, h in _PROBLEMS)
        if o == 5:
            return tuple(ins[d * per_dir + o][0, 0, h:h + 1, :] for d, h in _PROBLEMS)
        return tuple(ins[d * per_dir + o][:, _hs(h)].astype(F32) for d, h in _PROBLEMS)
    return [pick(o) for o in range(6)]


def _dn_b_fwd(ares, grows):
    t_ = ares[0].shape[0]
    nch = t_ // CHUNK
    sp = _dir_specs(nch)

    def body(*refs):
        ins, outs, s_ref = refs[:12], refs[12:16], refs[16]

        @pl.when(pl.program_id(0) == 0)
        def _():
            s_ref[...] = jnp.zeros_like(s_ref)

        ss = tuple(s_ref[p] for p in range(len(_PROBLEMS)))
        os_, s2s = _dn_step(*_step_inputs(ins, 6), ss)
        for p, (d, h) in enumerate(_PROBLEMS):
            outs[2 + d][0, h] = ss[p]
            outs[d][:, _hs(h)] = os_[p]
            s_ref[p] = s2s[p]

    in_specs, args = [], []
    for d in range(2):
        in_specs += [sp[d]["big"], sp[d]["big"], sp[d]["qk"], sp[d]["big"], sp[d]["big"], sp[d]["row"]]
        args += list(ares[d * _A_OUT:(d + 1) * _A_OUT]) + [grows]
    stsh = jax.ShapeDtypeStruct((nch, DN_HEADS, DN_DIM, DN_DIM), F32)
    osh = jax.ShapeDtypeStruct((t_, D), F32)
    return pl.pallas_call(
        body, name="dn_b_fwd", grid=(nch,), in_specs=in_specs,
        out_specs=[sp[0]["big"], sp[1]["big"], sp[0]["st"], sp[1]["st"]], out_shape=[osh, osh, stsh, stsh],
        scratch_shapes=[pltpu.VMEM((2 * DN_HEADS, DN_DIM, DN_DIM), F32)],
        compiler_params=_params(1))(*args)


def _dn_b_bwd(ares, grows, st_f, st_b, do):
    t_ = ares[0].shape[0]
    nch = t_ // CHUNK
    sp = _dir_specs(nch)
    rsp = [sp[1], sp[0]]

    def body(*refs):
        ins, outs, ds_ref = refs[:16], refs[16:28], refs[28]

        @pl.when(pl.program_id(0) == 0)
        def _():
            ds_ref[...] = jnp.zeros_like(ds_ref)

        ss = tuple(ins[d * 8 + 6][0, h] for d, h in _PROBLEMS)
        _, vjp = jax.vjp(_dn_step, *_step_inputs(ins, 8), ss)
        dos = tuple(ins[d * 8 + 7][:, _hs(h)] for d, h in _PROBLEMS)
        grads = vjp((dos, tuple(ds_ref[p] for p in range(len(_PROBLEMS)))))
        for p, (d, h) in enumerate(_PROBLEMS):
            du_ref, dw_ref, dqk_ref, dqd_ref, dkd_ref, dg_ref = outs[d * 6:(d + 1) * 6]
            du_ref[:, _hs(h)], dw_ref[:, _hs(h)], dqk_ref[0, h] = grads[0][p], grads[1][p], grads[2][p]
            dqd_ref[:, _hs(h)], dkd_ref[:, _hs(h)] = grads[3][p], grads[4][p]
            dg_ref[0, 0, h:h + 1, :] = grads[5][p]
            ds_ref[p] = grads[6][p]

    in_specs, args, out_specs, out_shape = [], [], [], []
    big_sh = jax.ShapeDtypeStruct((t_, D), F32)
    qk_sh = jax.ShapeDtypeStruct((nch, DN_HEADS, CHUNK, CHUNK), F32)
    row_sh = jax.ShapeDtypeStruct((1, nch, DN_HEADS, CHUNK), F32)
    for d in range(2):
        s = rsp[d]
        row0 = pl.BlockSpec((1, 1, DN_HEADS, CHUNK), lambda m, d=d: (0, (nch - 1 - m) if d == 0 else m, 0, 0))
        rowd = pl.BlockSpec((1, 1, DN_HEADS, CHUNK), lambda m, d=d: (d, (nch - 1 - m) if d == 0 else m, 0, 0))
        in_specs += [s["big"], s["big"], s["qk"], s["big"], s["big"], rowd, s["st"], s["big"]]
        args += list(ares[d * _A_OUT:(d + 1) * _A_OUT]) + [grows, (st_f, st_b)[d], do]
        out_specs += [s["big"], s["big"], s["qk"], s["big"], s["big"], row0]
        out_shape += [big_sh, big_sh, qk_sh, big_sh, big_sh, row_sh]
    res = pl.pallas_call(
        body, name="dn_b_bwd", grid=(nch,), in_specs=in_specs, out_specs=out_specs, out_shape=out_shape,
        scratch_shapes=[pltpu.VMEM((2 * DN_HEADS, DN_DIM, DN_DIM), F32)],
        compiler_params=_params(1))(*args)
    dares = list(res[0:5]) + list(res[6:11])
    return dares, jnp.concatenate([res[5], res[11]], axis=0)


def _rope(x, c, s):
    half = SW_DIM // 2
    return x * c + jnp.concatenate([-x[:, half:], x[:, :half]], axis=-1) * s


def _attn_f(blk, t_, cq, sq, ck, sk, q, kp, ko, kn, vp, vo, vn, sinks):
    kall = jnp.concatenate([kp, ko, kn], axis=0)
    vall = jnp.concatenate([vp, vo, vn], axis=0)
    nq, nk = SW_GRP * SW_BLOCK, 3 * SW_BLOCK
    qpos = lax.broadcasted_iota(jnp.int32, (nq, nk), 0) % SW_BLOCK
    krel = lax.broadcasted_iota(jnp.int32, (nq, nk), 1) - SW_BLOCK
    kglob = krel + blk * SW_BLOCK
    valid = (jnp.abs(qpos - krel) <= SW_BLOCK) & (kglob >= 0) & (kglob < t_)
    kvs = range(SW_KV)
    heads = [[kvh * SW_GRP + g for g in range(SW_GRP)] for kvh in kvs]
    khs = [_rope(kall[:, kvh * SW_DIM:(kvh + 1) * SW_DIM], ck, sk) for kvh in kvs]
    vhs = [vall[:, kvh * SW_DIM:(kvh + 1) * SW_DIM] for kvh in kvs]
    qgs = [jnp.concatenate([_rope(q[:, h * SW_DIM:(h + 1) * SW_DIM], cq, sq) for h in hs], axis=0) for hs in heads]
    ss = [jnp.where(valid, _nt(qg, kh) * (SW_DIM ** -0.5), -1e30) for qg, kh in zip(qgs, khs)]
    snks = [jnp.concatenate([jnp.broadcast_to(sinks[:, h:h + 1], (SW_BLOCK, 1)) for h in hs], axis=0) for hs in heads]
    ms = [lax.stop_gradient(jnp.maximum(jnp.max(s, axis=-1, keepdims=True), snk)) for s, snk in zip(ss, snks)]
    es = [jnp.exp(s - m) for s, m in zip(ss, ms)]
    ps = [e / (jnp.sum(e, axis=-1, keepdims=True) + jnp.exp(snk - m)) for e, snk, m in zip(es, snks, ms)]
    ogs = [_nn(p, vh) for p, vh in zip(ps, vhs)]
    return jnp.concatenate([og[g * SW_BLOCK:(g + 1) * SW_BLOCK] for og in ogs for g in range(SW_GRP)], axis=-1)


def _attn_specs(nb):
    prv = lambda i: jnp.maximum(i - 1, 0)
    nxt = lambda i: jnp.minimum(i + 1, nb - 1)
    rows = [lambda i: i, prv, lambda i: i, nxt]
    tab = [pl.BlockSpec((SW_BLOCK, SW_DIM), lambda i, r=r: (r(i), 0)) for r in rows]
    qs = pl.BlockSpec((SW_BLOCK, SW_HEADS * SW_DIM), lambda i: (i, O_QS // (SW_HEADS * SW_DIM)))
    kw = SW_KV * SW_DIM
    ks = [pl.BlockSpec((SW_BLOCK, kw), lambda i, r=r: (r(i), O_KS // kw)) for r in rows[1:]]
    vs = [pl.BlockSpec((SW_BLOCK, kw), lambda i, r=r: (r(i), O_VS // kw)) for r in rows[1:]]
    return tab, qs, ks, vs


def _attn_tables(refs):
    cq, cp, co, cn, sq, sp_, so, sn = [r[...] for r in refs]
    return cq, sq, jnp.concatenate([cp, co, cn], axis=0), jnp.concatenate([sp_, so, sn], axis=0)


def _attn_fwd(pm, cos, sin, sinks):
    t_ = pm.shape[0]
    nb = t_ // SW_BLOCK
    tab, qs, ks, vs = _attn_specs(nb)

    def body(*refs):
        tabs = _attn_tables(refs[:8])
        vals = [r[...] for r in refs[8:16]]
        refs[16][...] = _attn_f(pl.program_id(0), t_, *tabs, *vals).astype(refs[16].dtype)

    return pl.pallas_call(
        body, name="attn_fwd", grid=(nb,), in_specs=tab + tab + [qs] + ks + vs + [_full(sinks)],
        out_specs=pl.BlockSpec((SW_BLOCK, D), lambda i: (i, 0)), out_shape=jax.ShapeDtypeStruct((t_, D), _MXU),
        compiler_params=_params(1))(*([cos] * 4), *([sin] * 4), pm, pm, pm, pm, pm, pm, pm, sinks)


def _attn_bwd(pm, cos, sin, sinks, do, dpm):
    t_ = pm.shape[0]
    nb = t_ // SW_BLOCK
    tab, qs, ks, vs = _attn_specs(nb)
    kw = SW_KV * SW_DIM

    def body(*refs):
        tabs = _attn_tables(refs[:8])
        vals = [r[...] for r in refs[8:16]]
        do_ref, outs = refs[16], refs[18:]
        _, vjp = jax.vjp(functools.partial(_attn_f, pl.program_id(0), t_, *tabs), *vals)
        grads = vjp(do_ref[...])
        for o_ref, g in zip(outs[:7], grads[:7]):
            o_ref[...] = g.astype(o_ref.dtype)

        @pl.when(pl.program_id(0) == 0)
        def _():
            outs[7][...] = jnp.zeros_like(outs[7])
        outs[7][...] += grads[7]

    own = lambda w: pl.BlockSpec((SW_BLOCK, w), lambda i: (i, 0))
    return pl.pallas_call(
        body, name="attn_bwd", grid=(nb,),
        in_specs=tab + tab + [qs] + ks + vs + [_full(sinks), own(D), ANY],
        out_specs=[pl.BlockSpec((SW_BLOCK, D), lambda i: (i, O_QS // D))] + [own(kw)] * 6 + [_full(sinks)],
        out_shape=[jax.ShapeDtypeStruct(dpm.shape, dpm.dtype)] + [jax.ShapeDtypeStruct((t_, kw), F32)] * 6
        + [jax.ShapeDtypeStruct(sinks.shape, F32)],
        input_output_aliases={17: 0},
        compiler_params=_params(1))(*([cos] * 4), *([sin] * 4), pm, pm, pm, pm, pm, pm, pm, sinks, do, dpm)


def _band_sum(kparts, vparts, dpm):
    t_, kw = kparts[1].shape
    nb = t_ // SW_BLOCK

    def body(kp, ko, kn, vp, vo, vn, _, out_ref):
        j = pl.program_id(0)
        band = lambda p, o, n: o[...] + jnp.where(j + 1 < nb, p[...], 0.0) + jnp.where(j > 0, n[...], 0.0)
        out_ref[...] = jnp.concatenate([band(kp, ko, kn), band(vp, vo, vn)], axis=-1).astype(out_ref.dtype)

    specs = [pl.BlockSpec((SW_BLOCK, kw), lambda j: (jnp.minimum(j + 1, nb - 1), 0)),
             pl.BlockSpec((SW_BLOCK, kw), lambda j: (j, 0)),
             pl.BlockSpec((SW_BLOCK, kw), lambda j: (jnp.maximum(j - 1, 0), 0))]
    return pl.pallas_call(
        body, name="band_sum", grid=(nb,), in_specs=specs * 2 + [ANY],
        out_specs=pl.BlockSpec((SW_BLOCK, 2 * kw), lambda j: (j, O_KS // (2 * kw))),
        out_shape=jax.ShapeDtypeStruct(dpm.shape, dpm.dtype), input_output_aliases={6: 0},
        compiler_params=_params(1))(*kparts, *vparts, dpm)


def _loss_head(y, target, tb=256):
    t_ = y.shape[0]

    def body(y_ref, t_ref, dy_ref, acc_ref):
        err = y_ref[...] - t_ref[...]
        dy_ref[...] = err * (1.0 / D)
        sq = (err * err).reshape(tb // 8, 8, D).sum(axis=0)
        part = sq[:, 0:128]
        for c in range(1, D // 128):
            part = part + sq[:, c * 128:(c + 1) * 128]

        @pl.when(pl.program_id(0) == 0)
        def _():
            acc_ref[...] = jnp.zeros_like(acc_ref)
        acc_ref[...] += part

    row = pl.BlockSpec((tb, D), lambda i: (i, 0))
    return pl.pallas_call(
        body, name="loss_head", grid=(t_ // tb,), in_specs=[row, row],
        out_specs=[row, pl.BlockSpec((8, 128), lambda i: (0, 0))],
        out_shape=[jax.ShapeDtypeStruct((t_, D), F32), jax.ShapeDtypeStruct((8, 128), F32)],
        compiler_params=_params(1))(y, target)


def _adamw(name, w, g, m, v):
    shape = w.shape
    cols = shape[-1]
    rows = w.size // cols
    w2, g2, m2, v2 = [a.reshape(rows, cols) for a in (w, g, m, v)]
    tb = rows
    while tb * cols * 4 > (1 << 20) and tb % 16 == 0:
        tb //= 2
    bc1 = 1.0 - ADAM_B1 ** ADAM_STEP
    bc2 = 1.0 - ADAM_B2 ** ADAM_STEP

    def body(w_ref, g_ref, m_ref, v_ref, d_ref, nm_ref, nv_ref):
        gv = g_ref[...]
        nm = ADAM_B1 * m_ref[...] + (1.0 - ADAM_B1) * gv
        nv = ADAM_B2 * v_ref[...] + (1.0 - ADAM_B2) * (gv * gv)
        d_ref[...] = -ADAM_LR * ((nm / bc1) / (jnp.sqrt(nv / bc2) + ADAM_EPS) + ADAM_WD * w_ref[...])
        nm_ref[...] = nm
        nv_ref[...] = nv

    spec = pl.BlockSpec((tb, cols), lambda i: (i, 0))
    sh = jax.ShapeDtypeStruct((rows, cols), F32)
    outs = pl.pallas_call(body, name=name, grid=(rows // tb,), in_specs=[spec] * 4, out_specs=[spec] * 3,
                          out_shape=[sh] * 3, compiler_params=_params(1))(w2, g2, m2, v2)
    return [o.reshape(shape) for o in outs]


def _to_rows(bg, col0):
    t_ = bg.shape[0]
    a = bg[:, col0:col0 + 2 * DN_HEADS].reshape(t_ // CHUNK, CHUNK, 2, DN_HEADS)
    return jnp.transpose(a, (2, 0, 3, 1))


def _from_rows(db, dg):
    nch = db.shape[1]
    back = lambda a: jnp.transpose(a, (1, 3, 0, 2)).reshape(nch * CHUNK, 2 * DN_HEADS)
    return jnp.pad(jnp.concatenate([back(db), back(dg)], axis=1), ((0, 0), (0, 128 - 4 * DN_HEADS)))


def _layer_fwd(x, xm, w, cos, sin):
    t_ = x.shape[0]
    tb, nb = min(256, t_), min(1024, t_)
    pm = _mm(xm, w["in_main"], name="mm_in", tn=1536)
    pbg = _mm(xm, w["in_bg"], name="mm_in_bg")
    qkv = _prep_fwd(pm, w["conv"])
    bg, = _stage_fwd(_bg_f, "bg_fwd", [(pbg, 128, 0, False)], [w["arow"], w["dtrow"]], [(128, F32)], nb)
    brows, grows = _to_rows(bg, 0), _to_rows(bg, 2 * DN_HEADS)
    ares = _dn_a_fwd(qkv, grows, brows)
    o_f, o_b, st_f, st_b = _dn_b_fwd(ares, grows)
    odn, = _stage_fwd(_gnorm_f, "gnorm_fwd", [(o_f, 128, 0, True), (o_b, 128, 0, True), (pm, 128, O_Z // 128, True)],
                      [w["gnw"]], [(128, _MXU)], nb, ncol=DN_HEADS)
    osw = _attn_fwd(pm, cos, sin, w["sinks"])
    ya = _mm(odn, w["a"], name="mm_a")
    yb = _mm(osw, w["b"], name="mm_b")
    gates = [(pm, 512, O_GA // 512, True), (pm, 512, O_GB // 512, True)]
    merged, = _stage_fwd(_merge_f, "merge_fwd", [(ya, 512, 0, True), (yb, 512, 0, True)] + gates, [], [(512, _MXU)],
                         tb, ncol=2)
    mix = _mm(merged, w["o"], name="mm_o")
    x1, x1m = _stage_fwd(_ln_f2, "ln1_fwd", [(x, D, 0, False), (mix, D, 0, False)], [w["ln1g"], w["ln1b"]],
                         [(D, F32), (D, _MXU)], tb)
    gu = _mm(x1m, w["gu"], name="mm_gu", tn=FFN // 2)
    hid, = _stage_fwd(_swiglu_f, "swiglu_fwd", [(gu, FFN // 2, 0, True), (gu, FFN // 2, 2, True)], [],
                      [(FFN // 2, _MXU)], tb, ncol=2)
    ffn = _mm(hid, w["d"], name="mm_d", tk=FFN // 2)
    x2, x2m = _stage_fwd(_ln_f2, "ln2_fwd", [(x1, D, 0, False), (ffn, D, 0, False)], [w["ln2g"], w["ln2b"]],
                         [(D, F32), (D, _MXU)], tb)
    res = dict(x=x, xm=xm, pm=pm, pbg=pbg, qkv=qkv, grows=grows, brows=brows, ares=ares, o_f=o_f, o_b=o_b, st_f=st_f,
               st_b=st_b, odn=odn, osw=osw, ya=ya, yb=yb, merged=merged, mix=mix, x1=x1, x1m=x1m, gu=gu, hid=hid,
               ffn=ffn)
    return x2, x2m, res


def _layer_bwd(dx2, r, w, cos, sin):
    t_ = dx2.shape[0]
    tb, sb, nb = min(256, t_), min(128, t_), min(1024, t_)
    pm = r["pm"]
    g = {}
    dx1a, dffn, g["ln2g"], g["ln2b"] = _stage_bwd(
        _ln_f, "ln2_bwd", [(r["x1"], D, 0, False), (r["ffn"], D, 0, False)], [w["ln2g"], w["ln2b"]], [dx2], tb,
        dtypes=[F32, _MXU])
    dhid = _mm(dffn, w["d"], tb=True, name="mm_d_dx", tn=FFN // 2)
    g["d"] = _mm(r["hid"], dffn, ta=True, name="mm_d_dw", tm=FFN // 2)
    dgu, = _stage_bwd(_swiglu_f, "swiglu_bwd", [(r["gu"], FFN, 0, False), (r["gu"], FFN, 1, False)], [], [dhid], sb,
                      cat=[[0, 1]], dtypes=[_MXU])
    dx1 = _mm(dgu, w["gu"], tb=True, add=dx1a, name="mm_gu_dx", tk=FFN // 2)
    g["gu"] = _mm(r["x1m"], dgu, ta=True, name="mm_gu_dw", tn=FFN // 2)
    dxa, dmix, g["ln1g"], g["ln1b"] = _stage_bwd(
        _ln_f, "ln1_bwd", [(r["x"], D, 0, False), (r["mix"], D, 0, False)], [w["ln1g"], w["ln1b"]], [dx1], tb,
        dtypes=[F32, _MXU])
    dmerged = _mm(dmix, w["o"], tb=True, name="mm_o_dx")
    g["o"] = _mm(r["merged"], dmix, ta=True, name="mm_o_dw")
    dya, dyb, dpm = _stage_bwd(
        _merge_f, "merge_bwd", [(r["ya"], D, 0, False), (r["yb"], D, 0, False), (pm, D, O_GA // D, False),
                                (pm, D, O_GB // D, False)], [], [dmerged], tb, cat=[[0], [1], [2, 3]],
        dtypes=[_MXU, _MXU, _MXU], dest={2: (None, N_MAIN, O_GA // (2 * D))})
    dodn = _mm(dya, w["a"], tb=True, name="mm_a_dx")
    g["a"] = _mm(r["odn"], dya, ta=True, name="mm_a_dw")
    dosw = _mm(dyb, w["b"], tb=True, name="mm_b_dx")
    g["b"] = _mm(r["osw"], dyb, ta=True, name="mm_b_dw")
    ab = _attn_bwd(pm, cos, sin, w["sinks"], dosw, dpm)
    dpm, g["sinks"] = ab[0], ab[7]
    dpm = _band_sum(ab[1:4], ab[4:7], dpm)
    dof, _, dpm, g["gnw"] = _stage_bwd(
        _gnorm_f, "gnorm_bwd", [(r["o_f"], 128, 0, True), (r["o_b"], 128, 0, True), (pm, 128, O_Z // 128, True)],
        [w["gnw"]], [dodn], nb, ncol=DN_HEADS, dtypes=[F32, F32, _MXU], dest={2: (dpm, N_MAIN, O_Z // 128)})
    dares, dg_b = _dn_b_bwd(r["ares"], r["grows"], r["st_f"], r["st_b"], dof)
    dqkv, dgrows, dbrows = _dn_a_bwd(r["qkv"], r["grows"], r["brows"], dares, dg_b)
    dbg = _from_rows(dbrows, dgrows)
    dpbg, g["arow"], g["dtrow"] = _stage_bwd(_bg_f, "bg_bwd", [(r["pbg"], 128, 0, False)], [w["arow"], w["dtrow"]],
                                             [dbg], nb)
    dpm, g["conv"] = _prep_bwd(pm, w["conv"], dqkv, dpm)
    dx = _mm(dpm, w["in_main"], tb=True, add=dxa, name="mm_in_dx", tk=1920)
    dx = _mm(dpbg, w["in_bg"], tb=True, add=dx, name="mm_in_bg_dx")
    g["in_main"] = _mm(r["xm"], dpm, ta=True, name="mm_in_dw", tn=1536)
    g["in_bg"] = _mm(r["xm"], dpbg, ta=True, name="mm_in_bg_dw")
    return dx, g


def _place():
    return lax.axis_index("x"), lax.axis_index("y"), lax.axis_index("c")


def _other_chips(x, y):
    return [(1 - x, y), (x, 1 - y), (1 - x, 1 - y)]


HBM = pl.BlockSpec(memory_space=pltpu.HBM)
SEM = pl.BlockSpec(memory_space=pltpu.SEMAPHORE)
DATAFLOW = pltpu.SideEffectType.DATAFLOW_SIDE_EFFECTING


def _hbm(a):
    return pltpu.HBM(a.shape, a.dtype)


def _gather_start(locs, kinds, name):
    n = len(locs)
    locs = list(locs)
    lands = [lax.empty((N_CHIPS,) + a.shape if kind == "row" else (a.shape[0], N_CHIPS * a.shape[1]), a.dtype)
             for a, kind in zip(locs, kinds)]

    def body(*refs):
        loc_refs, land_refs = refs[:n], refs[n:2 * n]
        send_sems, recv_sems, token = refs[2 * n:3 * n], refs[3 * n:4 * n], refs[-1]
        x, y, c = _place()
        myq = 2 * x + y
        for t in range(n):
            width = locs[t].shape[1]
            mine = (land_refs[t].at[myq] if kinds[t] == "row"
                    else land_refs[t].at[:, pl.ds(pl.multiple_of(myq * width, 128), width)])
            for dev in [(cx, cy, c) for cx, cy in _other_chips(x, y)] + [(x, y, 1 - c)]:
                pltpu.make_async_remote_copy(src_ref=loc_refs[t], dst_ref=mine, send_sem=send_sems[t],
                                             recv_sem=recv_sems[t], device_id=dev, device_id_type=MESH).start()
        token[...] = jnp.zeros_like(token)

    res = pl.pallas_call(
        body, name=name,
        out_shape=[pltpu.SemaphoreType.DMA(())] * (2 * n) + [_hbm(a) for a in locs + lands]
        + [jax.ShapeDtypeStruct((8, 128), F32)],
        in_specs=[HBM] * (2 * n), out_specs=[SEM] * (2 * n) + [HBM] * (2 * n) + [pl.BlockSpec(memory_space=pltpu.VMEM)],
        input_output_aliases={t: 2 * n + t for t in range(2 * n)},
        compiler_params=pltpu.CompilerParams(has_side_effects=DATAFLOW),
    )(*[pltpu.with_memory_space_constraint(a, pltpu.HBM) for a in locs + lands])
    return (res[:n], res[n:2 * n], res[2 * n:3 * n], res[3 * n:4 * n]), res[-1]


def _split_wait(handle, after, name):
    send_sems, recv_sems, srcs, lands = handle
    n = len(srcs)

    def body(*refs):
        land_refs, ssems, rsems = refs[n:2 * n], refs[2 * n:3 * n], refs[3 * n:4 * n]
        x, y, c = _place()
        for t in range(n):
            done = pltpu.make_async_remote_copy(
                src_ref=land_refs[t], dst_ref=land_refs[t], send_sem=ssems[t], recv_sem=rsems[t],
                device_id=(x, y, c), device_id_type=MESH)
            done.wait_send()
            done.wait_recv()

    res = pl.pallas_call(
        body, name=name, out_shape=[_hbm(a) for a in list(srcs) + list(lands)],
        in_specs=[HBM] * (2 * n) + [SEM] * (2 * n) + [ANY], out_specs=[HBM] * (2 * n),
        input_output_aliases={t: t for t in range(2 * n)},
        compiler_params=pltpu.CompilerParams(has_side_effects=DATAFLOW),
    )(*srcs, *lands, *send_sems, *recv_sems, after)
    return res[:n], res[n:]


RS_CHUNKS = 2


def _piece(ref, kind, q, hf, pr, pc):
    if kind == "row":
        return ref.at[pl.ds((2 * q + hf) * pr, pr), :]
    return ref.at[pl.ds(hf * pr, pr), pl.ds(pl.multiple_of(q * pc, 128), pc)]


def _rs_to_sibling(ts, meta):
    n = len(ts)

    def body(*refs):
        t_refs, r1_refs, send_sems, recv_sems = refs[:n], refs[n:2 * n], refs[2 * n], refs[2 * n + 1]
        x, y, c = _place()
        sib = (x, y, 1 - c)
        for t, (kind, pr, pc) in enumerate(meta):
            for q in range(N_CHIPS):
                pltpu.make_async_remote_copy(
                    src_ref=_piece(t_refs[t], kind, q, 1 - c, pr, pc), dst_ref=r1_refs[t].at[q],
                    send_sem=send_sems.at[t], recv_sem=recv_sems.at[t], device_id=sib, device_id_type=MESH).start()
        for t in range(n):
            allq = pltpu.make_async_remote_copy(
                src_ref=r1_refs[t], dst_ref=r1_refs[t], send_sem=send_sems.at[t], recv_sem=recv_sems.at[t],
                device_id=sib, device_id_type=MESH)
            allq.wait_send()
            allq.wait_recv()

    return pl.pallas_call(
        body, name="rs_to_sibling", in_specs=[ANY] * n, out_specs=[ANY] * n,
        out_shape=[jax.ShapeDtypeStruct((N_CHIPS, pr, pc), F32) for (_, pr, pc) in meta],
        scratch_shapes=[pltpu.SemaphoreType.DMA((n,)), pltpu.SemaphoreType.DMA((n,))],
    )(*ts)


def _rs_add_sibling(ts, r1s, meta, c):
    n = len(ts)
    in_specs, out_specs, out_shape = [], [], []
    for kind, pr, pc in meta:
        rs = pr // RS_CHUNKS
        if kind == "row":
            in_specs.append(pl.BlockSpec((rs, pc), lambda q, r, c_ref: ((2 * q + c_ref[0]) * RS_CHUNKS + r, 0)))
        else:
            in_specs.append(pl.BlockSpec((rs, pc), lambda q, r, c_ref: (c_ref[0] * RS_CHUNKS + r, q)))
    for kind, pr, pc in meta:
        sp = pl.BlockSpec((None, pr // RS_CHUNKS, pc), lambda q, r, c_ref: (q, r, 0))
        in_specs.append(sp)
        out_specs.append(sp)
        out_shape.append(jax.ShapeDtypeStruct((N_CHIPS, pr, pc), BF16))

    def body(c_ref, *refs):
        for t in range(n):
            refs[2 * n + t][...] = (refs[t][...] + refs[n + t][...]).astype(BF16)

    return pl.pallas_call(
        body, name="rs_add_sibling", out_shape=out_shape,
        grid_spec=pltpu.PrefetchScalarGridSpec(num_scalar_prefetch=1, grid=(N_CHIPS, RS_CHUNKS), in_specs=in_specs,
                                               out_specs=out_specs),
        compiler_params=_params(2))(c.reshape(1).astype(jnp.int32), *ts, *r1s)


def _rs_chips_start(ps, meta, name):
    n = len(ps)
    ps = list(ps)
    lands = [lax.empty((N_CHIPS - 1, pr, pc), p.dtype) for p, (_, pr, pc) in zip(ps, meta)]

    def body(*refs):
        p_refs, land_refs = refs[:n], refs[n:2 * n]
        send_sems, recv_sems, token = refs[2 * n:3 * n], refs[3 * n:4 * n], refs[-1]
        x, y, c = _place()
        for t in range(n):
            for j, (cx, cy) in enumerate(_other_chips(x, y)):
                pltpu.make_async_remote_copy(
                    src_ref=p_refs[t].at[2 * cx + cy], dst_ref=land_refs[t].at[j], send_sem=send_sems[t],
                    recv_sem=recv_sems[t], device_id=(cx, cy, c), device_id_type=MESH).start()
        token[...] = jnp.zeros_like(token)

    res = pl.pallas_call(
        body, name=name,
        out_shape=[pltpu.SemaphoreType.DMA(())] * (2 * n) + [_hbm(a) for a in ps + lands]
        + [jax.ShapeDtypeStruct((8, 128), F32)],
        in_specs=[HBM] * (2 * n), out_specs=[SEM] * (2 * n) + [HBM] * (2 * n) + [pl.BlockSpec(memory_space=pltpu.VMEM)],
        input_output_aliases={t: 2 * n + t for t in range(2 * n)},
        compiler_params=pltpu.CompilerParams(has_side_effects=DATAFLOW),
    )(*[pltpu.with_memory_space_constraint(a, pltpu.HBM) for a in ps + lands])
    return (res[:n], res[n:2 * n], res[2 * n:3 * n], res[3 * n:4 * n]), res[-1]


def _rs_add_chips(ps, r2s, meta, myq, c):
    n = len(ps)
    in_specs, out_specs, out_shape = [], [], []
    for _, pr, pc in meta:
        in_specs.append(pl.BlockSpec((None, pr // RS_CHUNKS, pc), lambda r, q_ref, c_ref: (q_ref[0], r, 0)))
    for _, pr, pc in meta:
        in_specs.append(pl.BlockSpec((N_CHIPS - 1, pr // RS_CHUNKS, pc), lambda r, q_ref, c_ref: (0, r, 0)))
        out_specs.append(pl.BlockSpec((None, pr // RS_CHUNKS, pc), lambda r, q_ref, c_ref: (c_ref[0], r, 0)))
        out_shape.append(jax.ShapeDtypeStruct((2, pr, pc), F32))

    def body(q_ref, c_ref, *refs):
        for t in range(n):
            r2 = refs[n + t]
            own = refs[t][...].astype(F32)
            refs[2 * n + t][...] = ((own + r2[0].astype(F32)) + r2[1].astype(F32)) + r2[2].astype(F32)

    return pl.pallas_call(
        body, name="rs_add_chips", out_shape=out_shape,
        grid_spec=pltpu.PrefetchScalarGridSpec(num_scalar_prefetch=2, grid=(RS_CHUNKS,), in_specs=in_specs,
                                               out_specs=out_specs),
        compiler_params=_params(1))(myq.reshape(1).astype(jnp.int32), c.reshape(1).astype(jnp.int32), *ps, *r2s)


def _rs_share_halves(gs):
    n = len(gs)

    def body(*refs):
        g_refs, send_sems, recv_sems = refs[n:2 * n], refs[2 * n], refs[2 * n + 1]
        x, y, c = _place()
        sib = (x, y, 1 - c)
        for t in range(n):
            pltpu.make_async_remote_copy(
                src_ref=g_refs[t].at[c], dst_ref=g_refs[t].at[c], send_sem=send_sems.at[t], recv_sem=recv_sems.at[t],
                device_id=sib, device_id_type=MESH).start()
        for t in range(n):
            cp = pltpu.make_async_remote_copy(
                src_ref=g_refs[t].at[c], dst_ref=g_refs[t].at[1 - c], send_sem=send_sems.at[t],
                recv_sem=recv_sems.at[t], device_id=sib, device_id_type=MESH)
            cp.wait_send()
            cp.wait_recv()

    return pl.pallas_call(
        body, name="rs_share_halves", in_specs=[ANY] * n, out_specs=[ANY] * n,
        out_shape=[jax.ShapeDtypeStruct(g.shape, g.dtype) for g in gs], input_output_aliases={t: t for t in range(n)},
        scratch_shapes=[pltpu.SemaphoreType.DMA((n,)), pltpu.SemaphoreType.DMA((n,))],
    )(*gs)


def _rs_begin(ts, meta, c, name):
    r1s = _rs_to_sibling(ts, meta)
    ps = _rs_add_sibling(ts, r1s, meta, c)
    return _rs_chips_start(ps, meta, name + "_start")


def _rs_end(handle, after, meta, c, myq, name):
    ps, r2s = _split_wait(handle, after, name + "_wait")
    return _rs_share_halves(_rs_add_chips(ps, r2s, meta, myq, c))


def _allreduce_small(buf):
    rows = buf.shape[0]
    ndev = 8

    def body(b_ref, o_ref, slots, send_sems, recv_sems):
        x, y, c = _place()
        me = 4 * x + 2 * y + c
        slots[me] = b_ref[...]
        for k in range(1, ndev):
            kx, ky, kc = (k >> 2) & 1, (k >> 1) & 1, k & 1
            peer = (x ^ kx, y ^ ky, c ^ kc)
            pltpu.make_async_remote_copy(
                src_ref=b_ref, dst_ref=slots.at[me], send_sem=send_sems.at[k - 1], recv_sem=recv_sems.at[k - 1],
                device_id=peer, device_id_type=MESH).start()
        for k in range(1, ndev):
            kx, ky, kc = (k >> 2) & 1, (k >> 1) & 1, k & 1
            cp = pltpu.make_async_remote_copy(
                src_ref=b_ref, dst_ref=slots.at[me ^ k], send_sem=send_sems.at[k - 1], recv_sem=recv_sems.at[k - 1],
                device_id=(x ^ kx, y ^ ky, c ^ kc), device_id_type=MESH)
            cp.wait_send()
            cp.wait_recv()
        acc = slots[0]
        for s in range(1, ndev):
            acc = acc + slots[s]
        o_ref[...] = acc

    vm = pl.BlockSpec(memory_space=pltpu.VMEM)
    return pl.pallas_call(
        body, name="allreduce_small", in_specs=[vm], out_specs=vm, out_shape=jax.ShapeDtypeStruct((rows, 128), F32),
        scratch_shapes=[pltpu.VMEM((ndev, rows, 128), F32), pltpu.SemaphoreType.DMA((ndev - 1,)),
                        pltpu.SemaphoreType.DMA((ndev - 1,))],
        compiler_params=pltpu.CompilerParams(vmem_limit_bytes=VMEM_LIMIT))(buf)


RS_META = [("col", D // 2, IN_PAD), ("row", D // 8, D), ("row", D // 8, D), ("row", D // 8, D),
           ("col", D // 2, 2 * FFN // N_CHIPS), ("row", FFN // 8, D)]
SMALL_ROWS = 156


def _rope_tables(t_):
    half = SW_DIM // 2
    inv_freq = ROPE_THETA ** (-jnp.arange(half, dtype=F32) / half)
    ang = jnp.arange(t_, dtype=F32)[:, None] * inv_freq[None, :]
    return jnp.concatenate([jnp.cos(ang)] * 2, axis=1), jnp.concatenate([jnp.sin(ang)] * 2, axis=1)


def _lane_row(v16):
    return jnp.pad(v16.reshape(1, 2 * DN_HEADS), ((0, 0), (2 * DN_HEADS, 128 - 4 * DN_HEADS)))


def _pack_small(g):
    pad16 = jnp.pad(g["sinks"], ((0, 0), (0, 128 - SW_HEADS)))
    return jnp.concatenate([g["conv"].reshape(-1, 128), g["ln1g"].reshape(-1, 128), g["ln1b"].reshape(-1, 128),
                            g["ln2g"].reshape(-1, 128), g["ln2b"].reshape(-1, 128), g["gnw"], g["arow"], g["dtrow"],
                            pad16], axis=0)


def _unpack_small(buf):
    nconv = DN_CONV * 3 * D // 128
    o = nconv
    out = dict(conv=buf[:o].reshape(DN_CONV, 3 * D))
    for name in ("ln1g", "ln1b", "ln2g", "ln2b"):
        out[name] = buf[o:o + 8].reshape(D)
        o += 8
    out["gnw"] = buf[o]
    out["a_log"] = buf[o + 1, 2 * DN_HEADS:4 * DN_HEADS].reshape(2, DN_HEADS)
    out["dt_bias"] = buf[o + 2, 2 * DN_HEADS:4 * DN_HEADS].reshape(2, DN_HEADS)
    out["sinks"] = buf[o + 3, :SW_HEADS]
    return out


def kernel(x, w_in, conv_w, a_log, dt_bias, dn_norm_w, sinks, w_branch_a, w_branch_b, w_out, ln1_g, ln1_b, w_gate_up, w_down, ln2_g, ln2_b, loss_target, m_w_in, m_conv_w, m_a_log, m_dt_bias, m_dn_norm_w, m_sinks, m_w_branch_a, m_w_branch_b, m_w_out, m_ln1_g, m_ln1_b, m_w_gate_up, m_w_down, m_ln2_g, m_ln2_b, v_w_in, v_conv_w, v_a_log, v_dt_bias, v_dn_norm_w, v_sinks, v_w_branch_a, v_w_branch_b, v_w_out, v_ln1_g, v_ln1_b, v_w_gate_up, v_w_down, v_ln2_g, v_ln2_b):
    xi, yi, ci = _place()
    myq = 2 * xi + yi
    t_ = x.shape[1]
    cos, sin = _rope_tables(t_)

    loc_in = jnp.pad(w_in.astype(BF16), ((0, 0), (0, 0), (0, IN_PAD - IN_SHARD)))
    locs = [loc_in, w_branch_a.astype(BF16), w_branch_b.astype(BF16), w_out.astype(BF16), w_gate_up.astype(BF16),
            w_down.astype(BF16), conv_w]
    kinds = ["col", "row", "row", "row", "col", "row", "col"]
    gathers = []
    for l in range(DEPTH):
        srcs = [a[l] for a in locs]
        if gathers:
            srcs[-1] = srcs[-1] + gathers[-1][1][0, 0]
        gathers.append(_gather_start(srcs, kinds, "gather_%d_start" % l))

    def layer_weights(l, after):
        _, (full_in, full_a, full_b, full_o, full_gu, full_d, full_conv) = _split_wait(
            gathers[l][0], after, "gather_%d_wait" % l)
        orig = jnp.concatenate([full_in[:, q * IN_PAD:q * IN_PAD + IN_SHARD] for q in range(N_CHIPS)], axis=1)
        return dict(
            in_main=jnp.concatenate([orig[:, :R_BG], orig[:, R_GATES:], orig[:, R_SW:R_GATES]], axis=1),
            in_bg=jnp.pad(orig[:, R_BG:R_BG + 4 * DN_HEADS], ((0, 0), (0, 128 - 4 * DN_HEADS))),
            conv=full_conv, arow=_lane_row(a_log[l]), dtrow=_lane_row(dt_bias[l]), gnw=dn_norm_w[l][None],
            sinks=sinks[l][None], a=full_a.reshape(D, D), b=full_b.reshape(D, D), o=full_o.reshape(D, D),
            ln1g=ln1_g[l][None], ln1b=ln1_b[l][None], gu=full_gu, d=full_d.reshape(FFN, D),
            ln2g=ln2_g[l][None], ln2b=ln2_b[l][None])

    h = x[0]
    hm = h.astype(_MXU)
    layers, residuals = [], []
    for l in range(DEPTH):
        layers.append(layer_weights(l, gathers[-1][1] if l == 0 else h))
        h, hm, res = _layer_fwd(h, hm, layers[l], cos, sin)
        residuals.append(res)
    dh, sq = _loss_head(h, loss_target[0])
    loss = lax.psum((0.5 / D) * jnp.sum(sq), ("x", "y", "c"))

    big = [None] * DEPTH
    small = [None] * DEPTH
    pending = None
    for l in reversed(range(DEPTH)):
        dh, g = _layer_bwd(dh, residuals[l], layers[l], cos, sin)
        if pending is not None:
            big[pending[0]] = _rs_end(pending[1], dh, RS_META, ci, myq, "rs_chips_%d" % pending[0])
        g_orig = jnp.concatenate([g["in_main"][:, :R_BG], g["in_bg"][:, :4 * DN_HEADS], g["in_main"][:, O_QS:],
                                  g["in_main"][:, O_GA:O_QS]], axis=1)
        zeros = jnp.zeros((D, IN_PAD - IN_SHARD), F32)
        g_in = jnp.concatenate(
            [p for q in range(N_CHIPS) for p in (g_orig[:, q * IN_SHARD:(q + 1) * IN_SHARD], zeros)], axis=1)
        handle, token = _rs_begin([g_in, g["a"], g["b"], g["o"], g["gu"], g["d"]], RS_META, ci, "rs_chips_%d" % l)
        pending = (l, handle)
        dh = dh + token[0, 0]
        small[l] = _pack_small(g)
    tot = _allreduce_small(jnp.concatenate(small, axis=0))
    big[pending[0]] = _rs_end(pending[1], tot, RS_META, ci, myq, "rs_chips_%d" % pending[0])
    sm = [_unpack_small(tot[l * SMALL_ROWS:(l + 1) * SMALL_ROWS]) for l in range(DEPTH)]
    stack = lambda name: jnp.stack([s[name] for s in sm], axis=0)

    grads = dict(
        w_in=jnp.stack([big[l][0].reshape(D, IN_PAD)[:, :IN_SHARD] for l in range(DEPTH)]),
        conv_w=lax.dynamic_slice_in_dim(stack("conv"), myq * (3 * D // N_CHIPS), 3 * D // N_CHIPS, axis=2),
        a_log=stack("a_log"), dt_bias=stack("dt_bias"), dn_norm_w=stack("gnw"), sinks=stack("sinks"),
        w_branch_a=jnp.stack([big[l][1].reshape(D // N_CHIPS, D) for l in range(DEPTH)]),
        w_branch_b=jnp.stack([big[l][2].reshape(D // N_CHIPS, D) for l in range(DEPTH)]),
        w_out=jnp.stack([big[l][3].reshape(D // N_CHIPS, D) for l in range(DEPTH)]),
        ln1_g=stack("ln1g"), ln1_b=stack("ln1b"),
        w_gate_up=jnp.stack([big[l][4].reshape(D, 2 * FFN // N_CHIPS) for l in range(DEPTH)]),
        w_down=jnp.stack([big[l][5].reshape(FFN // N_CHIPS, D) for l in range(DEPTH)]),
        ln2_g=stack("ln2g"), ln2_b=stack("ln2b"))
    weights = dict(w_in=w_in, conv_w=conv_w, a_log=a_log, dt_bias=dt_bias, dn_norm_w=dn_norm_w, sinks=sinks,
                   w_branch_a=w_branch_a, w_branch_b=w_branch_b, w_out=w_out, ln1_g=ln1_g, ln1_b=ln1_b,
                   w_gate_up=w_gate_up, w_down=w_down, ln2_g=ln2_g, ln2_b=ln2_b)
    ms = dict(w_in=m_w_in, conv_w=m_conv_w, a_log=m_a_log, dt_bias=m_dt_bias, dn_norm_w=m_dn_norm_w, sinks=m_sinks,
              w_branch_a=m_w_branch_a, w_branch_b=m_w_branch_b, w_out=m_w_out, ln1_g=m_ln1_g, ln1_b=m_ln1_b,
              w_gate_up=m_w_gate_up, w_down=m_w_down, ln2_g=m_ln2_g, ln2_b=m_ln2_b)
    vs = dict(w_in=v_w_in, conv_w=v_conv_w, a_log=v_a_log, dt_bias=v_dt_bias, dn_norm_w=v_dn_norm_w, sinks=v_sinks,
              w_branch_a=v_w_branch_a, w_branch_b=v_w_branch_b, w_out=v_w_out, ln1_g=v_ln1_g, ln1_b=v_ln1_b,
              w_gate_up=v_w_gate_up, w_down=v_w_down, ln2_g=v_ln2_g, ln2_b=v_ln2_b)
    names = list(weights)
    upd = {n: _adamw("adamw_" + n, weights[n], grads[n], ms[n], vs[n]) for n in names}
    return (loss, dh[None], *[grads[n] for n in names], *[upd[n][0] for n in names], *[upd[n][1] for n in names],
            *[upd[n][2] for n in names])
```

```python
import functools

import jax
import jax.numpy as jnp
from jax import lax
from jax.experimental import pallas as pl
from jax.experimental.pallas import tpu as pltpu

F32 = jnp.float32
BF16 = jnp.bfloat16
_MXU = BF16

D = 1024
DEPTH = 4
DN_HEADS = 8
DN_DIM = 128
DN_CONV = 5
CHUNK = 64
SW_HEADS = 16
SW_KV = 4
SW_DIM = 64
SW_GRP = SW_HEADS // SW_KV
SW_BLOCK = 128
ROPE_THETA = 10000.0
FFN = 2816
ALPHA = (2.0 * DEPTH) ** 0.25
LN_EPS = 1e-5
RMS_EPS = 1e-6
IN_COLS = 7712
O_Z, O_GA, O_GB, O_QS, O_KS, O_VS, N_MAIN = 3072, 4096, 5120, 6144, 7168, 7424, 7680
R_BG, R_SW, R_GATES = 4096, 4128, 5664
N_CHIPS = 4
IN_SHARD = IN_COLS // N_CHIPS
IN_PAD = 2048
ADAM_LR, ADAM_B1, ADAM_B2, ADAM_EPS, ADAM_WD, ADAM_STEP = 0.001, 0.9, 0.999, 1e-08, 0.01, 10
VMEM_LIMIT = 52 * 1024 * 1024
MESH = pl.DeviceIdType.MESH
ANY = pl.BlockSpec(memory_space=pl.ANY)


def _params(n_grid, **kw):
    return pltpu.CompilerParams(dimension_semantics=("arbitrary",) * n_grid, vmem_limit_bytes=VMEM_LIMIT, **kw)


def _full(a):
    nd = a.ndim
    return pl.BlockSpec(a.shape, lambda *_, nd=nd: (0,) * nd)


def _raw_dot(a, b, ca, cb):
    return lax.dot_general(a.astype(_MXU), b.astype(_MXU), (((ca,), (cb,)), ((), ())), preferred_element_type=F32)


@jax.custom_vjp
def _nn(a, b):
    return _raw_dot(a, b, 1, 0)


@jax.custom_vjp
def _nt(a, b):
    return _raw_dot(a, b, 1, 1)


@jax.custom_vjp
def _tn(a, b):
    return _raw_dot(a, b, 0, 0)


_nn.defvjp(lambda a, b: (_nn(a, b), (a, b)), lambda r, g: (_nt(g, r[1]), _tn(r[0], g)))
_nt.defvjp(lambda a, b: (_nt(a, b), (a, b)), lambda r, g: (_nn(g, r[1]), _tn(g, r[0])))
_tn.defvjp(lambda a, b: (_tn(a, b), (a, b)), lambda r, g: (_nt(r[1], g), _nn(r[0], g)))


def _hdot(a, b, ca=1, cb=0):
    ah, bh = a.astype(BF16), b.astype(BF16)
    al, bl = (a - ah.astype(F32)).astype(BF16), (b - bh.astype(F32)).astype(BF16)
    dot = lambda u, v: lax.dot_general(u, v, (((ca,), (cb,)), ((), ())), preferred_element_type=F32)
    return dot(ah, bh) + (dot(ah, bl) + dot(al, bh))


def _inv_impl(mats):
    n = mats[0].shape[0]
    eye = (lax.broadcasted_iota(jnp.int32, (n, n), 0) == lax.broadcasted_iota(jnp.int32, (n, n), 1)).astype(F32)
    ps = [-a for a in mats]
    ts = [eye + p for p in ps]
    for _ in range(max(1, (n - 1).bit_length()) - 1):
        ps = [_hdot(p, p) for p in ps]
        ts = [t + _hdot(t, p) for t, p in zip(ts, ps)]
    return tuple(ts)


@jax.custom_vjp
def _inv(mats):
    return _inv_impl(mats)


def _inv_fwd(mats):
    ts = _inv_impl(mats)
    return ts, ts


def _inv_bwd(ts, gs):
    xs = [_hdot(t, g, 0, 0) for t, g in zip(ts, gs)]
    return (tuple(-_hdot(x, t, 1, 1) for x, t in zip(xs, ts)),)


_inv.defvjp(_inv_fwd, _inv_bwd)


@jax.custom_vjp
def _inv_saved(mats, saved):
    return saved


_inv_saved.defvjp(lambda mats, saved: (saved, saved),
                  lambda ts, gs: (_inv_bwd(ts, gs)[0], tuple(jnp.zeros_like(t) for t in ts)))


def _tile(n, cap):
    if n <= cap:
        return n
    best = [t for t in range(128, cap + 1, 128) if n % t == 0]
    assert best, (n, cap)
    return best[-1]


def _mm(a, b, *, name, ta=False, tb=False, add=None, tm=1024, tn=1024, tk=1024):
    if ta:
        k_, m_ = a.shape
    else:
        m_, k_ = a.shape
    n_ = b.shape[0] if tb else b.shape[1]
    tm, tn, tk = _tile(m_, tm), _tile(n_, tn), _tile(k_, tk)
    nk = k_ // tk
    has_add = add is not None

    def body(*refs):
        a_ref, b_ref = refs[:2]
        add_ref = refs[2] if has_add else None
        o_ref = refs[3] if has_add else refs[2]
        part = _raw_dot(a_ref[...], b_ref[...], 0 if ta else 1, 1 if tb else 0)
        if nk == 1:
            o_ref[...] = part + add_ref[...] if has_add else part
            return
        acc = refs[-1]
        k = pl.program_id(2)

        @pl.when(k == 0)
        def _():
            acc[...] = part

        @pl.when(jnp.logical_and(k > 0, k < nk - 1))
        def _():
            acc[...] += part

        @pl.when(k == nk - 1)
        def _():
            o_ref[...] = acc[...] + part + add_ref[...] if has_add else acc[...] + part

    a_spec = pl.BlockSpec((tk, tm), lambda i, j, k: (k, i)) if ta else pl.BlockSpec((tm, tk), lambda i, j, k: (i, k))
    b_spec = pl.BlockSpec((tn, tk), lambda i, j, k: (j, k)) if tb else pl.BlockSpec((tk, tn), lambda i, j, k: (k, j))
    o_spec = pl.BlockSpec((tm, tn), lambda i, j, k: (i, j))
    in_specs = [a_spec, b_spec] + ([o_spec] if has_add else [])
    args = (a, b) + ((add,) if has_add else ())
    return pl.pallas_call(
        body, name=name, grid=(m_ // tm, n_ // tn, nk), in_specs=in_specs, out_specs=o_spec,
        out_shape=jax.ShapeDtypeStruct((m_, n_), F32),
        scratch_shapes=[pltpu.VMEM((tm, tn), F32)] if nk > 1 else [],
        compiler_params=_params(3))(*args)


def _row_spec(tb, w, c0, percol):
    return pl.BlockSpec((tb, w), lambda i, j, c0=c0, pc=percol: (i, c0 + (j if pc else 0)))


def _stage_fwd(f, name, rows, params, outs, tb, ncol=1):
    t_ = rows[0][0].shape[0]
    nr, npar = len(rows), len(params)

    def body(*refs):
        res = f(*[r[...].astype(F32) for r in refs[:nr + npar]])
        for o_ref, val in zip(refs[nr + npar:], res):
            o_ref[...] = val.astype(o_ref.dtype)

    return pl.pallas_call(
        body, name=name, grid=(t_ // tb, ncol),
        in_specs=[_row_spec(tb, w, c0, pc) for (_, w, c0, pc) in rows] + [_full(p) for p in params],
        out_specs=[pl.BlockSpec((tb, w), lambda i, j: (i, j)) for w, _ in outs],
        out_shape=[jax.ShapeDtypeStruct((t_, w * ncol), dt) for w, dt in outs],
        compiler_params=_params(2))(*[r[0] for r in rows], *params)


def _into(dest, tb, width, t_, ncol, dtype):
    if dest is None:
        return pl.BlockSpec((tb, width), lambda i, j: (i, j)), jax.ShapeDtypeStruct((t_, width * ncol), dtype), None
    buf, total, c0 = dest
    return (pl.BlockSpec((tb, width), lambda i, j, c0=c0: (i, c0 + j)), jax.ShapeDtypeStruct((t_, total), dtype), buf)


def _stage_bwd(f, name, rows, params, douts, tb, ncol=1, cat=None, dtypes=None, dest=None):
    t_ = rows[0][0].shape[0]
    nr, npar, nd = len(rows), len(params), len(douts)
    cat = cat if cat is not None else [[r] for r in range(nr)]
    dtypes = dtypes if dtypes is not None else [F32] * len(cat)
    dest = dest or {}
    assert ncol == 1 or all(len(g) == 1 and rows[g[0]][3] for g in cat)
    nin = nr + npar + nd

    def body(*refs):
        ins = [r[...].astype(F32) for r in refs[:nr + npar]]
        dvals = tuple(r[...].astype(F32) for r in refs[nr + npar:nin])
        out_refs = refs[nin + len(aliased):]
        _, vjp = jax.vjp(f, *ins)
        grads = vjp(dvals)
        for o_ref, grp in zip(out_refs[:len(cat)], cat):
            val = grads[grp[0]] if len(grp) == 1 else jnp.concatenate([grads[r] for r in grp], axis=-1)
            o_ref[...] = val.astype(o_ref.dtype)
        first = jnp.logical_and(pl.program_id(0) == 0, pl.program_id(1) == 0)
        for p_ref, gp in zip(out_refs[len(cat):], grads[nr:]):
            @pl.when(first)
            def _(p_ref=p_ref):
                p_ref[...] = jnp.zeros_like(p_ref)
            p_ref[...] += gp

    gw = [sum(rows[r][1] for r in grp) for grp in cat]
    out_specs, out_shape, aliased, aliases = [], [], [], {}
    for gi, (w, dt) in enumerate(zip(gw, dtypes)):
        spec, shape, buf = _into(dest.get(gi), tb, w, t_, ncol, dt)
        out_specs.append(spec)
        out_shape.append(shape)
        if buf is not None:
            aliases[nin + len(aliased)] = gi
            aliased.append(buf)
    return pl.pallas_call(
        body, name=name, grid=(t_ // tb, ncol),
        in_specs=[_row_spec(tb, w, c0, pc) for (_, w, c0, pc) in rows] + [_full(p) for p in params]
        + [pl.BlockSpec((tb, d.shape[1] // ncol), lambda i, j: (i, j)) for d in douts] + [ANY] * len(aliased),
        out_specs=out_specs + [_full(p) for p in params],
        out_shape=out_shape + [jax.ShapeDtypeStruct(p.shape, F32) for p in params],
        input_output_aliases=aliases,
        compiler_params=_params(2))(*[r[0] for r in rows], *params, *douts, *aliased)


def _ln_f(x, y, g, b):
    u = ALPHA * x + y
    c = u - jnp.mean(u, axis=-1, keepdims=True)
    var = jnp.mean(c * c, axis=-1, keepdims=True)
    return (c * lax.rsqrt(var + LN_EPS) * g + b,)


def _ln_f2(x, y, g, b):
    out, = _ln_f(x, y, g, b)
    return out, out


def _swiglu_f(gate, up):
    return (jax.nn.silu(gate) * up,)


def _merge_f(ya, yb, ga, gb):
    return (jax.nn.sigmoid(ga) * ya + jax.nn.sigmoid(gb) * yb,)


def _gnorm_f(of, ob, z, w):
    o = of + ob
    return (o * lax.rsqrt(jnp.mean(o * o, axis=-1, keepdims=True) + RMS_EPS) * w * jax.nn.silu(z),)


def _bg_f(x, arow, dtrow):
    lane = lax.broadcasted_iota(jnp.int32, x.shape, 1)
    beta = jax.nn.sigmoid(x)
    g = -jnp.exp(arow) * jax.nn.softplus(x + dtrow)
    return (jnp.where(lane < 16, beta, jnp.where(lane < 32, g, 0.0)),)


PREP_ROWS = 512
PAD = 8


def _prep_f(part, w, *wins):
    xc = wins[0] * w[0:1, :]
    for k in range(1, DN_CONV):
        xc = xc + wins[k] * w[k:k + 1, :]
    a = jax.nn.silu(xc)
    nrm = a * lax.rsqrt(jnp.sum(a * a, axis=-1, keepdims=True) + RMS_EPS)
    return jnp.where(part == 0, nrm * (DN_DIM ** -0.5), jnp.where(part == 1, nrm, a))


def _windows(pad_ref, r0, rows):
    return [pad_ref[PAD + r0 - 2 + k:PAD + r0 - 2 + k + rows, :] for k in range(DN_CONV)]


def _prep_fwd(pm, conv):
    t_ = pm.shape[0]
    rows = min(PREP_ROWS, t_)

    def body(x_ref, w_ref, o_ref, pad_ref):
        part = pl.program_id(0) // DN_HEADS
        pad_ref[0:PAD, :] = jnp.zeros((PAD, DN_DIM), F32)
        pad_ref[PAD + t_:2 * PAD + t_, :] = jnp.zeros((PAD, DN_DIM), F32)
        pad_ref[PAD:PAD + t_, :] = x_ref[...]
        w = w_ref[...]
        for r in range(t_ // rows):
            o_ref[r * rows:(r + 1) * rows, :] = _prep_f(part, w, *_windows(pad_ref, r * rows, rows))

    ncb = 3 * DN_HEADS
    return pl.pallas_call(
        body, name="prep_fwd", grid=(ncb,),
        in_specs=[pl.BlockSpec((t_, DN_DIM), lambda j: (0, j)), pl.BlockSpec((DN_CONV, DN_DIM), lambda j: (0, j))],
        out_specs=pl.BlockSpec((t_, DN_DIM), lambda j: (0, j)),
        out_shape=jax.ShapeDtypeStruct((t_, ncb * DN_DIM), F32),
        scratch_shapes=[pltpu.VMEM((t_ + 2 * PAD, DN_DIM), F32)],
        compiler_params=_params(1))(pm, conv)


def _prep_bwd(pm, conv, dout, dpm):
    t_ = pm.shape[0]
    rows = min(PREP_ROWS, t_)

    def body(x_ref, w_ref, d_ref, _, dx_ref, dw_ref, pad_ref, dpad_ref):
        part = pl.program_id(0) // DN_HEADS
        pad_ref[0:PAD, :] = jnp.zeros((PAD, DN_DIM), F32)
        pad_ref[PAD + t_:2 * PAD + t_, :] = jnp.zeros((PAD, DN_DIM), F32)
        pad_ref[PAD:PAD + t_, :] = x_ref[...]
        dpad_ref[...] = jnp.zeros_like(dpad_ref)
        w = w_ref[...]
        dw = jnp.zeros((DN_CONV, DN_DIM), F32)
        for r in range(t_ // rows):
            r0 = r * rows
            _, vjp = jax.vjp(functools.partial(_prep_f, part), w, *_windows(pad_ref, r0, rows))
            grads = vjp(d_ref[r0:r0 + rows, :])
            dw = dw + grads[0]
            for k in range(DN_CONV):
                lo = PAD + r0 - 2 + k
                dpad_ref[lo:lo + rows, :] += grads[1 + k]
        dx_ref[...] = dpad_ref[PAD:PAD + t_, :].astype(dx_ref.dtype)
        dw_ref[...] = dw

    ncb = 3 * DN_HEADS
    col = pl.BlockSpec((t_, DN_DIM), lambda j: (0, j))
    wsp = pl.BlockSpec((DN_CONV, DN_DIM), lambda j: (0, j))
    return pl.pallas_call(
        body, name="prep_bwd", grid=(ncb,), in_specs=[col, wsp, col, ANY], out_specs=[col, wsp],
        out_shape=[jax.ShapeDtypeStruct(dpm.shape, dpm.dtype), jax.ShapeDtypeStruct((DN_CONV, ncb * DN_DIM), F32)],
        scratch_shapes=[pltpu.VMEM((t_ + 2 * PAD, DN_DIM), F32), pltpu.VMEM((t_ + 2 * PAD, DN_DIM), F32)],
        input_output_aliases={3: 0}, compiler_params=_params(1))(pm, conv, dout, dpm)


def _dn_chunk(sgns, qs, ks, vs, grows, brows, tsaved=None, with_t=False):
    c = qs[0].shape[0]
    i = lax.broadcasted_iota(jnp.int32, (c, c), 0)
    j = lax.broadcasted_iota(jnp.int32, (c, c), 1)
    eye = i == j
    incl = {s: (i - j) * int(s) >= 0 for s in set(sgns)}
    strict = {s: (i - j) * int(s) > 0 for s in set(sgns)}
    gcs = [jnp.sum(jnp.where(incl[s], g, 0.0), axis=1, keepdims=True) for s, g in zip(sgns, grows)]
    grs = [jnp.sum(jnp.where(eye, gc, 0.0), axis=0, keepdims=True) for gc in gcs]
    bcs = [jnp.sum(jnp.where(eye, b, 0.0), axis=1, keepdims=True) for b in brows]
    gls = [jnp.sum(g, axis=1, keepdims=True) for g in grows]
    decs = [jnp.exp(jnp.where(incl[s], gc - gr, -1e30)) for s, gc, gr in zip(sgns, gcs, grs)]
    kks = [_nt(k, k) for k in ks]
    amats = tuple(jnp.where(strict[s], bc * kk * dec, 0.0) for s, bc, kk, dec in zip(sgns, bcs, kks, decs))
    tinvs = _inv(amats) if tsaved is None else _inv_saved(amats, tsaved)
    egcs = [jnp.exp(gc) for gc in gcs]
    us = [_nn(t, v * bc) for t, v, bc in zip(tinvs, vs, bcs)]
    ws = [_nn(t, k * (bc * egc)) for t, k, bc, egc in zip(tinvs, ks, bcs, egcs)]
    qks = [_nt(q, k) * dec for q, k, dec in zip(qs, ks, decs)]
    qds = [q * egc for q, egc in zip(qs, egcs)]
    kds = [k * jnp.exp(gl - gc) for k, gl, gc in zip(ks, gls, gcs)]
    res = tuple(us), tuple(ws), tuple(qks), tuple(qds), tuple(kds)
    return res + (tinvs,) if with_t else res


def _dn_step(us, ws, qks, qds, kds, grows, ss):
    gls = [jnp.exp(jnp.sum(g, axis=1, keepdims=True)) for g in grows]
    wss = [_nn(w, s) for w, s in zip(ws, ss)]
    qss = [_nn(qd, s) for qd, s in zip(qds, ss)]
    vns = [u - x for u, x in zip(us, wss)]
    os_ = [a + _nn(qk, vn) for a, qk, vn in zip(qss, qks, vns)]
    s2s = [s * gl + _tn(kd, vn) for s, gl, kd, vn in zip(ss, gls, kds, vns)]
    return tuple(os_), tuple(s2s)


def _hs(h):
    return slice(h * DN_DIM, (h + 1) * DN_DIM)


_DIR_SGN = (1, -1)
_A_OUT = 5
_PROBLEMS = [(d, h) for d in range(2) for h in range(DN_HEADS)]
_SGNS = [_DIR_SGN[d] for d, _ in _PROBLEMS]


def _chunk_inputs(q_ref, k_ref, v_ref, g_ref, b_ref):
    heads = lambda ref: tuple(ref[:, _hs(h)].astype(F32) for _, h in _PROBLEMS)
    rows = lambda ref: tuple(ref[d, 0, h:h + 1, :] for d, h in _PROBLEMS)
    return heads(q_ref), heads(k_ref), heads(v_ref), rows(g_ref), rows(b_ref)


def _dn_a_fwd(qkv, grows, brows):
    t_ = qkv.shape[0]
    nch = t_ // CHUNK

    def body(q_ref, k_ref, v_ref, g_ref, b_ref, *outs):
        us, ws, qks, qds, kds, tinvs = _dn_chunk(_SGNS, *_chunk_inputs(q_ref, k_ref, v_ref, g_ref, b_ref), with_t=True)
        for p, (d, h) in enumerate(_PROBLEMS):
            u_ref, w_ref, qk_ref, qd_ref, kd_ref = outs[d * _A_OUT:(d + 1) * _A_OUT]
            u_ref[:, _hs(h)], w_ref[:, _hs(h)], qk_ref[0, h] = us[p], ws[p].astype(_MXU), qks[p].astype(_MXU)
            qd_ref[:, _hs(h)], kd_ref[:, _hs(h)] = qds[p].astype(_MXU), kds[p].astype(_MXU)
            outs[2 * _A_OUT + d][0, h] = tinvs[p]

    rspec = pl.BlockSpec((2, 1, DN_HEADS, CHUNK), lambda c: (0, c, 0, 0))
    big = pl.BlockSpec((CHUNK, D), lambda c: (c, 0))
    qks = pl.BlockSpec((1, DN_HEADS, CHUNK, CHUNK), lambda c: (c, 0, 0, 0))
    bigs = lambda dt: jax.ShapeDtypeStruct((t_, D), dt)
    qksh = lambda dt: jax.ShapeDtypeStruct((nch, DN_HEADS, CHUNK, CHUNK), dt)
    res = pl.pallas_call(
        body, name="dn_a_fwd", grid=(nch,),
        in_specs=[pl.BlockSpec((CHUNK, D), lambda c, p=p: (c, p)) for p in range(3)] + [rspec, rspec],
        out_specs=[big, big, qks, big, big] * 2 + [qks, qks],
        out_shape=[bigs(F32), bigs(_MXU), qksh(_MXU), bigs(_MXU), bigs(_MXU)] * 2 + [qksh(F32)] * 2,
        compiler_params=_params(1))(qkv, qkv, qkv, grows, brows)
    return res[:2 * _A_OUT], res[2 * _A_OUT:]


def _dn_a_bwd(qkv, grows, brows, tinv, dres, dg_b):
    t_ = qkv.shape[0]
    nch = t_ // CHUNK

    def body(q_ref, k_ref, v_ref, g_ref, b_ref, tf_ref, tb_ref, *rest):
        dins, dgb_ref, (dqkv_ref, dg_ref, db_ref) = rest[:2 * _A_OUT], rest[2 * _A_OUT], rest[2 * _A_OUT + 1:]
        tsaved = tuple((tf_ref, tb_ref)[d][0, h] for d, h in _PROBLEMS)
        _, vjp = jax.vjp(functools.partial(_dn_chunk, _SGNS, tsaved=tsaved),
                         *_chunk_inputs(q_ref, k_ref, v_ref, g_ref, b_ref))
        cots = []
        for o in range(_A_OUT):
            cots.append(tuple(dins[d * _A_OUT + o][0, h] if o == 2 else dins[d * _A_OUT + o][:, _hs(h)]
                              for d, h in _PROBLEMS))
        gq, gk, gv, gg, gb = vjp(tuple(cots))
        for p, (d, h) in enumerate(_PROBLEMS):
            dg_ref[d, 0, h:h + 1, :] = gg[p] + dgb_ref[d, 0, h:h + 1, :]
            db_ref[d, 0, h:h + 1, :] = gb[p]
        for h in range(DN_HEADS):
            dqkv_ref[:, _hs(h)] = gq[h] + gq[DN_HEADS + h]
            dqkv_ref[:, _hs(DN_HEADS + h)] = gk[h] + gk[DN_HEADS + h]
            dqkv_ref[:, _hs(2 * DN_HEADS + h)] = gv[h] + gv[DN_HEADS + h]

    rspec = pl.BlockSpec((2, 1, DN_HEADS, CHUNK), lambda c: (0, c, 0, 0))
    big = pl.BlockSpec((CHUNK, D), lambda c: (c, 0))
    qks = pl.BlockSpec((1, DN_HEADS, CHUNK, CHUNK), lambda c: (c, 0, 0, 0))
    rsh = jax.ShapeDtypeStruct(grows.shape, F32)
    return pl.pallas_call(
        body, name="dn_a_bwd", grid=(nch,),
        in_specs=[pl.BlockSpec((CHUNK, D), lambda c, p=p: (c, p)) for p in range(3)] + [rspec, rspec, qks, qks]
        + [big, big, qks, big, big] * 2 + [rspec],
        out_specs=[pl.BlockSpec((CHUNK, 3 * D), lambda c: (c, 0)), rspec, rspec],
        out_shape=[jax.ShapeDtypeStruct((t_, 3 * D), F32), rsh, rsh],
        compiler_params=_params(1))(qkv, qkv, qkv, grows, brows, *tinv, *dres, dg_b)


def _dir_specs(nch):
    def cidx(d):
        return (lambda n: n) if d == 0 else (lambda n: nch - 1 - n)
    out = []
    for d in range(2):
        ci = cidx(d)
        big = pl.BlockSpec((CHUNK, D), lambda n, ci=ci: (ci(n), 0))
        qks = pl.BlockSpec((1, DN_HEADS, CHUNK, CHUNK), lambda n, ci=ci: (ci(n), 0, 0, 0))
        row = pl.BlockSpec((1, 1, DN_HEADS, CHUNK), lambda n, ci=ci, d=d: (d, ci(n), 0, 0))
        st = pl.BlockSpec((1, DN_HEADS, DN_DIM, DN_DIM), lambda n, ci=ci: (ci(n), 0, 0, 0))
        out.append(dict(big=big, qk=qks, row=row, st=st))
    return out


def _step_inputs(ins, per_dir):
    def pick(o):
        if o == 2:
            return tuple(ins[d * per_dir + o][0, h].astype(F32) for d, h in _PROBLEMS)
        if o == 5:
            return tuple(ins[d * per_dir + o][0, 0, h:h + 1, :] for d, h in _PROBLEMS)
        return tuple(ins[d * per_dir + o][:, _hs(h)].astype(F32) for d, h in _PROBLEMS)
    return [pick(o) for o in range(6)]


def _dn_b_fwd(ares, grows):
    t_ = ares[0].shape[0]
    nch = t_ // CHUNK
    sp = _dir_specs(nch)

    def body(*refs):
        ins, outs, s_ref = refs[:12], refs[12:16], refs[16]

        @pl.when(pl.program_id(0) == 0)
        def _():
            s_ref[...] = jnp.zeros_like(s_ref)

        ss = tuple(s_ref[p] for p in range(len(_PROBLEMS)))
        os_, s2s = _dn_step(*_step_inputs(ins, 6), ss)
        for p, (d, h) in enumerate(_PROBLEMS):
            outs[2 + d][0, h] = ss[p]
            outs[d][:, _hs(h)] = os_[p]
            s_ref[p] = s2s[p]

    in_specs, args = [], []
    for d in range(2):
        in_specs += [sp[d]["big"], sp[d]["big"], sp[d]["qk"], sp[d]["big"], sp[d]["big"], sp[d]["row"]]
        args += list(ares[d * _A_OUT:(d + 1) * _A_OUT]) + [grows]
    stsh = jax.ShapeDtypeStruct((nch, DN_HEADS, DN_DIM, DN_DIM), F32)
    osh = jax.ShapeDtypeStruct((t_, D), F32)
    return pl.pallas_call(
        body, name="dn_b_fwd", grid=(nch,), in_specs=in_specs,
        out_specs=[sp[0]["big"], sp[1]["big"], sp[0]["st"], sp[1]["st"]], out_shape=[osh, osh, stsh, stsh],
        scratch_shapes=[pltpu.VMEM((2 * DN_HEADS, DN_DIM, DN_DIM), F32)],
        compiler_params=_params(1))(*args)


def _dn_b_bwd(ares, grows, st_f, st_b, do):
    t_ = ares[0].shape[0]
    nch = t_ // CHUNK
    sp = _dir_specs(nch)
    rsp = [sp[1], sp[0]]

    def body(*refs):
        ins, outs, ds_ref = refs[:16], refs[16:28], refs[28]

        @pl.when(pl.program_id(0) == 0)
        def _():
            ds_ref[...] = jnp.zeros_like(ds_ref)

        ss = tuple(ins[d * 8 + 6][0, h] for d, h in _PROBLEMS)
        _, vjp = jax.vjp(_dn_step, *_step_inputs(ins, 8), ss)
        dos = tuple(ins[d * 8 + 7][:, _hs(h)] for d, h in _PROBLEMS)
        grads = vjp((dos, tuple(ds_ref[p] for p in range(len(_PROBLEMS)))))
        for p, (d, h) in enumerate(_PROBLEMS):
            du_ref, dw_ref, dqk_ref, dqd_ref, dkd_ref, dg_ref = outs[d * 6:(d + 1) * 6]
            du_ref[:, _hs(h)], dw_ref[:, _hs(h)], dqk_ref[0, h] = grads[0][p], grads[1][p], grads[2][p]
            dqd_ref[:, _hs(h)], dkd_ref[:, _hs(h)] = grads[3][p], grads[4][p]
            dg_ref[0, 0, h:h + 1, :] = grads[5][p]
            ds_ref[p] = grads[6][p]

    in_specs, args, out_specs, out_shape = [], [], [], []
    big_sh = jax.ShapeDtypeStruct((t_, D), F32)
    qk_sh = jax.ShapeDtypeStruct((nch, DN_HEADS, CHUNK, CHUNK), F32)
    row_sh = jax.ShapeDtypeStruct((1, nch, DN_HEADS, CHUNK), F32)
    for d in range(2):
        s = rsp[d]
        row0 = pl.BlockSpec((1, 1, DN_HEADS, CHUNK), lambda m, d=d: (0, (nch - 1 - m) if d == 0 else m, 0, 0))
        rowd = pl.BlockSpec((1, 1, DN_HEADS, CHUNK), lambda m, d=d: (d, (nch - 1 - m) if d == 0 else m, 0, 0))
        in_specs += [s["big"], s["big"], s["qk"], s["big"], s["big"], rowd, s["st"], s["big"]]
        args += list(ares[d * _A_OUT:(d + 1) * _A_OUT]) + [grows, (st_f, st_b)[d], do]
        out_specs += [s["big"], s["big"], s["qk"], s["big"], s["big"], row0]
        out_shape += [big_sh, big_sh, qk_sh, big_sh, big_sh, row_sh]
    res = pl.pallas_call(
        body, name="dn_b_bwd", grid=(nch,), in_specs=in_specs, out_specs=out_specs, out_shape=out_shape,
        scratch_shapes=[pltpu.VMEM((2 * DN_HEADS, DN_DIM, DN_DIM), F32)],
        compiler_params=_params(1))(*args)
    dares = list(res[0:5]) + list(res[6:11])
    return dares, jnp.concatenate([res[5], res[11]], axis=0)


def _rope(x, c, s):
    half = SW_DIM // 2
    return x * c + jnp.concatenate([-x[:, half:], x[:, :half]], axis=-1) * s


def _attn_f(blk, t_, cq, sq, ck, sk, q, kp, ko, kn, vp, vo, vn, sinks):
    kall = jnp.concatenate([kp, ko, kn], axis=0)
    vall = jnp.concatenate([vp, vo, vn], axis=0)
    nq, nk = SW_GRP * SW_BLOCK, 3 * SW_BLOCK
    qpos = lax.broadcasted_iota(jnp.int32, (nq, nk), 0) % SW_BLOCK
    krel = lax.broadcasted_iota(jnp.int32, (nq, nk), 1) - SW_BLOCK
    kglob = krel + blk * SW_BLOCK
    valid = (jnp.abs(qpos - krel) <= SW_BLOCK) & (kglob >= 0) & (kglob < t_)
    kvs = range(SW_KV)
    heads = [[kvh * SW_GRP + g for g in range(SW_GRP)] for kvh in kvs]
    khs = [_rope(kall[:, kvh * SW_DIM:(kvh + 1) * SW_DIM], ck, sk) for kvh in kvs]
    vhs = [vall[:, kvh * SW_DIM:(kvh + 1) * SW_DIM] for kvh in kvs]
    qgs = [jnp.concatenate([_rope(q[:, h * SW_DIM:(h + 1) * SW_DIM], cq, sq) for h in hs], axis=0) for hs in heads]
    ss = [jnp.where(valid, _nt(qg, kh) * (SW_DIM ** -0.5), -1e30) for qg, kh in zip(qgs, khs)]
    snks = [jnp.concatenate([jnp.broadcast_to(sinks[:, h:h + 1], (SW_BLOCK, 1)) for h in hs], axis=0) for hs in heads]
    ms = [lax.stop_gradient(jnp.maximum(jnp.max(s, axis=-1, keepdims=True), snk)) for s, snk in zip(ss, snks)]
    es = [jnp.exp(s - m) for s, m in zip(ss, ms)]
    ps = [e / (jnp.sum(e, axis=-1, keepdims=True) + jnp.exp(snk - m)) for e, snk, m in zip(es, snks, ms)]
    ogs = [_nn(p, vh) for p, vh in zip(ps, vhs)]
    return jnp.concatenate([og[g * SW_BLOCK:(g + 1) * SW_BLOCK] for og in ogs for g in range(SW_GRP)], axis=-1)


def _attn_specs(nb):
    prv = lambda i: jnp.maximum(i - 1, 0)
    nxt = lambda i: jnp.minimum(i + 1, nb - 1)
    rows = [lambda i: i, prv, lambda i: i, nxt]
    tab = [pl.BlockSpec((SW_BLOCK, SW_DIM), lambda i, r=r: (r(i), 0)) for r in rows]
    qs = pl.BlockSpec((SW_BLOCK, SW_HEADS * SW_DIM), lambda i: (i, O_QS // (SW_HEADS * SW_DIM)))
    kw = SW_KV * SW_DIM
    ks = [pl.BlockSpec((SW_BLOCK, kw), lambda i, r=r: (r(i), O_KS // kw)) for r in rows[1:]]
    vs = [pl.BlockSpec((SW_BLOCK, kw), lambda i, r=r: (r(i), O_VS // kw)) for r in rows[1:]]
    return tab, qs, ks, vs


def _attn_tables(refs):
    cq, cp, co, cn, sq, sp_, so, sn = [r[...] for r in refs]
    return cq, sq, jnp.concatenate([cp, co, cn], axis=0), jnp.concatenate([sp_, so, sn], axis=0)


def _attn_fwd(pm, cos, sin, sinks):
    t_ = pm.shape[0]
    nb = t_ // SW_BLOCK
    tab, qs, ks, vs = _attn_specs(nb)

    def body(*refs):
        tabs = _attn_tables(refs[:8])
        vals = [r[...] for r in refs[8:16]]
        refs[16][...] = _attn_f(pl.program_id(0), t_, *tabs, *vals).astype(refs[16].dtype)

    return pl.pallas_call(
        body, name="attn_fwd", grid=(nb,), in_specs=tab + tab + [qs] + ks + vs + [_full(sinks)],
        out_specs=pl.BlockSpec((SW_BLOCK, D), lambda i: (i, 0)), out_shape=jax.ShapeDtypeStruct((t_, D), _MXU),
        compiler_params=_params(1))(*([cos] * 4), *([sin] * 4), pm, pm, pm, pm, pm, pm, pm, sinks)


def _attn_bwd(pm, cos, sin, sinks, do, dpm):
    t_ = pm.shape[0]
    nb = t_ // SW_BLOCK
    tab, qs, ks, vs = _attn_specs(nb)
    kw = SW_KV * SW_DIM

    def body(*refs):
        tabs = _attn_tables(refs[:8])
        vals = [r[...] for r in refs[8:16]]
        do_ref, outs = refs[16], refs[18:]
        _, vjp = jax.vjp(functools.partial(_attn_f, pl.program_id(0), t_, *tabs), *vals)
        grads = vjp(do_ref[...])
        for o_ref, g in zip(outs[:7], grads[:7]):
            o_ref[...] = g.astype(o_ref.dtype)

        @pl.when(pl.program_id(0) == 0)
        def _():
            outs[7][...] = jnp.zeros_like(outs[7])
        outs[7][...] += grads[7]

    own = lambda w: pl.BlockSpec((SW_BLOCK, w), lambda i: (i, 0))
    return pl.pallas_call(
        body, name="attn_bwd", grid=(nb,),
        in_specs=tab + tab + [qs] + ks + vs + [_full(sinks), own(D), ANY],
        out_specs=[pl.BlockSpec((SW_BLOCK, D), lambda i: (i, O_QS // D))] + [own(kw)] * 6 + [_full(sinks)],
        out_shape=[jax.ShapeDtypeStruct(dpm.shape, dpm.dtype)] + [jax.ShapeDtypeStruct((t_, kw), F32)] * 6
        + [jax.ShapeDtypeStruct(sinks.shape, F32)],
        input_output_aliases={17: 0},
        compiler_params=_params(1))(*([cos] * 4), *([sin] * 4), pm, pm, pm, pm, pm, pm, pm, sinks, do, dpm)


def _band_sum(kparts, vparts, dpm):
    t_, kw = kparts[1].shape
    nb = t_ // SW_BLOCK

    def body(kp, ko, kn, vp, vo, vn, _, out_ref):
        j = pl.program_id(0)
        band = lambda p, o, n: o[...] + jnp.where(j + 1 < nb, p[...], 0.0) + jnp.where(j > 0, n[...], 0.0)
        out_ref[...] = jnp.concatenate([band(kp, ko, kn), band(vp, vo, vn)], axis=-1).astype(out_ref.dtype)

    specs = [pl.BlockSpec((SW_BLOCK, kw), lambda j: (jnp.minimum(j + 1, nb - 1), 0)),
             pl.BlockSpec((SW_BLOCK, kw), lambda j: (j, 0)),
             pl.BlockSpec((SW_BLOCK, kw), lambda j: (jnp.maximum(j - 1, 0), 0))]
    return pl.pallas_call(
        body, name="band_sum", grid=(nb,), in_specs=specs * 2 + [ANY],
        out_specs=pl.BlockSpec((SW_BLOCK, 2 * kw), lambda j: (j, O_KS // (2 * kw))),
        out_shape=jax.ShapeDtypeStruct(dpm.shape, dpm.dtype), input_output_aliases={6: 0},
        compiler_params=_params(1))(*kparts, *vparts, dpm)


def _loss_head(y, target, tb=256):
    t_ = y.shape[0]

    def body(y_ref, t_ref, dy_ref, acc_ref):
        err = y_ref[...] - t_ref[...]
        dy_ref[...] = err * (1.0 / D)
        sq = (err * err).reshape(tb // 8, 8, D).sum(axis=0)
        part = sq[:, 0:128]
        for c in range(1, D // 128):
            part = part + sq[:, c * 128:(c + 1) * 128]

        @pl.when(pl.program_id(0) == 0)
        def _():
            acc_ref[...] = jnp.zeros_like(acc_ref)
        acc_ref[...] += part

    row = pl.BlockSpec((tb, D), lambda i: (i, 0))
    return pl.pallas_call(
        body, name="loss_head", grid=(t_ // tb,), in_specs=[row, row],
        out_specs=[row, pl.BlockSpec((8, 128), lambda i: (0, 0))],
        out_shape=[jax.ShapeDtypeStruct((t_, D), F32), jax.ShapeDtypeStruct((8, 128), F32)],
        compiler_params=_params(1))(y, target)


def _adamw(name, w, g, m, v):
    shape = w.shape
    cols = shape[-1]
    rows = w.size // cols
    w2, g2, m2, v2 = [a.reshape(rows, cols) for a in (w, g, m, v)]
    tb = rows
    while tb * cols * 4 > (1 << 20) and tb % 16 == 0:
        tb //= 2
    bc1 = 1.0 - ADAM_B1 ** ADAM_STEP
    bc2 = 1.0 - ADAM_B2 ** ADAM_STEP

    def body(w_ref, g_ref, m_ref, v_ref, d_ref, nm_ref, nv_ref):
        gv = g_ref[...]
        nm = ADAM_B1 * m_ref[...] + (1.0 - ADAM_B1) * gv
        nv = ADAM_B2 * v_ref[...] + (1.0 - ADAM_B2) * (gv * gv)
        d_ref[...] = -ADAM_LR * ((nm / bc1) / (jnp.sqrt(nv / bc2) + ADAM_EPS) + ADAM_WD * w_ref[...])
        nm_ref[...] = nm
        nv_ref[...] = nv

    spec = pl.BlockSpec((tb, cols), lambda i: (i, 0))
    sh = jax.ShapeDtypeStruct((rows, cols), F32)
    outs = pl.pallas_call(body, name=name, grid=(rows // tb,), in_specs=[spec] * 4, out_specs=[spec] * 3,
                          out_shape=[sh] * 3, compiler_params=_params(1))(w2, g2, m2, v2)
    return [o.reshape(shape) for o in outs]


def _to_rows(bg, col0):
    t_ = bg.shape[0]
    a = bg[:, col0:col0 + 2 * DN_HEADS].reshape(t_ // CHUNK, CHUNK, 2, DN_HEADS)
    return jnp.transpose(a, (2, 0, 3, 1))


def _from_rows(db, dg):
    nch = db.shape[1]
    back = lambda a: jnp.transpose(a, (1, 3, 0, 2)).reshape(nch * CHUNK, 2 * DN_HEADS)
    return jnp.pad(jnp.concatenate([back(db), back(dg)], axis=1), ((0, 0), (0, 128 - 4 * DN_HEADS)))


def _layer_fwd(x, xm, w, rest, cos, sin):
    t_ = x.shape[0]
    tb, nb = min(256, t_), min(1024, t_)
    pm = _mm(xm, w["in_main"], name="mm_in", tn=1536)
    pbg = _mm(xm, w["in_bg"], name="mm_in_bg")
    w = {**w, **rest(pbg)}
    qkv = _prep_fwd(pm, w["conv"])
    bg, = _stage_fwd(_bg_f, "bg_fwd", [(pbg, 128, 0, False)], [w["arow"], w["dtrow"]], [(128, F32)], nb)
    brows, grows = _to_rows(bg, 0), _to_rows(bg, 2 * DN_HEADS)
    ares, tinv = _dn_a_fwd(qkv, grows, brows)
    o_f, o_b, st_f, st_b = _dn_b_fwd(ares, grows)
    odn, = _stage_fwd(_gnorm_f, "gnorm_fwd", [(o_f, 128, 0, True), (o_b, 128, 0, True), (pm, 128, O_Z // 128, True)],
                      [w["gnw"]], [(128, _MXU)], nb, ncol=DN_HEADS)
    osw = _attn_fwd(pm, cos, sin, w["sinks"])
    ya = _mm(odn, w["a"], name="mm_a")
    yb = _mm(osw, w["b"], name="mm_b")
    gates = [(pm, 512, O_GA // 512, True), (pm, 512, O_GB // 512, True)]
    merged, = _stage_fwd(_merge_f, "merge_fwd", [(ya, 512, 0, True), (yb, 512, 0, True)] + gates, [], [(512, _MXU)],
                         tb, ncol=2)
    mix = _mm(merged, w["o"], name="mm_o")
    x1, x1m = _stage_fwd(_ln_f2, "ln1_fwd", [(x, D, 0, False), (mix, D, 0, False)], [w["ln1g"], w["ln1b"]],
                         [(D, F32), (D, _MXU)], tb)
    gu = _mm(x1m, w["gu"], name="mm_gu", tn=FFN // 2)
    hid, = _stage_fwd(_swiglu_f, "swiglu_fwd", [(gu, FFN // 2, 0, True), (gu, FFN // 2, 2, True)], [],
                      [(FFN // 2, _MXU)], tb, ncol=2)
    ffn = _mm(hid, w["d"], name="mm_d", tk=FFN // 2)
    x2, x2m = _stage_fwd(_ln_f2, "ln2_fwd", [(x1, D, 0, False), (ffn, D, 0, False)], [w["ln2g"], w["ln2b"]],
                         [(D, F32), (D, _MXU)], tb)
    res = dict(w=w, x=x, xm=xm, pm=pm, pbg=pbg, qkv=qkv, grows=grows, brows=brows, ares=ares, tinv=tinv, o_f=o_f, o_b=o_b, st_f=st_f,
               st_b=st_b, odn=odn, osw=osw, ya=ya, yb=yb, merged=merged, mix=mix, x1=x1, x1m=x1m, gu=gu, hid=hid,
               ffn=ffn)
    return x2, x2m, res


def _layer_bwd(dx2, r, w, cos, sin):
    t_ = dx2.shape[0]
    tb, sb, nb = min(256, t_), min(128, t_), min(1024, t_)
    pm = r["pm"]
    g = {}
    dx1a, dffn, g["ln2g"], g["ln2b"] = _stage_bwd(
        _ln_f, "ln2_bwd", [(r["x1"], D, 0, False), (r["ffn"], D, 0, False)], [w["ln2g"], w["ln2b"]], [dx2], tb,
        dtypes=[F32, _MXU])
    dhid = _mm(dffn, w["d"], tb=True, name="mm_d_dx", tn=FFN // 2)
    g["d"] = _mm(r["hid"], dffn, ta=True, name="mm_d_dw", tm=FFN // 2)
    dgu, = _stage_bwd(_swiglu_f, "swiglu_bwd", [(r["gu"], FFN, 0, False), (r["gu"], FFN, 1, False)], [], [dhid], sb,
                      cat=[[0, 1]], dtypes=[_MXU])
    dx1 = _mm(dgu, w["gu"], tb=True, add=dx1a, name="mm_gu_dx", tk=FFN // 2)
    g["gu"] = _mm(r["x1m"], dgu, ta=True, name="mm_gu_dw", tn=FFN // 2)
    dxa, dmix, g["ln1g"], g["ln1b"] = _stage_bwd(
        _ln_f, "ln1_bwd", [(r["x"], D, 0, False), (r["mix"], D, 0, False)], [w["ln1g"], w["ln1b"]], [dx1], tb,
        dtypes=[F32, _MXU])
    dmerged = _mm(dmix, w["o"], tb=True, name="mm_o_dx")
    g["o"] = _mm(r["merged"], dmix, ta=True, name="mm_o_dw")
    dya, dyb, dpm = _stage_bwd(
        _merge_f, "merge_bwd", [(r["ya"], D, 0, False), (r["yb"], D, 0, False), (pm, D, O_GA // D, False),
                                (pm, D, O_GB // D, False)], [], [dmerged], tb, cat=[[0], [1], [2, 3]],
        dtypes=[_MXU, _MXU, _MXU], dest={2: (None, N_MAIN, O_GA // (2 * D))})
    dodn = _mm(dya, w["a"], tb=True, name="mm_a_dx")
    g["a"] = _mm(r["odn"], dya, ta=True, name="mm_a_dw")
    dosw = _mm(dyb, w["b"], tb=True, name="mm_b_dx")
    g["b"] = _mm(r["osw"], dyb, ta=True, name="mm_b_dw")
    ab = _attn_bwd(pm, cos, sin, w["sinks"], dosw, dpm)
    dpm, g["sinks"] = ab[0], ab[7]
    dpm = _band_sum(ab[1:4], ab[4:7], dpm)
    dof, _, dpm, g["gnw"] = _stage_bwd(
        _gnorm_f, "gnorm_bwd", [(r["o_f"], 128, 0, True), (r["o_b"], 128, 0, True), (pm, 128, O_Z // 128, True)],
        [w["gnw"]], [dodn], nb, ncol=DN_HEADS, dtypes=[F32, F32, _MXU], dest={2: (dpm, N_MAIN, O_Z // 128)})
    dares, dg_b = _dn_b_bwd(r["ares"], r["grows"], r["st_f"], r["st_b"], dof)
    dqkv, dgrows, dbrows = _dn_a_bwd(r["qkv"], r["grows"], r["brows"], r["tinv"], dares, dg_b)
    dbg = _from_rows(dbrows, dgrows)
    dpbg, g["arow"], g["dtrow"] = _stage_bwd(_bg_f, "bg_bwd", [(r["pbg"], 128, 0, False)], [w["arow"], w["dtrow"]],
                                             [dbg], nb)
    dpm, g["conv"] = _prep_bwd(pm, w["conv"], dqkv, dpm)
    dx = _mm(dpm, w["in_main"], tb=True, add=dxa, name="mm_in_dx", tk=1920)
    dx = _mm(dpbg, w["in_bg"], tb=True, add=dx, name="mm_in_bg_dx")
    g["in_main"] = _mm(r["xm"], dpm, ta=True, name="mm_in_dw", tn=1536)
    g["in_bg"] = _mm(r["xm"], dpbg, ta=True, name="mm_in_bg_dw")
    return dx, g


def _place():
    return lax.axis_index("x"), lax.axis_index("y"), lax.axis_index("c")


def _other_chips(x, y):
    return [(1 - x, y), (x, 1 - y), (1 - x, 1 - y)]


HBM = pl.BlockSpec(memory_space=pltpu.HBM)
SEM = pl.BlockSpec(memory_space=pltpu.SEMAPHORE)
DATAFLOW = pltpu.SideEffectType.DATAFLOW_SIDE_EFFECTING


def _hbm(a):
    return pltpu.HBM(a.shape, a.dtype)


def _gather_start(locs, kinds, name):
    n = len(locs)
    locs = list(locs)
    lands = [lax.empty((N_CHIPS,) + a.shape if kind == "row" else (a.shape[0], N_CHIPS * a.shape[1]), a.dtype)
             for a, kind in zip(locs, kinds)]

    def body(*refs):
        loc_refs, land_refs = refs[:n], refs[n:2 * n]
        send_sems, recv_sems, token = refs[2 * n:3 * n], refs[3 * n:4 * n], refs[-1]
        x, y, c = _place()
        myq = 2 * x + y
        for t in range(n):
            width = locs[t].shape[1]
            mine = (land_refs[t].at[myq] if kinds[t] == "row"
                    else land_refs[t].at[:, pl.ds(pl.multiple_of(myq * width, 128), width)])
            for dev in [(cx, cy, c) for cx, cy in _other_chips(x, y)] + [(x, y, 1 - c)]:
                pltpu.make_async_remote_copy(src_ref=loc_refs[t], dst_ref=mine, send_sem=send_sems[t],
                                             recv_sem=recv_sems[t], device_id=dev, device_id_type=MESH).start()
        token[...] = jnp.zeros_like(token)

    res = pl.pallas_call(
        body, name=name,
        out_shape=[pltpu.SemaphoreType.DMA(())] * (2 * n) + [_hbm(a) for a in locs + lands]
        + [jax.ShapeDtypeStruct((8, 128), F32)],
        in_specs=[HBM] * (2 * n), out_specs=[SEM] * (2 * n) + [HBM] * (2 * n) + [pl.BlockSpec(memory_space=pltpu.VMEM)],
        input_output_aliases={t: 2 * n + t for t in range(2 * n)},
        compiler_params=pltpu.CompilerParams(has_side_effects=DATAFLOW),
    )(*[pltpu.with_memory_space_constraint(a, pltpu.HBM) for a in locs + lands])
    return (res[:n], res[n:2 * n], res[2 * n:3 * n], res[3 * n:4 * n]), res[-1]


def _split_wait(handle, after, name):
    send_sems, recv_sems, srcs, lands = handle
    n = len(srcs)

    def body(*refs):
        land_refs, ssems, rsems = refs[n:2 * n], refs[2 * n:3 * n], refs[3 * n:4 * n]
        x, y, c = _place()
        for t in range(n):
            done = pltpu.make_async_remote_copy(
                src_ref=land_refs[t], dst_ref=land_refs[t], send_sem=ssems[t], recv_sem=rsems[t],
                device_id=(x, y, c), device_id_type=MESH)
            done.wait_send()
            done.wait_recv()

    res = pl.pallas_call(
        body, name=name, out_shape=[_hbm(a) for a in list(srcs) + list(lands)],
        in_specs=[HBM] * (2 * n) + [SEM] * (2 * n) + [ANY], out_specs=[HBM] * (2 * n),
        input_output_aliases={t: t for t in range(2 * n)},
        compiler_params=pltpu.CompilerParams(has_side_effects=DATAFLOW),
    )(*srcs, *lands, *send_sems, *recv_sems, after)
    return res[:n], res[n:]


RS_CHUNKS = 2


def _piece(ref, kind, q, hf, pr, pc):
    if kind == "row":
        return ref.at[pl.ds((2 * q + hf) * pr, pr), :]
    return ref.at[pl.ds(hf * pr, pr), pl.ds(pl.multiple_of(q * pc, 128), pc)]


def _rs_to_sibling(ts, meta):
    n = len(ts)

    def body(*refs):
        t_refs, r1_refs, send_sems, recv_sems = refs[:n], refs[n:2 * n], refs[2 * n], refs[2 * n + 1]
        x, y, c = _place()
        sib = (x, y, 1 - c)
        for t, (kind, pr, pc) in enumerate(meta):
            for q in range(N_CHIPS):
                pltpu.make_async_remote_copy(
                    src_ref=_piece(t_refs[t], kind, q, 1 - c, pr, pc), dst_ref=r1_refs[t].at[q],
                    send_sem=send_sems.at[t], recv_sem=recv_sems.at[t], device_id=sib, device_id_type=MESH).start()
        for t in range(n):
            allq = pltpu.make_async_remote_copy(
                src_ref=r1_refs[t], dst_ref=r1_refs[t], send_sem=send_sems.at[t], recv_sem=recv_sems.at[t],
                device_id=sib, device_id_type=MESH)
            allq.wait_send()
            allq.wait_recv()

    return pl.pallas_call(
        body, name="rs_to_sibling", in_specs=[ANY] * n, out_specs=[ANY] * n,
        out_shape=[jax.ShapeDtypeStruct((N_CHIPS, pr, pc), F32) for (_, pr, pc) in meta],
        scratch_shapes=[pltpu.SemaphoreType.DMA((n,)), pltpu.SemaphoreType.DMA((n,))],
    )(*ts)


def _rs_add_sibling(ts, r1s, meta, c):
    n = len(ts)
    in_specs, out_specs, out_shape = [], [], []
    for kind, pr, pc in meta:
        rs = pr // RS_CHUNKS
        if kind == "row":
            in_specs.append(pl.BlockSpec((rs, pc), lambda q, r, c_ref: ((2 * q + c_ref[0]) * RS_CHUNKS + r, 0)))
        else:
            in_specs.append(pl.BlockSpec((rs, pc), lambda q, r, c_ref: (c_ref[0] * RS_CHUNKS + r, q)))
    for kind, pr, pc in meta:
        sp = pl.BlockSpec((None, pr // RS_CHUNKS, pc), lambda q, r, c_ref: (q, r, 0))
        in_specs.append(sp)
        out_specs.append(sp)
        out_shape.append(jax.ShapeDtypeStruct((N_CHIPS, pr, pc), BF16))

    def body(c_ref, *refs):
        for t in range(n):
            refs[2 * n + t][...] = (refs[t][...] + refs[n + t][...]).astype(BF16)

    return pl.pallas_call(
        body, name="rs_add_sibling", out_shape=out_shape,
        grid_spec=pltpu.PrefetchScalarGridSpec(num_scalar_prefetch=1, grid=(N_CHIPS, RS_CHUNKS), in_specs=in_specs,
                                               out_specs=out_specs),
        compiler_params=_params(2))(c.reshape(1).astype(jnp.int32), *ts, *r1s)


def _rs_chips_start(ps, meta, name):
    n = len(ps)
    ps = list(ps)
    lands = [lax.empty((N_CHIPS - 1, pr, pc), p.dtype) for p, (_, pr, pc) in zip(ps, meta)]

    def body(*refs):
        p_refs, land_refs = refs[:n], refs[n:2 * n]
        send_sems, recv_sems, token = refs[2 * n:3 * n], refs[3 * n:4 * n], refs[-1]
        x, y, c = _place()
        for t in range(n):
            for j, (cx, cy) in enumerate(_other_chips(x, y)):
                pltpu.make_async_remote_copy(
                    src_ref=p_refs[t].at[2 * cx + cy], dst_ref=land_refs[t].at[j], send_sem=send_sems[t],
                    recv_sem=recv_sems[t], device_id=(cx, cy, c), device_id_type=MESH).start()
        token[...] = jnp.zeros_like(token)

    res = pl.pallas_call(
        body, name=name,
        out_shape=[pltpu.SemaphoreType.DMA(())] * (2 * n) + [_hbm(a) for a in ps + lands]
        + [jax.ShapeDtypeStruct((8, 128), F32)],
        in_specs=[HBM] * (2 * n), out_specs=[SEM] * (2 * n) + [HBM] * (2 * n) + [pl.BlockSpec(memory_space=pltpu.VMEM)],
        input_output_aliases={t: 2 * n + t for t in range(2 * n)},
        compiler_params=pltpu.CompilerParams(has_side_effects=DATAFLOW),
    )(*[pltpu.with_memory_space_constraint(a, pltpu.HBM) for a in ps + lands])
    return (res[:n], res[n:2 * n], res[2 * n:3 * n], res[3 * n:4 * n]), res[-1]


def _rs_add_chips(ps, r2s, meta, myq, c):
    n = len(ps)
    in_specs, out_specs, out_shape = [], [], []
    for _, pr, pc in meta:
        in_specs.append(pl.BlockSpec((None, pr // RS_CHUNKS, pc), lambda r, q_ref, c_ref: (q_ref[0], r, 0)))
    for _, pr, pc in meta:
        in_specs.append(pl.BlockSpec((N_CHIPS - 1, pr // RS_CHUNKS, pc), lambda r, q_ref, c_ref: (0, r, 0)))
        out_specs.append(pl.BlockSpec((None, pr // RS_CHUNKS, pc), lambda r, q_ref, c_ref: (c_ref[0], r, 0)))
        out_shape.append(jax.ShapeDtypeStruct((2, pr, pc), F32))

    def body(q_ref, c_ref, *refs):
        for t in range(n):
            r2 = refs[n + t]
            own = refs[t][...].astype(F32)
            refs[2 * n + t][...] = ((own + r2[0].astype(F32)) + r2[1].astype(F32)) + r2[2].astype(F32)

    return pl.pallas_call(
        body, name="rs_add_chips", out_shape=out_shape,
        grid_spec=pltpu.PrefetchScalarGridSpec(num_scalar_prefetch=2, grid=(RS_CHUNKS,), in_specs=in_specs,
                                               out_specs=out_specs),
        compiler_params=_params(1))(myq.reshape(1).astype(jnp.int32), c.reshape(1).astype(jnp.int32), *ps, *r2s)


def _rs_share_halves(gs):
    n = len(gs)

    def body(*refs):
        g_refs, send_sems, recv_sems = refs[n:2 * n], refs[2 * n], refs[2 * n + 1]
        x, y, c = _place()
        sib = (x, y, 1 - c)
        for t in range(n):
            pltpu.make_async_remote_copy(
                src_ref=g_refs[t].at[c], dst_ref=g_refs[t].at[c], send_sem=send_sems.at[t], recv_sem=recv_sems.at[t],
                device_id=sib, device_id_type=MESH).start()
        for t in range(n):
            cp = pltpu.make_async_remote_copy(
                src_ref=g_refs[t].at[c], dst_ref=g_refs[t].at[1 - c], send_sem=send_sems.at[t],
                recv_sem=recv_sems.at[t], device_id=sib, device_id_type=MESH)
            cp.wait_send()
            cp.wait_recv()

    return pl.pallas_call(
        body, name="rs_share_halves", in_specs=[ANY] * n, out_specs=[ANY] * n,
        out_shape=[jax.ShapeDtypeStruct(g.shape, g.dtype) for g in gs], input_output_aliases={t: t for t in range(n)},
        scratch_shapes=[pltpu.SemaphoreType.DMA((n,)), pltpu.SemaphoreType.DMA((n,))],
    )(*gs)


def _rs_begin(ts, meta, c, name):
    r1s = _rs_to_sibling(ts, meta)
    ps = _rs_add_sibling(ts, r1s, meta, c)
    return _rs_chips_start(ps, meta, name + "_start")


def _rs_end(handle, after, meta, c, myq, name):
    ps, r2s = _split_wait(handle, after, name + "_wait")
    return _rs_share_halves(_rs_add_chips(ps, r2s, meta, myq, c))


def _allreduce_small(buf):
    rows = buf.shape[0]
    ndev = 8

    def body(b_ref, o_ref, slots, send_sems, recv_sems):
        x, y, c = _place()
        me = 4 * x + 2 * y + c
        slots[me] = b_ref[...]
        for k in range(1, ndev):
            kx, ky, kc = (k >> 2) & 1, (k >> 1) & 1, k & 1
            peer = (x ^ kx, y ^ ky, c ^ kc)
            pltpu.make_async_remote_copy(
                src_ref=b_ref, dst_ref=slots.at[me], send_sem=send_sems.at[k - 1], recv_sem=recv_sems.at[k - 1],
                device_id=peer, device_id_type=MESH).start()
        for k in range(1, ndev):
            kx, ky, kc = (k >> 2) & 1, (k >> 1) & 1, k & 1
            cp = pltpu.make_async_remote_copy(
                src_ref=b_ref, dst_ref=slots.at[me ^ k], send_sem=send_sems.at[k - 1], recv_sem=recv_sems.at[k - 1],
                device_id=(x ^ kx, y ^ ky, c ^ kc), device_id_type=MESH)
            cp.wait_send()
            cp.wait_recv()
        acc = slots[0]
        for s in range(1, ndev):
            acc = acc + slots[s]
        o_ref[...] = acc

    vm = pl.BlockSpec(memory_space=pltpu.VMEM)
    return pl.pallas_call(
        body, name="allreduce_small", in_specs=[vm], out_specs=vm, out_shape=jax.ShapeDtypeStruct((rows, 128), F32),
        scratch_shapes=[pltpu.VMEM((ndev, rows, 128), F32), pltpu.SemaphoreType.DMA((ndev - 1,)),
                        pltpu.SemaphoreType.DMA((ndev - 1,))],
        compiler_params=pltpu.CompilerParams(vmem_limit_bytes=VMEM_LIMIT))(buf)


RS_META = [("col", D // 2, IN_PAD), ("row", D // 8, D), ("row", D // 8, D), ("row", D // 8, D),
           ("col", D // 2, 2 * FFN // N_CHIPS), ("row", FFN // 8, D)]
SMALL_ROWS = 156


def _rope_tables(t_):
    half = SW_DIM // 2
    inv_freq = ROPE_THETA ** (-jnp.arange(half, dtype=F32) / half)
    ang = jnp.arange(t_, dtype=F32)[:, None] * inv_freq[None, :]
    return jnp.concatenate([jnp.cos(ang)] * 2, axis=1), jnp.concatenate([jnp.sin(ang)] * 2, axis=1)


def _orig_cols(padded, a, b):
    out = []
    for q in range(N_CHIPS):
        lo, hi = max(a, q * IN_SHARD), min(b, (q + 1) * IN_SHARD)
        if lo < hi:
            out.append(padded[:, q * IN_PAD + lo - q * IN_SHARD:q * IN_PAD + hi - q * IN_SHARD])
    return out


_ORIG_SEGMENTS = [(0, R_BG, "main", 0), (R_BG, R_SW, "bg", 0), (R_SW, R_GATES, "main", O_QS), (R_GATES, IN_COLS, "main", O_GA)]


def _to_padded_shards(main, bg):
    zeros = jnp.zeros((main.shape[0], IN_PAD - IN_SHARD), main.dtype)
    parts = []
    for q in range(N_CHIPS):
        for a, b, src, s0 in _ORIG_SEGMENTS:
            lo, hi = max(a, q * IN_SHARD), min(b, (q + 1) * IN_SHARD)
            if lo < hi:
                parts.append((main if src == "main" else bg)[:, s0 + lo - a:s0 + hi - a])
        parts.append(zeros)
    return jnp.concatenate(parts, axis=1)


def _lane_row(v16):
    return jnp.pad(v16.reshape(1, 2 * DN_HEADS), ((0, 0), (2 * DN_HEADS, 128 - 4 * DN_HEADS)))


def _pack_small(g):
    pad16 = jnp.pad(g["sinks"], ((0, 0), (0, 128 - SW_HEADS)))
    return jnp.concatenate([g["conv"].reshape(-1, 128), g["ln1g"].reshape(-1, 128), g["ln1b"].reshape(-1, 128),
                            g["ln2g"].reshape(-1, 128), g["ln2b"].reshape(-1, 128), g["gnw"], g["arow"], g["dtrow"],
                            pad16], axis=0)


def _unpack_small(buf):
    nconv = DN_CONV * 3 * D // 128
    o = nconv
    out = dict(conv=buf[:o].reshape(DN_CONV, 3 * D))
    for name in ("ln1g", "ln1b", "ln2g", "ln2b"):
        out[name] = buf[o:o + 8].reshape(D)
        o += 8
    out["gnw"] = buf[o]
    out["a_log"] = buf[o + 1, 2 * DN_HEADS:4 * DN_HEADS].reshape(2, DN_HEADS)
    out["dt_bias"] = buf[o + 2, 2 * DN_HEADS:4 * DN_HEADS].reshape(2, DN_HEADS)
    out["sinks"] = buf[o + 3, :SW_HEADS]
    return out


def kernel(x, w_in, conv_w, a_log, dt_bias, dn_norm_w, sinks, w_branch_a, w_branch_b, w_out, ln1_g, ln1_b, w_gate_up, w_down, ln2_g, ln2_b, loss_target, m_w_in, m_conv_w, m_a_log, m_dt_bias, m_dn_norm_w, m_sinks, m_w_branch_a, m_w_branch_b, m_w_out, m_ln1_g, m_ln1_b, m_w_gate_up, m_w_down, m_ln2_g, m_ln2_b, v_w_in, v_conv_w, v_a_log, v_dt_bias, v_dn_norm_w, v_sinks, v_w_branch_a, v_w_branch_b, v_w_out, v_ln1_g, v_ln1_b, v_w_gate_up, v_w_down, v_ln2_g, v_ln2_b):
    xi, yi, ci = _place()
    myq = 2 * xi + yi
    t_ = x.shape[1]
    cos, sin = _rope_tables(t_)

    loc_in = jnp.pad(w_in.astype(BF16), ((0, 0), (0, 0), (0, IN_PAD - IN_SHARD)))
    locs = [loc_in, w_branch_a.astype(BF16), w_branch_b.astype(BF16), w_out.astype(BF16), w_gate_up.astype(BF16),
            w_down.astype(BF16), conv_w]
    kinds = ["col", "row", "row", "row", "col", "row", "col"]
    gathers = []
    for l in range(DEPTH):
        srcs = [a[l] for a in locs]
        if gathers:
            srcs[-1] = srcs[-1] + gathers[-1][1][0, 0]
        gathers.append(_gather_start(srcs, kinds, "gather_%d_start" % l))

    def in_weights(l, after):
        _, (full_in,) = _split_wait(tuple(part[:1] for part in gathers[l][0]), after, "gather_%d_wait_in" % l)
        cols = lambda a, b: _orig_cols(full_in, a, b)
        return dict(
            in_main=jnp.concatenate(cols(0, R_BG) + cols(R_GATES, IN_COLS) + cols(R_SW, R_GATES), axis=1),
            in_bg=jnp.pad(jnp.concatenate(cols(R_BG, R_SW), axis=1), ((0, 0), (0, 128 - 4 * DN_HEADS))))

    def rest_weights(l, after):
        _, (full_a, full_b, full_o, full_gu, full_d, full_conv) = _split_wait(
            tuple(part[1:] for part in gathers[l][0]), after, "gather_%d_wait_rest" % l)
        return dict(
            conv=full_conv, arow=_lane_row(a_log[l]), dtrow=_lane_row(dt_bias[l]), gnw=dn_norm_w[l][None],
            sinks=sinks[l][None], a=full_a.reshape(D, D), b=full_b.reshape(D, D), o=full_o.reshape(D, D),
            ln1g=ln1_g[l][None], ln1b=ln1_b[l][None], gu=full_gu, d=full_d.reshape(FFN, D),
            ln2g=ln2_g[l][None], ln2b=ln2_b[l][None])

    h = x[0]
    hm = h.astype(_MXU)
    residuals = []
    for l in range(DEPTH):
        win = in_weights(l, gathers[-1][1] if l == 0 else h)
        h, hm, res = _layer_fwd(h, hm, win, functools.partial(rest_weights, l), cos, sin)
        residuals.append(res)
    dh, sq = _loss_head(h, loss_target[0])
    loss = lax.psum((0.5 / D) * jnp.sum(sq), ("x", "y", "c"))

    big = [None] * DEPTH
    small = [None] * DEPTH
    pending = None
    for l in reversed(range(DEPTH)):
        dh, g = _layer_bwd(dh, residuals[l], residuals[l]["w"], cos, sin)
        if pending is not None:
            big[pending[0]] = _rs_end(pending[1], dh, RS_META, ci, myq, "rs_chips_%d" % pending[0])
        g_in = _to_padded_shards(g["in_main"], g["in_bg"])
        handle, token = _rs_begin([g_in, g["a"], g["b"], g["o"], g["gu"], g["d"]], RS_META, ci, "rs_chips_%d" % l)
        pending = (l, handle)
        dh = dh + token[0, 0]
        small[l] = _pack_small(g)
    tot = _allreduce_small(jnp.concatenate(small, axis=0))
    big[pending[0]] = _rs_end(pending[1], tot, RS_META, ci, myq, "rs_chips_%d" % pending[0])
    sm = [_unpack_small(tot[l * SMALL_ROWS:(l + 1) * SMALL_ROWS]) for l in range(DEPTH)]
    stack = lambda name: jnp.stack([s[name] for s in sm], axis=0)

    grads = dict(
        w_in=jnp.stack([big[l][0].reshape(D, IN_PAD)[:, :IN_SHARD] for l in range(DEPTH)]),
        conv_w=lax.dynamic_slice_in_dim(stack("conv"), myq * (3 * D // N_CHIPS), 3 * D // N_CHIPS, axis=2),
        a_log=stack("a_log"), dt_bias=stack("dt_bias"), dn_norm_w=stack("gnw"), sinks=stack("sinks"),
        w_branch_a=jnp.stack([big[l][1].reshape(D // N_CHIPS, D) for l in range(DEPTH)]),
        w_branch_b=jnp.stack([big[l][2].reshape(D // N_CHIPS, D) for l in range(DEPTH)]),
        w_out=jnp.stack([big[l][3].reshape(D // N_CHIPS, D) for l in range(DEPTH)]),
        ln1_g=stack("ln1g"), ln1_b=stack("ln1b"),
        w_gate_up=jnp.stack([big[l][4].reshape(D, 2 * FFN // N_CHIPS) for l in range(DEPTH)]),
        w_down=jnp.stack([big[l][5].reshape(FFN // N_CHIPS, D) for l in range(DEPTH)]),
        ln2_g=stack("ln2g"), ln2_b=stack("ln2b"))
    weights = dict(w_in=w_in, conv_w=conv_w, a_log=a_log, dt_bias=dt_bias, dn_norm_w=dn_norm_w, sinks=sinks,
                   w_branch_a=w_branch_a, w_branch_b=w_branch_b, w_out=w_out, ln1_g=ln1_g, ln1_b=ln1_b,
                   w_gate_up=w_gate_up, w_down=w_down, ln2_g=ln2_g, ln2_b=ln2_b)
    ms = dict(w_in=m_w_in, conv_w=m_conv_w, a_log=m_a_log, dt_bias=m_dt_bias, dn_norm_w=m_dn_norm_w, sinks=m_sinks,
              w_branch_a=m_w_branch_a, w_branch_b=m_w_branch_b, w_out=m_w_out, ln1_g=m_ln1_g, ln1_b=m_ln1_b,
              w_gate_up=m_w_gate_up, w_down=m_w_down, ln2_g=m_ln2_g, ln2_b=m_ln2_b)
    vs = dict(w_in=v_w_in, conv_w=v_conv_w, a_log=v_a_log, dt_bias=v_dt_bias, dn_norm_w=v_dn_norm_w, sinks=v_sinks,
              w_branch_a=v_w_branch_a, w_branch_b=v_w_branch_b, w_out=v_w_out, ln1_g=v_ln1_g, ln1_b=v_ln1_b,
              w_gate_up=v_w_gate_up, w_down=v_w_down, ln2_g=v_ln2_g, ln2_b=v_ln2_b)
    names = list(weights)
    upd = {n: _adamw("adamw_" + n, weights[n], grads[n], ms[n], vs[n]) for n in names}
    return (loss, dh[None], *[grads[n] for n in names], *[upd[n][0] for n in names], *[upd[n][1] for n in names],
            *[upd[n][2] for n in names])
```

```python
import functools

import jax
import jax.numpy as jnp
from jax import lax
from jax.experimental import pallas as pl
from jax.experimental.pallas import tpu as pltpu

F32 = jnp.float32
BF16 = jnp.bfloat16
_MXU = BF16

D = 1024
DEPTH = 4
DN_HEADS = 8
DN_DIM = 128
DN_CONV = 5
CHUNK = 64
SW_HEADS = 16
SW_KV = 4
SW_DIM = 64
SW_GRP = SW_HEADS // SW_KV
SW_BLOCK = 128
ROPE_THETA = 10000.0
FFN = 2816
ALPHA = (2.0 * DEPTH) ** 0.25
LN_EPS = 1e-5
RMS_EPS = 1e-6
IN_COLS = 7712
O_Z, O_GA, O_GB, O_QS, O_KS, O_VS, N_MAIN = 3072, 4096, 5120, 6144, 7168, 7424, 7680
R_BG, R_SW, R_GATES = 4096, 4128, 5664
N_CHIPS = 4
IN_SHARD = IN_COLS // N_CHIPS
IN_PAD = 2048
ADAM_LR, ADAM_B1, ADAM_B2, ADAM_EPS, ADAM_WD, ADAM_STEP = 0.001, 0.9, 0.999, 1e-08, 0.01, 10
VMEM_LIMIT = 52 * 1024 * 1024
MESH = pl.DeviceIdType.MESH
ANY = pl.BlockSpec(memory_space=pl.ANY)


def _params(n_grid, **kw):
    return pltpu.CompilerParams(dimension_semantics=("arbitrary",) * n_grid, vmem_limit_bytes=VMEM_LIMIT, **kw)


def _full(a):
    nd = a.ndim
    return pl.BlockSpec(a.shape, lambda *_, nd=nd: (0,) * nd)


def _raw_dot(a, b, ca, cb):
    return lax.dot_general(a.astype(_MXU), b.astype(_MXU), (((ca,), (cb,)), ((), ())), preferred_element_type=F32)


@jax.custom_vjp
def _nn(a, b):
    return _raw_dot(a, b, 1, 0)


@jax.custom_vjp
def _nt(a, b):
    return _raw_dot(a, b, 1, 1)


@jax.custom_vjp
def _tn(a, b):
    if a.shape[1] > b.shape[1]:
        return _raw_dot(b, a, 0, 0).T
    return _raw_dot(a, b, 0, 0)


_nn.defvjp(lambda a, b: (_nn(a, b), (a, b)), lambda r, g: (_nt(g, r[1]), _tn(r[0], g)))
_nt.defvjp(lambda a, b: (_nt(a, b), (a, b)), lambda r, g: (_nn(g, r[1]), _tn(g, r[0])))
_tn.defvjp(lambda a, b: (_tn(a, b), (a, b)), lambda r, g: (_nt(r[1], g), _nn(r[0], g)))


def _hdot(a, b, ca=1, cb=0):
    ah, bh = a.astype(BF16), b.astype(BF16)
    al, bl = (a - ah.astype(F32)).astype(BF16), (b - bh.astype(F32)).astype(BF16)
    dot = lambda u, v: lax.dot_general(u, v, (((ca,), (cb,)), ((), ())), preferred_element_type=F32)
    return dot(ah, bh) + (dot(ah, bl) + dot(al, bh))


def _inv_impl(mats):
    n = mats[0].shape[0]
    eye = (lax.broadcasted_iota(jnp.int32, (n, n), 0) == lax.broadcasted_iota(jnp.int32, (n, n), 1)).astype(F32)
    ps = [-a for a in mats]
    ts = [eye + p for p in ps]
    for _ in range(max(1, (n - 1).bit_length()) - 1):
        ps = [_hdot(p, p) for p in ps]
        ts = [t + _hdot(t, p) for t, p in zip(ts, ps)]
    return tuple(ts)


@jax.custom_vjp
def _inv(mats):
    return _inv_impl(mats)


def _inv_fwd(mats):
    ts = _inv_impl(mats)
    return ts, ts


def _inv_bwd(ts, gs):
    xs = [_hdot(t, g, 0, 0) for t, g in zip(ts, gs)]
    return (tuple(-_hdot(x, t, 1, 1) for x, t in zip(xs, ts)),)


_inv.defvjp(_inv_fwd, _inv_bwd)


@jax.custom_vjp
def _inv_saved(mats, saved):
    return saved


_inv_saved.defvjp(lambda mats, saved: (saved, saved),
                  lambda ts, gs: (_inv_bwd(ts, gs)[0], tuple(jnp.zeros_like(t) for t in ts)))


def _tile(n, cap):
    if n <= cap:
        return n
    best = [t for t in range(128, cap + 1, 128) if n % t == 0]
    assert best, (n, cap)
    return best[-1]


def _mm(a, b, *, name, ta=False, tb=False, add=None, tm=1024, tn=1024, tk=1024):
    if ta:
        k_, m_ = a.shape
    else:
        m_, k_ = a.shape
    n_ = b.shape[0] if tb else b.shape[1]
    tm, tn, tk = _tile(m_, tm), _tile(n_, tn), _tile(k_, tk)
    nk = k_ // tk
    has_add = add is not None

    def body(*refs):
        a_ref, b_ref = refs[:2]
        add_ref = refs[2] if has_add else None
        o_ref = refs[3] if has_add else refs[2]
        part = _raw_dot(a_ref[...], b_ref[...], 0 if ta else 1, 1 if tb else 0)
        if nk == 1:
            o_ref[...] = part + add_ref[...] if has_add else part
            return
        acc = refs[-1]
        k = pl.program_id(2)

        @pl.when(k == 0)
        def _():
            acc[...] = part

        @pl.when(jnp.logical_and(k > 0, k < nk - 1))
        def _():
            acc[...] += part

        @pl.when(k == nk - 1)
        def _():
            o_ref[...] = acc[...] + part + add_ref[...] if has_add else acc[...] + part

    a_spec = pl.BlockSpec((tk, tm), lambda i, j, k: (k, i)) if ta else pl.BlockSpec((tm, tk), lambda i, j, k: (i, k))
    b_spec = pl.BlockSpec((tn, tk), lambda i, j, k: (j, k)) if tb else pl.BlockSpec((tk, tn), lambda i, j, k: (k, j))
    o_spec = pl.BlockSpec((tm, tn), lambda i, j, k: (i, j))
    in_specs = [a_spec, b_spec] + ([o_spec] if has_add else [])
    args = (a, b) + ((add,) if has_add else ())
    return pl.pallas_call(
        body, name=name, grid=(m_ // tm, n_ // tn, nk), in_specs=in_specs, out_specs=o_spec,
        out_shape=jax.ShapeDtypeStruct((m_, n_), F32),
        scratch_shapes=[pltpu.VMEM((tm, tn), F32)] if nk > 1 else [],
        compiler_params=_params(3))(*args)


def _row_spec(tb, w, c0, percol):
    return pl.BlockSpec((tb, w), lambda i, j, c0=c0, pc=percol: (i, c0 + (j if pc else 0)))


def _stage_fwd(f, name, rows, params, outs, tb, ncol=1):
    t_ = rows[0][0].shape[0]
    nr, npar = len(rows), len(params)

    def body(*refs):
        res = f(*[r[...].astype(F32) for r in refs[:nr + npar]])
        for o_ref, val in zip(refs[nr + npar:], res):
            o_ref[...] = val.astype(o_ref.dtype)

    return pl.pallas_call(
        body, name=name, grid=(t_ // tb, ncol),
        in_specs=[_row_spec(tb, w, c0, pc) for (_, w, c0, pc) in rows] + [_full(p) for p in params],
        out_specs=[pl.BlockSpec((tb, w), lambda i, j: (i, j)) for w, _ in outs],
        out_shape=[jax.ShapeDtypeStruct((t_, w * ncol), dt) for w, dt in outs],
        compiler_params=_params(2))(*[r[0] for r in rows], *params)


def _into(dest, tb, width, t_, ncol, dtype):
    if dest is None:
        return pl.BlockSpec((tb, width), lambda i, j: (i, j)), jax.ShapeDtypeStruct((t_, width * ncol), dtype), None
    buf, total, c0 = dest
    return (pl.BlockSpec((tb, width), lambda i, j, c0=c0: (i, c0 + j)), jax.ShapeDtypeStruct((t_, total), dtype), buf)


def _stage_bwd(f, name, rows, params, douts, tb, ncol=1, cat=None, dtypes=None, dest=None):
    t_ = rows[0][0].shape[0]
    nr, npar, nd = len(rows), len(params), len(douts)
    cat = cat if cat is not None else [[r] for r in range(nr)]
    dtypes = dtypes if dtypes is not None else [F32] * len(cat)
    dest = dest or {}
    assert ncol == 1 or all(len(g) == 1 and rows[g[0]][3] for g in cat)
    nin = nr + npar + nd

    def body(*refs):
        ins = [r[...].astype(F32) for r in refs[:nr + npar]]
        dvals = tuple(r[...].astype(F32) for r in refs[nr + npar:nin])
        out_refs = refs[nin + len(aliased):]
        _, vjp = jax.vjp(f, *ins)
        grads = vjp(dvals)
        for o_ref, grp in zip(out_refs[:len(cat)], cat):
            val = grads[grp[0]] if len(grp) == 1 else jnp.concatenate([grads[r] for r in grp], axis=-1)
            o_ref[...] = val.astype(o_ref.dtype)
        first = jnp.logical_and(pl.program_id(0) == 0, pl.program_id(1) == 0)
        for p_ref, gp in zip(out_refs[len(cat):], grads[nr:]):
            @pl.when(first)
            def _(p_ref=p_ref):
                p_ref[...] = jnp.zeros_like(p_ref)
            p_ref[...] += gp

    gw = [sum(rows[r][1] for r in grp) for grp in cat]
    out_specs, out_shape, aliased, aliases = [], [], [], {}
    for gi, (w, dt) in enumerate(zip(gw, dtypes)):
        spec, shape, buf = _into(dest.get(gi), tb, w, t_, ncol, dt)
        out_specs.append(spec)
        out_shape.append(shape)
        if buf is not None:
            aliases[nin + len(aliased)] = gi
            aliased.append(buf)
    return pl.pallas_call(
        body, name=name, grid=(t_ // tb, ncol),
        in_specs=[_row_spec(tb, w, c0, pc) for (_, w, c0, pc) in rows] + [_full(p) for p in params]
        + [pl.BlockSpec((tb, d.shape[1] // ncol), lambda i, j: (i, j)) for d in douts] + [ANY] * len(aliased),
        out_specs=out_specs + [_full(p) for p in params],
        out_shape=out_shape + [jax.ShapeDtypeStruct(p.shape, F32) for p in params],
        input_output_aliases=aliases,
        compiler_params=_params(2))(*[r[0] for r in rows], *params, *douts, *aliased)


def _ln_f(x, y, g, b):
    u = ALPHA * x + y
    c = u - jnp.mean(u, axis=-1, keepdims=True)
    var = jnp.mean(c * c, axis=-1, keepdims=True)
    return (c * lax.rsqrt(var + LN_EPS) * g + b,)


def _ln_f2(x, y, g, b):
    out, = _ln_f(x, y, g, b)
    return out, out


def _swiglu_f(gate, up):
    return (jax.nn.silu(gate) * up,)


def _merge_f(ya, yb, ga, gb):
    return (jax.nn.sigmoid(ga) * ya + jax.nn.sigmoid(gb) * yb,)


def _gnorm_f(of, ob, z, w):
    o = of + ob
    return (o * lax.rsqrt(jnp.mean(o * o, axis=-1, keepdims=True) + RMS_EPS) * w * jax.nn.silu(z),)


def _bg_f(x, arow, dtrow):
    lane = lax.broadcasted_iota(jnp.int32, x.shape, 1)
    beta = jax.nn.sigmoid(x)
    g = -jnp.exp(arow) * jax.nn.softplus(x + dtrow)
    return (jnp.where(lane < 16, beta, jnp.where(lane < 32, g, 0.0)),)


PREP_ROWS = 512
PAD = 8


def _prep_f(part, w, *wins):
    xc = wins[0] * w[0:1, :]
    for k in range(1, DN_CONV):
        xc = xc + wins[k] * w[k:k + 1, :]
    a = jax.nn.silu(xc)
    nrm = a * lax.rsqrt(jnp.sum(a * a, axis=-1, keepdims=True) + RMS_EPS)
    return jnp.where(part == 0, nrm * (DN_DIM ** -0.5), jnp.where(part == 1, nrm, a))


def _windows(pad_ref, r0, rows):
    return [pad_ref[PAD + r0 - 2 + k:PAD + r0 - 2 + k + rows, :] for k in range(DN_CONV)]


def _prep_fwd(pm, conv):
    t_ = pm.shape[0]
    rows = min(PREP_ROWS, t_)

    def body(x_ref, w_ref, o_ref, pad_ref):
        part = pl.program_id(0) // DN_HEADS
        pad_ref[0:PAD, :] = jnp.zeros((PAD, DN_DIM), F32)
        pad_ref[PAD + t_:2 * PAD + t_, :] = jnp.zeros((PAD, DN_DIM), F32)
        pad_ref[PAD:PAD + t_, :] = x_ref[...]
        w = w_ref[...]
        for r in range(t_ // rows):
            o_ref[r * rows:(r + 1) * rows, :] = _prep_f(part, w, *_windows(pad_ref, r * rows, rows))

    ncb = 3 * DN_HEADS
    return pl.pallas_call(
        body, name="prep_fwd", grid=(ncb,),
        in_specs=[pl.BlockSpec((t_, DN_DIM), lambda j: (0, j)), pl.BlockSpec((DN_CONV, DN_DIM), lambda j: (0, j))],
        out_specs=pl.BlockSpec((t_, DN_DIM), lambda j: (0, j)),
        out_shape=jax.ShapeDtypeStruct((t_, ncb * DN_DIM), F32),
        scratch_shapes=[pltpu.VMEM((t_ + 2 * PAD, DN_DIM), F32)],
        compiler_params=_params(1))(pm, conv)


def _prep_bwd(pm, conv, dout, dpm):
    t_ = pm.shape[0]
    rows = min(PREP_ROWS, t_)

    def body(x_ref, w_ref, d_ref, _, dx_ref, dw_ref, pad_ref, dpad_ref):
        part = pl.program_id(0) // DN_HEADS
        pad_ref[0:PAD, :] = jnp.zeros((PAD, DN_DIM), F32)
        pad_ref[PAD + t_:2 * PAD + t_, :] = jnp.zeros((PAD, DN_DIM), F32)
        pad_ref[PAD:PAD + t_, :] = x_ref[...]
        dpad_ref[...] = jnp.zeros_like(dpad_ref)
        w = w_ref[...]
        dw = jnp.zeros((DN_CONV, DN_DIM), F32)
        for r in range(t_ // rows):
            r0 = r * rows
            _, vjp = jax.vjp(functools.partial(_prep_f, part), w, *_windows(pad_ref, r0, rows))
            grads = vjp(d_ref[r0:r0 + rows, :])
            dw = dw + grads[0]
            for k in range(DN_CONV):
                lo = PAD + r0 - 2 + k
                dpad_ref[lo:lo + rows, :] += grads[1 + k]
        dx_ref[...] = dpad_ref[PAD:PAD + t_, :].astype(dx_ref.dtype)
        dw_ref[...] = dw

    ncb = 3 * DN_HEADS
    col = pl.BlockSpec((t_, DN_DIM), lambda j: (0, j))
    wsp = pl.BlockSpec((DN_CONV, DN_DIM), lambda j: (0, j))
    return pl.pallas_call(
        body, name="prep_bwd", grid=(ncb,), in_specs=[col, wsp, col, ANY], out_specs=[col, wsp],
        out_shape=[jax.ShapeDtypeStruct(dpm.shape, dpm.dtype), jax.ShapeDtypeStruct((DN_CONV, ncb * DN_DIM), F32)],
        scratch_shapes=[pltpu.VMEM((t_ + 2 * PAD, DN_DIM), F32), pltpu.VMEM((t_ + 2 * PAD, DN_DIM), F32)],
        input_output_aliases={3: 0}, compiler_params=_params(1))(pm, conv, dout, dpm)


def _dn_chunk(sgns, qs, ks, vs, grows, brows, tsaved=None, with_t=False):
    c = qs[0].shape[0]
    i = lax.broadcasted_iota(jnp.int32, (c, c), 0)
    j = lax.broadcasted_iota(jnp.int32, (c, c), 1)
    eye = i == j
    incl = {s: (i - j) * int(s) >= 0 for s in set(sgns)}
    strict = {s: (i - j) * int(s) > 0 for s in set(sgns)}
    gcs = [jnp.sum(jnp.where(incl[s], g, 0.0), axis=1, keepdims=True) for s, g in zip(sgns, grows)]
    grs = [jnp.sum(jnp.where(eye, gc, 0.0), axis=0, keepdims=True) for gc in gcs]
    bcs = [jnp.sum(jnp.where(eye, b, 0.0), axis=1, keepdims=True) for b in brows]
    gls = [jnp.sum(g, axis=1, keepdims=True) for g in grows]
    decs = [jnp.exp(jnp.where(incl[s], gc - gr, -1e30)) for s, gc, gr in zip(sgns, gcs, grs)]
    kks = [_nt(k, k) for k in ks]
    amats = tuple(jnp.where(strict[s], bc * kk * dec, 0.0) for s, bc, kk, dec in zip(sgns, bcs, kks, decs))
    tinvs = _inv(amats) if tsaved is None else _inv_saved(amats, tsaved)
    egcs = [jnp.exp(gc) for gc in gcs]
    us = [_nn(t, v * bc) for t, v, bc in zip(tinvs, vs, bcs)]
    ws = [_nn(t, k * (bc * egc)) for t, k, bc, egc in zip(tinvs, ks, bcs, egcs)]
    qks = [_nt(q, k) * dec for q, k, dec in zip(qs, ks, decs)]
    qds = [q * egc for q, egc in zip(qs, egcs)]
    kds = [k * jnp.exp(gl - gc) for k, gl, gc in zip(ks, gls, gcs)]
    res = tuple(us), tuple(ws), tuple(qks), tuple(qds), tuple(kds)
    return res + (tinvs,) if with_t else res


def _dn_step(us, ws, qks, qds, kds, grows, ss):
    gls = [jnp.exp(jnp.sum(g, axis=1, keepdims=True)) for g in grows]
    wss = [_nn(w, s) for w, s in zip(ws, ss)]
    qss = [_nn(qd, s) for qd, s in zip(qds, ss)]
    vns = [u - x for u, x in zip(us, wss)]
    os_ = [a + _nn(qk, vn) for a, qk, vn in zip(qss, qks, vns)]
    s2s = [s * gl + _tn(kd, vn) for s, gl, kd, vn in zip(ss, gls, kds, vns)]
    return tuple(os_), tuple(s2s)


def _hs(h):
    return slice(h * DN_DIM, (h + 1) * DN_DIM)


_DIR_SGN = (1, -1)
_A_OUT = 5
_PROBLEMS = [(d, h) for d in range(2) for h in range(DN_HEADS)]
_SGNS = [_DIR_SGN[d] for d, _ in _PROBLEMS]


def _chunk_inputs(q_ref, k_ref, v_ref, g_ref, b_ref):
    heads = lambda ref: tuple(ref[:, _hs(h)].astype(F32) for _, h in _PROBLEMS)
    rows = lambda ref: tuple(ref[d, 0, h:h + 1, :] for d, h in _PROBLEMS)
    return heads(q_ref), heads(k_ref), heads(v_ref), rows(g_ref), rows(b_ref)


def _dn_a_fwd(qkv, grows, brows):
    t_ = qkv.shape[0]
    nch = t_ // CHUNK

    def body(q_ref, k_ref, v_ref, g_ref, b_ref, *outs):
        us, ws, qks, qds, kds, tinvs = _dn_chunk(_SGNS, *_chunk_inputs(q_ref, k_ref, v_ref, g_ref, b_ref), with_t=True)
        for p, (d, h) in enumerate(_PROBLEMS):
            u_ref, w_ref, qk_ref, qd_ref, kd_ref = outs[d * _A_OUT:(d + 1) * _A_OUT]
            u_ref[:, _hs(h)], w_ref[:, _hs(h)], qk_ref[0, h] = us[p], ws[p].astype(_MXU), qks[p].astype(_MXU)
            qd_ref[:, _hs(h)], kd_ref[:, _hs(h)] = qds[p].astype(_MXU), kds[p].astype(_MXU)
            outs[2 * _A_OUT + d][0, h] = tinvs[p]

    rspec = pl.BlockSpec((2, 1, DN_HEADS, CHUNK), lambda c: (0, c, 0, 0))
    big = pl.BlockSpec((CHUNK, D), lambda c: (c, 0))
    qks = pl.BlockSpec((1, DN_HEADS, CHUNK, CHUNK), lambda c: (c, 0, 0, 0))
    bigs = lambda dt: jax.ShapeDtypeStruct((t_, D), dt)
    qksh = lambda dt: jax.ShapeDtypeStruct((nch, DN_HEADS, CHUNK, CHUNK), dt)
    res = pl.pallas_call(
        body, name="dn_a_fwd", grid=(nch,),
        in_specs=[pl.BlockSpec((CHUNK, D), lambda c, p=p: (c, p)) for p in range(3)] + [rspec, rspec],
        out_specs=[big, big, qks, big, big] * 2 + [qks, qks],
        out_shape=[bigs(F32), bigs(_MXU), qksh(_MXU), bigs(_MXU), bigs(_MXU)] * 2 + [qksh(F32)] * 2,
        compiler_params=_params(1))(qkv, qkv, qkv, grows, brows)
    return res[:2 * _A_OUT], res[2 * _A_OUT:]


def _dn_a_bwd(qkv, grows, brows, tinv, dres, dg_b):
    t_ = qkv.shape[0]
    nch = t_ // CHUNK

    def body(q_ref, k_ref, v_ref, g_ref, b_ref, tf_ref, tb_ref, *rest):
        dins, dgb_ref, (dqkv_ref, dg_ref, db_ref) = rest[:2 * _A_OUT], rest[2 * _A_OUT], rest[2 * _A_OUT + 1:]
        tsaved = tuple((tf_ref, tb_ref)[d][0, h] for d, h in _PROBLEMS)
        _, vjp = jax.vjp(functools.partial(_dn_chunk, _SGNS, tsaved=tsaved),
                         *_chunk_inputs(q_ref, k_ref, v_ref, g_ref, b_ref))
        cots = []
        for o in range(_A_OUT):
            cots.append(tuple(dins[d * _A_OUT + o][0, h] if o == 2 else dins[d * _A_OUT + o][:, _hs(h)]
                              for d, h in _PROBLEMS))
        gq, gk, gv, gg, gb = vjp(tuple(cots))
        for p, (d, h) in enumerate(_PROBLEMS):
            dg_ref[d, 0, h:h + 1, :] = gg[p] + dgb_ref[d, 0, h:h + 1, :]
            db_ref[d, 0, h:h + 1, :] = gb[p]
        for h in range(DN_HEADS):
            dqkv_ref[:, _hs(h)] = gq[h] + gq[DN_HEADS + h]
            dqkv_ref[:, _hs(DN_HEADS + h)] = gk[h] + gk[DN_HEADS + h]
            dqkv_ref[:, _hs(2 * DN_HEADS + h)] = gv[h] + gv[DN_HEADS + h]

    rspec = pl.BlockSpec((2, 1, DN_HEADS, CHUNK), lambda c: (0, c, 0, 0))
    big = pl.BlockSpec((CHUNK, D), lambda c: (c, 0))
    qks = pl.BlockSpec((1, DN_HEADS, CHUNK, CHUNK), lambda c: (c, 0, 0, 0))
    rsh = jax.ShapeDtypeStruct(grows.shape, F32)
    return pl.pallas_call(
        body, name="dn_a_bwd", grid=(nch,),
        in_specs=[pl.BlockSpec((CHUNK, D), lambda c, p=p: (c, p)) for p in range(3)] + [rspec, rspec, qks, qks]
        + [big, big, qks, big, big] * 2 + [rspec],
        out_specs=[pl.BlockSpec((CHUNK, 3 * D), lambda c: (c, 0)), rspec, rspec],
        out_shape=[jax.ShapeDtypeStruct((t_, 3 * D), F32), rsh, rsh],
        compiler_params=_params(1))(qkv, qkv, qkv, grows, brows, *tinv, *dres, dg_b)


def _dir_specs(nch):
    def cidx(d):
        return (lambda n: n) if d == 0 else (lambda n: nch - 1 - n)
    out = []
    for d in range(2):
        ci = cidx(d)
        big = pl.BlockSpec((CHUNK, D), lambda n, ci=ci: (ci(n), 0))
        qks = pl.BlockSpec((1, DN_HEADS, CHUNK, CHUNK), lambda n, ci=ci: (ci(n), 0, 0, 0))
        row = pl.BlockSpec((1, 1, DN_HEADS, CHUNK), lambda n, ci=ci, d=d: (d, ci(n), 0, 0))
        st = pl.BlockSpec((1, DN_HEADS, DN_DIM, DN_DIM), lambda n, ci=ci: (ci(n), 0, 0, 0))
        out.append(dict(big=big, qk=qks, row=row, st=st))
    return out


def _step_inputs(ins, per_dir):
    def pick(o):
        if o == 2:
            return tuple(ins[d * per_dir + o][0, h].astype(F32) for d, h in _PROBLEMS)
        if o == 5:
            return tuple(ins[d * per_dir + o][0, 0, h:h + 1, :] for d, h in _PROBLEMS)
        return tuple(ins[d * per_dir + o][:, _hs(h)].astype(F32) for d, h in _PROBLEMS)
    return [pick(o) for o in range(6)]


def _dn_b_fwd(ares, grows):
    t_ = ares[0].shape[0]
    nch = t_ // CHUNK
    sp = _dir_specs(nch)

    def body(*refs):
        ins, outs, s_ref = refs[:12], refs[12:16], refs[16]

        @pl.when(pl.program_id(0) == 0)
        def _():
            s_ref[...] = jnp.zeros_like(s_ref)

        ss = tuple(s_ref[p] for p in range(len(_PROBLEMS)))
        os_, s2s = _dn_step(*_step_inputs(ins, 6), ss)
        for p, (d, h) in enumerate(_PROBLEMS):
            outs[2 + d][0, h] = ss[p]
            outs[d][:, _hs(h)] = os_[p]
            s_ref[p] = s2s[p]

    in_specs, args = [], []
    for d in range(2):
        in_specs += [sp[d]["big"], sp[d]["big"], sp[d]["qk"], sp[d]["big"], sp[d]["big"], sp[d]["row"]]
        args += list(ares[d * _A_OUT:(d + 1) * _A_OUT]) + [grows]
    stsh = jax.ShapeDtypeStruct((nch, DN_HEADS, DN_DIM, DN_DIM), F32)
    osh = jax.ShapeDtypeStruct((t_, D), F32)
    return pl.pallas_call(
        body, name="dn_b_fwd", grid=(nch,), in_specs=in_specs,
        out_specs=[sp[0]["big"], sp[1]["big"], sp[0]["st"], sp[1]["st"]], out_shape=[osh, osh, stsh, stsh],
        scratch_shapes=[pltpu.VMEM((2 * DN_HEADS, DN_DIM, DN_DIM), F32)],
        compiler_params=_params(1))(*args)


def _dn_b_bwd(ares, grows, st_f, st_b, do):
    t_ = ares[0].shape[0]
    nch = t_ // CHUNK
    sp = _dir_specs(nch)
    rsp = [sp[1], sp[0]]

    def body(*refs):
        ins, outs, ds_ref = refs[:16], refs[16:28], refs[28]

        @pl.when(pl.program_id(0) == 0)
        def _():
            ds_ref[...] = jnp.zeros_like(ds_ref)

        ss = tuple(ins[d * 8 + 6][0, h] for d, h in _PROBLEMS)
        _, vjp = jax.vjp(_dn_step, *_step_inputs(ins, 8), ss)
        dos = tuple(ins[d * 8 + 7][:, _hs(h)] for d, h in _PROBLEMS)
        grads = vjp((dos, tuple(ds_ref[p] for p in range(len(_PROBLEMS)))))
        for p, (d, h) in enumerate(_PROBLEMS):
            du_ref, dw_ref, dqk_ref, dqd_ref, dkd_ref, dg_ref = outs[d * 6:(d + 1) * 6]
            du_ref[:, _hs(h)], dw_ref[:, _hs(h)], dqk_ref[0, h] = grads[0][p], grads[1][p], grads[2][p]
            dqd_ref[:, _hs(h)], dkd_ref[:, _hs(h)] = grads[3][p], grads[4][p]
            dg_ref[0, 0, h:h + 1, :] = grads[5][p]
            ds_ref[p] = grads[6][p]

    in_specs, args, out_specs, out_shape = [], [], [], []
    big_sh = jax.ShapeDtypeStruct((t_, D), F32)
    qk_sh = jax.ShapeDtypeStruct((nch, DN_HEADS, CHUNK, CHUNK), F32)
    row_sh = jax.ShapeDtypeStruct((1, nch, DN_HEADS, CHUNK), F32)
    for d in range(2):
        s = rsp[d]
        row0 = pl.BlockSpec((1, 1, DN_HEADS, CHUNK), lambda m, d=d: (0, (nch - 1 - m) if d == 0 else m, 0, 0))
        rowd = pl.BlockSpec((1, 1, DN_HEADS, CHUNK), lambda m, d=d: (d, (nch - 1 - m) if d == 0 else m, 0, 0))
        in_specs += [s["big"], s["big"], s["qk"], s["big"], s["big"], rowd, s["st"], s["big"]]
        args += list(ares[d * _A_OUT:(d + 1) * _A_OUT]) + [grows, (st_f, st_b)[d], do]
        out_specs += [s["big"], s["big"], s["qk"], s["big"], s["big"], row0]
        out_shape += [big_sh, big_sh, qk_sh, big_sh, big_sh, row_sh]
    res = pl.pallas_call(
        body, name="dn_b_bwd", grid=(nch,), in_specs=in_specs, out_specs=out_specs, out_shape=out_shape,
        scratch_shapes=[pltpu.VMEM((2 * DN_HEADS, DN_DIM, DN_DIM), F32)],
        compiler_params=_params(1))(*args)
    dares = list(res[0:5]) + list(res[6:11])
    return dares, jnp.concatenate([res[5], res[11]], axis=0)


@jax.custom_vjp
def _rot_half(x):
    half, width = SW_DIM // 2, x.shape[1]
    first = lax.broadcasted_iota(jnp.int32, x.shape, 1) % SW_DIM < half
    return jnp.where(first, -pltpu.roll(x, width - half, axis=1), pltpu.roll(x, half, axis=1))


_rot_half.defvjp(lambda x: (_rot_half(x), None), lambda _, g: (-_rot_half(g),))


def _rope(x, c, s):
    reps = x.shape[1] // c.shape[1]
    return x * jnp.tile(c, (1, reps)) + _rot_half(x) * jnp.tile(s, (1, reps))


def _rope_t(g, c, s):
    reps = g.shape[1] // c.shape[1]
    return g * jnp.tile(c, (1, reps)) - _rot_half(g * jnp.tile(s, (1, reps)))


_SW_SCALE = SW_DIM ** -0.5
_KV_HEADS = [[kvh * SW_GRP + g for g in range(SW_GRP)] for kvh in range(SW_KV)]


def _by_group(x):
    return [jnp.concatenate([x[:, h * SW_DIM:(h + 1) * SW_DIM] for h in hs], axis=0) for hs in _KV_HEADS]


def _from_groups(xs):
    return jnp.concatenate([x[g * SW_BLOCK:(g + 1) * SW_BLOCK] for x in xs for g in range(SW_GRP)], axis=-1)


def _attn_probs(blk, t_, cq, sq, ck, sk, q, kall, sinks):
    qgs = _by_group(_rope(q, cq, sq))
    kr = _rope(kall, ck, sk)
    khs = [kr[:, kvh * SW_DIM:(kvh + 1) * SW_DIM] for kvh in range(SW_KV)]
    nq, nk = SW_GRP * SW_BLOCK, 3 * SW_BLOCK
    qpos = lax.broadcasted_iota(jnp.int32, (nq, nk), 0) % SW_BLOCK
    krel = lax.broadcasted_iota(jnp.int32, (nq, nk), 1) - SW_BLOCK
    kglob = krel + blk * SW_BLOCK
    valid = (jnp.abs(qpos - krel) <= SW_BLOCK) & (kglob >= 0) & (kglob < t_)
    ss = [jnp.where(valid, _nt(qg * _SW_SCALE, kh), -1e30) for qg, kh in zip(qgs, khs)]
    snks = [jnp.concatenate([jnp.broadcast_to(sinks[:, h:h + 1], (SW_BLOCK, 1)) for h in hs], axis=0) for hs in _KV_HEADS]
    ms = [jnp.maximum(jnp.max(s, axis=-1, keepdims=True), snk) for s, snk in zip(ss, snks)]
    es = [jnp.exp(s - m) for s, m in zip(ss, ms)]
    esnks = [jnp.exp(snk - m) for snk, m in zip(snks, ms)]
    invs = [1.0 / (jnp.sum(e, axis=-1, keepdims=True) + esnk) for e, esnk in zip(es, esnks)]
    ps = [e * inv for e, inv in zip(es, invs)]
    return qgs, khs, ps, [esnk * inv for esnk, inv in zip(esnks, invs)]


def _attn_f(blk, t_, cq, sq, ck, sk, q, kp, ko, kn, vp, vo, vn, sinks):
    _, _, ps, _ = _attn_probs(blk, t_, cq, sq, ck, sk, q, jnp.concatenate([kp, ko, kn], axis=0), sinks)
    vall = jnp.concatenate([vp, vo, vn], axis=0)
    return _from_groups([_nn(p, vall[:, kvh * SW_DIM:(kvh + 1) * SW_DIM]) for kvh, p in enumerate(ps)])


def _attn_bwd_f(blk, t_, cq, sq, ck, sk, q, kp, ko, kn, vp, vo, vn, sinks, do):
    qgs, khs, ps, psinks = _attn_probs(blk, t_, cq, sq, ck, sk, q, jnp.concatenate([kp, ko, kn], axis=0), sinks)
    vall = jnp.concatenate([vp, vo, vn], axis=0)
    vhs = [vall[:, kvh * SW_DIM:(kvh + 1) * SW_DIM] for kvh in range(SW_KV)]
    dogs = _by_group(do)
    dvs = [_tn(p, dog) for p, dog in zip(ps, dogs)]
    dps = [_nt(dog, vh) for dog, vh in zip(dogs, vhs)]
    deltas = [jnp.sum(p * dp, axis=-1, keepdims=True) for p, dp in zip(ps, dps)]
    dss = [p * ((dp - delta) * _SW_SCALE) for p, dp, delta in zip(ps, dps, deltas)]
    dqr = _from_groups([_nn(ds, kh) for ds, kh in zip(dss, khs)])
    dkr = jnp.concatenate([_tn(ds, qg) for ds, qg in zip(dss, qgs)], axis=-1)
    dsnk = [-(psink * delta) for psink, delta in zip(psinks, deltas)]
    dsinks = jnp.concatenate([jnp.sum(d[g * SW_BLOCK:(g + 1) * SW_BLOCK], axis=0, keepdims=True)
                              for d in dsnk for g in range(SW_GRP)], axis=1)
    dq, dk, dv = _rope_t(dqr, cq, sq), _rope_t(dkr, ck, sk), jnp.concatenate(dvs, axis=-1)
    blocks = lambda a: [a[j * SW_BLOCK:(j + 1) * SW_BLOCK] for j in range(3)]
    return [dq] + blocks(dk) + blocks(dv) + [dsinks]


def _attn_specs(nb):
    prv = lambda i: jnp.maximum(i - 1, 0)
    nxt = lambda i: jnp.minimum(i + 1, nb - 1)
    rows = [lambda i: i, prv, lambda i: i, nxt]
    tab = [pl.BlockSpec((SW_BLOCK, 128), lambda i, r=r: (r(i), 0)) for r in rows]
    qs = pl.BlockSpec((SW_BLOCK, SW_HEADS * SW_DIM), lambda i: (i, O_QS // (SW_HEADS * SW_DIM)))
    kw = SW_KV * SW_DIM
    ks = [pl.BlockSpec((SW_BLOCK, kw), lambda i, r=r: (r(i), O_KS // kw)) for r in rows[1:]]
    vs = [pl.BlockSpec((SW_BLOCK, kw), lambda i, r=r: (r(i), O_VS // kw)) for r in rows[1:]]
    return tab, qs, ks, vs


def _attn_tables(refs):
    cq, cp, co, cn, sq, sp_, so, sn = [r[...] for r in refs]
    return cq, sq, jnp.concatenate([cp, co, cn], axis=0), jnp.concatenate([sp_, so, sn], axis=0)


def _attn_fwd(pm, cos, sin, sinks):
    t_ = pm.shape[0]
    nb = t_ // SW_BLOCK
    tab, qs, ks, vs = _attn_specs(nb)

    def body(*refs):
        tabs = _attn_tables(refs[:8])
        vals = [r[...] for r in refs[8:16]]
        refs[16][...] = _attn_f(pl.program_id(0), t_, *tabs, *vals).astype(refs[16].dtype)

    return pl.pallas_call(
        body, name="attn_fwd", grid=(nb,), in_specs=tab + tab + [qs] + ks + vs + [_full(sinks)],
        out_specs=pl.BlockSpec((SW_BLOCK, D), lambda i: (i, 0)), out_shape=jax.ShapeDtypeStruct((t_, D), _MXU),
        compiler_params=_params(1))(*([cos] * 4), *([sin] * 4), pm, pm, pm, pm, pm, pm, pm, sinks)


def _attn_bwd(pm, cos, sin, sinks, do, dpm):
    t_ = pm.shape[0]
    nb = t_ // SW_BLOCK
    tab, qs, ks, vs = _attn_specs(nb)
    kw = SW_KV * SW_DIM

    def body(*refs):
        tabs = _attn_tables(refs[:8])
        vals = [r[...] for r in refs[8:16]]
        do_ref, outs = refs[16], refs[18:]
        grads = _attn_bwd_f(pl.program_id(0), t_, *tabs, *vals, do_ref[...])
        for o_ref, g in zip(outs[:7], grads[:7]):
            o_ref[...] = g.astype(o_ref.dtype)

        @pl.when(pl.program_id(0) == 0)
        def _():
            outs[7][...] = jnp.zeros_like(outs[7])
        outs[7][...] += grads[7]

    own = lambda w: pl.BlockSpec((SW_BLOCK, w), lambda i: (i, 0))
    return pl.pallas_call(
        body, name="attn_bwd", grid=(nb,),
        in_specs=tab + tab + [qs] + ks + vs + [_full(sinks), own(D), ANY],
        out_specs=[pl.BlockSpec((SW_BLOCK, D), lambda i: (i, O_QS // D))] + [own(kw)] * 6 + [_full(sinks)],
        out_shape=[jax.ShapeDtypeStruct(dpm.shape, dpm.dtype)] + [jax.ShapeDtypeStruct((t_, kw), F32)] * 6
        + [jax.ShapeDtypeStruct(sinks.shape, F32)],
        input_output_aliases={17: 0},
        compiler_params=_params(1))(*([cos] * 4), *([sin] * 4), pm, pm, pm, pm, pm, pm, pm, sinks, do, dpm)


def _band_sum(kparts, vparts, dpm):
    t_, kw = kparts[1].shape
    nb = t_ // SW_BLOCK

    def body(kp, ko, kn, vp, vo, vn, _, out_ref):
        j = pl.program_id(0)
        band = lambda p, o, n: o[...] + jnp.where(j + 1 < nb, p[...], 0.0) + jnp.where(j > 0, n[...], 0.0)
        out_ref[...] = jnp.concatenate([band(kp, ko, kn), band(vp, vo, vn)], axis=-1).astype(out_ref.dtype)

    specs = [pl.BlockSpec((SW_BLOCK, kw), lambda j: (jnp.minimum(j + 1, nb - 1), 0)),
             pl.BlockSpec((SW_BLOCK, kw), lambda j: (j, 0)),
             pl.BlockSpec((SW_BLOCK, kw), lambda j: (jnp.maximum(j - 1, 0), 0))]
    return pl.pallas_call(
        body, name="band_sum", grid=(nb,), in_specs=specs * 2 + [ANY],
        out_specs=pl.BlockSpec((SW_BLOCK, 2 * kw), lambda j: (j, O_KS // (2 * kw))),
        out_shape=jax.ShapeDtypeStruct(dpm.shape, dpm.dtype), input_output_aliases={6: 0},
        compiler_params=_params(1))(*kparts, *vparts, dpm)


def _loss_head(y, target, tb=256):
    t_ = y.shape[0]

    def body(y_ref, t_ref, dy_ref, acc_ref):
        err = y_ref[...] - t_ref[...]
        dy_ref[...] = err * (1.0 / D)
        sq = (err * err).reshape(tb // 8, 8, D).sum(axis=0)
        part = sq[:, 0:128]
        for c in range(1, D // 128):
            part = part + sq[:, c * 128:(c + 1) * 128]

        @pl.when(pl.program_id(0) == 0)
        def _():
            acc_ref[...] = jnp.zeros_like(acc_ref)
        acc_ref[...] += part

    row = pl.BlockSpec((tb, D), lambda i: (i, 0))
    return pl.pallas_call(
        body, name="loss_head", grid=(t_ // tb,), in_specs=[row, row],
        out_specs=[row, pl.BlockSpec((8, 128), lambda i: (0, 0))],
        out_shape=[jax.ShapeDtypeStruct((t_, D), F32), jax.ShapeDtypeStruct((8, 128), F32)],
        compiler_params=_params(1))(y, target)


def _adamw(name, w, g, m, v):
    shape = w.shape
    cols = shape[-1]
    rows = w.size // cols
    w2, g2, m2, v2 = [a.reshape(rows, cols) for a in (w, g, m, v)]
    tb = rows
    while tb * cols * 4 > (1 << 20) and tb % 16 == 0:
        tb //= 2
    bc1 = 1.0 - ADAM_B1 ** ADAM_STEP
    bc2 = 1.0 - ADAM_B2 ** ADAM_STEP

    def body(w_ref, g_ref, m_ref, v_ref, d_ref, nm_ref, nv_ref):
        gv = g_ref[...]
        nm = ADAM_B1 * m_ref[...] + (1.0 - ADAM_B1) * gv
        nv = ADAM_B2 * v_ref[...] + (1.0 - ADAM_B2) * (gv * gv)
        d_ref[...] = -ADAM_LR * ((nm / bc1) / (jnp.sqrt(nv / bc2) + ADAM_EPS) + ADAM_WD * w_ref[...])
        nm_ref[...] = nm
        nv_ref[...] = nv

    spec = pl.BlockSpec((tb, cols), lambda i: (i, 0))
    sh = jax.ShapeDtypeStruct((rows, cols), F32)
    outs = pl.pallas_call(body, name=name, grid=(rows // tb,), in_specs=[spec] * 4, out_specs=[spec] * 3,
                          out_shape=[sh] * 3, compiler_params=_params(1))(w2, g2, m2, v2)
    return [o.reshape(shape) for o in outs]


def _to_rows(bg, col0):
    t_ = bg.shape[0]
    a = bg[:, col0:col0 + 2 * DN_HEADS].reshape(t_ // CHUNK, CHUNK, 2, DN_HEADS)
    return jnp.transpose(a, (2, 0, 3, 1))


def _from_rows(db, dg):
    nch = db.shape[1]
    back = lambda a: jnp.transpose(a, (1, 3, 0, 2)).reshape(nch * CHUNK, 2 * DN_HEADS)
    return jnp.pad(jnp.concatenate([back(db), back(dg)], axis=1), ((0, 0), (0, 128 - 4 * DN_HEADS)))


def _layer_fwd(x, xm, w, rest, cos, sin):
    t_ = x.shape[0]
    tb, nb = min(256, t_), min(1024, t_)
    pm = _mm(xm, w["in_main"], name="mm_in", tn=1536)
    pbg = _mm(xm, w["in_bg"], name="mm_in_bg")
    w = {**w, **rest(pbg)}
    qkv = _prep_fwd(pm, w["conv"])
    bg, = _stage_fwd(_bg_f, "bg_fwd", [(pbg, 128, 0, False)], [w["arow"], w["dtrow"]], [(128, F32)], nb)
    brows, grows = _to_rows(bg, 0), _to_rows(bg, 2 * DN_HEADS)
    ares, tinv = _dn_a_fwd(qkv, grows, brows)
    o_f, o_b, st_f, st_b = _dn_b_fwd(ares, grows)
    odn, = _stage_fwd(_gnorm_f, "gnorm_fwd", [(o_f, 128, 0, True), (o_b, 128, 0, True), (pm, 128, O_Z // 128, True)],
                      [w["gnw"]], [(128, _MXU)], nb, ncol=DN_HEADS)
    osw = _attn_fwd(pm, cos, sin, w["sinks"])
    ya = _mm(odn, w["a"], name="mm_a")
    yb = _mm(osw, w["b"], name="mm_b")
    gates = [(pm, 512, O_GA // 512, True), (pm, 512, O_GB // 512, True)]
    merged, = _stage_fwd(_merge_f, "merge_fwd", [(ya, 512, 0, True), (yb, 512, 0, True)] + gates, [], [(512, _MXU)],
                         tb, ncol=2)
    mix = _mm(merged, w["o"], name="mm_o")
    x1, x1m = _stage_fwd(_ln_f2, "ln1_fwd", [(x, D, 0, False), (mix, D, 0, False)], [w["ln1g"], w["ln1b"]],
                         [(D, F32), (D, _MXU)], tb)
    gu = _mm(x1m, w["gu"], name="mm_gu", tn=FFN // 2)
    hid, = _stage_fwd(_swiglu_f, "swiglu_fwd", [(gu, FFN // 2, 0, True), (gu, FFN // 2, 2, True)], [],
                      [(FFN // 2, _MXU)], tb, ncol=2)
    ffn = _mm(hid, w["d"], name="mm_d", tk=FFN // 2)
    x2, x2m = _stage_fwd(_ln_f2, "ln2_fwd", [(x1, D, 0, False), (ffn, D, 0, False)], [w["ln2g"], w["ln2b"]],
                         [(D, F32), (D, _MXU)], tb)
    res = dict(w=w, x=x, xm=xm, pm=pm, pbg=pbg, qkv=qkv, grows=grows, brows=brows, ares=ares, tinv=tinv, o_f=o_f, o_b=o_b, st_f=st_f,
               st_b=st_b, odn=odn, osw=osw, ya=ya, yb=yb, merged=merged, mix=mix, x1=x1, x1m=x1m, gu=gu, hid=hid,
               ffn=ffn)
    return x2, x2m, res


def _layer_bwd(dx2, r, w, cos, sin):
    t_ = dx2.shape[0]
    tb, sb, nb = min(256, t_), min(128, t_), min(1024, t_)
    pm = r["pm"]
    g = {}
    dx1a, dffn, g["ln2g"], g["ln2b"] = _stage_bwd(
        _ln_f, "ln2_bwd", [(r["x1"], D, 0, False), (r["ffn"], D, 0, False)], [w["ln2g"], w["ln2b"]], [dx2], tb,
        dtypes=[F32, _MXU])
    dhid = _mm(dffn, w["d"], tb=True, name="mm_d_dx", tn=FFN // 2)
    g["d"] = _mm(r["hid"], dffn, ta=True, name="mm_d_dw", tm=FFN // 2)
    dgu, = _stage_bwd(_swiglu_f, "swiglu_bwd", [(r["gu"], FFN, 0, False), (r["gu"], FFN, 1, False)], [], [dhid], sb,
                      cat=[[0, 1]], dtypes=[_MXU])
    dx1 = _mm(dgu, w["gu"], tb=True, add=dx1a, name="mm_gu_dx", tk=FFN // 2)
    g["gu"] = _mm(r["x1m"], dgu, ta=True, name="mm_gu_dw", tn=FFN // 2)
    dxa, dmix, g["ln1g"], g["ln1b"] = _stage_bwd(
        _ln_f, "ln1_bwd", [(r["x"], D, 0, False), (r["mix"], D, 0, False)], [w["ln1g"], w["ln1b"]], [dx1], tb,
        dtypes=[F32, _MXU])
    dmerged = _mm(dmix, w["o"], tb=True, name="mm_o_dx")
    g["o"] = _mm(r["merged"], dmix, ta=True, name="mm_o_dw")
    dya, dyb, dpm = _stage_bwd(
        _merge_f, "merge_bwd", [(r["ya"], D, 0, False), (r["yb"], D, 0, False), (pm, D, O_GA // D, False),
                                (pm, D, O_GB // D, False)], [], [dmerged], tb, cat=[[0], [1], [2, 3]],
        dtypes=[_MXU, _MXU, _MXU], dest={2: (None, N_MAIN, O_GA // (2 * D))})
    dodn = _mm(dya, w["a"], tb=True, name="mm_a_dx")
    g["a"] = _mm(r["odn"], dya, ta=True, name="mm_a_dw")
    dosw = _mm(dyb, w["b"], tb=True, name="mm_b_dx")
    g["b"] = _mm(r["osw"], dyb, ta=True, name="mm_b_dw")
    ab = _attn_bwd(pm, cos, sin, w["sinks"], dosw, dpm)
    dpm, g["sinks"] = ab[0], ab[7]
    dpm = _band_sum(ab[1:4], ab[4:7], dpm)
    dof, _, dpm, g["gnw"] = _stage_bwd(
        _gnorm_f, "gnorm_bwd", [(r["o_f"], 128, 0, True), (r["o_b"], 128, 0, True), (pm, 128, O_Z // 128, True)],
        [w["gnw"]], [dodn], nb, ncol=DN_HEADS, dtypes=[F32, F32, _MXU], dest={2: (dpm, N_MAIN, O_Z // 128)})
    dares, dg_b = _dn_b_bwd(r["ares"], r["grows"], r["st_f"], r["st_b"], dof)
    dqkv, dgrows, dbrows = _dn_a_bwd(r["qkv"], r["grows"], r["brows"], r["tinv"], dares, dg_b)
    dbg = _from_rows(dbrows, dgrows)
    dpbg, g["arow"], g["dtrow"] = _stage_bwd(_bg_f, "bg_bwd", [(r["pbg"], 128, 0, False)], [w["arow"], w["dtrow"]],
                                             [dbg], nb)
    dpm, g["conv"] = _prep_bwd(pm, w["conv"], dqkv, dpm)
    dx = _mm(dpm, w["in_main"], tb=True, add=dxa, name="mm_in_dx", tk=1920)
    dx = _mm(dpbg, w["in_bg"], tb=True, add=dx, name="mm_in_bg_dx")
    g["in_main"] = _mm(r["xm"], dpm, ta=True, name="mm_in_dw", tn=1536)
    g["in_bg"] = _mm(r["xm"], dpbg, ta=True, name="mm_in_bg_dw")
    return dx, g


def _place():
    return lax.axis_index("x"), lax.axis_index("y"), lax.axis_index("c")


def _other_chips(x, y):
    return [(1 - x, y), (x, 1 - y), (1 - x, 1 - y)]


HBM = pl.BlockSpec(memory_space=pltpu.HBM)
SEM = pl.BlockSpec(memory_space=pltpu.SEMAPHORE)
DATAFLOW = pltpu.SideEffectType.DATAFLOW_SIDE_EFFECTING


def _hbm(a):
    return pltpu.HBM(a.shape, a.dtype)


def _gather_start(locs, kinds, name):
    n = len(locs)
    locs = list(locs)
    lands = [lax.empty((N_CHIPS,) + a.shape if kind == "row" else (a.shape[0], N_CHIPS * a.shape[1]), a.dtype)
             for a, kind in zip(locs, kinds)]

    def body(*refs):
        loc_refs, land_refs = refs[:n], refs[n:2 * n]
        send_sems, recv_sems, token = refs[2 * n:3 * n], refs[3 * n:4 * n], refs[-1]
        x, y, c = _place()
        myq = 2 * x + y
        for t in range(n):
            width = locs[t].shape[1]
            mine = (land_refs[t].at[myq] if kinds[t] == "row"
                    else land_refs[t].at[:, pl.ds(pl.multiple_of(myq * width, 128), width)])
            for dev in [(cx, cy, c) for cx, cy in _other_chips(x, y)] + [(x, y, 1 - c)]:
                pltpu.make_async_remote_copy(src_ref=loc_refs[t], dst_ref=mine, send_sem=send_sems[t],
                                             recv_sem=recv_sems[t], device_id=dev, device_id_type=MESH).start()
        token[...] = jnp.zeros_like(token)

    res = pl.pallas_call(
        body, name=name,
        out_shape=[pltpu.SemaphoreType.DMA(())] * (2 * n) + [_hbm(a) for a in locs + lands]
        + [jax.ShapeDtypeStruct((8, 128), F32)],
        in_specs=[HBM] * (2 * n), out_specs=[SEM] * (2 * n) + [HBM] * (2 * n) + [pl.BlockSpec(memory_space=pltpu.VMEM)],
        input_output_aliases={t: 2 * n + t for t in range(2 * n)},
        compiler_params=pltpu.CompilerParams(has_side_effects=DATAFLOW),
    )(*[pltpu.with_memory_space_constraint(a, pltpu.HBM) for a in locs + lands])
    return (res[:n], res[n:2 * n], res[2 * n:3 * n], res[3 * n:4 * n]), res[-1]


def _split_wait(handle, after, name):
    send_sems, recv_sems, srcs, lands = handle
    n = len(srcs)

    def body(*refs):
        land_refs, ssems, rsems = refs[n:2 * n], refs[2 * n:3 * n], refs[3 * n:4 * n]
        x, y, c = _place()
        for t in range(n):
            done = pltpu.make_async_remote_copy(
                src_ref=land_refs[t], dst_ref=land_refs[t], send_sem=ssems[t], recv_sem=rsems[t],
                device_id=(x, y, c), device_id_type=MESH)
            done.wait_send()
            done.wait_recv()

    res = pl.pallas_call(
        body, name=name, out_shape=[_hbm(a) for a in list(srcs) + list(lands)],
        in_specs=[HBM] * (2 * n) + [SEM] * (2 * n) + [ANY], out_specs=[HBM] * (2 * n),
        input_output_aliases={t: t for t in range(2 * n)},
        compiler_params=pltpu.CompilerParams(has_side_effects=DATAFLOW),
    )(*srcs, *lands, *send_sems, *recv_sems, after)
    return res[:n], res[n:]


RS_CHUNKS = 2


def _piece(ref, kind, q, hf, pr, pc):
    if kind == "row":
        return ref.at[pl.ds((2 * q + hf) * pr, pr), :]
    return ref.at[pl.ds(hf * pr, pr), pl.ds(pl.multiple_of(q * pc, 128), pc)]


def _rs_to_sibling(ts, meta):
    n = len(ts)

    def body(*refs):
        t_refs, r1_refs, send_sems, recv_sems = refs[:n], refs[n:2 * n], refs[2 * n], refs[2 * n + 1]
        x, y, c = _place()
        sib = (x, y, 1 - c)
        for t, (kind, pr, pc) in enumerate(meta):
            for q in range(N_CHIPS):
                pltpu.make_async_remote_copy(
                    src_ref=_piece(t_refs[t], kind, q, 1 - c, pr, pc), dst_ref=r1_refs[t].at[q],
                    send_sem=send_sems.at[t], recv_sem=recv_sems.at[t], device_id=sib, device_id_type=MESH).start()
        for t in range(n):
            allq = pltpu.make_async_remote_copy(
                src_ref=r1_refs[t], dst_ref=r1_refs[t], send_sem=send_sems.at[t], recv_sem=recv_sems.at[t],
                device_id=sib, device_id_type=MESH)
            allq.wait_send()
            allq.wait_recv()

    return pl.pallas_call(
        body, name="rs_to_sibling", in_specs=[ANY] * n, out_specs=[ANY] * n,
        out_shape=[jax.ShapeDtypeStruct((N_CHIPS, pr, pc), F32) for (_, pr, pc) in meta],
        scratch_shapes=[pltpu.SemaphoreType.DMA((n,)), pltpu.SemaphoreType.DMA((n,))],
    )(*ts)


def _rs_add_sibling(ts, r1s, meta, c):
    n = len(ts)
    in_specs, out_specs, out_shape = [], [], []
    for kind, pr, pc in meta:
        rs = pr // RS_CHUNKS
        if kind == "row":
            in_specs.append(pl.BlockSpec((rs, pc), lambda q, r, c_ref: ((2 * q + c_ref[0]) * RS_CHUNKS + r, 0)))
        else:
            in_specs.append(pl.BlockSpec((rs, pc), lambda q, r, c_ref: (c_ref[0] * RS_CHUNKS + r, q)))
    for kind, pr, pc in meta:
        sp = pl.BlockSpec((None, pr // RS_CHUNKS, pc), lambda q, r, c_ref: (q, r, 0))
        in_specs.append(sp)
        out_specs.append(sp)
        out_shape.append(jax.ShapeDtypeStruct((N_CHIPS, pr, pc), BF16))

    def body(c_ref, *refs):
        for t in range(n):
            refs[2 * n + t][...] = (refs[t][...] + refs[n + t][...]).astype(BF16)

    return pl.pallas_call(
        body, name="rs_add_sibling", out_shape=out_shape,
        grid_spec=pltpu.PrefetchScalarGridSpec(num_scalar_prefetch=1, grid=(N_CHIPS, RS_CHUNKS), in_specs=in_specs,
                                               out_specs=out_specs),
        compiler_params=_params(2))(c.reshape(1).astype(jnp.int32), *ts, *r1s)


def _rs_chips_start(ps, meta, name):
    n = len(ps)
    ps = list(ps)
    lands = [lax.empty((N_CHIPS - 1, pr, pc), p.dtype) for p, (_, pr, pc) in zip(ps, meta)]

    def body(*refs):
        p_refs, land_refs = refs[:n], refs[n:2 * n]
        send_sems, recv_sems, token = refs[2 * n:3 * n], refs[3 * n:4 * n], refs[-1]
        x, y, c = _place()
        for t in range(n):
            for j, (cx, cy) in enumerate(_other_chips(x, y)):
                pltpu.make_async_remote_copy(
                    src_ref=p_refs[t].at[2 * cx + cy], dst_ref=land_refs[t].at[j], send_sem=send_sems[t],
                    recv_sem=recv_sems[t], device_id=(cx, cy, c), device_id_type=MESH).start()
        token[...] = jnp.zeros_like(token)

    res = pl.pallas_call(
        body, name=name,
        out_shape=[pltpu.SemaphoreType.DMA(())] * (2 * n) + [_hbm(a) for a in ps + lands]
        + [jax.ShapeDtypeStruct((8, 128), F32)],
        in_specs=[HBM] * (2 * n), out_specs=[SEM] * (2 * n) + [HBM] * (2 * n) + [pl.BlockSpec(memory_space=pltpu.VMEM)],
        input_output_aliases={t: 2 * n + t for t in range(2 * n)},
        compiler_params=pltpu.CompilerParams(has_side_effects=DATAFLOW),
    )(*[pltpu.with_memory_space_constraint(a, pltpu.HBM) for a in ps + lands])
    return (res[:n], res[n:2 * n], res[2 * n:3 * n], res[3 * n:4 * n]), res[-1]


def _rs_add_chips(ps, r2s, meta, myq, c):
    n = len(ps)
    in_specs, out_specs, out_shape = [], [], []
    for _, pr, pc in meta:
        in_specs.append(pl.BlockSpec((None, pr // RS_CHUNKS, pc), lambda r, q_ref, c_ref: (q_ref[0], r, 0)))
    for _, pr, pc in meta:
        in_specs.append(pl.BlockSpec((N_CHIPS - 1, pr // RS_CHUNKS, pc), lambda r, q_ref, c_ref: (0, r, 0)))
        out_specs.append(pl.BlockSpec((None, pr // RS_CHUNKS, pc), lambda r, q_ref, c_ref: (c_ref[0], r, 0)))
        out_shape.append(jax.ShapeDtypeStruct((2, pr, pc), F32))

    def body(q_ref, c_ref, *refs):
        for t in range(n):
            r2 = refs[n + t]
            own = refs[t][...].astype(F32)
            refs[2 * n + t][...] = ((own + r2[0].astype(F32)) + r2[1].astype(F32)) + r2[2].astype(F32)

    return pl.pallas_call(
        body, name="rs_add_chips", out_shape=out_shape,
        grid_spec=pltpu.PrefetchScalarGridSpec(num_scalar_prefetch=2, grid=(RS_CHUNKS,), in_specs=in_specs,
                                               out_specs=out_specs),
        compiler_params=_params(1))(myq.reshape(1).astype(jnp.int32), c.reshape(1).astype(jnp.int32), *ps, *r2s)


def _rs_share_halves(gs):
    n = len(gs)

    def body(*refs):
        g_refs, send_sems, recv_sems = refs[n:2 * n], refs[2 * n], refs[2 * n + 1]
        x, y, c = _place()
        sib = (x, y, 1 - c)
        for t in range(n):
            pltpu.make_async_remote_copy(
                src_ref=g_refs[t].at[c], dst_ref=g_refs[t].at[c], send_sem=send_sems.at[t], recv_sem=recv_sems.at[t],
                device_id=sib, device_id_type=MESH).start()
        for t in range(n):
            cp = pltpu.make_async_remote_copy(
                src_ref=g_refs[t].at[c], dst_ref=g_refs[t].at[1 - c], send_sem=send_sems.at[t],
                recv_sem=recv_sems.at[t], device_id=sib, device_id_type=MESH)
            cp.wait_send()
            cp.wait_recv()

    return pl.pallas_call(
        body, name="rs_share_halves", in_specs=[ANY] * n, out_specs=[ANY] * n,
        out_shape=[jax.ShapeDtypeStruct(g.shape, g.dtype) for g in gs], input_output_aliases={t: t for t in range(n)},
        scratch_shapes=[pltpu.SemaphoreType.DMA((n,)), pltpu.SemaphoreType.DMA((n,))],
    )(*gs)


def _rs_begin(ts, meta, c, name):
    r1s = _rs_to_sibling(ts, meta)
    ps = _rs_add_sibling(ts, r1s, meta, c)
    return _rs_chips_start(ps, meta, name + "_start")


def _rs_end(handle, after, meta, c, myq, name):
    ps, r2s = _split_wait(handle, after, name + "_wait")
    return _rs_share_halves(_rs_add_chips(ps, r2s, meta, myq, c))


def _allreduce_small(buf):
    rows = buf.shape[0]
    ndev = 8

    def body(b_ref, o_ref, slots, send_sems, recv_sems):
        x, y, c = _place()
        me = 4 * x + 2 * y + c
        slots[me] = b_ref[...]
        for k in range(1, ndev):
            kx, ky, kc = (k >> 2) & 1, (k >> 1) & 1, k & 1
            peer = (x ^ kx, y ^ ky, c ^ kc)
            pltpu.make_async_remote_copy(
                src_ref=b_ref, dst_ref=slots.at[me], send_sem=send_sems.at[k - 1], recv_sem=recv_sems.at[k - 1],
                device_id=peer, device_id_type=MESH).start()
        for k in range(1, ndev):
            kx, ky, kc = (k >> 2) & 1, (k >> 1) & 1, k & 1
            cp = pltpu.make_async_remote_copy(
                src_ref=b_ref, dst_ref=slots.at[me ^ k], send_sem=send_sems.at[k - 1], recv_sem=recv_sems.at[k - 1],
                device_id=(x ^ kx, y ^ ky, c ^ kc), device_id_type=MESH)
            cp.wait_send()
            cp.wait_recv()
        acc = slots[0]
        for s in range(1, ndev):
            acc = acc + slots[s]
        o_ref[...] = acc

    vm = pl.BlockSpec(memory_space=pltpu.VMEM)
    return pl.pallas_call(
        body, name="allreduce_small", in_specs=[vm], out_specs=vm, out_shape=jax.ShapeDtypeStruct((rows, 128), F32),
        scratch_shapes=[pltpu.VMEM((ndev, rows, 128), F32), pltpu.SemaphoreType.DMA((ndev - 1,)),
                        pltpu.SemaphoreType.DMA((ndev - 1,))],
        compiler_params=pltpu.CompilerParams(vmem_limit_bytes=VMEM_LIMIT))(buf)


RS_META = [("col", D // 2, IN_PAD), ("row", D // 8, D), ("row", D // 8, D), ("row", D // 8, D),
           ("col", D // 2, 2 * FFN // N_CHIPS), ("row", FFN // 8, D)]
SMALL_ROWS = 156


def _rope_tables(t_):
    half = SW_DIM // 2
    inv_freq = ROPE_THETA ** (-jnp.arange(half, dtype=F32) / half)
    ang = jnp.arange(t_, dtype=F32)[:, None] * inv_freq[None, :]
    reps = 128 // half
    return jnp.concatenate([jnp.cos(ang)] * reps, axis=1), jnp.concatenate([jnp.sin(ang)] * reps, axis=1)


def _orig_cols(padded, a, b):
    out = []
    for q in range(N_CHIPS):
        lo, hi = max(a, q * IN_SHARD), min(b, (q + 1) * IN_SHARD)
        if lo < hi:
            out.append(padded[:, q * IN_PAD + lo - q * IN_SHARD:q * IN_PAD + hi - q * IN_SHARD])
    return out


_ORIG_SEGMENTS = [(0, R_BG, "main", 0), (R_BG, R_SW, "bg", 0), (R_SW, R_GATES, "main", O_QS), (R_GATES, IN_COLS, "main", O_GA)]


def _to_padded_shards(main, bg):
    zeros = jnp.zeros((main.shape[0], IN_PAD - IN_SHARD), main.dtype)
    parts = []
    for q in range(N_CHIPS):
        for a, b, src, s0 in _ORIG_SEGMENTS:
            lo, hi = max(a, q * IN_SHARD), min(b, (q + 1) * IN_SHARD)
            if lo < hi:
                parts.append((main if src == "main" else bg)[:, s0 + lo - a:s0 + hi - a])
        parts.append(zeros)
    return jnp.concatenate(parts, axis=1)


def _lane_row(v16):
    return jnp.pad(v16.reshape(1, 2 * DN_HEADS), ((0, 0), (2 * DN_HEADS, 128 - 4 * DN_HEADS)))


def _pack_small(g):
    pad16 = jnp.pad(g["sinks"], ((0, 0), (0, 128 - SW_HEADS)))
    return jnp.concatenate([g["conv"].reshape(-1, 128), g["ln1g"].reshape(-1, 128), g["ln1b"].reshape(-1, 128),
                            g["ln2g"].reshape(-1, 128), g["ln2b"].reshape(-1, 128), g["gnw"], g["arow"], g["dtrow"],
                            pad16], axis=0)


def _unpack_small(buf):
    nconv = DN_CONV * 3 * D // 128
    o = nconv
    out = dict(conv=buf[:o].reshape(DN_CONV, 3 * D))
    for name in ("ln1g", "ln1b", "ln2g", "ln2b"):
        out[name] = buf[o:o + 8].reshape(D)
        o += 8
    out["gnw"] = buf[o]
    out["a_log"] = buf[o + 1, 2 * DN_HEADS:4 * DN_HEADS].reshape(2, DN_HEADS)
    out["dt_bias"] = buf[o + 2, 2 * DN_HEADS:4 * DN_HEADS].reshape(2, DN_HEADS)
    out["sinks"] = buf[o + 3, :SW_HEADS]
    return out


def kernel(x, w_in, conv_w, a_log, dt_bias, dn_norm_w, sinks, w_branch_a, w_branch_b, w_out, ln1_g, ln1_b, w_gate_up, w_down, ln2_g, ln2_b, loss_target, m_w_in, m_conv_w, m_a_log, m_dt_bias, m_dn_norm_w, m_sinks, m_w_branch_a, m_w_branch_b, m_w_out, m_ln1_g, m_ln1_b, m_w_gate_up, m_w_down, m_ln2_g, m_ln2_b, v_w_in, v_conv_w, v_a_log, v_dt_bias, v_dn_norm_w, v_sinks, v_w_branch_a, v_w_branch_b, v_w_out, v_ln1_g, v_ln1_b, v_w_gate_up, v_w_down, v_ln2_g, v_ln2_b):
    xi, yi, ci = _place()
    myq = 2 * xi + yi
    t_ = x.shape[1]
    cos, sin = _rope_tables(t_)

    loc_in = jnp.pad(w_in.astype(BF16), ((0, 0), (0, 0), (0, IN_PAD - IN_SHARD)))
    locs = [loc_in, w_branch_a.astype(BF16), w_branch_b.astype(BF16), w_out.astype(BF16), w_gate_up.astype(BF16),
            w_down.astype(BF16), conv_w]
    kinds = ["col", "row", "row", "row", "col", "row", "col"]
    gathers = []
    for l in range(DEPTH):
        srcs = [a[l] for a in locs]
        if gathers:
            srcs[-1] = srcs[-1] + gathers[-1][1][0, 0]
        gathers.append(_gather_start(srcs, kinds, "gather_%d_start" % l))

    def in_weights(l, after):
        _, (full_in,) = _split_wait(tuple(part[:1] for part in gathers[l][0]), after, "gather_%d_wait_in" % l)
        cols = lambda a, b: _orig_cols(full_in, a, b)
        return dict(
            in_main=jnp.concatenate(cols(0, R_BG) + cols(R_GATES, IN_COLS) + cols(R_SW, R_GATES), axis=1),
            in_bg=jnp.pad(jnp.concatenate(cols(R_BG, R_SW), axis=1), ((0, 0), (0, 128 - 4 * DN_HEADS))))

    def rest_weights(l, after):
        _, (full_a, full_b, full_o, full_gu, full_d, full_conv) = _split_wait(
            tuple(part[1:] for part in gathers[l][0]), after, "gather_%d_wait_rest" % l)
        return dict(
            conv=full_conv, arow=_lane_row(a_log[l]), dtrow=_lane_row(dt_bias[l]), gnw=dn_norm_w[l][None],
            sinks=sinks[l][None], a=full_a.reshape(D, D), b=full_b.reshape(D, D), o=full_o.reshape(D, D),
            ln1g=ln1_g[l][None], ln1b=ln1_b[l][None], gu=full_gu, d=full_d.reshape(FFN, D),
            ln2g=ln2_g[l][None], ln2b=ln2_b[l][None])

    h = x[0]
    hm = h.astype(_MXU)
    residuals = []
    for l in range(DEPTH):
        win = in_weights(l, gathers[-1][1] if l == 0 else h)
        h, hm, res = _layer_fwd(h, hm, win, functools.partial(rest_weights, l), cos, sin)
        residuals.append(res)
    dh, sq = _loss_head(h, loss_target[0])
    loss = lax.psum((0.5 / D) * jnp.sum(sq), ("x", "y", "c"))

    big = [None] * DEPTH
    small = [None] * DEPTH
    pending = None
    for l in reversed(range(DEPTH)):
        dh, g = _layer_bwd(dh, residuals[l], residuals[l]["w"], cos, sin)
        if pending is not None:
            big[pending[0]] = _rs_end(pending[1], dh, RS_META, ci, myq, "rs_chips_%d" % pending[0])
        g_in = _to_padded_shards(g["in_main"], g["in_bg"])
        handle, token = _rs_begin([g_in, g["a"], g["b"], g["o"], g["gu"], g["d"]], RS_META, ci, "rs_chips_%d" % l)
        pending = (l, handle)
        dh = dh + token[0, 0]
        small[l] = _pack_small(g)
    tot = _allreduce_small(jnp.concatenate(small, axis=0))
    big[pending[0]] = _rs_end(pending[1], tot, RS_META, ci, myq, "rs_chips_%d" % pending[0])
    sm = [_unpack_small(tot[l * SMALL_ROWS:(l + 1) * SMALL_ROWS]) for l in range(DEPTH)]
    stack = lambda name: jnp.stack([s[name] for s in sm], axis=0)

    grads = dict(
        w_in=jnp.stack([big[l][0].reshape(D, IN_PAD)[:, :IN_SHARD] for l in range(DEPTH)]),
        conv_w=lax.dynamic_slice_in_dim(stack("conv"), myq * (3 * D // N_CHIPS), 3 * D // N_CHIPS, axis=2),
        a_log=stack("a_log"), dt_bias=stack("dt_bias"), dn_norm_w=stack("gnw"), sinks=stack("sinks"),
        w_branch_a=jnp.stack([big[l][1].reshape(D // N_CHIPS, D) for l in range(DEPTH)]),
        w_branch_b=jnp.stack([big[l][2].reshape(D // N_CHIPS, D) for l in range(DEPTH)]),
        w_out=jnp.stack([big[l][3].reshape(D // N_CHIPS, D) for l in range(DEPTH)]),
        ln1_g=stack("ln1g"), ln1_b=stack("ln1b"),
        w_gate_up=jnp.stack([big[l][4].reshape(D, 2 * FFN // N_CHIPS) for l in range(DEPTH)]),
        w_down=jnp.stack([big[l][5].reshape(FFN // N_CHIPS, D) for l in range(DEPTH)]),
        ln2_g=stack("ln2g"), ln2_b=stack("ln2b"))
    weights = dict(w_in=w_in, conv_w=conv_w, a_log=a_log, dt_bias=dt_bias, dn_norm_w=dn_norm_w, sinks=sinks,
                   w_branch_a=w_branch_a, w_branch_b=w_branch_b, w_out=w_out, ln1_g=ln1_g, ln1_b=ln1_b,
                   w_gate_up=w_gate_up, w_down=w_down, ln2_g=ln2_g, ln2_b=ln2_b)
    ms = dict(w_in=m_w_in, conv_w=m_conv_w, a_log=m_a_log, dt_bias=m_dt_bias, dn_norm_w=m_dn_norm_w, sinks=m_sinks,
              w_branch_a=m_w_branch_a, w_branch_b=m_w_branch_b, w_out=m_w_out, ln1_g=m_ln1_g, ln1_b=m_ln1_b,
              w_gate_up=m_w_gate_up, w_down=m_w_down, ln2_g=m_ln2_g, ln2_b=m_ln2_b)
    vs = dict(w_in=v_w_in, conv_w=v_conv_w, a_log=v_a_log, dt_bias=v_dt_bias, dn_norm_w=v_dn_norm_w, sinks=v_sinks,
              w_branch_a=v_w_branch_a, w_branch_b=v_w_branch_b, w_out=v_w_out, ln1_g=v_ln1_g, ln1_b=v_ln1_b,
              w_gate_up=v_w_gate_up, w_down=v_w_down, ln2_g=v_ln2_g, ln2_b=v_ln2_b)
    names = list(weights)
    upd = {n: _adamw("adamw_" + n, weights[n], grads[n], ms[n], vs[n]) for n in names}
    return (loss, dh[None], *[grads[n] for n in names], *[upd[n][0] for n in names], *[upd[n][1] for n in names],
            *[upd[n][2] for n in names])
```

```python
import functools

import jax
import jax.numpy as jnp
from jax import lax
from jax.experimental import pallas as pl
from jax.experimental.pallas import tpu as pltpu

F32 = jnp.float32
BF16 = jnp.bfloat16
_MXU = BF16

D = 1024
DEPTH = 4
DN_HEADS = 8
DN_DIM = 128
DN_CONV = 5
CHUNK = 64
SW_HEADS = 16
SW_KV = 4
SW_DIM = 64
SW_GRP = SW_HEADS // SW_KV
SW_BLOCK = 128
ROPE_THETA = 10000.0
FFN = 2816
ALPHA = (2.0 * DEPTH) ** 0.25
LN_EPS = 1e-5
RMS_EPS = 1e-6
IN_COLS = 7712
O_Z, O_GA, O_GB, O_QS, O_KS, O_VS, N_MAIN = 3072, 4096, 5120, 6144, 7168, 7424, 7680
R_BG, R_SW, R_GATES = 4096, 4128, 5664
N_CHIPS = 4
IN_SHARD = IN_COLS // N_CHIPS
IN_PAD = 2048
ADAM_LR, ADAM_B1, ADAM_B2, ADAM_EPS, ADAM_WD, ADAM_STEP = 0.001, 0.9, 0.999, 1e-08, 0.01, 10
VMEM_LIMIT = 52 * 1024 * 1024
MESH = pl.DeviceIdType.MESH
ANY = pl.BlockSpec(memory_space=pl.ANY)


def _params(n_grid, **kw):
    return pltpu.CompilerParams(dimension_semantics=("arbitrary",) * n_grid, vmem_limit_bytes=VMEM_LIMIT, **kw)


def _full(a):
    nd = a.ndim
    return pl.BlockSpec(a.shape, lambda *_, nd=nd: (0,) * nd)


def _raw_dot(a, b, ca, cb):
    return lax.dot_general(a.astype(_MXU), b.astype(_MXU), (((ca,), (cb,)), ((), ())), preferred_element_type=F32)


@jax.custom_vjp
def _nn(a, b):
    return _raw_dot(a, b, 1, 0)


@jax.custom_vjp
def _nt(a, b):
    return _raw_dot(a, b, 1, 1)


@jax.custom_vjp
def _tn(a, b):
    if a.shape[1] > b.shape[1]:
        return _raw_dot(b, a, 0, 0).T
    return _raw_dot(a, b, 0, 0)


_nn.defvjp(lambda a, b: (_nn(a, b), (a, b)), lambda r, g: (_nt(g, r[1]), _tn(r[0], g)))
_nt.defvjp(lambda a, b: (_nt(a, b), (a, b)), lambda r, g: (_nn(g, r[1]), _tn(g, r[0])))
_tn.defvjp(lambda a, b: (_tn(a, b), (a, b)), lambda r, g: (_nt(r[1], g), _nn(r[0], g)))


def _hdot(a, b, ca=1, cb=0):
    ah, bh = a.astype(BF16), b.astype(BF16)
    al, bl = (a - ah.astype(F32)).astype(BF16), (b - bh.astype(F32)).astype(BF16)
    dot = lambda u, v: lax.dot_general(u, v, (((ca,), (cb,)), ((), ())), preferred_element_type=F32)
    return dot(ah, bh) + (dot(ah, bl) + dot(al, bh))


def _inv_impl(mats):
    n = mats[0].shape[0]
    eye = (lax.broadcasted_iota(jnp.int32, (n, n), 0) == lax.broadcasted_iota(jnp.int32, (n, n), 1)).astype(F32)
    ps = [-a for a in mats]
    ts = [eye + p for p in ps]
    for _ in range(max(1, (n - 1).bit_length()) - 1):
        ps = [_hdot(p, p) for p in ps]
        ts = [t + _hdot(t, p) for t, p in zip(ts, ps)]
    return tuple(ts)


@jax.custom_vjp
def _inv(mats):
    return _inv_impl(mats)


def _inv_fwd(mats):
    ts = _inv_impl(mats)
    return ts, ts


def _inv_bwd(ts, gs):
    xs = [_hdot(t, g, 0, 0) for t, g in zip(ts, gs)]
    return (tuple(-_hdot(x, t, 1, 1) for x, t in zip(xs, ts)),)


_inv.defvjp(_inv_fwd, _inv_bwd)


@jax.custom_vjp
def _inv_saved(mats, saved):
    return saved


_inv_saved.defvjp(lambda mats, saved: (saved, saved),
                  lambda ts, gs: (_inv_bwd(ts, gs)[0], tuple(jnp.zeros_like(t) for t in ts)))


def _tile(n, cap):
    if n <= cap:
        return n
    best = [t for t in range(128, cap + 1, 128) if n % t == 0]
    assert best, (n, cap)
    return best[-1]


def _mm(a, b, *, name, ta=False, tb=False, add=None, tm=1024, tn=1024, tk=1024):
    if ta:
        k_, m_ = a.shape
    else:
        m_, k_ = a.shape
    n_ = b.shape[0] if tb else b.shape[1]
    tm, tn, tk = _tile(m_, tm), _tile(n_, tn), _tile(k_, tk)
    nk = k_ // tk
    has_add = add is not None

    def body(*refs):
        a_ref, b_ref = refs[:2]
        add_ref = refs[2] if has_add else None
        o_ref = refs[3] if has_add else refs[2]
        part = _raw_dot(a_ref[...], b_ref[...], 0 if ta else 1, 1 if tb else 0)
        if nk == 1:
            o_ref[...] = part + add_ref[...] if has_add else part
            return
        acc = refs[-1]
        k = pl.program_id(2)

        @pl.when(k == 0)
        def _():
            acc[...] = part

        @pl.when(jnp.logical_and(k > 0, k < nk - 1))
        def _():
            acc[...] += part

        @pl.when(k == nk - 1)
        def _():
            o_ref[...] = acc[...] + part + add_ref[...] if has_add else acc[...] + part

    a_spec = pl.BlockSpec((tk, tm), lambda i, j, k: (k, i)) if ta else pl.BlockSpec((tm, tk), lambda i, j, k: (i, k))
    b_spec = pl.BlockSpec((tn, tk), lambda i, j, k: (j, k)) if tb else pl.BlockSpec((tk, tn), lambda i, j, k: (k, j))
    o_spec = pl.BlockSpec((tm, tn), lambda i, j, k: (i, j))
    in_specs = [a_spec, b_spec] + ([o_spec] if has_add else [])
    args = (a, b) + ((add,) if has_add else ())
    return pl.pallas_call(
        body, name=name, grid=(m_ // tm, n_ // tn, nk), in_specs=in_specs, out_specs=o_spec,
        out_shape=jax.ShapeDtypeStruct((m_, n_), F32),
        scratch_shapes=[pltpu.VMEM((tm, tn), F32)] if nk > 1 else [],
        compiler_params=_params(3))(*args)


def _mm_fused(a, b, post, outs, *, name, rows=(), params=(), n_sums=0, tb=False, tm=1024, tn=1024, tk=1024):
    m_, k_ = a.shape
    n_ = b.shape[0] if tb else b.shape[1]
    tm, tn, tk = _tile(m_, tm), _tile(n_, tn), _tile(k_, tk)
    nk = k_ // tk
    nr, npar, no = len(rows), len(params), len(outs)
    aliased = [o[4] for o in outs if o[4] is not None]
    nin = 2 + nr + npar

    def body(*refs):
        a_ref, b_ref = refs[:2]
        row_refs, par_refs = refs[2:2 + nr], refs[2 + nr:nin]
        out_refs = refs[nin + len(aliased):nin + len(aliased) + no + n_sums]
        part = _raw_dot(a_ref[...], b_ref[...], 1, 1 if tb else 0)
        k = pl.program_id(2)
        first = jnp.logical_and(pl.program_id(0) == 0, pl.program_id(1) == 0)
        if nk > 1:
            acc = refs[-1]

            @pl.when(k == 0)
            def _():
                acc[...] = part

            @pl.when(jnp.logical_and(k > 0, k < nk - 1))
            def _():
                acc[...] += part

        @pl.when(k == nk - 1)
        def _():
            total = acc[...] + part if nk > 1 else part
            res = post(total, *[r[...].astype(F32) for r in row_refs], *[p[...] for p in par_refs])
            for o_ref, val in zip(out_refs[:no], res[:no]):
                o_ref[...] = val.astype(o_ref.dtype)
            for s_ref, val in zip(out_refs[no:], res[no:]):
                @pl.when(first)
                def _(s_ref=s_ref):
                    s_ref[...] = jnp.zeros_like(s_ref)
                s_ref[...] += val

    b_spec = pl.BlockSpec((tn, tk), lambda i, j, k: (j, k)) if tb else pl.BlockSpec((tk, tn), lambda i, j, k: (k, j))
    in_specs = [pl.BlockSpec((tm, tk), lambda i, j, k: (i, k)), b_spec]
    in_specs += [pl.BlockSpec((tm, w), lambda i, j, k, cb=cb: (i, cb(j))) for _, w, cb in rows]
    in_specs += [_full(p) for p in params] + [ANY] * len(aliased)
    out_specs = [pl.BlockSpec((tm, w), lambda i, j, k, cb=cb: (i, cb(j))) for _, w, _, cb, _ in outs]
    out_specs += [_full(p) for p in params[:n_sums]]
    out_shape = [jax.ShapeDtypeStruct((m_, tot), dt) for tot, _, dt, _, _ in outs]
    out_shape += [jax.ShapeDtypeStruct(p.shape, F32) for p in params[:n_sums]]
    aliases, pos = {}, nin
    for oi, o in enumerate(outs):
        if o[4] is not None:
            aliases[pos] = oi
            pos += 1
    return pl.pallas_call(
        body, name=name, grid=(m_ // tm, n_ // tn, nk), in_specs=in_specs, out_specs=out_specs, out_shape=out_shape,
        scratch_shapes=[pltpu.VMEM((tm, tn), F32)] if nk > 1 else [], input_output_aliases=aliases,
        compiler_params=_params(3))(a, b, *[r[0] for r in rows], *params, *aliased)


def _row_spec(tb, w, c0, percol):
    return pl.BlockSpec((tb, w), lambda i, j, c0=c0, pc=percol: (i, c0 + (j if pc else 0)))


def _stage_fwd(f, name, rows, params, outs, tb, ncol=1):
    t_ = rows[0][0].shape[0]
    nr, npar = len(rows), len(params)

    def body(*refs):
        res = f(*[r[...].astype(F32) for r in refs[:nr + npar]])
        for o_ref, val in zip(refs[nr + npar:], res):
            o_ref[...] = val.astype(o_ref.dtype)

    return pl.pallas_call(
        body, name=name, grid=(t_ // tb, ncol),
        in_specs=[_row_spec(tb, w, c0, pc) for (_, w, c0, pc) in rows] + [_full(p) for p in params],
        out_specs=[pl.BlockSpec((tb, w), lambda i, j: (i, j)) for w, _ in outs],
        out_shape=[jax.ShapeDtypeStruct((t_, w * ncol), dt) for w, dt in outs],
        compiler_params=_params(2))(*[r[0] for r in rows], *params)


def _into(dest, tb, width, t_, ncol, dtype):
    if dest is None:
        return pl.BlockSpec((tb, width), lambda i, j: (i, j)), jax.ShapeDtypeStruct((t_, width * ncol), dtype), None
    buf, total, c0 = dest
    return (pl.BlockSpec((tb, width), lambda i, j, c0=c0: (i, c0 + j)), jax.ShapeDtypeStruct((t_, total), dtype), buf)


def _stage_bwd(f, name, rows, params, douts, tb, ncol=1, cat=None, dtypes=None, dest=None):
    t_ = rows[0][0].shape[0]
    nr, npar, nd = len(rows), len(params), len(douts)
    cat = cat if cat is not None else [[r] for r in range(nr)]
    dtypes = dtypes if dtypes is not None else [F32] * len(cat)
    dest = dest or {}
    assert ncol == 1 or all(len(g) == 1 and rows[g[0]][3] for g in cat)
    nin = nr + npar + nd

    def body(*refs):
        ins = [r[...].astype(F32) for r in refs[:nr + npar]]
        dvals = tuple(r[...].astype(F32) for r in refs[nr + npar:nin])
        out_refs = refs[nin + len(aliased):]
        _, vjp = jax.vjp(f, *ins)
        grads = vjp(dvals)
        for o_ref, grp in zip(out_refs[:len(cat)], cat):
            val = grads[grp[0]] if len(grp) == 1 else jnp.concatenate([grads[r] for r in grp], axis=-1)
            o_ref[...] = val.astype(o_ref.dtype)
        first = jnp.logical_and(pl.program_id(0) == 0, pl.program_id(1) == 0)
        for p_ref, gp in zip(out_refs[len(cat):], grads[nr:]):
            @pl.when(first)
            def _(p_ref=p_ref):
                p_ref[...] = jnp.zeros_like(p_ref)
            p_ref[...] += gp

    gw = [sum(rows[r][1] for r in grp) for grp in cat]
    out_specs, out_shape, aliased, aliases = [], [], [], {}
    for gi, (w, dt) in enumerate(zip(gw, dtypes)):
        spec, shape, buf = _into(dest.get(gi), tb, w, t_, ncol, dt)
        out_specs.append(spec)
        out_shape.append(shape)
        if buf is not None:
            aliases[nin + len(aliased)] = gi
            aliased.append(buf)
    return pl.pallas_call(
        body, name=name, grid=(t_ // tb, ncol),
        in_specs=[_row_spec(tb, w, c0, pc) for (_, w, c0, pc) in rows] + [_full(p) for p in params]
        + [pl.BlockSpec((tb, d.shape[1] // ncol), lambda i, j: (i, j)) for d in douts] + [ANY] * len(aliased),
        out_specs=out_specs + [_full(p) for p in params],
        out_shape=out_shape + [jax.ShapeDtypeStruct(p.shape, F32) for p in params],
        input_output_aliases=aliases,
        compiler_params=_params(2))(*[r[0] for r in rows], *params, *douts, *aliased)


def _ln_f(x, y, g, b):
    u = ALPHA * x + y
    c = u - jnp.mean(u, axis=-1, keepdims=True)
    var = jnp.mean(c * c, axis=-1, keepdims=True)
    return (c * lax.rsqrt(var + LN_EPS) * g + b,)


def _ln_f2(x, y, g, b):
    out, = _ln_f(x, y, g, b)
    return out, out


def _swiglu_tile(gu):
    half = gu.shape[1] // 2
    return (jax.nn.silu(gu[:, :half]) * gu[:, half:],)


def _merge_f(ya, yb, ga, gb):
    return (jax.nn.sigmoid(ga) * ya + jax.nn.sigmoid(gb) * yb,)


def _gnorm_f(of, ob, z, w):
    o = of + ob
    return (o * lax.rsqrt(jnp.mean(o * o, axis=-1, keepdims=True) + RMS_EPS) * w * jax.nn.silu(z),)


def _bg_f(x, arow, dtrow):
    lane = lax.broadcasted_iota(jnp.int32, x.shape, 1)
    beta = jax.nn.sigmoid(x)
    g = -jnp.exp(arow) * jax.nn.softplus(x + dtrow)
    return (jnp.where(lane < 16, beta, jnp.where(lane < 32, g, 0.0)),)


PREP_ROWS = 512
PAD = 8


def _prep_f(part, w, *wins):
    xc = wins[0] * w[0:1, :]
    for k in range(1, DN_CONV):
        xc = xc + wins[k] * w[k:k + 1, :]
    a = jax.nn.silu(xc)
    nrm = a * lax.rsqrt(jnp.sum(a * a, axis=-1, keepdims=True) + RMS_EPS)
    return jnp.where(part == 0, nrm * (DN_DIM ** -0.5), jnp.where(part == 1, nrm, a))


def _windows(pad_ref, r0, rows):
    return [pad_ref[PAD + r0 - 2 + k:PAD + r0 - 2 + k + rows, :] for k in range(DN_CONV)]


def _prep_fwd(pm, conv):
    t_ = pm.shape[0]
    rows = min(PREP_ROWS, t_)

    def body(x_ref, w_ref, o_ref, pad_ref):
        part = pl.program_id(0) // DN_HEADS
        pad_ref[0:PAD, :] = jnp.zeros((PAD, DN_DIM), F32)
        pad_ref[PAD + t_:2 * PAD + t_, :] = jnp.zeros((PAD, DN_DIM), F32)
        pad_ref[PAD:PAD + t_, :] = x_ref[...]
        w = w_ref[...]
        for r in range(t_ // rows):
            o_ref[r * rows:(r + 1) * rows, :] = _prep_f(part, w, *_windows(pad_ref, r * rows, rows))

    ncb = 3 * DN_HEADS
    return pl.pallas_call(
        body, name="prep_fwd", grid=(ncb,),
        in_specs=[pl.BlockSpec((t_, DN_DIM), lambda j: (0, j)), pl.BlockSpec((DN_CONV, DN_DIM), lambda j: (0, j))],
        out_specs=pl.BlockSpec((t_, DN_DIM), lambda j: (0, j)),
        out_shape=jax.ShapeDtypeStruct((t_, ncb * DN_DIM), F32),
        scratch_shapes=[pltpu.VMEM((t_ + 2 * PAD, DN_DIM), F32)],
        compiler_params=_params(1))(pm, conv)


def _prep_bwd(pm, conv, dout, dpm):
    t_ = pm.shape[0]
    rows = min(PREP_ROWS, t_)

    def body(x_ref, w_ref, d_ref, _, dx_ref, dw_ref, pad_ref, dpad_ref):
        part = pl.program_id(0) // DN_HEADS
        pad_ref[0:PAD, :] = jnp.zeros((PAD, DN_DIM), F32)
        pad_ref[PAD + t_:2 * PAD + t_, :] = jnp.zeros((PAD, DN_DIM), F32)
        pad_ref[PAD:PAD + t_, :] = x_ref[...]
        dpad_ref[...] = jnp.zeros_like(dpad_ref)
        w = w_ref[...]
        dw = jnp.zeros((DN_CONV, DN_DIM), F32)
        for r in range(t_ // rows):
            r0 = r * rows
            _, vjp = jax.vjp(functools.partial(_prep_f, part), w, *_windows(pad_ref, r0, rows))
            grads = vjp(d_ref[r0:r0 + rows, :])
            dw = dw + grads[0]
            for k in range(DN_CONV):
                lo = PAD + r0 - 2 + k
                dpad_ref[lo:lo + rows, :] += grads[1 + k]
        dx_ref[...] = dpad_ref[PAD:PAD + t_, :].astype(dx_ref.dtype)
        dw_ref[...] = dw

    ncb = 3 * DN_HEADS
    col = pl.BlockSpec((t_, DN_DIM), lambda j: (0, j))
    wsp = pl.BlockSpec((DN_CONV, DN_DIM), lambda j: (0, j))
    return pl.pallas_call(
        body, name="prep_bwd", grid=(ncb,), in_specs=[col, wsp, col, ANY], out_specs=[col, wsp],
        out_shape=[jax.ShapeDtypeStruct(dpm.shape, dpm.dtype), jax.ShapeDtypeStruct((DN_CONV, ncb * DN_DIM), F32)],
        scratch_shapes=[pltpu.VMEM((t_ + 2 * PAD, DN_DIM), F32), pltpu.VMEM((t_ + 2 * PAD, DN_DIM), F32)],
        input_output_aliases={3: 0}, compiler_params=_params(1))(pm, conv, dout, dpm)


def _dn_chunk(sgns, qs, ks, vs, grows, brows, tsaved=None, with_t=False):
    c = qs[0].shape[0]
    i = lax.broadcasted_iota(jnp.int32, (c, c), 0)
    j = lax.broadcasted_iota(jnp.int32, (c, c), 1)
    eye = i == j
    incl = {s: (i - j) * int(s) >= 0 for s in set(sgns)}
    strict = {s: (i - j) * int(s) > 0 for s in set(sgns)}
    gcs = [jnp.sum(jnp.where(incl[s], g, 0.0), axis=1, keepdims=True) for s, g in zip(sgns, grows)]
    grs = [jnp.sum(jnp.where(eye, gc, 0.0), axis=0, keepdims=True) for gc in gcs]
    bcs = [jnp.sum(jnp.where(eye, b, 0.0), axis=1, keepdims=True) for b in brows]
    gls = [jnp.sum(g, axis=1, keepdims=True) for g in grows]
    decs = [jnp.exp(jnp.where(incl[s], gc - gr, -1e30)) for s, gc, gr in zip(sgns, gcs, grs)]
    kks = [_nt(k, k) for k in ks]
    amats = tuple(jnp.where(strict[s], bc * kk * dec, 0.0) for s, bc, kk, dec in zip(sgns, bcs, kks, decs))
    tinvs = _inv(amats) if tsaved is None else _inv_saved(amats, tsaved)
    egcs = [jnp.exp(gc) for gc in gcs]
    us = [_nn(t, v * bc) for t, v, bc in zip(tinvs, vs, bcs)]
    ws = [_nn(t, k * (bc * egc)) for t, k, bc, egc in zip(tinvs, ks, bcs, egcs)]
    qks = [_nt(q, k) * dec for q, k, dec in zip(qs, ks, decs)]
    qds = [q * egc for q, egc in zip(qs, egcs)]
    kds = [k * jnp.exp(gl - gc) for k, gl, gc in zip(ks, gls, gcs)]
    res = tuple(us), tuple(ws), tuple(qks), tuple(qds), tuple(kds)
    return res + (tinvs,) if with_t else res


def _dn_step(us, ws, qks, qds, kds, grows, ss):
    gls = [jnp.exp(jnp.sum(g, axis=1, keepdims=True)) for g in grows]
    wss = [_nn(w, s) for w, s in zip(ws, ss)]
    qss = [_nn(qd, s) for qd, s in zip(qds, ss)]
    vns = [u - x for u, x in zip(us, wss)]
    os_ = [a + _nn(qk, vn) for a, qk, vn in zip(qss, qks, vns)]
    s2s = [s * gl + _tn(kd, vn) for s, gl, kd, vn in zip(ss, gls, kds, vns)]
    return tuple(os_), tuple(s2s)


def _hs(h):
    return slice(h * DN_DIM, (h + 1) * DN_DIM)


_DIR_SGN = (1, -1)
_A_OUT = 5
_PROBLEMS = [(d, h) for d in range(2) for h in range(DN_HEADS)]
_SGNS = [_DIR_SGN[d] for d, _ in _PROBLEMS]


def _chunk_inputs(q_ref, k_ref, v_ref, g_ref, b_ref):
    heads = lambda ref: tuple(ref[:, _hs(h)].astype(F32) for _, h in _PROBLEMS)
    rows = lambda ref: tuple(ref[d, 0, h:h + 1, :] for d, h in _PROBLEMS)
    return heads(q_ref), heads(k_ref), heads(v_ref), rows(g_ref), rows(b_ref)


def _dn_a_fwd(qkv, grows, brows):
    t_ = qkv.shape[0]
    nch = t_ // CHUNK

    def body(q_ref, k_ref, v_ref, g_ref, b_ref, *outs):
        us, ws, qks, qds, kds, tinvs = _dn_chunk(_SGNS, *_chunk_inputs(q_ref, k_ref, v_ref, g_ref, b_ref), with_t=True)
        for p, (d, h) in enumerate(_PROBLEMS):
            u_ref, w_ref, qk_ref, qd_ref, kd_ref = outs[d * _A_OUT:(d + 1) * _A_OUT]
            u_ref[:, _hs(h)], w_ref[:, _hs(h)], qk_ref[0, h] = us[p], ws[p].astype(_MXU), qks[p].astype(_MXU)
            qd_ref[:, _hs(h)], kd_ref[:, _hs(h)] = qds[p].astype(_MXU), kds[p].astype(_MXU)
            outs[2 * _A_OUT + d][0, h] = tinvs[p]

    rspec = pl.BlockSpec((2, 1, DN_HEADS, CHUNK), lambda c: (0, c, 0, 0))
    big = pl.BlockSpec((CHUNK, D), lambda c: (c, 0))
    qks = pl.BlockSpec((1, DN_HEADS, CHUNK, CHUNK), lambda c: (c, 0, 0, 0))
    bigs = lambda dt: jax.ShapeDtypeStruct((t_, D), dt)
    qksh = lambda dt: jax.ShapeDtypeStruct((nch, DN_HEADS, CHUNK, CHUNK), dt)
    res = pl.pallas_call(
        body, name="dn_a_fwd", grid=(nch,),
        in_specs=[pl.BlockSpec((CHUNK, D), lambda c, p=p: (c, p)) for p in range(3)] + [rspec, rspec],
        out_specs=[big, big, qks, big, big] * 2 + [qks, qks],
        out_shape=[bigs(F32), bigs(_MXU), qksh(_MXU), bigs(_MXU), bigs(_MXU)] * 2 + [qksh(F32)] * 2,
        compiler_params=_params(1))(qkv, qkv, qkv, grows, brows)
    return res[:2 * _A_OUT], res[2 * _A_OUT:]


def _dn_a_bwd(qkv, grows, brows, tinv, dres, dg_b):
    t_ = qkv.shape[0]
    nch = t_ // CHUNK

    def body(q_ref, k_ref, v_ref, g_ref, b_ref, tf_ref, tb_ref, *rest):
        dins, dgb_ref, (dqkv_ref, dg_ref, db_ref) = rest[:2 * _A_OUT], rest[2 * _A_OUT], rest[2 * _A_OUT + 1:]
        tsaved = tuple((tf_ref, tb_ref)[d][0, h] for d, h in _PROBLEMS)
        _, vjp = jax.vjp(functools.partial(_dn_chunk, _SGNS, tsaved=tsaved),
                         *_chunk_inputs(q_ref, k_ref, v_ref, g_ref, b_ref))
        cots = []
        for o in range(_A_OUT):
            cots.append(tuple(dins[d * _A_OUT + o][0, h] if o == 2 else dins[d * _A_OUT + o][:, _hs(h)]
                              for d, h in _PROBLEMS))
        gq, gk, gv, gg, gb = vjp(tuple(cots))
        for p, (d, h) in enumerate(_PROBLEMS):
            dg_ref[d, 0, h:h + 1, :] = gg[p] + dgb_ref[d, 0, h:h + 1, :]
            db_ref[d, 0, h:h + 1, :] = gb[p]
        for h in range(DN_HEADS):
            dqkv_ref[:, _hs(h)] = gq[h] + gq[DN_HEADS + h]
            dqkv_ref[:, _hs(DN_HEADS + h)] = gk[h] + gk[DN_HEADS + h]
            dqkv_ref[:, _hs(2 * DN_HEADS + h)] = gv[h] + gv[DN_HEADS + h]

    rspec = pl.BlockSpec((2, 1, DN_HEADS, CHUNK), lambda c: (0, c, 0, 0))
    big = pl.BlockSpec((CHUNK, D), lambda c: (c, 0))
    qks = pl.BlockSpec((1, DN_HEADS, CHUNK, CHUNK), lambda c: (c, 0, 0, 0))
    rsh = jax.ShapeDtypeStruct(grows.shape, F32)
    return pl.pallas_call(
        body, name="dn_a_bwd", grid=(nch,),
        in_specs=[pl.BlockSpec((CHUNK, D), lambda c, p=p: (c, p)) for p in range(3)] + [rspec, rspec, qks, qks]
        + [big, big, qks, big, big] * 2 + [rspec],
        out_specs=[pl.BlockSpec((CHUNK, 3 * D), lambda c: (c, 0)), rspec, rspec],
        out_shape=[jax.ShapeDtypeStruct((t_, 3 * D), F32), rsh, rsh],
        compiler_params=_params(1))(qkv, qkv, qkv, grows, brows, *tinv, *dres, dg_b)


def _dir_specs(nch):
    def cidx(d):
        return (lambda n: n) if d == 0 else (lambda n: nch - 1 - n)
    out = []
    for d in range(2):
        ci = cidx(d)
        big = pl.BlockSpec((CHUNK, D), lambda n, ci=ci: (ci(n), 0))
        qks = pl.BlockSpec((1, DN_HEADS, CHUNK, CHUNK), lambda n, ci=ci: (ci(n), 0, 0, 0))
        row = pl.BlockSpec((1, 1, DN_HEADS, CHUNK), lambda n, ci=ci, d=d: (d, ci(n), 0, 0))
        st = pl.BlockSpec((1, DN_HEADS, DN_DIM, DN_DIM), lambda n, ci=ci: (ci(n), 0, 0, 0))
        out.append(dict(big=big, qk=qks, row=row, st=st))
    return out


def _step_inputs(ins, per_dir):
    def pick(o):
        if o == 2:
            return tuple(ins[d * per_dir + o][0, h].astype(F32) for d, h in _PROBLEMS)
        if o == 5:
            return tuple(ins[d * per_dir + o][0, 0, h:h + 1, :] for d, h in _PROBLEMS)
        return tuple(ins[d * per_dir + o][:, _hs(h)].astype(F32) for d, h in _PROBLEMS)
    return [pick(o) for o in range(6)]


def _dn_b_fwd(ares, grows):
    t_ = ares[0].shape[0]
    nch = t_ // CHUNK
    sp = _dir_specs(nch)

    def body(*refs):
        ins, outs, s_ref = refs[:12], refs[12:16], refs[16]

        @pl.when(pl.program_id(0) == 0)
        def _():
            s_ref[...] = jnp.zeros_like(s_ref)

        ss = tuple(s_ref[p] for p in range(len(_PROBLEMS)))
        os_, s2s = _dn_step(*_step_inputs(ins, 6), ss)
        for p, (d, h) in enumerate(_PROBLEMS):
            outs[2 + d][0, h] = ss[p]
            outs[d][:, _hs(h)] = os_[p]
            s_ref[p] = s2s[p]

    in_specs, args = [], []
    for d in range(2):
        in_specs += [sp[d]["big"], sp[d]["big"], sp[d]["qk"], sp[d]["big"], sp[d]["big"], sp[d]["row"]]
        args += list(ares[d * _A_OUT:(d + 1) * _A_OUT]) + [grows]
    stsh = jax.ShapeDtypeStruct((nch, DN_HEADS, DN_DIM, DN_DIM), F32)
    osh = jax.ShapeDtypeStruct((t_, D), F32)
    return pl.pallas_call(
        body, name="dn_b_fwd", grid=(nch,), in_specs=in_specs,
        out_specs=[sp[0]["big"], sp[1]["big"], sp[0]["st"], sp[1]["st"]], out_shape=[osh, osh, stsh, stsh],
        scratch_shapes=[pltpu.VMEM((2 * DN_HEADS, DN_DIM, DN_DIM), F32)],
        compiler_params=_params(1))(*args)


def _dn_b_bwd(ares, grows, st_f, st_b, do):
    t_ = ares[0].shape[0]
    nch = t_ // CHUNK
    sp = _dir_specs(nch)
    rsp = [sp[1], sp[0]]

    def body(*refs):
        ins, outs, ds_ref = refs[:16], refs[16:28], refs[28]

        @pl.when(pl.program_id(0) == 0)
        def _():
            ds_ref[...] = jnp.zeros_like(ds_ref)

        ss = tuple(ins[d * 8 + 6][0, h] for d, h in _PROBLEMS)
        _, vjp = jax.vjp(_dn_step, *_step_inputs(ins, 8), ss)
        dos = tuple(ins[d * 8 + 7][:, _hs(h)] for d, h in _PROBLEMS)
        grads = vjp((dos, tuple(ds_ref[p] for p in range(len(_PROBLEMS)))))
        for p, (d, h) in enumerate(_PROBLEMS):
            du_ref, dw_ref, dqk_ref, dqd_ref, dkd_ref, dg_ref = outs[d * 6:(d + 1) * 6]
            du_ref[:, _hs(h)], dw_ref[:, _hs(h)], dqk_ref[0, h] = grads[0][p], grads[1][p], grads[2][p]
            dqd_ref[:, _hs(h)], dkd_ref[:, _hs(h)] = grads[3][p], grads[4][p]
            dg_ref[0, 0, h:h + 1, :] = grads[5][p]
            ds_ref[p] = grads[6][p]

    in_specs, args, out_specs, out_shape = [], [], [], []
    big_sh = jax.ShapeDtypeStruct((t_, D), F32)
    qk_sh = jax.ShapeDtypeStruct((nch, DN_HEADS, CHUNK, CHUNK), F32)
    row_sh = jax.ShapeDtypeStruct((1, nch, DN_HEADS, CHUNK), F32)
    for d in range(2):
        s = rsp[d]
        row0 = pl.BlockSpec((1, 1, DN_HEADS, CHUNK), lambda m, d=d: (0, (nch - 1 - m) if d == 0 else m, 0, 0))
        rowd = pl.BlockSpec((1, 1, DN_HEADS, CHUNK), lambda m, d=d: (d, (nch - 1 - m) if d == 0 else m, 0, 0))
        in_specs += [s["big"], s["big"], s["qk"], s["big"], s["big"], rowd, s["st"], s["big"]]
        args += list(ares[d * _A_OUT:(d + 1) * _A_OUT]) + [grows, (st_f, st_b)[d], do]
        out_specs += [s["big"], s["big"], s["qk"], s["big"], s["big"], row0]
        out_shape += [big_sh, big_sh, qk_sh, big_sh, big_sh, row_sh]
    res = pl.pallas_call(
        body, name="dn_b_bwd", grid=(nch,), in_specs=in_specs, out_specs=out_specs, out_shape=out_shape,
        scratch_shapes=[pltpu.VMEM((2 * DN_HEADS, DN_DIM, DN_DIM), F32)],
        compiler_params=_params(1))(*args)
    dares = list(res[0:5]) + list(res[6:11])
    return dares, jnp.concatenate([res[5], res[11]], axis=0)


@jax.custom_vjp
def _rot_half(x):
    half, width = SW_DIM // 2, x.shape[1]
    first = lax.broadcasted_iota(jnp.int32, x.shape, 1) % SW_DIM < half
    return jnp.where(first, -pltpu.roll(x, width - half, axis=1), pltpu.roll(x, half, axis=1))


_rot_half.defvjp(lambda x: (_rot_half(x), None), lambda _, g: (-_rot_half(g),))


def _rope(x, c, s):
    reps = x.shape[1] // c.shape[1]
    return x * jnp.tile(c, (1, reps)) + _rot_half(x) * jnp.tile(s, (1, reps))


def _rope_t(g, c, s):
    reps = g.shape[1] // c.shape[1]
    return g * jnp.tile(c, (1, reps)) - _rot_half(g * jnp.tile(s, (1, reps)))


_SW_SCALE = SW_DIM ** -0.5
_KV_HEADS = [[kvh * SW_GRP + g for g in range(SW_GRP)] for kvh in range(SW_KV)]


def _by_group(x):
    return [jnp.concatenate([x[:, h * SW_DIM:(h + 1) * SW_DIM] for h in hs], axis=0) for hs in _KV_HEADS]


def _from_groups(xs):
    return jnp.concatenate([x[g * SW_BLOCK:(g + 1) * SW_BLOCK] for x in xs for g in range(SW_GRP)], axis=-1)


def _attn_probs(blk, t_, cq, sq, ck, sk, q, kall, sinks):
    qgs = _by_group(_rope(q, cq, sq))
    kr = _rope(kall, ck, sk)
    khs = [kr[:, kvh * SW_DIM:(kvh + 1) * SW_DIM] for kvh in range(SW_KV)]
    nq, nk = SW_GRP * SW_BLOCK, 3 * SW_BLOCK
    qpos = lax.broadcasted_iota(jnp.int32, (nq, nk), 0) % SW_BLOCK
    krel = lax.broadcasted_iota(jnp.int32, (nq, nk), 1) - SW_BLOCK
    kglob = krel + blk * SW_BLOCK
    valid = (jnp.abs(qpos - krel) <= SW_BLOCK) & (kglob >= 0) & (kglob < t_)
    ss = [jnp.where(valid, _nt(qg * _SW_SCALE, kh), -1e30) for qg, kh in zip(qgs, khs)]
    snks = [jnp.concatenate([jnp.broadcast_to(sinks[:, h:h + 1], (SW_BLOCK, 1)) for h in hs], axis=0) for hs in _KV_HEADS]
    ms = [jnp.maximum(jnp.max(s, axis=-1, keepdims=True), snk) for s, snk in zip(ss, snks)]
    es = [jnp.exp(s - m) for s, m in zip(ss, ms)]
    esnks = [jnp.exp(snk - m) for snk, m in zip(snks, ms)]
    invs = [1.0 / (jnp.sum(e, axis=-1, keepdims=True) + esnk) for e, esnk in zip(es, esnks)]
    ps = [e * inv for e, inv in zip(es, invs)]
    return qgs, khs, ps, [esnk * inv for esnk, inv in zip(esnks, invs)]


def _attn_f(blk, t_, cq, sq, ck, sk, q, kp, ko, kn, vp, vo, vn, sinks):
    _, _, ps, _ = _attn_probs(blk, t_, cq, sq, ck, sk, q, jnp.concatenate([kp, ko, kn], axis=0), sinks)
    vall = jnp.concatenate([vp, vo, vn], axis=0)
    return _from_groups([_nn(p, vall[:, kvh * SW_DIM:(kvh + 1) * SW_DIM]) for kvh, p in enumerate(ps)])


def _attn_bwd_f(blk, t_, cq, sq, ck, sk, q, kp, ko, kn, vp, vo, vn, sinks, do):
    qgs, khs, ps, psinks = _attn_probs(blk, t_, cq, sq, ck, sk, q, jnp.concatenate([kp, ko, kn], axis=0), sinks)
    vall = jnp.concatenate([vp, vo, vn], axis=0)
    vhs = [vall[:, kvh * SW_DIM:(kvh + 1) * SW_DIM] for kvh in range(SW_KV)]
    dogs = _by_group(do)
    dvs = [_tn(p, dog) for p, dog in zip(ps, dogs)]
    dps = [_nt(dog, vh) for dog, vh in zip(dogs, vhs)]
    deltas = [jnp.sum(p * dp, axis=-1, keepdims=True) for p, dp in zip(ps, dps)]
    dss = [p * ((dp - delta) * _SW_SCALE) for p, dp, delta in zip(ps, dps, deltas)]
    dqr = _from_groups([_nn(ds, kh) for ds, kh in zip(dss, khs)])
    dkr = jnp.concatenate([_tn(ds, qg) for ds, qg in zip(dss, qgs)], axis=-1)
    dsnk = [-(psink * delta) for psink, delta in zip(psinks, deltas)]
    dsinks = jnp.concatenate([jnp.sum(d[g * SW_BLOCK:(g + 1) * SW_BLOCK], axis=0, keepdims=True)
                              for d in dsnk for g in range(SW_GRP)], axis=1)
    dq, dk, dv = _rope_t(dqr, cq, sq), _rope_t(dkr, ck, sk), jnp.concatenate(dvs, axis=-1)
    blocks = lambda a: [a[j * SW_BLOCK:(j + 1) * SW_BLOCK] for j in range(3)]
    return [dq] + blocks(dk) + blocks(dv) + [dsinks]


def _attn_specs(nb):
    prv = lambda i: jnp.maximum(i - 1, 0)
    nxt = lambda i: jnp.minimum(i + 1, nb - 1)
    rows = [lambda i: i, prv, lambda i: i, nxt]
    tab = [pl.BlockSpec((SW_BLOCK, 128), lambda i, r=r: (r(i), 0)) for r in rows]
    qs = pl.BlockSpec((SW_BLOCK, SW_HEADS * SW_DIM), lambda i: (i, O_QS // (SW_HEADS * SW_DIM)))
    kw = SW_KV * SW_DIM
    ks = [pl.BlockSpec((SW_BLOCK, kw), lambda i, r=r: (r(i), O_KS // kw)) for r in rows[1:]]
    vs = [pl.BlockSpec((SW_BLOCK, kw), lambda i, r=r: (r(i), O_VS // kw)) for r in rows[1:]]
    return tab, qs, ks, vs


def _attn_tables(refs):
    cq, cp, co, cn, sq, sp_, so, sn = [r[...] for r in refs]
    return cq, sq, jnp.concatenate([cp, co, cn], axis=0), jnp.concatenate([sp_, so, sn], axis=0)


def _attn_fwd(pm, cos, sin, sinks):
    t_ = pm.shape[0]
    nb = t_ // SW_BLOCK
    tab, qs, ks, vs = _attn_specs(nb)

    def body(*refs):
        tabs = _attn_tables(refs[:8])
        vals = [r[...] for r in refs[8:16]]
        refs[16][...] = _attn_f(pl.program_id(0), t_, *tabs, *vals).astype(refs[16].dtype)

    return pl.pallas_call(
        body, name="attn_fwd", grid=(nb,), in_specs=tab + tab + [qs] + ks + vs + [_full(sinks)],
        out_specs=pl.BlockSpec((SW_BLOCK, D), lambda i: (i, 0)), out_shape=jax.ShapeDtypeStruct((t_, D), _MXU),
        compiler_params=_params(1))(*([cos] * 4), *([sin] * 4), pm, pm, pm, pm, pm, pm, pm, sinks)


def _attn_bwd(pm, cos, sin, sinks, do, dpm):
    t_ = pm.shape[0]
    nb = t_ // SW_BLOCK
    tab, qs, ks, vs = _attn_specs(nb)
    kw = SW_KV * SW_DIM

    def body(*refs):
        tabs = _attn_tables(refs[:8])
        vals = [r[...] for r in refs[8:16]]
        do_ref, outs = refs[16], refs[18:]
        grads = _attn_bwd_f(pl.program_id(0), t_, *tabs, *vals, do_ref[...])
        for o_ref, g in zip(outs[:7], grads[:7]):
            o_ref[...] = g.astype(o_ref.dtype)

        @pl.when(pl.program_id(0) == 0)
        def _():
            outs[7][...] = jnp.zeros_like(outs[7])
        outs[7][...] += grads[7]

    own = lambda w: pl.BlockSpec((SW_BLOCK, w), lambda i: (i, 0))
    return pl.pallas_call(
        body, name="attn_bwd", grid=(nb,),
        in_specs=tab + tab + [qs] + ks + vs + [_full(sinks), own(D), ANY],
        out_specs=[pl.BlockSpec((SW_BLOCK, D), lambda i: (i, O_QS // D))] + [own(kw)] * 6 + [_full(sinks)],
        out_shape=[jax.ShapeDtypeStruct(dpm.shape, dpm.dtype)] + [jax.ShapeDtypeStruct((t_, kw), F32)] * 6
        + [jax.ShapeDtypeStruct(sinks.shape, F32)],
        input_output_aliases={17: 0},
        compiler_params=_params(1))(*([cos] * 4), *([sin] * 4), pm, pm, pm, pm, pm, pm, pm, sinks, do, dpm)


def _band_sum(kparts, vparts, dpm):
    t_, kw = kparts[1].shape
    nb = t_ // SW_BLOCK

    def body(kp, ko, kn, vp, vo, vn, _, out_ref):
        j = pl.program_id(0)
        band = lambda p, o, n: o[...] + jnp.where(j + 1 < nb, p[...], 0.0) + jnp.where(j > 0, n[...], 0.0)
        out_ref[...] = jnp.concatenate([band(kp, ko, kn), band(vp, vo, vn)], axis=-1).astype(out_ref.dtype)

    specs = [pl.BlockSpec((SW_BLOCK, kw), lambda j: (jnp.minimum(j + 1, nb - 1), 0)),
             pl.BlockSpec((SW_BLOCK, kw), lambda j: (j, 0)),
             pl.BlockSpec((SW_BLOCK, kw), lambda j: (jnp.maximum(j - 1, 0), 0))]
    return pl.pallas_call(
        body, name="band_sum", grid=(nb,), in_specs=specs * 2 + [ANY],
        out_specs=pl.BlockSpec((SW_BLOCK, 2 * kw), lambda j: (j, O_KS // (2 * kw))),
        out_shape=jax.ShapeDtypeStruct(dpm.shape, dpm.dtype), input_output_aliases={6: 0},
        compiler_params=_params(1))(*kparts, *vparts, dpm)


def _loss_head(y, target, tb=256):
    t_ = y.shape[0]

    def body(y_ref, t_ref, dy_ref, acc_ref):
        err = y_ref[...] - t_ref[...]
        dy_ref[...] = err * (1.0 / D)
        sq = (err * err).reshape(tb // 8, 8, D).sum(axis=0)
        part = sq[:, 0:128]
        for c in range(1, D // 128):
            part = part + sq[:, c * 128:(c + 1) * 128]

        @pl.when(pl.program_id(0) == 0)
        def _():
            acc_ref[...] = jnp.zeros_like(acc_ref)
        acc_ref[...] += part

    row = pl.BlockSpec((tb, D), lambda i: (i, 0))
    return pl.pallas_call(
        body, name="loss_head", grid=(t_ // tb,), in_specs=[row, row],
        out_specs=[row, pl.BlockSpec((8, 128), lambda i: (0, 0))],
        out_shape=[jax.ShapeDtypeStruct((t_, D), F32), jax.ShapeDtypeStruct((8, 128), F32)],
        compiler_params=_params(1))(y, target)


def _adamw(name, w, g, m, v):
    shape = w.shape
    cols = shape[-1]
    rows = w.size // cols
    w2, g2, m2, v2 = [a.reshape(rows, cols) for a in (w, g, m, v)]
    tb = rows
    while tb * cols * 4 > (1 << 20) and tb % 16 == 0:
        tb //= 2
    bc1 = 1.0 - ADAM_B1 ** ADAM_STEP
    bc2 = 1.0 - ADAM_B2 ** ADAM_STEP

    def body(w_ref, g_ref, m_ref, v_ref, d_ref, nm_ref, nv_ref):
        gv = g_ref[...]
        nm = ADAM_B1 * m_ref[...] + (1.0 - ADAM_B1) * gv
        nv = ADAM_B2 * v_ref[...] + (1.0 - ADAM_B2) * (gv * gv)
        d_ref[...] = -ADAM_LR * ((nm / bc1) / (jnp.sqrt(nv / bc2) + ADAM_EPS) + ADAM_WD * w_ref[...])
        nm_ref[...] = nm
        nv_ref[...] = nv

    spec = pl.BlockSpec((tb, cols), lambda i: (i, 0))
    sh = jax.ShapeDtypeStruct((rows, cols), F32)
    outs = pl.pallas_call(body, name=name, grid=(rows // tb,), in_specs=[spec] * 4, out_specs=[spec] * 3,
                          out_shape=[sh] * 3, compiler_params=_params(1))(w2, g2, m2, v2)
    return [o.reshape(shape) for o in outs]


def _to_rows(bg, col0):
    t_ = bg.shape[0]
    a = bg[:, col0:col0 + 2 * DN_HEADS].reshape(t_ // CHUNK, CHUNK, 2, DN_HEADS)
    return jnp.transpose(a, (2, 0, 3, 1))


def _from_rows(db, dg):
    nch = db.shape[1]
    back = lambda a: jnp.transpose(a, (1, 3, 0, 2)).reshape(nch * CHUNK, 2 * DN_HEADS)
    return jnp.pad(jnp.concatenate([back(db), back(dg)], axis=1), ((0, 0), (0, 128 - 4 * DN_HEADS)))


def _layer_fwd(x, xm, w, rest, cos, sin):
    t_ = x.shape[0]
    tb, nb = min(256, t_), min(1024, t_)
    pm = _mm(xm, w["in_main"], name="mm_in", tn=1536)
    pbg = _mm(xm, w["in_bg"], name="mm_in_bg")
    w = {**w, **rest(pbg)}
    qkv = _prep_fwd(pm, w["conv"])
    bg, = _stage_fwd(_bg_f, "bg_fwd", [(pbg, 128, 0, False)], [w["arow"], w["dtrow"]], [(128, F32)], nb)
    brows, grows = _to_rows(bg, 0), _to_rows(bg, 2 * DN_HEADS)
    ares, tinv = _dn_a_fwd(qkv, grows, brows)
    o_f, o_b, st_f, st_b = _dn_b_fwd(ares, grows)
    odn, = _stage_fwd(_gnorm_f, "gnorm_fwd", [(o_f, 128, 0, True), (o_b, 128, 0, True), (pm, 128, O_Z // 128, True)],
                      [w["gnw"]], [(128, _MXU)], nb, ncol=DN_HEADS)
    osw = _attn_fwd(pm, cos, sin, w["sinks"])
    same, first = (lambda j: j), (lambda j: 0)
    full = lambda dt: (D, D, dt, first, None)
    ya = _mm(odn, w["a"], name="mm_a")
    yb, merged = _mm_fused(
        osw, w["b"], lambda acc, ya_, ga, gb: (acc,) + _merge_f(ya_, acc, ga, gb), [full(F32), full(_MXU)],
        name="mm_b_merge", rows=[(ya, D, first), (pm, D, lambda j: O_GA // D), (pm, D, lambda j: O_GB // D)], tm=512)
    mix, x1, x1m = _mm_fused(
        merged, w["o"], lambda acc, x_, g_, b_: (acc,) + _ln_f2(x_, acc, g_, b_), [full(F32), full(F32), full(_MXU)],
        name="mm_o_ln", rows=[(x, D, first)], params=[w["ln1g"], w["ln1b"]], tm=512)
    gu, hid = _mm_fused(x1m, w["gu"], lambda acc: (acc,) + _swiglu_tile(acc),
                        [(2 * FFN, FFN, F32, same, None), (FFN, FFN // 2, _MXU, same, None)], name="mm_gu_swiglu", tn=FFN)
    ffn, x2, x2m = _mm_fused(
        hid, w["d"], lambda acc, x_, g_, b_: (acc,) + _ln_f2(x_, acc, g_, b_), [full(F32), full(F32), full(_MXU)],
        name="mm_d_ln", rows=[(x1, D, first)], params=[w["ln2g"], w["ln2b"]], tm=512, tk=FFN // 2)
    res = dict(w=w, x=x, xm=xm, pm=pm, pbg=pbg, qkv=qkv, grows=grows, brows=brows, ares=ares, tinv=tinv, o_f=o_f, o_b=o_b, st_f=st_f,
               st_b=st_b, odn=odn, osw=osw, ya=ya, yb=yb, merged=merged, mix=mix, x1=x1, x1m=x1m, gu=gu, hid=hid,
               ffn=ffn)
    return x2, x2m, res


def _layer_bwd(dx2, r, w, cos, sin):
    t_ = dx2.shape[0]
    tb, nb = min(256, t_), min(1024, t_)
    pm = r["pm"]
    g = {}
    dx1a, dffn, g["ln2g"], g["ln2b"] = _stage_bwd(
        _ln_f, "ln2_bwd", [(r["x1"], D, 0, False), (r["ffn"], D, 0, False)], [w["ln2g"], w["ln2b"]], [dx2], tb,
        dtypes=[F32, _MXU])
    same = lambda j: j
    dgu, = _mm_fused(dffn, w["d"], lambda dhid, gu: jax.vjp(_swiglu_tile, gu)[1]((dhid,)),
                     [(2 * FFN, FFN, _MXU, same, None)], name="mm_d_dx_swiglu", rows=[(r["gu"], FFN, same)], tb=True,
                     tn=FFN // 2)
    g["d"] = _mm(r["hid"], dffn, ta=True, name="mm_d_dw", tm=FFN // 2)
    first = lambda j: 0
    full = lambda dt: (D, D, dt, first, None)

    def ln1_back(acc, dx1a_, x_, mix_, g_, b_):
        dx_, dmix_, dg_, db_ = jax.vjp(_ln_f, x_, mix_, g_, b_)[1]((acc + dx1a_,))
        return dx_, dmix_, dg_, db_

    dxa, dmix, g["ln1g"], g["ln1b"] = _mm_fused(
        dgu, w["gu"], ln1_back, [full(F32), full(_MXU)], name="mm_gu_dx_ln", tb=True, tm=512, tk=FFN // 2, n_sums=2,
        rows=[(dx1a, D, first), (r["x"], D, first), (r["mix"], D, first)], params=[w["ln1g"], w["ln1b"]])
    g["gu"] = _mm(r["x1m"], dgu, ta=True, name="mm_gu_dw", tn=FFN // 2)
    g["o"] = _mm(r["merged"], dmix, ta=True, name="mm_o_dw")

    def merge_back(acc, ya_, yb_, ga, gb):
        dya_, dyb_, dga, dgb = jax.vjp(_merge_f, ya_, yb_, ga, gb)[1]((acc,))
        return dya_, dyb_, jnp.concatenate([dga, dgb], axis=-1)

    dya, dyb, dpm = _mm_fused(
        dmix, w["o"], merge_back, [full(_MXU), full(_MXU), (N_MAIN, 2 * D, _MXU, lambda j: O_GA // (2 * D), None)],
        name="mm_o_dx_merge", tb=True, tm=512,
        rows=[(r["ya"], D, first), (r["yb"], D, first), (pm, D, lambda j: O_GA // D), (pm, D, lambda j: O_GB // D)])
    dodn = _mm(dya, w["a"], tb=True, name="mm_a_dx")
    g["a"] = _mm(r["odn"], dya, ta=True, name="mm_a_dw")
    dosw = _mm(dyb, w["b"], tb=True, name="mm_b_dx")
    g["b"] = _mm(r["osw"], dyb, ta=True, name="mm_b_dw")
    ab = _attn_bwd(pm, cos, sin, w["sinks"], dosw, dpm)
    dpm, g["sinks"] = ab[0], ab[7]
    dpm = _band_sum(ab[1:4], ab[4:7], dpm)
    dof, _, dpm, g["gnw"] = _stage_bwd(
        _gnorm_f, "gnorm_bwd", [(r["o_f"], 128, 0, True), (r["o_b"], 128, 0, True), (pm, 128, O_Z // 128, True)],
        [w["gnw"]], [dodn], nb, ncol=DN_HEADS, dtypes=[F32, F32, _MXU], dest={2: (dpm, N_MAIN, O_Z // 128)})
    dares, dg_b = _dn_b_bwd(r["ares"], r["grows"], r["st_f"], r["st_b"], dof)
    dqkv, dgrows, dbrows = _dn_a_bwd(r["qkv"], r["grows"], r["brows"], r["tinv"], dares, dg_b)
    dbg = _from_rows(dbrows, dgrows)
    dpbg, g["arow"], g["dtrow"] = _stage_bwd(_bg_f, "bg_bwd", [(r["pbg"], 128, 0, False)], [w["arow"], w["dtrow"]],
                                             [dbg], nb)
    dpm, g["conv"] = _prep_bwd(pm, w["conv"], dqkv, dpm)
    dx = _mm(dpm, w["in_main"], tb=True, add=dxa, name="mm_in_dx", tk=1920)
    dx = _mm(dpbg, w["in_bg"], tb=True, add=dx, name="mm_in_bg_dx")
    g["in_main"] = _mm(r["xm"], dpm, ta=True, name="mm_in_dw", tn=1536)
    g["in_bg"] = _mm(r["xm"], dpbg, ta=True, name="mm_in_bg_dw")
    return dx, g


def _place():
    return lax.axis_index("x"), lax.axis_index("y"), lax.axis_index("c")


def _colblock(kind, q):
    return q if kind == "col" else (q >> 1) | ((q & 1) << 1)


def _other_chips(x, y):
    return [(1 - x, y), (x, 1 - y), (1 - x, 1 - y)]


HBM = pl.BlockSpec(memory_space=pltpu.HBM)
SEM = pl.BlockSpec(memory_space=pltpu.SEMAPHORE)
DATAFLOW = pltpu.SideEffectType.DATAFLOW_SIDE_EFFECTING


def _hbm(a):
    return pltpu.HBM(a.shape, a.dtype)


def _gather_start(locs, kinds, name):
    n = len(locs)
    locs = list(locs)
    lands = [lax.empty((N_CHIPS,) + a.shape if kind == "row" else (a.shape[0], N_CHIPS * a.shape[1]), a.dtype)
             for a, kind in zip(locs, kinds)]

    def body(*refs):
        loc_refs, land_refs = refs[:n], refs[n:2 * n]
        send_sems, recv_sems, token = refs[2 * n:3 * n], refs[3 * n:4 * n], refs[-1]
        x, y, c = _place()
        myq = 2 * x + y
        for t in range(n):
            width = locs[t].shape[1]
            mine = (land_refs[t].at[myq] if kinds[t] == "row" else
                    land_refs[t].at[:, pl.ds(pl.multiple_of(_colblock(kinds[t], myq) * width, 128), width)])
            for dev in [(cx, cy, c) for cx, cy in _other_chips(x, y)] + [(x, y, 1 - c)]:
                pltpu.make_async_remote_copy(src_ref=loc_refs[t], dst_ref=mine, send_sem=send_sems[t],
                                             recv_sem=recv_sems[t], device_id=dev, device_id_type=MESH).start()
        token[...] = jnp.zeros_like(token)

    res = pl.pallas_call(
        body, name=name,
        out_shape=[pltpu.SemaphoreType.DMA(())] * (2 * n) + [_hbm(a) for a in locs + lands]
        + [jax.ShapeDtypeStruct((8, 128), F32)],
        in_specs=[HBM] * (2 * n), out_specs=[SEM] * (2 * n) + [HBM] * (2 * n) + [pl.BlockSpec(memory_space=pltpu.VMEM)],
        input_output_aliases={t: 2 * n + t for t in range(2 * n)},
        compiler_params=pltpu.CompilerParams(has_side_effects=DATAFLOW),
    )(*[pltpu.with_memory_space_constraint(a, pltpu.HBM) for a in locs + lands])
    return (res[:n], res[n:2 * n], res[2 * n:3 * n], res[3 * n:4 * n]), res[-1]


def _split_wait(handle, after, name):
    send_sems, recv_sems, srcs, lands = handle
    n = len(srcs)

    def body(*refs):
        land_refs, ssems, rsems = refs[n:2 * n], refs[2 * n:3 * n], refs[3 * n:4 * n]
        x, y, c = _place()
        for t in range(n):
            done = pltpu.make_async_remote_copy(
                src_ref=land_refs[t], dst_ref=land_refs[t], send_sem=ssems[t], recv_sem=rsems[t],
                device_id=(x, y, c), device_id_type=MESH)
            done.wait_send()
            done.wait_recv()

    res = pl.pallas_call(
        body, name=name, out_shape=[_hbm(a) for a in list(srcs) + list(lands)],
        in_specs=[HBM] * (2 * n) + [SEM] * (2 * n) + [ANY], out_specs=[HBM] * (2 * n),
        input_output_aliases={t: t for t in range(2 * n)},
        compiler_params=pltpu.CompilerParams(has_side_effects=DATAFLOW),
    )(*srcs, *lands, *send_sems, *recv_sems, after)
    return res[:n], res[n:]


RS_CHUNKS = 2


def _piece(ref, kind, q, hf, pr, pc):
    if kind == "row":
        return ref.at[pl.ds((2 * q + hf) * pr, pr), :]
    return ref.at[pl.ds(hf * pr, pr), pl.ds(pl.multiple_of(_colblock(kind, q) * pc, 128), pc)]


def _rs_to_sibling(ts, meta):
    n = len(ts)

    def body(*refs):
        t_refs, r1_refs, send_sems, recv_sems = refs[:n], refs[n:2 * n], refs[2 * n], refs[2 * n + 1]
        x, y, c = _place()
        sib = (x, y, 1 - c)
        for t, (kind, pr, pc) in enumerate(meta):
            for q in range(N_CHIPS):
                pltpu.make_async_remote_copy(
                    src_ref=_piece(t_refs[t], kind, q, 1 - c, pr, pc), dst_ref=r1_refs[t].at[q],
                    send_sem=send_sems.at[t], recv_sem=recv_sems.at[t], device_id=sib, device_id_type=MESH).start()
        for t in range(n):
            allq = pltpu.make_async_remote_copy(
                src_ref=r1_refs[t], dst_ref=r1_refs[t], send_sem=send_sems.at[t], recv_sem=recv_sems.at[t],
                device_id=sib, device_id_type=MESH)
            allq.wait_send()
            allq.wait_recv()

    return pl.pallas_call(
        body, name="rs_to_sibling", in_specs=[ANY] * n, out_specs=[ANY] * n,
        out_shape=[jax.ShapeDtypeStruct((N_CHIPS, pr, pc), F32) for (_, pr, pc) in meta],
        scratch_shapes=[pltpu.SemaphoreType.DMA((n,)), pltpu.SemaphoreType.DMA((n,))],
    )(*ts)


def _rs_add_sibling(ts, r1s, meta, c):
    n = len(ts)
    in_specs, out_specs, out_shape = [], [], []
    for kind, pr, pc in meta:
        rs = pr // RS_CHUNKS
        if kind == "row":
            in_specs.append(pl.BlockSpec((rs, pc), lambda q, r, c_ref: ((2 * q + c_ref[0]) * RS_CHUNKS + r, 0)))
        else:
            in_specs.append(pl.BlockSpec(
                (rs, pc), lambda q, r, c_ref, kind=kind: (c_ref[0] * RS_CHUNKS + r, _colblock(kind, q))))
    for kind, pr, pc in meta:
        sp = pl.BlockSpec((None, pr // RS_CHUNKS, pc), lambda q, r, c_ref: (q, r, 0))
        in_specs.append(sp)
        out_specs.append(sp)
        out_shape.append(jax.ShapeDtypeStruct((N_CHIPS, pr, pc), BF16))

    def body(c_ref, *refs):
        for t in range(n):
            refs[2 * n + t][...] = (refs[t][...] + refs[n + t][...]).astype(BF16)

    return pl.pallas_call(
        body, name="rs_add_sibling", out_shape=out_shape,
        grid_spec=pltpu.PrefetchScalarGridSpec(num_scalar_prefetch=1, grid=(N_CHIPS, RS_CHUNKS), in_specs=in_specs,
                                               out_specs=out_specs),
        compiler_params=_params(2))(c.reshape(1).astype(jnp.int32), *ts, *r1s)


def _rs_chips_start(ps, meta, name):
    n = len(ps)
    ps = list(ps)
    lands = [lax.empty((N_CHIPS - 1, pr, pc), p.dtype) for p, (_, pr, pc) in zip(ps, meta)]

    def body(*refs):
        p_refs, land_refs = refs[:n], refs[n:2 * n]
        send_sems, recv_sems, token = refs[2 * n:3 * n], refs[3 * n:4 * n], refs[-1]
        x, y, c = _place()
        for t in range(n):
            for j, (cx, cy) in enumerate(_other_chips(x, y)):
                pltpu.make_async_remote_copy(
                    src_ref=p_refs[t].at[2 * cx + cy], dst_ref=land_refs[t].at[j], send_sem=send_sems[t],
                    recv_sem=recv_sems[t], device_id=(cx, cy, c), device_id_type=MESH).start()
        token[...] = jnp.zeros_like(token)

    res = pl.pallas_call(
        body, name=name,
        out_shape=[pltpu.SemaphoreType.DMA(())] * (2 * n) + [_hbm(a) for a in ps + lands]
        + [jax.ShapeDtypeStruct((8, 128), F32)],
        in_specs=[HBM] * (2 * n), out_specs=[SEM] * (2 * n) + [HBM] * (2 * n) + [pl.BlockSpec(memory_space=pltpu.VMEM)],
        input_output_aliases={t: 2 * n + t for t in range(2 * n)},
        compiler_params=pltpu.CompilerParams(has_side_effects=DATAFLOW),
    )(*[pltpu.with_memory_space_constraint(a, pltpu.HBM) for a in ps + lands])
    return (res[:n], res[n:2 * n], res[2 * n:3 * n], res[3 * n:4 * n]), res[-1]


def _rs_add_chips(ps, r2s, meta, myq, c):
    n = len(ps)
    in_specs, out_specs, out_shape = [], [], []
    for _, pr, pc in meta:
        in_specs.append(pl.BlockSpec((None, pr // RS_CHUNKS, pc), lambda r, q_ref, c_ref: (q_ref[0], r, 0)))
    for _, pr, pc in meta:
        in_specs.append(pl.BlockSpec((N_CHIPS - 1, pr // RS_CHUNKS, pc), lambda r, q_ref, c_ref: (0, r, 0)))
        out_specs.append(pl.BlockSpec((None, pr // RS_CHUNKS, pc), lambda r, q_ref, c_ref: (c_ref[0], r, 0)))
        out_shape.append(jax.ShapeDtypeStruct((2, pr, pc), F32))

    def body(q_ref, c_ref, *refs):
        for t in range(n):
            r2 = refs[n + t]
            own = refs[t][...].astype(F32)
            refs[2 * n + t][...] = ((own + r2[0].astype(F32)) + r2[1].astype(F32)) + r2[2].astype(F32)

    return pl.pallas_call(
        body, name="rs_add_chips", out_shape=out_shape,
        grid_spec=pltpu.PrefetchScalarGridSpec(num_scalar_prefetch=2, grid=(RS_CHUNKS,), in_specs=in_specs,
                                               out_specs=out_specs),
        compiler_params=_params(1))(myq.reshape(1).astype(jnp.int32), c.reshape(1).astype(jnp.int32), *ps, *r2s)


def _rs_share_halves(gs):
    n = len(gs)

    def body(*refs):
        g_refs, send_sems, recv_sems = refs[n:2 * n], refs[2 * n], refs[2 * n + 1]
        x, y, c = _place()
        sib = (x, y, 1 - c)
        for t in range(n):
            pltpu.make_async_remote_copy(
                src_ref=g_refs[t].at[c], dst_ref=g_refs[t].at[c], send_sem=send_sems.at[t], recv_sem=recv_sems.at[t],
                device_id=sib, device_id_type=MESH).start()
        for t in range(n):
            cp = pltpu.make_async_remote_copy(
                src_ref=g_refs[t].at[c], dst_ref=g_refs[t].at[1 - c], send_sem=send_sems.at[t],
                recv_sem=recv_sems.at[t], device_id=sib, device_id_type=MESH)
            cp.wait_send()
            cp.wait_recv()

    return pl.pallas_call(
        body, name="rs_share_halves", in_specs=[ANY] * n, out_specs=[ANY] * n,
        out_shape=[jax.ShapeDtypeStruct(g.shape, g.dtype) for g in gs], input_output_aliases={t: t for t in range(n)},
        scratch_shapes=[pltpu.SemaphoreType.DMA((n,)), pltpu.SemaphoreType.DMA((n,))],
    )(*gs)


def _rs_begin(ts, meta, c, name):
    r1s = _rs_to_sibling(ts, meta)
    ps = _rs_add_sibling(ts, r1s, meta, c)
    return _rs_chips_start(ps, meta, name + "_start")


def _rs_end(handle, after, meta, c, myq, name):
    ps, r2s = _split_wait(handle, after, name + "_wait")
    return _rs_share_halves(_rs_add_chips(ps, r2s, meta, myq, c))


def _allreduce_small(buf):
    rows = buf.shape[0]
    ndev = 8

    def body(b_ref, o_ref, slots, send_sems, recv_sems):
        x, y, c = _place()
        me = 4 * x + 2 * y + c
        slots[me] = b_ref[...]
        for k in range(1, ndev):
            kx, ky, kc = (k >> 2) & 1, (k >> 1) & 1, k & 1
            peer = (x ^ kx, y ^ ky, c ^ kc)
            pltpu.make_async_remote_copy(
                src_ref=b_ref, dst_ref=slots.at[me], send_sem=send_sems.at[k - 1], recv_sem=recv_sems.at[k - 1],
                device_id=peer, device_id_type=MESH).start()
        for k in range(1, ndev):
            kx, ky, kc = (k >> 2) & 1, (k >> 1) & 1, k & 1
            cp = pltpu.make_async_remote_copy(
                src_ref=b_ref, dst_ref=slots.at[me ^ k], send_sem=send_sems.at[k - 1], recv_sem=recv_sems.at[k - 1],
                device_id=(x ^ kx, y ^ ky, c ^ kc), device_id_type=MESH)
            cp.wait_send()
            cp.wait_recv()
        acc = slots[0]
        for s in range(1, ndev):
            acc = acc + slots[s]
        o_ref[...] = acc

    vm = pl.BlockSpec(memory_space=pltpu.VMEM)
    return pl.pallas_call(
        body, name="allreduce_small", in_specs=[vm], out_specs=vm, out_shape=jax.ShapeDtypeStruct((rows, 128), F32),
        scratch_shapes=[pltpu.VMEM((ndev, rows, 128), F32), pltpu.SemaphoreType.DMA((ndev - 1,)),
                        pltpu.SemaphoreType.DMA((ndev - 1,))],
        compiler_params=pltpu.CompilerParams(vmem_limit_bytes=VMEM_LIMIT))(buf)


RS_META = [("col", D // 2, IN_PAD), ("row", D // 8, D), ("row", D // 8, D), ("row", D // 8, D),
           ("colx", D // 2, 2 * FFN // N_CHIPS), ("row", FFN // 8, D)]
SMALL_ROWS = 156


def _rope_tables(t_):
    half = SW_DIM // 2
    inv_freq = ROPE_THETA ** (-jnp.arange(half, dtype=F32) / half)
    ang = jnp.arange(t_, dtype=F32)[:, None] * inv_freq[None, :]
    reps = 128 // half
    return jnp.concatenate([jnp.cos(ang)] * reps, axis=1), jnp.concatenate([jnp.sin(ang)] * reps, axis=1)


def _orig_cols(padded, a, b):
    out = []
    for q in range(N_CHIPS):
        lo, hi = max(a, q * IN_SHARD), min(b, (q + 1) * IN_SHARD)
        if lo < hi:
            out.append(padded[:, q * IN_PAD + lo - q * IN_SHARD:q * IN_PAD + hi - q * IN_SHARD])
    return out


_ORIG_SEGMENTS = [(0, R_BG, "main", 0), (R_BG, R_SW, "bg", 0), (R_SW, R_GATES, "main", O_QS), (R_GATES, IN_COLS, "main", O_GA)]


def _to_padded_shards(main, bg):
    zeros = jnp.zeros((main.shape[0], IN_PAD - IN_SHARD), main.dtype)
    parts = []
    for q in range(N_CHIPS):
        for a, b, src, s0 in _ORIG_SEGMENTS:
            lo, hi = max(a, q * IN_SHARD), min(b, (q + 1) * IN_SHARD)
            if lo < hi:
                parts.append((main if src == "main" else bg)[:, s0 + lo - a:s0 + hi - a])
        parts.append(zeros)
    return jnp.concatenate(parts, axis=1)


def _lane_row(v16):
    return jnp.pad(v16.reshape(1, 2 * DN_HEADS), ((0, 0), (2 * DN_HEADS, 128 - 4 * DN_HEADS)))


def _pack_small(g):
    pad16 = jnp.pad(g["sinks"], ((0, 0), (0, 128 - SW_HEADS)))
    return jnp.concatenate([g["conv"].reshape(-1, 128), g["ln1g"].reshape(-1, 128), g["ln1b"].reshape(-1, 128),
                            g["ln2g"].reshape(-1, 128), g["ln2b"].reshape(-1, 128), g["gnw"], g["arow"], g["dtrow"],
                            pad16], axis=0)


def _unpack_small(buf):
    nconv = DN_CONV * 3 * D // 128
    o = nconv
    out = dict(conv=buf[:o].reshape(DN_CONV, 3 * D))
    for name in ("ln1g", "ln1b", "ln2g", "ln2b"):
        out[name] = buf[o:o + 8].reshape(D)
        o += 8
    out["gnw"] = buf[o]
    out["a_log"] = buf[o + 1, 2 * DN_HEADS:4 * DN_HEADS].reshape(2, DN_HEADS)
    out["dt_bias"] = buf[o + 2, 2 * DN_HEADS:4 * DN_HEADS].reshape(2, DN_HEADS)
    out["sinks"] = buf[o + 3, :SW_HEADS]
    return out


def kernel(x, w_in, conv_w, a_log, dt_bias, dn_norm_w, sinks, w_branch_a, w_branch_b, w_out, ln1_g, ln1_b, w_gate_up, w_down, ln2_g, ln2_b, loss_target, m_w_in, m_conv_w, m_a_log, m_dt_bias, m_dn_norm_w, m_sinks, m_w_branch_a, m_w_branch_b, m_w_out, m_ln1_g, m_ln1_b, m_w_gate_up, m_w_down, m_ln2_g, m_ln2_b, v_w_in, v_conv_w, v_a_log, v_dt_bias, v_dn_norm_w, v_sinks, v_w_branch_a, v_w_branch_b, v_w_out, v_ln1_g, v_ln1_b, v_w_gate_up, v_w_down, v_ln2_g, v_ln2_b):
    xi, yi, ci = _place()
    myq = 2 * xi + yi
    t_ = x.shape[1]
    cos, sin = _rope_tables(t_)

    loc_in = jnp.pad(w_in.astype(BF16), ((0, 0), (0, 0), (0, IN_PAD - IN_SHARD)))
    locs = [loc_in, w_branch_a.astype(BF16), w_branch_b.astype(BF16), w_out.astype(BF16), w_gate_up.astype(BF16),
            w_down.astype(BF16), conv_w]
    kinds = ["col", "row", "row", "row", "colx", "row", "col"]
    gathers = []
    for l in range(DEPTH):
        srcs = [a[l] for a in locs]
        if gathers:
            srcs[-1] = srcs[-1] + gathers[-1][1][0, 0]
        gathers.append(_gather_start(srcs, kinds, "gather_%d_start" % l))

    def in_weights(l, after):
        _, (full_in,) = _split_wait(tuple(part[:1] for part in gathers[l][0]), after, "gather_%d_wait_in" % l)
        cols = lambda a, b: _orig_cols(full_in, a, b)
        return dict(
            in_main=jnp.concatenate(cols(0, R_BG) + cols(R_GATES, IN_COLS) + cols(R_SW, R_GATES), axis=1),
            in_bg=jnp.pad(jnp.concatenate(cols(R_BG, R_SW), axis=1), ((0, 0), (0, 128 - 4 * DN_HEADS))))

    def rest_weights(l, after):
        _, (full_a, full_b, full_o, full_gu, full_d, full_conv) = _split_wait(
            tuple(part[1:] for part in gathers[l][0]), after, "gather_%d_wait_rest" % l)
        return dict(
            conv=full_conv, arow=_lane_row(a_log[l]), dtrow=_lane_row(dt_bias[l]), gnw=dn_norm_w[l][None],
            sinks=sinks[l][None], a=full_a.reshape(D, D), b=full_b.reshape(D, D), o=full_o.reshape(D, D),
            ln1g=ln1_g[l][None], ln1b=ln1_b[l][None], gu=full_gu, d=full_d.reshape(FFN, D),
            ln2g=ln2_g[l][None], ln2b=ln2_b[l][None])

    h = x[0]
    hm = h.astype(_MXU)
    residuals = []
    for l in range(DEPTH):
        win = in_weights(l, gathers[-1][1] if l == 0 else h)
        h, hm, res = _layer_fwd(h, hm, win, functools.partial(rest_weights, l), cos, sin)
        residuals.append(res)
    dh, sq = _loss_head(h, loss_target[0])
    loss = lax.psum((0.5 / D) * jnp.sum(sq), ("x", "y", "c"))

    big = [None] * DEPTH
    small = [None] * DEPTH
    pending = None
    for l in reversed(range(DEPTH)):
        dh, g = _layer_bwd(dh, residuals[l], residuals[l]["w"], cos, sin)
        if pending is not None:
            big[pending[0]] = _rs_end(pending[1], dh, RS_META, ci, myq, "rs_chips_%d" % pending[0])
        g_in = _to_padded_shards(g["in_main"], g["in_bg"])
        handle, token = _rs_begin([g_in, g["a"], g["b"], g["o"], g["gu"], g["d"]], RS_META, ci, "rs_chips_%d" % l)
        pending = (l, handle)
        dh = dh + token[0, 0]
        small[l] = _pack_small(g)
    tot = _allreduce_small(jnp.concatenate(small, axis=0))
    big[pending[0]] = _rs_end(pending[1], tot, RS_META, ci, myq, "rs_chips_%d" % pending[0])
    sm = [_unpack_small(tot[l * SMALL_ROWS:(l + 1) * SMALL_ROWS]) for l in range(DEPTH)]
    stack = lambda name: jnp.stack([s[name] for s in sm], axis=0)

    grads = dict(
        w_in=jnp.stack([big[l][0].reshape(D, IN_PAD)[:, :IN_SHARD] for l in range(DEPTH)]),
        conv_w=lax.dynamic_slice_in_dim(stack("conv"), myq * (3 * D // N_CHIPS), 3 * D // N_CHIPS, axis=2),
        a_log=stack("a_log"), dt_bias=stack("dt_bias"), dn_norm_w=stack("gnw"), sinks=stack("sinks"),
        w_branch_a=jnp.stack([big[l][1].reshape(D // N_CHIPS, D) for l in range(DEPTH)]),
        w_branch_b=jnp.stack([big[l][2].reshape(D // N_CHIPS, D) for l in range(DEPTH)]),
        w_out=jnp.stack([big[l][3].reshape(D // N_CHIPS, D) for l in range(DEPTH)]),
        ln1_g=stack("ln1g"), ln1_b=stack("ln1b"),
        w_gate_up=jnp.stack([big[l][4].reshape(D, 2 * FFN // N_CHIPS) for l in range(DEPTH)]),
        w_down=jnp.stack([big[l][5].reshape(FFN // N_CHIPS, D) for l in range(DEPTH)]),
        ln2_g=stack("ln2g"), ln2_b=stack("ln2b"))
    weights = dict(w_in=w_in, conv_w=conv_w, a_log=a_log, dt_bias=dt_bias, dn_norm_w=dn_norm_w, sinks=sinks,
                   w_branch_a=w_branch_a, w_branch_b=w_branch_b, w_out=w_out, ln1_g=ln1_g, ln1_b=ln1_b,
                   w_gate_up=w_gate_up, w_down=w_down, ln2_g=ln2_g, ln2_b=ln2_b)
    ms = dict(w_in=m_w_in, conv_w=m_conv_w, a_log=m_a_log, dt_bias=m_dt_bias, dn_norm_w=m_dn_norm_w, sinks=m_sinks,
              w_branch_a=m_w_branch_a, w_branch_b=m_w_branch_b, w_out=m_w_out, ln1_g=m_ln1_g, ln1_b=m_ln1_b,
              w_gate_up=m_w_gate_up, w_down=m_w_down, ln2_g=m_ln2_g, ln2_b=m_ln2_b)
    vs = dict(w_in=v_w_in, conv_w=v_conv_w, a_log=v_a_log, dt_bias=v_dt_bias, dn_norm_w=v_dn_norm_w, sinks=v_sinks,
              w_branch_a=v_w_branch_a, w_branch_b=v_w_branch_b, w_out=v_w_out, ln1_g=v_ln1_g, ln1_b=v_ln1_b,
              w_gate_up=v_w_gate_up, w_down=v_w_down, ln2_g=v_ln2_g, ln2_b=v_ln2_b)
    names = list(weights)
    upd = {n: _adamw("adamw_" + n, weights[n], grads[n], ms[n], vs[n]) for n in names}
    return (loss, dh[None], *[grads[n] for n in names], *[upd[n][0] for n in names], *[upd[n][1] for n in names],
            *[upd[n][2] for n in names])
```

```python
import functools

import jax
import jax.numpy as jnp
from jax import lax
from jax.experimental import pallas as pl
from jax.experimental.pallas import tpu as pltpu

F32 = jnp.float32
BF16 = jnp.bfloat16
_MXU = BF16

D = 1024
DEPTH = 4
DN_HEADS = 8
DN_DIM = 128
DN_CONV = 5
CHUNK = 64
SW_HEADS = 16
SW_KV = 4
SW_DIM = 64
SW_GRP = SW_HEADS // SW_KV
SW_BLOCK = 128
ROPE_THETA = 10000.0
FFN = 2816
ALPHA = (2.0 * DEPTH) ** 0.25
LN_EPS = 1e-5
RMS_EPS = 1e-6
IN_COLS = 7712
O_Z, O_GA, O_GB, O_QS, O_KS, O_VS, N_MAIN = 3072, 4096, 5120, 6144, 7168, 7424, 7680
R_BG, R_SW, R_GATES = 4096, 4128, 5664
N_CHIPS = 4
IN_SHARD = IN_COLS // N_CHIPS
IN_PAD = 2048
ADAM_LR, ADAM_B1, ADAM_B2, ADAM_EPS, ADAM_WD, ADAM_STEP = 0.001, 0.9, 0.999, 1e-08, 0.01, 10
VMEM_LIMIT = 52 * 1024 * 1024
MESH = pl.DeviceIdType.MESH
ANY = pl.BlockSpec(memory_space=pl.ANY)


def _params(n_grid, **kw):
    return pltpu.CompilerParams(dimension_semantics=("arbitrary",) * n_grid, vmem_limit_bytes=VMEM_LIMIT, **kw)


def _full(a):
    nd = a.ndim
    return pl.BlockSpec(a.shape, lambda *_, nd=nd: (0,) * nd)


def _raw_dot(a, b, ca, cb):
    return lax.dot_general(a.astype(_MXU), b.astype(_MXU), (((ca,), (cb,)), ((), ())), preferred_element_type=F32)


@jax.custom_vjp
def _nn(a, b):
    return _raw_dot(a, b, 1, 0)


@jax.custom_vjp
def _nt(a, b):
    return _raw_dot(a, b, 1, 1)


@jax.custom_vjp
def _tn(a, b):
    if a.shape[1] > b.shape[1]:
        return _raw_dot(b, a, 0, 0).T
    return _raw_dot(a, b, 0, 0)


_nn.defvjp(lambda a, b: (_nn(a, b), (a, b)), lambda r, g: (_nt(g, r[1]), _tn(r[0], g)))
_nt.defvjp(lambda a, b: (_nt(a, b), (a, b)), lambda r, g: (_nn(g, r[1]), _tn(g, r[0])))
_tn.defvjp(lambda a, b: (_tn(a, b), (a, b)), lambda r, g: (_nt(r[1], g), _nn(r[0], g)))


def _hdot(a, b, ca=1, cb=0):
    ah, bh = a.astype(BF16), b.astype(BF16)
    al, bl = (a - ah.astype(F32)).astype(BF16), (b - bh.astype(F32)).astype(BF16)
    dot = lambda u, v: lax.dot_general(u, v, (((ca,), (cb,)), ((), ())), preferred_element_type=F32)
    return dot(ah, bh) + (dot(ah, bl) + dot(al, bh))


def _inv_impl(mats):
    n = mats[0].shape[0]
    eye = (lax.broadcasted_iota(jnp.int32, (n, n), 0) == lax.broadcasted_iota(jnp.int32, (n, n), 1)).astype(F32)
    ps = [-a for a in mats]
    ts = [eye + p for p in ps]
    for _ in range(max(1, (n - 1).bit_length()) - 1):
        ps = [_hdot(p, p) for p in ps]
        ts = [t + _hdot(t, p) for t, p in zip(ts, ps)]
    return tuple(ts)


@jax.custom_vjp
def _inv(mats):
    return _inv_impl(mats)


def _inv_fwd(mats):
    ts = _inv_impl(mats)
    return ts, ts


def _inv_bwd(ts, gs):
    xs = [_hdot(t, g, 0, 0) for t, g in zip(ts, gs)]
    return (tuple(-_hdot(x, t, 1, 1) for x, t in zip(xs, ts)),)


_inv.defvjp(_inv_fwd, _inv_bwd)


@jax.custom_vjp
def _inv_saved(mats, saved):
    return saved


_inv_saved.defvjp(lambda mats, saved: (saved, saved),
                  lambda ts, gs: (_inv_bwd(ts, gs)[0], tuple(jnp.zeros_like(t) for t in ts)))


def _tile(n, cap):
    if n <= cap:
        return n
    best = [t for t in range(128, cap + 1, 128) if n % t == 0]
    assert best, (n, cap)
    return best[-1]


def _mm(a, b, *, name, ta=False, tb=False, add=None, tm=1024, tn=1024, tk=1024):
    if ta:
        k_, m_ = a.shape
    else:
        m_, k_ = a.shape
    n_ = b.shape[0] if tb else b.shape[1]
    tm, tn, tk = _tile(m_, tm), _tile(n_, tn), _tile(k_, tk)
    nk = k_ // tk
    has_add = add is not None

    def body(*refs):
        a_ref, b_ref = refs[:2]
        add_ref = refs[2] if has_add else None
        o_ref = refs[3] if has_add else refs[2]
        part = _raw_dot(a_ref[...], b_ref[...], 0 if ta else 1, 1 if tb else 0)
        if nk == 1:
            o_ref[...] = part + add_ref[...] if has_add else part
            return
        acc = refs[-1]
        k = pl.program_id(2)

        @pl.when(k == 0)
        def _():
            acc[...] = part

        @pl.when(jnp.logical_and(k > 0, k < nk - 1))
        def _():
            acc[...] += part

        @pl.when(k == nk - 1)
        def _():
            o_ref[...] = acc[...] + part + add_ref[...] if has_add else acc[...] + part

    a_spec = pl.BlockSpec((tk, tm), lambda i, j, k: (k, i)) if ta else pl.BlockSpec((tm, tk), lambda i, j, k: (i, k))
    b_spec = pl.BlockSpec((tn, tk), lambda i, j, k: (j, k)) if tb else pl.BlockSpec((tk, tn), lambda i, j, k: (k, j))
    o_spec = pl.BlockSpec((tm, tn), lambda i, j, k: (i, j))
    in_specs = [a_spec, b_spec] + ([o_spec] if has_add else [])
    args = (a, b) + ((add,) if has_add else ())
    return pl.pallas_call(
        body, name=name, grid=(m_ // tm, n_ // tn, nk), in_specs=in_specs, out_specs=o_spec,
        out_shape=jax.ShapeDtypeStruct((m_, n_), F32),
        scratch_shapes=[pltpu.VMEM((tm, tn), F32)] if nk > 1 else [],
        compiler_params=_params(3))(*args)


def _mm_fused(a, b, post, outs, *, name, rows=(), params=(), n_sums=0, tb=False, tm=1024, tn=1024, tk=1024):
    m_, k_ = a.shape
    n_ = b.shape[0] if tb else b.shape[1]
    tm, tn, tk = _tile(m_, tm), _tile(n_, tn), _tile(k_, tk)
    nk = k_ // tk
    nr, npar, no = len(rows), len(params), len(outs)
    aliased = [o[4] for o in outs if o[4] is not None]
    nin = 2 + nr + npar

    def body(*refs):
        a_ref, b_ref = refs[:2]
        row_refs, par_refs = refs[2:2 + nr], refs[2 + nr:nin]
        out_refs = refs[nin + len(aliased):nin + len(aliased) + no + n_sums]
        part = _raw_dot(a_ref[...], b_ref[...], 1, 1 if tb else 0)
        k = pl.program_id(2)
        first = jnp.logical_and(pl.program_id(0) == 0, pl.program_id(1) == 0)
        if nk > 1:
            acc = refs[-1]

            @pl.when(k == 0)
            def _():
                acc[...] = part

            @pl.when(jnp.logical_and(k > 0, k < nk - 1))
            def _():
                acc[...] += part

        @pl.when(k == nk - 1)
        def _():
            total = acc[...] + part if nk > 1 else part
            res = post(total, *[r[...].astype(F32) for r in row_refs], *[p[...] for p in par_refs])
            for o_ref, val in zip(out_refs[:no], res[:no]):
                o_ref[...] = val.astype(o_ref.dtype)
            for s_ref, val in zip(out_refs[no:], res[no:]):
                @pl.when(first)
                def _(s_ref=s_ref):
                    s_ref[...] = jnp.zeros_like(s_ref)
                s_ref[...] += val

    b_spec = pl.BlockSpec((tn, tk), lambda i, j, k: (j, k)) if tb else pl.BlockSpec((tk, tn), lambda i, j, k: (k, j))
    in_specs = [pl.BlockSpec((tm, tk), lambda i, j, k: (i, k)), b_spec]
    in_specs += [pl.BlockSpec((tm, w), lambda i, j, k, cb=cb: (i, cb(j))) for _, w, cb in rows]
    in_specs += [_full(p) for p in params] + [ANY] * len(aliased)
    out_specs = [pl.BlockSpec((tm, w), lambda i, j, k, cb=cb: (i, cb(j))) for _, w, _, cb, _ in outs]
    out_specs += [_full(p) for p in params[:n_sums]]
    out_shape = [jax.ShapeDtypeStruct((m_, tot), dt) for tot, _, dt, _, _ in outs]
    out_shape += [jax.ShapeDtypeStruct(p.shape, F32) for p in params[:n_sums]]
    aliases, pos = {}, nin
    for oi, o in enumerate(outs):
        if o[4] is not None:
            aliases[pos] = oi
            pos += 1
    return pl.pallas_call(
        body, name=name, grid=(m_ // tm, n_ // tn, nk), in_specs=in_specs, out_specs=out_specs, out_shape=out_shape,
        scratch_shapes=[pltpu.VMEM((tm, tn), F32)] if nk > 1 else [], input_output_aliases=aliases,
        compiler_params=_params(3))(a, b, *[r[0] for r in rows], *params, *aliased)


def _row_spec(tb, w, c0, percol):
    return pl.BlockSpec((tb, w), lambda i, j, c0=c0, pc=percol: (i, c0 + (j if pc else 0)))


def _stage_fwd(f, name, rows, params, outs, tb, ncol=1):
    t_ = rows[0][0].shape[0]
    nr, npar = len(rows), len(params)

    def body(*refs):
        res = f(*[r[...].astype(F32) for r in refs[:nr + npar]])
        for o_ref, val in zip(refs[nr + npar:], res):
            o_ref[...] = val.astype(o_ref.dtype)

    return pl.pallas_call(
        body, name=name, grid=(t_ // tb, ncol),
        in_specs=[_row_spec(tb, w, c0, pc) for (_, w, c0, pc) in rows] + [_full(p) for p in params],
        out_specs=[pl.BlockSpec((tb, w), lambda i, j: (i, j)) for w, _ in outs],
        out_shape=[jax.ShapeDtypeStruct((t_, w * ncol), dt) for w, dt in outs],
        compiler_params=_params(2))(*[r[0] for r in rows], *params)


def _into(dest, tb, width, t_, ncol, dtype):
    if dest is None:
        return pl.BlockSpec((tb, width), lambda i, j: (i, j)), jax.ShapeDtypeStruct((t_, width * ncol), dtype), None
    buf, total, c0 = dest
    return (pl.BlockSpec((tb, width), lambda i, j, c0=c0: (i, c0 + j)), jax.ShapeDtypeStruct((t_, total), dtype), buf)


def _stage_bwd(f, name, rows, params, douts, tb, ncol=1, cat=None, dtypes=None, dest=None):
    t_ = rows[0][0].shape[0]
    nr, npar, nd = len(rows), len(params), len(douts)
    cat = cat if cat is not None else [[r] for r in range(nr)]
    dtypes = dtypes if dtypes is not None else [F32] * len(cat)
    dest = dest or {}
    assert ncol == 1 or all(len(g) == 1 and rows[g[0]][3] for g in cat)
    nin = nr + npar + nd

    def body(*refs):
        ins = [r[...].astype(F32) for r in refs[:nr + npar]]
        dvals = tuple(r[...].astype(F32) for r in refs[nr + npar:nin])
        out_refs = refs[nin + len(aliased):]
        _, vjp = jax.vjp(f, *ins)
        grads = vjp(dvals)
        for o_ref, grp in zip(out_refs[:len(cat)], cat):
            val = grads[grp[0]] if len(grp) == 1 else jnp.concatenate([grads[r] for r in grp], axis=-1)
            o_ref[...] = val.astype(o_ref.dtype)
        first = jnp.logical_and(pl.program_id(0) == 0, pl.program_id(1) == 0)
        for p_ref, gp in zip(out_refs[len(cat):], grads[nr:]):
            @pl.when(first)
            def _(p_ref=p_ref):
                p_ref[...] = jnp.zeros_like(p_ref)
            p_ref[...] += gp

    gw = [sum(rows[r][1] for r in grp) for grp in cat]
    out_specs, out_shape, aliased, aliases = [], [], [], {}
    for gi, (w, dt) in enumerate(zip(gw, dtypes)):
        spec, shape, buf = _into(dest.get(gi), tb, w, t_, ncol, dt)
        out_specs.append(spec)
        out_shape.append(shape)
        if buf is not None:
            aliases[nin + len(aliased)] = gi
            aliased.append(buf)
    return pl.pallas_call(
        body, name=name, grid=(t_ // tb, ncol),
        in_specs=[_row_spec(tb, w, c0, pc) for (_, w, c0, pc) in rows] + [_full(p) for p in params]
        + [pl.BlockSpec((tb, d.shape[1] // ncol), lambda i, j: (i, j)) for d in douts] + [ANY] * len(aliased),
        out_specs=out_specs + [_full(p) for p in params],
        out_shape=out_shape + [jax.ShapeDtypeStruct(p.shape, F32) for p in params],
        input_output_aliases=aliases,
        compiler_params=_params(2))(*[r[0] for r in rows], *params, *douts, *aliased)


def _ln_f(x, y, g, b):
    u = ALPHA * x + y
    c = u - jnp.mean(u, axis=-1, keepdims=True)
    var = jnp.mean(c * c, axis=-1, keepdims=True)
    return (c * lax.rsqrt(var + LN_EPS) * g + b,)


def _ln_f2(x, y, g, b):
    out, = _ln_f(x, y, g, b)
    return out, out


def _swiglu_tile(gu):
    half = gu.shape[1] // 2
    return (jax.nn.silu(gu[:, :half]) * gu[:, half:],)


def _merge_f(ya, yb, ga, gb):
    return (jax.nn.sigmoid(ga) * ya + jax.nn.sigmoid(gb) * yb,)


def _gnorm_f(of, ob, z, w):
    o = of + ob
    return (o * lax.rsqrt(jnp.mean(o * o, axis=-1, keepdims=True) + RMS_EPS) * w * jax.nn.silu(z),)


def _bg_f(x, arow, dtrow):
    lane = lax.broadcasted_iota(jnp.int32, x.shape, 1)
    beta = jax.nn.sigmoid(x)
    g = -jnp.exp(arow) * jax.nn.softplus(x + dtrow)
    return (jnp.where(lane < 16, beta, jnp.where(lane < 32, g, 0.0)),)


PREP_ROWS = 512
PAD = 8


def _prep_f(part, w, *wins):
    xc = wins[0] * w[0:1, :]
    for k in range(1, DN_CONV):
        xc = xc + wins[k] * w[k:k + 1, :]
    a = jax.nn.silu(xc)
    nrm = a * lax.rsqrt(jnp.sum(a * a, axis=-1, keepdims=True) + RMS_EPS)
    return jnp.where(part == 0, nrm * (DN_DIM ** -0.5), jnp.where(part == 1, nrm, a))


def _windows(pad_ref, r0, rows):
    return [pad_ref[PAD + r0 - 2 + k:PAD + r0 - 2 + k + rows, :] for k in range(DN_CONV)]


def _prep_fwd(pm, conv):
    t_ = pm.shape[0]
    rows = min(PREP_ROWS, t_)

    def body(x_ref, w_ref, o_ref, pad_ref):
        part = pl.program_id(0) // DN_HEADS
        pad_ref[0:PAD, :] = jnp.zeros((PAD, DN_DIM), F32)
        pad_ref[PAD + t_:2 * PAD + t_, :] = jnp.zeros((PAD, DN_DIM), F32)
        pad_ref[PAD:PAD + t_, :] = x_ref[...]
        w = w_ref[...]
        for r in range(t_ // rows):
            o_ref[r * rows:(r + 1) * rows, :] = _prep_f(part, w, *_windows(pad_ref, r * rows, rows))

    ncb = 3 * DN_HEADS
    return pl.pallas_call(
        body, name="prep_fwd", grid=(ncb,),
        in_specs=[pl.BlockSpec((t_, DN_DIM), lambda j: (0, j)), pl.BlockSpec((DN_CONV, DN_DIM), lambda j: (0, j))],
        out_specs=pl.BlockSpec((t_, DN_DIM), lambda j: (0, j)),
        out_shape=jax.ShapeDtypeStruct((t_, ncb * DN_DIM), F32),
        scratch_shapes=[pltpu.VMEM((t_ + 2 * PAD, DN_DIM), F32)],
        compiler_params=_params(1))(pm, conv)


def _prep_bwd(pm, conv, dout, dpm):
    t_ = pm.shape[0]
    rows = min(PREP_ROWS, t_)

    def body(x_ref, w_ref, d_ref, _, dx_ref, dw_ref, pad_ref, dpad_ref):
        part = pl.program_id(0) // DN_HEADS
        pad_ref[0:PAD, :] = jnp.zeros((PAD, DN_DIM), F32)
        pad_ref[PAD + t_:2 * PAD + t_, :] = jnp.zeros((PAD, DN_DIM), F32)
        pad_ref[PAD:PAD + t_, :] = x_ref[...]
        dpad_ref[...] = jnp.zeros_like(dpad_ref)
        w = w_ref[...]
        dw = jnp.zeros((DN_CONV, DN_DIM), F32)
        for r in range(t_ // rows):
            r0 = r * rows
            _, vjp = jax.vjp(functools.partial(_prep_f, part), w, *_windows(pad_ref, r0, rows))
            grads = vjp(d_ref[r0:r0 + rows, :])
            dw = dw + grads[0]
            for k in range(DN_CONV):
                lo = PAD + r0 - 2 + k
                dpad_ref[lo:lo + rows, :] += grads[1 + k]
        dx_ref[...] = dpad_ref[PAD:PAD + t_, :].astype(dx_ref.dtype)
        dw_ref[...] = dw

    ncb = 3 * DN_HEADS
    col = pl.BlockSpec((t_, DN_DIM), lambda j: (0, j))
    wsp = pl.BlockSpec((DN_CONV, DN_DIM), lambda j: (0, j))
    return pl.pallas_call(
        body, name="prep_bwd", grid=(ncb,), in_specs=[col, wsp, col, ANY], out_specs=[col, wsp],
        out_shape=[jax.ShapeDtypeStruct(dpm.shape, dpm.dtype), jax.ShapeDtypeStruct((DN_CONV, ncb * DN_DIM), F32)],
        scratch_shapes=[pltpu.VMEM((t_ + 2 * PAD, DN_DIM), F32), pltpu.VMEM((t_ + 2 * PAD, DN_DIM), F32)],
        input_output_aliases={3: 0}, compiler_params=_params(1))(pm, conv, dout, dpm)


def _dn_chunk(sgns, qs, ks, vs, grows, brows, tsaved=None, with_t=False):
    c = qs[0].shape[0]
    i = lax.broadcasted_iota(jnp.int32, (c, c), 0)
    j = lax.broadcasted_iota(jnp.int32, (c, c), 1)
    eye = i == j
    incl = {s: (i - j) * int(s) >= 0 for s in set(sgns)}
    strict = {s: (i - j) * int(s) > 0 for s in set(sgns)}
    gcs = [jnp.sum(jnp.where(incl[s], g, 0.0), axis=1, keepdims=True) for s, g in zip(sgns, grows)]
    grs = [jnp.sum(jnp.where(eye, gc, 0.0), axis=0, keepdims=True) for gc in gcs]
    bcs = [jnp.sum(jnp.where(eye, b, 0.0), axis=1, keepdims=True) for b in brows]
    gls = [jnp.sum(g, axis=1, keepdims=True) for g in grows]
    decs = [jnp.exp(jnp.where(incl[s], gc - gr, -1e30)) for s, gc, gr in zip(sgns, gcs, grs)]
    kks = [_nt(k, k) for k in ks]
    amats = tuple(jnp.where(strict[s], bc * kk * dec, 0.0) for s, bc, kk, dec in zip(sgns, bcs, kks, decs))
    tinvs = _inv(amats) if tsaved is None else _inv_saved(amats, tsaved)
    egcs = [jnp.exp(gc) for gc in gcs]
    us = [_nn(t, v * bc) for t, v, bc in zip(tinvs, vs, bcs)]
    ws = [_nn(t, k * (bc * egc)) for t, k, bc, egc in zip(tinvs, ks, bcs, egcs)]
    qks = [_nt(q, k) * dec for q, k, dec in zip(qs, ks, decs)]
    qds = [q * egc for q, egc in zip(qs, egcs)]
    kds = [k * jnp.exp(gl - gc) for k, gl, gc in zip(ks, gls, gcs)]
    res = tuple(us), tuple(ws), tuple(qks), tuple(qds), tuple(kds)
    return res + (tinvs,) if with_t else res


def _dn_step(us, ws, qks, qds, kds, grows, ss):
    gls = [jnp.exp(jnp.sum(g, axis=1, keepdims=True)) for g in grows]
    wss = [_nn(w, s) for w, s in zip(ws, ss)]
    qss = [_nn(qd, s) for qd, s in zip(qds, ss)]
    vns = [u - x for u, x in zip(us, wss)]
    os_ = [a + _nn(qk, vn) for a, qk, vn in zip(qss, qks, vns)]
    s2s = [s * gl + _tn(kd, vn) for s, gl, kd, vn in zip(ss, gls, kds, vns)]
    return tuple(os_), tuple(s2s)


def _hs(h):
    return slice(h * DN_DIM, (h + 1) * DN_DIM)


_DIR_SGN = (1, -1)
_A_OUT = 5
_PROBLEMS = [(d, h) for d in range(2) for h in range(DN_HEADS)]
_SGNS = [_DIR_SGN[d] for d, _ in _PROBLEMS]


def _chunk_inputs(q_ref, k_ref, v_ref, g_ref, b_ref):
    heads = lambda ref: tuple(ref[:, _hs(h)].astype(F32) for _, h in _PROBLEMS)
    rows = lambda ref: tuple(ref[d, 0, h:h + 1, :] for d, h in _PROBLEMS)
    return heads(q_ref), heads(k_ref), heads(v_ref), rows(g_ref), rows(b_ref)


def _dn_a_fwd(qkv, grows, brows):
    t_ = qkv.shape[0]
    nch = t_ // CHUNK

    def body(q_ref, k_ref, v_ref, g_ref, b_ref, *outs):
        us, ws, qks, qds, kds, tinvs = _dn_chunk(_SGNS, *_chunk_inputs(q_ref, k_ref, v_ref, g_ref, b_ref), with_t=True)
        for p, (d, h) in enumerate(_PROBLEMS):
            u_ref, w_ref, qk_ref, qd_ref, kd_ref = outs[d * _A_OUT:(d + 1) * _A_OUT]
            u_ref[:, _hs(h)], w_ref[:, _hs(h)], qk_ref[0, h] = us[p], ws[p].astype(_MXU), qks[p].astype(_MXU)
            qd_ref[:, _hs(h)], kd_ref[:, _hs(h)] = qds[p].astype(_MXU), kds[p].astype(_MXU)
            outs[2 * _A_OUT + d][0, h] = tinvs[p]

    rspec = pl.BlockSpec((2, 1, DN_HEADS, CHUNK), lambda c: (0, c, 0, 0))
    big = pl.BlockSpec((CHUNK, D), lambda c: (c, 0))
    qks = pl.BlockSpec((1, DN_HEADS, CHUNK, CHUNK), lambda c: (c, 0, 0, 0))
    bigs = lambda dt: jax.ShapeDtypeStruct((t_, D), dt)
    qksh = lambda dt: jax.ShapeDtypeStruct((nch, DN_HEADS, CHUNK, CHUNK), dt)
    res = pl.pallas_call(
        body, name="dn_a_fwd", grid=(nch,),
        in_specs=[pl.BlockSpec((CHUNK, D), lambda c, p=p: (c, p)) for p in range(3)] + [rspec, rspec],
        out_specs=[big, big, qks, big, big] * 2 + [qks, qks],
        out_shape=[bigs(F32), bigs(_MXU), qksh(_MXU), bigs(_MXU), bigs(_MXU)] * 2 + [qksh(F32)] * 2,
        compiler_params=_params(1))(qkv, qkv, qkv, grows, brows)
    return res[:2 * _A_OUT], res[2 * _A_OUT:]


def _dn_a_bwd(qkv, grows, brows, tinv, dres, dg_b):
    t_ = qkv.shape[0]
    nch = t_ // CHUNK

    def body(q_ref, k_ref, v_ref, g_ref, b_ref, tf_ref, tb_ref, *rest):
        dins, dgb_ref, (dqkv_ref, dg_ref, db_ref) = rest[:2 * _A_OUT], rest[2 * _A_OUT], rest[2 * _A_OUT + 1:]
        tsaved = tuple((tf_ref, tb_ref)[d][0, h] for d, h in _PROBLEMS)
        _, vjp = jax.vjp(functools.partial(_dn_chunk, _SGNS, tsaved=tsaved),
                         *_chunk_inputs(q_ref, k_ref, v_ref, g_ref, b_ref))
        cots = []
        for o in range(_A_OUT):
            cots.append(tuple(dins[d * _A_OUT + o][0, h] if o == 2 else dins[d * _A_OUT + o][:, _hs(h)]
                              for d, h in _PROBLEMS))
        gq, gk, gv, gg, gb = vjp(tuple(cots))
        for p, (d, h) in enumerate(_PROBLEMS):
            dg_ref[d, 0, h:h + 1, :] = gg[p] + dgb_ref[d, 0, h:h + 1, :]
            db_ref[d, 0, h:h + 1, :] = gb[p]
        for h in range(DN_HEADS):
            dqkv_ref[:, _hs(h)] = gq[h] + gq[DN_HEADS + h]
            dqkv_ref[:, _hs(DN_HEADS + h)] = gk[h] + gk[DN_HEADS + h]
            dqkv_ref[:, _hs(2 * DN_HEADS + h)] = gv[h] + gv[DN_HEADS + h]

    rspec = pl.BlockSpec((2, 1, DN_HEADS, CHUNK), lambda c: (0, c, 0, 0))
    big = pl.BlockSpec((CHUNK, D), lambda c: (c, 0))
    qks = pl.BlockSpec((1, DN_HEADS, CHUNK, CHUNK), lambda c: (c, 0, 0, 0))
    rsh = jax.ShapeDtypeStruct(grows.shape, F32)
    return pl.pallas_call(
        body, name="dn_a_bwd", grid=(nch,),
        in_specs=[pl.BlockSpec((CHUNK, D), lambda c, p=p: (c, p)) for p in range(3)] + [rspec, rspec, qks, qks]
        + [big, big, qks, big, big] * 2 + [rspec],
        out_specs=[pl.BlockSpec((CHUNK, 3 * D), lambda c: (c, 0)), rspec, rspec],
        out_shape=[jax.ShapeDtypeStruct((t_, 3 * D), F32), rsh, rsh],
        compiler_params=_params(1))(qkv, qkv, qkv, grows, brows, *tinv, *dres, dg_b)


def _dir_specs(nch):
    def cidx(d):
        return (lambda n: n) if d == 0 else (lambda n: nch - 1 - n)
    out = []
    for d in range(2):
        ci = cidx(d)
        big = pl.BlockSpec((CHUNK, D), lambda n, ci=ci: (ci(n), 0))
        qks = pl.BlockSpec((1, DN_HEADS, CHUNK, CHUNK), lambda n, ci=ci: (ci(n), 0, 0, 0))
        row = pl.BlockSpec((1, 1, DN_HEADS, CHUNK), lambda n, ci=ci, d=d: (d, ci(n), 0, 0))
        st = pl.BlockSpec((1, DN_HEADS, DN_DIM, DN_DIM), lambda n, ci=ci: (ci(n), 0, 0, 0))
        out.append(dict(big=big, qk=qks, row=row, st=st))
    return out


def _step_inputs(ins, per_dir):
    def pick(o):
        if o == 2:
            return tuple(ins[d * per_dir + o][0, h].astype(F32) for d, h in _PROBLEMS)
        if o == 5:
            return tuple(ins[d * per_dir + o][0, 0, h:h + 1, :] for d, h in _PROBLEMS)
        return tuple(ins[d * per_dir + o][:, _hs(h)].astype(F32) for d, h in _PROBLEMS)
    return [pick(o) for o in range(6)]


def _dn_b_fwd(ares, grows):
    t_ = ares[0].shape[0]
    nch = t_ // CHUNK
    sp = _dir_specs(nch)

    def body(*refs):
        ins, outs, s_ref = refs[:12], refs[12:16], refs[16]

        @pl.when(pl.program_id(0) == 0)
        def _():
            s_ref[...] = jnp.zeros_like(s_ref)

        ss = tuple(s_ref[p] for p in range(len(_PROBLEMS)))
        os_, s2s = _dn_step(*_step_inputs(ins, 6), ss)
        for p, (d, h) in enumerate(_PROBLEMS):
            outs[2 + d][0, h] = ss[p]
            outs[d][:, _hs(h)] = os_[p]
            s_ref[p] = s2s[p]

    in_specs, args = [], []
    for d in range(2):
        in_specs += [sp[d]["big"], sp[d]["big"], sp[d]["qk"], sp[d]["big"], sp[d]["big"], sp[d]["row"]]
        args += list(ares[d * _A_OUT:(d + 1) * _A_OUT]) + [grows]
    stsh = jax.ShapeDtypeStruct((nch, DN_HEADS, DN_DIM, DN_DIM), F32)
    osh = jax.ShapeDtypeStruct((t_, D), F32)
    return pl.pallas_call(
        body, name="dn_b_fwd", grid=(nch,), in_specs=in_specs,
        out_specs=[sp[0]["big"], sp[1]["big"], sp[0]["st"], sp[1]["st"]], out_shape=[osh, osh, stsh, stsh],
        scratch_shapes=[pltpu.VMEM((2 * DN_HEADS, DN_DIM, DN_DIM), F32)],
        compiler_params=_params(1))(*args)


def _dn_b_bwd(ares, grows, st_f, st_b, do):
    t_ = ares[0].shape[0]
    nch = t_ // CHUNK
    sp = _dir_specs(nch)
    rsp = [sp[1], sp[0]]

    def body(*refs):
        ins, outs, ds_ref = refs[:16], refs[16:28], refs[28]

        @pl.when(pl.program_id(0) == 0)
        def _():
            ds_ref[...] = jnp.zeros_like(ds_ref)

        ss = tuple(ins[d * 8 + 6][0, h] for d, h in _PROBLEMS)
        _, vjp = jax.vjp(_dn_step, *_step_inputs(ins, 8), ss)
        dos = tuple(ins[d * 8 + 7][:, _hs(h)] for d, h in _PROBLEMS)
        grads = vjp((dos, tuple(ds_ref[p] for p in range(len(_PROBLEMS)))))
        for p, (d, h) in enumerate(_PROBLEMS):
            du_ref, dw_ref, dqk_ref, dqd_ref, dkd_ref, dg_ref = outs[d * 6:(d + 1) * 6]
            du_ref[:, _hs(h)], dw_ref[:, _hs(h)], dqk_ref[0, h] = grads[0][p], grads[1][p], grads[2][p]
            dqd_ref[:, _hs(h)], dkd_ref[:, _hs(h)] = grads[3][p], grads[4][p]
            dg_ref[0, 0, h:h + 1, :] = grads[5][p]
            ds_ref[p] = grads[6][p]

    in_specs, args, out_specs, out_shape = [], [], [], []
    big_sh = jax.ShapeDtypeStruct((t_, D), F32)
    qk_sh = jax.ShapeDtypeStruct((nch, DN_HEADS, CHUNK, CHUNK), F32)
    row_sh = jax.ShapeDtypeStruct((1, nch, DN_HEADS, CHUNK), F32)
    for d in range(2):
        s = rsp[d]
        row0 = pl.BlockSpec((1, 1, DN_HEADS, CHUNK), lambda m, d=d: (0, (nch - 1 - m) if d == 0 else m, 0, 0))
        rowd = pl.BlockSpec((1, 1, DN_HEADS, CHUNK), lambda m, d=d: (d, (nch - 1 - m) if d == 0 else m, 0, 0))
        in_specs += [s["big"], s["big"], s["qk"], s["big"], s["big"], rowd, s["st"], s["big"]]
        args += list(ares[d * _A_OUT:(d + 1) * _A_OUT]) + [grows, (st_f, st_b)[d], do]
        out_specs += [s["big"], s["big"], s["qk"], s["big"], s["big"], row0]
        out_shape += [big_sh, big_sh, qk_sh, big_sh, big_sh, row_sh]
    res = pl.pallas_call(
        body, name="dn_b_bwd", grid=(nch,), in_specs=in_specs, out_specs=out_specs, out_shape=out_shape,
        scratch_shapes=[pltpu.VMEM((2 * DN_HEADS, DN_DIM, DN_DIM), F32)],
        compiler_params=_params(1))(*args)
    dares = list(res[0:5]) + list(res[6:11])
    return dares, jnp.concatenate([res[5], res[11]], axis=0)


@jax.custom_vjp
def _rot_half(x):
    half, width = SW_DIM // 2, x.shape[1]
    first = lax.broadcasted_iota(jnp.int32, x.shape, 1) % SW_DIM < half
    return jnp.where(first, -pltpu.roll(x, width - half, axis=1), pltpu.roll(x, half, axis=1))


_rot_half.defvjp(lambda x: (_rot_half(x), None), lambda _, g: (-_rot_half(g),))


def _rope(x, c, s):
    reps = x.shape[1] // c.shape[1]
    return x * jnp.tile(c, (1, reps)) + _rot_half(x) * jnp.tile(s, (1, reps))


def _rope_t(g, c, s):
    reps = g.shape[1] // c.shape[1]
    return g * jnp.tile(c, (1, reps)) - _rot_half(g * jnp.tile(s, (1, reps)))


_SW_SCALE = SW_DIM ** -0.5
_KV_HEADS = [[kvh * SW_GRP + g for g in range(SW_GRP)] for kvh in range(SW_KV)]


def _by_group(x):
    return [jnp.concatenate([x[:, h * SW_DIM:(h + 1) * SW_DIM] for h in hs], axis=0) for hs in _KV_HEADS]


def _from_groups(xs):
    return jnp.concatenate([x[g * SW_BLOCK:(g + 1) * SW_BLOCK] for x in xs for g in range(SW_GRP)], axis=-1)


def _attn_probs(blk, t_, cq, sq, ck, sk, q, kall, sinks):
    qgs = _by_group(_rope(q, cq, sq))
    kr = _rope(kall, ck, sk)
    khs = [kr[:, kvh * SW_DIM:(kvh + 1) * SW_DIM] for kvh in range(SW_KV)]
    nq, nk = SW_GRP * SW_BLOCK, 3 * SW_BLOCK
    qpos = lax.broadcasted_iota(jnp.int32, (nq, nk), 0) % SW_BLOCK
    krel = lax.broadcasted_iota(jnp.int32, (nq, nk), 1) - SW_BLOCK
    kglob = krel + blk * SW_BLOCK
    valid = (jnp.abs(qpos - krel) <= SW_BLOCK) & (kglob >= 0) & (kglob < t_)
    ss = [jnp.where(valid, _nt(qg * _SW_SCALE, kh), -1e30) for qg, kh in zip(qgs, khs)]
    snks = [jnp.concatenate([jnp.broadcast_to(sinks[:, h:h + 1], (SW_BLOCK, 1)) for h in hs], axis=0) for hs in _KV_HEADS]
    ms = [jnp.maximum(jnp.max(s, axis=-1, keepdims=True), snk) for s, snk in zip(ss, snks)]
    es = [jnp.exp(s - m) for s, m in zip(ss, ms)]
    esnks = [jnp.exp(snk - m) for snk, m in zip(snks, ms)]
    invs = [1.0 / (jnp.sum(e, axis=-1, keepdims=True) + esnk) for e, esnk in zip(es, esnks)]
    ps = [e * inv for e, inv in zip(es, invs)]
    return qgs, khs, ps, [esnk * inv for esnk, inv in zip(esnks, invs)]


def _attn_f(blk, t_, cq, sq, ck, sk, q, kp, ko, kn, vp, vo, vn, sinks):
    _, _, ps, _ = _attn_probs(blk, t_, cq, sq, ck, sk, q, jnp.concatenate([kp, ko, kn], axis=0), sinks)
    vall = jnp.concatenate([vp, vo, vn], axis=0)
    return _from_groups([_nn(p, vall[:, kvh * SW_DIM:(kvh + 1) * SW_DIM]) for kvh, p in enumerate(ps)])


def _attn_bwd_f(blk, t_, cq, sq, ck, sk, q, kp, ko, kn, vp, vo, vn, sinks, do):
    qgs, khs, ps, psinks = _attn_probs(blk, t_, cq, sq, ck, sk, q, jnp.concatenate([kp, ko, kn], axis=0), sinks)
    vall = jnp.concatenate([vp, vo, vn], axis=0)
    vhs = [vall[:, kvh * SW_DIM:(kvh + 1) * SW_DIM] for kvh in range(SW_KV)]
    dogs = _by_group(do)
    dvs = [_tn(p, dog) for p, dog in zip(ps, dogs)]
    dps = [_nt(dog, vh) for dog, vh in zip(dogs, vhs)]
    deltas = [jnp.sum(p * dp, axis=-1, keepdims=True) for p, dp in zip(ps, dps)]
    dss = [p * ((dp - delta) * _SW_SCALE) for p, dp, delta in zip(ps, dps, deltas)]
    dqr = _from_groups([_nn(ds, kh) for ds, kh in zip(dss, khs)])
    dkr = jnp.concatenate([_tn(ds, qg) for ds, qg in zip(dss, qgs)], axis=-1)
    dsnk = [-(psink * delta) for psink, delta in zip(psinks, deltas)]
    dsinks = jnp.concatenate([jnp.sum(d[g * SW_BLOCK:(g + 1) * SW_BLOCK], axis=0, keepdims=True)
                              for d in dsnk for g in range(SW_GRP)], axis=1)
    dq, dk, dv = _rope_t(dqr, cq, sq), _rope_t(dkr, ck, sk), jnp.concatenate(dvs, axis=-1)
    blocks = lambda a: [a[j * SW_BLOCK:(j + 1) * SW_BLOCK] for j in range(3)]
    return [dq] + blocks(dk) + blocks(dv) + [dsinks]


def _attn_specs(nb):
    prv = lambda i: jnp.maximum(i - 1, 0)
    nxt = lambda i: jnp.minimum(i + 1, nb - 1)
    rows = [lambda i: i, prv, lambda i: i, nxt]
    tab = [pl.BlockSpec((SW_BLOCK, 128), lambda i, r=r: (r(i), 0)) for r in rows]
    qs = pl.BlockSpec((SW_BLOCK, SW_HEADS * SW_DIM), lambda i: (i, O_QS // (SW_HEADS * SW_DIM)))
    kw = SW_KV * SW_DIM
    ks = [pl.BlockSpec((SW_BLOCK, kw), lambda i, r=r: (r(i), O_KS // kw)) for r in rows[1:]]
    vs = [pl.BlockSpec((SW_BLOCK, kw), lambda i, r=r: (r(i), O_VS // kw)) for r in rows[1:]]
    return tab, qs, ks, vs


def _attn_tables(refs):
    cq, cp, co, cn, sq, sp_, so, sn = [r[...] for r in refs]
    return cq, sq, jnp.concatenate([cp, co, cn], axis=0), jnp.concatenate([sp_, so, sn], axis=0)


def _attn_fwd(pm, cos, sin, sinks):
    t_ = pm.shape[0]
    nb = t_ // SW_BLOCK
    tab, qs, ks, vs = _attn_specs(nb)

    def body(*refs):
        tabs = _attn_tables(refs[:8])
        vals = [r[...] for r in refs[8:16]]
        refs[16][...] = _attn_f(pl.program_id(0), t_, *tabs, *vals).astype(refs[16].dtype)

    return pl.pallas_call(
        body, name="attn_fwd", grid=(nb,), in_specs=tab + tab + [qs] + ks + vs + [_full(sinks)],
        out_specs=pl.BlockSpec((SW_BLOCK, D), lambda i: (i, 0)), out_shape=jax.ShapeDtypeStruct((t_, D), _MXU),
        compiler_params=_params(1))(*([cos] * 4), *([sin] * 4), pm, pm, pm, pm, pm, pm, pm, sinks)


def _attn_bwd(pm, cos, sin, sinks, do, dpm):
    t_ = pm.shape[0]
    nb = t_ // SW_BLOCK
    tab, qs, ks, vs = _attn_specs(nb)
    kw = SW_KV * SW_DIM

    def body(*refs):
        tabs = _attn_tables(refs[:8])
        vals = [r[...] for r in refs[8:16]]
        do_ref, outs = refs[16], refs[18:]
        grads = _attn_bwd_f(pl.program_id(0), t_, *tabs, *vals, do_ref[...])
        for o_ref, g in zip(outs[:7], grads[:7]):
            o_ref[...] = g.astype(o_ref.dtype)

        @pl.when(pl.program_id(0) == 0)
        def _():
            outs[7][...] = jnp.zeros_like(outs[7])
        outs[7][...] += grads[7]

    own = lambda w: pl.BlockSpec((SW_BLOCK, w), lambda i: (i, 0))
    return pl.pallas_call(
        body, name="attn_bwd", grid=(nb,),
        in_specs=tab + tab + [qs] + ks + vs + [_full(sinks), own(D), ANY],
        out_specs=[pl.BlockSpec((SW_BLOCK, D), lambda i: (i, O_QS // D))] + [own(kw)] * 6 + [_full(sinks)],
        out_shape=[jax.ShapeDtypeStruct(dpm.shape, dpm.dtype)] + [jax.ShapeDtypeStruct((t_, kw), F32)] * 6
        + [jax.ShapeDtypeStruct(sinks.shape, F32)],
        input_output_aliases={17: 0},
        compiler_params=_params(1))(*([cos] * 4), *([sin] * 4), pm, pm, pm, pm, pm, pm, pm, sinks, do, dpm)


def _band_sum(kparts, vparts, dpm):
    t_, kw = kparts[1].shape
    nb = t_ // SW_BLOCK

    def body(kp, ko, kn, vp, vo, vn, _, out_ref):
        j = pl.program_id(0)
        band = lambda p, o, n: o[...] + jnp.where(j + 1 < nb, p[...], 0.0) + jnp.where(j > 0, n[...], 0.0)
        out_ref[...] = jnp.concatenate([band(kp, ko, kn), band(vp, vo, vn)], axis=-1).astype(out_ref.dtype)

    specs = [pl.BlockSpec((SW_BLOCK, kw), lambda j: (jnp.minimum(j + 1, nb - 1), 0)),
             pl.BlockSpec((SW_BLOCK, kw), lambda j: (j, 0)),
             pl.BlockSpec((SW_BLOCK, kw), lambda j: (jnp.maximum(j - 1, 0), 0))]
    return pl.pallas_call(
        body, name="band_sum", grid=(nb,), in_specs=specs * 2 + [ANY],
        out_specs=pl.BlockSpec((SW_BLOCK, 2 * kw), lambda j: (j, O_KS // (2 * kw))),
        out_shape=jax.ShapeDtypeStruct(dpm.shape, dpm.dtype), input_output_aliases={6: 0},
        compiler_params=_params(1))(*kparts, *vparts, dpm)


def _loss_head(y, target, tb=256):
    t_ = y.shape[0]

    def body(y_ref, t_ref, dy_ref, acc_ref):
        err = y_ref[...] - t_ref[...]
        dy_ref[...] = err * (1.0 / D)
        sq = (err * err).reshape(tb // 8, 8, D).sum(axis=0)
        part = sq[:, 0:128]
        for c in range(1, D // 128):
            part = part + sq[:, c * 128:(c + 1) * 128]

        @pl.when(pl.program_id(0) == 0)
        def _():
            acc_ref[...] = jnp.zeros_like(acc_ref)
        acc_ref[...] += part

    row = pl.BlockSpec((tb, D), lambda i: (i, 0))
    return pl.pallas_call(
        body, name="loss_head", grid=(t_ // tb,), in_specs=[row, row],
        out_specs=[row, pl.BlockSpec((8, 128), lambda i: (0, 0))],
        out_shape=[jax.ShapeDtypeStruct((t_, D), F32), jax.ShapeDtypeStruct((8, 128), F32)],
        compiler_params=_params(1))(y, target)


def _adam_math(w, g, m, v):
    bc1 = 1.0 - ADAM_B1 ** ADAM_STEP
    bc2 = 1.0 - ADAM_B2 ** ADAM_STEP
    nm = ADAM_B1 * m + (1.0 - ADAM_B1) * g
    nv = ADAM_B2 * v + (1.0 - ADAM_B2) * (g * g)
    return -ADAM_LR * ((nm / bc1) / (jnp.sqrt(nv / bc2) + ADAM_EPS) + ADAM_WD * w), nm, nv


def _adamw_layer(name, l, w, g, m, v, carry):
    nl, r, c = w.shape
    g2 = g.reshape(r, -1)
    gc = g2.shape[1]
    w2, m2, v2 = [a.reshape(nl * r, c) for a in (w, m, v)]
    tb = r
    while tb * gc * 4 > (1 << 20) and tb % 16 == 0:
        tb //= 2
    nb = r // tb

    def body(w_ref, g_ref, m_ref, v_ref, *rest):
        go_ref, d_ref, nm_ref, nv_ref = rest[-4:]
        gv = g_ref[...][:, :c]
        go_ref[...] = gv
        d_ref[...], nm_ref[...], nv_ref[...] = _adam_math(w_ref[...], gv, m_ref[...], v_ref[...])

    spec = pl.BlockSpec((tb, c), lambda i: (l * nb + i, 0))
    carried = list(carry) if carry is not None else []
    outs = pl.pallas_call(
        body, name=name, grid=(nb,),
        in_specs=[spec, pl.BlockSpec((tb, gc), lambda i: (i, 0)), spec, spec] + [ANY] * len(carried),
        out_specs=[spec] * 4, out_shape=[jax.ShapeDtypeStruct((nl * r, c), F32)] * 4,
        input_output_aliases={4 + k: k for k in range(len(carried))},
        compiler_params=_params(1))(w2, g2, m2, v2, *[a.reshape(nl * r, c) for a in carried])
    return tuple(o.reshape(w.shape) for o in outs)


def _adamw(name, w, g, m, v):
    shape = w.shape
    cols = shape[-1]
    rows = w.size // cols
    w2, g2, m2, v2 = [a.reshape(rows, cols) for a in (w, g, m, v)]
    tb = rows
    while tb * cols * 4 > (1 << 20) and tb % 16 == 0:
        tb //= 2

    def body(w_ref, g_ref, m_ref, v_ref, d_ref, nm_ref, nv_ref):
        d_ref[...], nm_ref[...], nv_ref[...] = _adam_math(w_ref[...], g_ref[...], m_ref[...], v_ref[...])

    spec = pl.BlockSpec((tb, cols), lambda i: (i, 0))
    sh = jax.ShapeDtypeStruct((rows, cols), F32)
    outs = pl.pallas_call(body, name=name, grid=(rows // tb,), in_specs=[spec] * 4, out_specs=[spec] * 3,
                          out_shape=[sh] * 3, compiler_params=_params(1))(w2, g2, m2, v2)
    return [o.reshape(shape) for o in outs]


def _to_rows(bg, col0):
    t_ = bg.shape[0]
    a = bg[:, col0:col0 + 2 * DN_HEADS].reshape(t_ // CHUNK, CHUNK, 2, DN_HEADS)
    return jnp.transpose(a, (2, 0, 3, 1))


def _from_rows(db, dg):
    nch = db.shape[1]
    back = lambda a: jnp.transpose(a, (1, 3, 0, 2)).reshape(nch * CHUNK, 2 * DN_HEADS)
    return jnp.pad(jnp.concatenate([back(db), back(dg)], axis=1), ((0, 0), (0, 128 - 4 * DN_HEADS)))


def _layer_fwd(x, xm, w, rest, cos, sin):
    t_ = x.shape[0]
    tb, nb = min(256, t_), min(1024, t_)
    pm = _mm(xm, w["in_main"], name="mm_in", tn=1536)
    pbg = _mm(xm, w["in_bg"], name="mm_in_bg")
    w = {**w, **rest(pbg)}
    qkv = _prep_fwd(pm, w["conv"])
    bg, = _stage_fwd(_bg_f, "bg_fwd", [(pbg, 128, 0, False)], [w["arow"], w["dtrow"]], [(128, F32)], nb)
    brows, grows = _to_rows(bg, 0), _to_rows(bg, 2 * DN_HEADS)
    ares, tinv = _dn_a_fwd(qkv, grows, brows)
    o_f, o_b, st_f, st_b = _dn_b_fwd(ares, grows)
    odn, = _stage_fwd(_gnorm_f, "gnorm_fwd", [(o_f, 128, 0, True), (o_b, 128, 0, True), (pm, 128, O_Z // 128, True)],
                      [w["gnw"]], [(128, _MXU)], nb, ncol=DN_HEADS)
    osw = _attn_fwd(pm, cos, sin, w["sinks"])
    same, first = (lambda j: j), (lambda j: 0)
    full = lambda dt: (D, D, dt, first, None)
    ya = _mm(odn, w["a"], name="mm_a")
    yb, merged = _mm_fused(
        osw, w["b"], lambda acc, ya_, ga, gb: (acc,) + _merge_f(ya_, acc, ga, gb), [full(F32), full(_MXU)],
        name="mm_b_merge", rows=[(ya, D, first), (pm, D, lambda j: O_GA // D), (pm, D, lambda j: O_GB // D)], tm=512)
    mix, x1, x1m = _mm_fused(
        merged, w["o"], lambda acc, x_, g_, b_: (acc,) + _ln_f2(x_, acc, g_, b_), [full(F32), full(F32), full(_MXU)],
        name="mm_o_ln", rows=[(x, D, first)], params=[w["ln1g"], w["ln1b"]], tm=512)
    gu, hid = _mm_fused(x1m, w["gu"], lambda acc: (acc,) + _swiglu_tile(acc),
                        [(2 * FFN, FFN, F32, same, None), (FFN, FFN // 2, _MXU, same, None)], name="mm_gu_swiglu", tn=FFN)
    ffn, x2, x2m = _mm_fused(
        hid, w["d"], lambda acc, x_, g_, b_: (acc,) + _ln_f2(x_, acc, g_, b_), [full(F32), full(F32), full(_MXU)],
        name="mm_d_ln", rows=[(x1, D, first)], params=[w["ln2g"], w["ln2b"]], tm=512, tk=FFN)
    res = dict(w=w, x=x, xm=xm, pm=pm, pbg=pbg, qkv=qkv, grows=grows, brows=brows, ares=ares, tinv=tinv, o_f=o_f, o_b=o_b, st_f=st_f,
               st_b=st_b, odn=odn, osw=osw, ya=ya, yb=yb, merged=merged, mix=mix, x1=x1, x1m=x1m, gu=gu, hid=hid,
               ffn=ffn)
    return x2, x2m, res


def _layer_bwd(dx2, r, w, cos, sin):
    t_ = dx2.shape[0]
    tb, nb = min(256, t_), min(1024, t_)
    pm = r["pm"]
    g = {}
    dx1a, dffn, g["ln2g"], g["ln2b"] = _stage_bwd(
        _ln_f, "ln2_bwd", [(r["x1"], D, 0, False), (r["ffn"], D, 0, False)], [w["ln2g"], w["ln2b"]], [dx2], tb,
        dtypes=[F32, _MXU])
    same = lambda j: j
    dgu, = _mm_fused(dffn, w["d"], lambda dhid, gu: jax.vjp(_swiglu_tile, gu)[1]((dhid,)),
                     [(2 * FFN, FFN, _MXU, same, None)], name="mm_d_dx_swiglu", rows=[(r["gu"], FFN, same)], tb=True,
                     tn=FFN // 2)
    g["d"] = _mm(r["hid"], dffn, ta=True, name="mm_d_dw", tm=FFN // 2)
    first = lambda j: 0
    full = lambda dt: (D, D, dt, first, None)

    def ln1_back(acc, dx1a_, x_, mix_, g_, b_):
        dx_, dmix_, dg_, db_ = jax.vjp(_ln_f, x_, mix_, g_, b_)[1]((acc + dx1a_,))
        return dx_, dmix_, dg_, db_

    dxa, dmix, g["ln1g"], g["ln1b"] = _mm_fused(
        dgu, w["gu"], ln1_back, [full(F32), full(_MXU)], name="mm_gu_dx_ln", tb=True, tm=256, tk=2 * FFN, n_sums=2,
        rows=[(dx1a, D, first), (r["x"], D, first), (r["mix"], D, first)], params=[w["ln1g"], w["ln1b"]])
    g["gu"] = _mm(r["x1m"], dgu, ta=True, name="mm_gu_dw", tn=FFN // 2)
    g["o"] = _mm(r["merged"], dmix, ta=True, name="mm_o_dw")

    def merge_back(acc, ya_, yb_, ga, gb):
        dya_, dyb_, dga, dgb = jax.vjp(_merge_f, ya_, yb_, ga, gb)[1]((acc,))
        return dya_, dyb_, jnp.concatenate([dga, dgb], axis=-1)

    dya, dyb, dpm = _mm_fused(
        dmix, w["o"], merge_back, [full(_MXU), full(_MXU), (N_MAIN, 2 * D, _MXU, lambda j: O_GA // (2 * D), None)],
        name="mm_o_dx_merge", tb=True, tm=512,
        rows=[(r["ya"], D, first), (r["yb"], D, first), (pm, D, lambda j: O_GA // D), (pm, D, lambda j: O_GB // D)])
    dodn = _mm(dya, w["a"], tb=True, name="mm_a_dx")
    g["a"] = _mm(r["odn"], dya, ta=True, name="mm_a_dw")
    dosw = _mm(dyb, w["b"], tb=True, name="mm_b_dx")
    g["b"] = _mm(r["osw"], dyb, ta=True, name="mm_b_dw")
    ab = _attn_bwd(pm, cos, sin, w["sinks"], dosw, dpm)
    dpm, g["sinks"] = ab[0], ab[7]
    dpm = _band_sum(ab[1:4], ab[4:7], dpm)
    dof, _, dpm, g["gnw"] = _stage_bwd(
        _gnorm_f, "gnorm_bwd", [(r["o_f"], 128, 0, True), (r["o_b"], 128, 0, True), (pm, 128, O_Z // 128, True)],
        [w["gnw"]], [dodn], nb, ncol=DN_HEADS, dtypes=[F32, F32, _MXU], dest={2: (dpm, N_MAIN, O_Z // 128)})
    dares, dg_b = _dn_b_bwd(r["ares"], r["grows"], r["st_f"], r["st_b"], dof)
    dqkv, dgrows, dbrows = _dn_a_bwd(r["qkv"], r["grows"], r["brows"], r["tinv"], dares, dg_b)
    dbg = _from_rows(dbrows, dgrows)
    dpbg, g["arow"], g["dtrow"] = _stage_bwd(_bg_f, "bg_bwd", [(r["pbg"], 128, 0, False)], [w["arow"], w["dtrow"]],
                                             [dbg], nb)
    dpm, g["conv"] = _prep_bwd(pm, w["conv"], dqkv, dpm)
    dx = _mm(dpm, w["in_main"], tb=True, add=dxa, name="mm_in_dx", tk=1920)
    dx = _mm(dpbg, w["in_bg"], tb=True, add=dx, name="mm_in_bg_dx")
    g["in_main"] = _mm(r["xm"], dpm, ta=True, name="mm_in_dw", tn=1536)
    g["in_bg"] = _mm(r["xm"], dpbg, ta=True, name="mm_in_bg_dw")
    return dx, g


def _place():
    return lax.axis_index("x"), lax.axis_index("y"), lax.axis_index("c")


def _colblock(kind, q):
    return q if kind == "col" else (q >> 1) | ((q & 1) << 1)


def _other_chips(x, y):
    return [(1 - x, y), (x, 1 - y), (1 - x, 1 - y)]


HBM = pl.BlockSpec(memory_space=pltpu.HBM)
SEM = pl.BlockSpec(memory_space=pltpu.SEMAPHORE)
DATAFLOW = pltpu.SideEffectType.DATAFLOW_SIDE_EFFECTING


def _hbm(a):
    return pltpu.HBM(a.shape, a.dtype)


def _gather_start(locs, kinds, name):
    n = len(locs)
    locs = list(locs)
    lands = [lax.empty((N_CHIPS,) + a.shape if kind == "row" else (a.shape[0], N_CHIPS * a.shape[1]), a.dtype)
             for a, kind in zip(locs, kinds)]

    def body(*refs):
        loc_refs, land_refs = refs[:n], refs[n:2 * n]
        send_sems, recv_sems, token = refs[2 * n:3 * n], refs[3 * n:4 * n], refs[-1]
        x, y, c = _place()
        myq = 2 * x + y
        for t in range(n):
            width = locs[t].shape[1]
            mine = (land_refs[t].at[myq] if kinds[t] == "row" else
                    land_refs[t].at[:, pl.ds(pl.multiple_of(_colblock(kinds[t], myq) * width, 128), width)])
            for dev in [(cx, cy, c) for cx, cy in _other_chips(x, y)] + [(x, y, 1 - c)]:
                pltpu.make_async_remote_copy(src_ref=loc_refs[t], dst_ref=mine, send_sem=send_sems[t],
                                             recv_sem=recv_sems[t], device_id=dev, device_id_type=MESH).start()
        token[...] = jnp.zeros_like(token)

    res = pl.pallas_call(
        body, name=name,
        out_shape=[pltpu.SemaphoreType.DMA(())] * (2 * n) + [_hbm(a) for a in locs + lands]
        + [jax.ShapeDtypeStruct((8, 128), F32)],
        in_specs=[HBM] * (2 * n), out_specs=[SEM] * (2 * n) + [HBM] * (2 * n) + [pl.BlockSpec(memory_space=pltpu.VMEM)],
        input_output_aliases={t: 2 * n + t for t in range(2 * n)},
        compiler_params=pltpu.CompilerParams(has_side_effects=DATAFLOW),
    )(*[pltpu.with_memory_space_constraint(a, pltpu.HBM) for a in locs + lands])
    return (res[:n], res[n:2 * n], res[2 * n:3 * n], res[3 * n:4 * n]), res[-1]


def _split_wait(handle, after, name):
    send_sems, recv_sems, srcs, lands = handle
    n = len(srcs)

    def body(*refs):
        land_refs, ssems, rsems = refs[n:2 * n], refs[2 * n:3 * n], refs[3 * n:4 * n]
        x, y, c = _place()
        for t in range(n):
            done = pltpu.make_async_remote_copy(
                src_ref=land_refs[t], dst_ref=land_refs[t], send_sem=ssems[t], recv_sem=rsems[t],
                device_id=(x, y, c), device_id_type=MESH)
            done.wait_send()
            done.wait_recv()

    res = pl.pallas_call(
        body, name=name, out_shape=[_hbm(a) for a in list(srcs) + list(lands)],
        in_specs=[HBM] * (2 * n) + [SEM] * (2 * n) + [ANY], out_specs=[HBM] * (2 * n),
        input_output_aliases={t: t for t in range(2 * n)},
        compiler_params=pltpu.CompilerParams(has_side_effects=DATAFLOW),
    )(*srcs, *lands, *send_sems, *recv_sems, after)
    return res[:n], res[n:]


RS_CHUNKS = 2


def _piece(ref, kind, q, hf, pr, pc):
    if kind == "row":
        return ref.at[pl.ds((2 * q + hf) * pr, pr), :]
    return ref.at[pl.ds(hf * pr, pr), pl.ds(pl.multiple_of(_colblock(kind, q) * pc, 128), pc)]


def _rs_to_sibling(ts, meta):
    n = len(ts)

    def body(*refs):
        t_refs, r1_refs, send_sems, recv_sems = refs[:n], refs[n:2 * n], refs[2 * n], refs[2 * n + 1]
        x, y, c = _place()
        sib = (x, y, 1 - c)
        for t, (kind, pr, pc) in enumerate(meta):
            for q in range(N_CHIPS):
                pltpu.make_async_remote_copy(
                    src_ref=_piece(t_refs[t], kind, q, 1 - c, pr, pc), dst_ref=r1_refs[t].at[q],
                    send_sem=send_sems.at[t], recv_sem=recv_sems.at[t], device_id=sib, device_id_type=MESH).start()
        for t in range(n):
            allq = pltpu.make_async_remote_copy(
                src_ref=r1_refs[t], dst_ref=r1_refs[t], send_sem=send_sems.at[t], recv_sem=recv_sems.at[t],
                device_id=sib, device_id_type=MESH)
            allq.wait_send()
            allq.wait_recv()

    return pl.pallas_call(
        body, name="rs_to_sibling", in_specs=[ANY] * n, out_specs=[ANY] * n,
        out_shape=[jax.ShapeDtypeStruct((N_CHIPS, pr, pc), F32) for (_, pr, pc) in meta],
        scratch_shapes=[pltpu.SemaphoreType.DMA((n,)), pltpu.SemaphoreType.DMA((n,))],
    )(*ts)


def _rs_add_sibling(ts, r1s, meta, c):
    n = len(ts)
    in_specs, out_specs, out_shape = [], [], []
    for kind, pr, pc in meta:
        rs = pr // RS_CHUNKS
        if kind == "row":
            in_specs.append(pl.BlockSpec((rs, pc), lambda q, r, c_ref: ((2 * q + c_ref[0]) * RS_CHUNKS + r, 0)))
        else:
            in_specs.append(pl.BlockSpec(
                (rs, pc), lambda q, r, c_ref, kind=kind: (c_ref[0] * RS_CHUNKS + r, _colblock(kind, q))))
    for kind, pr, pc in meta:
        sp = pl.BlockSpec((None, pr // RS_CHUNKS, pc), lambda q, r, c_ref: (q, r, 0))
        in_specs.append(sp)
        out_specs.append(sp)
        out_shape.append(jax.ShapeDtypeStruct((N_CHIPS, pr, pc), BF16))

    def body(c_ref, *refs):
        for t in range(n):
            refs[2 * n + t][...] = (refs[t][...] + refs[n + t][...]).astype(BF16)

    return pl.pallas_call(
        body, name="rs_add_sibling", out_shape=out_shape,
        grid_spec=pltpu.PrefetchScalarGridSpec(num_scalar_prefetch=1, grid=(N_CHIPS, RS_CHUNKS), in_specs=in_specs,
                                               out_specs=out_specs),
        compiler_params=_params(2))(c.reshape(1).astype(jnp.int32), *ts, *r1s)


def _rs_chips_start(ps, meta, name):
    n = len(ps)
    ps = list(ps)
    lands = [lax.empty((N_CHIPS - 1, pr, pc), p.dtype) for p, (_, pr, pc) in zip(ps, meta)]

    def body(*refs):
        p_refs, land_refs = refs[:n], refs[n:2 * n]
        send_sems, recv_sems, token = refs[2 * n:3 * n], refs[3 * n:4 * n], refs[-1]
        x, y, c = _place()
        for t in range(n):
            for j, (cx, cy) in enumerate(_other_chips(x, y)):
                pltpu.make_async_remote_copy(
                    src_ref=p_refs[t].at[2 * cx + cy], dst_ref=land_refs[t].at[j], send_sem=send_sems[t],
                    recv_sem=recv_sems[t], device_id=(cx, cy, c), device_id_type=MESH).start()
        token[...] = jnp.zeros_like(token)

    res = pl.pallas_call(
        body, name=name,
        out_shape=[pltpu.SemaphoreType.DMA(())] * (2 * n) + [_hbm(a) for a in ps + lands]
        + [jax.ShapeDtypeStruct((8, 128), F32)],
        in_specs=[HBM] * (2 * n), out_specs=[SEM] * (2 * n) + [HBM] * (2 * n) + [pl.BlockSpec(memory_space=pltpu.VMEM)],
        input_output_aliases={t: 2 * n + t for t in range(2 * n)},
        compiler_params=pltpu.CompilerParams(has_side_effects=DATAFLOW),
    )(*[pltpu.with_memory_space_constraint(a, pltpu.HBM) for a in ps + lands])
    return (res[:n], res[n:2 * n], res[2 * n:3 * n], res[3 * n:4 * n]), res[-1]


def _rs_add_chips(ps, r2s, meta, myq, c):
    n = len(ps)
    in_specs, out_specs, out_shape = [], [], []
    for _, pr, pc in meta:
        in_specs.append(pl.BlockSpec((None, pr // RS_CHUNKS, pc), lambda r, q_ref, c_ref: (q_ref[0], r, 0)))
    for _, pr, pc in meta:
        in_specs.append(pl.BlockSpec((N_CHIPS - 1, pr // RS_CHUNKS, pc), lambda r, q_ref, c_ref: (0, r, 0)))
        out_specs.append(pl.BlockSpec((None, pr // RS_CHUNKS, pc), lambda r, q_ref, c_ref: (c_ref[0], r, 0)))
        out_shape.append(jax.ShapeDtypeStruct((2, pr, pc), F32))

    def body(q_ref, c_ref, *refs):
        for t in range(n):
            r2 = refs[n + t]
            own = refs[t][...].astype(F32)
            refs[2 * n + t][...] = ((own + r2[0].astype(F32)) + r2[1].astype(F32)) + r2[2].astype(F32)

    return pl.pallas_call(
        body, name="rs_add_chips", out_shape=out_shape,
        grid_spec=pltpu.PrefetchScalarGridSpec(num_scalar_prefetch=2, grid=(RS_CHUNKS,), in_specs=in_specs,
                                               out_specs=out_specs),
        compiler_params=_params(1))(myq.reshape(1).astype(jnp.int32), c.reshape(1).astype(jnp.int32), *ps, *r2s)


def _rs_share_halves(gs):
    n = len(gs)

    def body(*refs):
        g_refs, send_sems, recv_sems = refs[n:2 * n], refs[2 * n], refs[2 * n + 1]
        x, y, c = _place()
        sib = (x, y, 1 - c)
        for t in range(n):
            pltpu.make_async_remote_copy(
                src_ref=g_refs[t].at[c], dst_ref=g_refs[t].at[c], send_sem=send_sems.at[t], recv_sem=recv_sems.at[t],
                device_id=sib, device_id_type=MESH).start()
        for t in range(n):
            cp = pltpu.make_async_remote_copy(
                src_ref=g_refs[t].at[c], dst_ref=g_refs[t].at[1 - c], send_sem=send_sems.at[t],
                recv_sem=recv_sems.at[t], device_id=sib, device_id_type=MESH)
            cp.wait_send()
            cp.wait_recv()

    return pl.pallas_call(
        body, name="rs_share_halves", in_specs=[ANY] * n, out_specs=[ANY] * n,
        out_shape=[jax.ShapeDtypeStruct(g.shape, g.dtype) for g in gs], input_output_aliases={t: t for t in range(n)},
        scratch_shapes=[pltpu.SemaphoreType.DMA((n,)), pltpu.SemaphoreType.DMA((n,))],
    )(*gs)


def _rs_begin(ts, meta, c, name):
    r1s = _rs_to_sibling(ts, meta)
    ps = _rs_add_sibling(ts, r1s, meta, c)
    return _rs_chips_start(ps, meta, name + "_start")


def _rs_end(handle, after, meta, c, myq, name):
    ps, r2s = _split_wait(handle, after, name + "_wait")
    return _rs_share_halves(_rs_add_chips(ps, r2s, meta, myq, c))


def _allreduce_small(buf):
    rows = buf.shape[0]
    ndev = 8

    def body(b_ref, o_ref, slots, send_sems, recv_sems):
        x, y, c = _place()
        me = 4 * x + 2 * y + c
        slots[me] = b_ref[...]
        for k in range(1, ndev):
            kx, ky, kc = (k >> 2) & 1, (k >> 1) & 1, k & 1
            peer = (x ^ kx, y ^ ky, c ^ kc)
            pltpu.make_async_remote_copy(
                src_ref=b_ref, dst_ref=slots.at[me], send_sem=send_sems.at[k - 1], recv_sem=recv_sems.at[k - 1],
                device_id=peer, device_id_type=MESH).start()
        for k in range(1, ndev):
            kx, ky, kc = (k >> 2) & 1, (k >> 1) & 1, k & 1
            cp = pltpu.make_async_remote_copy(
                src_ref=b_ref, dst_ref=slots.at[me ^ k], send_sem=send_sems.at[k - 1], recv_sem=recv_sems.at[k - 1],
                device_id=(x ^ kx, y ^ ky, c ^ kc), device_id_type=MESH)
            cp.wait_send()
            cp.wait_recv()
        acc = slots[0]
        for s in range(1, ndev):
            acc = acc + slots[s]
        o_ref[...] = acc

    vm = pl.BlockSpec(memory_space=pltpu.VMEM)
    return pl.pallas_call(
        body, name="allreduce_small", in_specs=[vm], out_specs=vm, out_shape=jax.ShapeDtypeStruct((rows, 128), F32),
        scratch_shapes=[pltpu.VMEM((ndev, rows, 128), F32), pltpu.SemaphoreType.DMA((ndev - 1,)),
                        pltpu.SemaphoreType.DMA((ndev - 1,))],
        compiler_params=pltpu.CompilerParams(vmem_limit_bytes=VMEM_LIMIT))(buf)


RS_META = [("col", D // 2, IN_PAD), ("row", D // 8, D), ("row", D // 8, D), ("row", D // 8, D),
           ("colx", D // 2, 2 * FFN // N_CHIPS), ("row", FFN // 8, D)]
SMALL_ROWS = 156


def _rope_tables(t_):
    half = SW_DIM // 2
    inv_freq = ROPE_THETA ** (-jnp.arange(half, dtype=F32) / half)
    ang = jnp.arange(t_, dtype=F32)[:, None] * inv_freq[None, :]
    reps = 128 // half
    return jnp.concatenate([jnp.cos(ang)] * reps, axis=1), jnp.concatenate([jnp.sin(ang)] * reps, axis=1)


def _orig_cols(padded, a, b):
    out = []
    for q in range(N_CHIPS):
        lo, hi = max(a, q * IN_SHARD), min(b, (q + 1) * IN_SHARD)
        if lo < hi:
            out.append(padded[:, q * IN_PAD + lo - q * IN_SHARD:q * IN_PAD + hi - q * IN_SHARD])
    return out


_ORIG_SEGMENTS = [(0, R_BG, "main", 0), (R_BG, R_SW, "bg", 0), (R_SW, R_GATES, "main", O_QS), (R_GATES, IN_COLS, "main", O_GA)]


def _to_padded_shards(main, bg):
    zeros = jnp.zeros((main.shape[0], IN_PAD - IN_SHARD), main.dtype)
    parts = []
    for q in range(N_CHIPS):
        for a, b, src, s0 in _ORIG_SEGMENTS:
            lo, hi = max(a, q * IN_SHARD), min(b, (q + 1) * IN_SHARD)
            if lo < hi:
                parts.append((main if src == "main" else bg)[:, s0 + lo - a:s0 + hi - a])
        parts.append(zeros)
    return jnp.concatenate(parts, axis=1)


def _lane_row(v16):
    return jnp.pad(v16.reshape(1, 2 * DN_HEADS), ((0, 0), (2 * DN_HEADS, 128 - 4 * DN_HEADS)))


def _pack_small(g):
    pad16 = jnp.pad(g["sinks"], ((0, 0), (0, 128 - SW_HEADS)))
    return jnp.concatenate([g["conv"].reshape(-1, 128), g["ln1g"].reshape(-1, 128), g["ln1b"].reshape(-1, 128),
                            g["ln2g"].reshape(-1, 128), g["ln2b"].reshape(-1, 128), g["gnw"], g["arow"], g["dtrow"],
                            pad16], axis=0)


def _unpack_small(buf):
    nconv = DN_CONV * 3 * D // 128
    o = nconv
    out = dict(conv=buf[:o].reshape(DN_CONV, 3 * D))
    for name in ("ln1g", "ln1b", "ln2g", "ln2b"):
        out[name] = buf[o:o + 8].reshape(D)
        o += 8
    out["gnw"] = buf[o]
    out["a_log"] = buf[o + 1, 2 * DN_HEADS:4 * DN_HEADS].reshape(2, DN_HEADS)
    out["dt_bias"] = buf[o + 2, 2 * DN_HEADS:4 * DN_HEADS].reshape(2, DN_HEADS)
    out["sinks"] = buf[o + 3, :SW_HEADS]
    return out


def kernel(x, w_in, conv_w, a_log, dt_bias, dn_norm_w, sinks, w_branch_a, w_branch_b, w_out, ln1_g, ln1_b, w_gate_up, w_down, ln2_g, ln2_b, loss_target, m_w_in, m_conv_w, m_a_log, m_dt_bias, m_dn_norm_w, m_sinks, m_w_branch_a, m_w_branch_b, m_w_out, m_ln1_g, m_ln1_b, m_w_gate_up, m_w_down, m_ln2_g, m_ln2_b, v_w_in, v_conv_w, v_a_log, v_dt_bias, v_dn_norm_w, v_sinks, v_w_branch_a, v_w_branch_b, v_w_out, v_ln1_g, v_ln1_b, v_w_gate_up, v_w_down, v_ln2_g, v_ln2_b):
    xi, yi, ci = _place()
    myq = 2 * xi + yi
    t_ = x.shape[1]
    cos, sin = _rope_tables(t_)

    loc_in = jnp.pad(w_in.astype(BF16), ((0, 0), (0, 0), (0, IN_PAD - IN_SHARD)))
    locs = [loc_in, w_branch_a.astype(BF16), w_branch_b.astype(BF16), w_out.astype(BF16), w_gate_up.astype(BF16),
            w_down.astype(BF16), conv_w]
    kinds = ["col", "row", "row", "row", "colx", "row", "col"]
    gathers = []
    for l in range(DEPTH):
        srcs = [a[l] for a in locs]
        if gathers:
            srcs[-1] = srcs[-1] + gathers[-1][1][0, 0]
        gathers.append(_gather_start(srcs, kinds, "gather_%d_start" % l))

    def in_weights(l, after):
        _, (full_in,) = _split_wait(tuple(part[:1] for part in gathers[l][0]), after, "gather_%d_wait_in" % l)
        cols = lambda a, b: _orig_cols(full_in, a, b)
        return dict(
            in_main=jnp.concatenate(cols(0, R_BG) + cols(R_GATES, IN_COLS) + cols(R_SW, R_GATES), axis=1),
            in_bg=jnp.pad(jnp.concatenate(cols(R_BG, R_SW), axis=1), ((0, 0), (0, 128 - 4 * DN_HEADS))))

    def rest_weights(l, after):
        _, (full_a, full_b, full_o, full_gu, full_d, full_conv) = _split_wait(
            tuple(part[1:] for part in gathers[l][0]), after, "gather_%d_wait_rest" % l)
        return dict(
            conv=full_conv, arow=_lane_row(a_log[l]), dtrow=_lane_row(dt_bias[l]), gnw=dn_norm_w[l][None],
            sinks=sinks[l][None], a=full_a.reshape(D, D), b=full_b.reshape(D, D), o=full_o.reshape(D, D),
            ln1g=ln1_g[l][None], ln1b=ln1_b[l][None], gu=full_gu, d=full_d.reshape(FFN, D),
            ln2g=ln2_g[l][None], ln2b=ln2_b[l][None])

    h = x[0]
    hm = h.astype(_MXU)
    residuals = []
    for l in range(DEPTH):
        win = in_weights(l, gathers[-1][1] if l == 0 else h)
        h, hm, res = _layer_fwd(h, hm, win, functools.partial(rest_weights, l), cos, sin)
        residuals.append(res)
    dh, sq = _loss_head(h, loss_target[0])
    loss = lax.psum((0.5 / D) * jnp.sum(sq), ("x", "y", "c"))

    big = [None] * DEPTH
    small = [None] * DEPTH
    pending = None
    for l in reversed(range(DEPTH)):
        dh, g = _layer_bwd(dh, residuals[l], residuals[l]["w"], cos, sin)
        if pending is not None:
            big[pending[0]] = _rs_end(pending[1], dh, RS_META, ci, myq, "rs_chips_%d" % pending[0])
        g_in = _to_padded_shards(g["in_main"], g["in_bg"])
        handle, token = _rs_begin([g_in, g["a"], g["b"], g["o"], g["gu"], g["d"]], RS_META, ci, "rs_chips_%d" % l)
        pending = (l, handle)
        dh = dh + token[0, 0]
        small[l] = _pack_small(g)
    tot = _allreduce_small(jnp.concatenate(small, axis=0))
    sm = [_unpack_small(tot[l * SMALL_ROWS:(l + 1) * SMALL_ROWS]) for l in range(DEPTH)]
    stack = lambda name: jnp.stack([s[name] for s in sm], axis=0)
    grads = dict(
        conv_w=lax.dynamic_slice_in_dim(stack("conv"), myq * (3 * D // N_CHIPS), 3 * D // N_CHIPS, axis=2),
        a_log=stack("a_log"), dt_bias=stack("dt_bias"), dn_norm_w=stack("gnw"), sinks=stack("sinks"),
        ln1_g=stack("ln1g"), ln1_b=stack("ln1b"), ln2_g=stack("ln2g"), ln2_b=stack("ln2b"))
    weights = dict(w_in=w_in, conv_w=conv_w, a_log=a_log, dt_bias=dt_bias, dn_norm_w=dn_norm_w, sinks=sinks,
                   w_branch_a=w_branch_a, w_branch_b=w_branch_b, w_out=w_out, ln1_g=ln1_g, ln1_b=ln1_b,
                   w_gate_up=w_gate_up, w_down=w_down, ln2_g=ln2_g, ln2_b=ln2_b)
    ms = dict(w_in=m_w_in, conv_w=m_conv_w, a_log=m_a_log, dt_bias=m_dt_bias, dn_norm_w=m_dn_norm_w, sinks=m_sinks,
              w_branch_a=m_w_branch_a, w_branch_b=m_w_branch_b, w_out=m_w_out, ln1_g=m_ln1_g, ln1_b=m_ln1_b,
              w_gate_up=m_w_gate_up, w_down=m_w_down, ln2_g=m_ln2_g, ln2_b=m_ln2_b)
    vs = dict(w_in=v_w_in, conv_w=v_conv_w, a_log=v_a_log, dt_bias=v_dt_bias, dn_norm_w=v_dn_norm_w, sinks=v_sinks,
              w_branch_a=v_w_branch_a, w_branch_b=v_w_branch_b, w_out=v_w_out, ln1_g=v_ln1_g, ln1_b=v_ln1_b,
              w_gate_up=v_w_gate_up, w_down=v_w_down, ln2_g=v_ln2_g, ln2_b=v_ln2_b)
    names = list(weights)
    upd = {n: _adamw("adamw_" + n, weights[n], grads[n], ms[n], vs[n]) for n in grads}
    big_names = ["w_in", "w_branch_a", "w_branch_b", "w_out", "w_gate_up", "w_down"]
    carry = {n: None for n in big_names}

    def update_layer(l):
        for t, n in enumerate(big_names):
            carry[n] = _adamw_layer("adamw_" + n, l, weights[n], big[l][t], ms[n], vs[n], carry[n])

    for l in range(DEPTH - 1, pending[0], -1):
        update_layer(l)
    big[pending[0]] = _rs_end(pending[1], carry[big_names[-1]][1], RS_META, ci, myq, "rs_chips_%d" % pending[0])
    update_layer(pending[0])
    for n in big_names:
        grads[n], upd[n] = carry[n][0], carry[n][1:]
    return (loss, dh[None], *[grads[n] for n in names], *[upd[n][0] for n in names], *[upd[n][1] for n in names],
            *[upd[n][2] for n in names])
```

```python
import functools

import jax
import jax.numpy as jnp
from jax import lax
from jax.experimental import pallas as pl
from jax.experimental.pallas import tpu as pltpu

F32 = jnp.float32
BF16 = jnp.bfloat16
_MXU = BF16

D = 1024
DEPTH = 4
DN_HEADS = 8
DN_DIM = 128
DN_CONV = 5
CHUNK = 64
SW_HEADS = 16
SW_KV = 4
SW_DIM = 64
SW_GRP = SW_HEADS // SW_KV
SW_BLOCK = 128
ROPE_THETA = 10000.0
FFN = 2816
ALPHA = (2.0 * DEPTH) ** 0.25
LN_EPS = 1e-5
RMS_EPS = 1e-6
IN_COLS = 7712
O_Z, O_GA, O_GB, O_QS, O_KS, O_VS, N_MAIN = 3072, 4096, 5120, 6144, 7168, 7424, 7680
R_BG, R_SW, R_GATES = 4096, 4128, 5664
N_CHIPS = 4
IN_SHARD = IN_COLS // N_CHIPS
IN_PAD = 2048
ADAM_LR, ADAM_B1, ADAM_B2, ADAM_EPS, ADAM_WD, ADAM_STEP = 0.001, 0.9, 0.999, 1e-08, 0.01, 10
VMEM_LIMIT = 52 * 1024 * 1024
MESH = pl.DeviceIdType.MESH
ANY = pl.BlockSpec(memory_space=pl.ANY)


def _params(n_grid, **kw):
    return pltpu.CompilerParams(dimension_semantics=("arbitrary",) * n_grid, vmem_limit_bytes=VMEM_LIMIT, **kw)


def _full(a):
    nd = a.ndim
    return pl.BlockSpec(a.shape, lambda *_, nd=nd: (0,) * nd)


def _raw_dot(a, b, ca, cb):
    return lax.dot_general(a.astype(_MXU), b.astype(_MXU), (((ca,), (cb,)), ((), ())), preferred_element_type=F32)


@jax.custom_vjp
def _nn(a, b):
    return _raw_dot(a, b, 1, 0)


@jax.custom_vjp
def _nt(a, b):
    return _raw_dot(a, b, 1, 1)


@jax.custom_vjp
def _tn(a, b):
    if a.shape[1] > b.shape[1]:
        return _raw_dot(b, a, 0, 0).T
    return _raw_dot(a, b, 0, 0)


_nn.defvjp(lambda a, b: (_nn(a, b), (a, b)), lambda r, g: (_nt(g, r[1]), _tn(r[0], g)))
_nt.defvjp(lambda a, b: (_nt(a, b), (a, b)), lambda r, g: (_nn(g, r[1]), _tn(g, r[0])))
_tn.defvjp(lambda a, b: (_tn(a, b), (a, b)), lambda r, g: (_nt(r[1], g), _nn(r[0], g)))


def _hdot(a, b, ca=1, cb=0):
    ah, bh = a.astype(BF16), b.astype(BF16)
    al, bl = (a - ah.astype(F32)).astype(BF16), (b - bh.astype(F32)).astype(BF16)
    dot = lambda u, v: lax.dot_general(u, v, (((ca,), (cb,)), ((), ())), preferred_element_type=F32)
    return dot(ah, bh) + (dot(ah, bl) + dot(al, bh))


def _inv_impl(mats):
    n = mats[0].shape[0]
    eye = (lax.broadcasted_iota(jnp.int32, (n, n), 0) == lax.broadcasted_iota(jnp.int32, (n, n), 1)).astype(F32)
    ps = [-a for a in mats]
    ts = [eye + p for p in ps]
    for _ in range(max(1, (n - 1).bit_length()) - 1):
        ps = [_hdot(p, p) for p in ps]
        ts = [t + _hdot(t, p) for t, p in zip(ts, ps)]
    return tuple(ts)


@jax.custom_vjp
def _inv(mats):
    return _inv_impl(mats)


def _inv_fwd(mats):
    ts = _inv_impl(mats)
    return ts, ts


def _inv_bwd(ts, gs):
    xs = [_hdot(t, g, 0, 0) for t, g in zip(ts, gs)]
    return (tuple(-_hdot(x, t, 1, 1) for x, t in zip(xs, ts)),)


_inv.defvjp(_inv_fwd, _inv_bwd)


@jax.custom_vjp
def _inv_saved(mats, saved):
    return saved


_inv_saved.defvjp(lambda mats, saved: (saved, saved),
                  lambda ts, gs: (_inv_bwd(ts, gs)[0], tuple(jnp.zeros_like(t) for t in ts)))


def _tile(n, cap):
    if n <= cap:
        return n
    best = [t for t in range(128, cap + 1, 128) if n % t == 0]
    assert best, (n, cap)
    return best[-1]


def _mm(a, b, *, name, ta=False, tb=False, add=None, tm=1024, tn=1024, tk=1024):
    if ta:
        k_, m_ = a.shape
    else:
        m_, k_ = a.shape
    n_ = b.shape[0] if tb else b.shape[1]
    tm, tn, tk = _tile(m_, tm), _tile(n_, tn), _tile(k_, tk)
    nk = k_ // tk
    has_add = add is not None

    def body(*refs):
        a_ref, b_ref = refs[:2]
        add_ref = refs[2] if has_add else None
        o_ref = refs[3] if has_add else refs[2]
        part = _raw_dot(a_ref[...], b_ref[...], 0 if ta else 1, 1 if tb else 0)
        if nk == 1:
            o_ref[...] = part + add_ref[...] if has_add else part
            return
        acc = refs[-1]
        k = pl.program_id(2)

        @pl.when(k == 0)
        def _():
            acc[...] = part

        @pl.when(jnp.logical_and(k > 0, k < nk - 1))
        def _():
            acc[...] += part

        @pl.when(k == nk - 1)
        def _():
            o_ref[...] = acc[...] + part + add_ref[...] if has_add else acc[...] + part

    a_spec = pl.BlockSpec((tk, tm), lambda i, j, k: (k, i)) if ta else pl.BlockSpec((tm, tk), lambda i, j, k: (i, k))
    b_spec = pl.BlockSpec((tn, tk), lambda i, j, k: (j, k)) if tb else pl.BlockSpec((tk, tn), lambda i, j, k: (k, j))
    o_spec = pl.BlockSpec((tm, tn), lambda i, j, k: (i, j))
    in_specs = [a_spec, b_spec] + ([o_spec] if has_add else [])
    args = (a, b) + ((add,) if has_add else ())
    return pl.pallas_call(
        body, name=name, grid=(m_ // tm, n_ // tn, nk), in_specs=in_specs, out_specs=o_spec,
        out_shape=jax.ShapeDtypeStruct((m_, n_), F32),
        scratch_shapes=[pltpu.VMEM((tm, tn), F32)] if nk > 1 else [],
        compiler_params=_params(3))(*args)


def _mm_fused(a, b, post, outs, *, name, rows=(), params=(), n_sums=0, tb=False, tm=1024, tn=1024, tk=1024):
    m_, k_ = a.shape
    n_ = b.shape[0] if tb else b.shape[1]
    tm, tn, tk = _tile(m_, tm), _tile(n_, tn), _tile(k_, tk)
    nk = k_ // tk
    nr, npar, no = len(rows), len(params), len(outs)
    aliased = [o[4] for o in outs if o[4] is not None]
    nin = 2 + nr + npar

    def body(*refs):
        a_ref, b_ref = refs[:2]
        row_refs, par_refs = refs[2:2 + nr], refs[2 + nr:nin]
        out_refs = refs[nin + len(aliased):nin + len(aliased) + no + n_sums]
        part = _raw_dot(a_ref[...], b_ref[...], 1, 1 if tb else 0)
        k = pl.program_id(2)
        first = jnp.logical_and(pl.program_id(0) == 0, pl.program_id(1) == 0)
        if nk > 1:
            acc = refs[-1]

            @pl.when(k == 0)
            def _():
                acc[...] = part

            @pl.when(jnp.logical_and(k > 0, k < nk - 1))
            def _():
                acc[...] += part

        @pl.when(k == nk - 1)
        def _():
            total = acc[...] + part if nk > 1 else part
            res = post(total, *[r[...].astype(F32) for r in row_refs], *[p[...] for p in par_refs])
            for o_ref, val in zip(out_refs[:no], res[:no]):
                o_ref[...] = val.astype(o_ref.dtype)
            for s_ref, val in zip(out_refs[no:], res[no:]):
                @pl.when(first)
                def _(s_ref=s_ref):
                    s_ref[...] = jnp.zeros_like(s_ref)
                s_ref[...] += val

    b_spec = pl.BlockSpec((tn, tk), lambda i, j, k: (j, k)) if tb else pl.BlockSpec((tk, tn), lambda i, j, k: (k, j))
    in_specs = [pl.BlockSpec((tm, tk), lambda i, j, k: (i, k)), b_spec]
    in_specs += [pl.BlockSpec((tm, w), lambda i, j, k, cb=cb: (i, cb(j))) for _, w, cb in rows]
    in_specs += [_full(p) for p in params] + [ANY] * len(aliased)
    out_specs = [pl.BlockSpec((tm, w), lambda i, j, k, cb=cb: (i, cb(j))) for _, w, _, cb, _ in outs]
    out_specs += [_full(p) for p in params[:n_sums]]
    out_shape = [jax.ShapeDtypeStruct((m_, tot), dt) for tot, _, dt, _, _ in outs]
    out_shape += [jax.ShapeDtypeStruct(p.shape, F32) for p in params[:n_sums]]
    aliases, pos = {}, nin
    for oi, o in enumerate(outs):
        if o[4] is not None:
            aliases[pos] = oi
            pos += 1
    return pl.pallas_call(
        body, name=name, grid=(m_ // tm, n_ // tn, nk), in_specs=in_specs, out_specs=out_specs, out_shape=out_shape,
        scratch_shapes=[pltpu.VMEM((tm, tn), F32)] if nk > 1 else [], input_output_aliases=aliases,
        compiler_params=_params(3))(a, b, *[r[0] for r in rows], *params, *aliased)


def _row_spec(tb, w, c0, percol):
    return pl.BlockSpec((tb, w), lambda i, j, c0=c0, pc=percol: (i, c0 + (j if pc else 0)))


def _stage_fwd(f, name, rows, params, outs, tb, ncol=1):
    t_ = rows[0][0].shape[0]
    nr, npar = len(rows), len(params)

    def body(*refs):
        res = f(*[r[...].astype(F32) for r in refs[:nr + npar]])
        for o_ref, val in zip(refs[nr + npar:], res):
            o_ref[...] = val.astype(o_ref.dtype)

    return pl.pallas_call(
        body, name=name, grid=(t_ // tb, ncol),
        in_specs=[_row_spec(tb, w, c0, pc) for (_, w, c0, pc) in rows] + [_full(p) for p in params],
        out_specs=[pl.BlockSpec((tb, w), lambda i, j: (i, j)) for w, _ in outs],
        out_shape=[jax.ShapeDtypeStruct((t_, w * ncol), dt) for w, dt in outs],
        compiler_params=_params(2))(*[r[0] for r in rows], *params)


def _into(dest, tb, width, t_, ncol, dtype):
    if dest is None:
        return pl.BlockSpec((tb, width), lambda i, j: (i, j)), jax.ShapeDtypeStruct((t_, width * ncol), dtype), None
    buf, total, c0 = dest
    return (pl.BlockSpec((tb, width), lambda i, j, c0=c0: (i, c0 + j)), jax.ShapeDtypeStruct((t_, total), dtype), buf)


def _stage_bwd(f, name, rows, params, douts, tb, ncol=1, cat=None, dtypes=None, dest=None):
    t_ = rows[0][0].shape[0]
    nr, npar, nd = len(rows), len(params), len(douts)
    cat = cat if cat is not None else [[r] for r in range(nr)]
    dtypes = dtypes if dtypes is not None else [F32] * len(cat)
    dest = dest or {}
    assert ncol == 1 or all(len(g) == 1 and rows[g[0]][3] for g in cat)
    nin = nr + npar + nd

    def body(*refs):
        ins = [r[...].astype(F32) for r in refs[:nr + npar]]
        dvals = tuple(r[...].astype(F32) for r in refs[nr + npar:nin])
        out_refs = refs[nin + len(aliased):]
        _, vjp = jax.vjp(f, *ins)
        grads = vjp(dvals)
        for o_ref, grp in zip(out_refs[:len(cat)], cat):
            val = grads[grp[0]] if len(grp) == 1 else jnp.concatenate([grads[r] for r in grp], axis=-1)
            o_ref[...] = val.astype(o_ref.dtype)
        first = jnp.logical_and(pl.program_id(0) == 0, pl.program_id(1) == 0)
        for p_ref, gp in zip(out_refs[len(cat):], grads[nr:]):
            @pl.when(first)
            def _(p_ref=p_ref):
                p_ref[...] = jnp.zeros_like(p_ref)
            p_ref[...] += gp

    gw = [sum(rows[r][1] for r in grp) for grp in cat]
    out_specs, out_shape, aliased, aliases = [], [], [], {}
    for gi, (w, dt) in enumerate(zip(gw, dtypes)):
        spec, shape, buf = _into(dest.get(gi), tb, w, t_, ncol, dt)
        out_specs.append(spec)
        out_shape.append(shape)
        if buf is not None:
            aliases[nin + len(aliased)] = gi
            aliased.append(buf)
    return pl.pallas_call(
        body, name=name, grid=(t_ // tb, ncol),
        in_specs=[_row_spec(tb, w, c0, pc) for (_, w, c0, pc) in rows] + [_full(p) for p in params]
        + [pl.BlockSpec((tb, d.shape[1] // ncol), lambda i, j: (i, j)) for d in douts] + [ANY] * len(aliased),
        out_specs=out_specs + [_full(p) for p in params],
        out_shape=out_shape + [jax.ShapeDtypeStruct(p.shape, F32) for p in params],
        input_output_aliases=aliases,
        compiler_params=_params(2))(*[r[0] for r in rows], *params, *douts, *aliased)


def _ln_f(x, y, g, b):
    u = ALPHA * x + y
    c = u - jnp.mean(u, axis=-1, keepdims=True)
    var = jnp.mean(c * c, axis=-1, keepdims=True)
    return (c * lax.rsqrt(var + LN_EPS) * g + b,)


def _ln_f2(x, y, g, b):
    out, = _ln_f(x, y, g, b)
    return out, out


def _swiglu_tile(gu):
    half = gu.shape[1] // 2
    return (jax.nn.silu(gu[:, :half]) * gu[:, half:],)


def _merge_f(ya, yb, ga, gb):
    return (jax.nn.sigmoid(ga) * ya + jax.nn.sigmoid(gb) * yb,)


def _gnorm_f(of, ob, z, w):
    o = of + ob
    return (o * lax.rsqrt(jnp.mean(o * o, axis=-1, keepdims=True) + RMS_EPS) * w * jax.nn.silu(z),)


def _bg_f(x, arow, dtrow):
    lane = lax.broadcasted_iota(jnp.int32, x.shape, 1)
    beta = jax.nn.sigmoid(x)
    g = -jnp.exp(arow) * jax.nn.softplus(x + dtrow)
    return (jnp.where(lane < 16, beta, jnp.where(lane < 32, g, 0.0)),)


PREP_ROWS = 512
PAD = 8


def _prep_f(part, w, *wins):
    xc = wins[0] * w[0:1, :]
    for k in range(1, DN_CONV):
        xc = xc + wins[k] * w[k:k + 1, :]
    a = jax.nn.silu(xc)
    nrm = a * lax.rsqrt(jnp.sum(a * a, axis=-1, keepdims=True) + RMS_EPS)
    return jnp.where(part == 0, nrm * (DN_DIM ** -0.5), jnp.where(part == 1, nrm, a))


def _windows(pad_ref, r0, rows):
    return [pad_ref[PAD + r0 - 2 + k:PAD + r0 - 2 + k + rows, :] for k in range(DN_CONV)]


def _prep_fwd(pm, conv):
    t_ = pm.shape[0]
    rows = min(PREP_ROWS, t_)

    def body(x_ref, w_ref, o_ref, pad_ref):
        part = pl.program_id(0) // DN_HEADS
        pad_ref[0:PAD, :] = jnp.zeros((PAD, DN_DIM), F32)
        pad_ref[PAD + t_:2 * PAD + t_, :] = jnp.zeros((PAD, DN_DIM), F32)
        pad_ref[PAD:PAD + t_, :] = x_ref[...]
        w = w_ref[...]
        for r in range(t_ // rows):
            o_ref[r * rows:(r + 1) * rows, :] = _prep_f(part, w, *_windows(pad_ref, r * rows, rows))

    ncb = 3 * DN_HEADS
    return pl.pallas_call(
        body, name="prep_fwd", grid=(ncb,),
        in_specs=[pl.BlockSpec((t_, DN_DIM), lambda j: (0, j)), pl.BlockSpec((DN_CONV, DN_DIM), lambda j: (0, j))],
        out_specs=pl.BlockSpec((t_, DN_DIM), lambda j: (0, j)),
        out_shape=jax.ShapeDtypeStruct((t_, ncb * DN_DIM), F32),
        scratch_shapes=[pltpu.VMEM((t_ + 2 * PAD, DN_DIM), F32)],
        compiler_params=_params(1))(pm, conv)


def _prep_bwd(pm, conv, dout, dpm):
    t_ = pm.shape[0]
    rows = min(PREP_ROWS, t_)

    def body(x_ref, w_ref, d_ref, _, dx_ref, dw_ref, pad_ref, dpad_ref):
        part = pl.program_id(0) // DN_HEADS
        pad_ref[0:PAD, :] = jnp.zeros((PAD, DN_DIM), F32)
        pad_ref[PAD + t_:2 * PAD + t_, :] = jnp.zeros((PAD, DN_DIM), F32)
        pad_ref[PAD:PAD + t_, :] = x_ref[...]
        dpad_ref[...] = jnp.zeros_like(dpad_ref)
        w = w_ref[...]
        dw = jnp.zeros((DN_CONV, DN_DIM), F32)
        for r in range(t_ // rows):
            r0 = r * rows
            _, vjp = jax.vjp(functools.partial(_prep_f, part), w, *_windows(pad_ref, r0, rows))
            grads = vjp(d_ref[r0:r0 + rows, :])
            dw = dw + grads[0]
            for k in range(DN_CONV):
                lo = PAD + r0 - 2 + k
                dpad_ref[lo:lo + rows, :] += grads[1 + k]
        dx_ref[...] = dpad_ref[PAD:PAD + t_, :].astype(dx_ref.dtype)
        dw_ref[...] = dw

    ncb = 3 * DN_HEADS
    col = pl.BlockSpec((t_, DN_DIM), lambda j: (0, j))
    wsp = pl.BlockSpec((DN_CONV, DN_DIM), lambda j: (0, j))
    return pl.pallas_call(
        body, name="prep_bwd", grid=(ncb,), in_specs=[col, wsp, col, ANY], out_specs=[col, wsp],
        out_shape=[jax.ShapeDtypeStruct(dpm.shape, dpm.dtype), jax.ShapeDtypeStruct((DN_CONV, ncb * DN_DIM), F32)],
        scratch_shapes=[pltpu.VMEM((t_ + 2 * PAD, DN_DIM), F32), pltpu.VMEM((t_ + 2 * PAD, DN_DIM), F32)],
        input_output_aliases={3: 0}, compiler_params=_params(1))(pm, conv, dout, dpm)


def _dn_chunk(sgns, qs, ks, vs, grows, brows, tsaved=None, with_t=False):
    c = qs[0].shape[0]
    i = lax.broadcasted_iota(jnp.int32, (c, c), 0)
    j = lax.broadcasted_iota(jnp.int32, (c, c), 1)
    eye = i == j
    incl = {s: (i - j) * int(s) >= 0 for s in set(sgns)}
    strict = {s: (i - j) * int(s) > 0 for s in set(sgns)}
    gcs = [jnp.sum(jnp.where(incl[s], g, 0.0), axis=1, keepdims=True) for s, g in zip(sgns, grows)]
    grs = [jnp.sum(jnp.where(eye, gc, 0.0), axis=0, keepdims=True) for gc in gcs]
    bcs = [jnp.sum(jnp.where(eye, b, 0.0), axis=1, keepdims=True) for b in brows]
    gls = [jnp.sum(g, axis=1, keepdims=True) for g in grows]
    decs = [jnp.exp(jnp.where(incl[s], gc - gr, -1e30)) for s, gc, gr in zip(sgns, gcs, grs)]
    kks = [_nt(k, k) for k in ks]
    amats = tuple(jnp.where(strict[s], bc * kk * dec, 0.0) for s, bc, kk, dec in zip(sgns, bcs, kks, decs))
    tinvs = _inv(amats) if tsaved is None else _inv_saved(amats, tsaved)
    egcs = [jnp.exp(gc) for gc in gcs]
    us = [_nn(t, v * bc) for t, v, bc in zip(tinvs, vs, bcs)]
    ws = [_nn(t, k * (bc * egc)) for t, k, bc, egc in zip(tinvs, ks, bcs, egcs)]
    qks = [_nt(q, k) * dec for q, k, dec in zip(qs, ks, decs)]
    qds = [q * egc for q, egc in zip(qs, egcs)]
    kds = [k * jnp.exp(gl - gc) for k, gl, gc in zip(ks, gls, gcs)]
    res = tuple(us), tuple(ws), tuple(qks), tuple(qds), tuple(kds)
    return res + (tinvs,) if with_t else res


def _dn_step(us, ws, qks, qds, kds, grows, ss):
    gls = [jnp.exp(jnp.sum(g, axis=1, keepdims=True)) for g in grows]
    wss = [_nn(w, s) for w, s in zip(ws, ss)]
    qss = [_nn(qd, s) for qd, s in zip(qds, ss)]
    vns = [u - x for u, x in zip(us, wss)]
    os_ = [a + _nn(qk, vn) for a, qk, vn in zip(qss, qks, vns)]
    s2s = [s * gl + _tn(kd, vn) for s, gl, kd, vn in zip(ss, gls, kds, vns)]
    return tuple(os_), tuple(s2s)


def _hs(h):
    return slice(h * DN_DIM, (h + 1) * DN_DIM)


_DIR_SGN = (1, -1)
_A_OUT = 5
_PROBLEMS = [(d, h) for d in range(2) for h in range(DN_HEADS)]
_SGNS = [_DIR_SGN[d] for d, _ in _PROBLEMS]


def _chunk_inputs(q_ref, k_ref, v_ref, g_ref, b_ref):
    heads = lambda ref: tuple(ref[:, _hs(h)].astype(F32) for _, h in _PROBLEMS)
    rows = lambda ref: tuple(ref[d, 0, h:h + 1, :] for d, h in _PROBLEMS)
    return heads(q_ref), heads(k_ref), heads(v_ref), rows(g_ref), rows(b_ref)


def _dn_a_fwd(qkv, grows, brows):
    t_ = qkv.shape[0]
    nch = t_ // CHUNK

    def body(q_ref, k_ref, v_ref, g_ref, b_ref, *outs):
        us, ws, qks, qds, kds, tinvs = _dn_chunk(_SGNS, *_chunk_inputs(q_ref, k_ref, v_ref, g_ref, b_ref), with_t=True)
        for p, (d, h) in enumerate(_PROBLEMS):
            u_ref, w_ref, qk_ref, qd_ref, kd_ref = outs[d * _A_OUT:(d + 1) * _A_OUT]
            u_ref[:, _hs(h)], w_ref[:, _hs(h)], qk_ref[0, h] = us[p], ws[p].astype(_MXU), qks[p].astype(_MXU)
            qd_ref[:, _hs(h)], kd_ref[:, _hs(h)] = qds[p].astype(_MXU), kds[p].astype(_MXU)
            outs[2 * _A_OUT + d][0, h] = tinvs[p]

    rspec = pl.BlockSpec((2, 1, DN_HEADS, CHUNK), lambda c: (0, c, 0, 0))
    big = pl.BlockSpec((CHUNK, D), lambda c: (c, 0))
    qks = pl.BlockSpec((1, DN_HEADS, CHUNK, CHUNK), lambda c: (c, 0, 0, 0))
    bigs = lambda dt: jax.ShapeDtypeStruct((t_, D), dt)
    qksh = lambda dt: jax.ShapeDtypeStruct((nch, DN_HEADS, CHUNK, CHUNK), dt)
    res = pl.pallas_call(
        body, name="dn_a_fwd", grid=(nch,),
        in_specs=[pl.BlockSpec((CHUNK, D), lambda c, p=p: (c, p)) for p in range(3)] + [rspec, rspec],
        out_specs=[big, big, qks, big, big] * 2 + [qks, qks],
        out_shape=[bigs(F32), bigs(_MXU), qksh(_MXU), bigs(_MXU), bigs(_MXU)] * 2 + [qksh(F32)] * 2,
        compiler_params=_params(1))(qkv, qkv, qkv, grows, brows)
    return res[:2 * _A_OUT], res[2 * _A_OUT:]


def _dn_a_bwd(qkv, grows, brows, tinv, dres, dg_b):
    t_ = qkv.shape[0]
    nch = t_ // CHUNK

    def body(q_ref, k_ref, v_ref, g_ref, b_ref, tf_ref, tb_ref, *rest):
        dins, dgb_ref, (dqkv_ref, dg_ref, db_ref) = rest[:2 * _A_OUT], rest[2 * _A_OUT], rest[2 * _A_OUT + 1:]
        tsaved = tuple((tf_ref, tb_ref)[d][0, h] for d, h in _PROBLEMS)
        _, vjp = jax.vjp(functools.partial(_dn_chunk, _SGNS, tsaved=tsaved),
                         *_chunk_inputs(q_ref, k_ref, v_ref, g_ref, b_ref))
        cots = []
        for o in range(_A_OUT):
            cots.append(tuple(dins[d * _A_OUT + o][0, h] if o == 2 else dins[d * _A_OUT + o][:, _hs(h)]
                              for d, h in _PROBLEMS))
        gq, gk, gv, gg, gb = vjp(tuple(cots))
        for p, (d, h) in enumerate(_PROBLEMS):
            dg_ref[d, 0, h:h + 1, :] = gg[p] + dgb_ref[d, 0, h:h + 1, :]
            db_ref[d, 0, h:h + 1, :] = gb[p]
        for h in range(DN_HEADS):
            dqkv_ref[:, _hs(h)] = gq[h] + gq[DN_HEADS + h]
            dqkv_ref[:, _hs(DN_HEADS + h)] = gk[h] + gk[DN_HEADS + h]
            dqkv_ref[:, _hs(2 * DN_HEADS + h)] = gv[h] + gv[DN_HEADS + h]

    rspec = pl.BlockSpec((2, 1, DN_HEADS, CHUNK), lambda c: (0, c, 0, 0))
    big = pl.BlockSpec((CHUNK, D), lambda c: (c, 0))
    qks = pl.BlockSpec((1, DN_HEADS, CHUNK, CHUNK), lambda c: (c, 0, 0, 0))
    rsh = jax.ShapeDtypeStruct(grows.shape, F32)
    return pl.pallas_call(
        body, name="dn_a_bwd", grid=(nch,),
        in_specs=[pl.BlockSpec((CHUNK, D), lambda c, p=p: (c, p)) for p in range(3)] + [rspec, rspec, qks, qks]
        + [big, big, qks, big, big] * 2 + [rspec],
        out_specs=[pl.BlockSpec((CHUNK, 3 * D), lambda c: (c, 0)), rspec, rspec],
        out_shape=[jax.ShapeDtypeStruct((t_, 3 * D), F32), rsh, rsh],
        compiler_params=_params(1))(qkv, qkv, qkv, grows, brows, *tinv, *dres, dg_b)


def _dir_specs(nch):
    def cidx(d):
        return (lambda n: n) if d == 0 else (lambda n: nch - 1 - n)
    out = []
    for d in range(2):
        ci = cidx(d)
        big = pl.BlockSpec((CHUNK, D), lambda n, ci=ci: (ci(n), 0))
        qks = pl.BlockSpec((1, DN_HEADS, CHUNK, CHUNK), lambda n, ci=ci: (ci(n), 0, 0, 0))
        row = pl.BlockSpec((1, 1, DN_HEADS, CHUNK), lambda n, ci=ci, d=d: (d, ci(n), 0, 0))
        st = pl.BlockSpec((1, DN_HEADS, DN_DIM, DN_DIM), lambda n, ci=ci: (ci(n), 0, 0, 0))
        out.append(dict(big=big, qk=qks, row=row, st=st))
    return out


def _step_inputs(ins, per_dir):
    def pick(o):
        if o == 2:
            return tuple(ins[d * per_dir + o][0, h].astype(F32) for d, h in _PROBLEMS)
        if o == 5:
            return tuple(ins[d * per_dir + o][0, 0, h:h + 1, :] for d, h in _PROBLEMS)
        return tuple(ins[d * per_dir + o][:, _hs(h)].astype(F32) for d, h in _PROBLEMS)
    return [pick(o) for o in range(6)]


def _dn_b_fwd(ares, grows):
    t_ = ares[0].shape[0]
    nch = t_ // CHUNK
    sp = _dir_specs(nch)

    def body(*refs):
        ins, outs, s_ref = refs[:12], refs[12:16], refs[16]

        @pl.when(pl.program_id(0) == 0)
        def _():
            s_ref[...] = jnp.zeros_like(s_ref)

        ss = tuple(s_ref[p] for p in range(len(_PROBLEMS)))
        os_, s2s = _dn_step(*_step_inputs(ins, 6), ss)
        for p, (d, h) in enumerate(_PROBLEMS):
            outs[2 + d][0, h] = ss[p]
            outs[d][:, _hs(h)] = os_[p]
            s_ref[p] = s2s[p]

    in_specs, args = [], []
    for d in range(2):
        in_specs += [sp[d]["big"], sp[d]["big"], sp[d]["qk"], sp[d]["big"], sp[d]["big"], sp[d]["row"]]
        args += list(ares[d * _A_OUT:(d + 1) * _A_OUT]) + [grows]
    stsh = jax.ShapeDtypeStruct((nch, DN_HEADS, DN_DIM, DN_DIM), F32)
    osh = jax.ShapeDtypeStruct((t_, D), F32)
    return pl.pallas_call(
        body, name="dn_b_fwd", grid=(nch,), in_specs=in_specs,
        out_specs=[sp[0]["big"], sp[1]["big"], sp[0]["st"], sp[1]["st"]], out_shape=[osh, osh, stsh, stsh],
        scratch_shapes=[pltpu.VMEM((2 * DN_HEADS, DN_DIM, DN_DIM), F32)],
        compiler_params=_params(1))(*args)


def _dn_b_bwd(ares, grows, st_f, st_b, do):
    t_ = ares[0].shape[0]
    nch = t_ // CHUNK
    sp = _dir_specs(nch)
    rsp = [sp[1], sp[0]]

    def body(*refs):
        ins, outs, ds_ref = refs[:16], refs[16:28], refs[28]

        @pl.when(pl.program_id(0) == 0)
        def _():
            ds_ref[...] = jnp.zeros_like(ds_ref)

        ss = tuple(ins[d * 8 + 6][0, h] for d, h in _PROBLEMS)
        _, vjp = jax.vjp(_dn_step, *_step_inputs(ins, 8), ss)
        dos = tuple(ins[d * 8 + 7][:, _hs(h)] for d, h in _PROBLEMS)
        grads = vjp((dos, tuple(ds_ref[p] for p in range(len(_PROBLEMS)))))
        for p, (d, h) in enumerate(_PROBLEMS):
            du_ref, dw_ref, dqk_ref, dqd_ref, dkd_ref, dg_ref = outs[d * 6:(d + 1) * 6]
            du_ref[:, _hs(h)], dw_ref[:, _hs(h)], dqk_ref[0, h] = grads[0][p], grads[1][p], grads[2][p]
            dqd_ref[:, _hs(h)], dkd_ref[:, _hs(h)] = grads[3][p], grads[4][p]
            dg_ref[0, 0, h:h + 1, :] = grads[5][p]
            ds_ref[p] = grads[6][p]

    in_specs, args, out_specs, out_shape = [], [], [], []
    big_sh = jax.ShapeDtypeStruct((t_, D), F32)
    qk_sh = jax.ShapeDtypeStruct((nch, DN_HEADS, CHUNK, CHUNK), F32)
    row_sh = jax.ShapeDtypeStruct((1, nch, DN_HEADS, CHUNK), F32)
    for d in range(2):
        s = rsp[d]
        row0 = pl.BlockSpec((1, 1, DN_HEADS, CHUNK), lambda m, d=d: (0, (nch - 1 - m) if d == 0 else m, 0, 0))
        rowd = pl.BlockSpec((1, 1, DN_HEADS, CHUNK), lambda m, d=d: (d, (nch - 1 - m) if d == 0 else m, 0, 0))
        in_specs += [s["big"], s["big"], s["qk"], s["big"], s["big"], rowd, s["st"], s["big"]]
        args += list(ares[d * _A_OUT:(d + 1) * _A_OUT]) + [grows, (st_f, st_b)[d], do]
        out_specs += [s["big"], s["big"], s["qk"], s["big"], s["big"], row0]
        out_shape += [big_sh, big_sh, qk_sh, big_sh, big_sh, row_sh]
    res = pl.pallas_call(
        body, name="dn_b_bwd", grid=(nch,), in_specs=in_specs, out_specs=out_specs, out_shape=out_shape,
        scratch_shapes=[pltpu.VMEM((2 * DN_HEADS, DN_DIM, DN_DIM), F32)],
        compiler_params=_params(1))(*args)
    dares = list(res[0:5]) + list(res[6:11])
    return dares, jnp.concatenate([res[5], res[11]], axis=0)


@jax.custom_vjp
def _rot_half(x):
    half, width = SW_DIM // 2, x.shape[1]
    first = lax.broadcasted_iota(jnp.int32, x.shape, 1) % SW_DIM < half
    return jnp.where(first, -pltpu.roll(x, width - half, axis=1), pltpu.roll(x, half, axis=1))


_rot_half.defvjp(lambda x: (_rot_half(x), None), lambda _, g: (-_rot_half(g),))


def _rope(x, c, s):
    reps = x.shape[1] // c.shape[1]
    return x * jnp.tile(c, (1, reps)) + _rot_half(x) * jnp.tile(s, (1, reps))


def _rope_t(g, c, s):
    reps = g.shape[1] // c.shape[1]
    return g * jnp.tile(c, (1, reps)) - _rot_half(g * jnp.tile(s, (1, reps)))


_SW_SCALE = SW_DIM ** -0.5
_KV_HEADS = [[kvh * SW_GRP + g for g in range(SW_GRP)] for kvh in range(SW_KV)]


def _by_group(x):
    return [jnp.concatenate([x[:, h * SW_DIM:(h + 1) * SW_DIM] for h in hs], axis=0) for hs in _KV_HEADS]


def _from_groups(xs):
    return jnp.concatenate([x[g * SW_BLOCK:(g + 1) * SW_BLOCK] for x in xs for g in range(SW_GRP)], axis=-1)


def _attn_probs(blk, t_, cq, sq, ck, sk, q, kall, sinks):
    qgs = _by_group(_rope(q, cq, sq))
    kr = _rope(kall, ck, sk)
    khs = [kr[:, kvh * SW_DIM:(kvh + 1) * SW_DIM] for kvh in range(SW_KV)]
    nq, nk = SW_GRP * SW_BLOCK, 3 * SW_BLOCK
    qpos = lax.broadcasted_iota(jnp.int32, (nq, nk), 0) % SW_BLOCK
    krel = lax.broadcasted_iota(jnp.int32, (nq, nk), 1) - SW_BLOCK
    kglob = krel + blk * SW_BLOCK
    valid = (jnp.abs(qpos - krel) <= SW_BLOCK) & (kglob >= 0) & (kglob < t_)
    ss = [jnp.where(valid, _nt(qg * _SW_SCALE, kh), -1e30) for qg, kh in zip(qgs, khs)]
    snks = [jnp.concatenate([jnp.broadcast_to(sinks[:, h:h + 1], (SW_BLOCK, 1)) for h in hs], axis=0) for hs in _KV_HEADS]
    ms = [jnp.maximum(jnp.max(s, axis=-1, keepdims=True), snk) for s, snk in zip(ss, snks)]
    es = [jnp.exp(s - m) for s, m in zip(ss, ms)]
    esnks = [jnp.exp(snk - m) for snk, m in zip(snks, ms)]
    invs = [1.0 / (jnp.sum(e, axis=-1, keepdims=True) + esnk) for e, esnk in zip(es, esnks)]
    ps = [e * inv for e, inv in zip(es, invs)]
    return qgs, khs, ps, [esnk * inv for esnk, inv in zip(esnks, invs)]


def _attn_f(blk, t_, cq, sq, ck, sk, q, kp, ko, kn, vp, vo, vn, sinks):
    _, _, ps, _ = _attn_probs(blk, t_, cq, sq, ck, sk, q, jnp.concatenate([kp, ko, kn], axis=0), sinks)
    vall = jnp.concatenate([vp, vo, vn], axis=0)
    return _from_groups([_nn(p, vall[:, kvh * SW_DIM:(kvh + 1) * SW_DIM]) for kvh, p in enumerate(ps)])


def _attn_bwd_f(blk, t_, cq, sq, ck, sk, q, kp, ko, kn, vp, vo, vn, sinks, do):
    qgs, khs, ps, psinks = _attn_probs(blk, t_, cq, sq, ck, sk, q, jnp.concatenate([kp, ko, kn], axis=0), sinks)
    vall = jnp.concatenate([vp, vo, vn], axis=0)
    vhs = [vall[:, kvh * SW_DIM:(kvh + 1) * SW_DIM] for kvh in range(SW_KV)]
    dogs = _by_group(do)
    dvs = [_tn(p, dog) for p, dog in zip(ps, dogs)]
    dps = [_nt(dog, vh) for dog, vh in zip(dogs, vhs)]
    deltas = [jnp.sum(p * dp, axis=-1, keepdims=True) for p, dp in zip(ps, dps)]
    dss = [p * ((dp - delta) * _SW_SCALE) for p, dp, delta in zip(ps, dps, deltas)]
    dqr = _from_groups([_nn(ds, kh) for ds, kh in zip(dss, khs)])
    dkr = jnp.concatenate([_tn(ds, qg) for ds, qg in zip(dss, qgs)], axis=-1)
    dsnk = [-(psink * delta) for psink, delta in zip(psinks, deltas)]
    dsinks = jnp.concatenate([jnp.sum(d[g * SW_BLOCK:(g + 1) * SW_BLOCK], axis=0, keepdims=True)
                              for d in dsnk for g in range(SW_GRP)], axis=1)
    dq, dk, dv = _rope_t(dqr, cq, sq), _rope_t(dkr, ck, sk), jnp.concatenate(dvs, axis=-1)
    blocks = lambda a: [a[j * SW_BLOCK:(j + 1) * SW_BLOCK] for j in range(3)]
    return [dq] + blocks(dk) + blocks(dv) + [dsinks]


def _attn_specs(nb):
    prv = lambda i: jnp.maximum(i - 1, 0)
    nxt = lambda i: jnp.minimum(i + 1, nb - 1)
    rows = [lambda i: i, prv, lambda i: i, nxt]
    tab = [pl.BlockSpec((SW_BLOCK, 128), lambda i, r=r: (r(i), 0)) for r in rows]
    qs = pl.BlockSpec((SW_BLOCK, SW_HEADS * SW_DIM), lambda i: (i, O_QS // (SW_HEADS * SW_DIM)))
    kw = SW_KV * SW_DIM
    ks = [pl.BlockSpec((SW_BLOCK, kw), lambda i, r=r: (r(i), O_KS // kw)) for r in rows[1:]]
    vs = [pl.BlockSpec((SW_BLOCK, kw), lambda i, r=r: (r(i), O_VS // kw)) for r in rows[1:]]
    return tab, qs, ks, vs


def _attn_tables(refs):
    cq, cp, co, cn, sq, sp_, so, sn = [r[...] for r in refs]
    return cq, sq, jnp.concatenate([cp, co, cn], axis=0), jnp.concatenate([sp_, so, sn], axis=0)


def _attn_fwd(pm, cos, sin, sinks):
    t_ = pm.shape[0]
    nb = t_ // SW_BLOCK
    tab, qs, ks, vs = _attn_specs(nb)

    def body(*refs):
        tabs = _attn_tables(refs[:8])
        vals = [r[...] for r in refs[8:16]]
        refs[16][...] = _attn_f(pl.program_id(0), t_, *tabs, *vals).astype(refs[16].dtype)

    return pl.pallas_call(
        body, name="attn_fwd", grid=(nb,), in_specs=tab + tab + [qs] + ks + vs + [_full(sinks)],
        out_specs=pl.BlockSpec((SW_BLOCK, D), lambda i: (i, 0)), out_shape=jax.ShapeDtypeStruct((t_, D), _MXU),
        compiler_params=_params(1))(*([cos] * 4), *([sin] * 4), pm, pm, pm, pm, pm, pm, pm, sinks)


def _attn_bwd(pm, cos, sin, sinks, do, dpm):
    t_ = pm.shape[0]
    nb = t_ // SW_BLOCK
    tab, qs, ks, vs = _attn_specs(nb)
    kw = SW_KV * SW_DIM

    def body(*refs):
        tabs = _attn_tables(refs[:8])
        vals = [r[...] for r in refs[8:16]]
        do_ref, outs = refs[16], refs[18:]
        grads = _attn_bwd_f(pl.program_id(0), t_, *tabs, *vals, do_ref[...])
        for o_ref, g in zip(outs[:7], grads[:7]):
            o_ref[...] = g.astype(o_ref.dtype)

        @pl.when(pl.program_id(0) == 0)
        def _():
            outs[7][...] = jnp.zeros_like(outs[7])
        outs[7][...] += grads[7]

    own = lambda w: pl.BlockSpec((SW_BLOCK, w), lambda i: (i, 0))
    return pl.pallas_call(
        body, name="attn_bwd", grid=(nb,),
        in_specs=tab + tab + [qs] + ks + vs + [_full(sinks), own(D), ANY],
        out_specs=[pl.BlockSpec((SW_BLOCK, D), lambda i: (i, O_QS // D))] + [own(kw)] * 6 + [_full(sinks)],
        out_shape=[jax.ShapeDtypeStruct(dpm.shape, dpm.dtype)] + [jax.ShapeDtypeStruct((t_, kw), F32)] * 6
        + [jax.ShapeDtypeStruct(sinks.shape, F32)],
        input_output_aliases={17: 0},
        compiler_params=_params(1))(*([cos] * 4), *([sin] * 4), pm, pm, pm, pm, pm, pm, pm, sinks, do, dpm)


def _band_sum(kparts, vparts, dpm):
    t_, kw = kparts[1].shape
    nb = t_ // SW_BLOCK

    def body(kp, ko, kn, vp, vo, vn, _, out_ref):
        j = pl.program_id(0)
        band = lambda p, o, n: o[...] + jnp.where(j + 1 < nb, p[...], 0.0) + jnp.where(j > 0, n[...], 0.0)
        out_ref[...] = jnp.concatenate([band(kp, ko, kn), band(vp, vo, vn)], axis=-1).astype(out_ref.dtype)

    specs = [pl.BlockSpec((SW_BLOCK, kw), lambda j: (jnp.minimum(j + 1, nb - 1), 0)),
             pl.BlockSpec((SW_BLOCK, kw), lambda j: (j, 0)),
             pl.BlockSpec((SW_BLOCK, kw), lambda j: (jnp.maximum(j - 1, 0), 0))]
    return pl.pallas_call(
        body, name="band_sum", grid=(nb,), in_specs=specs * 2 + [ANY],
        out_specs=pl.BlockSpec((SW_BLOCK, 2 * kw), lambda j: (j, O_KS // (2 * kw))),
        out_shape=jax.ShapeDtypeStruct(dpm.shape, dpm.dtype), input_output_aliases={6: 0},
        compiler_params=_params(1))(*kparts, *vparts, dpm)


def _loss_head(y, target, tb=256):
    t_ = y.shape[0]

    def body(y_ref, t_ref, dy_ref, acc_ref):
        err = y_ref[...] - t_ref[...]
        dy_ref[...] = err * (1.0 / D)
        sq = (err * err).reshape(tb // 8, 8, D).sum(axis=0)
        part = sq[:, 0:128]
        for c in range(1, D // 128):
            part = part + sq[:, c * 128:(c + 1) * 128]

        @pl.when(pl.program_id(0) == 0)
        def _():
            acc_ref[...] = jnp.zeros_like(acc_ref)
        acc_ref[...] += part

    row = pl.BlockSpec((tb, D), lambda i: (i, 0))
    return pl.pallas_call(
        body, name="loss_head", grid=(t_ // tb,), in_specs=[row, row],
        out_specs=[row, pl.BlockSpec((8, 128), lambda i: (0, 0))],
        out_shape=[jax.ShapeDtypeStruct((t_, D), F32), jax.ShapeDtypeStruct((8, 128), F32)],
        compiler_params=_params(1))(y, target)


def _adam_math(w, g, m, v):
    bc1 = 1.0 - ADAM_B1 ** ADAM_STEP
    bc2 = 1.0 - ADAM_B2 ** ADAM_STEP
    nm = ADAM_B1 * m + (1.0 - ADAM_B1) * g
    nv = ADAM_B2 * v + (1.0 - ADAM_B2) * (g * g)
    return -ADAM_LR * ((nm / bc1) / (jnp.sqrt(nv / bc2) + ADAM_EPS) + ADAM_WD * w), nm, nv


def _adamw_layer(name, l, w, g, m, v, carry, after):
    nl, r, c = w.shape
    g2 = g.reshape(r, -1)
    gc = g2.shape[1]
    w2, m2, v2 = [a.reshape(nl * r, c) for a in (w, m, v)]
    tb = r
    while tb * gc * 4 > (1 << 20) and tb % 16 == 0:
        tb //= 2
    nb = r // tb

    def body(w_ref, g_ref, m_ref, v_ref, *rest):
        go_ref, d_ref, nm_ref, nv_ref = rest[-4:]
        gv = g_ref[...][:, :c]
        go_ref[...] = gv
        d_ref[...], nm_ref[...], nv_ref[...] = _adam_math(w_ref[...], gv, m_ref[...], v_ref[...])

    spec = pl.BlockSpec((tb, c), lambda i: (l * nb + i, 0))
    carried = list(carry) if carry is not None else []
    outs = pl.pallas_call(
        body, name=name, grid=(nb,),
        in_specs=[spec, pl.BlockSpec((tb, gc), lambda i: (i, 0)), spec, spec] + [ANY] * (len(carried) + 1),
        out_specs=[spec] * 4, out_shape=[jax.ShapeDtypeStruct((nl * r, c), F32)] * 4,
        input_output_aliases={4 + k: k for k in range(len(carried))},
        compiler_params=_params(1))(w2, g2, m2, v2, *[a.reshape(nl * r, c) for a in carried], after)
    return tuple(o.reshape(w.shape) for o in outs)


def _adamw(name, w, g, m, v):
    shape = w.shape
    cols = shape[-1]
    rows = w.size // cols
    w2, g2, m2, v2 = [a.reshape(rows, cols) for a in (w, g, m, v)]
    tb = rows
    while tb * cols * 4 > (1 << 20) and tb % 16 == 0:
        tb //= 2

    def body(w_ref, g_ref, m_ref, v_ref, d_ref, nm_ref, nv_ref):
        d_ref[...], nm_ref[...], nv_ref[...] = _adam_math(w_ref[...], g_ref[...], m_ref[...], v_ref[...])

    spec = pl.BlockSpec((tb, cols), lambda i: (i, 0))
    sh = jax.ShapeDtypeStruct((rows, cols), F32)
    outs = pl.pallas_call(body, name=name, grid=(rows // tb,), in_specs=[spec] * 4, out_specs=[spec] * 3,
                          out_shape=[sh] * 3, compiler_params=_params(1))(w2, g2, m2, v2)
    return [o.reshape(shape) for o in outs]


def _to_rows(bg, col0):
    t_ = bg.shape[0]
    a = bg[:, col0:col0 + 2 * DN_HEADS].reshape(t_ // CHUNK, CHUNK, 2, DN_HEADS)
    return jnp.transpose(a, (2, 0, 3, 1))


def _from_rows(db, dg):
    nch = db.shape[1]
    back = lambda a: jnp.transpose(a, (1, 3, 0, 2)).reshape(nch * CHUNK, 2 * DN_HEADS)
    return jnp.pad(jnp.concatenate([back(db), back(dg)], axis=1), ((0, 0), (0, 128 - 4 * DN_HEADS)))


def _layer_fwd(x, xm, w, rest, cos, sin):
    t_ = x.shape[0]
    tb, nb = min(256, t_), min(1024, t_)
    pm = _mm(xm, w["in_main"], name="mm_in", tn=1536)
    pbg = _mm(xm, w["in_bg"], name="mm_in_bg")
    w = {**w, **rest(pbg)}
    qkv = _prep_fwd(pm, w["conv"])
    bg, = _stage_fwd(_bg_f, "bg_fwd", [(pbg, 128, 0, False)], [w["arow"], w["dtrow"]], [(128, F32)], nb)
    brows, grows = _to_rows(bg, 0), _to_rows(bg, 2 * DN_HEADS)
    ares, tinv = _dn_a_fwd(qkv, grows, brows)
    o_f, o_b, st_f, st_b = _dn_b_fwd(ares, grows)
    odn, = _stage_fwd(_gnorm_f, "gnorm_fwd", [(o_f, 128, 0, True), (o_b, 128, 0, True), (pm, 128, O_Z // 128, True)],
                      [w["gnw"]], [(128, _MXU)], nb, ncol=DN_HEADS)
    osw = _attn_fwd(pm, cos, sin, w["sinks"])
    same, first = (lambda j: j), (lambda j: 0)
    full = lambda dt: (D, D, dt, first, None)
    ya = _mm(odn, w["a"], name="mm_a")
    yb, merged = _mm_fused(
        osw, w["b"], lambda acc, ya_, ga, gb: (acc,) + _merge_f(ya_, acc, ga, gb), [full(F32), full(_MXU)],
        name="mm_b_merge", rows=[(ya, D, first), (pm, D, lambda j: O_GA // D), (pm, D, lambda j: O_GB // D)], tm=512)
    mix, x1, x1m = _mm_fused(
        merged, w["o"], lambda acc, x_, g_, b_: (acc,) + _ln_f2(x_, acc, g_, b_), [full(F32), full(F32), full(_MXU)],
        name="mm_o_ln", rows=[(x, D, first)], params=[w["ln1g"], w["ln1b"]], tm=512)
    gu, hid = _mm_fused(x1m, w["gu"], lambda acc: (acc,) + _swiglu_tile(acc),
                        [(2 * FFN, FFN, F32, same, None), (FFN, FFN // 2, _MXU, same, None)], name="mm_gu_swiglu", tn=FFN)
    ffn, x2, x2m = _mm_fused(
        hid, w["d"], lambda acc, x_, g_, b_: (acc,) + _ln_f2(x_, acc, g_, b_), [full(F32), full(F32), full(_MXU)],
        name="mm_d_ln", rows=[(x1, D, first)], params=[w["ln2g"], w["ln2b"]], tm=512, tk=FFN)
    res = dict(w=w, x=x, xm=xm, pm=pm, pbg=pbg, qkv=qkv, grows=grows, brows=brows, ares=ares, tinv=tinv, o_f=o_f, o_b=o_b, st_f=st_f,
               st_b=st_b, odn=odn, osw=osw, ya=ya, yb=yb, merged=merged, mix=mix, x1=x1, x1m=x1m, gu=gu, hid=hid,
               ffn=ffn)
    return x2, x2m, res


def _layer_bwd(dx2, r, w, cos, sin, mid=None):
    t_ = dx2.shape[0]
    tb, nb = min(256, t_), min(1024, t_)
    pm = r["pm"]
    g = {}
    dx1a, dffn, g["ln2g"], g["ln2b"] = _stage_bwd(
        _ln_f, "ln2_bwd", [(r["x1"], D, 0, False), (r["ffn"], D, 0, False)], [w["ln2g"], w["ln2b"]], [dx2], tb,
        dtypes=[F32, _MXU])
    same = lambda j: j
    dgu, = _mm_fused(dffn, w["d"], lambda dhid, gu: jax.vjp(_swiglu_tile, gu)[1]((dhid,)),
                     [(2 * FFN, FFN, _MXU, same, None)], name="mm_d_dx_swiglu", rows=[(r["gu"], FFN, same)], tb=True,
                     tn=FFN // 2)
    g["d"] = _mm(r["hid"], dffn, ta=True, name="mm_d_dw", tm=FFN // 2)
    first = lambda j: 0
    full = lambda dt: (D, D, dt, first, None)

    def ln1_back(acc, dx1a_, x_, mix_, g_, b_):
        dx_, dmix_, dg_, db_ = jax.vjp(_ln_f, x_, mix_, g_, b_)[1]((acc + dx1a_,))
        return dx_, dmix_, dg_, db_

    dxa, dmix, g["ln1g"], g["ln1b"] = _mm_fused(
        dgu, w["gu"], ln1_back, [full(F32), full(_MXU)], name="mm_gu_dx_ln", tb=True, tm=256, tk=2 * FFN, n_sums=2,
        rows=[(dx1a, D, first), (r["x"], D, first), (r["mix"], D, first)], params=[w["ln1g"], w["ln1b"]])
    g["gu"] = _mm(r["x1m"], dgu, ta=True, name="mm_gu_dw", tn=FFN // 2)
    tok = mid(g["gu"]) if mid is not None else None
    if tok is not None:
        dmix = dmix + tok[0, 0].astype(dmix.dtype)
    g["o"] = _mm(r["merged"], dmix, ta=True, name="mm_o_dw")

    def merge_back(acc, ya_, yb_, ga, gb):
        dya_, dyb_, dga, dgb = jax.vjp(_merge_f, ya_, yb_, ga, gb)[1]((acc,))
        return dya_, dyb_, jnp.concatenate([dga, dgb], axis=-1)

    dya, dyb, dpm = _mm_fused(
        dmix, w["o"], merge_back, [full(_MXU), full(_MXU), (N_MAIN, 2 * D, _MXU, lambda j: O_GA // (2 * D), None)],
        name="mm_o_dx_merge", tb=True, tm=512,
        rows=[(r["ya"], D, first), (r["yb"], D, first), (pm, D, lambda j: O_GA // D), (pm, D, lambda j: O_GB // D)])
    dodn = _mm(dya, w["a"], tb=True, name="mm_a_dx")
    g["a"] = _mm(r["odn"], dya, ta=True, name="mm_a_dw")
    dosw = _mm(dyb, w["b"], tb=True, name="mm_b_dx")
    g["b"] = _mm(r["osw"], dyb, ta=True, name="mm_b_dw")
    ab = _attn_bwd(pm, cos, sin, w["sinks"], dosw, dpm)
    dpm, g["sinks"] = ab[0], ab[7]
    dpm = _band_sum(ab[1:4], ab[4:7], dpm)
    dof, _, dpm, g["gnw"] = _stage_bwd(
        _gnorm_f, "gnorm_bwd", [(r["o_f"], 128, 0, True), (r["o_b"], 128, 0, True), (pm, 128, O_Z // 128, True)],
        [w["gnw"]], [dodn], nb, ncol=DN_HEADS, dtypes=[F32, F32, _MXU], dest={2: (dpm, N_MAIN, O_Z // 128)})
    dares, dg_b = _dn_b_bwd(r["ares"], r["grows"], r["st_f"], r["st_b"], dof)
    dqkv, dgrows, dbrows = _dn_a_bwd(r["qkv"], r["grows"], r["brows"], r["tinv"], dares, dg_b)
    dbg = _from_rows(dbrows, dgrows)
    dpbg, g["arow"], g["dtrow"] = _stage_bwd(_bg_f, "bg_bwd", [(r["pbg"], 128, 0, False)], [w["arow"], w["dtrow"]],
                                             [dbg], nb)
    dpm, g["conv"] = _prep_bwd(pm, w["conv"], dqkv, dpm)
    dx = _mm(dpm, w["in_main"], tb=True, add=dxa, name="mm_in_dx", tk=1920)
    dx = _mm(dpbg, w["in_bg"], tb=True, add=dx, name="mm_in_bg_dx")
    g["in_main"] = _mm(r["xm"], dpm, ta=True, name="mm_in_dw", tn=1536)
    g["in_bg"] = _mm(r["xm"], dpbg, ta=True, name="mm_in_bg_dw")
    return dx, g


def _place():
    return lax.axis_index("x"), lax.axis_index("y"), lax.axis_index("c")


def _colblock(kind, q):
    return q if kind == "col" else (q >> 1) | ((q & 1) << 1)


def _other_chips(x, y):
    return [(1 - x, y), (x, 1 - y), (1 - x, 1 - y)]


HBM = pl.BlockSpec(memory_space=pltpu.HBM)
SEM = pl.BlockSpec(memory_space=pltpu.SEMAPHORE)
DATAFLOW = pltpu.SideEffectType.DATAFLOW_SIDE_EFFECTING


def _hbm(a):
    return pltpu.HBM(a.shape, a.dtype)


def _gather_start(locs, kinds, name):
    n = len(locs)
    locs = list(locs)
    lands = [lax.empty((N_CHIPS,) + a.shape if kind == "row" else (a.shape[0], N_CHIPS * a.shape[1]), a.dtype)
             for a, kind in zip(locs, kinds)]

    def body(*refs):
        loc_refs, land_refs = refs[:n], refs[n:2 * n]
        send_sems, recv_sems, token = refs[2 * n:3 * n], refs[3 * n:4 * n], refs[-1]
        x, y, c = _place()
        myq = 2 * x + y
        for t in range(n):
            width = locs[t].shape[1]
            mine = (land_refs[t].at[myq] if kinds[t] == "row" else
                    land_refs[t].at[:, pl.ds(pl.multiple_of(_colblock(kinds[t], myq) * width, 128), width)])
            for dev in [(cx, cy, c) for cx, cy in _other_chips(x, y)] + [(x, y, 1 - c)]:
                pltpu.make_async_remote_copy(src_ref=loc_refs[t], dst_ref=mine, send_sem=send_sems[t],
                                             recv_sem=recv_sems[t], device_id=dev, device_id_type=MESH).start()
        token[...] = jnp.zeros_like(token)

    res = pl.pallas_call(
        body, name=name,
        out_shape=[pltpu.SemaphoreType.DMA(())] * (2 * n) + [_hbm(a) for a in locs + lands]
        + [jax.ShapeDtypeStruct((8, 128), F32)],
        in_specs=[HBM] * (2 * n), out_specs=[SEM] * (2 * n) + [HBM] * (2 * n) + [pl.BlockSpec(memory_space=pltpu.VMEM)],
        input_output_aliases={t: 2 * n + t for t in range(2 * n)},
        compiler_params=pltpu.CompilerParams(has_side_effects=DATAFLOW),
    )(*[pltpu.with_memory_space_constraint(a, pltpu.HBM) for a in locs + lands])
    return (res[:n], res[n:2 * n], res[2 * n:3 * n], res[3 * n:4 * n]), res[-1]


def _split_wait(handle, after, name):
    send_sems, recv_sems, srcs, lands = handle
    n = len(srcs)

    def body(*refs):
        land_refs, ssems, rsems = refs[n:2 * n], refs[2 * n:3 * n], refs[3 * n:4 * n]
        x, y, c = _place()
        for t in range(n):
            done = pltpu.make_async_remote_copy(
                src_ref=land_refs[t], dst_ref=land_refs[t], send_sem=ssems[t], recv_sem=rsems[t],
                device_id=(x, y, c), device_id_type=MESH)
            done.wait_send()
            done.wait_recv()

    res = pl.pallas_call(
        body, name=name, out_shape=[_hbm(a) for a in list(srcs) + list(lands)],
        in_specs=[HBM] * (2 * n) + [SEM] * (2 * n) + [ANY], out_specs=[HBM] * (2 * n),
        input_output_aliases={t: t for t in range(2 * n)},
        compiler_params=pltpu.CompilerParams(has_side_effects=DATAFLOW),
    )(*srcs, *lands, *send_sems, *recv_sems, after)
    return res[:n], res[n:]


RS_CHUNKS = 2


def _piece(ref, kind, q, hf, pr, pc):
    if kind == "row":
        return ref.at[pl.ds((2 * q + hf) * pr, pr), :]
    return ref.at[pl.ds(hf * pr, pr), pl.ds(pl.multiple_of(_colblock(kind, q) * pc, 128), pc)]


def _rs_sibling_start(ts, meta, name):
    n = len(ts)
    ts = list(ts)
    lands = [lax.empty((N_CHIPS, pr, pc), F32) for _, pr, pc in meta]

    def body(*refs):
        t_refs, land_refs = refs[:n], refs[n:2 * n]
        send_sems, recv_sems, token = refs[2 * n:3 * n], refs[3 * n:4 * n], refs[-1]
        x, y, c = _place()
        for t, (kind, pr, pc) in enumerate(meta):
            for q in range(N_CHIPS):
                pltpu.make_async_remote_copy(
                    src_ref=_piece(t_refs[t], kind, q, 1 - c, pr, pc), dst_ref=land_refs[t].at[q],
                    send_sem=send_sems[t], recv_sem=recv_sems[t], device_id=(x, y, 1 - c), device_id_type=MESH).start()
        token[...] = jnp.zeros_like(token)

    res = pl.pallas_call(
        body, name=name,
        out_shape=[pltpu.SemaphoreType.DMA(())] * (2 * n) + [_hbm(a) for a in ts + lands]
        + [jax.ShapeDtypeStruct((8, 128), F32)],
        in_specs=[HBM] * (2 * n), out_specs=[SEM] * (2 * n) + [HBM] * (2 * n) + [pl.BlockSpec(memory_space=pltpu.VMEM)],
        input_output_aliases={t: 2 * n + t for t in range(2 * n)},
        compiler_params=pltpu.CompilerParams(has_side_effects=DATAFLOW),
    )(*[pltpu.with_memory_space_constraint(a, pltpu.HBM) for a in ts + lands])
    return (res[:n], res[n:2 * n], res[2 * n:3 * n], res[3 * n:4 * n]), res[-1]


def _rs_add_sibling(ts, r1s, meta, c):
    n = len(ts)
    in_specs, out_specs, out_shape = [], [], []
    for kind, pr, pc in meta:
        rs = pr // RS_CHUNKS
        if kind == "row":
            in_specs.append(pl.BlockSpec((rs, pc), lambda q, r, c_ref: ((2 * q + c_ref[0]) * RS_CHUNKS + r, 0)))
        else:
            in_specs.append(pl.BlockSpec(
                (rs, pc), lambda q, r, c_ref, kind=kind: (c_ref[0] * RS_CHUNKS + r, _colblock(kind, q))))
    for kind, pr, pc in meta:
        sp = pl.BlockSpec((None, pr // RS_CHUNKS, pc), lambda q, r, c_ref: (q, r, 0))
        in_specs.append(sp)
        out_specs.append(sp)
        out_shape.append(jax.ShapeDtypeStruct((N_CHIPS, pr, pc), BF16))

    def body(c_ref, *refs):
        for t in range(n):
            refs[2 * n + t][...] = (refs[t][...] + refs[n + t][...]).astype(BF16)

    return pl.pallas_call(
        body, name="rs_add_sibling", out_shape=out_shape,
        grid_spec=pltpu.PrefetchScalarGridSpec(num_scalar_prefetch=1, grid=(N_CHIPS, RS_CHUNKS), in_specs=in_specs,
                                               out_specs=out_specs),
        compiler_params=_params(2))(c.reshape(1).astype(jnp.int32), *ts, *r1s)


def _rs_chips_start(ps, meta, name):
    n = len(ps)
    ps = list(ps)
    lands = [lax.empty((N_CHIPS - 1, pr, pc), p.dtype) for p, (_, pr, pc) in zip(ps, meta)]

    def body(*refs):
        p_refs, land_refs = refs[:n], refs[n:2 * n]
        send_sems, recv_sems, token = refs[2 * n:3 * n], refs[3 * n:4 * n], refs[-1]
        x, y, c = _place()
        for t in range(n):
            for j, (cx, cy) in enumerate(_other_chips(x, y)):
                pltpu.make_async_remote_copy(
                    src_ref=p_refs[t].at[2 * cx + cy], dst_ref=land_refs[t].at[j], send_sem=send_sems[t],
                    recv_sem=recv_sems[t], device_id=(cx, cy, c), device_id_type=MESH).start()
        token[...] = jnp.zeros_like(token)

    res = pl.pallas_call(
        body, name=name,
        out_shape=[pltpu.SemaphoreType.DMA(())] * (2 * n) + [_hbm(a) for a in ps + lands]
        + [jax.ShapeDtypeStruct((8, 128), F32)],
        in_specs=[HBM] * (2 * n), out_specs=[SEM] * (2 * n) + [HBM] * (2 * n) + [pl.BlockSpec(memory_space=pltpu.VMEM)],
        input_output_aliases={t: 2 * n + t for t in range(2 * n)},
        compiler_params=pltpu.CompilerParams(has_side_effects=DATAFLOW),
    )(*[pltpu.with_memory_space_constraint(a, pltpu.HBM) for a in ps + lands])
    return (res[:n], res[n:2 * n], res[2 * n:3 * n], res[3 * n:4 * n]), res[-1]


def _rs_add_chips(ps, r2s, meta, myq, c):
    n = len(ps)
    in_specs, out_specs, out_shape = [], [], []
    for _, pr, pc in meta:
        in_specs.append(pl.BlockSpec((None, pr // RS_CHUNKS, pc), lambda r, q_ref, c_ref: (q_ref[0], r, 0)))
    for _, pr, pc in meta:
        in_specs.append(pl.BlockSpec((N_CHIPS - 1, pr // RS_CHUNKS, pc), lambda r, q_ref, c_ref: (0, r, 0)))
        out_specs.append(pl.BlockSpec((None, pr // RS_CHUNKS, pc), lambda r, q_ref, c_ref: (c_ref[0], r, 0)))
        out_shape.append(jax.ShapeDtypeStruct((2, pr, pc), F32))

    def body(q_ref, c_ref, *refs):
        for t in range(n):
            r2 = refs[n + t]
            own = refs[t][...].astype(F32)
            refs[2 * n + t][...] = ((own + r2[0].astype(F32)) + r2[1].astype(F32)) + r2[2].astype(F32)

    return pl.pallas_call(
        body, name="rs_add_chips", out_shape=out_shape,
        grid_spec=pltpu.PrefetchScalarGridSpec(num_scalar_prefetch=2, grid=(RS_CHUNKS,), in_specs=in_specs,
                                               out_specs=out_specs),
        compiler_params=_params(1))(myq.reshape(1).astype(jnp.int32), c.reshape(1).astype(jnp.int32), *ps, *r2s)


def _rs_share_halves(gs):
    n = len(gs)

    def body(*refs):
        g_refs, send_sems, recv_sems = refs[n:2 * n], refs[2 * n], refs[2 * n + 1]
        x, y, c = _place()
        sib = (x, y, 1 - c)
        for t in range(n):
            pltpu.make_async_remote_copy(
                src_ref=g_refs[t].at[c], dst_ref=g_refs[t].at[c], send_sem=send_sems.at[t], recv_sem=recv_sems.at[t],
                device_id=sib, device_id_type=MESH).start()
        for t in range(n):
            cp = pltpu.make_async_remote_copy(
                src_ref=g_refs[t].at[c], dst_ref=g_refs[t].at[1 - c], send_sem=send_sems.at[t],
                recv_sem=recv_sems.at[t], device_id=sib, device_id_type=MESH)
            cp.wait_send()
            cp.wait_recv()

    return pl.pallas_call(
        body, name="rs_share_halves", in_specs=[ANY] * n, out_specs=[ANY] * n,
        out_shape=[jax.ShapeDtypeStruct(g.shape, g.dtype) for g in gs], input_output_aliases={t: t for t in range(n)},
        scratch_shapes=[pltpu.SemaphoreType.DMA((n,)), pltpu.SemaphoreType.DMA((n,))],
    )(*gs)


def _rs_middle(handle, after, meta, c, name):
    ts, r1s = _split_wait(handle, after, name + "_sib_wait")
    ps = _rs_add_sibling(ts, r1s, meta, c)
    return _rs_chips_start(ps, meta, name + "_start")


def _rs_end(handle, after, meta, c, myq, name):
    ps, r2s = _split_wait(handle, after, name + "_wait")
    return _rs_share_halves(_rs_add_chips(ps, r2s, meta, myq, c))


def _allreduce_small(buf):
    rows = buf.shape[0]
    ndev = 8

    def body(b_ref, o_ref, slots, send_sems, recv_sems):
        x, y, c = _place()
        me = 4 * x + 2 * y + c
        slots[me] = b_ref[...]
        for k in range(1, ndev):
            kx, ky, kc = (k >> 2) & 1, (k >> 1) & 1, k & 1
            peer = (x ^ kx, y ^ ky, c ^ kc)
            pltpu.make_async_remote_copy(
                src_ref=b_ref, dst_ref=slots.at[me], send_sem=send_sems.at[k - 1], recv_sem=recv_sems.at[k - 1],
                device_id=peer, device_id_type=MESH).start()
        for k in range(1, ndev):
            kx, ky, kc = (k >> 2) & 1, (k >> 1) & 1, k & 1
            cp = pltpu.make_async_remote_copy(
                src_ref=b_ref, dst_ref=slots.at[me ^ k], send_sem=send_sems.at[k - 1], recv_sem=recv_sems.at[k - 1],
                device_id=(x ^ kx, y ^ ky, c ^ kc), device_id_type=MESH)
            cp.wait_send()
            cp.wait_recv()
        acc = slots[0]
        for s in range(1, ndev):
            acc = acc + slots[s]
        o_ref[...] = acc

    vm = pl.BlockSpec(memory_space=pltpu.VMEM)
    return pl.pallas_call(
        body, name="allreduce_small", in_specs=[vm], out_specs=vm, out_shape=jax.ShapeDtypeStruct((rows, 128), F32),
        scratch_shapes=[pltpu.VMEM((ndev, rows, 128), F32), pltpu.SemaphoreType.DMA((ndev - 1,)),
                        pltpu.SemaphoreType.DMA((ndev - 1,))],
        compiler_params=pltpu.CompilerParams(vmem_limit_bytes=VMEM_LIMIT))(buf)


RS_META = [("col", D // 2, IN_PAD), ("row", D // 8, D), ("row", D // 8, D), ("row", D // 8, D),
           ("colx", D // 2, 2 * FFN // N_CHIPS), ("row", FFN // 8, D)]
SMALL_ROWS = 156


def _rope_tables(t_):
    half = SW_DIM // 2
    inv_freq = ROPE_THETA ** (-jnp.arange(half, dtype=F32) / half)
    ang = jnp.arange(t_, dtype=F32)[:, None] * inv_freq[None, :]
    reps = 128 // half
    return jnp.concatenate([jnp.cos(ang)] * reps, axis=1), jnp.concatenate([jnp.sin(ang)] * reps, axis=1)


def _orig_cols(padded, a, b):
    out = []
    for q in range(N_CHIPS):
        lo, hi = max(a, q * IN_SHARD), min(b, (q + 1) * IN_SHARD)
        if lo < hi:
            out.append(padded[:, q * IN_PAD + lo - q * IN_SHARD:q * IN_PAD + hi - q * IN_SHARD])
    return out


_ORIG_SEGMENTS = [(0, R_BG, "main", 0), (R_BG, R_SW, "bg", 0), (R_SW, R_GATES, "main", O_QS), (R_GATES, IN_COLS, "main", O_GA)]


def _to_padded_shards(main, bg):
    zeros = jnp.zeros((main.shape[0], IN_PAD - IN_SHARD), main.dtype)
    parts = []
    for q in range(N_CHIPS):
        for a, b, src, s0 in _ORIG_SEGMENTS:
            lo, hi = max(a, q * IN_SHARD), min(b, (q + 1) * IN_SHARD)
            if lo < hi:
                parts.append((main if src == "main" else bg)[:, s0 + lo - a:s0 + hi - a])
        parts.append(zeros)
    return jnp.concatenate(parts, axis=1)


def _lane_row(v16):
    return jnp.pad(v16.reshape(1, 2 * DN_HEADS), ((0, 0), (2 * DN_HEADS, 128 - 4 * DN_HEADS)))


def _pack_small(g):
    pad16 = jnp.pad(g["sinks"], ((0, 0), (0, 128 - SW_HEADS)))
    return jnp.concatenate([g["conv"].reshape(-1, 128), g["ln1g"].reshape(-1, 128), g["ln1b"].reshape(-1, 128),
                            g["ln2g"].reshape(-1, 128), g["ln2b"].reshape(-1, 128), g["gnw"], g["arow"], g["dtrow"],
                            pad16], axis=0)


def _unpack_small(buf):
    nconv = DN_CONV * 3 * D // 128
    o = nconv
    out = dict(conv=buf[:o].reshape(DN_CONV, 3 * D))
    for name in ("ln1g", "ln1b", "ln2g", "ln2b"):
        out[name] = buf[o:o + 8].reshape(D)
        o += 8
    out["gnw"] = buf[o]
    out["a_log"] = buf[o + 1, 2 * DN_HEADS:4 * DN_HEADS].reshape(2, DN_HEADS)
    out["dt_bias"] = buf[o + 2, 2 * DN_HEADS:4 * DN_HEADS].reshape(2, DN_HEADS)
    out["sinks"] = buf[o + 3, :SW_HEADS]
    return out


def kernel(x, w_in, conv_w, a_log, dt_bias, dn_norm_w, sinks, w_branch_a, w_branch_b, w_out, ln1_g, ln1_b, w_gate_up, w_down, ln2_g, ln2_b, loss_target, m_w_in, m_conv_w, m_a_log, m_dt_bias, m_dn_norm_w, m_sinks, m_w_branch_a, m_w_branch_b, m_w_out, m_ln1_g, m_ln1_b, m_w_gate_up, m_w_down, m_ln2_g, m_ln2_b, v_w_in, v_conv_w, v_a_log, v_dt_bias, v_dn_norm_w, v_sinks, v_w_branch_a, v_w_branch_b, v_w_out, v_ln1_g, v_ln1_b, v_w_gate_up, v_w_down, v_ln2_g, v_ln2_b):
    xi, yi, ci = _place()
    myq = 2 * xi + yi
    t_ = x.shape[1]
    cos, sin = _rope_tables(t_)

    loc_in = jnp.pad(w_in.astype(BF16), ((0, 0), (0, 0), (0, IN_PAD - IN_SHARD)))
    locs = [loc_in, w_branch_a.astype(BF16), w_branch_b.astype(BF16), w_out.astype(BF16), w_gate_up.astype(BF16),
            w_down.astype(BF16), conv_w]
    kinds = ["col", "row", "row", "row", "colx", "row", "col"]
    gathers = []
    for l in range(DEPTH):
        srcs = [a[l] for a in locs]
        if gathers:
            srcs[-1] = srcs[-1] + gathers[-1][1][0, 0]
        gathers.append(_gather_start(srcs, kinds, "gather_%d_start" % l))

    def in_weights(l, after):
        _, (full_in,) = _split_wait(tuple(part[:1] for part in gathers[l][0]), after, "gather_%d_wait_in" % l)
        cols = lambda a, b: _orig_cols(full_in, a, b)
        return dict(
            in_main=jnp.concatenate(cols(0, R_BG) + cols(R_GATES, IN_COLS) + cols(R_SW, R_GATES), axis=1),
            in_bg=jnp.pad(jnp.concatenate(cols(R_BG, R_SW), axis=1), ((0, 0), (0, 128 - 4 * DN_HEADS))))

    def rest_weights(l, after):
        _, (full_a, full_b, full_o, full_gu, full_d, full_conv) = _split_wait(
            tuple(part[1:] for part in gathers[l][0]), after, "gather_%d_wait_rest" % l)
        return dict(
            conv=full_conv, arow=_lane_row(a_log[l]), dtrow=_lane_row(dt_bias[l]), gnw=dn_norm_w[l][None],
            sinks=sinks[l][None], a=full_a.reshape(D, D), b=full_b.reshape(D, D), o=full_o.reshape(D, D),
            ln1g=ln1_g[l][None], ln1b=ln1_b[l][None], gu=full_gu, d=full_d.reshape(FFN, D),
            ln2g=ln2_g[l][None], ln2b=ln2_b[l][None])

    h = x[0]
    hm = h.astype(_MXU)
    residuals = []
    for l in range(DEPTH):
        win = in_weights(l, gathers[-1][1] if l == 0 else h)
        h, hm, res = _layer_fwd(h, hm, win, functools.partial(rest_weights, l), cos, sin)
        residuals.append(res)
    dh, sq = _loss_head(h, loss_target[0])
    loss = lax.psum((0.5 / D) * jnp.sum(sq), ("x", "y", "c"))

    big = [None] * DEPTH
    small = [None] * DEPTH
    hop1 = hop2 = None

    def mid(after):
        nonlocal hop1, hop2
        if hop1 is None:
            return None
        handle, tok = _rs_middle(hop1[1], after, RS_META, ci, "rs_chips_%d" % hop1[0])
        hop1, hop2 = None, (hop1[0], handle)
        return tok

    for l in reversed(range(DEPTH)):
        dh, g = _layer_bwd(dh, residuals[l], residuals[l]["w"], cos, sin, mid)
        if hop2 is not None:
            big[hop2[0]] = _rs_end(hop2[1], dh, RS_META, ci, myq, "rs_chips_%d" % hop2[0])
        g_in = _to_padded_shards(g["in_main"], g["in_bg"])
        handle, token = _rs_sibling_start([g_in, g["a"], g["b"], g["o"], g["gu"], g["d"]], RS_META, "rs_sib_%d_start" % l)
        hop1 = (l, handle)
        dh = dh + token[0, 0]
        small[l] = _pack_small(g)
    tot = _allreduce_small(jnp.concatenate(small, axis=0))
    token = mid(tot)
    pending = hop2
    sm = [_unpack_small(tot[l * SMALL_ROWS:(l + 1) * SMALL_ROWS]) for l in range(DEPTH)]
    stack = lambda name: jnp.stack([s[name] for s in sm], axis=0)
    grads = dict(
        conv_w=lax.dynamic_slice_in_dim(stack("conv"), myq * (3 * D // N_CHIPS), 3 * D // N_CHIPS, axis=2),
        a_log=stack("a_log"), dt_bias=stack("dt_bias"), dn_norm_w=stack("gnw"), sinks=stack("sinks"),
        ln1_g=stack("ln1g"), ln1_b=stack("ln1b"), ln2_g=stack("ln2g"), ln2_b=stack("ln2b"))
    weights = dict(w_in=w_in, conv_w=conv_w, a_log=a_log, dt_bias=dt_bias, dn_norm_w=dn_norm_w, sinks=sinks,
                   w_branch_a=w_branch_a, w_branch_b=w_branch_b, w_out=w_out, ln1_g=ln1_g, ln1_b=ln1_b,
                   w_gate_up=w_gate_up, w_down=w_down, ln2_g=ln2_g, ln2_b=ln2_b)
    ms = dict(w_in=m_w_in, conv_w=m_conv_w, a_log=m_a_log, dt_bias=m_dt_bias, dn_norm_w=m_dn_norm_w, sinks=m_sinks,
              w_branch_a=m_w_branch_a, w_branch_b=m_w_branch_b, w_out=m_w_out, ln1_g=m_ln1_g, ln1_b=m_ln1_b,
              w_gate_up=m_w_gate_up, w_down=m_w_down, ln2_g=m_ln2_g, ln2_b=m_ln2_b)
    vs = dict(w_in=v_w_in, conv_w=v_conv_w, a_log=v_a_log, dt_bias=v_dt_bias, dn_norm_w=v_dn_norm_w, sinks=v_sinks,
              w_branch_a=v_w_branch_a, w_branch_b=v_w_branch_b, w_out=v_w_out, ln1_g=v_ln1_g, ln1_b=v_ln1_b,
              w_gate_up=v_w_gate_up, w_down=v_w_down, ln2_g=v_ln2_g, ln2_b=v_ln2_b)
    names = list(weights)
    upd = {n: _adamw("adamw_" + n, weights[n], grads[n], ms[n], vs[n]) for n in grads}
    big_names = ["w_in", "w_branch_a", "w_branch_b", "w_out", "w_gate_up", "w_down"]
    carry = {n: None for n in big_names}

    def update_layer(l):
        for t, n in enumerate(big_names):
            carry[n] = _adamw_layer("adamw_" + n, l, weights[n], big[l][t], ms[n], vs[n], carry[n], token)

    for l in range(DEPTH - 1, pending[0], -1):
        update_layer(l)
    big[pending[0]] = _rs_end(pending[1], carry[big_names[-1]][1], RS_META, ci, myq, "rs_chips_%d" % pending[0])
    update_layer(pending[0])
    for n in big_names:
        grads[n], upd[n] = carry[n][0], carry[n][1:]
    return (loss, dh[None], *[grads[n] for n in names], *[upd[n][0] for n in names], *[upd[n][1] for n in names],
            *[upd[n][2] for n in names])
```

```python
import functools

import jax
import jax.numpy as jnp
from jax import lax
from jax.experimental import pallas as pl
from jax.experimental.pallas import tpu as pltpu

F32 = jnp.float32
BF16 = jnp.bfloat16
_MXU = BF16

D = 1024
DEPTH = 4
DN_HEADS = 8
DN_DIM = 128
DN_CONV = 5
CHUNK = 64
SW_HEADS = 16
SW_KV = 4
SW_DIM = 64
SW_GRP = SW_HEADS // SW_KV
SW_BLOCK = 128
ROPE_THETA = 10000.0
FFN = 2816
ALPHA = (2.0 * DEPTH) ** 0.25
LN_EPS = 1e-5
RMS_EPS = 1e-6
IN_COLS = 7712
O_Z, O_GA, O_GB, O_QS, O_KS, O_VS, N_MAIN = 3072, 4096, 5120, 6144, 7168, 7424, 7680
R_BG, R_SW, R_GATES = 4096, 4128, 5664
N_CHIPS = 4
IN_SHARD = IN_COLS // N_CHIPS
IN_PAD = 2048
ADAM_LR, ADAM_B1, ADAM_B2, ADAM_EPS, ADAM_WD, ADAM_STEP = 0.001, 0.9, 0.999, 1e-08, 0.01, 10
VMEM_LIMIT = 52 * 1024 * 1024
MESH = pl.DeviceIdType.MESH
ANY = pl.BlockSpec(memory_space=pl.ANY)


def _params(n_grid, **kw):
    return pltpu.CompilerParams(dimension_semantics=("arbitrary",) * n_grid, vmem_limit_bytes=VMEM_LIMIT, **kw)


def _full(a):
    nd = a.ndim
    return pl.BlockSpec(a.shape, lambda *_, nd=nd: (0,) * nd)


def _raw_dot(a, b, ca, cb):
    return lax.dot_general(a.astype(_MXU), b.astype(_MXU), (((ca,), (cb,)), ((), ())), preferred_element_type=F32)


@jax.custom_vjp
def _nn(a, b):
    return _raw_dot(a, b, 1, 0)


@jax.custom_vjp
def _nt(a, b):
    return _raw_dot(a, b, 1, 1)


@jax.custom_vjp
def _tn(a, b):
    if a.shape[1] > b.shape[1]:
        return _raw_dot(b, a, 0, 0).T
    return _raw_dot(a, b, 0, 0)


_nn.defvjp(lambda a, b: (_nn(a, b), (a, b)), lambda r, g: (_nt(g, r[1]), _tn(r[0], g)))
_nt.defvjp(lambda a, b: (_nt(a, b), (a, b)), lambda r, g: (_nn(g, r[1]), _tn(g, r[0])))
_tn.defvjp(lambda a, b: (_tn(a, b), (a, b)), lambda r, g: (_nt(r[1], g), _nn(r[0], g)))


def _hdot(a, b, ca=1, cb=0):
    ah, bh = a.astype(BF16), b.astype(BF16)
    al, bl = (a - ah.astype(F32)).astype(BF16), (b - bh.astype(F32)).astype(BF16)
    dot = lambda u, v: lax.dot_general(u, v, (((ca,), (cb,)), ((), ())), preferred_element_type=F32)
    return dot(ah, bh) + (dot(ah, bl) + dot(al, bh))


def _inv_impl(mats):
    n = mats[0].shape[0]
    eye = (lax.broadcasted_iota(jnp.int32, (n, n), 0) == lax.broadcasted_iota(jnp.int32, (n, n), 1)).astype(F32)
    ps = [-a for a in mats]
    ts = [eye + p for p in ps]
    for _ in range(max(1, (n - 1).bit_length()) - 1):
        ps = [_hdot(p, p) for p in ps]
        ts = [t + _hdot(t, p) for t, p in zip(ts, ps)]
    return tuple(ts)


@jax.custom_vjp
def _inv(mats):
    return _inv_impl(mats)


def _inv_fwd(mats):
    ts = _inv_impl(mats)
    return ts, ts


def _inv_bwd(ts, gs):
    xs = [_hdot(t, g, 0, 0) for t, g in zip(ts, gs)]
    return (tuple(-_hdot(x, t, 1, 1) for x, t in zip(xs, ts)),)


_inv.defvjp(_inv_fwd, _inv_bwd)


@jax.custom_vjp
def _inv_saved(mats, saved):
    return saved


_inv_saved.defvjp(lambda mats, saved: (saved, saved),
                  lambda ts, gs: (_inv_bwd(ts, gs)[0], tuple(jnp.zeros_like(t) for t in ts)))


def _tile(n, cap):
    if n <= cap:
        return n
    best = [t for t in range(128, cap + 1, 128) if n % t == 0]
    assert best, (n, cap)
    return best[-1]


def _mm(a, b, *, name, ta=False, tb=False, add=None, tm=1024, tn=1024, tk=1024, after=None, out_dtype=F32):
    if ta:
        k_, m_ = a.shape
    else:
        m_, k_ = a.shape
    n_ = b.shape[0] if tb else b.shape[1]
    tm, tn, tk = _tile(m_, tm), _tile(n_, tn), _tile(k_, tk)
    nk = k_ // tk
    has_add = add is not None

    def body(*refs):
        a_ref, b_ref = refs[:2]
        add_ref = refs[2] if has_add else None
        o_ref = refs[2 + has_add + (after is not None)]
        part = _raw_dot(a_ref[...], b_ref[...], 0 if ta else 1, 1 if tb else 0)
        if nk == 1:
            o_ref[...] = (part + add_ref[...] if has_add else part).astype(o_ref.dtype)
            return
        acc = refs[-1]
        k = pl.program_id(2)

        @pl.when(k == 0)
        def _():
            acc[...] = part

        @pl.when(jnp.logical_and(k > 0, k < nk - 1))
        def _():
            acc[...] += part

        @pl.when(k == nk - 1)
        def _():
            o_ref[...] = (acc[...] + part + add_ref[...] if has_add else acc[...] + part).astype(o_ref.dtype)

    a_spec = pl.BlockSpec((tk, tm), lambda i, j, k: (k, i)) if ta else pl.BlockSpec((tm, tk), lambda i, j, k: (i, k))
    b_spec = pl.BlockSpec((tn, tk), lambda i, j, k: (j, k)) if tb else pl.BlockSpec((tk, tn), lambda i, j, k: (k, j))
    o_spec = pl.BlockSpec((tm, tn), lambda i, j, k: (i, j))
    in_specs = [a_spec, b_spec] + ([o_spec] if has_add else []) + ([ANY] if after is not None else [])
    args = (a, b) + ((add,) if has_add else ()) + ((after,) if after is not None else ())
    return pl.pallas_call(
        body, name=name, grid=(m_ // tm, n_ // tn, nk), in_specs=in_specs, out_specs=o_spec,
        out_shape=jax.ShapeDtypeStruct((m_, n_), out_dtype),
        scratch_shapes=[pltpu.VMEM((tm, tn), F32)] if nk > 1 else [],
        compiler_params=_params(3))(*args)


def _mm_fused(a, b, post, outs, *, name, rows=(), params=(), n_sums=0, tb=False, tm=1024, tn=1024, tk=1024):
    m_, k_ = a.shape
    n_ = b.shape[0] if tb else b.shape[1]
    tm, tn, tk = _tile(m_, tm), _tile(n_, tn), _tile(k_, tk)
    nk = k_ // tk
    nr, npar, no = len(rows), len(params), len(outs)
    aliased = [o[4] for o in outs if o[4] is not None]
    nin = 2 + nr + npar

    def body(*refs):
        a_ref, b_ref = refs[:2]
        row_refs, par_refs = refs[2:2 + nr], refs[2 + nr:nin]
        out_refs = refs[nin + len(aliased):nin + len(aliased) + no + n_sums]
        part = _raw_dot(a_ref[...], b_ref[...], 1, 1 if tb else 0)
        k = pl.program_id(2)
        first = jnp.logical_and(pl.program_id(0) == 0, pl.program_id(1) == 0)
        if nk > 1:
            acc = refs[-1]

            @pl.when(k == 0)
            def _():
                acc[...] = part

            @pl.when(jnp.logical_and(k > 0, k < nk - 1))
            def _():
                acc[...] += part

        @pl.when(k == nk - 1)
        def _():
            total = acc[...] + part if nk > 1 else part
            res = post(total, *[r[...].astype(F32) for r in row_refs], *[p[...] for p in par_refs])
            for o_ref, val in zip(out_refs[:no], res[:no]):
                o_ref[...] = val.astype(o_ref.dtype)
            for s_ref, val in zip(out_refs[no:], res[no:]):
                @pl.when(first)
                def _(s_ref=s_ref):
                    s_ref[...] = jnp.zeros_like(s_ref)
                s_ref[...] += val

    b_spec = pl.BlockSpec((tn, tk), lambda i, j, k: (j, k)) if tb else pl.BlockSpec((tk, tn), lambda i, j, k: (k, j))
    in_specs = [pl.BlockSpec((tm, tk), lambda i, j, k: (i, k)), b_spec]
    in_specs += [pl.BlockSpec((tm, w), lambda i, j, k, cb=cb: (i, cb(j))) for _, w, cb in rows]
    in_specs += [_full(p) for p in params] + [ANY] * len(aliased)
    out_specs = [pl.BlockSpec((tm, w), lambda i, j, k, cb=cb: (i, cb(j))) for _, w, _, cb, _ in outs]
    out_specs += [_full(p) for p in params[:n_sums]]
    out_shape = [jax.ShapeDtypeStruct((m_, tot), dt) for tot, _, dt, _, _ in outs]
    out_shape += [jax.ShapeDtypeStruct(p.shape, F32) for p in params[:n_sums]]
    aliases, pos = {}, nin
    for oi, o in enumerate(outs):
        if o[4] is not None:
            aliases[pos] = oi
            pos += 1
    return pl.pallas_call(
        body, name=name, grid=(m_ // tm, n_ // tn, nk), in_specs=in_specs, out_specs=out_specs, out_shape=out_shape,
        scratch_shapes=[pltpu.VMEM((tm, tn), F32)] if nk > 1 else [], input_output_aliases=aliases,
        compiler_params=_params(3))(a, b, *[r[0] for r in rows], *params, *aliased)


def _row_spec(tb, w, c0, percol):
    return pl.BlockSpec((tb, w), lambda i, j, c0=c0, pc=percol: (i, c0 + (j if pc else 0)))


def _stage_fwd(f, name, rows, params, outs, tb, ncol=1):
    t_ = rows[0][0].shape[0]
    nr, npar = len(rows), len(params)

    def body(*refs):
        res = f(*[r[...].astype(F32) for r in refs[:nr + npar]])
        for o_ref, val in zip(refs[nr + npar:], res):
            o_ref[...] = val.astype(o_ref.dtype)

    return pl.pallas_call(
        body, name=name, grid=(t_ // tb, ncol),
        in_specs=[_row_spec(tb, w, c0, pc) for (_, w, c0, pc) in rows] + [_full(p) for p in params],
        out_specs=[pl.BlockSpec((tb, w), lambda i, j: (i, j)) for w, _ in outs],
        out_shape=[jax.ShapeDtypeStruct((t_, w * ncol), dt) for w, dt in outs],
        compiler_params=_params(2))(*[r[0] for r in rows], *params)


def _into(dest, tb, width, t_, ncol, dtype):
    if dest is None:
        return pl.BlockSpec((tb, width), lambda i, j: (i, j)), jax.ShapeDtypeStruct((t_, width * ncol), dtype), None
    buf, total, c0 = dest
    return (pl.BlockSpec((tb, width), lambda i, j, c0=c0: (i, c0 + j)), jax.ShapeDtypeStruct((t_, total), dtype), buf)


def _stage_bwd(f, name, rows, params, douts, tb, ncol=1, cat=None, dtypes=None, dest=None, after=None):
    t_ = rows[0][0].shape[0]
    nr, npar, nd = len(rows), len(params), len(douts)
    cat = cat if cat is not None else [[r] for r in range(nr)]
    dtypes = dtypes if dtypes is not None else [F32] * len(cat)
    dest = dest or {}
    assert ncol == 1 or all(len(g) == 1 and rows[g[0]][3] for g in cat)
    nin = nr + npar + nd
    unread = [after] if after is not None else []

    def body(*refs):
        ins = [r[...].astype(F32) for r in refs[:nr + npar]]
        dvals = tuple(r[...].astype(F32) for r in refs[nr + npar:nin])
        out_refs = refs[nin + len(aliased) + len(unread):]
        _, vjp = jax.vjp(f, *ins)
        grads = vjp(dvals)
        for o_ref, grp in zip(out_refs[:len(cat)], cat):
            val = grads[grp[0]] if len(grp) == 1 else jnp.concatenate([grads[r] for r in grp], axis=-1)
            o_ref[...] = val.astype(o_ref.dtype)
        first = jnp.logical_and(pl.program_id(0) == 0, pl.program_id(1) == 0)
        for p_ref, gp in zip(out_refs[len(cat):], grads[nr:]):
            @pl.when(first)
            def _(p_ref=p_ref):
                p_ref[...] = jnp.zeros_like(p_ref)
            p_ref[...] += gp

    gw = [sum(rows[r][1] for r in grp) for grp in cat]
    out_specs, out_shape, aliased, aliases = [], [], [], {}
    for gi, (w, dt) in enumerate(zip(gw, dtypes)):
        spec, shape, buf = _into(dest.get(gi), tb, w, t_, ncol, dt)
        out_specs.append(spec)
        out_shape.append(shape)
        if buf is not None:
            aliases[nin + len(aliased)] = gi
            aliased.append(buf)
    return pl.pallas_call(
        body, name=name, grid=(t_ // tb, ncol),
        in_specs=[_row_spec(tb, w, c0, pc) for (_, w, c0, pc) in rows] + [_full(p) for p in params]
        + [pl.BlockSpec((tb, d.shape[1] // ncol), lambda i, j: (i, j)) for d in douts]
        + [ANY] * (len(aliased) + len(unread)),
        out_specs=out_specs + [_full(p) for p in params],
        out_shape=out_shape + [jax.ShapeDtypeStruct(p.shape, F32) for p in params],
        input_output_aliases=aliases,
        compiler_params=_params(2))(*[r[0] for r in rows], *params, *douts, *aliased, *unread)


def _ln_f(x, y, g, b):
    u = ALPHA * x + y
    c = u - jnp.mean(u, axis=-1, keepdims=True)
    var = jnp.mean(c * c, axis=-1, keepdims=True)
    return (c * lax.rsqrt(var + LN_EPS) * g + b,)


def _ln_f2(x, y, g, b):
    out, = _ln_f(x, y, g, b)
    return out, out


def _swiglu_tile(gu):
    half = gu.shape[1] // 2
    return (jax.nn.silu(gu[:, :half]) * gu[:, half:],)


def _merge_f(ya, yb, ga, gb):
    return (jax.nn.sigmoid(ga) * ya + jax.nn.sigmoid(gb) * yb,)


def _gnorm_f(of, ob, z, w):
    o = of + ob
    return (o * lax.rsqrt(jnp.mean(o * o, axis=-1, keepdims=True) + RMS_EPS) * w * jax.nn.silu(z),)


def _bg_f(x, arow, dtrow):
    lane = lax.broadcasted_iota(jnp.int32, x.shape, 1)
    beta = jax.nn.sigmoid(x)
    g = -jnp.exp(arow) * jax.nn.softplus(x + dtrow)
    return (jnp.where(lane < 16, beta, jnp.where(lane < 32, g, 0.0)),)


PREP_ROWS = 512
PAD = 8


def _prep_f(part, w, *wins):
    xc = wins[0] * w[0:1, :]
    for k in range(1, DN_CONV):
        xc = xc + wins[k] * w[k:k + 1, :]
    a = jax.nn.silu(xc)
    nrm = a * lax.rsqrt(jnp.sum(a * a, axis=-1, keepdims=True) + RMS_EPS)
    return jnp.where(part == 0, nrm * (DN_DIM ** -0.5), jnp.where(part == 1, nrm, a))


def _windows(pad_ref, r0, rows):
    return [pad_ref[PAD + r0 - 2 + k:PAD + r0 - 2 + k + rows, :] for k in range(DN_CONV)]


def _prep_fwd(pm, conv):
    t_ = pm.shape[0]
    rows = min(PREP_ROWS, t_)

    def body(x_ref, w_ref, o_ref, pad_ref):
        part = pl.program_id(0) // DN_HEADS
        pad_ref[0:PAD, :] = jnp.zeros((PAD, DN_DIM), F32)
        pad_ref[PAD + t_:2 * PAD + t_, :] = jnp.zeros((PAD, DN_DIM), F32)
        pad_ref[PAD:PAD + t_, :] = x_ref[...]
        w = w_ref[...]
        for r in range(t_ // rows):
            o_ref[r * rows:(r + 1) * rows, :] = _prep_f(part, w, *_windows(pad_ref, r * rows, rows))

    ncb = 3 * DN_HEADS
    return pl.pallas_call(
        body, name="prep_fwd", grid=(ncb,),
        in_specs=[pl.BlockSpec((t_, DN_DIM), lambda j: (0, j)), pl.BlockSpec((DN_CONV, DN_DIM), lambda j: (0, j))],
        out_specs=pl.BlockSpec((t_, DN_DIM), lambda j: (0, j)),
        out_shape=jax.ShapeDtypeStruct((t_, ncb * DN_DIM), F32),
        scratch_shapes=[pltpu.VMEM((t_ + 2 * PAD, DN_DIM), F32)],
        compiler_params=_params(1))(pm, conv)


def _prep_bwd(pm, conv, dout, dpm):
    t_ = pm.shape[0]
    rows = min(PREP_ROWS, t_)

    def body(x_ref, w_ref, d_ref, _, dx_ref, dw_ref, pad_ref, dpad_ref):
        part = pl.program_id(0) // DN_HEADS
        pad_ref[0:PAD, :] = jnp.zeros((PAD, DN_DIM), F32)
        pad_ref[PAD + t_:2 * PAD + t_, :] = jnp.zeros((PAD, DN_DIM), F32)
        pad_ref[PAD:PAD + t_, :] = x_ref[...]
        dpad_ref[...] = jnp.zeros_like(dpad_ref)
        w = w_ref[...]
        dw = jnp.zeros((DN_CONV, DN_DIM), F32)
        for r in range(t_ // rows):
            r0 = r * rows
            _, vjp = jax.vjp(functools.partial(_prep_f, part), w, *_windows(pad_ref, r0, rows))
            grads = vjp(d_ref[r0:r0 + rows, :])
            dw = dw + grads[0]
            for k in range(DN_CONV):
                lo = PAD + r0 - 2 + k
                dpad_ref[lo:lo + rows, :] += grads[1 + k]
        dx_ref[...] = dpad_ref[PAD:PAD + t_, :].astype(dx_ref.dtype)
        dw_ref[...] = dw

    ncb = 3 * DN_HEADS
    col = pl.BlockSpec((t_, DN_DIM), lambda j: (0, j))
    wsp = pl.BlockSpec((DN_CONV, DN_DIM), lambda j: (0, j))
    return pl.pallas_call(
        body, name="prep_bwd", grid=(ncb,), in_specs=[col, wsp, col, ANY], out_specs=[col, wsp],
        out_shape=[jax.ShapeDtypeStruct(dpm.shape, dpm.dtype), jax.ShapeDtypeStruct((DN_CONV, ncb * DN_DIM), F32)],
        scratch_shapes=[pltpu.VMEM((t_ + 2 * PAD, DN_DIM), F32), pltpu.VMEM((t_ + 2 * PAD, DN_DIM), F32)],
        input_output_aliases={3: 0}, compiler_params=_params(1))(pm, conv, dout, dpm)


def _dn_chunk(sgns, qs, ks, vs, grows, brows, tsaved=None, with_t=False):
    c = qs[0].shape[0]
    i = lax.broadcasted_iota(jnp.int32, (c, c), 0)
    j = lax.broadcasted_iota(jnp.int32, (c, c), 1)
    eye = i == j
    incl = {s: (i - j) * int(s) >= 0 for s in set(sgns)}
    strict = {s: (i - j) * int(s) > 0 for s in set(sgns)}
    gcs = [jnp.sum(jnp.where(incl[s], g, 0.0), axis=1, keepdims=True) for s, g in zip(sgns, grows)]
    grs = [jnp.sum(jnp.where(eye, gc, 0.0), axis=0, keepdims=True) for gc in gcs]
    bcs = [jnp.sum(jnp.where(eye, b, 0.0), axis=1, keepdims=True) for b in brows]
    gls = [jnp.sum(g, axis=1, keepdims=True) for g in grows]
    decs = [jnp.exp(jnp.where(incl[s], gc - gr, -1e30)) for s, gc, gr in zip(sgns, gcs, grs)]
    kks = [_nt(k, k) for k in ks]
    amats = tuple(jnp.where(strict[s], bc * kk * dec, 0.0) for s, bc, kk, dec in zip(sgns, bcs, kks, decs))
    tinvs = _inv(amats) if tsaved is None else _inv_saved(amats, tsaved)
    egcs = [jnp.exp(gc) for gc in gcs]
    us = [_nn(t, v * bc) for t, v, bc in zip(tinvs, vs, bcs)]
    ws = [_nn(t, k * (bc * egc)) for t, k, bc, egc in zip(tinvs, ks, bcs, egcs)]
    qks = [_nt(q, k) * dec for q, k, dec in zip(qs, ks, decs)]
    qds = [q * egc for q, egc in zip(qs, egcs)]
    kds = [k * jnp.exp(gl - gc) for k, gl, gc in zip(ks, gls, gcs)]
    res = tuple(us), tuple(ws), tuple(qks), tuple(qds), tuple(kds)
    return res + (tinvs,) if with_t else res


def _dn_step(us, ws, qks, qds, kds, grows, ss):
    gls = [jnp.exp(jnp.sum(g, axis=1, keepdims=True)) for g in grows]
    wss = [_nn(w, s) for w, s in zip(ws, ss)]
    qss = [_nn(qd, s) for qd, s in zip(qds, ss)]
    vns = [u - x for u, x in zip(us, wss)]
    os_ = [a + _nn(qk, vn) for a, qk, vn in zip(qss, qks, vns)]
    s2s = [s * gl + _tn(kd, vn) for s, gl, kd, vn in zip(ss, gls, kds, vns)]
    return tuple(os_), tuple(s2s)


def _hs(h):
    return slice(h * DN_DIM, (h + 1) * DN_DIM)


_DIR_SGN = (1, -1)
_A_OUT = 5
_PROBLEMS = [(d, h) for d in range(2) for h in range(DN_HEADS)]
_SGNS = [_DIR_SGN[d] for d, _ in _PROBLEMS]


def _chunk_inputs(q_ref, k_ref, v_ref, g_ref, b_ref):
    heads = lambda ref: tuple(ref[:, _hs(h)].astype(F32) for _, h in _PROBLEMS)
    rows = lambda ref: tuple(ref[d, 0, h:h + 1, :] for d, h in _PROBLEMS)
    return heads(q_ref), heads(k_ref), heads(v_ref), rows(g_ref), rows(b_ref)


def _dn_a_fwd(qkv, grows, brows):
    t_ = qkv.shape[0]
    nch = t_ // CHUNK

    def body(q_ref, k_ref, v_ref, g_ref, b_ref, *outs):
        us, ws, qks, qds, kds, tinvs = _dn_chunk(_SGNS, *_chunk_inputs(q_ref, k_ref, v_ref, g_ref, b_ref), with_t=True)
        for p, (d, h) in enumerate(_PROBLEMS):
            u_ref, w_ref, qk_ref, qd_ref, kd_ref = outs[d * _A_OUT:(d + 1) * _A_OUT]
            u_ref[:, _hs(h)], w_ref[:, _hs(h)], qk_ref[0, h] = us[p], ws[p].astype(_MXU), qks[p].astype(_MXU)
            qd_ref[:, _hs(h)], kd_ref[:, _hs(h)] = qds[p].astype(_MXU), kds[p].astype(_MXU)
            outs[2 * _A_OUT + d][0, h] = tinvs[p]

    rspec = pl.BlockSpec((2, 1, DN_HEADS, CHUNK), lambda c: (0, c, 0, 0))
    big = pl.BlockSpec((CHUNK, D), lambda c: (c, 0))
    qks = pl.BlockSpec((1, DN_HEADS, CHUNK, CHUNK), lambda c: (c, 0, 0, 0))
    bigs = lambda dt: jax.ShapeDtypeStruct((t_, D), dt)
    qksh = lambda dt: jax.ShapeDtypeStruct((nch, DN_HEADS, CHUNK, CHUNK), dt)
    res = pl.pallas_call(
        body, name="dn_a_fwd", grid=(nch,),
        in_specs=[pl.BlockSpec((CHUNK, D), lambda c, p=p: (c, p)) for p in range(3)] + [rspec, rspec],
        out_specs=[big, big, qks, big, big] * 2 + [qks, qks],
        out_shape=[bigs(F32), bigs(_MXU), qksh(_MXU), bigs(_MXU), bigs(_MXU)] * 2 + [qksh(F32)] * 2,
        compiler_params=_params(1))(qkv, qkv, qkv, grows, brows)
    return res[:2 * _A_OUT], res[2 * _A_OUT:]


def _dn_a_bwd(qkv, grows, brows, tinv, dres, dg_b):
    t_ = qkv.shape[0]
    nch = t_ // CHUNK

    def body(q_ref, k_ref, v_ref, g_ref, b_ref, tf_ref, tb_ref, *rest):
        dins, dgb_ref, (dqkv_ref, dg_ref, db_ref) = rest[:2 * _A_OUT], rest[2 * _A_OUT], rest[2 * _A_OUT + 1:]
        tsaved = tuple((tf_ref, tb_ref)[d][0, h] for d, h in _PROBLEMS)
        _, vjp = jax.vjp(functools.partial(_dn_chunk, _SGNS, tsaved=tsaved),
                         *_chunk_inputs(q_ref, k_ref, v_ref, g_ref, b_ref))
        cots = []
        for o in range(_A_OUT):
            cots.append(tuple(dins[d * _A_OUT + o][0, h] if o == 2 else dins[d * _A_OUT + o][:, _hs(h)]
                              for d, h in _PROBLEMS))
        gq, gk, gv, gg, gb = vjp(tuple(cots))
        for p, (d, h) in enumerate(_PROBLEMS):
            dg_ref[d, 0, h:h + 1, :] = gg[p] + dgb_ref[d, 0, h:h + 1, :]
            db_ref[d, 0, h:h + 1, :] = gb[p]
        for h in range(DN_HEADS):
            dqkv_ref[:, _hs(h)] = gq[h] + gq[DN_HEADS + h]
            dqkv_ref[:, _hs(DN_HEADS + h)] = gk[h] + gk[DN_HEADS + h]
            dqkv_ref[:, _hs(2 * DN_HEADS + h)] = gv[h] + gv[DN_HEADS + h]

    rspec = pl.BlockSpec((2, 1, DN_HEADS, CHUNK), lambda c: (0, c, 0, 0))
    big = pl.BlockSpec((CHUNK, D), lambda c: (c, 0))
    qks = pl.BlockSpec((1, DN_HEADS, CHUNK, CHUNK), lambda c: (c, 0, 0, 0))
    rsh = jax.ShapeDtypeStruct(grows.shape, F32)
    return pl.pallas_call(
        body, name="dn_a_bwd", grid=(nch,),
        in_specs=[pl.BlockSpec((CHUNK, D), lambda c, p=p: (c, p)) for p in range(3)] + [rspec, rspec, qks, qks]
        + [big, big, qks, big, big] * 2 + [rspec],
        out_specs=[pl.BlockSpec((CHUNK, 3 * D), lambda c: (c, 0)), rspec, rspec],
        out_shape=[jax.ShapeDtypeStruct((t_, 3 * D), F32), rsh, rsh],
        compiler_params=_params(1))(qkv, qkv, qkv, grows, brows, *tinv, *dres, dg_b)


def _dir_specs(nch):
    def cidx(d):
        return (lambda n: n) if d == 0 else (lambda n: nch - 1 - n)
    out = []
    for d in range(2):
        ci = cidx(d)
        big = pl.BlockSpec((CHUNK, D), lambda n, ci=ci: (ci(n), 0))
        qks = pl.BlockSpec((1, DN_HEADS, CHUNK, CHUNK), lambda n, ci=ci: (ci(n), 0, 0, 0))
        row = pl.BlockSpec((1, 1, DN_HEADS, CHUNK), lambda n, ci=ci, d=d: (d, ci(n), 0, 0))
        st = pl.BlockSpec((1, DN_HEADS, DN_DIM, DN_DIM), lambda n, ci=ci: (ci(n), 0, 0, 0))
        out.append(dict(big=big, qk=qks, row=row, st=st))
    return out


def _step_inputs(ins, per_dir):
    def pick(o):
        if o == 2:
            return tuple(ins[d * per_dir + o][0, h].astype(F32) for d, h in _PROBLEMS)
        if o == 5:
            return tuple(ins[d * per_dir + o][0, 0, h:h + 1, :] for d, h in _PROBLEMS)
        return tuple(ins[d * per_dir + o][:, _hs(h)].astype(F32) for d, h in _PROBLEMS)
    return [pick(o) for o in range(6)]


def _dn_b_fwd(ares, grows):
    t_ = ares[0].shape[0]
    nch = t_ // CHUNK
    sp = _dir_specs(nch)

    def body(*refs):
        ins, outs, s_ref = refs[:12], refs[12:16], refs[16]

        @pl.when(pl.program_id(0) == 0)
        def _():
            s_ref[...] = jnp.zeros_like(s_ref)

        ss = tuple(s_ref[p] for p in range(len(_PROBLEMS)))
        os_, s2s = _dn_step(*_step_inputs(ins, 6), ss)
        for p, (d, h) in enumerate(_PROBLEMS):
            outs[2 + d][0, h] = ss[p]
            outs[d][:, _hs(h)] = os_[p]
            s_ref[p] = s2s[p]

    in_specs, args = [], []
    for d in range(2):
        in_specs += [sp[d]["big"], sp[d]["big"], sp[d]["qk"], sp[d]["big"], sp[d]["big"], sp[d]["row"]]
        args += list(ares[d * _A_OUT:(d + 1) * _A_OUT]) + [grows]
    stsh = jax.ShapeDtypeStruct((nch, DN_HEADS, DN_DIM, DN_DIM), F32)
    osh = jax.ShapeDtypeStruct((t_, D), F32)
    return pl.pallas_call(
        body, name="dn_b_fwd", grid=(nch,), in_specs=in_specs,
        out_specs=[sp[0]["big"], sp[1]["big"], sp[0]["st"], sp[1]["st"]], out_shape=[osh, osh, stsh, stsh],
        scratch_shapes=[pltpu.VMEM((2 * DN_HEADS, DN_DIM, DN_DIM), F32)],
        compiler_params=_params(1))(*args)


def _dn_b_bwd(ares, grows, st_f, st_b, do):
    t_ = ares[0].shape[0]
    nch = t_ // CHUNK
    sp = _dir_specs(nch)
    rsp = [sp[1], sp[0]]

    def body(*refs):
        ins, outs, ds_ref = refs[:16], refs[16:28], refs[28]

        @pl.when(pl.program_id(0) == 0)
        def _():
            ds_ref[...] = jnp.zeros_like(ds_ref)

        ss = tuple(ins[d * 8 + 6][0, h] for d, h in _PROBLEMS)
        _, vjp = jax.vjp(_dn_step, *_step_inputs(ins, 8), ss)
        dos = tuple(ins[d * 8 + 7][:, _hs(h)] for d, h in _PROBLEMS)
        grads = vjp((dos, tuple(ds_ref[p] for p in range(len(_PROBLEMS)))))
        for p, (d, h) in enumerate(_PROBLEMS):
            du_ref, dw_ref, dqk_ref, dqd_ref, dkd_ref, dg_ref = outs[d * 6:(d + 1) * 6]
            du_ref[:, _hs(h)], dw_ref[:, _hs(h)], dqk_ref[0, h] = grads[0][p], grads[1][p], grads[2][p]
            dqd_ref[:, _hs(h)], dkd_ref[:, _hs(h)] = grads[3][p], grads[4][p]
            dg_ref[0, 0, h:h + 1, :] = grads[5][p]
            ds_ref[p] = grads[6][p]

    in_specs, args, out_specs, out_shape = [], [], [], []
    big_sh = jax.ShapeDtypeStruct((t_, D), F32)
    qk_sh = jax.ShapeDtypeStruct((nch, DN_HEADS, CHUNK, CHUNK), F32)
    row_sh = jax.ShapeDtypeStruct((1, nch, DN_HEADS, CHUNK), F32)
    for d in range(2):
        s = rsp[d]
        row0 = pl.BlockSpec((1, 1, DN_HEADS, CHUNK), lambda m, d=d: (0, (nch - 1 - m) if d == 0 else m, 0, 0))
        rowd = pl.BlockSpec((1, 1, DN_HEADS, CHUNK), lambda m, d=d: (d, (nch - 1 - m) if d == 0 else m, 0, 0))
        in_specs += [s["big"], s["big"], s["qk"], s["big"], s["big"], rowd, s["st"], s["big"]]
        args += list(ares[d * _A_OUT:(d + 1) * _A_OUT]) + [grows, (st_f, st_b)[d], do]
        out_specs += [s["big"], s["big"], s["qk"], s["big"], s["big"], row0]
        out_shape += [big_sh, big_sh, qk_sh, big_sh, big_sh, row_sh]
    res = pl.pallas_call(
        body, name="dn_b_bwd", grid=(nch,), in_specs=in_specs, out_specs=out_specs, out_shape=out_shape,
        scratch_shapes=[pltpu.VMEM((2 * DN_HEADS, DN_DIM, DN_DIM), F32)],
        compiler_params=_params(1))(*args)
    dares = list(res[0:5]) + list(res[6:11])
    return dares, jnp.concatenate([res[5], res[11]], axis=0)


@jax.custom_vjp
def _rot_half(x):
    half, width = SW_DIM // 2, x.shape[1]
    first = lax.broadcasted_iota(jnp.int32, x.shape, 1) % SW_DIM < half
    return jnp.where(first, -pltpu.roll(x, width - half, axis=1), pltpu.roll(x, half, axis=1))


_rot_half.defvjp(lambda x: (_rot_half(x), None), lambda _, g: (-_rot_half(g),))


def _rope(x, c, s):
    reps = x.shape[1] // c.shape[1]
    return x * jnp.tile(c, (1, reps)) + _rot_half(x) * jnp.tile(s, (1, reps))


def _rope_t(g, c, s):
    reps = g.shape[1] // c.shape[1]
    return g * jnp.tile(c, (1, reps)) - _rot_half(g * jnp.tile(s, (1, reps)))


_SW_SCALE = SW_DIM ** -0.5
_KV_HEADS = [[kvh * SW_GRP + g for g in range(SW_GRP)] for kvh in range(SW_KV)]


def _by_group(x):
    return [jnp.concatenate([x[:, h * SW_DIM:(h + 1) * SW_DIM] for h in hs], axis=0) for hs in _KV_HEADS]


def _from_groups(xs):
    return jnp.concatenate([x[g * SW_BLOCK:(g + 1) * SW_BLOCK] for x in xs for g in range(SW_GRP)], axis=-1)


def _attn_probs(blk, t_, cq, sq, ck, sk, q, kall, sinks):
    qgs = _by_group(_rope(q, cq, sq))
    kr = _rope(kall, ck, sk)
    khs = [kr[:, kvh * SW_DIM:(kvh + 1) * SW_DIM] for kvh in range(SW_KV)]
    nq, nk = SW_GRP * SW_BLOCK, 3 * SW_BLOCK
    qpos = lax.broadcasted_iota(jnp.int32, (nq, nk), 0) % SW_BLOCK
    krel = lax.broadcasted_iota(jnp.int32, (nq, nk), 1) - SW_BLOCK
    kglob = krel + blk * SW_BLOCK
    valid = (jnp.abs(qpos - krel) <= SW_BLOCK) & (kglob >= 0) & (kglob < t_)
    ss = [jnp.where(valid, _nt(qg * _SW_SCALE, kh), -1e30) for qg, kh in zip(qgs, khs)]
    snks = [jnp.concatenate([jnp.broadcast_to(sinks[:, h:h + 1], (SW_BLOCK, 1)) for h in hs], axis=0) for hs in _KV_HEADS]
    ms = [jnp.maximum(jnp.max(s, axis=-1, keepdims=True), snk) for s, snk in zip(ss, snks)]
    es = [jnp.exp(s - m) for s, m in zip(ss, ms)]
    esnks = [jnp.exp(snk - m) for snk, m in zip(snks, ms)]
    invs = [1.0 / (jnp.sum(e, axis=-1, keepdims=True) + esnk) for e, esnk in zip(es, esnks)]
    ps = [e * inv for e, inv in zip(es, invs)]
    return qgs, khs, ps, [esnk * inv for esnk, inv in zip(esnks, invs)]


def _attn_f(blk, t_, cq, sq, ck, sk, q, kp, ko, kn, vp, vo, vn, sinks):
    _, _, ps, _ = _attn_probs(blk, t_, cq, sq, ck, sk, q, jnp.concatenate([kp, ko, kn], axis=0), sinks)
    vall = jnp.concatenate([vp, vo, vn], axis=0)
    return _from_groups([_nn(p, vall[:, kvh * SW_DIM:(kvh + 1) * SW_DIM]) for kvh, p in enumerate(ps)])


def _attn_bwd_f(blk, t_, cq, sq, ck, sk, q, kp, ko, kn, vp, vo, vn, sinks, do):
    qgs, khs, ps, psinks = _attn_probs(blk, t_, cq, sq, ck, sk, q, jnp.concatenate([kp, ko, kn], axis=0), sinks)
    vall = jnp.concatenate([vp, vo, vn], axis=0)
    vhs = [vall[:, kvh * SW_DIM:(kvh + 1) * SW_DIM] for kvh in range(SW_KV)]
    dogs = _by_group(do)
    dvs = [_tn(p, dog) for p, dog in zip(ps, dogs)]
    dps = [_nt(dog, vh) for dog, vh in zip(dogs, vhs)]
    deltas = [jnp.sum(p * dp, axis=-1, keepdims=True) for p, dp in zip(ps, dps)]
    dss = [p * ((dp - delta) * _SW_SCALE) for p, dp, delta in zip(ps, dps, deltas)]
    dqr = _from_groups([_nn(ds, kh) for ds, kh in zip(dss, khs)])
    dkr = jnp.concatenate([_tn(ds, qg) for ds, qg in zip(dss, qgs)], axis=-1)
    dsnk = [-(psink * delta) for psink, delta in zip(psinks, deltas)]
    dsinks = jnp.concatenate([jnp.sum(d[g * SW_BLOCK:(g + 1) * SW_BLOCK], axis=0, keepdims=True)
                              for d in dsnk for g in range(SW_GRP)], axis=1)
    dq, dk, dv = _rope_t(dqr, cq, sq), _rope_t(dkr, ck, sk), jnp.concatenate(dvs, axis=-1)
    blocks = lambda a: [a[j * SW_BLOCK:(j + 1) * SW_BLOCK] for j in range(3)]
    return [dq] + blocks(dk) + blocks(dv) + [dsinks]


def _attn_specs(nb):
    prv = lambda i: jnp.maximum(i - 1, 0)
    nxt = lambda i: jnp.minimum(i + 1, nb - 1)
    rows = [lambda i: i, prv, lambda i: i, nxt]
    tab = [pl.BlockSpec((SW_BLOCK, 128), lambda i, r=r: (r(i), 0)) for r in rows]
    qs = pl.BlockSpec((SW_BLOCK, SW_HEADS * SW_DIM), lambda i: (i, O_QS // (SW_HEADS * SW_DIM)))
    kw = SW_KV * SW_DIM
    ks = [pl.BlockSpec((SW_BLOCK, kw), lambda i, r=r: (r(i), O_KS // kw)) for r in rows[1:]]
    vs = [pl.BlockSpec((SW_BLOCK, kw), lambda i, r=r: (r(i), O_VS // kw)) for r in rows[1:]]
    return tab, qs, ks, vs


def _attn_tables(refs):
    cq, cp, co, cn, sq, sp_, so, sn = [r[...] for r in refs]
    return cq, sq, jnp.concatenate([cp, co, cn], axis=0), jnp.concatenate([sp_, so, sn], axis=0)


def _attn_fwd(pm, cos, sin, sinks):
    t_ = pm.shape[0]
    nb = t_ // SW_BLOCK
    tab, qs, ks, vs = _attn_specs(nb)

    def body(*refs):
        tabs = _attn_tables(refs[:8])
        vals = [r[...] for r in refs[8:16]]
        refs[16][...] = _attn_f(pl.program_id(0), t_, *tabs, *vals).astype(refs[16].dtype)

    return pl.pallas_call(
        body, name="attn_fwd", grid=(nb,), in_specs=tab + tab + [qs] + ks + vs + [_full(sinks)],
        out_specs=pl.BlockSpec((SW_BLOCK, D), lambda i: (i, 0)), out_shape=jax.ShapeDtypeStruct((t_, D), _MXU),
        compiler_params=_params(1))(*([cos] * 4), *([sin] * 4), pm, pm, pm, pm, pm, pm, pm, sinks)


def _attn_bwd(pm, cos, sin, sinks, do, dpm):
    t_ = pm.shape[0]
    nb = t_ // SW_BLOCK
    tab, qs, ks, vs = _attn_specs(nb)
    kw = SW_KV * SW_DIM

    def body(*refs):
        tabs = _attn_tables(refs[:8])
        vals = [r[...] for r in refs[8:16]]
        do_ref, outs = refs[16], refs[18:]
        grads = _attn_bwd_f(pl.program_id(0), t_, *tabs, *vals, do_ref[...])
        for o_ref, g in zip(outs[:7], grads[:7]):
            o_ref[...] = g.astype(o_ref.dtype)

        @pl.when(pl.program_id(0) == 0)
        def _():
            outs[7][...] = jnp.zeros_like(outs[7])
        outs[7][...] += grads[7]

    own = lambda w: pl.BlockSpec((SW_BLOCK, w), lambda i: (i, 0))
    return pl.pallas_call(
        body, name="attn_bwd", grid=(nb,),
        in_specs=tab + tab + [qs] + ks + vs + [_full(sinks), own(D), ANY],
        out_specs=[pl.BlockSpec((SW_BLOCK, D), lambda i: (i, O_QS // D))] + [own(kw)] * 6 + [_full(sinks)],
        out_shape=[jax.ShapeDtypeStruct(dpm.shape, dpm.dtype)] + [jax.ShapeDtypeStruct((t_, kw), F32)] * 6
        + [jax.ShapeDtypeStruct(sinks.shape, F32)],
        input_output_aliases={17: 0},
        compiler_params=_params(1))(*([cos] * 4), *([sin] * 4), pm, pm, pm, pm, pm, pm, pm, sinks, do, dpm)


def _band_sum(kparts, vparts, dpm):
    t_, kw = kparts[1].shape
    nb = t_ // SW_BLOCK

    def body(kp, ko, kn, vp, vo, vn, _, out_ref):
        j = pl.program_id(0)
        band = lambda p, o, n: o[...] + jnp.where(j + 1 < nb, p[...], 0.0) + jnp.where(j > 0, n[...], 0.0)
        out_ref[...] = jnp.concatenate([band(kp, ko, kn), band(vp, vo, vn)], axis=-1).astype(out_ref.dtype)

    specs = [pl.BlockSpec((SW_BLOCK, kw), lambda j: (jnp.minimum(j + 1, nb - 1), 0)),
             pl.BlockSpec((SW_BLOCK, kw), lambda j: (j, 0)),
             pl.BlockSpec((SW_BLOCK, kw), lambda j: (jnp.maximum(j - 1, 0), 0))]
    return pl.pallas_call(
        body, name="band_sum", grid=(nb,), in_specs=specs * 2 + [ANY],
        out_specs=pl.BlockSpec((SW_BLOCK, 2 * kw), lambda j: (j, O_KS // (2 * kw))),
        out_shape=jax.ShapeDtypeStruct(dpm.shape, dpm.dtype), input_output_aliases={6: 0},
        compiler_params=_params(1))(*kparts, *vparts, dpm)


def _loss_head(y, target, tb=256):
    t_ = y.shape[0]

    def body(y_ref, t_ref, dy_ref, acc_ref):
        err = y_ref[...] - t_ref[...]
        dy_ref[...] = err * (1.0 / D)
        sq = (err * err).reshape(tb // 8, 8, D).sum(axis=0)
        part = sq[:, 0:128]
        for c in range(1, D // 128):
            part = part + sq[:, c * 128:(c + 1) * 128]

        @pl.when(pl.program_id(0) == 0)
        def _():
            acc_ref[...] = jnp.zeros_like(acc_ref)
        acc_ref[...] += part

    row = pl.BlockSpec((tb, D), lambda i: (i, 0))
    return pl.pallas_call(
        body, name="loss_head", grid=(t_ // tb,), in_specs=[row, row],
        out_specs=[row, pl.BlockSpec((8, 128), lambda i: (0, 0))],
        out_shape=[jax.ShapeDtypeStruct((t_, D), F32), jax.ShapeDtypeStruct((8, 128), F32)],
        compiler_params=_params(1))(y, target)


def _adam_math(w, g, m, v):
    bc1 = 1.0 - ADAM_B1 ** ADAM_STEP
    bc2 = 1.0 - ADAM_B2 ** ADAM_STEP
    nm = ADAM_B1 * m + (1.0 - ADAM_B1) * g
    nv = ADAM_B2 * v + (1.0 - ADAM_B2) * (g * g)
    return -ADAM_LR * ((nm / bc1) / (jnp.sqrt(nv / bc2) + ADAM_EPS) + ADAM_WD * w), nm, nv


def _adamw_layer(name, l, w, g, m, v, carry, after):
    nl, r, c = w.shape
    g2 = g.reshape(r, -1)
    gc = g2.shape[1]
    w2, m2, v2 = [a.reshape(nl * r, c) for a in (w, m, v)]
    tb = r
    while tb * gc * 4 > (1 << 20) and tb % 16 == 0:
        tb //= 2
    nb = r // tb

    def body(w_ref, g_ref, m_ref, v_ref, *rest):
        go_ref, d_ref, nm_ref, nv_ref = rest[-4:]
        gv = g_ref[...][:, :c]
        go_ref[...] = gv
        d_ref[...], nm_ref[...], nv_ref[...] = _adam_math(w_ref[...], gv, m_ref[...], v_ref[...])

    spec = pl.BlockSpec((tb, c), lambda i: (l * nb + i, 0))
    carried = list(carry) if carry is not None else []
    outs = pl.pallas_call(
        body, name=name, grid=(nb,),
        in_specs=[spec, pl.BlockSpec((tb, gc), lambda i: (i, 0)), spec, spec] + [ANY] * (len(carried) + 1),
        out_specs=[spec] * 4, out_shape=[jax.ShapeDtypeStruct((nl * r, c), F32)] * 4,
        input_output_aliases={4 + k: k for k in range(len(carried))},
        compiler_params=_params(1))(w2, g2, m2, v2, *[a.reshape(nl * r, c) for a in carried], after)
    return tuple(o.reshape(w.shape) for o in outs)


def _adamw(name, w, g, m, v):
    shape = w.shape
    cols = shape[-1]
    rows = w.size // cols
    w2, g2, m2, v2 = [a.reshape(rows, cols) for a in (w, g, m, v)]
    tb = rows
    while tb * cols * 4 > (1 << 20) and tb % 16 == 0:
        tb //= 2

    def body(w_ref, g_ref, m_ref, v_ref, d_ref, nm_ref, nv_ref):
        d_ref[...], nm_ref[...], nv_ref[...] = _adam_math(w_ref[...], g_ref[...], m_ref[...], v_ref[...])

    spec = pl.BlockSpec((tb, cols), lambda i: (i, 0))
    sh = jax.ShapeDtypeStruct((rows, cols), F32)
    outs = pl.pallas_call(body, name=name, grid=(rows // tb,), in_specs=[spec] * 4, out_specs=[spec] * 3,
                          out_shape=[sh] * 3, compiler_params=_params(1))(w2, g2, m2, v2)
    return [o.reshape(shape) for o in outs]


def _to_rows(bg, col0):
    t_ = bg.shape[0]
    a = bg[:, col0:col0 + 2 * DN_HEADS].reshape(t_ // CHUNK, CHUNK, 2, DN_HEADS)
    return jnp.transpose(a, (2, 0, 3, 1))


def _from_rows(db, dg):
    nch = db.shape[1]
    back = lambda a: jnp.transpose(a, (1, 3, 0, 2)).reshape(nch * CHUNK, 2 * DN_HEADS)
    return jnp.pad(jnp.concatenate([back(db), back(dg)], axis=1), ((0, 0), (0, 128 - 4 * DN_HEADS)))


def _layer_fwd(x, xm, w, rest, cos, sin):
    t_ = x.shape[0]
    tb, nb = min(256, t_), min(1024, t_)
    pm = _mm(xm, w["in_main"], name="mm_in", tn=1536)
    pbg = _mm(xm, w["in_bg"], name="mm_in_bg")
    w = {**w, **rest(pbg)}
    qkv = _prep_fwd(pm, w["conv"])
    bg, = _stage_fwd(_bg_f, "bg_fwd", [(pbg, 128, 0, False)], [w["arow"], w["dtrow"]], [(128, F32)], nb)
    brows, grows = _to_rows(bg, 0), _to_rows(bg, 2 * DN_HEADS)
    ares, tinv = _dn_a_fwd(qkv, grows, brows)
    o_f, o_b, st_f, st_b = _dn_b_fwd(ares, grows)
    odn, = _stage_fwd(_gnorm_f, "gnorm_fwd", [(o_f, 128, 0, True), (o_b, 128, 0, True), (pm, 128, O_Z // 128, True)],
                      [w["gnw"]], [(128, _MXU)], nb, ncol=DN_HEADS)
    osw = _attn_fwd(pm, cos, sin, w["sinks"])
    same, first = (lambda j: j), (lambda j: 0)
    full = lambda dt: (D, D, dt, first, None)
    ya = _mm(odn, w["a"], name="mm_a")
    yb, merged = _mm_fused(
        osw, w["b"], lambda acc, ya_, ga, gb: (acc,) + _merge_f(ya_, acc, ga, gb), [full(F32), full(_MXU)],
        name="mm_b_merge", rows=[(ya, D, first), (pm, D, lambda j: O_GA // D), (pm, D, lambda j: O_GB // D)], tm=512)
    mix, x1, x1m = _mm_fused(
        merged, w["o"], lambda acc, x_, g_, b_: (acc,) + _ln_f2(x_, acc, g_, b_), [full(F32), full(F32), full(_MXU)],
        name="mm_o_ln", rows=[(x, D, first)], params=[w["ln1g"], w["ln1b"]], tm=512)
    gu, hid = _mm_fused(x1m, w["gu"], lambda acc: (acc,) + _swiglu_tile(acc),
                        [(2 * FFN, FFN, F32, same, None), (FFN, FFN // 2, _MXU, same, None)], name="mm_gu_swiglu", tn=FFN)
    ffn, x2, x2m = _mm_fused(
        hid, w["d"], lambda acc, x_, g_, b_: (acc,) + _ln_f2(x_, acc, g_, b_), [full(F32), full(F32), full(_MXU)],
        name="mm_d_ln", rows=[(x1, D, first)], params=[w["ln2g"], w["ln2b"]], tm=512, tk=FFN)
    res = dict(w=w, x=x, xm=xm, pm=pm, pbg=pbg, qkv=qkv, grows=grows, brows=brows, ares=ares, tinv=tinv, o_f=o_f, o_b=o_b, st_f=st_f,
               st_b=st_b, odn=odn, osw=osw, ya=ya, yb=yb, merged=merged, mix=mix, x1=x1, x1m=x1m, gu=gu, hid=hid,
               ffn=ffn)
    return x2, x2m, res


def _layer_bwd(dx2, r, w, cos, sin, mid=None, after=None):
    t_ = dx2.shape[0]
    tb, nb = min(256, t_), min(1024, t_)
    pm = r["pm"]
    g = {}
    dx1a, dffn, g["ln2g"], g["ln2b"] = _stage_bwd(
        _ln_f, "ln2_bwd", [(r["x1"], D, 0, False), (r["ffn"], D, 0, False)], [w["ln2g"], w["ln2b"]], [dx2], tb,
        dtypes=[F32, _MXU], after=after)
    same = lambda j: j
    dgu, = _mm_fused(dffn, w["d"], lambda dhid, gu: jax.vjp(_swiglu_tile, gu)[1]((dhid,)),
                     [(2 * FFN, FFN, _MXU, same, None)], name="mm_d_dx_swiglu", rows=[(r["gu"], FFN, same)], tb=True,
                     tn=FFN // 2)
    g["d"] = _mm(r["hid"], dffn, ta=True, name="mm_d_dw", tm=FFN // 2, out_dtype=_MXU)
    first = lambda j: 0
    full = lambda dt: (D, D, dt, first, None)

    def ln1_back(acc, dx1a_, x_, mix_, g_, b_):
        dx_, dmix_, dg_, db_ = jax.vjp(_ln_f, x_, mix_, g_, b_)[1]((acc + dx1a_,))
        return dx_, dmix_, dg_, db_

    dxa, dmix, g["ln1g"], g["ln1b"] = _mm_fused(
        dgu, w["gu"], ln1_back, [full(F32), full(_MXU)], name="mm_gu_dx_ln", tb=True, tm=256, tk=2 * FFN, n_sums=2,
        rows=[(dx1a, D, first), (r["x"], D, first), (r["mix"], D, first)], params=[w["ln1g"], w["ln1b"]])
    g["gu"] = _mm(r["x1m"], dgu, ta=True, name="mm_gu_dw", tn=FFN // 2, out_dtype=_MXU)
    tok = mid(g["gu"]) if mid is not None else None
    g["o"] = _mm(r["merged"], dmix, ta=True, name="mm_o_dw", after=tok, out_dtype=_MXU)

    def merge_back(acc, ya_, yb_, ga, gb):
        dya_, dyb_, dga, dgb = jax.vjp(_merge_f, ya_, yb_, ga, gb)[1]((acc,))
        return dya_, dyb_, jnp.concatenate([dga, dgb], axis=-1)

    dya, dyb, dpm = _mm_fused(
        dmix, w["o"], merge_back, [full(_MXU), full(_MXU), (N_MAIN, 2 * D, _MXU, lambda j: O_GA // (2 * D), None)],
        name="mm_o_dx_merge", tb=True, tm=512,
        rows=[(r["ya"], D, first), (r["yb"], D, first), (pm, D, lambda j: O_GA // D), (pm, D, lambda j: O_GB // D)])

    def gnorm_back(acc, of, ob, z, gnw):
        dos, dzs, dws = [], [], None
        for h in range(DN_HEADS):
            do_h, _, dz_h, dw_h = jax.vjp(_gnorm_f, of[:, _hs(h)], ob[:, _hs(h)], z[:, _hs(h)], gnw)[1]((acc[:, _hs(h)],))
            dos.append(do_h)
            dzs.append(dz_h)
            dws = dw_h if dws is None else dws + dw_h
        return jnp.concatenate(dos, axis=-1), jnp.concatenate(dzs, axis=-1), dws

    dof, dpm, g["gnw"] = _mm_fused(
        dya, w["a"], gnorm_back, [full(F32), (N_MAIN, D, _MXU, lambda j: O_Z // D, dpm)], name="mm_a_dx_gnorm", tb=True,
        tm=512, rows=[(r["o_f"], D, first), (r["o_b"], D, first), (pm, D, lambda j: O_Z // D)], params=[w["gnw"]],
        n_sums=1)
    g["a"] = _mm(r["odn"], dya, ta=True, name="mm_a_dw", out_dtype=_MXU)
    dosw = _mm(dyb, w["b"], tb=True, name="mm_b_dx")
    g["b"] = _mm(r["osw"], dyb, ta=True, name="mm_b_dw", out_dtype=_MXU)
    ab = _attn_bwd(pm, cos, sin, w["sinks"], dosw, dpm)
    dpm, g["sinks"] = ab[0], ab[7]
    dpm = _band_sum(ab[1:4], ab[4:7], dpm)
    dares, dg_b = _dn_b_bwd(r["ares"], r["grows"], r["st_f"], r["st_b"], dof)
    dqkv, dgrows, dbrows = _dn_a_bwd(r["qkv"], r["grows"], r["brows"], r["tinv"], dares, dg_b)
    dbg = _from_rows(dbrows, dgrows)
    dpbg, g["arow"], g["dtrow"] = _stage_bwd(_bg_f, "bg_bwd", [(r["pbg"], 128, 0, False)], [w["arow"], w["dtrow"]],
                                             [dbg], nb)
    dpm, g["conv"] = _prep_bwd(pm, w["conv"], dqkv, dpm)
    dx = _mm(dpm, w["in_main"], tb=True, add=dxa, name="mm_in_dx", tk=1920)
    dx = _mm(dpbg, w["in_bg"], tb=True, add=dx, name="mm_in_bg_dx")
    g["in_main"] = _mm(r["xm"], dpm, ta=True, name="mm_in_dw", tn=1536, out_dtype=_MXU)
    g["in_bg"] = _mm(r["xm"], dpbg, ta=True, name="mm_in_bg_dw", out_dtype=_MXU)
    return dx, g


def _place():
    return lax.axis_index("x"), lax.axis_index("y"), lax.axis_index("c")


def _colblock(kind, q):
    return q if kind == "col" else (q >> 1) | ((q & 1) << 1)


def _other_chips(x, y):
    return [(1 - x, y), (x, 1 - y), (1 - x, 1 - y)]


HBM = pl.BlockSpec(memory_space=pltpu.HBM)
SEM = pl.BlockSpec(memory_space=pltpu.SEMAPHORE)
DATAFLOW = pltpu.SideEffectType.DATAFLOW_SIDE_EFFECTING


def _hbm(a):
    return pltpu.HBM(a.shape, a.dtype)


def _gather_start(locs, kinds, name):
    n = len(locs)
    locs = list(locs)
    lands = [lax.empty((N_CHIPS,) + a.shape if kind == "row" else (a.shape[0], N_CHIPS * a.shape[1]), a.dtype)
             for a, kind in zip(locs, kinds)]

    def body(*refs):
        loc_refs, land_refs = refs[:n], refs[n:2 * n]
        send_sems, recv_sems, token = refs[2 * n:3 * n], refs[3 * n:4 * n], refs[-1]
        x, y, c = _place()
        myq = 2 * x + y
        for t in range(n):
            width = locs[t].shape[1]
            mine = (land_refs[t].at[myq] if kinds[t] == "row" else
                    land_refs[t].at[:, pl.ds(pl.multiple_of(_colblock(kinds[t], myq) * width, 128), width)])
            for dev in [(cx, cy, c) for cx, cy in _other_chips(x, y)] + [(x, y, 1 - c)]:
                pltpu.make_async_remote_copy(src_ref=loc_refs[t], dst_ref=mine, send_sem=send_sems[t],
                                             recv_sem=recv_sems[t], device_id=dev, device_id_type=MESH).start()
        token[...] = jnp.zeros_like(token)

    res = pl.pallas_call(
        body, name=name,
        out_shape=[pltpu.SemaphoreType.DMA(())] * (2 * n) + [_hbm(a) for a in locs + lands]
        + [jax.ShapeDtypeStruct((8, 128), F32)],
        in_specs=[HBM] * (2 * n), out_specs=[SEM] * (2 * n) + [HBM] * (2 * n) + [pl.BlockSpec(memory_space=pltpu.VMEM)],
        input_output_aliases={t: 2 * n + t for t in range(2 * n)},
        compiler_params=pltpu.CompilerParams(has_side_effects=DATAFLOW),
    )(*[pltpu.with_memory_space_constraint(a, pltpu.HBM) for a in locs + lands])
    return (res[:n], res[n:2 * n], res[2 * n:3 * n], res[3 * n:4 * n]), res[-1]


def _split_wait(handle, after, name):
    send_sems, recv_sems, srcs, lands = handle
    n = len(srcs)

    def body(*refs):
        land_refs, ssems, rsems = refs[n:2 * n], refs[2 * n:3 * n], refs[3 * n:4 * n]
        x, y, c = _place()
        for t in range(n):
            done = pltpu.make_async_remote_copy(
                src_ref=land_refs[t], dst_ref=land_refs[t], send_sem=ssems[t], recv_sem=rsems[t],
                device_id=(x, y, c), device_id_type=MESH)
            done.wait_send()
            done.wait_recv()

    res = pl.pallas_call(
        body, name=name, out_shape=[_hbm(a) for a in list(srcs) + list(lands)],
        in_specs=[HBM] * (2 * n) + [SEM] * (2 * n) + [ANY], out_specs=[HBM] * (2 * n),
        input_output_aliases={t: t for t in range(2 * n)},
        compiler_params=pltpu.CompilerParams(has_side_effects=DATAFLOW),
    )(*srcs, *lands, *send_sems, *recv_sems, after)
    return res[:n], res[n:]


RS_CHUNKS = 2


def _piece(ref, kind, q, hf, pr, pc):
    if kind == "row":
        return ref.at[pl.ds((2 * q + hf) * pr, pr), :]
    return ref.at[pl.ds(hf * pr, pr), pl.ds(pl.multiple_of(_colblock(kind, q) * pc, 128), pc)]


def _rs_sibling_start(ts, meta, name):
    n = len(ts)
    ts = list(ts)
    lands = [lax.empty((N_CHIPS, pr, pc), a.dtype) for a, (_, pr, pc) in zip(ts, meta)]

    def body(*refs):
        t_refs, land_refs = refs[:n], refs[n:2 * n]
        send_sems, recv_sems, token = refs[2 * n:3 * n], refs[3 * n:4 * n], refs[-1]
        x, y, c = _place()
        for t, (kind, pr, pc) in enumerate(meta):
            for q in range(N_CHIPS):
                pltpu.make_async_remote_copy(
                    src_ref=_piece(t_refs[t], kind, q, 1 - c, pr, pc), dst_ref=land_refs[t].at[q],
                    send_sem=send_sems[t], recv_sem=recv_sems[t], device_id=(x, y, 1 - c), device_id_type=MESH).start()
        token[...] = jnp.zeros_like(token)

    res = pl.pallas_call(
        body, name=name,
        out_shape=[pltpu.SemaphoreType.DMA(())] * (2 * n) + [_hbm(a) for a in ts + lands]
        + [jax.ShapeDtypeStruct((8, 128), F32)],
        in_specs=[HBM] * (2 * n), out_specs=[SEM] * (2 * n) + [HBM] * (2 * n) + [pl.BlockSpec(memory_space=pltpu.VMEM)],
        input_output_aliases={t: 2 * n + t for t in range(2 * n)},
        compiler_params=pltpu.CompilerParams(has_side_effects=DATAFLOW),
    )(*[pltpu.with_memory_space_constraint(a, pltpu.HBM) for a in ts + lands])
    return (res[:n], res[n:2 * n], res[2 * n:3 * n], res[3 * n:4 * n]), res[-1]


def _rs_add_sibling(ts, r1s, meta, c):
    n = len(ts)
    in_specs, out_specs, out_shape = [], [], []
    for kind, pr, pc in meta:
        rs = pr // RS_CHUNKS
        if kind == "row":
            in_specs.append(pl.BlockSpec((rs, pc), lambda q, r, c_ref: ((2 * q + c_ref[0]) * RS_CHUNKS + r, 0)))
        else:
            in_specs.append(pl.BlockSpec(
                (rs, pc), lambda q, r, c_ref, kind=kind: (c_ref[0] * RS_CHUNKS + r, _colblock(kind, q))))
    for kind, pr, pc in meta:
        sp = pl.BlockSpec((None, pr // RS_CHUNKS, pc), lambda q, r, c_ref: (q, r, 0))
        in_specs.append(sp)
        out_specs.append(sp)
        out_shape.append(jax.ShapeDtypeStruct((N_CHIPS, pr, pc), BF16))

    def body(c_ref, *refs):
        for t in range(n):
            refs[2 * n + t][...] = (refs[t][...].astype(F32) + refs[n + t][...].astype(F32)).astype(BF16)

    return pl.pallas_call(
        body, name="rs_add_sibling", out_shape=out_shape,
        grid_spec=pltpu.PrefetchScalarGridSpec(num_scalar_prefetch=1, grid=(N_CHIPS, RS_CHUNKS), in_specs=in_specs,
                                               out_specs=out_specs),
        compiler_params=_params(2))(c.reshape(1).astype(jnp.int32), *ts, *r1s)


def _rs_chips_start(ps, meta, name):
    n = len(ps)
    ps = list(ps)
    lands = [lax.empty((N_CHIPS - 1, pr, pc), p.dtype) for p, (_, pr, pc) in zip(ps, meta)]

    def body(*refs):
        p_refs, land_refs = refs[:n], refs[n:2 * n]
        send_sems, recv_sems, token = refs[2 * n:3 * n], refs[3 * n:4 * n], refs[-1]
        x, y, c = _place()
        for t in range(n):
            for j, (cx, cy) in enumerate(_other_chips(x, y)):
                pltpu.make_async_remote_copy(
                    src_ref=p_refs[t].at[2 * cx + cy], dst_ref=land_refs[t].at[j], send_sem=send_sems[t],
                    recv_sem=recv_sems[t], device_id=(cx, cy, c), device_id_type=MESH).start()
        token[...] = jnp.zeros_like(token)

    res = pl.pallas_call(
        body, name=name,
        out_shape=[pltpu.SemaphoreType.DMA(())] * (2 * n) + [_hbm(a) for a in ps + lands]
        + [jax.ShapeDtypeStruct((8, 128), F32)],
        in_specs=[HBM] * (2 * n), out_specs=[SEM] * (2 * n) + [HBM] * (2 * n) + [pl.BlockSpec(memory_space=pltpu.VMEM)],
        input_output_aliases={t: 2 * n + t for t in range(2 * n)},
        compiler_params=pltpu.CompilerParams(has_side_effects=DATAFLOW),
    )(*[pltpu.with_memory_space_constraint(a, pltpu.HBM) for a in ps + lands])
    return (res[:n], res[n:2 * n], res[2 * n:3 * n], res[3 * n:4 * n]), res[-1]


def _rs_add_chips(ps, r2s, meta, myq, c):
    n = len(ps)
    in_specs, out_specs, out_shape = [], [], []
    for _, pr, pc in meta:
        in_specs.append(pl.BlockSpec((None, pr // RS_CHUNKS, pc), lambda r, q_ref, c_ref: (q_ref[0], r, 0)))
    for _, pr, pc in meta:
        in_specs.append(pl.BlockSpec((N_CHIPS - 1, pr // RS_CHUNKS, pc), lambda r, q_ref, c_ref: (0, r, 0)))
        out_specs.append(pl.BlockSpec((None, pr // RS_CHUNKS, pc), lambda r, q_ref, c_ref: (c_ref[0], r, 0)))
        out_shape.append(jax.ShapeDtypeStruct((2, pr, pc), F32))

    def body(q_ref, c_ref, *refs):
        for t in range(n):
            r2 = refs[n + t]
            own = refs[t][...].astype(F32)
            refs[2 * n + t][...] = ((own + r2[0].astype(F32)) + r2[1].astype(F32)) + r2[2].astype(F32)

    return pl.pallas_call(
        body, name="rs_add_chips", out_shape=out_shape,
        grid_spec=pltpu.PrefetchScalarGridSpec(num_scalar_prefetch=2, grid=(RS_CHUNKS,), in_specs=in_specs,
                                               out_specs=out_specs),
        compiler_params=_params(1))(myq.reshape(1).astype(jnp.int32), c.reshape(1).astype(jnp.int32), *ps, *r2s)


def _rs_share_halves(gs):
    n = len(gs)

    def body(*refs):
        g_refs, send_sems, recv_sems = refs[n:2 * n], refs[2 * n], refs[2 * n + 1]
        x, y, c = _place()
        sib = (x, y, 1 - c)
        for t in range(n):
            pltpu.make_async_remote_copy(
                src_ref=g_refs[t].at[c], dst_ref=g_refs[t].at[c], send_sem=send_sems.at[t], recv_sem=recv_sems.at[t],
                device_id=sib, device_id_type=MESH).start()
        for t in range(n):
            cp = pltpu.make_async_remote_copy(
                src_ref=g_refs[t].at[c], dst_ref=g_refs[t].at[1 - c], send_sem=send_sems.at[t],
                recv_sem=recv_sems.at[t], device_id=sib, device_id_type=MESH)
            cp.wait_send()
            cp.wait_recv()

    return pl.pallas_call(
        body, name="rs_share_halves", in_specs=[ANY] * n, out_specs=[ANY] * n,
        out_shape=[jax.ShapeDtypeStruct(g.shape, g.dtype) for g in gs], input_output_aliases={t: t for t in range(n)},
        scratch_shapes=[pltpu.SemaphoreType.DMA((n,)), pltpu.SemaphoreType.DMA((n,))],
    )(*gs)


def _rs_middle(handle, after, meta, c, name):
    ts, r1s = _split_wait(handle, after, name + "_sib_wait")
    ps = _rs_add_sibling(ts, r1s, meta, c)
    return _rs_chips_start(ps, meta, name + "_start")


def _rs_end(handle, after, meta, c, myq, name):
    ps, r2s = _split_wait(handle, after, name + "_wait")
    return _rs_share_halves(_rs_add_chips(ps, r2s, meta, myq, c))


def _allreduce_small(buf):
    rows = buf.shape[0]
    ndev = 8

    def body(b_ref, o_ref, slots, send_sems, recv_sems):
        x, y, c = _place()
        me = 4 * x + 2 * y + c
        slots[me] = b_ref[...]
        for k in range(1, ndev):
            kx, ky, kc = (k >> 2) & 1, (k >> 1) & 1, k & 1
            peer = (x ^ kx, y ^ ky, c ^ kc)
            pltpu.make_async_remote_copy(
                src_ref=b_ref, dst_ref=slots.at[me], send_sem=send_sems.at[k - 1], recv_sem=recv_sems.at[k - 1],
                device_id=peer, device_id_type=MESH).start()
        for k in range(1, ndev):
            kx, ky, kc = (k >> 2) & 1, (k >> 1) & 1, k & 1
            cp = pltpu.make_async_remote_copy(
                src_ref=b_ref, dst_ref=slots.at[me ^ k], send_sem=send_sems.at[k - 1], recv_sem=recv_sems.at[k - 1],
                device_id=(x ^ kx, y ^ ky, c ^ kc), device_id_type=MESH)
            cp.wait_send()
            cp.wait_recv()
        acc = slots[0]
        for s in range(1, ndev):
            acc = acc + slots[s]
        o_ref[...] = acc

    vm = pl.BlockSpec(memory_space=pltpu.VMEM)
    return pl.pallas_call(
        body, name="allreduce_small", in_specs=[vm], out_specs=vm, out_shape=jax.ShapeDtypeStruct((rows, 128), F32),
        scratch_shapes=[pltpu.VMEM((ndev, rows, 128), F32), pltpu.SemaphoreType.DMA((ndev - 1,)),
                        pltpu.SemaphoreType.DMA((ndev - 1,))],
        compiler_params=pltpu.CompilerParams(vmem_limit_bytes=VMEM_LIMIT))(buf)


RS_META = [("col", D // 2, IN_PAD), ("row", D // 8, D), ("row", D // 8, D), ("row", D // 8, D),
           ("colx", D // 2, 2 * FFN // N_CHIPS), ("row", FFN // 8, D)]
SMALL_ROWS = 156


def _rope_tables(t_):
    half = SW_DIM // 2
    inv_freq = ROPE_THETA ** (-jnp.arange(half, dtype=F32) / half)
    ang = jnp.arange(t_, dtype=F32)[:, None] * inv_freq[None, :]
    reps = 128 // half
    return jnp.concatenate([jnp.cos(ang)] * reps, axis=1), jnp.concatenate([jnp.sin(ang)] * reps, axis=1)


def _orig_cols(padded, a, b):
    out = []
    for q in range(N_CHIPS):
        lo, hi = max(a, q * IN_SHARD), min(b, (q + 1) * IN_SHARD)
        if lo < hi:
            out.append(padded[:, q * IN_PAD + lo - q * IN_SHARD:q * IN_PAD + hi - q * IN_SHARD])
    return out


_ORIG_SEGMENTS = [(0, R_BG, "main", 0), (R_BG, R_SW, "bg", 0), (R_SW, R_GATES, "main", O_QS), (R_GATES, IN_COLS, "main", O_GA)]


def _to_padded_shards(main, bg):
    zeros = jnp.zeros((main.shape[0], IN_PAD - IN_SHARD), main.dtype)
    parts = []
    for q in range(N_CHIPS):
        for a, b, src, s0 in _ORIG_SEGMENTS:
            lo, hi = max(a, q * IN_SHARD), min(b, (q + 1) * IN_SHARD)
            if lo < hi:
                parts.append((main if src == "main" else bg)[:, s0 + lo - a:s0 + hi - a])
        parts.append(zeros)
    return jnp.concatenate(parts, axis=1)


def _lane_row(v16):
    return jnp.pad(v16.reshape(1, 2 * DN_HEADS), ((0, 0), (2 * DN_HEADS, 128 - 4 * DN_HEADS)))


def _pack_small(g):
    pad16 = jnp.pad(g["sinks"], ((0, 0), (0, 128 - SW_HEADS)))
    return jnp.concatenate([g["conv"].reshape(-1, 128), g["ln1g"].reshape(-1, 128), g["ln1b"].reshape(-1, 128),
                            g["ln2g"].reshape(-1, 128), g["ln2b"].reshape(-1, 128), g["gnw"], g["arow"], g["dtrow"],
                            pad16], axis=0)


def _unpack_small(buf):
    nconv = DN_CONV * 3 * D // 128
    o = nconv
    out = dict(conv=buf[:o].reshape(DN_CONV, 3 * D))
    for name in ("ln1g", "ln1b", "ln2g", "ln2b"):
        out[name] = buf[o:o + 8].reshape(D)
        o += 8
    out["gnw"] = buf[o]
    out["a_log"] = buf[o + 1, 2 * DN_HEADS:4 * DN_HEADS].reshape(2, DN_HEADS)
    out["dt_bias"] = buf[o + 2, 2 * DN_HEADS:4 * DN_HEADS].reshape(2, DN_HEADS)
    out["sinks"] = buf[o + 3, :SW_HEADS]
    return out


def kernel(x, w_in, conv_w, a_log, dt_bias, dn_norm_w, sinks, w_branch_a, w_branch_b, w_out, ln1_g, ln1_b, w_gate_up, w_down, ln2_g, ln2_b, loss_target, m_w_in, m_conv_w, m_a_log, m_dt_bias, m_dn_norm_w, m_sinks, m_w_branch_a, m_w_branch_b, m_w_out, m_ln1_g, m_ln1_b, m_w_gate_up, m_w_down, m_ln2_g, m_ln2_b, v_w_in, v_conv_w, v_a_log, v_dt_bias, v_dn_norm_w, v_sinks, v_w_branch_a, v_w_branch_b, v_w_out, v_ln1_g, v_ln1_b, v_w_gate_up, v_w_down, v_ln2_g, v_ln2_b):
    xi, yi, ci = _place()
    myq = 2 * xi + yi
    t_ = x.shape[1]
    cos, sin = _rope_tables(t_)

    loc_in = jnp.pad(w_in.astype(BF16), ((0, 0), (0, 0), (0, IN_PAD - IN_SHARD)))
    locs = [loc_in, w_branch_a.astype(BF16), w_branch_b.astype(BF16), w_out.astype(BF16), w_gate_up.astype(BF16),
            w_down.astype(BF16), conv_w]
    kinds = ["col", "row", "row", "row", "colx", "row", "col"]
    gathers = []
    for l in range(DEPTH):
        srcs = [a[l] for a in locs]
        if gathers:
            srcs[-1] = srcs[-1] + gathers[-1][1][0, 0]
        gathers.append(_gather_start(srcs, kinds, "gather_%d_start" % l))

    def in_weights(l, after):
        _, (full_in,) = _split_wait(tuple(part[:1] for part in gathers[l][0]), after, "gather_%d_wait_in" % l)
        cols = lambda a, b: _orig_cols(full_in, a, b)
        return dict(
            in_main=jnp.concatenate(cols(0, R_BG) + cols(R_GATES, IN_COLS) + cols(R_SW, R_GATES), axis=1),
            in_bg=jnp.pad(jnp.concatenate(cols(R_BG, R_SW), axis=1), ((0, 0), (0, 128 - 4 * DN_HEADS))))

    def rest_weights(l, after):
        _, (full_a, full_b, full_o, full_gu, full_d, full_conv) = _split_wait(
            tuple(part[1:] for part in gathers[l][0]), after, "gather_%d_wait_rest" % l)
        return dict(
            conv=full_conv, arow=_lane_row(a_log[l]), dtrow=_lane_row(dt_bias[l]), gnw=dn_norm_w[l][None],
            sinks=sinks[l][None], a=full_a.reshape(D, D), b=full_b.reshape(D, D), o=full_o.reshape(D, D),
            ln1g=ln1_g[l][None], ln1b=ln1_b[l][None], gu=full_gu, d=full_d.reshape(FFN, D),
            ln2g=ln2_g[l][None], ln2b=ln2_b[l][None])

    h = x[0]
    hm = h.astype(_MXU)
    residuals = []
    for l in range(DEPTH):
        win = in_weights(l, gathers[-1][1] if l == 0 else h)
        h, hm, res = _layer_fwd(h, hm, win, functools.partial(rest_weights, l), cos, sin)
        residuals.append(res)
    dh, sq = _loss_head(h, loss_target[0])
    loss = lax.psum((0.5 / D) * jnp.sum(sq), ("x", "y", "c"))

    big = [None] * DEPTH
    small = [None] * DEPTH
    hop1 = hop2 = None

    def mid(after):
        nonlocal hop1, hop2
        if hop1 is None:
            return None
        handle, tok = _rs_middle(hop1[1], after, RS_META, ci, "rs_chips_%d" % hop1[0])
        hop1, hop2 = None, (hop1[0], handle)
        return tok

    token = None
    for l in reversed(range(DEPTH)):
        dh, g = _layer_bwd(dh, residuals[l], residuals[l]["w"], cos, sin, mid, token)
        if hop2 is not None:
            big[hop2[0]] = _rs_end(hop2[1], dh, RS_META, ci, myq, "rs_chips_%d" % hop2[0])
        g_in = _to_padded_shards(g["in_main"], g["in_bg"])
        handle, token = _rs_sibling_start([g_in, g["a"], g["b"], g["o"], g["gu"], g["d"]], RS_META, "rs_sib_%d_start" % l)
        hop1 = (l, handle)
        small[l] = _pack_small(g)
    tot = _allreduce_small(jnp.concatenate(small, axis=0))
    token = mid(tot)
    pending = hop2
    sm = [_unpack_small(tot[l * SMALL_ROWS:(l + 1) * SMALL_ROWS]) for l in range(DEPTH)]
    stack = lambda name: jnp.stack([s[name] for s in sm], axis=0)
    grads = dict(
        conv_w=lax.dynamic_slice_in_dim(stack("conv"), myq * (3 * D // N_CHIPS), 3 * D // N_CHIPS, axis=2),
        a_log=stack("a_log"), dt_bias=stack("dt_bias"), dn_norm_w=stack("gnw"), sinks=stack("sinks"),
        ln1_g=stack("ln1g"), ln1_b=stack("ln1b"), ln2_g=stack("ln2g"), ln2_b=stack("ln2b"))
    weights = dict(w_in=w_in, conv_w=conv_w, a_log=a_log, dt_bias=dt_bias, dn_norm_w=dn_norm_w, sinks=sinks,
                   w_branch_a=w_branch_a, w_branch_b=w_branch_b, w_out=w_out, ln1_g=ln1_g, ln1_b=ln1_b,
                   w_gate_up=w_gate_up, w_down=w_down, ln2_g=ln2_g, ln2_b=ln2_b)
    ms = dict(w_in=m_w_in, conv_w=m_conv_w, a_log=m_a_log, dt_bias=m_dt_bias, dn_norm_w=m_dn_norm_w, sinks=m_sinks,
              w_branch_a=m_w_branch_a, w_branch_b=m_w_branch_b, w_out=m_w_out, ln1_g=m_ln1_g, ln1_b=m_ln1_b,
              w_gate_up=m_w_gate_up, w_down=m_w_down, ln2_g=m_ln2_g, ln2_b=m_ln2_b)
    vs = dict(w_in=v_w_in, conv_w=v_conv_w, a_log=v_a_log, dt_bias=v_dt_bias, dn_norm_w=v_dn_norm_w, sinks=v_sinks,
              w_branch_a=v_w_branch_a, w_branch_b=v_w_branch_b, w_out=v_w_out, ln1_g=v_ln1_g, ln1_b=v_ln1_b,
              w_gate_up=v_w_gate_up, w_down=v_w_down, ln2_g=v_ln2_g, ln2_b=v_ln2_b)
    names = list(weights)
    upd = {n: _adamw("adamw_" + n, weights[n], grads[n], ms[n], vs[n]) for n in grads}
    big_names = ["w_in", "w_branch_a", "w_branch_b", "w_out", "w_gate_up", "w_down"]
    carry = {n: None for n in big_names}

    def update_layer(l):
        for t, n in enumerate(big_names):
            carry[n] = _adamw_layer("adamw_" + n, l, weights[n], big[l][t], ms[n], vs[n], carry[n], token)

    for l in range(DEPTH - 1, pending[0], -1):
        update_layer(l)
    big[pending[0]] = _rs_end(pending[1], carry[big_names[-1]][1], RS_META, ci, myq, "rs_chips_%d" % pending[0])
    update_layer(pending[0])
    for n in big_names:
        grads[n], upd[n] = carry[n][0], carry[n][1:]
    return (loss, dh[None], *[grads[n] for n in names], *[upd[n][0] for n in names], *[upd[n][1] for n in names],
            *[upd[n][2] for n in names])
```

```python
import functools

import jax
import jax.numpy as jnp
from jax import lax
from jax.experimental import pallas as pl
from jax.experimental.pallas import tpu as pltpu

F32 = jnp.float32
BF16 = jnp.bfloat16
_MXU = BF16

D = 1024
DEPTH = 4
DN_HEADS = 8
DN_DIM = 128
DN_CONV = 5
CHUNK = 64
SW_HEADS = 16
SW_KV = 4
SW_DIM = 64
SW_GRP = SW_HEADS // SW_KV
SW_BLOCK = 128
ROPE_THETA = 10000.0
FFN = 2816
ALPHA = (2.0 * DEPTH) ** 0.25
LN_EPS = 1e-5
RMS_EPS = 1e-6
IN_COLS = 7712
O_Z, O_GA, O_GB, O_QS, O_KS, O_VS, N_MAIN = 3072, 4096, 5120, 6144, 7168, 7424, 7680
R_BG, R_SW, R_GATES = 4096, 4128, 5664
N_CHIPS = 4
IN_SHARD = IN_COLS // N_CHIPS
IN_PAD = 2048
ADAM_LR, ADAM_B1, ADAM_B2, ADAM_EPS, ADAM_WD, ADAM_STEP = 0.001, 0.9, 0.999, 1e-08, 0.01, 10
VMEM_LIMIT = 52 * 1024 * 1024
MESH = pl.DeviceIdType.MESH
ANY = pl.BlockSpec(memory_space=pl.ANY)


def _params(n_grid, **kw):
    return pltpu.CompilerParams(dimension_semantics=("arbitrary",) * n_grid, vmem_limit_bytes=VMEM_LIMIT, **kw)


def _full(a):
    nd = a.ndim
    return pl.BlockSpec(a.shape, lambda *_, nd=nd: (0,) * nd)


def _raw_dot(a, b, ca, cb):
    return lax.dot_general(a.astype(_MXU), b.astype(_MXU), (((ca,), (cb,)), ((), ())), preferred_element_type=F32)


@jax.custom_vjp
def _nn(a, b):
    return _raw_dot(a, b, 1, 0)


@jax.custom_vjp
def _nt(a, b):
    return _raw_dot(a, b, 1, 1)


@jax.custom_vjp
def _tn(a, b):
    if a.shape[1] > b.shape[1]:
        return _raw_dot(b, a, 0, 0).T
    return _raw_dot(a, b, 0, 0)


_nn.defvjp(lambda a, b: (_nn(a, b), (a, b)), lambda r, g: (_nt(g, r[1]), _tn(r[0], g)))
_nt.defvjp(lambda a, b: (_nt(a, b), (a, b)), lambda r, g: (_nn(g, r[1]), _tn(g, r[0])))
_tn.defvjp(lambda a, b: (_tn(a, b), (a, b)), lambda r, g: (_nt(r[1], g), _nn(r[0], g)))


def _hdot(a, b, ca=1, cb=0):
    ah, bh = a.astype(BF16), b.astype(BF16)
    al, bl = (a - ah.astype(F32)).astype(BF16), (b - bh.astype(F32)).astype(BF16)
    dot = lambda u, v: lax.dot_general(u, v, (((ca,), (cb,)), ((), ())), preferred_element_type=F32)
    return dot(ah, bh) + (dot(ah, bl) + dot(al, bh))


def _inv_impl(mats):
    n = mats[0].shape[0]
    eye = (lax.broadcasted_iota(jnp.int32, (n, n), 0) == lax.broadcasted_iota(jnp.int32, (n, n), 1)).astype(F32)
    ps = [-a for a in mats]
    ts = [eye + p for p in ps]
    for _ in range(max(1, (n - 1).bit_length()) - 1):
        ps = [_hdot(p, p) for p in ps]
        ts = [t + _hdot(t, p) for t, p in zip(ts, ps)]
    return tuple(ts)


@jax.custom_vjp
def _inv(mats):
    return _inv_impl(mats)


def _inv_fwd(mats):
    ts = _inv_impl(mats)
    return ts, ts


def _inv_bwd(ts, gs):
    xs = [_hdot(t, g, 0, 0) for t, g in zip(ts, gs)]
    return (tuple(-_hdot(x, t, 1, 1) for x, t in zip(xs, ts)),)


_inv.defvjp(_inv_fwd, _inv_bwd)


@jax.custom_vjp
def _inv_saved(mats, saved):
    return saved


_inv_saved.defvjp(lambda mats, saved: (saved, saved),
                  lambda ts, gs: (_inv_bwd(ts, gs)[0], tuple(jnp.zeros_like(t) for t in ts)))


def _tile(n, cap):
    if n <= cap:
        return n
    best = [t for t in range(128, cap + 1, 128) if n % t == 0]
    assert best, (n, cap)
    return best[-1]


def _mm(a, b, *, name, ta=False, tb=False, add=None, tm=1024, tn=1024, tk=1024, after=None, out_dtype=F32):
    if ta:
        k_, m_ = a.shape
    else:
        m_, k_ = a.shape
    n_ = b.shape[0] if tb else b.shape[1]
    tm, tn, tk = _tile(m_, tm), _tile(n_, tn), _tile(k_, tk)
    nk = k_ // tk
    has_add = add is not None

    def body(*refs):
        a_ref, b_ref = refs[:2]
        add_ref = refs[2] if has_add else None
        o_ref = refs[2 + has_add + (after is not None)]
        part = _raw_dot(a_ref[...], b_ref[...], 0 if ta else 1, 1 if tb else 0)
        if nk == 1:
            o_ref[...] = (part + add_ref[...] if has_add else part).astype(o_ref.dtype)
            return
        acc = refs[-1]
        k = pl.program_id(2)

        @pl.when(k == 0)
        def _():
            acc[...] = part

        @pl.when(jnp.logical_and(k > 0, k < nk - 1))
        def _():
            acc[...] += part

        @pl.when(k == nk - 1)
        def _():
            o_ref[...] = (acc[...] + part + add_ref[...] if has_add else acc[...] + part).astype(o_ref.dtype)

    a_spec = pl.BlockSpec((tk, tm), lambda i, j, k: (k, i)) if ta else pl.BlockSpec((tm, tk), lambda i, j, k: (i, k))
    b_spec = pl.BlockSpec((tn, tk), lambda i, j, k: (j, k)) if tb else pl.BlockSpec((tk, tn), lambda i, j, k: (k, j))
    o_spec = pl.BlockSpec((tm, tn), lambda i, j, k: (i, j))
    in_specs = [a_spec, b_spec] + ([o_spec] if has_add else []) + ([ANY] if after is not None else [])
    args = (a, b) + ((add,) if has_add else ()) + ((after,) if after is not None else ())
    return pl.pallas_call(
        body, name=name, grid=(m_ // tm, n_ // tn, nk), in_specs=in_specs, out_specs=o_spec,
        out_shape=jax.ShapeDtypeStruct((m_, n_), out_dtype),
        scratch_shapes=[pltpu.VMEM((tm, tn), F32)] if nk > 1 else [],
        compiler_params=_params(3))(*args)


def _mm_fused(a, b, post, outs, *, name, rows=(), params=(), n_sums=0, tb=False, tm=1024, tn=1024, tk=1024):
    m_, k_ = a.shape
    n_ = b.shape[0] if tb else b.shape[1]
    tm, tn, tk = _tile(m_, tm), _tile(n_, tn), _tile(k_, tk)
    nk = k_ // tk
    nr, npar, no = len(rows), len(params), len(outs)
    aliased = [o[4] for o in outs if o[4] is not None]
    nin = 2 + nr + npar

    def body(*refs):
        a_ref, b_ref = refs[:2]
        row_refs, par_refs = refs[2:2 + nr], refs[2 + nr:nin]
        out_refs = refs[nin + len(aliased):nin + len(aliased) + no + n_sums]
        part = _raw_dot(a_ref[...], b_ref[...], 1, 1 if tb else 0)
        k = pl.program_id(2)
        first = jnp.logical_and(pl.program_id(0) == 0, pl.program_id(1) == 0)
        if nk > 1:
            acc = refs[-1]

            @pl.when(k == 0)
            def _():
                acc[...] = part

            @pl.when(jnp.logical_and(k > 0, k < nk - 1))
            def _():
                acc[...] += part

        @pl.when(k == nk - 1)
        def _():
            total = acc[...] + part if nk > 1 else part
            res = post(total, *[r[...].astype(F32) for r in row_refs], *[p[...] for p in par_refs])
            for o_ref, val in zip(out_refs[:no], res[:no]):
                o_ref[...] = val.astype(o_ref.dtype)
            for s_ref, val in zip(out_refs[no:], res[no:]):
                @pl.when(first)
                def _(s_ref=s_ref):
                    s_ref[...] = jnp.zeros_like(s_ref)
                s_ref[...] += val

    b_spec = pl.BlockSpec((tn, tk), lambda i, j, k: (j, k)) if tb else pl.BlockSpec((tk, tn), lambda i, j, k: (k, j))
    in_specs = [pl.BlockSpec((tm, tk), lambda i, j, k: (i, k)), b_spec]
    in_specs += [pl.BlockSpec((tm, w), lambda i, j, k, cb=cb: (i, cb(j))) for _, w, cb in rows]
    in_specs += [_full(p) for p in params] + [ANY] * len(aliased)
    out_specs = [pl.BlockSpec((tm, w), lambda i, j, k, cb=cb: (i, cb(j))) for _, w, _, cb, _ in outs]
    out_specs += [_full(p) for p in params[:n_sums]]
    out_shape = [jax.ShapeDtypeStruct((m_, tot), dt) for tot, _, dt, _, _ in outs]
    out_shape += [jax.ShapeDtypeStruct(p.shape, F32) for p in params[:n_sums]]
    aliases, pos = {}, nin
    for oi, o in enumerate(outs):
        if o[4] is not None:
            aliases[pos] = oi
            pos += 1
    return pl.pallas_call(
        body, name=name, grid=(m_ // tm, n_ // tn, nk), in_specs=in_specs, out_specs=out_specs, out_shape=out_shape,
        scratch_shapes=[pltpu.VMEM((tm, tn), F32)] if nk > 1 else [], input_output_aliases=aliases,
        compiler_params=_params(3))(a, b, *[r[0] for r in rows], *params, *aliased)


def _row_spec(tb, w, c0, percol):
    return pl.BlockSpec((tb, w), lambda i, j, c0=c0, pc=percol: (i, c0 + (j if pc else 0)))


def _stage_fwd(f, name, rows, params, outs, tb, ncol=1):
    t_ = rows[0][0].shape[0]
    nr, npar = len(rows), len(params)

    def body(*refs):
        res = f(*[r[...].astype(F32) for r in refs[:nr + npar]])
        for o_ref, val in zip(refs[nr + npar:], res):
            o_ref[...] = val.astype(o_ref.dtype)

    return pl.pallas_call(
        body, name=name, grid=(t_ // tb, ncol),
        in_specs=[_row_spec(tb, w, c0, pc) for (_, w, c0, pc) in rows] + [_full(p) for p in params],
        out_specs=[pl.BlockSpec((tb, w), lambda i, j: (i, j)) for w, _ in outs],
        out_shape=[jax.ShapeDtypeStruct((t_, w * ncol), dt) for w, dt in outs],
        compiler_params=_params(2))(*[r[0] for r in rows], *params)


def _into(dest, tb, width, t_, ncol, dtype):
    if dest is None:
        return pl.BlockSpec((tb, width), lambda i, j: (i, j)), jax.ShapeDtypeStruct((t_, width * ncol), dtype), None
    buf, total, c0 = dest
    return (pl.BlockSpec((tb, width), lambda i, j, c0=c0: (i, c0 + j)), jax.ShapeDtypeStruct((t_, total), dtype), buf)


def _stage_bwd(f, name, rows, params, douts, tb, ncol=1, cat=None, dtypes=None, dest=None, after=None):
    t_ = rows[0][0].shape[0]
    nr, npar, nd = len(rows), len(params), len(douts)
    cat = cat if cat is not None else [[r] for r in range(nr)]
    dtypes = dtypes if dtypes is not None else [F32] * len(cat)
    dest = dest or {}
    assert ncol == 1 or all(len(g) == 1 and rows[g[0]][3] for g in cat)
    nin = nr + npar + nd
    unread = [after] if after is not None else []

    def body(*refs):
        ins = [r[...].astype(F32) for r in refs[:nr + npar]]
        dvals = tuple(r[...].astype(F32) for r in refs[nr + npar:nin])
        out_refs = refs[nin + len(aliased) + len(unread):]
        _, vjp = jax.vjp(f, *ins)
        grads = vjp(dvals)
        for o_ref, grp in zip(out_refs[:len(cat)], cat):
            val = grads[grp[0]] if len(grp) == 1 else jnp.concatenate([grads[r] for r in grp], axis=-1)
            o_ref[...] = val.astype(o_ref.dtype)
        first = jnp.logical_and(pl.program_id(0) == 0, pl.program_id(1) == 0)
        for p_ref, gp in zip(out_refs[len(cat):], grads[nr:]):
            @pl.when(first)
            def _(p_ref=p_ref):
                p_ref[...] = jnp.zeros_like(p_ref)
            p_ref[...] += gp

    gw = [sum(rows[r][1] for r in grp) for grp in cat]
    out_specs, out_shape, aliased, aliases = [], [], [], {}
    for gi, (w, dt) in enumerate(zip(gw, dtypes)):
        spec, shape, buf = _into(dest.get(gi), tb, w, t_, ncol, dt)
        out_specs.append(spec)
        out_shape.append(shape)
        if buf is not None:
            aliases[nin + len(aliased)] = gi
            aliased.append(buf)
    return pl.pallas_call(
        body, name=name, grid=(t_ // tb, ncol),
        in_specs=[_row_spec(tb, w, c0, pc) for (_, w, c0, pc) in rows] + [_full(p) for p in params]
        + [pl.BlockSpec((tb, d.shape[1] // ncol), lambda i, j: (i, j)) for d in douts]
        + [ANY] * (len(aliased) + len(unread)),
        out_specs=out_specs + [_full(p) for p in params],
        out_shape=out_shape + [jax.ShapeDtypeStruct(p.shape, F32) for p in params],
        input_output_aliases=aliases,
        compiler_params=_params(2))(*[r[0] for r in rows], *params, *douts, *aliased, *unread)


def _ln_f(x, y, g, b):
    u = ALPHA * x + y
    c = u - jnp.mean(u, axis=-1, keepdims=True)
    var = jnp.mean(c * c, axis=-1, keepdims=True)
    return (c * lax.rsqrt(var + LN_EPS) * g + b,)


def _ln_f2(x, y, g, b):
    out, = _ln_f(x, y, g, b)
    return out, out


def _swiglu_tile(gu):
    half = gu.shape[1] // 2
    return (jax.nn.silu(gu[:, :half]) * gu[:, half:],)


def _merge_f(ya, yb, ga, gb):
    return (jax.nn.sigmoid(ga) * ya + jax.nn.sigmoid(gb) * yb,)


def _gnorm_f(of, ob, z, w):
    o = of + ob
    return (o * lax.rsqrt(jnp.mean(o * o, axis=-1, keepdims=True) + RMS_EPS) * w * jax.nn.silu(z),)


def _bg_f(x, arow, dtrow):
    lane = lax.broadcasted_iota(jnp.int32, x.shape, 1)
    beta = jax.nn.sigmoid(x)
    g = -jnp.exp(arow) * jax.nn.softplus(x + dtrow)
    return (jnp.where(lane < 16, beta, jnp.where(lane < 32, g, 0.0)),)


PREP_ROWS = 512
PAD = 8


def _prep_f(part, w, *wins):
    xc = wins[0] * w[0:1, :]
    for k in range(1, DN_CONV):
        xc = xc + wins[k] * w[k:k + 1, :]
    a = jax.nn.silu(xc)
    nrm = a * lax.rsqrt(jnp.sum(a * a, axis=-1, keepdims=True) + RMS_EPS)
    return jnp.where(part == 0, nrm * (DN_DIM ** -0.5), jnp.where(part == 1, nrm, a))


def _windows(pad_ref, r0, rows):
    return [pad_ref[PAD + r0 - 2 + k:PAD + r0 - 2 + k + rows, :] for k in range(DN_CONV)]


def _prep_fwd(pm, conv):
    t_ = pm.shape[0]
    rows = min(PREP_ROWS, t_)

    def body(x_ref, w_ref, o_ref, pad_ref):
        part = pl.program_id(0) // DN_HEADS
        pad_ref[0:PAD, :] = jnp.zeros((PAD, DN_DIM), F32)
        pad_ref[PAD + t_:2 * PAD + t_, :] = jnp.zeros((PAD, DN_DIM), F32)
        pad_ref[PAD:PAD + t_, :] = x_ref[...]
        w = w_ref[...]
        for r in range(t_ // rows):
            o_ref[r * rows:(r + 1) * rows, :] = _prep_f(part, w, *_windows(pad_ref, r * rows, rows))

    ncb = 3 * DN_HEADS
    return pl.pallas_call(
        body, name="prep_fwd", grid=(ncb,),
        in_specs=[pl.BlockSpec((t_, DN_DIM), lambda j: (0, j)), pl.BlockSpec((DN_CONV, DN_DIM), lambda j: (0, j))],
        out_specs=pl.BlockSpec((t_, DN_DIM), lambda j: (0, j)),
        out_shape=jax.ShapeDtypeStruct((t_, ncb * DN_DIM), F32),
        scratch_shapes=[pltpu.VMEM((t_ + 2 * PAD, DN_DIM), F32)],
        compiler_params=_params(1))(pm, conv)


def _prep_bwd(pm, conv, dout, dpm):
    t_ = pm.shape[0]
    rows = min(PREP_ROWS, t_)

    def body(x_ref, w_ref, d_ref, _, dx_ref, dw_ref, pad_ref, dpad_ref):
        part = pl.program_id(0) // DN_HEADS
        pad_ref[0:PAD, :] = jnp.zeros((PAD, DN_DIM), F32)
        pad_ref[PAD + t_:2 * PAD + t_, :] = jnp.zeros((PAD, DN_DIM), F32)
        pad_ref[PAD:PAD + t_, :] = x_ref[...]
        dpad_ref[...] = jnp.zeros_like(dpad_ref)
        w = w_ref[...]
        dw = jnp.zeros((DN_CONV, DN_DIM), F32)
        for r in range(t_ // rows):
            r0 = r * rows
            _, vjp = jax.vjp(functools.partial(_prep_f, part), w, *_windows(pad_ref, r0, rows))
            grads = vjp(d_ref[r0:r0 + rows, :])
            dw = dw + grads[0]
            for k in range(DN_CONV):
                lo = PAD + r0 - 2 + k
                dpad_ref[lo:lo + rows, :] += grads[1 + k]
        dx_ref[...] = dpad_ref[PAD:PAD + t_, :].astype(dx_ref.dtype)
        dw_ref[...] = dw

    ncb = 3 * DN_HEADS
    col = pl.BlockSpec((t_, DN_DIM), lambda j: (0, j))
    wsp = pl.BlockSpec((DN_CONV, DN_DIM), lambda j: (0, j))
    return pl.pallas_call(
        body, name="prep_bwd", grid=(ncb,), in_specs=[col, wsp, col, ANY], out_specs=[col, wsp],
        out_shape=[jax.ShapeDtypeStruct(dpm.shape, dpm.dtype), jax.ShapeDtypeStruct((DN_CONV, ncb * DN_DIM), F32)],
        scratch_shapes=[pltpu.VMEM((t_ + 2 * PAD, DN_DIM), F32), pltpu.VMEM((t_ + 2 * PAD, DN_DIM), F32)],
        input_output_aliases={3: 0}, compiler_params=_params(1))(pm, conv, dout, dpm)


def _dn_chunk(sgns, qs, ks, vs, grows, brows, tsaved=None, with_t=False):
    c = qs[0].shape[0]
    i = lax.broadcasted_iota(jnp.int32, (c, c), 0)
    j = lax.broadcasted_iota(jnp.int32, (c, c), 1)
    eye = i == j
    incl = {s: (i - j) * int(s) >= 0 for s in set(sgns)}
    strict = {s: (i - j) * int(s) > 0 for s in set(sgns)}
    gcs = [jnp.sum(jnp.where(incl[s], g, 0.0), axis=1, keepdims=True) for s, g in zip(sgns, grows)]
    grs = [jnp.sum(jnp.where(eye, gc, 0.0), axis=0, keepdims=True) for gc in gcs]
    bcs = [jnp.sum(jnp.where(eye, b, 0.0), axis=1, keepdims=True) for b in brows]
    gls = [jnp.sum(g, axis=1, keepdims=True) for g in grows]
    decs = [jnp.exp(jnp.where(incl[s], gc - gr, -1e30)) for s, gc, gr in zip(sgns, gcs, grs)]
    kks = [_nt(k, k) for k in ks]
    amats = tuple(jnp.where(strict[s], bc * kk * dec, 0.0) for s, bc, kk, dec in zip(sgns, bcs, kks, decs))
    tinvs = _inv(amats) if tsaved is None else _inv_saved(amats, tsaved)
    egcs = [jnp.exp(gc) for gc in gcs]
    us = [_nn(t, v * bc) for t, v, bc in zip(tinvs, vs, bcs)]
    ws = [_nn(t, k * (bc * egc)) for t, k, bc, egc in zip(tinvs, ks, bcs, egcs)]
    qks = [_nt(q, k) * dec for q, k, dec in zip(qs, ks, decs)]
    qds = [q * egc for q, egc in zip(qs, egcs)]
    kds = [k * jnp.exp(gl - gc) for k, gl, gc in zip(ks, gls, gcs)]
    res = tuple(us), tuple(ws), tuple(qks), tuple(qds), tuple(kds)
    return res + (tinvs,) if with_t else res


def _dn_step(us, ws, qks, qds, kds, grows, ss):
    gls = [jnp.exp(jnp.sum(g, axis=1, keepdims=True)) for g in grows]
    wss = [_nn(w, s) for w, s in zip(ws, ss)]
    qss = [_nn(qd, s) for qd, s in zip(qds, ss)]
    vns = [u - x for u, x in zip(us, wss)]
    os_ = [a + _nn(qk, vn) for a, qk, vn in zip(qss, qks, vns)]
    s2s = [s * gl + _tn(kd, vn) for s, gl, kd, vn in zip(ss, gls, kds, vns)]
    return tuple(os_), tuple(s2s)


def _hs(h):
    return slice(h * DN_DIM, (h + 1) * DN_DIM)


_DIR_SGN = (1, -1)
_A_OUT = 5
_PROBLEMS = [(d, h) for d in range(2) for h in range(DN_HEADS)]
_SGNS = [_DIR_SGN[d] for d, _ in _PROBLEMS]


def _chunk_inputs(q_ref, k_ref, v_ref, g_ref, b_ref):
    heads = lambda ref: tuple(ref[:, _hs(h)].astype(F32) for _, h in _PROBLEMS)
    rows = lambda ref: tuple(ref[d, 0, h:h + 1, :] for d, h in _PROBLEMS)
    return heads(q_ref), heads(k_ref), heads(v_ref), rows(g_ref), rows(b_ref)


def _dn_a_fwd(qkv, grows, brows):
    t_ = qkv.shape[0]
    nch = t_ // CHUNK

    def body(q_ref, k_ref, v_ref, g_ref, b_ref, *outs):
        us, ws, qks, qds, kds, tinvs = _dn_chunk(_SGNS, *_chunk_inputs(q_ref, k_ref, v_ref, g_ref, b_ref), with_t=True)
        for p, (d, h) in enumerate(_PROBLEMS):
            u_ref, w_ref, qk_ref, qd_ref, kd_ref = outs[d * _A_OUT:(d + 1) * _A_OUT]
            u_ref[:, _hs(h)], w_ref[:, _hs(h)], qk_ref[0, h] = us[p], ws[p].astype(_MXU), qks[p].astype(_MXU)
            qd_ref[:, _hs(h)], kd_ref[:, _hs(h)] = qds[p].astype(_MXU), kds[p].astype(_MXU)
            outs[2 * _A_OUT + d][0, h] = tinvs[p]

    rspec = pl.BlockSpec((2, 1, DN_HEADS, CHUNK), lambda c: (0, c, 0, 0))
    big = pl.BlockSpec((CHUNK, D), lambda c: (c, 0))
    qks = pl.BlockSpec((1, DN_HEADS, CHUNK, CHUNK), lambda c: (c, 0, 0, 0))
    bigs = lambda dt: jax.ShapeDtypeStruct((t_, D), dt)
    qksh = lambda dt: jax.ShapeDtypeStruct((nch, DN_HEADS, CHUNK, CHUNK), dt)
    res = pl.pallas_call(
        body, name="dn_a_fwd", grid=(nch,),
        in_specs=[pl.BlockSpec((CHUNK, D), lambda c, p=p: (c, p)) for p in range(3)] + [rspec, rspec],
        out_specs=[big, big, qks, big, big] * 2 + [qks, qks],
        out_shape=[bigs(F32), bigs(_MXU), qksh(_MXU), bigs(_MXU), bigs(_MXU)] * 2 + [qksh(F32)] * 2,
        compiler_params=_params(1))(qkv, qkv, qkv, grows, brows)
    return res[:2 * _A_OUT], res[2 * _A_OUT:]


def _dn_a_bwd(qkv, grows, brows, tinv, dres, dg_b):
    t_ = qkv.shape[0]
    nch = t_ // CHUNK

    def body(q_ref, k_ref, v_ref, g_ref, b_ref, tf_ref, tb_ref, *rest):
        dins, dgb_ref, (dqkv_ref, dg_ref, db_ref) = rest[:2 * _A_OUT], rest[2 * _A_OUT], rest[2 * _A_OUT + 1:]
        tsaved = tuple((tf_ref, tb_ref)[d][0, h] for d, h in _PROBLEMS)
        _, vjp = jax.vjp(functools.partial(_dn_chunk, _SGNS, tsaved=tsaved),
                         *_chunk_inputs(q_ref, k_ref, v_ref, g_ref, b_ref))
        cots = []
        for o in range(_A_OUT):
            cots.append(tuple(dins[d * _A_OUT + o][0, h] if o == 2 else dins[d * _A_OUT + o][:, _hs(h)]
                              for d, h in _PROBLEMS))
        gq, gk, gv, gg, gb = vjp(tuple(cots))
        for p, (d, h) in enumerate(_PROBLEMS):
            dg_ref[d, 0, h:h + 1, :] = gg[p] + dgb_ref[d, 0, h:h + 1, :]
            db_ref[d, 0, h:h + 1, :] = gb[p]
        for h in range(DN_HEADS):
            dqkv_ref[:, _hs(h)] = gq[h] + gq[DN_HEADS + h]
            dqkv_ref[:, _hs(DN_HEADS + h)] = gk[h] + gk[DN_HEADS + h]
            dqkv_ref[:, _hs(2 * DN_HEADS + h)] = gv[h] + gv[DN_HEADS + h]

    rspec = pl.BlockSpec((2, 1, DN_HEADS, CHUNK), lambda c: (0, c, 0, 0))
    big = pl.BlockSpec((CHUNK, D), lambda c: (c, 0))
    qks = pl.BlockSpec((1, DN_HEADS, CHUNK, CHUNK), lambda c: (c, 0, 0, 0))
    rsh = jax.ShapeDtypeStruct(grows.shape, F32)
    return pl.pallas_call(
        body, name="dn_a_bwd", grid=(nch,),
        in_specs=[pl.BlockSpec((CHUNK, D), lambda c, p=p: (c, p)) for p in range(3)] + [rspec, rspec, qks, qks]
        + [big, big, qks, big, big] * 2 + [rspec],
        out_specs=[pl.BlockSpec((CHUNK, 3 * D), lambda c: (c, 0)), rspec, rspec],
        out_shape=[jax.ShapeDtypeStruct((t_, 3 * D), F32), rsh, rsh],
        compiler_params=_params(1))(qkv, qkv, qkv, grows, brows, *tinv, *dres, dg_b)


def _dir_specs(nch):
    def cidx(d):
        return (lambda n: n) if d == 0 else (lambda n: nch - 1 - n)
    out = []
    for d in range(2):
        ci = cidx(d)
        big = pl.BlockSpec((CHUNK, D), lambda n, ci=ci: (ci(n), 0))
        qks = pl.BlockSpec((1, DN_HEADS, CHUNK, CHUNK), lambda n, ci=ci: (ci(n), 0, 0, 0))
        row = pl.BlockSpec((1, 1, DN_HEADS, CHUNK), lambda n, ci=ci, d=d: (d, ci(n), 0, 0))
        st = pl.BlockSpec((1, DN_HEADS, DN_DIM, DN_DIM), lambda n, ci=ci: (ci(n), 0, 0, 0))
        out.append(dict(big=big, qk=qks, row=row, st=st))
    return out


def _step_inputs(ins, per_dir):
    def pick(o):
        if o == 2:
            return tuple(ins[d * per_dir + o][0, h].astype(F32) for d, h in _PROBLEMS)
        if o == 5:
            return tuple(ins[d * per_dir + o][0, 0, h:h + 1, :] for d, h in _PROBLEMS)
        return tuple(ins[d * per_dir + o][:, _hs(h)].astype(F32) for d, h in _PROBLEMS)
    return [pick(o) for o in range(6)]


def _dn_b_fwd(ares, grows):
    t_ = ares[0].shape[0]
    nch = t_ // CHUNK
    sp = _dir_specs(nch)

    def body(*refs):
        ins, outs, s_ref = refs[:12], refs[12:16], refs[16]

        @pl.when(pl.program_id(0) == 0)
        def _():
            s_ref[...] = jnp.zeros_like(s_ref)

        ss = tuple(s_ref[p] for p in range(len(_PROBLEMS)))
        os_, s2s = _dn_step(*_step_inputs(ins, 6), ss)
        for p, (d, h) in enumerate(_PROBLEMS):
            outs[2 + d][0, h] = ss[p]
            outs[d][:, _hs(h)] = os_[p]
            s_ref[p] = s2s[p]

    in_specs, args = [], []
    for d in range(2):
        in_specs += [sp[d]["big"], sp[d]["big"], sp[d]["qk"], sp[d]["big"], sp[d]["big"], sp[d]["row"]]
        args += list(ares[d * _A_OUT:(d + 1) * _A_OUT]) + [grows]
    stsh = jax.ShapeDtypeStruct((nch, DN_HEADS, DN_DIM, DN_DIM), F32)
    osh = jax.ShapeDtypeStruct((t_, D), F32)
    return pl.pallas_call(
        body, name="dn_b_fwd", grid=(nch,), in_specs=in_specs,
        out_specs=[sp[0]["big"], sp[1]["big"], sp[0]["st"], sp[1]["st"]], out_shape=[osh, osh, stsh, stsh],
        scratch_shapes=[pltpu.VMEM((2 * DN_HEADS, DN_DIM, DN_DIM), F32)],
        compiler_params=_params(1))(*args)


def _dn_b_bwd(ares, grows, st_f, st_b, do):
    t_ = ares[0].shape[0]
    nch = t_ // CHUNK
    sp = _dir_specs(nch)
    rsp = [sp[1], sp[0]]

    def body(*refs):
        ins, outs, ds_ref = refs[:16], refs[16:28], refs[28]

        @pl.when(pl.program_id(0) == 0)
        def _():
            ds_ref[...] = jnp.zeros_like(ds_ref)

        ss = tuple(ins[d * 8 + 6][0, h] for d, h in _PROBLEMS)
        _, vjp = jax.vjp(_dn_step, *_step_inputs(ins, 8), ss)
        dos = tuple(ins[d * 8 + 7][:, _hs(h)] for d, h in _PROBLEMS)
        grads = vjp((dos, tuple(ds_ref[p] for p in range(len(_PROBLEMS)))))
        for p, (d, h) in enumerate(_PROBLEMS):
            du_ref, dw_ref, dqk_ref, dqd_ref, dkd_ref, dg_ref = outs[d * 6:(d + 1) * 6]
            du_ref[:, _hs(h)], dw_ref[:, _hs(h)], dqk_ref[0, h] = grads[0][p], grads[1][p], grads[2][p]
            dqd_ref[:, _hs(h)], dkd_ref[:, _hs(h)] = grads[3][p], grads[4][p]
            dg_ref[0, 0, h:h + 1, :] = grads[5][p]
            ds_ref[p] = grads[6][p]

    in_specs, args, out_specs, out_shape = [], [], [], []
    big_sh = jax.ShapeDtypeStruct((t_, D), F32)
    qk_sh = jax.ShapeDtypeStruct((nch, DN_HEADS, CHUNK, CHUNK), F32)
    row_sh = jax.ShapeDtypeStruct((1, nch, DN_HEADS, CHUNK), F32)
    for d in range(2):
        s = rsp[d]
        row0 = pl.BlockSpec((1, 1, DN_HEADS, CHUNK), lambda m, d=d: (0, (nch - 1 - m) if d == 0 else m, 0, 0))
        rowd = pl.BlockSpec((1, 1, DN_HEADS, CHUNK), lambda m, d=d: (d, (nch - 1 - m) if d == 0 else m, 0, 0))
        in_specs += [s["big"], s["big"], s["qk"], s["big"], s["big"], rowd, s["st"], s["big"]]
        args += list(ares[d * _A_OUT:(d + 1) * _A_OUT]) + [grows, (st_f, st_b)[d], do]
        out_specs += [s["big"], s["big"], s["qk"], s["big"], s["big"], row0]
        out_shape += [big_sh, big_sh, qk_sh, big_sh, big_sh, row_sh]
    res = pl.pallas_call(
        body, name="dn_b_bwd", grid=(nch,), in_specs=in_specs, out_specs=out_specs, out_shape=out_shape,
        scratch_shapes=[pltpu.VMEM((2 * DN_HEADS, DN_DIM, DN_DIM), F32)],
        compiler_params=_params(1))(*args)
    dares = list(res[0:5]) + list(res[6:11])
    return dares, jnp.concatenate([res[5], res[11]], axis=0)


@jax.custom_vjp
def _rot_half(x):
    half, width = SW_DIM // 2, x.shape[1]
    first = lax.broadcasted_iota(jnp.int32, x.shape, 1) % SW_DIM < half
    return jnp.where(first, -pltpu.roll(x, width - half, axis=1), pltpu.roll(x, half, axis=1))


_rot_half.defvjp(lambda x: (_rot_half(x), None), lambda _, g: (-_rot_half(g),))


def _rope(x, c, s):
    reps = x.shape[1] // c.shape[1]
    return x * jnp.tile(c, (1, reps)) + _rot_half(x) * jnp.tile(s, (1, reps))


def _rope_t(g, c, s):
    reps = g.shape[1] // c.shape[1]
    return g * jnp.tile(c, (1, reps)) - _rot_half(g * jnp.tile(s, (1, reps)))


_SW_SCALE = SW_DIM ** -0.5
_KV_HEADS = [[kvh * SW_GRP + g for g in range(SW_GRP)] for kvh in range(SW_KV)]


def _by_group(x):
    return [jnp.concatenate([x[:, h * SW_DIM:(h + 1) * SW_DIM] for h in hs], axis=0) for hs in _KV_HEADS]


def _from_groups(xs):
    return jnp.concatenate([x[g * SW_BLOCK:(g + 1) * SW_BLOCK] for x in xs for g in range(SW_GRP)], axis=-1)


def _attn_probs(blk, t_, cq, sq, ck, sk, q, kall, sinks):
    qgs = _by_group(_rope(q, cq, sq))
    kr = _rope(kall, ck, sk)
    khs = [kr[:, kvh * SW_DIM:(kvh + 1) * SW_DIM] for kvh in range(SW_KV)]
    nq, nk = SW_GRP * SW_BLOCK, 3 * SW_BLOCK
    qpos = lax.broadcasted_iota(jnp.int32, (nq, nk), 0) % SW_BLOCK
    krel = lax.broadcasted_iota(jnp.int32, (nq, nk), 1) - SW_BLOCK
    kglob = krel + blk * SW_BLOCK
    valid = (jnp.abs(qpos - krel) <= SW_BLOCK) & (kglob >= 0) & (kglob < t_)
    ss = [jnp.where(valid, _nt(qg * _SW_SCALE, kh), -1e30) for qg, kh in zip(qgs, khs)]
    snks = [jnp.concatenate([jnp.broadcast_to(sinks[:, h:h + 1], (SW_BLOCK, 1)) for h in hs], axis=0) for hs in _KV_HEADS]
    ms = [jnp.maximum(jnp.max(s, axis=-1, keepdims=True), snk) for s, snk in zip(ss, snks)]
    es = [jnp.exp(s - m) for s, m in zip(ss, ms)]
    esnks = [jnp.exp(snk - m) for snk, m in zip(snks, ms)]
    invs = [1.0 / (jnp.sum(e, axis=-1, keepdims=True) + esnk) for e, esnk in zip(es, esnks)]
    ps = [e * inv for e, inv in zip(es, invs)]
    return qgs, khs, ps, [esnk * inv for esnk, inv in zip(esnks, invs)]


def _attn_f(blk, t_, cq, sq, ck, sk, q, kp, ko, kn, vp, vo, vn, sinks):
    _, _, ps, psinks = _attn_probs(blk, t_, cq, sq, ck, sk, q, jnp.concatenate([kp, ko, kn], axis=0), sinks)
    vall = jnp.concatenate([vp, vo, vn], axis=0)
    ps = [p.astype(_MXU) for p in ps]
    kept = [jnp.concatenate([p, jnp.broadcast_to(psink.astype(_MXU), (p.shape[0], 128))], axis=-1)
            for p, psink in zip(ps, psinks)]
    return _from_groups([_nn(p, vall[:, kvh * SW_DIM:(kvh + 1) * SW_DIM]) for kvh, p in enumerate(ps)]), kept


def _attn_bwd_f(cq, sq, ck, sk, q, kp, ko, kn, vp, vo, vn, do, kept):
    nk = 3 * SW_BLOCK
    ps = [x[:, :nk] for x in kept]
    psinks = [x[:, nk:nk + 1].astype(F32) for x in kept]
    qgs = _by_group(_rope(q, cq, sq))
    kr = _rope(jnp.concatenate([kp, ko, kn], axis=0), ck, sk)
    khs = [kr[:, kvh * SW_DIM:(kvh + 1) * SW_DIM] for kvh in range(SW_KV)]
    vall = jnp.concatenate([vp, vo, vn], axis=0)
    vhs = [vall[:, kvh * SW_DIM:(kvh + 1) * SW_DIM] for kvh in range(SW_KV)]
    dogs = _by_group(do)
    dvs = [_tn(p, dog) for p, dog in zip(ps, dogs)]
    ps = [p.astype(F32) for p in ps]
    dps = [_nt(dog * _SW_SCALE, vh) for dog, vh in zip(dogs, vhs)]
    deltas = [jnp.sum(p * dp, axis=-1, keepdims=True) for p, dp in zip(ps, dps)]
    dss = [p * (dp - delta) for p, dp, delta in zip(ps, dps, deltas)]
    dqr = _from_groups([_nn(ds, kh) for ds, kh in zip(dss, khs)])
    dkr = jnp.concatenate([_tn(ds, qg) for ds, qg in zip(dss, qgs)], axis=-1)
    dsnk = [-(psink * delta) * (1.0 / _SW_SCALE) for psink, delta in zip(psinks, deltas)]
    dsinks = jnp.concatenate([jnp.sum(d[g * SW_BLOCK:(g + 1) * SW_BLOCK], axis=0, keepdims=True)
                              for d in dsnk for g in range(SW_GRP)], axis=1)
    dq, dk, dv = _rope_t(dqr, cq, sq), _rope_t(dkr, ck, sk), jnp.concatenate(dvs, axis=-1)
    blocks = lambda a: [a[j * SW_BLOCK:(j + 1) * SW_BLOCK] for j in range(3)]
    return [dq] + blocks(dk) + blocks(dv) + [dsinks]


def _attn_specs(nb):
    prv = lambda i: jnp.maximum(i - 1, 0)
    nxt = lambda i: jnp.minimum(i + 1, nb - 1)
    rows = [lambda i: i, prv, lambda i: i, nxt]
    tab = [pl.BlockSpec((SW_BLOCK, 128), lambda i, r=r: (r(i), 0)) for r in rows]
    qs = pl.BlockSpec((SW_BLOCK, SW_HEADS * SW_DIM), lambda i: (i, O_QS // (SW_HEADS * SW_DIM)))
    kw = SW_KV * SW_DIM
    ks = [pl.BlockSpec((SW_BLOCK, kw), lambda i, r=r: (r(i), O_KS // kw)) for r in rows[1:]]
    vs = [pl.BlockSpec((SW_BLOCK, kw), lambda i, r=r: (r(i), O_VS // kw)) for r in rows[1:]]
    return tab, qs, ks, vs


def _attn_tables(refs):
    cq, cp, co, cn, sq, sp_, so, sn = [r[...] for r in refs]
    return cq, sq, jnp.concatenate([cp, co, cn], axis=0), jnp.concatenate([sp_, so, sn], axis=0)


_P_BLOCK = (None, SW_KV, SW_GRP * SW_BLOCK, 3 * SW_BLOCK + 128)


def _attn_fwd(pm, cos, sin, sinks):
    t_ = pm.shape[0]
    nb = t_ // SW_BLOCK
    tab, qs, ks, vs = _attn_specs(nb)

    def body(*refs):
        tabs = _attn_tables(refs[:8])
        vals = [r[...] for r in refs[8:16]]
        out, ps = _attn_f(pl.program_id(0), t_, *tabs, *vals)
        refs[16][...] = out.astype(refs[16].dtype)
        for kvh, p in enumerate(ps):
            refs[17][kvh] = p

    return pl.pallas_call(
        body, name="attn_fwd", grid=(nb,), in_specs=tab + tab + [qs] + ks + vs + [_full(sinks)],
        out_specs=[pl.BlockSpec((SW_BLOCK, D), lambda i: (i, 0)), pl.BlockSpec(_P_BLOCK, lambda i: (i, 0, 0, 0))],
        out_shape=[jax.ShapeDtypeStruct((t_, D), _MXU), jax.ShapeDtypeStruct((nb,) + _P_BLOCK[1:], _MXU)],
        compiler_params=_params(1))(*([cos] * 4), *([sin] * 4), pm, pm, pm, pm, pm, pm, pm, sinks)


def _attn_bwd(pm, cos, sin, sinks, probs, do, dpm):
    t_ = pm.shape[0]
    nb = t_ // SW_BLOCK
    tab, qs, ks, vs = _attn_specs(nb)
    kw = SW_KV * SW_DIM

    def body(*refs):
        tabs = _attn_tables(refs[:8])
        vals = [r[...] for r in refs[8:15]]
        p_ref, do_ref, outs = refs[16], refs[17], refs[19:]
        grads = _attn_bwd_f(*tabs, *vals, do_ref[...], [p_ref[kvh] for kvh in range(SW_KV)])
        for o_ref, g in zip(outs[:7], grads[:7]):
            o_ref[...] = g.astype(o_ref.dtype)

        @pl.when(pl.program_id(0) == 0)
        def _():
            outs[7][...] = jnp.zeros_like(outs[7])
        outs[7][...] += grads[7]

    own = lambda w: pl.BlockSpec((SW_BLOCK, w), lambda i: (i, 0))
    return pl.pallas_call(
        body, name="attn_bwd", grid=(nb,),
        in_specs=tab + tab + [qs] + ks + vs + [_full(sinks), pl.BlockSpec(_P_BLOCK, lambda i: (i, 0, 0, 0)), own(D), ANY],
        out_specs=[pl.BlockSpec((SW_BLOCK, D), lambda i: (i, O_QS // D))] + [own(kw)] * 6 + [_full(sinks)],
        out_shape=[jax.ShapeDtypeStruct(dpm.shape, dpm.dtype)] + [jax.ShapeDtypeStruct((t_, kw), F32)] * 6
        + [jax.ShapeDtypeStruct(sinks.shape, F32)],
        input_output_aliases={18: 0},
        compiler_params=_params(1))(*([cos] * 4), *([sin] * 4), pm, pm, pm, pm, pm, pm, pm, sinks, probs, do, dpm)


def _band_sum(kparts, vparts, dpm):
    t_, kw = kparts[1].shape
    nb = t_ // SW_BLOCK

    def body(kp, ko, kn, vp, vo, vn, _, out_ref):
        j = pl.program_id(0)
        band = lambda p, o, n: o[...] + jnp.where(j + 1 < nb, p[...], 0.0) + jnp.where(j > 0, n[...], 0.0)
        out_ref[...] = jnp.concatenate([band(kp, ko, kn), band(vp, vo, vn)], axis=-1).astype(out_ref.dtype)

    specs = [pl.BlockSpec((SW_BLOCK, kw), lambda j: (jnp.minimum(j + 1, nb - 1), 0)),
             pl.BlockSpec((SW_BLOCK, kw), lambda j: (j, 0)),
             pl.BlockSpec((SW_BLOCK, kw), lambda j: (jnp.maximum(j - 1, 0), 0))]
    return pl.pallas_call(
        body, name="band_sum", grid=(nb,), in_specs=specs * 2 + [ANY],
        out_specs=pl.BlockSpec((SW_BLOCK, 2 * kw), lambda j: (j, O_KS // (2 * kw))),
        out_shape=jax.ShapeDtypeStruct(dpm.shape, dpm.dtype), input_output_aliases={6: 0},
        compiler_params=_params(1))(*kparts, *vparts, dpm)


def _loss_head(y, target, tb=256):
    t_ = y.shape[0]

    def body(y_ref, t_ref, dy_ref, acc_ref):
        err = y_ref[...] - t_ref[...]
        dy_ref[...] = err * (1.0 / D)
        sq = (err * err).reshape(tb // 8, 8, D).sum(axis=0)
        part = sq[:, 0:128]
        for c in range(1, D // 128):
            part = part + sq[:, c * 128:(c + 1) * 128]

        @pl.when(pl.program_id(0) == 0)
        def _():
            acc_ref[...] = jnp.zeros_like(acc_ref)
        acc_ref[...] += part

    row = pl.BlockSpec((tb, D), lambda i: (i, 0))
    return pl.pallas_call(
        body, name="loss_head", grid=(t_ // tb,), in_specs=[row, row],
        out_specs=[row, pl.BlockSpec((8, 128), lambda i: (0, 0))],
        out_shape=[jax.ShapeDtypeStruct((t_, D), F32), jax.ShapeDtypeStruct((8, 128), F32)],
        compiler_params=_params(1))(y, target)


def _adam_math(w, g, m, v):
    bc1 = 1.0 - ADAM_B1 ** ADAM_STEP
    bc2 = 1.0 - ADAM_B2 ** ADAM_STEP
    nm = ADAM_B1 * m + (1.0 - ADAM_B1) * g
    nv = ADAM_B2 * v + (1.0 - ADAM_B2) * (g * g)
    return -ADAM_LR * ((nm / bc1) / (jnp.sqrt(nv / bc2) + ADAM_EPS) + ADAM_WD * w), nm, nv


def _adamw_layer(name, l, w, g, m, v, carry, after):
    nl, r, c = w.shape
    g2 = g.reshape(r, -1)
    gc = g2.shape[1]
    w2, m2, v2 = [a.reshape(nl * r, c) for a in (w, m, v)]
    tb = r
    while tb * gc * 4 > (1 << 20) and tb % 16 == 0:
        tb //= 2
    nb = r // tb

    def body(w_ref, g_ref, m_ref, v_ref, *rest):
        go_ref, d_ref, nm_ref, nv_ref = rest[-4:]
        gv = g_ref[...][:, :c]
        go_ref[...] = gv
        d_ref[...], nm_ref[...], nv_ref[...] = _adam_math(w_ref[...], gv, m_ref[...], v_ref[...])

    spec = pl.BlockSpec((tb, c), lambda i: (l * nb + i, 0))
    carried = list(carry) if carry is not None else []
    outs = pl.pallas_call(
        body, name=name, grid=(nb,),
        in_specs=[spec, pl.BlockSpec((tb, gc), lambda i: (i, 0)), spec, spec] + [ANY] * (len(carried) + 1),
        out_specs=[spec] * 4, out_shape=[jax.ShapeDtypeStruct((nl * r, c), F32)] * 4,
        input_output_aliases={4 + k: k for k in range(len(carried))},
        compiler_params=_params(1))(w2, g2, m2, v2, *[a.reshape(nl * r, c) for a in carried], after)
    return tuple(o.reshape(w.shape) for o in outs)


def _adamw(name, w, g, m, v):
    shape = w.shape
    cols = shape[-1]
    rows = w.size // cols
    w2, g2, m2, v2 = [a.reshape(rows, cols) for a in (w, g, m, v)]
    tb = rows
    while tb * cols * 4 > (1 << 20) and tb % 16 == 0:
        tb //= 2

    def body(w_ref, g_ref, m_ref, v_ref, d_ref, nm_ref, nv_ref):
        d_ref[...], nm_ref[...], nv_ref[...] = _adam_math(w_ref[...], g_ref[...], m_ref[...], v_ref[...])

    spec = pl.BlockSpec((tb, cols), lambda i: (i, 0))
    sh = jax.ShapeDtypeStruct((rows, cols), F32)
    outs = pl.pallas_call(body, name=name, grid=(rows // tb,), in_specs=[spec] * 4, out_specs=[spec] * 3,
                          out_shape=[sh] * 3, compiler_params=_params(1))(w2, g2, m2, v2)
    return [o.reshape(shape) for o in outs]


def _to_rows(bg, col0):
    t_ = bg.shape[0]
    a = bg[:, col0:col0 + 2 * DN_HEADS].reshape(t_ // CHUNK, CHUNK, 2, DN_HEADS)
    return jnp.transpose(a, (2, 0, 3, 1))


def _from_rows(db, dg):
    nch = db.shape[1]
    back = lambda a: jnp.transpose(a, (1, 3, 0, 2)).reshape(nch * CHUNK, 2 * DN_HEADS)
    return jnp.pad(jnp.concatenate([back(db), back(dg)], axis=1), ((0, 0), (0, 128 - 4 * DN_HEADS)))


def _layer_fwd(x, xm, w, rest, cos, sin):
    t_ = x.shape[0]
    tb, nb = min(256, t_), min(1024, t_)
    pm = _mm(xm, w["in_main"], name="mm_in", tn=1536)
    pbg = _mm(xm, w["in_bg"], name="mm_in_bg")
    w = {**w, **rest(pbg)}
    qkv = _prep_fwd(pm, w["conv"])
    bg, = _stage_fwd(_bg_f, "bg_fwd", [(pbg, 128, 0, False)], [w["arow"], w["dtrow"]], [(128, F32)], nb)
    brows, grows = _to_rows(bg, 0), _to_rows(bg, 2 * DN_HEADS)
    ares, tinv = _dn_a_fwd(qkv, grows, brows)
    o_f, o_b, st_f, st_b = _dn_b_fwd(ares, grows)
    odn, = _stage_fwd(_gnorm_f, "gnorm_fwd", [(o_f, 128, 0, True), (o_b, 128, 0, True), (pm, 128, O_Z // 128, True)],
                      [w["gnw"]], [(128, _MXU)], nb, ncol=DN_HEADS)
    osw, probs = _attn_fwd(pm, cos, sin, w["sinks"])
    same, first = (lambda j: j), (lambda j: 0)
    full = lambda dt: (D, D, dt, first, None)
    ya = _mm(odn, w["a"], name="mm_a")
    yb, merged = _mm_fused(
        osw, w["b"], lambda acc, ya_, ga, gb: (acc,) + _merge_f(ya_, acc, ga, gb), [full(F32), full(_MXU)],
        name="mm_b_merge", rows=[(ya, D, first), (pm, D, lambda j: O_GA // D), (pm, D, lambda j: O_GB // D)], tm=512)
    mix, x1, x1m = _mm_fused(
        merged, w["o"], lambda acc, x_, g_, b_: (acc,) + _ln_f2(x_, acc, g_, b_), [full(F32), full(F32), full(_MXU)],
        name="mm_o_ln", rows=[(x, D, first)], params=[w["ln1g"], w["ln1b"]], tm=512)
    gu, hid = _mm_fused(x1m, w["gu"], lambda acc: (acc,) + _swiglu_tile(acc),
                        [(2 * FFN, FFN, F32, same, None), (FFN, FFN // 2, _MXU, same, None)], name="mm_gu_swiglu", tn=FFN)
    ffn, x2, x2m = _mm_fused(
        hid, w["d"], lambda acc, x_, g_, b_: (acc,) + _ln_f2(x_, acc, g_, b_), [full(F32), full(F32), full(_MXU)],
        name="mm_d_ln", rows=[(x1, D, first)], params=[w["ln2g"], w["ln2b"]], tm=512, tk=FFN)
    res = dict(w=w, x=x, xm=xm, pm=pm, pbg=pbg, qkv=qkv, grows=grows, brows=brows, ares=ares, tinv=tinv, o_f=o_f, o_b=o_b, st_f=st_f,
               st_b=st_b, odn=odn, osw=osw, probs=probs, ya=ya, yb=yb, merged=merged, mix=mix, x1=x1, x1m=x1m, gu=gu, hid=hid,
               ffn=ffn)
    return x2, x2m, res


def _layer_bwd(dx2, r, w, cos, sin, mid=None, after=None):
    t_ = dx2.shape[0]
    tb, nb = min(256, t_), min(1024, t_)
    pm = r["pm"]
    g = {}
    dx1a, dffn, g["ln2g"], g["ln2b"] = _stage_bwd(
        _ln_f, "ln2_bwd", [(r["x1"], D, 0, False), (r["ffn"], D, 0, False)], [w["ln2g"], w["ln2b"]], [dx2], tb,
        dtypes=[F32, _MXU], after=after)
    same = lambda j: j
    dgu, = _mm_fused(dffn, w["d"], lambda dhid, gu: jax.vjp(_swiglu_tile, gu)[1]((dhid,)),
                     [(2 * FFN, FFN, _MXU, same, None)], name="mm_d_dx_swiglu", rows=[(r["gu"], FFN, same)], tb=True,
                     tn=FFN // 2)
    g["d"] = _mm(r["hid"], dffn, ta=True, name="mm_d_dw", tm=FFN // 2, out_dtype=_MXU)
    first = lambda j: 0
    full = lambda dt: (D, D, dt, first, None)

    def ln1_back(acc, dx1a_, x_, mix_, g_, b_):
        dx_, dmix_, dg_, db_ = jax.vjp(_ln_f, x_, mix_, g_, b_)[1]((acc + dx1a_,))
        return dx_, dmix_, dg_, db_

    dxa, dmix, g["ln1g"], g["ln1b"] = _mm_fused(
        dgu, w["gu"], ln1_back, [full(F32), full(_MXU)], name="mm_gu_dx_ln", tb=True, tm=256, tk=2 * FFN, n_sums=2,
        rows=[(dx1a, D, first), (r["x"], D, first), (r["mix"], D, first)], params=[w["ln1g"], w["ln1b"]])
    g["gu"] = _mm(r["x1m"], dgu, ta=True, name="mm_gu_dw", tn=FFN // 2, out_dtype=_MXU)
    tok = mid(g["gu"]) if mid is not None else None
    g["o"] = _mm(r["merged"], dmix, ta=True, name="mm_o_dw", after=tok, out_dtype=_MXU)

    def merge_back(acc, ya_, yb_, ga, gb):
        dya_, dyb_, dga, dgb = jax.vjp(_merge_f, ya_, yb_, ga, gb)[1]((acc,))
        return dya_, dyb_, jnp.concatenate([dga, dgb], axis=-1)

    dya, dyb, dpm = _mm_fused(
        dmix, w["o"], merge_back, [full(_MXU), full(_MXU), (N_MAIN, 2 * D, _MXU, lambda j: O_GA // (2 * D), None)],
        name="mm_o_dx_merge", tb=True, tm=512,
        rows=[(r["ya"], D, first), (r["yb"], D, first), (pm, D, lambda j: O_GA // D), (pm, D, lambda j: O_GB // D)])

    def gnorm_back(acc, of, ob, z, gnw):
        dos, dzs, dws = [], [], None
        for h in range(DN_HEADS):
            do_h, _, dz_h, dw_h = jax.vjp(_gnorm_f, of[:, _hs(h)], ob[:, _hs(h)], z[:, _hs(h)], gnw)[1]((acc[:, _hs(h)],))
            dos.append(do_h)
            dzs.append(dz_h)
            dws = dw_h if dws is None else dws + dw_h
        return jnp.concatenate(dos, axis=-1), jnp.concatenate(dzs, axis=-1), dws

    dof, dpm, g["gnw"] = _mm_fused(
        dya, w["a"], gnorm_back, [full(F32), (N_MAIN, D, _MXU, lambda j: O_Z // D, dpm)], name="mm_a_dx_gnorm", tb=True,
        tm=512, rows=[(r["o_f"], D, first), (r["o_b"], D, first), (pm, D, lambda j: O_Z // D)], params=[w["gnw"]],
        n_sums=1)
    g["a"] = _mm(r["odn"], dya, ta=True, name="mm_a_dw", out_dtype=_MXU)
    dosw = _mm(dyb, w["b"], tb=True, name="mm_b_dx")
    g["b"] = _mm(r["osw"], dyb, ta=True, name="mm_b_dw", out_dtype=_MXU)
    ab = _attn_bwd(pm, cos, sin, w["sinks"], r["probs"], dosw, dpm)
    dpm, g["sinks"] = ab[0], ab[7]
    dpm = _band_sum(ab[1:4], ab[4:7], dpm)
    dares, dg_b = _dn_b_bwd(r["ares"], r["grows"], r["st_f"], r["st_b"], dof)
    dqkv, dgrows, dbrows = _dn_a_bwd(r["qkv"], r["grows"], r["brows"], r["tinv"], dares, dg_b)
    dbg = _from_rows(dbrows, dgrows)
    dpbg, g["arow"], g["dtrow"] = _stage_bwd(_bg_f, "bg_bwd", [(r["pbg"], 128, 0, False)], [w["arow"], w["dtrow"]],
                                             [dbg], nb)
    dpm, g["conv"] = _prep_bwd(pm, w["conv"], dqkv, dpm)
    dx = _mm(dpm, w["in_main"], tb=True, add=dxa, name="mm_in_dx", tk=1920)
    dx = _mm(dpbg, w["in_bg"], tb=True, add=dx, name="mm_in_bg_dx")
    g["in_main"] = _mm(r["xm"], dpm, ta=True, name="mm_in_dw", tn=1536, out_dtype=_MXU)
    g["in_bg"] = _mm(r["xm"], dpbg, ta=True, name="mm_in_bg_dw", out_dtype=_MXU)
    return dx, g


def _place():
    return lax.axis_index("x"), lax.axis_index("y"), lax.axis_index("c")


def _colblock(kind, q):
    return q if kind == "col" else (q >> 1) | ((q & 1) << 1)


def _other_chips(x, y):
    return [(1 - x, y), (x, 1 - y), (1 - x, 1 - y)]


HBM = pl.BlockSpec(memory_space=pltpu.HBM)
SEM = pl.BlockSpec(memory_space=pltpu.SEMAPHORE)
DATAFLOW = pltpu.SideEffectType.DATAFLOW_SIDE_EFFECTING


def _hbm(a):
    return pltpu.HBM(a.shape, a.dtype)


def _gather_start(locs, kinds, name):
    n = len(locs)
    locs = list(locs)
    lands = [lax.empty((N_CHIPS,) + a.shape if kind == "row" else (a.shape[0], N_CHIPS * a.shape[1]), a.dtype)
             for a, kind in zip(locs, kinds)]

    def body(*refs):
        loc_refs, land_refs = refs[:n], refs[n:2 * n]
        send_sems, recv_sems, token = refs[2 * n:3 * n], refs[3 * n:4 * n], refs[-1]
        x, y, c = _place()
        myq = 2 * x + y
        for t in range(n):
            width = locs[t].shape[1]
            mine = (land_refs[t].at[myq] if kinds[t] == "row" else
                    land_refs[t].at[:, pl.ds(pl.multiple_of(_colblock(kinds[t], myq) * width, 128), width)])
            for dev in [(cx, cy, c) for cx, cy in _other_chips(x, y)] + [(x, y, 1 - c)]:
                pltpu.make_async_remote_copy(src_ref=loc_refs[t], dst_ref=mine, send_sem=send_sems[t],
                                             recv_sem=recv_sems[t], device_id=dev, device_id_type=MESH).start()
        token[...] = jnp.zeros_like(token)

    res = pl.pallas_call(
        body, name=name,
        out_shape=[pltpu.SemaphoreType.DMA(())] * (2 * n) + [_hbm(a) for a in locs + lands]
        + [jax.ShapeDtypeStruct((8, 128), F32)],
        in_specs=[HBM] * (2 * n), out_specs=[SEM] * (2 * n) + [HBM] * (2 * n) + [pl.BlockSpec(memory_space=pltpu.VMEM)],
        input_output_aliases={t: 2 * n + t for t in range(2 * n)},
        compiler_params=pltpu.CompilerParams(has_side_effects=DATAFLOW),
    )(*[pltpu.with_memory_space_constraint(a, pltpu.HBM) for a in locs + lands])
    return (res[:n], res[n:2 * n], res[2 * n:3 * n], res[3 * n:4 * n]), res[-1]


def _split_wait(handle, after, name):
    send_sems, recv_sems, srcs, lands = handle
    n = len(srcs)

    def body(*refs):
        land_refs, ssems, rsems = refs[n:2 * n], refs[2 * n:3 * n], refs[3 * n:4 * n]
        x, y, c = _place()
        for t in range(n):
            done = pltpu.make_async_remote_copy(
                src_ref=land_refs[t], dst_ref=land_refs[t], send_sem=ssems[t], recv_sem=rsems[t],
                device_id=(x, y, c), device_id_type=MESH)
            done.wait_send()
            done.wait_recv()

    res = pl.pallas_call(
        body, name=name, out_shape=[_hbm(a) for a in list(srcs) + list(lands)],
        in_specs=[HBM] * (2 * n) + [SEM] * (2 * n) + [ANY], out_specs=[HBM] * (2 * n),
        input_output_aliases={t: t for t in range(2 * n)},
        compiler_params=pltpu.CompilerParams(has_side_effects=DATAFLOW),
    )(*srcs, *lands, *send_sems, *recv_sems, after)
    return res[:n], res[n:]


RS_CHUNKS = 2


def _piece(ref, kind, q, hf, pr, pc):
    if kind == "row":
        return ref.at[pl.ds((2 * q + hf) * pr, pr), :]
    return ref.at[pl.ds(hf * pr, pr), pl.ds(pl.multiple_of(_colblock(kind, q) * pc, 128), pc)]


def _rs_sibling_start(ts, meta, name):
    n = len(ts)
    ts = list(ts)
    lands = [lax.empty((N_CHIPS, pr, pc), a.dtype) for a, (_, pr, pc) in zip(ts, meta)]

    def body(*refs):
        t_refs, land_refs = refs[:n], refs[n:2 * n]
        send_sems, recv_sems, token = refs[2 * n:3 * n], refs[3 * n:4 * n], refs[-1]
        x, y, c = _place()
        for t, (kind, pr, pc) in enumerate(meta):
            for q in range(N_CHIPS):
                pltpu.make_async_remote_copy(
                    src_ref=_piece(t_refs[t], kind, q, 1 - c, pr, pc), dst_ref=land_refs[t].at[q],
                    send_sem=send_sems[t], recv_sem=recv_sems[t], device_id=(x, y, 1 - c), device_id_type=MESH).start()
        token[...] = jnp.zeros_like(token)

    res = pl.pallas_call(
        body, name=name,
        out_shape=[pltpu.SemaphoreType.DMA(())] * (2 * n) + [_hbm(a) for a in ts + lands]
        + [jax.ShapeDtypeStruct((8, 128), F32)],
        in_specs=[HBM] * (2 * n), out_specs=[SEM] * (2 * n) + [HBM] * (2 * n) + [pl.BlockSpec(memory_space=pltpu.VMEM)],
        input_output_aliases={t: 2 * n + t for t in range(2 * n)},
        compiler_params=pltpu.CompilerParams(has_side_effects=DATAFLOW),
    )(*[pltpu.with_memory_space_constraint(a, pltpu.HBM) for a in ts + lands])
    return (res[:n], res[n:2 * n], res[2 * n:3 * n], res[3 * n:4 * n]), res[-1]


def _rs_add_sibling(ts, r1s, meta, c):
    n = len(ts)
    in_specs, out_specs, out_shape = [], [], []
    for kind, pr, pc in meta:
        rs = pr // RS_CHUNKS
        if kind == "row":
            in_specs.append(pl.BlockSpec((rs, pc), lambda q, r, c_ref: ((2 * q + c_ref[0]) * RS_CHUNKS + r, 0)))
        else:
            in_specs.append(pl.BlockSpec(
                (rs, pc), lambda q, r, c_ref, kind=kind: (c_ref[0] * RS_CHUNKS + r, _colblock(kind, q))))
    for kind, pr, pc in meta:
        sp = pl.BlockSpec((None, pr // RS_CHUNKS, pc), lambda q, r, c_ref: (q, r, 0))
        in_specs.append(sp)
        out_specs.append(sp)
        out_shape.append(jax.ShapeDtypeStruct((N_CHIPS, pr, pc), BF16))

    def body(c_ref, *refs):
        for t in range(n):
            refs[2 * n + t][...] = (refs[t][...].astype(F32) + refs[n + t][...].astype(F32)).astype(BF16)

    return pl.pallas_call(
        body, name="rs_add_sibling", out_shape=out_shape,
        grid_spec=pltpu.PrefetchScalarGridSpec(num_scalar_prefetch=1, grid=(N_CHIPS, RS_CHUNKS), in_specs=in_specs,
                                               out_specs=out_specs),
        compiler_params=_params(2))(c.reshape(1).astype(jnp.int32), *ts, *r1s)


def _rs_chips_start(ps, meta, name):
    n = len(ps)
    ps = list(ps)
    lands = [lax.empty((N_CHIPS - 1, pr, pc), p.dtype) for p, (_, pr, pc) in zip(ps, meta)]

    def body(*refs):
        p_refs, land_refs = refs[:n], refs[n:2 * n]
        send_sems, recv_sems, token = refs[2 * n:3 * n], refs[3 * n:4 * n], refs[-1]
        x, y, c = _place()
        for t in range(n):
            for j, (cx, cy) in enumerate(_other_chips(x, y)):
                pltpu.make_async_remote_copy(
                    src_ref=p_refs[t].at[2 * cx + cy], dst_ref=land_refs[t].at[j], send_sem=send_sems[t],
                    recv_sem=recv_sems[t], device_id=(cx, cy, c), device_id_type=MESH).start()
        token[...] = jnp.zeros_like(token)

    res = pl.pallas_call(
        body, name=name,
        out_shape=[pltpu.SemaphoreType.DMA(())] * (2 * n) + [_hbm(a) for a in ps + lands]
        + [jax.ShapeDtypeStruct((8, 128), F32)],
        in_specs=[HBM] * (2 * n), out_specs=[SEM] * (2 * n) + [HBM] * (2 * n) + [pl.BlockSpec(memory_space=pltpu.VMEM)],
        input_output_aliases={t: 2 * n + t for t in range(2 * n)},
        compiler_params=pltpu.CompilerParams(has_side_effects=DATAFLOW),
    )(*[pltpu.with_memory_space_constraint(a, pltpu.HBM) for a in ps + lands])
    return (res[:n], res[n:2 * n], res[2 * n:3 * n], res[3 * n:4 * n]), res[-1]


def _rs_add_chips(ps, r2s, meta, myq, c):
    n = len(ps)
    in_specs, out_specs, out_shape = [], [], []
    for _, pr, pc in meta:
        in_specs.append(pl.BlockSpec((None, pr // RS_CHUNKS, pc), lambda r, q_ref, c_ref: (q_ref[0], r, 0)))
    for _, pr, pc in meta:
        in_specs.append(pl.BlockSpec((N_CHIPS - 1, pr // RS_CHUNKS, pc), lambda r, q_ref, c_ref: (0, r, 0)))
        out_specs.append(pl.BlockSpec((None, pr // RS_CHUNKS, pc), lambda r, q_ref, c_ref: (c_ref[0], r, 0)))
        out_shape.append(jax.ShapeDtypeStruct((2, pr, pc), F32))

    def body(q_ref, c_ref, *refs):
        for t in range(n):
            r2 = refs[n + t]
            own = refs[t][...].astype(F32)
            refs[2 * n + t][...] = ((own + r2[0].astype(F32)) + r2[1].astype(F32)) + r2[2].astype(F32)

    return pl.pallas_call(
        body, name="rs_add_chips", out_shape=out_shape,
        grid_spec=pltpu.PrefetchScalarGridSpec(num_scalar_prefetch=2, grid=(RS_CHUNKS,), in_specs=in_specs,
                                               out_specs=out_specs),
        compiler_params=_params(1))(myq.reshape(1).astype(jnp.int32), c.reshape(1).astype(jnp.int32), *ps, *r2s)


def _rs_share_halves(gs):
    n = len(gs)

    def body(*refs):
        g_refs, send_sems, recv_sems = refs[n:2 * n], refs[2 * n], refs[2 * n + 1]
        x, y, c = _place()
        sib = (x, y, 1 - c)
        for t in range(n):
            pltpu.make_async_remote_copy(
                src_ref=g_refs[t].at[c], dst_ref=g_refs[t].at[c], send_sem=send_sems.at[t], recv_sem=recv_sems.at[t],
                device_id=sib, device_id_type=MESH).start()
        for t in range(n):
            cp = pltpu.make_async_remote_copy(
                src_ref=g_refs[t].at[c], dst_ref=g_refs[t].at[1 - c], send_sem=send_sems.at[t],
                recv_sem=recv_sems.at[t], device_id=sib, device_id_type=MESH)
            cp.wait_send()
            cp.wait_recv()

    return pl.pallas_call(
        body, name="rs_share_halves", in_specs=[ANY] * n, out_specs=[ANY] * n,
        out_shape=[jax.ShapeDtypeStruct(g.shape, g.dtype) for g in gs], input_output_aliases={t: t for t in range(n)},
        scratch_shapes=[pltpu.SemaphoreType.DMA((n,)), pltpu.SemaphoreType.DMA((n,))],
    )(*gs)


def _rs_middle(handle, after, meta, c, name):
    ts, r1s = _split_wait(handle, after, name + "_sib_wait")
    ps = _rs_add_sibling(ts, r1s, meta, c)
    return _rs_chips_start(ps, meta, name + "_start")


def _rs_end(handle, after, meta, c, myq, name):
    ps, r2s = _split_wait(handle, after, name + "_wait")
    return _rs_share_halves(_rs_add_chips(ps, r2s, meta, myq, c))


def _allreduce_small(buf):
    rows = buf.shape[0]
    ndev = 8

    def body(b_ref, o_ref, slots, send_sems, recv_sems):
        x, y, c = _place()
        me = 4 * x + 2 * y + c
        slots[me] = b_ref[...]
        for k in range(1, ndev):
            kx, ky, kc = (k >> 2) & 1, (k >> 1) & 1, k & 1
            peer = (x ^ kx, y ^ ky, c ^ kc)
            pltpu.make_async_remote_copy(
                src_ref=b_ref, dst_ref=slots.at[me], send_sem=send_sems.at[k - 1], recv_sem=recv_sems.at[k - 1],
                device_id=peer, device_id_type=MESH).start()
        for k in range(1, ndev):
            kx, ky, kc = (k >> 2) & 1, (k >> 1) & 1, k & 1
            cp = pltpu.make_async_remote_copy(
                src_ref=b_ref, dst_ref=slots.at[me ^ k], send_sem=send_sems.at[k - 1], recv_sem=recv_sems.at[k - 1],
                device_id=(x ^ kx, y ^ ky, c ^ kc), device_id_type=MESH)
            cp.wait_send()
            cp.wait_recv()
        acc = slots[0]
        for s in range(1, ndev):
            acc = acc + slots[s]
        o_ref[...] = acc

    vm = pl.BlockSpec(memory_space=pltpu.VMEM)
    return pl.pallas_call(
        body, name="allreduce_small", in_specs=[vm], out_specs=vm, out_shape=jax.ShapeDtypeStruct((rows, 128), F32),
        scratch_shapes=[pltpu.VMEM((ndev, rows, 128), F32), pltpu.SemaphoreType.DMA((ndev - 1,)),
                        pltpu.SemaphoreType.DMA((ndev - 1,))],
        compiler_params=pltpu.CompilerParams(vmem_limit_bytes=VMEM_LIMIT))(buf)


RS_META = [("col", D // 2, IN_PAD), ("row", D // 8, D), ("row", D // 8, D), ("row", D // 8, D),
           ("colx", D // 2, 2 * FFN // N_CHIPS), ("row", FFN // 8, D)]
SMALL_ROWS = 156


def _rope_tables(t_):
    half = SW_DIM // 2
    inv_freq = ROPE_THETA ** (-jnp.arange(half, dtype=F32) / half)
    ang = jnp.arange(t_, dtype=F32)[:, None] * inv_freq[None, :]
    reps = 128 // half
    return jnp.concatenate([jnp.cos(ang)] * reps, axis=1), jnp.concatenate([jnp.sin(ang)] * reps, axis=1)


def _orig_cols(padded, a, b):
    out = []
    for q in range(N_CHIPS):
        lo, hi = max(a, q * IN_SHARD), min(b, (q + 1) * IN_SHARD)
        if lo < hi:
            out.append(padded[:, q * IN_PAD + lo - q * IN_SHARD:q * IN_PAD + hi - q * IN_SHARD])
    return out


_ORIG_SEGMENTS = [(0, R_BG, "main", 0), (R_BG, R_SW, "bg", 0), (R_SW, R_GATES, "main", O_QS), (R_GATES, IN_COLS, "main", O_GA)]


def _to_padded_shards(main, bg):
    zeros = jnp.zeros((main.shape[0], IN_PAD - IN_SHARD), main.dtype)
    parts = []
    for q in range(N_CHIPS):
        for a, b, src, s0 in _ORIG_SEGMENTS:
            lo, hi = max(a, q * IN_SHARD), min(b, (q + 1) * IN_SHARD)
            if lo < hi:
                parts.append((main if src == "main" else bg)[:, s0 + lo - a:s0 + hi - a])
        parts.append(zeros)
    return jnp.concatenate(parts, axis=1)


def _lane_row(v16):
    return jnp.pad(v16.reshape(1, 2 * DN_HEADS), ((0, 0), (2 * DN_HEADS, 128 - 4 * DN_HEADS)))


def _pack_small(g):
    pad16 = jnp.pad(g["sinks"], ((0, 0), (0, 128 - SW_HEADS)))
    return jnp.concatenate([g["conv"].reshape(-1, 128), g["ln1g"].reshape(-1, 128), g["ln1b"].reshape(-1, 128),
                            g["ln2g"].reshape(-1, 128), g["ln2b"].reshape(-1, 128), g["gnw"], g["arow"], g["dtrow"],
                            pad16], axis=0)


def _unpack_small(buf):
    nconv = DN_CONV * 3 * D // 128
    o = nconv
    out = dict(conv=buf[:o].reshape(DN_CONV, 3 * D))
    for name in ("ln1g", "ln1b", "ln2g", "ln2b"):
        out[name] = buf[o:o + 8].reshape(D)
        o += 8
    out["gnw"] = buf[o]
    out["a_log"] = buf[o + 1, 2 * DN_HEADS:4 * DN_HEADS].reshape(2, DN_HEADS)
    out["dt_bias"] = buf[o + 2, 2 * DN_HEADS:4 * DN_HEADS].reshape(2, DN_HEADS)
    out["sinks"] = buf[o + 3, :SW_HEADS]
    return out


def kernel(x, w_in, conv_w, a_log, dt_bias, dn_norm_w, sinks, w_branch_a, w_branch_b, w_out, ln1_g, ln1_b, w_gate_up, w_down, ln2_g, ln2_b, loss_target, m_w_in, m_conv_w, m_a_log, m_dt_bias, m_dn_norm_w, m_sinks, m_w_branch_a, m_w_branch_b, m_w_out, m_ln1_g, m_ln1_b, m_w_gate_up, m_w_down, m_ln2_g, m_ln2_b, v_w_in, v_conv_w, v_a_log, v_dt_bias, v_dn_norm_w, v_sinks, v_w_branch_a, v_w_branch_b, v_w_out, v_ln1_g, v_ln1_b, v_w_gate_up, v_w_down, v_ln2_g, v_ln2_b):
    xi, yi, ci = _place()
    myq = 2 * xi + yi
    t_ = x.shape[1]
    cos, sin = _rope_tables(t_)

    loc_in = jnp.pad(w_in.astype(BF16), ((0, 0), (0, 0), (0, IN_PAD - IN_SHARD)))
    locs = [loc_in, w_branch_a.astype(BF16), w_branch_b.astype(BF16), w_out.astype(BF16), w_gate_up.astype(BF16),
            w_down.astype(BF16), conv_w]
    kinds = ["col", "row", "row", "row", "colx", "row", "col"]
    gathers = []
    for l in range(DEPTH):
        srcs = [a[l] for a in locs]
        if gathers:
            srcs[-1] = srcs[-1] + gathers[-1][1][0, 0]
        gathers.append(_gather_start(srcs, kinds, "gather_%d_start" % l))

    def in_weights(l, after):
        _, (full_in,) = _split_wait(tuple(part[:1] for part in gathers[l][0]), after, "gather_%d_wait_in" % l)
        cols = lambda a, b: _orig_cols(full_in, a, b)
        return dict(
            in_main=jnp.concatenate(cols(0, R_BG) + cols(R_GATES, IN_COLS) + cols(R_SW, R_GATES), axis=1),
            in_bg=jnp.pad(jnp.concatenate(cols(R_BG, R_SW), axis=1), ((0, 0), (0, 128 - 4 * DN_HEADS))))

    def rest_weights(l, after):
        _, (full_a, full_b, full_o, full_gu, full_d, full_conv) = _split_wait(
            tuple(part[1:] for part in gathers[l][0]), after, "gather_%d_wait_rest" % l)
        return dict(
            conv=full_conv, arow=_lane_row(a_log[l]), dtrow=_lane_row(dt_bias[l]), gnw=dn_norm_w[l][None],
            sinks=sinks[l][None], a=full_a.reshape(D, D), b=full_b.reshape(D, D), o=full_o.reshape(D, D),
            ln1g=ln1_g[l][None], ln1b=ln1_b[l][None], gu=full_gu, d=full_d.reshape(FFN, D),
            ln2g=ln2_g[l][None], ln2b=ln2_b[l][None])

    h = x[0]
    hm = h.astype(_MXU)
    residuals = []
    for l in range(DEPTH):
        win = in_weights(l, gathers[-1][1] if l == 0 else h)
        h, hm, res = _layer_fwd(h, hm, win, functools.partial(rest_weights, l), cos, sin)
        residuals.append(res)
    dh, sq = _loss_head(h, loss_target[0])
    loss = lax.psum((0.5 / D) * jnp.sum(sq), ("x", "y", "c"))

    big = [None] * DEPTH
    small = [None] * DEPTH
    hop1 = hop2 = None

    def mid(after):
        nonlocal hop1, hop2
        if hop1 is None:
            return None
        handle, tok = _rs_middle(hop1[1], after, RS_META, ci, "rs_chips_%d" % hop1[0])
        hop1, hop2 = None, (hop1[0], handle)
        return tok

    token = None
    for l in reversed(range(DEPTH)):
        dh, g = _layer_bwd(dh, residuals[l], residuals[l]["w"], cos, sin, mid, token)
        if hop2 is not None:
            big[hop2[0]] = _rs_end(hop2[1], dh, RS_META, ci, myq, "rs_chips_%d" % hop2[0])
        g_in = _to_padded_shards(g["in_main"], g["in_bg"])
        handle, token = _rs_sibling_start([g_in, g["a"], g["b"], g["o"], g["gu"], g["d"]], RS_META, "rs_sib_%d_start" % l)
        hop1 = (l, handle)
        small[l] = _pack_small(g)
    tot = _allreduce_small(jnp.concatenate(small, axis=0))
    token = mid(tot)
    pending = hop2
    sm = [_unpack_small(tot[l * SMALL_ROWS:(l + 1) * SMALL_ROWS]) for l in range(DEPTH)]
    stack = lambda name: jnp.stack([s[name] for s in sm], axis=0)
    grads = dict(
        conv_w=lax.dynamic_slice_in_dim(stack("conv"), myq * (3 * D // N_CHIPS), 3 * D // N_CHIPS, axis=2),
        a_log=stack("a_log"), dt_bias=stack("dt_bias"), dn_norm_w=stack("gnw"), sinks=stack("sinks"),
        ln1_g=stack("ln1g"), ln1_b=stack("ln1b"), ln2_g=stack("ln2g"), ln2_b=stack("ln2b"))
    weights = dict(w_in=w_in, conv_w=conv_w, a_log=a_log, dt_bias=dt_bias, dn_norm_w=dn_norm_w, sinks=sinks,
                   w_branch_a=w_branch_a, w_branch_b=w_branch_b, w_out=w_out, ln1_g=ln1_g, ln1_b=ln1_b,
                   w_gate_up=w_gate_up, w_down=w_down, ln2_g=ln2_g, ln2_b=ln2_b)
    ms = dict(w_in=m_w_in, conv_w=m_conv_w, a_log=m_a_log, dt_bias=m_dt_bias, dn_norm_w=m_dn_norm_w, sinks=m_sinks,
              w_branch_a=m_w_branch_a, w_branch_b=m_w_branch_b, w_out=m_w_out, ln1_g=m_ln1_g, ln1_b=m_ln1_b,
              w_gate_up=m_w_gate_up, w_down=m_w_down, ln2_g=m_ln2_g, ln2_b=m_ln2_b)
    vs = dict(w_in=v_w_in, conv_w=v_conv_w, a_log=v_a_log, dt_bias=v_dt_bias, dn_norm_w=v_dn_norm_w, sinks=v_sinks,
              w_branch_a=v_w_branch_a, w_branch_b=v_w_branch_b, w_out=v_w_out, ln1_g=v_ln1_g, ln1_b=v_ln1_b,
              w_gate_up=v_w_gate_up, w_down=v_w_down, ln2_g=v_ln2_g, ln2_b=v_ln2_b)
    names = list(weights)
    upd = {n: _adamw("adamw_" + n, weights[n], grads[n], ms[n], vs[n]) for n in grads}
    big_names = ["w_in", "w_branch_a", "w_branch_b", "w_out", "w_gate_up", "w_down"]
    carry = {n: None for n in big_names}

    def update_layer(l):
        for t, n in enumerate(big_names):
            carry[n] = _adamw_layer("adamw_" + n, l, weights[n], big[l][t], ms[n], vs[n], carry[n], token)

    for l in range(DEPTH - 1, pending[0], -1):
        update_layer(l)
    big[pending[0]] = _rs_end(pending[1], carry[big_names[-1]][1], RS_META, ci, myq, "rs_chips_%d" % pending[0])
    update_layer(pending[0])
    for n in big_names:
        grads[n], upd[n] = carry[n][0], carry[n][1:]
    return (loss, dh[None], *[grads[n] for n in names], *[upd[n][0] for n in names], *[upd[n][1] for n in names],
            *[upd[n][2] for n in names])
```

```python
import functools

import jax
import jax.numpy as jnp
from jax import lax
from jax.experimental import pallas as pl
from jax.experimental.pallas import tpu as pltpu

F32 = jnp.float32
BF16 = jnp.bfloat16
_MXU = BF16

D = 1024
DEPTH = 4
DN_HEADS = 8
DN_DIM = 128
DN_CONV = 5
CHUNK = 64
SW_HEADS = 16
SW_KV = 4
SW_DIM = 64
SW_GRP = SW_HEADS // SW_KV
SW_BLOCK = 128
ROPE_THETA = 10000.0
FFN = 2816
ALPHA = (2.0 * DEPTH) ** 0.25
LN_EPS = 1e-5
RMS_EPS = 1e-6
IN_COLS = 7712
O_Z, O_GA, O_GB, O_QS, O_KS, O_VS, N_MAIN = 3072, 4096, 5120, 6144, 7168, 7424, 7680
R_BG, R_SW, R_GATES = 4096, 4128, 5664
N_CHIPS = 4
IN_SHARD = IN_COLS // N_CHIPS
IN_PAD = 2048
ADAM_LR, ADAM_B1, ADAM_B2, ADAM_EPS, ADAM_WD, ADAM_STEP = 0.001, 0.9, 0.999, 1e-08, 0.01, 10
VMEM_LIMIT = 52 * 1024 * 1024
MESH = pl.DeviceIdType.MESH
ANY = pl.BlockSpec(memory_space=pl.ANY)


def _params(n_grid, **kw):
    return pltpu.CompilerParams(dimension_semantics=("arbitrary",) * n_grid, vmem_limit_bytes=VMEM_LIMIT, **kw)


def _full(a):
    nd = a.ndim
    return pl.BlockSpec(a.shape, lambda *_, nd=nd: (0,) * nd)


def _raw_dot(a, b, ca, cb):
    return lax.dot_general(a.astype(_MXU), b.astype(_MXU), (((ca,), (cb,)), ((), ())), preferred_element_type=F32)


@jax.custom_vjp
def _nn(a, b):
    return _raw_dot(a, b, 1, 0)


@jax.custom_vjp
def _nt(a, b):
    return _raw_dot(a, b, 1, 1)


@jax.custom_vjp
def _tn(a, b):
    if a.shape[1] > b.shape[1]:
        return _raw_dot(b, a, 0, 0).T
    return _raw_dot(a, b, 0, 0)


_nn.defvjp(lambda a, b: (_nn(a, b), (a, b)), lambda r, g: (_nt(g, r[1]), _tn(r[0], g)))
_nt.defvjp(lambda a, b: (_nt(a, b), (a, b)), lambda r, g: (_nn(g, r[1]), _tn(g, r[0])))
_tn.defvjp(lambda a, b: (_tn(a, b), (a, b)), lambda r, g: (_nt(r[1], g), _nn(r[0], g)))


def _hdot(a, b, ca=1, cb=0):
    ah, bh = a.astype(BF16), b.astype(BF16)
    al, bl = (a - ah.astype(F32)).astype(BF16), (b - bh.astype(F32)).astype(BF16)
    dot = lambda u, v: lax.dot_general(u, v, (((ca,), (cb,)), ((), ())), preferred_element_type=F32)
    return dot(ah, bh) + (dot(ah, bl) + dot(al, bh))


def _inv_impl(mats):
    n = mats[0].shape[0]
    eye = (lax.broadcasted_iota(jnp.int32, (n, n), 0) == lax.broadcasted_iota(jnp.int32, (n, n), 1)).astype(F32)
    ps = [-a for a in mats]
    ts = [eye + p for p in ps]
    for _ in range(max(1, (n - 1).bit_length()) - 1):
        ps = [_hdot(p, p) for p in ps]
        ts = [t + _hdot(t, p) for t, p in zip(ts, ps)]
    return tuple(ts)


@jax.custom_vjp
def _inv(mats):
    return _inv_impl(mats)


def _inv_fwd(mats):
    ts = _inv_impl(mats)
    return ts, ts


def _inv_bwd(ts, gs):
    xs = [_hdot(t, g, 0, 0) for t, g in zip(ts, gs)]
    return (tuple(-_hdot(x, t, 1, 1) for x, t in zip(xs, ts)),)


_inv.defvjp(_inv_fwd, _inv_bwd)


@jax.custom_vjp
def _inv_saved(mats, saved):
    return saved


_inv_saved.defvjp(lambda mats, saved: (saved, saved),
                  lambda ts, gs: (_inv_bwd(ts, gs)[0], tuple(jnp.zeros_like(t) for t in ts)))


def _tile(n, cap):
    if n <= cap:
        return n
    best = [t for t in range(128, cap + 1, 128) if n % t == 0]
    assert best, (n, cap)
    return best[-1]


def _mm(a, b, *, name, ta=False, tb=False, add=None, tm=1024, tn=1024, tk=1024, after=None, out_dtype=F32):
    if ta:
        k_, m_ = a.shape
    else:
        m_, k_ = a.shape
    n_ = b.shape[0] if tb else b.shape[1]
    tm, tn, tk = _tile(m_, tm), _tile(n_, tn), _tile(k_, tk)
    nk = k_ // tk
    has_add = add is not None

    def body(*refs):
        a_ref, b_ref = refs[:2]
        add_ref = refs[2] if has_add else None
        o_ref = refs[2 + has_add + (after is not None)]
        part = _raw_dot(a_ref[...], b_ref[...], 0 if ta else 1, 1 if tb else 0)
        if nk == 1:
            o_ref[...] = (part + add_ref[...] if has_add else part).astype(o_ref.dtype)
            return
        acc = refs[-1]
        k = pl.program_id(2)

        @pl.when(k == 0)
        def _():
            acc[...] = part

        @pl.when(jnp.logical_and(k > 0, k < nk - 1))
        def _():
            acc[...] += part

        @pl.when(k == nk - 1)
        def _():
            o_ref[...] = (acc[...] + part + add_ref[...] if has_add else acc[...] + part).astype(o_ref.dtype)

    a_spec = pl.BlockSpec((tk, tm), lambda i, j, k: (k, i)) if ta else pl.BlockSpec((tm, tk), lambda i, j, k: (i, k))
    b_spec = pl.BlockSpec((tn, tk), lambda i, j, k: (j, k)) if tb else pl.BlockSpec((tk, tn), lambda i, j, k: (k, j))
    o_spec = pl.BlockSpec((tm, tn), lambda i, j, k: (i, j))
    in_specs = [a_spec, b_spec] + ([o_spec] if has_add else []) + ([ANY] if after is not None else [])
    args = (a, b) + ((add,) if has_add else ()) + ((after,) if after is not None else ())
    return pl.pallas_call(
        body, name=name, grid=(m_ // tm, n_ // tn, nk), in_specs=in_specs, out_specs=o_spec,
        out_shape=jax.ShapeDtypeStruct((m_, n_), out_dtype),
        scratch_shapes=[pltpu.VMEM((tm, tn), F32)] if nk > 1 else [],
        compiler_params=_params(3))(*args)


def _mm_fused(a, b, post, outs, *, name, rows=(), params=(), n_sums=0, tb=False, tm=1024, tn=1024, tk=1024):
    m_, k_ = a.shape
    n_ = b.shape[0] if tb else b.shape[1]
    tm, tn, tk = _tile(m_, tm), _tile(n_, tn), _tile(k_, tk)
    nk = k_ // tk
    nr, npar, no = len(rows), len(params), len(outs)
    aliased = [o[4] for o in outs if o[4] is not None]
    nin = 2 + nr + npar

    def body(*refs):
        a_ref, b_ref = refs[:2]
        row_refs, par_refs = refs[2:2 + nr], refs[2 + nr:nin]
        out_refs = refs[nin + len(aliased):nin + len(aliased) + no + n_sums]
        part = _raw_dot(a_ref[...], b_ref[...], 1, 1 if tb else 0)
        k = pl.program_id(2)
        first = jnp.logical_and(pl.program_id(0) == 0, pl.program_id(1) == 0)
        if nk > 1:
            acc = refs[-1]

            @pl.when(k == 0)
            def _():
                acc[...] = part

            @pl.when(jnp.logical_and(k > 0, k < nk - 1))
            def _():
                acc[...] += part

        @pl.when(k == nk - 1)
        def _():
            total = acc[...] + part if nk > 1 else part
            res = post(total, *[r[...].astype(F32) for r in row_refs], *[p[...] for p in par_refs])
            for o_ref, val in zip(out_refs[:no], res[:no]):
                o_ref[...] = val.astype(o_ref.dtype)
            for s_ref, val in zip(out_refs[no:], res[no:]):
                @pl.when(first)
                def _(s_ref=s_ref):
                    s_ref[...] = jnp.zeros_like(s_ref)
                s_ref[...] += val

    b_spec = pl.BlockSpec((tn, tk), lambda i, j, k: (j, k)) if tb else pl.BlockSpec((tk, tn), lambda i, j, k: (k, j))
    in_specs = [pl.BlockSpec((tm, tk), lambda i, j, k: (i, k)), b_spec]
    in_specs += [pl.BlockSpec((tm, w), lambda i, j, k, cb=cb: (i, cb(j))) for _, w, cb in rows]
    in_specs += [_full(p) for p in params] + [ANY] * len(aliased)
    out_specs = [pl.BlockSpec((tm, w), lambda i, j, k, cb=cb: (i, cb(j))) for _, w, _, cb, _ in outs]
    out_specs += [_full(p) for p in params[:n_sums]]
    out_shape = [jax.ShapeDtypeStruct((m_, tot), dt) for tot, _, dt, _, _ in outs]
    out_shape += [jax.ShapeDtypeStruct(p.shape, F32) for p in params[:n_sums]]
    aliases, pos = {}, nin
    for oi, o in enumerate(outs):
        if o[4] is not None:
            aliases[pos] = oi
            pos += 1
    return pl.pallas_call(
        body, name=name, grid=(m_ // tm, n_ // tn, nk), in_specs=in_specs, out_specs=out_specs, out_shape=out_shape,
        scratch_shapes=[pltpu.VMEM((tm, tn), F32)] if nk > 1 else [], input_output_aliases=aliases,
        compiler_params=_params(3))(a, b, *[r[0] for r in rows], *params, *aliased)


def _mm_pre(pre, rows, params, b, *, name, tm=512):
    m_ = rows[0][0].shape[0]
    k_, n_ = b.shape
    tm = _tile(m_, tm)
    nr, npar = len(rows), len(params)

    def body(*refs):
        b_ref, a_out, o_ref = refs[nr + npar:]
        a = pre(*[r[...].astype(F32) for r in refs[:nr]], *[p[...] for p in refs[nr:nr + npar]]).astype(a_out.dtype)
        a_out[...] = a
        o_ref[...] = _raw_dot(a, b_ref[...], 1, 0)

    return pl.pallas_call(
        body, name=name, grid=(m_ // tm,),
        in_specs=[pl.BlockSpec((tm, w), lambda i, cb=cb: (i, cb)) for _, w, cb in rows] + [_full(p) for p in params]
        + [_full(b)],
        out_specs=[pl.BlockSpec((tm, k_), lambda i: (i, 0)), pl.BlockSpec((tm, n_), lambda i: (i, 0))],
        out_shape=[jax.ShapeDtypeStruct((m_, k_), _MXU), jax.ShapeDtypeStruct((m_, n_), F32)],
        compiler_params=_params(1))(*[r[0] for r in rows], *params, b)


def _row_spec(tb, w, c0, percol):
    return pl.BlockSpec((tb, w), lambda i, j, c0=c0, pc=percol: (i, c0 + (j if pc else 0)))


def _stage_fwd(f, name, rows, params, outs, tb, ncol=1):
    t_ = rows[0][0].shape[0]
    nr, npar = len(rows), len(params)

    def body(*refs):
        res = f(*[r[...].astype(F32) for r in refs[:nr + npar]])
        for o_ref, val in zip(refs[nr + npar:], res):
            o_ref[...] = val.astype(o_ref.dtype)

    return pl.pallas_call(
        body, name=name, grid=(t_ // tb, ncol),
        in_specs=[_row_spec(tb, w, c0, pc) for (_, w, c0, pc) in rows] + [_full(p) for p in params],
        out_specs=[pl.BlockSpec((tb, w), lambda i, j: (i, j)) for w, _ in outs],
        out_shape=[jax.ShapeDtypeStruct((t_, w * ncol), dt) for w, dt in outs],
        compiler_params=_params(2))(*[r[0] for r in rows], *params)


def _into(dest, tb, width, t_, ncol, dtype):
    if dest is None:
        return pl.BlockSpec((tb, width), lambda i, j: (i, j)), jax.ShapeDtypeStruct((t_, width * ncol), dtype), None
    buf, total, c0 = dest
    return (pl.BlockSpec((tb, width), lambda i, j, c0=c0: (i, c0 + j)), jax.ShapeDtypeStruct((t_, total), dtype), buf)


def _stage_bwd(f, name, rows, params, douts, tb, ncol=1, cat=None, dtypes=None, dest=None, after=None):
    t_ = rows[0][0].shape[0]
    nr, npar, nd = len(rows), len(params), len(douts)
    cat = cat if cat is not None else [[r] for r in range(nr)]
    dtypes = dtypes if dtypes is not None else [F32] * len(cat)
    dest = dest or {}
    assert ncol == 1 or all(len(g) == 1 and rows[g[0]][3] for g in cat)
    nin = nr + npar + nd
    unread = [after] if after is not None else []

    def body(*refs):
        ins = [r[...].astype(F32) for r in refs[:nr + npar]]
        dvals = tuple(r[...].astype(F32) for r in refs[nr + npar:nin])
        out_refs = refs[nin + len(aliased) + len(unread):]
        _, vjp = jax.vjp(f, *ins)
        grads = vjp(dvals)
        for o_ref, grp in zip(out_refs[:len(cat)], cat):
            val = grads[grp[0]] if len(grp) == 1 else jnp.concatenate([grads[r] for r in grp], axis=-1)
            o_ref[...] = val.astype(o_ref.dtype)
        first = jnp.logical_and(pl.program_id(0) == 0, pl.program_id(1) == 0)
        for p_ref, gp in zip(out_refs[len(cat):], grads[nr:]):
            @pl.when(first)
            def _(p_ref=p_ref):
                p_ref[...] = jnp.zeros_like(p_ref)
            p_ref[...] += gp

    gw = [sum(rows[r][1] for r in grp) for grp in cat]
    out_specs, out_shape, aliased, aliases = [], [], [], {}
    for gi, (w, dt) in enumerate(zip(gw, dtypes)):
        spec, shape, buf = _into(dest.get(gi), tb, w, t_, ncol, dt)
        out_specs.append(spec)
        out_shape.append(shape)
        if buf is not None:
            aliases[nin + len(aliased)] = gi
            aliased.append(buf)
    return pl.pallas_call(
        body, name=name, grid=(t_ // tb, ncol),
        in_specs=[_row_spec(tb, w, c0, pc) for (_, w, c0, pc) in rows] + [_full(p) for p in params]
        + [pl.BlockSpec((tb, d.shape[1] // ncol), lambda i, j: (i, j)) for d in douts]
        + [ANY] * (len(aliased) + len(unread)),
        out_specs=out_specs + [_full(p) for p in params],
        out_shape=out_shape + [jax.ShapeDtypeStruct(p.shape, F32) for p in params],
        input_output_aliases=aliases,
        compiler_params=_params(2))(*[r[0] for r in rows], *params, *douts, *aliased, *unread)


def _ln_f(x, y, g, b):
    u = ALPHA * x + y
    c = u - jnp.mean(u, axis=-1, keepdims=True)
    var = jnp.mean(c * c, axis=-1, keepdims=True)
    return (c * lax.rsqrt(var + LN_EPS) * g + b,)


def _ln_f2(x, y, g, b):
    out, = _ln_f(x, y, g, b)
    return out, out


def _swiglu_tile(gu):
    half = gu.shape[1] // 2
    return (jax.nn.silu(gu[:, :half]) * gu[:, half:],)


def _merge_f(ya, yb, ga, gb):
    return (jax.nn.sigmoid(ga) * ya + jax.nn.sigmoid(gb) * yb,)


def _gnorm_f(of, ob, z, w):
    o = of + ob
    return (o * lax.rsqrt(jnp.mean(o * o, axis=-1, keepdims=True) + RMS_EPS) * w * jax.nn.silu(z),)


def _bg_f(x, arow, dtrow):
    lane = lax.broadcasted_iota(jnp.int32, x.shape, 1)
    beta = jax.nn.sigmoid(x)
    g = -jnp.exp(arow) * jax.nn.softplus(x + dtrow)
    return (jnp.where(lane < 16, beta, jnp.where(lane < 32, g, 0.0)),)


PREP_ROWS = 512
PAD = 8


def _prep_f(part, w, *wins):
    xc = wins[0] * w[0:1, :]
    for k in range(1, DN_CONV):
        xc = xc + wins[k] * w[k:k + 1, :]
    a = jax.nn.silu(xc)
    nrm = a * lax.rsqrt(jnp.sum(a * a, axis=-1, keepdims=True) + RMS_EPS)
    return jnp.where(part == 0, nrm * (DN_DIM ** -0.5), jnp.where(part == 1, nrm, a))


def _windows(pad_ref, r0, rows):
    return [pad_ref[PAD + r0 - 2 + k:PAD + r0 - 2 + k + rows, :] for k in range(DN_CONV)]


def _prep_fwd(pm, conv):
    t_ = pm.shape[0]
    rows = min(PREP_ROWS, t_)

    def body(x_ref, w_ref, o_ref, pad_ref):
        part = pl.program_id(0) // DN_HEADS
        pad_ref[0:PAD, :] = jnp.zeros((PAD, DN_DIM), F32)
        pad_ref[PAD + t_:2 * PAD + t_, :] = jnp.zeros((PAD, DN_DIM), F32)
        pad_ref[PAD:PAD + t_, :] = x_ref[...]
        w = w_ref[...]
        for r in range(t_ // rows):
            o_ref[r * rows:(r + 1) * rows, :] = _prep_f(part, w, *_windows(pad_ref, r * rows, rows))

    ncb = 3 * DN_HEADS
    return pl.pallas_call(
        body, name="prep_fwd", grid=(ncb,),
        in_specs=[pl.BlockSpec((t_, DN_DIM), lambda j: (0, j)), pl.BlockSpec((DN_CONV, DN_DIM), lambda j: (0, j))],
        out_specs=pl.BlockSpec((t_, DN_DIM), lambda j: (0, j)),
        out_shape=jax.ShapeDtypeStruct((t_, ncb * DN_DIM), F32),
        scratch_shapes=[pltpu.VMEM((t_ + 2 * PAD, DN_DIM), F32)],
        compiler_params=_params(1))(pm, conv)


def _prep_bwd(pm, conv, dout, dpm):
    t_ = pm.shape[0]
    rows = min(PREP_ROWS, t_)

    def body(x_ref, w_ref, d_ref, _, dx_ref, dw_ref, pad_ref, dpad_ref):
        part = pl.program_id(0) // DN_HEADS
        pad_ref[0:PAD, :] = jnp.zeros((PAD, DN_DIM), F32)
        pad_ref[PAD + t_:2 * PAD + t_, :] = jnp.zeros((PAD, DN_DIM), F32)
        pad_ref[PAD:PAD + t_, :] = x_ref[...]
        dpad_ref[...] = jnp.zeros_like(dpad_ref)
        w = w_ref[...]
        dw = jnp.zeros((DN_CONV, DN_DIM), F32)
        for r in range(t_ // rows):
            r0 = r * rows
            _, vjp = jax.vjp(functools.partial(_prep_f, part), w, *_windows(pad_ref, r0, rows))
            grads = vjp(d_ref[r0:r0 + rows, :])
            dw = dw + grads[0]
            for k in range(DN_CONV):
                lo = PAD + r0 - 2 + k
                dpad_ref[lo:lo + rows, :] += grads[1 + k]
        dx_ref[...] = dpad_ref[PAD:PAD + t_, :].astype(dx_ref.dtype)
        dw_ref[...] = dw

    ncb = 3 * DN_HEADS
    col = pl.BlockSpec((t_, DN_DIM), lambda j: (0, j))
    wsp = pl.BlockSpec((DN_CONV, DN_DIM), lambda j: (0, j))
    return pl.pallas_call(
        body, name="prep_bwd", grid=(ncb,), in_specs=[col, wsp, col, ANY], out_specs=[col, wsp],
        out_shape=[jax.ShapeDtypeStruct(dpm.shape, dpm.dtype), jax.ShapeDtypeStruct((DN_CONV, ncb * DN_DIM), F32)],
        scratch_shapes=[pltpu.VMEM((t_ + 2 * PAD, DN_DIM), F32), pltpu.VMEM((t_ + 2 * PAD, DN_DIM), F32)],
        input_output_aliases={3: 0}, compiler_params=_params(1))(pm, conv, dout, dpm)


def _dn_chunk(sgns, qs, ks, vs, grows, brows, tsaved=None, with_t=False):
    c = qs[0].shape[0]
    i = lax.broadcasted_iota(jnp.int32, (c, c), 0)
    j = lax.broadcasted_iota(jnp.int32, (c, c), 1)
    eye = i == j
    incl = {s: (i - j) * int(s) >= 0 for s in set(sgns)}
    strict = {s: (i - j) * int(s) > 0 for s in set(sgns)}
    gcs = [jnp.sum(jnp.where(incl[s], g, 0.0), axis=1, keepdims=True) for s, g in zip(sgns, grows)]
    grs = [jnp.sum(jnp.where(eye, gc, 0.0), axis=0, keepdims=True) for gc in gcs]
    bcs = [jnp.sum(jnp.where(eye, b, 0.0), axis=1, keepdims=True) for b in brows]
    gls = [jnp.sum(g, axis=1, keepdims=True) for g in grows]
    decs = [jnp.exp(jnp.where(incl[s], gc - gr, -1e30)) for s, gc, gr in zip(sgns, gcs, grs)]
    kks = [_nt(k, k) for k in ks]
    amats = tuple(jnp.where(strict[s], bc * kk * dec, 0.0) for s, bc, kk, dec in zip(sgns, bcs, kks, decs))
    tinvs = _inv(amats) if tsaved is None else _inv_saved(amats, tsaved)
    egcs = [jnp.exp(gc) for gc in gcs]
    us = [_nn(t, v * bc) for t, v, bc in zip(tinvs, vs, bcs)]
    ws = [_nn(t, k * (bc * egc)) for t, k, bc, egc in zip(tinvs, ks, bcs, egcs)]
    qks = [_nt(q, k) * dec for q, k, dec in zip(qs, ks, decs)]
    qds = [q * egc for q, egc in zip(qs, egcs)]
    kds = [k * jnp.exp(gl - gc) for k, gl, gc in zip(ks, gls, gcs)]
    res = tuple(us), tuple(ws), tuple(qks), tuple(qds), tuple(kds)
    return res + (tinvs,) if with_t else res


def _dn_step(us, ws, qks, qds, kds, grows, ss):
    gls = [jnp.exp(jnp.sum(g, axis=1, keepdims=True)) for g in grows]
    wss = [_nn(w, s) for w, s in zip(ws, ss)]
    qss = [_nn(qd, s) for qd, s in zip(qds, ss)]
    vns = [u - x for u, x in zip(us, wss)]
    os_ = [a + _nn(qk, vn) for a, qk, vn in zip(qss, qks, vns)]
    s2s = [s * gl + _tn(kd, vn) for s, gl, kd, vn in zip(ss, gls, kds, vns)]
    return tuple(os_), tuple(s2s)


def _hs(h):
    return slice(h * DN_DIM, (h + 1) * DN_DIM)


_DIR_SGN = (1, -1)
_A_OUT = 5
_PROBLEMS = [(d, h) for d in range(2) for h in range(DN_HEADS)]
_SGNS = [_DIR_SGN[d] for d, _ in _PROBLEMS]


def _chunk_inputs(q_ref, k_ref, v_ref, g_ref, b_ref):
    heads = lambda ref: tuple(ref[:, _hs(h)].astype(F32) for _, h in _PROBLEMS)
    rows = lambda ref: tuple(ref[d, 0, h:h + 1, :] for d, h in _PROBLEMS)
    return heads(q_ref), heads(k_ref), heads(v_ref), rows(g_ref), rows(b_ref)


def _dn_a_fwd(qkv, grows, brows):
    t_ = qkv.shape[0]
    nch = t_ // CHUNK

    def body(q_ref, k_ref, v_ref, g_ref, b_ref, *outs):
        us, ws, qks, qds, kds, tinvs = _dn_chunk(_SGNS, *_chunk_inputs(q_ref, k_ref, v_ref, g_ref, b_ref), with_t=True)
        for p, (d, h) in enumerate(_PROBLEMS):
            u_ref, w_ref, qk_ref, qd_ref, kd_ref = outs[d * _A_OUT:(d + 1) * _A_OUT]
            u_ref[:, _hs(h)], w_ref[:, _hs(h)], qk_ref[0, h] = us[p], ws[p].astype(_MXU), qks[p].astype(_MXU)
            qd_ref[:, _hs(h)], kd_ref[:, _hs(h)] = qds[p].astype(_MXU), kds[p].astype(_MXU)
            outs[2 * _A_OUT + d][0, h] = tinvs[p]

    rspec = pl.BlockSpec((2, 1, DN_HEADS, CHUNK), lambda c: (0, c, 0, 0))
    big = pl.BlockSpec((CHUNK, D), lambda c: (c, 0))
    qks = pl.BlockSpec((1, DN_HEADS, CHUNK, CHUNK), lambda c: (c, 0, 0, 0))
    bigs = lambda dt: jax.ShapeDtypeStruct((t_, D), dt)
    qksh = lambda dt: jax.ShapeDtypeStruct((nch, DN_HEADS, CHUNK, CHUNK), dt)
    res = pl.pallas_call(
        body, name="dn_a_fwd", grid=(nch,),
        in_specs=[pl.BlockSpec((CHUNK, D), lambda c, p=p: (c, p)) for p in range(3)] + [rspec, rspec],
        out_specs=[big, big, qks, big, big] * 2 + [qks, qks],
        out_shape=[bigs(F32), bigs(_MXU), qksh(_MXU), bigs(_MXU), bigs(_MXU)] * 2 + [qksh(F32)] * 2,
        compiler_params=_params(1))(qkv, qkv, qkv, grows, brows)
    return res[:2 * _A_OUT], res[2 * _A_OUT:]


def _dn_a_bwd(qkv, grows, brows, tinv, dres, dg_b):
    t_ = qkv.shape[0]
    nch = t_ // CHUNK

    def body(q_ref, k_ref, v_ref, g_ref, b_ref, tf_ref, tb_ref, *rest):
        dins, dgb_ref, (dqkv_ref, dg_ref, db_ref) = rest[:2 * _A_OUT], rest[2 * _A_OUT], rest[2 * _A_OUT + 1:]
        tsaved = tuple((tf_ref, tb_ref)[d][0, h] for d, h in _PROBLEMS)
        _, vjp = jax.vjp(functools.partial(_dn_chunk, _SGNS, tsaved=tsaved),
                         *_chunk_inputs(q_ref, k_ref, v_ref, g_ref, b_ref))
        cots = []
        for o in range(_A_OUT):
            cots.append(tuple(dins[d * _A_OUT + o][0, h] if o == 2 else dins[d * _A_OUT + o][:, _hs(h)]
                              for d, h in _PROBLEMS))
        gq, gk, gv, gg, gb = vjp(tuple(cots))
        for p, (d, h) in enumerate(_PROBLEMS):
            dg_ref[d, 0, h:h + 1, :] = gg[p] + dgb_ref[d, 0, h:h + 1, :]
            db_ref[d, 0, h:h + 1, :] = gb[p]
        for h in range(DN_HEADS):
            dqkv_ref[:, _hs(h)] = gq[h] + gq[DN_HEADS + h]
            dqkv_ref[:, _hs(DN_HEADS + h)] = gk[h] + gk[DN_HEADS + h]
            dqkv_ref[:, _hs(2 * DN_HEADS + h)] = gv[h] + gv[DN_HEADS + h]

    rspec = pl.BlockSpec((2, 1, DN_HEADS, CHUNK), lambda c: (0, c, 0, 0))
    big = pl.BlockSpec((CHUNK, D), lambda c: (c, 0))
    qks = pl.BlockSpec((1, DN_HEADS, CHUNK, CHUNK), lambda c: (c, 0, 0, 0))
    rsh = jax.ShapeDtypeStruct(grows.shape, F32)
    return pl.pallas_call(
        body, name="dn_a_bwd", grid=(nch,),
        in_specs=[pl.BlockSpec((CHUNK, D), lambda c, p=p: (c, p)) for p in range(3)] + [rspec, rspec, qks, qks]
        + [big, big, qks, big, big] * 2 + [rspec],
        out_specs=[pl.BlockSpec((CHUNK, 3 * D), lambda c: (c, 0)), rspec, rspec],
        out_shape=[jax.ShapeDtypeStruct((t_, 3 * D), F32), rsh, rsh],
        compiler_params=_params(1))(qkv, qkv, qkv, grows, brows, *tinv, *dres, dg_b)


def _dir_specs(nch):
    def cidx(d):
        return (lambda n: n) if d == 0 else (lambda n: nch - 1 - n)
    out = []
    for d in range(2):
        ci = cidx(d)
        big = pl.BlockSpec((CHUNK, D), lambda n, ci=ci: (ci(n), 0))
        qks = pl.BlockSpec((1, DN_HEADS, CHUNK, CHUNK), lambda n, ci=ci: (ci(n), 0, 0, 0))
        row = pl.BlockSpec((1, 1, DN_HEADS, CHUNK), lambda n, ci=ci, d=d: (d, ci(n), 0, 0))
        st = pl.BlockSpec((1, DN_HEADS, DN_DIM, DN_DIM), lambda n, ci=ci: (ci(n), 0, 0, 0))
        out.append(dict(big=big, qk=qks, row=row, st=st))
    return out


def _step_inputs(ins, per_dir):
    def pick(o):
        if o == 2:
            return tuple(ins[d * per_dir + o][0, h].astype(F32) for d, h in _PROBLEMS)
        if o == 5:
            return tuple(ins[d * per_dir + o][0, 0, h:h + 1, :] for d, h in _PROBLEMS)
        return tuple(ins[d * per_dir + o][:, _hs(h)].astype(F32) for d, h in _PROBLEMS)
    return [pick(o) for o in range(6)]


def _dn_b_fwd(ares, grows):
    t_ = ares[0].shape[0]
    nch = t_ // CHUNK
    sp = _dir_specs(nch)

    def body(*refs):
        ins, outs, s_ref = refs[:12], refs[12:16], refs[16]

        @pl.when(pl.program_id(0) == 0)
        def _():
            s_ref[...] = jnp.zeros_like(s_ref)

        ss = tuple(s_ref[p] for p in range(len(_PROBLEMS)))
        os_, s2s = _dn_step(*_step_inputs(ins, 6), ss)
        for p, (d, h) in enumerate(_PROBLEMS):
            outs[2 + d][0, h] = ss[p]
            outs[d][:, _hs(h)] = os_[p]
            s_ref[p] = s2s[p]

    in_specs, args = [], []
    for d in range(2):
        in_specs += [sp[d]["big"], sp[d]["big"], sp[d]["qk"], sp[d]["big"], sp[d]["big"], sp[d]["row"]]
        args += list(ares[d * _A_OUT:(d + 1) * _A_OUT]) + [grows]
    stsh = jax.ShapeDtypeStruct((nch, DN_HEADS, DN_DIM, DN_DIM), F32)
    osh = jax.ShapeDtypeStruct((t_, D), F32)
    return pl.pallas_call(
        body, name="dn_b_fwd", grid=(nch,), in_specs=in_specs,
        out_specs=[sp[0]["big"], sp[1]["big"], sp[0]["st"], sp[1]["st"]], out_shape=[osh, osh, stsh, stsh],
        scratch_shapes=[pltpu.VMEM((2 * DN_HEADS, DN_DIM, DN_DIM), F32)],
        compiler_params=_params(1))(*args)


def _dn_b_bwd(ares, grows, st_f, st_b, do):
    t_ = ares[0].shape[0]
    nch = t_ // CHUNK
    sp = _dir_specs(nch)
    rsp = [sp[1], sp[0]]

    def body(*refs):
        ins, outs, ds_ref = refs[:16], refs[16:28], refs[28]

        @pl.when(pl.program_id(0) == 0)
        def _():
            ds_ref[...] = jnp.zeros_like(ds_ref)

        ss = tuple(ins[d * 8 + 6][0, h] for d, h in _PROBLEMS)
        _, vjp = jax.vjp(_dn_step, *_step_inputs(ins, 8), ss)
        dos = tuple(ins[d * 8 + 7][:, _hs(h)] for d, h in _PROBLEMS)
        grads = vjp((dos, tuple(ds_ref[p] for p in range(len(_PROBLEMS)))))
        for p, (d, h) in enumerate(_PROBLEMS):
            du_ref, dw_ref, dqk_ref, dqd_ref, dkd_ref, dg_ref = outs[d * 6:(d + 1) * 6]
            du_ref[:, _hs(h)], dw_ref[:, _hs(h)], dqk_ref[0, h] = grads[0][p], grads[1][p], grads[2][p]
            dqd_ref[:, _hs(h)], dkd_ref[:, _hs(h)] = grads[3][p], grads[4][p]
            dg_ref[0, 0, h:h + 1, :] = grads[5][p]
            ds_ref[p] = grads[6][p]

    in_specs, args, out_specs, out_shape = [], [], [], []
    big_sh = jax.ShapeDtypeStruct((t_, D), F32)
    qk_sh = jax.ShapeDtypeStruct((nch, DN_HEADS, CHUNK, CHUNK), F32)
    row_sh = jax.ShapeDtypeStruct((1, nch, DN_HEADS, CHUNK), F32)
    for d in range(2):
        s = rsp[d]
        row0 = pl.BlockSpec((1, 1, DN_HEADS, CHUNK), lambda m, d=d: (0, (nch - 1 - m) if d == 0 else m, 0, 0))
        rowd = pl.BlockSpec((1, 1, DN_HEADS, CHUNK), lambda m, d=d: (d, (nch - 1 - m) if d == 0 else m, 0, 0))
        in_specs += [s["big"], s["big"], s["qk"], s["big"], s["big"], rowd, s["st"], s["big"]]
        args += list(ares[d * _A_OUT:(d + 1) * _A_OUT]) + [grows, (st_f, st_b)[d], do]
        out_specs += [s["big"], s["big"], s["qk"], s["big"], s["big"], row0]
        out_shape += [big_sh, big_sh, qk_sh, big_sh, big_sh, row_sh]
    res = pl.pallas_call(
        body, name="dn_b_bwd", grid=(nch,), in_specs=in_specs, out_specs=out_specs, out_shape=out_shape,
        scratch_shapes=[pltpu.VMEM((2 * DN_HEADS, DN_DIM, DN_DIM), F32)],
        compiler_params=_params(1))(*args)
    dares = list(res[0:5]) + list(res[6:11])
    return dares, jnp.concatenate([res[5], res[11]], axis=0)


@jax.custom_vjp
def _rot_half(x):
    half, width = SW_DIM // 2, x.shape[1]
    first = lax.broadcasted_iota(jnp.int32, x.shape, 1) % SW_DIM < half
    return jnp.where(first, -pltpu.roll(x, width - half, axis=1), pltpu.roll(x, half, axis=1))


_rot_half.defvjp(lambda x: (_rot_half(x), None), lambda _, g: (-_rot_half(g),))


def _rope(x, c, s):
    reps = x.shape[1] // c.shape[1]
    return x * jnp.tile(c, (1, reps)) + _rot_half(x) * jnp.tile(s, (1, reps))


def _rope_t(g, c, s):
    reps = g.shape[1] // c.shape[1]
    return g * jnp.tile(c, (1, reps)) - _rot_half(g * jnp.tile(s, (1, reps)))


_SW_SCALE = SW_DIM ** -0.5
_KV_HEADS = [[kvh * SW_GRP + g for g in range(SW_GRP)] for kvh in range(SW_KV)]


def _by_group(x):
    return [jnp.concatenate([x[:, h * SW_DIM:(h + 1) * SW_DIM] for h in hs], axis=0) for hs in _KV_HEADS]


def _from_groups(xs):
    return jnp.concatenate([x[g * SW_BLOCK:(g + 1) * SW_BLOCK] for x in xs for g in range(SW_GRP)], axis=-1)


def _attn_probs(blk, t_, cq, sq, ck, sk, q, kall, sinks):
    qgs = _by_group(_rope(q, cq, sq))
    kr = _rope(kall, ck, sk)
    khs = [kr[:, kvh * SW_DIM:(kvh + 1) * SW_DIM] for kvh in range(SW_KV)]
    nq, nk = SW_GRP * SW_BLOCK, 3 * SW_BLOCK
    qpos = lax.broadcasted_iota(jnp.int32, (nq, nk), 0) % SW_BLOCK
    krel = lax.broadcasted_iota(jnp.int32, (nq, nk), 1) - SW_BLOCK
    kglob = krel + blk * SW_BLOCK
    valid = (jnp.abs(qpos - krel) <= SW_BLOCK) & (kglob >= 0) & (kglob < t_)
    ss = [jnp.where(valid, _nt(qg * _SW_SCALE, kh), -1e30) for qg, kh in zip(qgs, khs)]
    snks = [jnp.concatenate([jnp.broadcast_to(sinks[:, h:h + 1], (SW_BLOCK, 1)) for h in hs], axis=0) for hs in _KV_HEADS]
    ms = [jnp.maximum(jnp.max(s, axis=-1, keepdims=True), snk) for s, snk in zip(ss, snks)]
    es = [jnp.exp(s - m) for s, m in zip(ss, ms)]
    esnks = [jnp.exp(snk - m) for snk, m in zip(snks, ms)]
    invs = [1.0 / (jnp.sum(e, axis=-1, keepdims=True) + esnk) for e, esnk in zip(es, esnks)]
    ps = [e * inv for e, inv in zip(es, invs)]
    return qgs, khs, ps, [esnk * inv for esnk, inv in zip(esnks, invs)]


def _attn_f(blk, t_, cq, sq, ck, sk, q, kp, ko, kn, vp, vo, vn, sinks):
    _, _, ps, psinks = _attn_probs(blk, t_, cq, sq, ck, sk, q, jnp.concatenate([kp, ko, kn], axis=0), sinks)
    vall = jnp.concatenate([vp, vo, vn], axis=0)
    ps = [p.astype(_MXU) for p in ps]
    kept = [jnp.concatenate([p, jnp.broadcast_to(psink.astype(_MXU), (p.shape[0], 128))], axis=-1)
            for p, psink in zip(ps, psinks)]
    return _from_groups([_nn(p, vall[:, kvh * SW_DIM:(kvh + 1) * SW_DIM]) for kvh, p in enumerate(ps)]), kept


def _attn_bwd_f(cq, sq, ck, sk, q, kp, ko, kn, vp, vo, vn, do, kept):
    nk = 3 * SW_BLOCK
    ps = [x[:, :nk] for x in kept]
    psinks = [x[:, nk:nk + 1].astype(F32) for x in kept]
    qgs = _by_group(_rope(q, cq, sq))
    kr = _rope(jnp.concatenate([kp, ko, kn], axis=0), ck, sk)
    khs = [kr[:, kvh * SW_DIM:(kvh + 1) * SW_DIM] for kvh in range(SW_KV)]
    vall = jnp.concatenate([vp, vo, vn], axis=0)
    vhs = [vall[:, kvh * SW_DIM:(kvh + 1) * SW_DIM] for kvh in range(SW_KV)]
    dogs = _by_group(do)
    dvs = [_tn(p, dog) for p, dog in zip(ps, dogs)]
    ps = [p.astype(F32) for p in ps]
    dps = [_nt(dog * _SW_SCALE, vh) for dog, vh in zip(dogs, vhs)]
    deltas = [jnp.sum(p * dp, axis=-1, keepdims=True) for p, dp in zip(ps, dps)]
    dss = [p * (dp - delta) for p, dp, delta in zip(ps, dps, deltas)]
    dqr = _from_groups([_nn(ds, kh) for ds, kh in zip(dss, khs)])
    dkr = jnp.concatenate([_tn(ds, qg) for ds, qg in zip(dss, qgs)], axis=-1)
    dsnk = [-(psink * delta) * (1.0 / _SW_SCALE) for psink, delta in zip(psinks, deltas)]
    dsinks = jnp.concatenate([jnp.sum(d[g * SW_BLOCK:(g + 1) * SW_BLOCK], axis=0, keepdims=True)
                              for d in dsnk for g in range(SW_GRP)], axis=1)
    dq, dk, dv = _rope_t(dqr, cq, sq), _rope_t(dkr, ck, sk), jnp.concatenate(dvs, axis=-1)
    blocks = lambda a: [a[j * SW_BLOCK:(j + 1) * SW_BLOCK] for j in range(3)]
    return [dq] + blocks(dk) + blocks(dv) + [dsinks]


def _attn_specs(nb):
    prv = lambda i: jnp.maximum(i - 1, 0)
    nxt = lambda i: jnp.minimum(i + 1, nb - 1)
    rows = [lambda i: i, prv, lambda i: i, nxt]
    tab = [pl.BlockSpec((SW_BLOCK, 128), lambda i, r=r: (r(i), 0)) for r in rows]
    qs = pl.BlockSpec((SW_BLOCK, SW_HEADS * SW_DIM), lambda i: (i, O_QS // (SW_HEADS * SW_DIM)))
    kw = SW_KV * SW_DIM
    ks = [pl.BlockSpec((SW_BLOCK, kw), lambda i, r=r: (r(i), O_KS // kw)) for r in rows[1:]]
    vs = [pl.BlockSpec((SW_BLOCK, kw), lambda i, r=r: (r(i), O_VS // kw)) for r in rows[1:]]
    return tab, qs, ks, vs


def _attn_tables(refs):
    cq, cp, co, cn, sq, sp_, so, sn = [r[...] for r in refs]
    return cq, sq, jnp.concatenate([cp, co, cn], axis=0), jnp.concatenate([sp_, so, sn], axis=0)


_P_BLOCK = (None, SW_KV, SW_GRP * SW_BLOCK, 3 * SW_BLOCK + 128)


def _attn_fwd(pm, cos, sin, sinks):
    t_ = pm.shape[0]
    nb = t_ // SW_BLOCK
    tab, qs, ks, vs = _attn_specs(nb)

    def body(*refs):
        tabs = _attn_tables(refs[:8])
        vals = [r[...] for r in refs[8:16]]
        out, ps = _attn_f(pl.program_id(0), t_, *tabs, *vals)
        refs[16][...] = out.astype(refs[16].dtype)
        for kvh, p in enumerate(ps):
            refs[17][kvh] = p

    return pl.pallas_call(
        body, name="attn_fwd", grid=(nb,), in_specs=tab + tab + [qs] + ks + vs + [_full(sinks)],
        out_specs=[pl.BlockSpec((SW_BLOCK, D), lambda i: (i, 0)), pl.BlockSpec(_P_BLOCK, lambda i: (i, 0, 0, 0))],
        out_shape=[jax.ShapeDtypeStruct((t_, D), _MXU), jax.ShapeDtypeStruct((nb,) + _P_BLOCK[1:], _MXU)],
        compiler_params=_params(1))(*([cos] * 4), *([sin] * 4), pm, pm, pm, pm, pm, pm, pm, sinks)


def _attn_bwd(pm, cos, sin, sinks, probs, do, dpm):
    t_ = pm.shape[0]
    nb = t_ // SW_BLOCK
    tab, qs, ks, vs = _attn_specs(nb)
    kw = SW_KV * SW_DIM

    def body(*refs):
        tabs = _attn_tables(refs[:8])
        vals = [r[...] for r in refs[8:15]]
        p_ref, do_ref, outs = refs[16], refs[17], refs[19:]
        grads = _attn_bwd_f(*tabs, *vals, do_ref[...], [p_ref[kvh] for kvh in range(SW_KV)])
        for o_ref, g in zip(outs[:7], grads[:7]):
            o_ref[...] = g.astype(o_ref.dtype)

        @pl.when(pl.program_id(0) == 0)
        def _():
            outs[7][...] = jnp.zeros_like(outs[7])
        outs[7][...] += grads[7]

    own = lambda w: pl.BlockSpec((SW_BLOCK, w), lambda i: (i, 0))
    return pl.pallas_call(
        body, name="attn_bwd", grid=(nb,),
        in_specs=tab + tab + [qs] + ks + vs + [_full(sinks), pl.BlockSpec(_P_BLOCK, lambda i: (i, 0, 0, 0)), own(D), ANY],
        out_specs=[pl.BlockSpec((SW_BLOCK, D), lambda i: (i, O_QS // D))] + [own(kw)] * 6 + [_full(sinks)],
        out_shape=[jax.ShapeDtypeStruct(dpm.shape, dpm.dtype)] + [jax.ShapeDtypeStruct((t_, kw), F32)] * 6
        + [jax.ShapeDtypeStruct(sinks.shape, F32)],
        input_output_aliases={18: 0},
        compiler_params=_params(1))(*([cos] * 4), *([sin] * 4), pm, pm, pm, pm, pm, pm, pm, sinks, probs, do, dpm)


def _band_sum(kparts, vparts, dpm):
    t_, kw = kparts[1].shape
    nb = t_ // SW_BLOCK

    def body(kp, ko, kn, vp, vo, vn, _, out_ref):
        j = pl.program_id(0)
        band = lambda p, o, n: o[...] + jnp.where(j + 1 < nb, p[...], 0.0) + jnp.where(j > 0, n[...], 0.0)
        out_ref[...] = jnp.concatenate([band(kp, ko, kn), band(vp, vo, vn)], axis=-1).astype(out_ref.dtype)

    specs = [pl.BlockSpec((SW_BLOCK, kw), lambda j: (jnp.minimum(j + 1, nb - 1), 0)),
             pl.BlockSpec((SW_BLOCK, kw), lambda j: (j, 0)),
             pl.BlockSpec((SW_BLOCK, kw), lambda j: (jnp.maximum(j - 1, 0), 0))]
    return pl.pallas_call(
        body, name="band_sum", grid=(nb,), in_specs=specs * 2 + [ANY],
        out_specs=pl.BlockSpec((SW_BLOCK, 2 * kw), lambda j: (j, O_KS // (2 * kw))),
        out_shape=jax.ShapeDtypeStruct(dpm.shape, dpm.dtype), input_output_aliases={6: 0},
        compiler_params=_params(1))(*kparts, *vparts, dpm)


def _loss_head(y, target, tb=256):
    t_ = y.shape[0]

    def body(y_ref, t_ref, dy_ref, acc_ref):
        err = y_ref[...] - t_ref[...]
        dy_ref[...] = err * (1.0 / D)
        sq = (err * err).reshape(tb // 8, 8, D).sum(axis=0)
        part = sq[:, 0:128]
        for c in range(1, D // 128):
            part = part + sq[:, c * 128:(c + 1) * 128]

        @pl.when(pl.program_id(0) == 0)
        def _():
            acc_ref[...] = jnp.zeros_like(acc_ref)
        acc_ref[...] += part

    row = pl.BlockSpec((tb, D), lambda i: (i, 0))
    return pl.pallas_call(
        body, name="loss_head", grid=(t_ // tb,), in_specs=[row, row],
        out_specs=[row, pl.BlockSpec((8, 128), lambda i: (0, 0))],
        out_shape=[jax.ShapeDtypeStruct((t_, D), F32), jax.ShapeDtypeStruct((8, 128), F32)],
        compiler_params=_params(1))(y, target)


def _adam_math(w, g, m, v):
    bc1 = 1.0 - ADAM_B1 ** ADAM_STEP
    bc2 = 1.0 - ADAM_B2 ** ADAM_STEP
    nm = ADAM_B1 * m + (1.0 - ADAM_B1) * g
    nv = ADAM_B2 * v + (1.0 - ADAM_B2) * (g * g)
    return -ADAM_LR * ((nm / bc1) / (jnp.sqrt(nv / bc2) + ADAM_EPS) + ADAM_WD * w), nm, nv


def _adamw_layer(name, l, w, g, m, v, carry, after):
    nl, r, c = w.shape
    g2 = g.reshape(r, -1)
    gc = g2.shape[1]
    w2, m2, v2 = [a.reshape(nl * r, c) for a in (w, m, v)]
    tb = r
    while tb * gc * 4 > (1 << 20) and tb % 16 == 0:
        tb //= 2
    nb = r // tb

    def body(w_ref, g_ref, m_ref, v_ref, *rest):
        go_ref, d_ref, nm_ref, nv_ref = rest[-4:]
        gv = g_ref[...][:, :c]
        go_ref[...] = gv
        d_ref[...], nm_ref[...], nv_ref[...] = _adam_math(w_ref[...], gv, m_ref[...], v_ref[...])

    spec = pl.BlockSpec((tb, c), lambda i: (l * nb + i, 0))
    carried = list(carry) if carry is not None else []
    outs = pl.pallas_call(
        body, name=name, grid=(nb,),
        in_specs=[spec, pl.BlockSpec((tb, gc), lambda i: (i, 0)), spec, spec] + [ANY] * (len(carried) + 1),
        out_specs=[spec] * 4, out_shape=[jax.ShapeDtypeStruct((nl * r, c), F32)] * 4,
        input_output_aliases={4 + k: k for k in range(len(carried))},
        compiler_params=_params(1))(w2, g2, m2, v2, *[a.reshape(nl * r, c) for a in carried], after)
    return tuple(o.reshape(w.shape) for o in outs)


def _adamw(name, w, g, m, v):
    shape = w.shape
    cols = shape[-1]
    rows = w.size // cols
    w2, g2, m2, v2 = [a.reshape(rows, cols) for a in (w, g, m, v)]
    tb = rows
    while tb * cols * 4 > (1 << 20) and tb % 16 == 0:
        tb //= 2

    def body(w_ref, g_ref, m_ref, v_ref, d_ref, nm_ref, nv_ref):
        d_ref[...], nm_ref[...], nv_ref[...] = _adam_math(w_ref[...], g_ref[...], m_ref[...], v_ref[...])

    spec = pl.BlockSpec((tb, cols), lambda i: (i, 0))
    sh = jax.ShapeDtypeStruct((rows, cols), F32)
    outs = pl.pallas_call(body, name=name, grid=(rows // tb,), in_specs=[spec] * 4, out_specs=[spec] * 3,
                          out_shape=[sh] * 3, compiler_params=_params(1))(w2, g2, m2, v2)
    return [o.reshape(shape) for o in outs]


def _to_rows(bg, col0):
    t_ = bg.shape[0]
    a = bg[:, col0:col0 + 2 * DN_HEADS].reshape(t_ // CHUNK, CHUNK, 2, DN_HEADS)
    return jnp.transpose(a, (2, 0, 3, 1))


def _from_rows(db, dg):
    nch = db.shape[1]
    back = lambda a: jnp.transpose(a, (1, 3, 0, 2)).reshape(nch * CHUNK, 2 * DN_HEADS)
    return jnp.pad(jnp.concatenate([back(db), back(dg)], axis=1), ((0, 0), (0, 128 - 4 * DN_HEADS)))


def _layer_fwd(x, xm, w, rest, cos, sin):
    t_ = x.shape[0]
    nb = min(1024, t_)
    pm = _mm(xm, w["in_main"], name="mm_in", tn=1536)
    pbg = _mm(xm, w["in_bg"], name="mm_in_bg")
    w = {**w, **rest(pbg)}
    qkv = _prep_fwd(pm, w["conv"])
    bg, = _stage_fwd(_bg_f, "bg_fwd", [(pbg, 128, 0, False)], [w["arow"], w["dtrow"]], [(128, F32)], nb)
    brows, grows = _to_rows(bg, 0), _to_rows(bg, 2 * DN_HEADS)
    ares, tinv = _dn_a_fwd(qkv, grows, brows)
    o_f, o_b, st_f, st_b = _dn_b_fwd(ares, grows)

    def gnorm_all(of, ob, z, gnw):
        return jnp.concatenate([_gnorm_f(of[:, _hs(h)], ob[:, _hs(h)], z[:, _hs(h)], gnw)[0]
                                for h in range(DN_HEADS)], axis=-1)

    odn, ya = _mm_pre(gnorm_all, [(o_f, D, 0), (o_b, D, 0), (pm, D, O_Z // D)], [w["gnw"]], w["a"], name="mm_a_gnorm")
    osw, probs = _attn_fwd(pm, cos, sin, w["sinks"])
    same, first = (lambda j: j), (lambda j: 0)
    full = lambda dt: (D, D, dt, first, None)
    yb, merged = _mm_fused(
        osw, w["b"], lambda acc, ya_, ga, gb: (acc,) + _merge_f(ya_, acc, ga, gb), [full(F32), full(_MXU)],
        name="mm_b_merge", rows=[(ya, D, first), (pm, D, lambda j: O_GA // D), (pm, D, lambda j: O_GB // D)], tm=512)
    mix, x1, x1m = _mm_fused(
        merged, w["o"], lambda acc, x_, g_, b_: (acc,) + _ln_f2(x_, acc, g_, b_), [full(F32), full(F32), full(_MXU)],
        name="mm_o_ln", rows=[(x, D, first)], params=[w["ln1g"], w["ln1b"]], tm=512)
    gu, hid = _mm_fused(x1m, w["gu"], lambda acc: (acc,) + _swiglu_tile(acc),
                        [(2 * FFN, FFN, F32, same, None), (FFN, FFN // 2, _MXU, same, None)], name="mm_gu_swiglu", tn=FFN)
    ffn, x2, x2m = _mm_fused(
        hid, w["d"], lambda acc, x_, g_, b_: (acc,) + _ln_f2(x_, acc, g_, b_), [full(F32), full(F32), full(_MXU)],
        name="mm_d_ln", rows=[(x1, D, first)], params=[w["ln2g"], w["ln2b"]], tm=512, tk=FFN)
    res = dict(w=w, x=x, xm=xm, pm=pm, pbg=pbg, qkv=qkv, grows=grows, brows=brows, ares=ares, tinv=tinv, o_f=o_f, o_b=o_b, st_f=st_f,
               st_b=st_b, odn=odn, osw=osw, probs=probs, ya=ya, yb=yb, merged=merged, mix=mix, x1=x1, x1m=x1m, gu=gu, hid=hid,
               ffn=ffn)
    return x2, x2m, res


def _layer_bwd(dx2, r, w, cos, sin, mid=None, after=None):
    t_ = dx2.shape[0]
    nb = min(1024, t_)
    pm = r["pm"]
    g = {}
    dx1a, dffn, g["ln2g"], g["ln2b"] = _stage_bwd(
        _ln_f, "ln2_bwd", [(r["x1"], D, 0, False), (r["ffn"], D, 0, False)], [w["ln2g"], w["ln2b"]], [dx2],
        min(512, t_), dtypes=[F32, _MXU], after=after)
    same = lambda j: j
    dgu, = _mm_fused(dffn, w["d"], lambda dhid, gu: jax.vjp(_swiglu_tile, gu)[1]((dhid,)),
                     [(2 * FFN, FFN, _MXU, same, None)], name="mm_d_dx_swiglu", rows=[(r["gu"], FFN, same)], tb=True,
                     tn=FFN // 2)
    g["d"] = _mm(r["hid"], dffn, ta=True, name="mm_d_dw", tm=FFN // 2, out_dtype=_MXU)
    first = lambda j: 0
    full = lambda dt: (D, D, dt, first, None)

    def ln1_back(acc, dx1a_, x_, mix_, g_, b_):
        dx_, dmix_, dg_, db_ = jax.vjp(_ln_f, x_, mix_, g_, b_)[1]((acc + dx1a_,))
        return dx_, dmix_, dg_, db_

    dxa, dmix, g["ln1g"], g["ln1b"] = _mm_fused(
        dgu, w["gu"], ln1_back, [full(F32), full(_MXU)], name="mm_gu_dx_ln", tb=True, tm=256, tk=2 * FFN, n_sums=2,
        rows=[(dx1a, D, first), (r["x"], D, first), (r["mix"], D, first)], params=[w["ln1g"], w["ln1b"]])
    g["gu"] = _mm(r["x1m"], dgu, ta=True, name="mm_gu_dw", tn=FFN // 2, out_dtype=_MXU)
    tok = mid(g["gu"]) if mid is not None else None
    g["o"] = _mm(r["merged"], dmix, ta=True, name="mm_o_dw", after=tok, out_dtype=_MXU)

    def merge_back(acc, ya_, yb_, ga, gb):
        dya_, dyb_, dga, dgb = jax.vjp(_merge_f, ya_, yb_, ga, gb)[1]((acc,))
        return dya_, dyb_, jnp.concatenate([dga, dgb], axis=-1)

    dya, dyb, dpm = _mm_fused(
        dmix, w["o"], merge_back, [full(_MXU), full(_MXU), (N_MAIN, 2 * D, _MXU, lambda j: O_GA // (2 * D), None)],
        name="mm_o_dx_merge", tb=True, tm=512,
        rows=[(r["ya"], D, first), (r["yb"], D, first), (pm, D, lambda j: O_GA // D), (pm, D, lambda j: O_GB // D)])

    def gnorm_back(acc, of, ob, z, gnw):
        dos, dzs, dws = [], [], None
        for h in range(DN_HEADS):
            do_h, _, dz_h, dw_h = jax.vjp(_gnorm_f, of[:, _hs(h)], ob[:, _hs(h)], z[:, _hs(h)], gnw)[1]((acc[:, _hs(h)],))
            dos.append(do_h)
            dzs.append(dz_h)
            dws = dw_h if dws is None else dws + dw_h
        return jnp.concatenate(dos, axis=-1), jnp.concatenate(dzs, axis=-1), dws

    dof, dpm, g["gnw"] = _mm_fused(
        dya, w["a"], gnorm_back, [full(F32), (N_MAIN, D, _MXU, lambda j: O_Z // D, dpm)], name="mm_a_dx_gnorm", tb=True,
        tm=512, rows=[(r["o_f"], D, first), (r["o_b"], D, first), (pm, D, lambda j: O_Z // D)], params=[w["gnw"]],
        n_sums=1)
    g["a"] = _mm(r["odn"], dya, ta=True, name="mm_a_dw", out_dtype=_MXU)
    dosw = _mm(dyb, w["b"], tb=True, name="mm_b_dx")
    g["b"] = _mm(r["osw"], dyb, ta=True, name="mm_b_dw", out_dtype=_MXU)
    ab = _attn_bwd(pm, cos, sin, w["sinks"], r["probs"], dosw, dpm)
    dpm, g["sinks"] = ab[0], ab[7]
    dpm = _band_sum(ab[1:4], ab[4:7], dpm)
    dares, dg_b = _dn_b_bwd(r["ares"], r["grows"], r["st_f"], r["st_b"], dof)
    dqkv, dgrows, dbrows = _dn_a_bwd(r["qkv"], r["grows"], r["brows"], r["tinv"], dares, dg_b)
    dbg = _from_rows(dbrows, dgrows)
    dpbg, g["arow"], g["dtrow"] = _stage_bwd(_bg_f, "bg_bwd", [(r["pbg"], 128, 0, False)], [w["arow"], w["dtrow"]],
                                             [dbg], nb)
    dpm, g["conv"] = _prep_bwd(pm, w["conv"], dqkv, dpm)
    dx = _mm(dpm, w["in_main"], tb=True, add=dxa, name="mm_in_dx", tk=1920)
    dx = _mm(dpbg, w["in_bg"], tb=True, add=dx, name="mm_in_bg_dx")
    g["in_main"] = _mm(r["xm"], dpm, ta=True, name="mm_in_dw", tn=1536, out_dtype=_MXU)
    g["in_bg"] = _mm(r["xm"], dpbg, ta=True, name="mm_in_bg_dw", out_dtype=_MXU)
    return dx, g


def _place():
    return lax.axis_index("x"), lax.axis_index("y"), lax.axis_index("c")


def _colblock(kind, q):
    return q if kind == "col" else (q >> 1) | ((q & 1) << 1)


def _other_chips(x, y):
    return [(1 - x, y), (x, 1 - y), (1 - x, 1 - y)]


HBM = pl.BlockSpec(memory_space=pltpu.HBM)
SEM = pl.BlockSpec(memory_space=pltpu.SEMAPHORE)
DATAFLOW = pltpu.SideEffectType.DATAFLOW_SIDE_EFFECTING


def _hbm(a):
    return pltpu.HBM(a.shape, a.dtype)


def _gather_start(locs, kinds, name):
    n = len(locs)
    locs = list(locs)
    lands = [lax.empty((N_CHIPS,) + a.shape if kind == "row" else (a.shape[0], N_CHIPS * a.shape[1]), a.dtype)
             for a, kind in zip(locs, kinds)]

    def body(*refs):
        loc_refs, land_refs = refs[:n], refs[n:2 * n]
        send_sems, recv_sems, token = refs[2 * n:3 * n], refs[3 * n:4 * n], refs[-1]
        x, y, c = _place()
        myq = 2 * x + y
        for t in range(n):
            width = locs[t].shape[1]
            mine = (land_refs[t].at[myq] if kinds[t] == "row" else
                    land_refs[t].at[:, pl.ds(pl.multiple_of(_colblock(kinds[t], myq) * width, 128), width)])
            for dev in [(cx, cy, c) for cx, cy in _other_chips(x, y)] + [(x, y, 1 - c)]:
                pltpu.make_async_remote_copy(src_ref=loc_refs[t], dst_ref=mine, send_sem=send_sems[t],
                                             recv_sem=recv_sems[t], device_id=dev, device_id_type=MESH).start()
        token[...] = jnp.zeros_like(token)

    res = pl.pallas_call(
        body, name=name,
        out_shape=[pltpu.SemaphoreType.DMA(())] * (2 * n) + [_hbm(a) for a in locs + lands]
        + [jax.ShapeDtypeStruct((8, 128), F32)],
        in_specs=[HBM] * (2 * n), out_specs=[SEM] * (2 * n) + [HBM] * (2 * n) + [pl.BlockSpec(memory_space=pltpu.VMEM)],
        input_output_aliases={t: 2 * n + t for t in range(2 * n)},
        compiler_params=pltpu.CompilerParams(has_side_effects=DATAFLOW),
    )(*[pltpu.with_memory_space_constraint(a, pltpu.HBM) for a in locs + lands])
    return (res[:n], res[n:2 * n], res[2 * n:3 * n], res[3 * n:4 * n]), res[-1]


def _split_wait(handle, after, name):
    send_sems, recv_sems, srcs, lands = handle
    n = len(srcs)

    def body(*refs):
        land_refs, ssems, rsems = refs[n:2 * n], refs[2 * n:3 * n], refs[3 * n:4 * n]
        x, y, c = _place()
        for t in range(n):
            done = pltpu.make_async_remote_copy(
                src_ref=land_refs[t], dst_ref=land_refs[t], send_sem=ssems[t], recv_sem=rsems[t],
                device_id=(x, y, c), device_id_type=MESH)
            done.wait_send()
            done.wait_recv()

    res = pl.pallas_call(
        body, name=name, out_shape=[_hbm(a) for a in list(srcs) + list(lands)],
        in_specs=[HBM] * (2 * n) + [SEM] * (2 * n) + [ANY], out_specs=[HBM] * (2 * n),
        input_output_aliases={t: t for t in range(2 * n)},
        compiler_params=pltpu.CompilerParams(has_side_effects=DATAFLOW),
    )(*srcs, *lands, *send_sems, *recv_sems, after)
    return res[:n], res[n:]


RS_CHUNKS = 2


def _piece(ref, kind, q, hf, pr, pc):
    if kind == "row":
        return ref.at[pl.ds((2 * q + hf) * pr, pr), :]
    return ref.at[pl.ds(hf * pr, pr), pl.ds(pl.multiple_of(_colblock(kind, q) * pc, 128), pc)]


def _rs_sibling_start(ts, meta, name):
    n = len(ts)
    ts = list(ts)
    lands = [lax.empty((N_CHIPS, pr, pc), a.dtype) for a, (_, pr, pc) in zip(ts, meta)]

    def body(*refs):
        t_refs, land_refs = refs[:n], refs[n:2 * n]
        send_sems, recv_sems, token = refs[2 * n:3 * n], refs[3 * n:4 * n], refs[-1]
        x, y, c = _place()
        for t, (kind, pr, pc) in enumerate(meta):
            for q in range(N_CHIPS):
                pltpu.make_async_remote_copy(
                    src_ref=_piece(t_refs[t], kind, q, 1 - c, pr, pc), dst_ref=land_refs[t].at[q],
                    send_sem=send_sems[t], recv_sem=recv_sems[t], device_id=(x, y, 1 - c), device_id_type=MESH).start()
        token[...] = jnp.zeros_like(token)

    res = pl.pallas_call(
        body, name=name,
        out_shape=[pltpu.SemaphoreType.DMA(())] * (2 * n) + [_hbm(a) for a in ts + lands]
        + [jax.ShapeDtypeStruct((8, 128), F32)],
        in_specs=[HBM] * (2 * n), out_specs=[SEM] * (2 * n) + [HBM] * (2 * n) + [pl.BlockSpec(memory_space=pltpu.VMEM)],
        input_output_aliases={t: 2 * n + t for t in range(2 * n)},
        compiler_params=pltpu.CompilerParams(has_side_effects=DATAFLOW),
    )(*[pltpu.with_memory_space_constraint(a, pltpu.HBM) for a in ts + lands])
    return (res[:n], res[n:2 * n], res[2 * n:3 * n], res[3 * n:4 * n]), res[-1]


def _rs_add_sibling(ts, r1s, meta, c):
    n = len(ts)
    in_specs, out_specs, out_shape = [], [], []
    for kind, pr, pc in meta:
        rs = pr // RS_CHUNKS
        if kind == "row":
            in_specs.append(pl.BlockSpec((rs, pc), lambda q, r, c_ref: ((2 * q + c_ref[0]) * RS_CHUNKS + r, 0)))
        else:
            in_specs.append(pl.BlockSpec(
                (rs, pc), lambda q, r, c_ref, kind=kind: (c_ref[0] * RS_CHUNKS + r, _colblock(kind, q))))
    for kind, pr, pc in meta:
        sp = pl.BlockSpec((None, pr // RS_CHUNKS, pc), lambda q, r, c_ref: (q, r, 0))
        in_specs.append(sp)
        out_specs.append(sp)
        out_shape.append(jax.ShapeDtypeStruct((N_CHIPS, pr, pc), BF16))

    def body(c_ref, *refs):
        for t in range(n):
            refs[2 * n + t][...] = (refs[t][...].astype(F32) + refs[n + t][...].astype(F32)).astype(BF16)

    return pl.pallas_call(
        body, name="rs_add_sibling", out_shape=out_shape,
        grid_spec=pltpu.PrefetchScalarGridSpec(num_scalar_prefetch=1, grid=(N_CHIPS, RS_CHUNKS), in_specs=in_specs,
                                               out_specs=out_specs),
        compiler_params=_params(2))(c.reshape(1).astype(jnp.int32), *ts, *r1s)


def _rs_chips_start(ps, meta, name):
    n = len(ps)
    ps = list(ps)
    lands = [lax.empty((N_CHIPS - 1, pr, pc), p.dtype) for p, (_, pr, pc) in zip(ps, meta)]

    def body(*refs):
        p_refs, land_refs = refs[:n], refs[n:2 * n]
        send_sems, recv_sems, token = refs[2 * n:3 * n], refs[3 * n:4 * n], refs[-1]
        x, y, c = _place()
        for t in range(n):
            for j, (cx, cy) in enumerate(_other_chips(x, y)):
                pltpu.make_async_remote_copy(
                    src_ref=p_refs[t].at[2 * cx + cy], dst_ref=land_refs[t].at[j], send_sem=send_sems[t],
                    recv_sem=recv_sems[t], device_id=(cx, cy, c), device_id_type=MESH).start()
        token[...] = jnp.zeros_like(token)

    res = pl.pallas_call(
        body, name=name,
        out_shape=[pltpu.SemaphoreType.DMA(())] * (2 * n) + [_hbm(a) for a in ps + lands]
        + [jax.ShapeDtypeStruct((8, 128), F32)],
        in_specs=[HBM] * (2 * n), out_specs=[SEM] * (2 * n) + [HBM] * (2 * n) + [pl.BlockSpec(memory_space=pltpu.VMEM)],
        input_output_aliases={t: 2 * n + t for t in range(2 * n)},
        compiler_params=pltpu.CompilerParams(has_side_effects=DATAFLOW),
    )(*[pltpu.with_memory_space_constraint(a, pltpu.HBM) for a in ps + lands])
    return (res[:n], res[n:2 * n], res[2 * n:3 * n], res[3 * n:4 * n]), res[-1]


def _rs_add_chips(ps, r2s, meta, myq, c):
    n = len(ps)
    in_specs, out_specs, out_shape = [], [], []
    for _, pr, pc in meta:
        in_specs.append(pl.BlockSpec((None, pr // RS_CHUNKS, pc), lambda r, q_ref, c_ref: (q_ref[0], r, 0)))
    for _, pr, pc in meta:
        in_specs.append(pl.BlockSpec((N_CHIPS - 1, pr // RS_CHUNKS, pc), lambda r, q_ref, c_ref: (0, r, 0)))
        out_specs.append(pl.BlockSpec((None, pr // RS_CHUNKS, pc), lambda r, q_ref, c_ref: (c_ref[0], r, 0)))
        out_shape.append(jax.ShapeDtypeStruct((2, pr, pc), F32))

    def body(q_ref, c_ref, *refs):
        for t in range(n):
            r2 = refs[n + t]
            own = refs[t][...].astype(F32)
            refs[2 * n + t][...] = ((own + r2[0].astype(F32)) + r2[1].astype(F32)) + r2[2].astype(F32)

    return pl.pallas_call(
        body, name="rs_add_chips", out_shape=out_shape,
        grid_spec=pltpu.PrefetchScalarGridSpec(num_scalar_prefetch=2, grid=(RS_CHUNKS,), in_specs=in_specs,
                                               out_specs=out_specs),
        compiler_params=_params(1))(myq.reshape(1).astype(jnp.int32), c.reshape(1).astype(jnp.int32), *ps, *r2s)


def _rs_share_halves(gs):
    n = len(gs)

    def body(*refs):
        g_refs, send_sems, recv_sems = refs[n:2 * n], refs[2 * n], refs[2 * n + 1]
        x, y, c = _place()
        sib = (x, y, 1 - c)
        for t in range(n):
            pltpu.make_async_remote_copy(
                src_ref=g_refs[t].at[c], dst_ref=g_refs[t].at[c], send_sem=send_sems.at[t], recv_sem=recv_sems.at[t],
                device_id=sib, device_id_type=MESH).start()
        for t in range(n):
            cp = pltpu.make_async_remote_copy(
                src_ref=g_refs[t].at[c], dst_ref=g_refs[t].at[1 - c], send_sem=send_sems.at[t],
                recv_sem=recv_sems.at[t], device_id=sib, device_id_type=MESH)
            cp.wait_send()
            cp.wait_recv()

    return pl.pallas_call(
        body, name="rs_share_halves", in_specs=[ANY] * n, out_specs=[ANY] * n,
        out_shape=[jax.ShapeDtypeStruct(g.shape, g.dtype) for g in gs], input_output_aliases={t: t for t in range(n)},
        scratch_shapes=[pltpu.SemaphoreType.DMA((n,)), pltpu.SemaphoreType.DMA((n,))],
    )(*gs)


def _rs_middle(handle, after, meta, c, name):
    ts, r1s = _split_wait(handle, after, name + "_sib_wait")
    ps = _rs_add_sibling(ts, r1s, meta, c)
    return _rs_chips_start(ps, meta, name + "_start")


def _rs_end(handle, after, meta, c, myq, name):
    ps, r2s = _split_wait(handle, after, name + "_wait")
    return _rs_share_halves(_rs_add_chips(ps, r2s, meta, myq, c))


def _allreduce_small(buf):
    rows = buf.shape[0]
    ndev = 8

    def body(b_ref, o_ref, slots, send_sems, recv_sems):
        x, y, c = _place()
        me = 4 * x + 2 * y + c
        slots[me] = b_ref[...]
        for k in range(1, ndev):
            kx, ky, kc = (k >> 2) & 1, (k >> 1) & 1, k & 1
            peer = (x ^ kx, y ^ ky, c ^ kc)
            pltpu.make_async_remote_copy(
                src_ref=b_ref, dst_ref=slots.at[me], send_sem=send_sems.at[k - 1], recv_sem=recv_sems.at[k - 1],
                device_id=peer, device_id_type=MESH).start()
        for k in range(1, ndev):
            kx, ky, kc = (k >> 2) & 1, (k >> 1) & 1, k & 1
            cp = pltpu.make_async_remote_copy(
                src_ref=b_ref, dst_ref=slots.at[me ^ k], send_sem=send_sems.at[k - 1], recv_sem=recv_sems.at[k - 1],
                device_id=(x ^ kx, y ^ ky, c ^ kc), device_id_type=MESH)
            cp.wait_send()
            cp.wait_recv()
        acc = slots[0]
        for s in range(1, ndev):
            acc = acc + slots[s]
        o_ref[...] = acc

    vm = pl.BlockSpec(memory_space=pltpu.VMEM)
    return pl.pallas_call(
        body, name="allreduce_small", in_specs=[vm], out_specs=vm, out_shape=jax.ShapeDtypeStruct((rows, 128), F32),
        scratch_shapes=[pltpu.VMEM((ndev, rows, 128), F32), pltpu.SemaphoreType.DMA((ndev - 1,)),
                        pltpu.SemaphoreType.DMA((ndev - 1,))],
        compiler_params=pltpu.CompilerParams(vmem_limit_bytes=VMEM_LIMIT))(buf)


RS_META = [("col", D // 2, IN_PAD), ("row", D // 8, D), ("row", D // 8, D), ("row", D // 8, D),
           ("colx", D // 2, 2 * FFN // N_CHIPS), ("row", FFN // 8, D)]
SMALL_ROWS = 156


def _rope_tables(t_):
    half = SW_DIM // 2
    inv_freq = ROPE_THETA ** (-jnp.arange(half, dtype=F32) / half)
    ang = jnp.arange(t_, dtype=F32)[:, None] * inv_freq[None, :]
    reps = 128 // half
    return jnp.concatenate([jnp.cos(ang)] * reps, axis=1), jnp.concatenate([jnp.sin(ang)] * reps, axis=1)


def _orig_cols(padded, a, b):
    out = []
    for q in range(N_CHIPS):
        lo, hi = max(a, q * IN_SHARD), min(b, (q + 1) * IN_SHARD)
        if lo < hi:
            out.append(padded[:, q * IN_PAD + lo - q * IN_SHARD:q * IN_PAD + hi - q * IN_SHARD])
    return out


_ORIG_SEGMENTS = [(0, R_BG, "main", 0), (R_BG, R_SW, "bg", 0), (R_SW, R_GATES, "main", O_QS), (R_GATES, IN_COLS, "main", O_GA)]


def _to_padded_shards(main, bg):
    zeros = jnp.zeros((main.shape[0], IN_PAD - IN_SHARD), main.dtype)
    parts = []
    for q in range(N_CHIPS):
        for a, b, src, s0 in _ORIG_SEGMENTS:
            lo, hi = max(a, q * IN_SHARD), min(b, (q + 1) * IN_SHARD)
            if lo < hi:
                parts.append((main if src == "main" else bg)[:, s0 + lo - a:s0 + hi - a])
        parts.append(zeros)
    return jnp.concatenate(parts, axis=1)


def _lane_row(v16):
    return jnp.pad(v16.reshape(1, 2 * DN_HEADS), ((0, 0), (2 * DN_HEADS, 128 - 4 * DN_HEADS)))


def _pack_small(g):
    pad16 = jnp.pad(g["sinks"], ((0, 0), (0, 128 - SW_HEADS)))
    return jnp.concatenate([g["conv"].reshape(-1, 128), g["ln1g"].reshape(-1, 128), g["ln1b"].reshape(-1, 128),
                            g["ln2g"].reshape(-1, 128), g["ln2b"].reshape(-1, 128), g["gnw"], g["arow"], g["dtrow"],
                            pad16], axis=0)


def _unpack_small(buf):
    nconv = DN_CONV * 3 * D // 128
    o = nconv
    out = dict(conv=buf[:o].reshape(DN_CONV, 3 * D))
    for name in ("ln1g", "ln1b", "ln2g", "ln2b"):
        out[name] = buf[o:o + 8].reshape(D)
        o += 8
    out["gnw"] = buf[o]
    out["a_log"] = buf[o + 1, 2 * DN_HEADS:4 * DN_HEADS].reshape(2, DN_HEADS)
    out["dt_bias"] = buf[o + 2, 2 * DN_HEADS:4 * DN_HEADS].reshape(2, DN_HEADS)
    out["sinks"] = buf[o + 3, :SW_HEADS]
    return out


def kernel(x, w_in, conv_w, a_log, dt_bias, dn_norm_w, sinks, w_branch_a, w_branch_b, w_out, ln1_g, ln1_b, w_gate_up, w_down, ln2_g, ln2_b, loss_target, m_w_in, m_conv_w, m_a_log, m_dt_bias, m_dn_norm_w, m_sinks, m_w_branch_a, m_w_branch_b, m_w_out, m_ln1_g, m_ln1_b, m_w_gate_up, m_w_down, m_ln2_g, m_ln2_b, v_w_in, v_conv_w, v_a_log, v_dt_bias, v_dn_norm_w, v_sinks, v_w_branch_a, v_w_branch_b, v_w_out, v_ln1_g, v_ln1_b, v_w_gate_up, v_w_down, v_ln2_g, v_ln2_b):
    xi, yi, ci = _place()
    myq = 2 * xi + yi
    t_ = x.shape[1]
    cos, sin = _rope_tables(t_)

    loc_in = jnp.pad(w_in.astype(BF16), ((0, 0), (0, 0), (0, IN_PAD - IN_SHARD)))
    locs = [loc_in, w_branch_a.astype(BF16), w_branch_b.astype(BF16), w_out.astype(BF16), w_gate_up.astype(BF16),
            w_down.astype(BF16), conv_w]
    kinds = ["col", "row", "row", "row", "colx", "row", "col"]
    gathers = []
    for l in range(DEPTH):
        srcs = [a[l] for a in locs]
        if gathers:
            srcs[-1] = srcs[-1] + gathers[-1][1][0, 0]
        gathers.append(_gather_start(srcs, kinds, "gather_%d_start" % l))

    def in_weights(l, after):
        _, (full_in,) = _split_wait(tuple(part[:1] for part in gathers[l][0]), after, "gather_%d_wait_in" % l)
        cols = lambda a, b: _orig_cols(full_in, a, b)
        return dict(
            in_main=jnp.concatenate(cols(0, R_BG) + cols(R_GATES, IN_COLS) + cols(R_SW, R_GATES), axis=1),
            in_bg=jnp.pad(jnp.concatenate(cols(R_BG, R_SW), axis=1), ((0, 0), (0, 128 - 4 * DN_HEADS))))

    def rest_weights(l, after):
        _, (full_a, full_b, full_o, full_gu, full_d, full_conv) = _split_wait(
            tuple(part[1:] for part in gathers[l][0]), after, "gather_%d_wait_rest" % l)
        return dict(
            conv=full_conv, arow=_lane_row(a_log[l]), dtrow=_lane_row(dt_bias[l]), gnw=dn_norm_w[l][None],
            sinks=sinks[l][None], a=full_a.reshape(D, D), b=full_b.reshape(D, D), o=full_o.reshape(D, D),
            ln1g=ln1_g[l][None], ln1b=ln1_b[l][None], gu=full_gu, d=full_d.reshape(FFN, D),
            ln2g=ln2_g[l][None], ln2b=ln2_b[l][None])

    h = x[0]
    hm = h.astype(_MXU)
    residuals = []
    for l in range(DEPTH):
        win = in_weights(l, gathers[-1][1] if l == 0 else h)
        h, hm, res = _layer_fwd(h, hm, win, functools.partial(rest_weights, l), cos, sin)
        residuals.append(res)
    dh, sq = _loss_head(h, loss_target[0])
    loss = lax.psum((0.5 / D) * jnp.sum(sq), ("x", "y", "c"))

    big = [None] * DEPTH
    small = [None] * DEPTH
    hop1 = hop2 = None

    def mid(after):
        nonlocal hop1, hop2
        if hop1 is None:
            return None
        handle, tok = _rs_middle(hop1[1], after, RS_META, ci, "rs_chips_%d" % hop1[0])
        hop1, hop2 = None, (hop1[0], handle)
        return tok

    token = None
    for l in reversed(range(DEPTH)):
        dh, g = _layer_bwd(dh, residuals[l], residuals[l]["w"], cos, sin, mid, token)
        if hop2 is not None:
            big[hop2[0]] = _rs_end(hop2[1], dh, RS_META, ci, myq, "rs_chips_%d" % hop2[0])
        g_in = _to_padded_shards(g["in_main"], g["in_bg"])
        handle, token = _rs_sibling_start([g_in, g["a"], g["b"], g["o"], g["gu"], g["d"]], RS_META, "rs_sib_%d_start" % l)
        hop1 = (l, handle)
        small[l] = _pack_small(g)
    tot = _allreduce_small(jnp.concatenate(small, axis=0))
    token = mid(tot)
    pending = hop2
    sm = [_unpack_small(tot[l * SMALL_ROWS:(l + 1) * SMALL_ROWS]) for l in range(DEPTH)]
    stack = lambda name: jnp.stack([s[name] for s in sm], axis=0)
    grads = dict(
        conv_w=lax.dynamic_slice_in_dim(stack("conv"), myq * (3 * D // N_CHIPS), 3 * D // N_CHIPS, axis=2),
        a_log=stack("a_log"), dt_bias=stack("dt_bias"), dn_norm_w=stack("gnw"), sinks=stack("sinks"),
        ln1_g=stack("ln1g"), ln1_b=stack("ln1b"), ln2_g=stack("ln2g"), ln2_b=stack("ln2b"))
    weights = dict(w_in=w_in, conv_w=conv_w, a_log=a_log, dt_bias=dt_bias, dn_norm_w=dn_norm_w, sinks=sinks,
                   w_branch_a=w_branch_a, w_branch_b=w_branch_b, w_out=w_out, ln1_g=ln1_g, ln1_b=ln1_b,
                   w_gate_up=w_gate_up, w_down=w_down, ln2_g=ln2_g, ln2_b=ln2_b)
    ms = dict(w_in=m_w_in, conv_w=m_conv_w, a_log=m_a_log, dt_bias=m_dt_bias, dn_norm_w=m_dn_norm_w, sinks=m_sinks,
              w_branch_a=m_w_branch_a, w_branch_b=m_w_branch_b, w_out=m_w_out, ln1_g=m_ln1_g, ln1_b=m_ln1_b,
              w_gate_up=m_w_gate_up, w_down=m_w_down, ln2_g=m_ln2_g, ln2_b=m_ln2_b)
    vs = dict(w_in=v_w_in, conv_w=v_conv_w, a_log=v_a_log, dt_bias=v_dt_bias, dn_norm_w=v_dn_norm_w, sinks=v_sinks,
              w_branch_a=v_w_branch_a, w_branch_b=v_w_branch_b, w_out=v_w_out, ln1_g=v_ln1_g, ln1_b=v_ln1_b,
              w_gate_up=v_w_gate_up, w_down=v_w_down, ln2_g=v_ln2_g, ln2_b=v_ln2_b)
    names = list(weights)
    upd = {n: _adamw("adamw_" + n, weights[n], grads[n], ms[n], vs[n]) for n in grads}
    big_names = ["w_in", "w_branch_a", "w_branch_b", "w_out", "w_gate_up", "w_down"]
    carry = {n: None for n in big_names}

    def update_layer(l):
        for t, n in enumerate(big_names):
            carry[n] = _adamw_layer("adamw_" + n, l, weights[n], big[l][t], ms[n], vs[n], carry[n], token)

    for l in range(DEPTH - 1, pending[0], -1):
        update_layer(l)
    big[pending[0]] = _rs_end(pending[1], carry[big_names[-1]][1], RS_META, ci, myq, "rs_chips_%d" % pending[0])
    update_layer(pending[0])
    for n in big_names:
        grads[n], upd[n] = carry[n][0], carry[n][1:]
    return (loss, dh[None], *[grads[n] for n in names], *[upd[n][0] for n in names], *[upd[n][1] for n in names],
            *[upd[n][2] for n in names])
```

```python
import functools

import jax
import jax.numpy as jnp
from jax import lax
from jax.experimental import pallas as pl
from jax.experimental.pallas import tpu as pltpu

F32 = jnp.float32
BF16 = jnp.bfloat16
_MXU = BF16

D = 1024
DEPTH = 4
DN_HEADS = 8
DN_DIM = 128
DN_CONV = 5
CHUNK = 64
SW_HEADS = 16
SW_KV = 4
SW_DIM = 64
SW_GRP = SW_HEADS // SW_KV
SW_BLOCK = 128
ROPE_THETA = 10000.0
FFN = 2816
ALPHA = (2.0 * DEPTH) ** 0.25
LN_EPS = 1e-5
RMS_EPS = 1e-6
IN_COLS = 7712
O_Z, O_GA, O_GB, O_QS, O_KS, O_VS, N_MAIN = 3072, 4096, 5120, 6144, 7168, 7424, 7680
R_BG, R_SW, R_GATES = 4096, 4128, 5664
N_CHIPS = 4
IN_SHARD = IN_COLS // N_CHIPS
IN_PAD = 2048
ADAM_LR, ADAM_B1, ADAM_B2, ADAM_EPS, ADAM_WD, ADAM_STEP = 0.001, 0.9, 0.999, 1e-08, 0.01, 10
VMEM_LIMIT = 52 * 1024 * 1024
MESH = pl.DeviceIdType.MESH
ANY = pl.BlockSpec(memory_space=pl.ANY)


def _params(n_grid, **kw):
    return pltpu.CompilerParams(dimension_semantics=("arbitrary",) * n_grid, vmem_limit_bytes=VMEM_LIMIT, **kw)


def _full(a):
    nd = a.ndim
    return pl.BlockSpec(a.shape, lambda *_, nd=nd: (0,) * nd)


def _raw_dot(a, b, ca, cb):
    return lax.dot_general(a.astype(_MXU), b.astype(_MXU), (((ca,), (cb,)), ((), ())), preferred_element_type=F32)


@jax.custom_vjp
def _nn(a, b):
    return _raw_dot(a, b, 1, 0)


@jax.custom_vjp
def _nt(a, b):
    return _raw_dot(a, b, 1, 1)


@jax.custom_vjp
def _tn(a, b):
    if a.shape[1] > b.shape[1]:
        return _raw_dot(b, a, 0, 0).T
    return _raw_dot(a, b, 0, 0)


_nn.defvjp(lambda a, b: (_nn(a, b), (a, b)), lambda r, g: (_nt(g, r[1]), _tn(r[0], g)))
_nt.defvjp(lambda a, b: (_nt(a, b), (a, b)), lambda r, g: (_nn(g, r[1]), _tn(g, r[0])))
_tn.defvjp(lambda a, b: (_tn(a, b), (a, b)), lambda r, g: (_nt(r[1], g), _nn(r[0], g)))


def _hdot(a, b, ca=1, cb=0):
    ah, bh = a.astype(BF16), b.astype(BF16)
    al, bl = (a - ah.astype(F32)).astype(BF16), (b - bh.astype(F32)).astype(BF16)
    dot = lambda u, v: lax.dot_general(u, v, (((ca,), (cb,)), ((), ())), preferred_element_type=F32)
    return dot(ah, bh) + (dot(ah, bl) + dot(al, bh))


def _inv_impl(mats):
    n = mats[0].shape[0]
    eye = (lax.broadcasted_iota(jnp.int32, (n, n), 0) == lax.broadcasted_iota(jnp.int32, (n, n), 1)).astype(F32)
    ps = [-a for a in mats]
    ts = [eye + p for p in ps]
    for _ in range(max(1, (n - 1).bit_length()) - 1):
        ps = [_hdot(p, p) for p in ps]
        ts = [t + _hdot(t, p) for t, p in zip(ts, ps)]
    return tuple(ts)


@jax.custom_vjp
def _inv(mats):
    return _inv_impl(mats)


def _inv_fwd(mats):
    ts = _inv_impl(mats)
    return ts, ts


def _inv_bwd(ts, gs):
    xs = [_hdot(t, g, 0, 0) for t, g in zip(ts, gs)]
    return (tuple(-_hdot(x, t, 1, 1) for x, t in zip(xs, ts)),)


_inv.defvjp(_inv_fwd, _inv_bwd)


@jax.custom_vjp
def _inv_saved(mats, saved):
    return saved


_inv_saved.defvjp(lambda mats, saved: (saved, saved),
                  lambda ts, gs: (_inv_bwd(ts, gs)[0], tuple(jnp.zeros_like(t) for t in ts)))


def _tile(n, cap):
    if n <= cap:
        return n
    best = [t for t in range(128, cap + 1, 128) if n % t == 0]
    assert best, (n, cap)
    return best[-1]


def _mm(a, b, *, name, ta=False, tb=False, add=None, tm=1024, tn=1024, tk=1024, after=None, out_dtype=F32):
    if ta:
        k_, m_ = a.shape
    else:
        m_, k_ = a.shape
    n_ = b.shape[0] if tb else b.shape[1]
    tm, tn, tk = _tile(m_, tm), _tile(n_, tn), _tile(k_, tk)
    nk = k_ // tk
    has_add = add is not None

    def body(*refs):
        a_ref, b_ref = refs[:2]
        add_ref = refs[2] if has_add else None
        o_ref = refs[2 + has_add + (after is not None)]
        part = _raw_dot(a_ref[...], b_ref[...], 0 if ta else 1, 1 if tb else 0)
        if nk == 1:
            o_ref[...] = (part + add_ref[...] if has_add else part).astype(o_ref.dtype)
            return
        acc = refs[-1]
        k = pl.program_id(2)

        @pl.when(k == 0)
        def _():
            acc[...] = part

        @pl.when(jnp.logical_and(k > 0, k < nk - 1))
        def _():
            acc[...] += part

        @pl.when(k == nk - 1)
        def _():
            o_ref[...] = (acc[...] + part + add_ref[...] if has_add else acc[...] + part).astype(o_ref.dtype)

    a_spec = pl.BlockSpec((tk, tm), lambda i, j, k: (k, i)) if ta else pl.BlockSpec((tm, tk), lambda i, j, k: (i, k))
    b_spec = pl.BlockSpec((tn, tk), lambda i, j, k: (j, k)) if tb else pl.BlockSpec((tk, tn), lambda i, j, k: (k, j))
    o_spec = pl.BlockSpec((tm, tn), lambda i, j, k: (i, j))
    in_specs = [a_spec, b_spec] + ([o_spec] if has_add else []) + ([ANY] if after is not None else [])
    args = (a, b) + ((add,) if has_add else ()) + ((after,) if after is not None else ())
    return pl.pallas_call(
        body, name=name, grid=(m_ // tm, n_ // tn, nk), in_specs=in_specs, out_specs=o_spec,
        out_shape=jax.ShapeDtypeStruct((m_, n_), out_dtype),
        scratch_shapes=[pltpu.VMEM((tm, tn), F32)] if nk > 1 else [],
        compiler_params=_params(3))(*args)


def _mm_fused(a, b, post, outs, *, name, rows=(), params=(), n_sums=0, tb=False, tm=1024, tn=1024, tk=1024):
    m_, k_ = a.shape
    n_ = b.shape[0] if tb else b.shape[1]
    tm, tn, tk = _tile(m_, tm), _tile(n_, tn), _tile(k_, tk)
    nk = k_ // tk
    nr, npar, no = len(rows), len(params), len(outs)
    aliased = [o[4] for o in outs if o[4] is not None]
    nin = 2 + nr + npar

    def body(*refs):
        a_ref, b_ref = refs[:2]
        row_refs, par_refs = refs[2:2 + nr], refs[2 + nr:nin]
        out_refs = refs[nin + len(aliased):nin + len(aliased) + no + n_sums]
        part = _raw_dot(a_ref[...], b_ref[...], 1, 1 if tb else 0)
        k = pl.program_id(2)
        first = jnp.logical_and(pl.program_id(0) == 0, pl.program_id(1) == 0)
        if nk > 1:
            acc = refs[-1]

            @pl.when(k == 0)
            def _():
                acc[...] = part

            @pl.when(jnp.logical_and(k > 0, k < nk - 1))
            def _():
                acc[...] += part

        @pl.when(k == nk - 1)
        def _():
            total = acc[...] + part if nk > 1 else part
            res = post(total, *[r[...].astype(F32) for r in row_refs], *[p[...] for p in par_refs])
            for o_ref, val in zip(out_refs[:no], res[:no]):
                o_ref[...] = val.astype(o_ref.dtype)
            for s_ref, val in zip(out_refs[no:], res[no:]):
                @pl.when(first)
                def _(s_ref=s_ref):
                    s_ref[...] = jnp.zeros_like(s_ref)
                s_ref[...] += val

    b_spec = pl.BlockSpec((tn, tk), lambda i, j, k: (j, k)) if tb else pl.BlockSpec((tk, tn), lambda i, j, k: (k, j))
    in_specs = [pl.BlockSpec((tm, tk), lambda i, j, k: (i, k)), b_spec]
    in_specs += [pl.BlockSpec((tm, w), lambda i, j, k, cb=cb: (i, cb(j))) for _, w, cb in rows]
    in_specs += [_full(p) for p in params] + [ANY] * len(aliased)
    out_specs = [pl.BlockSpec((tm, w), lambda i, j, k, cb=cb: (i, cb(j))) for _, w, _, cb, _ in outs]
    out_specs += [_full(p) for p in params[:n_sums]]
    out_shape = [jax.ShapeDtypeStruct((m_, tot), dt) for tot, _, dt, _, _ in outs]
    out_shape += [jax.ShapeDtypeStruct(p.shape, F32) for p in params[:n_sums]]
    aliases, pos = {}, nin
    for oi, o in enumerate(outs):
        if o[4] is not None:
            aliases[pos] = oi
            pos += 1
    return pl.pallas_call(
        body, name=name, grid=(m_ // tm, n_ // tn, nk), in_specs=in_specs, out_specs=out_specs, out_shape=out_shape,
        scratch_shapes=[pltpu.VMEM((tm, tn), F32)] if nk > 1 else [], input_output_aliases=aliases,
        compiler_params=_params(3))(a, b, *[r[0] for r in rows], *params, *aliased)


def _mm_pre(pre, rows, params, b, *, name, tm=512):
    m_ = rows[0][0].shape[0]
    k_, n_ = b.shape
    tm = _tile(m_, tm)
    nr, npar = len(rows), len(params)

    def body(*refs):
        b_ref, a_out, o_ref = refs[nr + npar:]
        a = pre(*[r[...].astype(F32) for r in refs[:nr]], *[p[...] for p in refs[nr:nr + npar]]).astype(a_out.dtype)
        a_out[...] = a
        o_ref[...] = _raw_dot(a, b_ref[...], 1, 0)

    return pl.pallas_call(
        body, name=name, grid=(m_ // tm,),
        in_specs=[pl.BlockSpec((tm, w), lambda i, cb=cb: (i, cb)) for _, w, cb in rows] + [_full(p) for p in params]
        + [_full(b)],
        out_specs=[pl.BlockSpec((tm, k_), lambda i: (i, 0)), pl.BlockSpec((tm, n_), lambda i: (i, 0))],
        out_shape=[jax.ShapeDtypeStruct((m_, k_), _MXU), jax.ShapeDtypeStruct((m_, n_), F32)],
        compiler_params=_params(1))(*[r[0] for r in rows], *params, b)


def _row_spec(tb, w, c0, percol):
    return pl.BlockSpec((tb, w), lambda i, j, c0=c0, pc=percol: (i, c0 + (j if pc else 0)))


def _stage_fwd(f, name, rows, params, outs, tb, ncol=1):
    t_ = rows[0][0].shape[0]
    nr, npar = len(rows), len(params)

    def body(*refs):
        res = f(*[r[...].astype(F32) for r in refs[:nr + npar]])
        for o_ref, val in zip(refs[nr + npar:], res):
            o_ref[...] = val.astype(o_ref.dtype)

    return pl.pallas_call(
        body, name=name, grid=(t_ // tb, ncol),
        in_specs=[_row_spec(tb, w, c0, pc) for (_, w, c0, pc) in rows] + [_full(p) for p in params],
        out_specs=[pl.BlockSpec((tb, w), lambda i, j: (i, j)) for w, _ in outs],
        out_shape=[jax.ShapeDtypeStruct((t_, w * ncol), dt) for w, dt in outs],
        compiler_params=_params(2))(*[r[0] for r in rows], *params)


def _into(dest, tb, width, t_, ncol, dtype):
    if dest is None:
        return pl.BlockSpec((tb, width), lambda i, j: (i, j)), jax.ShapeDtypeStruct((t_, width * ncol), dtype), None
    buf, total, c0 = dest
    return (pl.BlockSpec((tb, width), lambda i, j, c0=c0: (i, c0 + j)), jax.ShapeDtypeStruct((t_, total), dtype), buf)


def _stage_bwd(f, name, rows, params, douts, tb, ncol=1, cat=None, dtypes=None, dest=None, after=None):
    t_ = rows[0][0].shape[0]
    nr, npar, nd = len(rows), len(params), len(douts)
    cat = cat if cat is not None else [[r] for r in range(nr)]
    dtypes = dtypes if dtypes is not None else [F32] * len(cat)
    dest = dest or {}
    assert ncol == 1 or all(len(g) == 1 and rows[g[0]][3] for g in cat)
    nin = nr + npar + nd
    unread = [after] if after is not None else []

    def body(*refs):
        ins = [r[...].astype(F32) for r in refs[:nr + npar]]
        dvals = tuple(r[...].astype(F32) for r in refs[nr + npar:nin])
        out_refs = refs[nin + len(aliased) + len(unread):]
        _, vjp = jax.vjp(f, *ins)
        grads = vjp(dvals)
        for o_ref, grp in zip(out_refs[:len(cat)], cat):
            val = grads[grp[0]] if len(grp) == 1 else jnp.concatenate([grads[r] for r in grp], axis=-1)
            o_ref[...] = val.astype(o_ref.dtype)
        first = jnp.logical_and(pl.program_id(0) == 0, pl.program_id(1) == 0)
        for p_ref, gp in zip(out_refs[len(cat):], grads[nr:]):
            @pl.when(first)
            def _(p_ref=p_ref):
                p_ref[...] = jnp.zeros_like(p_ref)
            p_ref[...] += gp

    gw = [sum(rows[r][1] for r in grp) for grp in cat]
    out_specs, out_shape, aliased, aliases = [], [], [], {}
    for gi, (w, dt) in enumerate(zip(gw, dtypes)):
        spec, shape, buf = _into(dest.get(gi), tb, w, t_, ncol, dt)
        out_specs.append(spec)
        out_shape.append(shape)
        if buf is not None:
            aliases[nin + len(aliased)] = gi
            aliased.append(buf)
    return pl.pallas_call(
        body, name=name, grid=(t_ // tb, ncol),
        in_specs=[_row_spec(tb, w, c0, pc) for (_, w, c0, pc) in rows] + [_full(p) for p in params]
        + [pl.BlockSpec((tb, d.shape[1] // ncol), lambda i, j: (i, j)) for d in douts]
        + [ANY] * (len(aliased) + len(unread)),
        out_specs=out_specs + [_full(p) for p in params],
        out_shape=out_shape + [jax.ShapeDtypeStruct(p.shape, F32) for p in params],
        input_output_aliases=aliases,
        compiler_params=_params(2))(*[r[0] for r in rows], *params, *douts, *aliased, *unread)


def _ln_f(x, y, g, b):
    u = ALPHA * x + y
    c = u - jnp.mean(u, axis=-1, keepdims=True)
    var = jnp.mean(c * c, axis=-1, keepdims=True)
    return (c * lax.rsqrt(var + LN_EPS) * g + b,)


def _ln_f2(x, y, g, b):
    out, = _ln_f(x, y, g, b)
    return out, out


def _swiglu_tile(gu):
    half = gu.shape[1] // 2
    return (jax.nn.silu(gu[:, :half]) * gu[:, half:],)


def _merge_f(ya, yb, ga, gb):
    return (jax.nn.sigmoid(ga) * ya + jax.nn.sigmoid(gb) * yb,)


def _gnorm_f(of, ob, z, w):
    o = of + ob
    return (o * lax.rsqrt(jnp.mean(o * o, axis=-1, keepdims=True) + RMS_EPS) * w * jax.nn.silu(z),)


def _bg_f(x, arow, dtrow):
    lane = lax.broadcasted_iota(jnp.int32, x.shape, 1)
    beta = jax.nn.sigmoid(x)
    g = -jnp.exp(arow) * jax.nn.softplus(x + dtrow)
    return (jnp.where(lane < 16, beta, jnp.where(lane < 32, g, 0.0)),)


PREP_ROWS = 512
PAD = 8


def _prep_f(part, w, *wins):
    xc = wins[0] * w[0:1, :]
    for k in range(1, DN_CONV):
        xc = xc + wins[k] * w[k:k + 1, :]
    a = jax.nn.silu(xc)
    nrm = a * lax.rsqrt(jnp.sum(a * a, axis=-1, keepdims=True) + RMS_EPS)
    return jnp.where(part == 0, nrm * (DN_DIM ** -0.5), jnp.where(part == 1, nrm, a))


def _windows(pad_ref, r0, rows):
    return [pad_ref[PAD + r0 - 2 + k:PAD + r0 - 2 + k + rows, :] for k in range(DN_CONV)]


def _prep_fwd(pm, conv):
    t_ = pm.shape[0]
    rows = min(PREP_ROWS, t_)

    def body(x_ref, w_ref, o_ref, pad_ref):
        part = pl.program_id(0) // DN_HEADS
        pad_ref[0:PAD, :] = jnp.zeros((PAD, DN_DIM), F32)
        pad_ref[PAD + t_:2 * PAD + t_, :] = jnp.zeros((PAD, DN_DIM), F32)
        pad_ref[PAD:PAD + t_, :] = x_ref[...]
        w = w_ref[...]
        for r in range(t_ // rows):
            o_ref[r * rows:(r + 1) * rows, :] = _prep_f(part, w, *_windows(pad_ref, r * rows, rows))

    ncb = 3 * DN_HEADS
    return pl.pallas_call(
        body, name="prep_fwd", grid=(ncb,),
        in_specs=[pl.BlockSpec((t_, DN_DIM), lambda j: (0, j)), pl.BlockSpec((DN_CONV, DN_DIM), lambda j: (0, j))],
        out_specs=pl.BlockSpec((t_, DN_DIM), lambda j: (0, j)),
        out_shape=jax.ShapeDtypeStruct((t_, ncb * DN_DIM), F32),
        scratch_shapes=[pltpu.VMEM((t_ + 2 * PAD, DN_DIM), F32)],
        compiler_params=_params(1))(pm, conv)


def _prep_bwd(pm, conv, dout, dpm):
    t_ = pm.shape[0]
    rows = min(PREP_ROWS, t_)

    def body(x_ref, w_ref, d_ref, _, dx_ref, dw_ref, pad_ref, dpad_ref):
        part = pl.program_id(0) // DN_HEADS
        pad_ref[0:PAD, :] = jnp.zeros((PAD, DN_DIM), F32)
        pad_ref[PAD + t_:2 * PAD + t_, :] = jnp.zeros((PAD, DN_DIM), F32)
        pad_ref[PAD:PAD + t_, :] = x_ref[...]
        dpad_ref[...] = jnp.zeros_like(dpad_ref)
        w = w_ref[...]
        dw = jnp.zeros((DN_CONV, DN_DIM), F32)
        for r in range(t_ // rows):
            r0 = r * rows
            _, vjp = jax.vjp(functools.partial(_prep_f, part), w, *_windows(pad_ref, r0, rows))
            grads = vjp(d_ref[r0:r0 + rows, :])
            dw = dw + grads[0]
            for k in range(DN_CONV):
                lo = PAD + r0 - 2 + k
                dpad_ref[lo:lo + rows, :] += grads[1 + k]
        dx_ref[...] = dpad_ref[PAD:PAD + t_, :].astype(dx_ref.dtype)
        dw_ref[...] = dw

    ncb = 3 * DN_HEADS
    col = pl.BlockSpec((t_, DN_DIM), lambda j: (0, j))
    wsp = pl.BlockSpec((DN_CONV, DN_DIM), lambda j: (0, j))
    return pl.pallas_call(
        body, name="prep_bwd", grid=(ncb,), in_specs=[col, wsp, col, ANY], out_specs=[col, wsp],
        out_shape=[jax.ShapeDtypeStruct(dpm.shape, dpm.dtype), jax.ShapeDtypeStruct((DN_CONV, ncb * DN_DIM), F32)],
        scratch_shapes=[pltpu.VMEM((t_ + 2 * PAD, DN_DIM), F32), pltpu.VMEM((t_ + 2 * PAD, DN_DIM), F32)],
        input_output_aliases={3: 0}, compiler_params=_params(1))(pm, conv, dout, dpm)


def _dn_chunk(sgns, qs, ks, vs, grows, brows, tsaved=None, with_t=False):
    c = qs[0].shape[0]
    i = lax.broadcasted_iota(jnp.int32, (c, c), 0)
    j = lax.broadcasted_iota(jnp.int32, (c, c), 1)
    eye = i == j
    incl = {s: (i - j) * int(s) >= 0 for s in set(sgns)}
    strict = {s: (i - j) * int(s) > 0 for s in set(sgns)}
    gcs = [jnp.sum(jnp.where(incl[s], g, 0.0), axis=1, keepdims=True) for s, g in zip(sgns, grows)]
    grs = [jnp.sum(jnp.where(eye, gc, 0.0), axis=0, keepdims=True) for gc in gcs]
    bcs = [jnp.sum(jnp.where(eye, b, 0.0), axis=1, keepdims=True) for b in brows]
    gls = [jnp.sum(g, axis=1, keepdims=True) for g in grows]
    decs = [jnp.exp(jnp.where(incl[s], gc - gr, -1e30)) for s, gc, gr in zip(sgns, gcs, grs)]
    kks = [_nt(k, k) for k in ks]
    amats = tuple(jnp.where(strict[s], bc * kk * dec, 0.0) for s, bc, kk, dec in zip(sgns, bcs, kks, decs))
    tinvs = _inv(amats) if tsaved is None else _inv_saved(amats, tsaved)
    egcs = [jnp.exp(gc) for gc in gcs]
    us = [_nn(t, v * bc) for t, v, bc in zip(tinvs, vs, bcs)]
    ws = [_nn(t, k * (bc * egc)) for t, k, bc, egc in zip(tinvs, ks, bcs, egcs)]
    qks = [_nt(q, k) * dec for q, k, dec in zip(qs, ks, decs)]
    qds = [q * egc for q, egc in zip(qs, egcs)]
    kds = [k * jnp.exp(gl - gc) for k, gl, gc in zip(ks, gls, gcs)]
    res = tuple(us), tuple(ws), tuple(qks), tuple(qds), tuple(kds)
    return res + (tinvs,) if with_t else res


def _dn_step(us, ws, qks, qds, kds, grows, ss):
    gls = [jnp.exp(jnp.sum(g, axis=1, keepdims=True)) for g in grows]
    wss = [_nn(w, s) for w, s in zip(ws, ss)]
    qss = [_nn(qd, s) for qd, s in zip(qds, ss)]
    vns = [u - x for u, x in zip(us, wss)]
    os_ = [a + _nn(qk, vn) for a, qk, vn in zip(qss, qks, vns)]
    s2s = [s * gl + _tn(kd, vn) for s, gl, kd, vn in zip(ss, gls, kds, vns)]
    return tuple(os_), tuple(s2s)


def _hs(h):
    return slice(h * DN_DIM, (h + 1) * DN_DIM)


_DIR_SGN = (1, -1)
_A_OUT = 5
_PROBLEMS = [(d, h) for d in range(2) for h in range(DN_HEADS)]
_SGNS = [_DIR_SGN[d] for d, _ in _PROBLEMS]


def _chunk_inputs(q_ref, k_ref, v_ref, g_ref, b_ref):
    heads = lambda ref: tuple(ref[:, _hs(h)].astype(F32) for _, h in _PROBLEMS)
    rows = lambda ref: tuple(ref[d, 0, h:h + 1, :] for d, h in _PROBLEMS)
    return heads(q_ref), heads(k_ref), heads(v_ref), rows(g_ref), rows(b_ref)


def _dn_a_fwd(qkv, grows, brows):
    t_ = qkv.shape[0]
    nch = t_ // CHUNK

    def body(q_ref, k_ref, v_ref, g_ref, b_ref, *outs):
        us, ws, qks, qds, kds, tinvs = _dn_chunk(_SGNS, *_chunk_inputs(q_ref, k_ref, v_ref, g_ref, b_ref), with_t=True)
        for p, (d, h) in enumerate(_PROBLEMS):
            u_ref, w_ref, qk_ref, qd_ref, kd_ref = outs[d * _A_OUT:(d + 1) * _A_OUT]
            u_ref[:, _hs(h)], w_ref[:, _hs(h)], qk_ref[0, h] = us[p], ws[p].astype(_MXU), qks[p].astype(_MXU)
            qd_ref[:, _hs(h)], kd_ref[:, _hs(h)] = qds[p].astype(_MXU), kds[p].astype(_MXU)
            outs[2 * _A_OUT + d][0, h] = tinvs[p]

    rspec = pl.BlockSpec((2, 1, DN_HEADS, CHUNK), lambda c: (0, c, 0, 0))
    big = pl.BlockSpec((CHUNK, D), lambda c: (c, 0))
    qks = pl.BlockSpec((1, DN_HEADS, CHUNK, CHUNK), lambda c: (c, 0, 0, 0))
    bigs = lambda dt: jax.ShapeDtypeStruct((t_, D), dt)
    qksh = lambda dt: jax.ShapeDtypeStruct((nch, DN_HEADS, CHUNK, CHUNK), dt)
    res = pl.pallas_call(
        body, name="dn_a_fwd", grid=(nch,),
        in_specs=[pl.BlockSpec((CHUNK, D), lambda c, p=p: (c, p)) for p in range(3)] + [rspec, rspec],
        out_specs=[big, big, qks, big, big] * 2 + [qks, qks],
        out_shape=[bigs(F32), bigs(_MXU), qksh(_MXU), bigs(_MXU), bigs(_MXU)] * 2 + [qksh(F32)] * 2,
        compiler_params=_params(1))(qkv, qkv, qkv, grows, brows)
    return res[:2 * _A_OUT], res[2 * _A_OUT:]


def _dn_a_bwd(qkv, grows, brows, tinv, dres, dg_b):
    t_ = qkv.shape[0]
    nch = t_ // CHUNK

    def body(q_ref, k_ref, v_ref, g_ref, b_ref, tf_ref, tb_ref, *rest):
        dins, dgb_ref, (dqkv_ref, dg_ref, db_ref) = rest[:2 * _A_OUT], rest[2 * _A_OUT], rest[2 * _A_OUT + 1:]
        tsaved = tuple((tf_ref, tb_ref)[d][0, h] for d, h in _PROBLEMS)
        _, vjp = jax.vjp(functools.partial(_dn_chunk, _SGNS, tsaved=tsaved),
                         *_chunk_inputs(q_ref, k_ref, v_ref, g_ref, b_ref))
        cots = []
        for o in range(_A_OUT):
            cots.append(tuple(dins[d * _A_OUT + o][0, h] if o == 2 else dins[d * _A_OUT + o][:, _hs(h)]
                              for d, h in _PROBLEMS))
        gq, gk, gv, gg, gb = vjp(tuple(cots))
        for p, (d, h) in enumerate(_PROBLEMS):
            dg_ref[d, 0, h:h + 1, :] = gg[p] + dgb_ref[d, 0, h:h + 1, :]
            db_ref[d, 0, h:h + 1, :] = gb[p]
        for h in range(DN_HEADS):
            dqkv_ref[:, _hs(h)] = gq[h] + gq[DN_HEADS + h]
            dqkv_ref[:, _hs(DN_HEADS + h)] = gk[h] + gk[DN_HEADS + h]
            dqkv_ref[:, _hs(2 * DN_HEADS + h)] = gv[h] + gv[DN_HEADS + h]

    rspec = pl.BlockSpec((2, 1, DN_HEADS, CHUNK), lambda c: (0, c, 0, 0))
    big = pl.BlockSpec((CHUNK, D), lambda c: (c, 0))
    qks = pl.BlockSpec((1, DN_HEADS, CHUNK, CHUNK), lambda c: (c, 0, 0, 0))
    rsh = jax.ShapeDtypeStruct(grows.shape, F32)
    return pl.pallas_call(
        body, name="dn_a_bwd", grid=(nch,),
        in_specs=[pl.BlockSpec((CHUNK, D), lambda c, p=p: (c, p)) for p in range(3)] + [rspec, rspec, qks, qks]
        + [big, big, qks, big, big] * 2 + [rspec],
        out_specs=[pl.BlockSpec((CHUNK, 3 * D), lambda c: (c, 0)), rspec, rspec],
        out_shape=[jax.ShapeDtypeStruct((t_, 3 * D), F32), rsh, rsh],
        compiler_params=_params(1))(qkv, qkv, qkv, grows, brows, *tinv, *dres, dg_b)


def _dir_specs(nch):
    def cidx(d):
        return (lambda n: n) if d == 0 else (lambda n: nch - 1 - n)
    out = []
    for d in range(2):
        ci = cidx(d)
        big = pl.BlockSpec((CHUNK, D), lambda n, ci=ci: (ci(n), 0))
        qks = pl.BlockSpec((1, DN_HEADS, CHUNK, CHUNK), lambda n, ci=ci: (ci(n), 0, 0, 0))
        row = pl.BlockSpec((1, 1, DN_HEADS, CHUNK), lambda n, ci=ci, d=d: (d, ci(n), 0, 0))
        st = pl.BlockSpec((1, DN_HEADS, DN_DIM, DN_DIM), lambda n, ci=ci: (ci(n), 0, 0, 0))
        out.append(dict(big=big, qk=qks, row=row, st=st))
    return out


def _step_inputs(ins, per_dir):
    def pick(o):
        if o == 2:
            return tuple(ins[d * per_dir + o][0, h].astype(F32) for d, h in _PROBLEMS)
        if o == 5:
            return tuple(ins[d * per_dir + o][0, 0, h:h + 1, :] for d, h in _PROBLEMS)
        return tuple(ins[d * per_dir + o][:, _hs(h)].astype(F32) for d, h in _PROBLEMS)
    return [pick(o) for o in range(6)]


def _dn_b_fwd(ares, grows):
    t_ = ares[0].shape[0]
    nch = t_ // CHUNK
    sp = _dir_specs(nch)

    def body(*refs):
        ins, outs, s_ref = refs[:12], refs[12:16], refs[16]

        @pl.when(pl.program_id(0) == 0)
        def _():
            s_ref[...] = jnp.zeros_like(s_ref)

        ss = tuple(s_ref[p] for p in range(len(_PROBLEMS)))
        os_, s2s = _dn_step(*_step_inputs(ins, 6), ss)
        for p, (d, h) in enumerate(_PROBLEMS):
            outs[2 + d][0, h] = ss[p]
            outs[d][:, _hs(h)] = os_[p]
            s_ref[p] = s2s[p]

    in_specs, args = [], []
    for d in range(2):
        in_specs += [sp[d]["big"], sp[d]["big"], sp[d]["qk"], sp[d]["big"], sp[d]["big"], sp[d]["row"]]
        args += list(ares[d * _A_OUT:(d + 1) * _A_OUT]) + [grows]
    stsh = jax.ShapeDtypeStruct((nch, DN_HEADS, DN_DIM, DN_DIM), F32)
    osh = jax.ShapeDtypeStruct((t_, D), F32)
    return pl.pallas_call(
        body, name="dn_b_fwd", grid=(nch,), in_specs=in_specs,
        out_specs=[sp[0]["big"], sp[1]["big"], sp[0]["st"], sp[1]["st"]], out_shape=[osh, osh, stsh, stsh],
        scratch_shapes=[pltpu.VMEM((2 * DN_HEADS, DN_DIM, DN_DIM), F32)],
        compiler_params=_params(1))(*args)


def _dn_b_bwd(ares, grows, st_f, st_b, do):
    t_ = ares[0].shape[0]
    nch = t_ // CHUNK
    sp = _dir_specs(nch)
    rsp = [sp[1], sp[0]]

    def body(*refs):
        ins, outs, ds_ref = refs[:16], refs[16:28], refs[28]

        @pl.when(pl.program_id(0) == 0)
        def _():
            ds_ref[...] = jnp.zeros_like(ds_ref)

        ss = tuple(ins[d * 8 + 6][0, h] for d, h in _PROBLEMS)
        _, vjp = jax.vjp(_dn_step, *_step_inputs(ins, 8), ss)
        dos = tuple(ins[d * 8 + 7][:, _hs(h)] for d, h in _PROBLEMS)
        grads = vjp((dos, tuple(ds_ref[p] for p in range(len(_PROBLEMS)))))
        for p, (d, h) in enumerate(_PROBLEMS):
            du_ref, dw_ref, dqk_ref, dqd_ref, dkd_ref, dg_ref = outs[d * 6:(d + 1) * 6]
            du_ref[:, _hs(h)], dw_ref[:, _hs(h)], dqk_ref[0, h] = grads[0][p], grads[1][p], grads[2][p]
            dqd_ref[:, _hs(h)], dkd_ref[:, _hs(h)] = grads[3][p], grads[4][p]
            dg_ref[0, 0, h:h + 1, :] = grads[5][p]
            ds_ref[p] = grads[6][p]

    in_specs, args, out_specs, out_shape = [], [], [], []
    big_sh = jax.ShapeDtypeStruct((t_, D), F32)
    qk_sh = jax.ShapeDtypeStruct((nch, DN_HEADS, CHUNK, CHUNK), F32)
    row_sh = jax.ShapeDtypeStruct((1, nch, DN_HEADS, CHUNK), F32)
    for d in range(2):
        s = rsp[d]
        row0 = pl.BlockSpec((1, 1, DN_HEADS, CHUNK), lambda m, d=d: (0, (nch - 1 - m) if d == 0 else m, 0, 0))
        rowd = pl.BlockSpec((1, 1, DN_HEADS, CHUNK), lambda m, d=d: (d, (nch - 1 - m) if d == 0 else m, 0, 0))
        in_specs += [s["big"], s["big"], s["qk"], s["big"], s["big"], rowd, s["st"], s["big"]]
        args += list(ares[d * _A_OUT:(d + 1) * _A_OUT]) + [grows, (st_f, st_b)[d], do]
        out_specs += [s["big"], s["big"], s["qk"], s["big"], s["big"], row0]
        out_shape += [big_sh, big_sh, qk_sh, big_sh, big_sh, row_sh]
    res = pl.pallas_call(
        body, name="dn_b_bwd", grid=(nch,), in_specs=in_specs, out_specs=out_specs, out_shape=out_shape,
        scratch_shapes=[pltpu.VMEM((2 * DN_HEADS, DN_DIM, DN_DIM), F32)],
        compiler_params=_params(1))(*args)
    dares = list(res[0:5]) + list(res[6:11])
    return dares, jnp.concatenate([res[5], res[11]], axis=0)


@jax.custom_vjp
def _rot_half(x):
    half, width = SW_DIM // 2, x.shape[1]
    first = lax.broadcasted_iota(jnp.int32, x.shape, 1) % SW_DIM < half
    return jnp.where(first, -pltpu.roll(x, width - half, axis=1), pltpu.roll(x, half, axis=1))


_rot_half.defvjp(lambda x: (_rot_half(x), None), lambda _, g: (-_rot_half(g),))


def _rope(x, c, s):
    reps = x.shape[1] // c.shape[1]
    return x * jnp.tile(c, (1, reps)) + _rot_half(x) * jnp.tile(s, (1, reps))


def _rope_t(g, c, s):
    reps = g.shape[1] // c.shape[1]
    return g * jnp.tile(c, (1, reps)) - _rot_half(g * jnp.tile(s, (1, reps)))


_SW_SCALE = SW_DIM ** -0.5
_KV_HEADS = [[kvh * SW_GRP + g for g in range(SW_GRP)] for kvh in range(SW_KV)]


def _by_group(x):
    return [jnp.concatenate([x[:, h * SW_DIM:(h + 1) * SW_DIM] for h in hs], axis=0) for hs in _KV_HEADS]


def _from_groups(xs):
    return jnp.concatenate([x[g * SW_BLOCK:(g + 1) * SW_BLOCK] for x in xs for g in range(SW_GRP)], axis=-1)


def _attn_probs(blk, t_, cq, sq, ck, sk, q, kall, sinks):
    qgs = _by_group(_rope(q, cq, sq))
    kr = _rope(kall, ck, sk)
    khs = [kr[:, kvh * SW_DIM:(kvh + 1) * SW_DIM] for kvh in range(SW_KV)]
    nq, nk = SW_GRP * SW_BLOCK, 3 * SW_BLOCK
    qpos = lax.broadcasted_iota(jnp.int32, (nq, nk), 0) % SW_BLOCK
    krel = lax.broadcasted_iota(jnp.int32, (nq, nk), 1) - SW_BLOCK
    kglob = krel + blk * SW_BLOCK
    valid = (jnp.abs(qpos - krel) <= SW_BLOCK) & (kglob >= 0) & (kglob < t_)
    ss = [jnp.where(valid, _nt(qg * _SW_SCALE, kh), -1e30) for qg, kh in zip(qgs, khs)]
    snks = [jnp.concatenate([jnp.broadcast_to(sinks[:, h:h + 1], (SW_BLOCK, 1)) for h in hs], axis=0) for hs in _KV_HEADS]
    ms = [jnp.maximum(jnp.max(s, axis=-1, keepdims=True), snk) for s, snk in zip(ss, snks)]
    es = [jnp.exp(s - m) for s, m in zip(ss, ms)]
    esnks = [jnp.exp(snk - m) for snk, m in zip(snks, ms)]
    invs = [1.0 / (jnp.sum(e, axis=-1, keepdims=True) + esnk) for e, esnk in zip(es, esnks)]
    ps = [e * inv for e, inv in zip(es, invs)]
    return qgs, khs, ps, [esnk * inv for esnk, inv in zip(esnks, invs)]


def _attn_f(blk, t_, cq, sq, ck, sk, q, kp, ko, kn, vp, vo, vn, sinks):
    _, _, ps, psinks = _attn_probs(blk, t_, cq, sq, ck, sk, q, jnp.concatenate([kp, ko, kn], axis=0), sinks)
    vall = jnp.concatenate([vp, vo, vn], axis=0)
    ps = [p.astype(_MXU) for p in ps]
    kept = [jnp.concatenate([p, jnp.broadcast_to(psink.astype(_MXU), (p.shape[0], 128))], axis=-1)
            for p, psink in zip(ps, psinks)]
    return _from_groups([_nn(p, vall[:, kvh * SW_DIM:(kvh + 1) * SW_DIM]) for kvh, p in enumerate(ps)]), kept


def _attn_bwd_f(cq, sq, ck, sk, q, kp, ko, kn, vp, vo, vn, do, kept):
    nk = 3 * SW_BLOCK
    ps = [x[:, :nk] for x in kept]
    psinks = [x[:, nk:nk + 1].astype(F32) for x in kept]
    qgs = _by_group(_rope(q, cq, sq))
    kr = _rope(jnp.concatenate([kp, ko, kn], axis=0), ck, sk)
    khs = [kr[:, kvh * SW_DIM:(kvh + 1) * SW_DIM] for kvh in range(SW_KV)]
    vall = jnp.concatenate([vp, vo, vn], axis=0)
    vhs = [vall[:, kvh * SW_DIM:(kvh + 1) * SW_DIM] for kvh in range(SW_KV)]
    dogs = _by_group(do)
    dvs = [_tn(p, dog) for p, dog in zip(ps, dogs)]
    ps = [p.astype(F32) for p in ps]
    dps = [_nt(dog * _SW_SCALE, vh) for dog, vh in zip(dogs, vhs)]
    deltas = [jnp.sum(p * dp, axis=-1, keepdims=True) for p, dp in zip(ps, dps)]
    dss = [p * (dp - delta) for p, dp, delta in zip(ps, dps, deltas)]
    dqr = _from_groups([_nn(ds, kh) for ds, kh in zip(dss, khs)])
    dkr = jnp.concatenate([_tn(ds, qg) for ds, qg in zip(dss, qgs)], axis=-1)
    dsnk = [-(psink * delta) * (1.0 / _SW_SCALE) for psink, delta in zip(psinks, deltas)]
    dsinks = jnp.concatenate([jnp.sum(d[g * SW_BLOCK:(g + 1) * SW_BLOCK], axis=0, keepdims=True)
                              for d in dsnk for g in range(SW_GRP)], axis=1)
    dq, dk, dv = _rope_t(dqr, cq, sq), _rope_t(dkr, ck, sk), jnp.concatenate(dvs, axis=-1)
    blocks = lambda a: [a[j * SW_BLOCK:(j + 1) * SW_BLOCK] for j in range(3)]
    return [dq] + blocks(dk) + blocks(dv) + [dsinks]


def _attn_specs(nb):
    prv = lambda i: jnp.maximum(i - 1, 0)
    nxt = lambda i: jnp.minimum(i + 1, nb - 1)
    rows = [lambda i: i, prv, lambda i: i, nxt]
    tab = [pl.BlockSpec((SW_BLOCK, 128), lambda i, r=r: (r(i), 0)) for r in rows]
    qs = pl.BlockSpec((SW_BLOCK, SW_HEADS * SW_DIM), lambda i: (i, O_QS // (SW_HEADS * SW_DIM)))
    kw = SW_KV * SW_DIM
    ks = [pl.BlockSpec((SW_BLOCK, kw), lambda i, r=r: (r(i), O_KS // kw)) for r in rows[1:]]
    vs = [pl.BlockSpec((SW_BLOCK, kw), lambda i, r=r: (r(i), O_VS // kw)) for r in rows[1:]]
    return tab, qs, ks, vs


def _attn_tables(refs):
    cq, cp, co, cn, sq, sp_, so, sn = [r[...] for r in refs]
    return cq, sq, jnp.concatenate([cp, co, cn], axis=0), jnp.concatenate([sp_, so, sn], axis=0)


_P_BLOCK = (None, SW_KV, SW_GRP * SW_BLOCK, 3 * SW_BLOCK + 128)


def _attn_fwd(pm, cos, sin, sinks):
    t_ = pm.shape[0]
    nb = t_ // SW_BLOCK
    tab, qs, ks, vs = _attn_specs(nb)

    def body(*refs):
        tabs = _attn_tables(refs[:8])
        vals = [r[...] for r in refs[8:16]]
        out, ps = _attn_f(pl.program_id(0), t_, *tabs, *vals)
        refs[16][...] = out.astype(refs[16].dtype)
        for kvh, p in enumerate(ps):
            refs[17][kvh] = p

    return pl.pallas_call(
        body, name="attn_fwd", grid=(nb,), in_specs=tab + tab + [qs] + ks + vs + [_full(sinks)],
        out_specs=[pl.BlockSpec((SW_BLOCK, D), lambda i: (i, 0)), pl.BlockSpec(_P_BLOCK, lambda i: (i, 0, 0, 0))],
        out_shape=[jax.ShapeDtypeStruct((t_, D), _MXU), jax.ShapeDtypeStruct((nb,) + _P_BLOCK[1:], _MXU)],
        compiler_params=_params(1))(*([cos] * 4), *([sin] * 4), pm, pm, pm, pm, pm, pm, pm, sinks)


def _attn_bwd(pm, cos, sin, sinks, probs, do, dpm):
    t_ = pm.shape[0]
    nb = t_ // SW_BLOCK
    tab, qs, ks, vs = _attn_specs(nb)
    kw = SW_KV * SW_DIM

    def body(*refs):
        tabs = _attn_tables(refs[:8])
        vals = [r[...] for r in refs[8:15]]
        p_ref, do_ref, outs = refs[16], refs[17], refs[19:]
        grads = _attn_bwd_f(*tabs, *vals, do_ref[...], [p_ref[kvh] for kvh in range(SW_KV)])
        for o_ref, g in zip(outs[:7], grads[:7]):
            o_ref[...] = g.astype(o_ref.dtype)

        @pl.when(pl.program_id(0) == 0)
        def _():
            outs[7][...] = jnp.zeros_like(outs[7])
        outs[7][...] += grads[7]

    own = lambda w: pl.BlockSpec((SW_BLOCK, w), lambda i: (i, 0))
    return pl.pallas_call(
        body, name="attn_bwd", grid=(nb,),
        in_specs=tab + tab + [qs] + ks + vs + [_full(sinks), pl.BlockSpec(_P_BLOCK, lambda i: (i, 0, 0, 0)), own(D), ANY],
        out_specs=[pl.BlockSpec((SW_BLOCK, D), lambda i: (i, O_QS // D))] + [own(kw)] * 6 + [_full(sinks)],
        out_shape=[jax.ShapeDtypeStruct(dpm.shape, dpm.dtype)] + [jax.ShapeDtypeStruct((t_, kw), F32)] * 6
        + [jax.ShapeDtypeStruct(sinks.shape, F32)],
        input_output_aliases={18: 0},
        compiler_params=_params(1))(*([cos] * 4), *([sin] * 4), pm, pm, pm, pm, pm, pm, pm, sinks, probs, do, dpm)


def _band_sum(kparts, vparts, dpm):
    t_, kw = kparts[1].shape
    nb = t_ // SW_BLOCK

    def body(kp, ko, kn, vp, vo, vn, _, out_ref):
        j = pl.program_id(0)
        band = lambda p, o, n: o[...] + jnp.where(j + 1 < nb, p[...], 0.0) + jnp.where(j > 0, n[...], 0.0)
        out_ref[...] = jnp.concatenate([band(kp, ko, kn), band(vp, vo, vn)], axis=-1).astype(out_ref.dtype)

    specs = [pl.BlockSpec((SW_BLOCK, kw), lambda j: (jnp.minimum(j + 1, nb - 1), 0)),
             pl.BlockSpec((SW_BLOCK, kw), lambda j: (j, 0)),
             pl.BlockSpec((SW_BLOCK, kw), lambda j: (jnp.maximum(j - 1, 0), 0))]
    return pl.pallas_call(
        body, name="band_sum", grid=(nb,), in_specs=specs * 2 + [ANY],
        out_specs=pl.BlockSpec((SW_BLOCK, 2 * kw), lambda j: (j, O_KS // (2 * kw))),
        out_shape=jax.ShapeDtypeStruct(dpm.shape, dpm.dtype), input_output_aliases={6: 0},
        compiler_params=_params(1))(*kparts, *vparts, dpm)


def _loss_head(y, target, tb=256):
    t_ = y.shape[0]

    def body(y_ref, t_ref, dy_ref, acc_ref):
        err = y_ref[...] - t_ref[...]
        dy_ref[...] = err * (1.0 / D)
        sq = (err * err).reshape(tb // 8, 8, D).sum(axis=0)
        part = sq[:, 0:128]
        for c in range(1, D // 128):
            part = part + sq[:, c * 128:(c + 1) * 128]

        @pl.when(pl.program_id(0) == 0)
        def _():
            acc_ref[...] = jnp.zeros_like(acc_ref)
        acc_ref[...] += part

    row = pl.BlockSpec((tb, D), lambda i: (i, 0))
    return pl.pallas_call(
        body, name="loss_head", grid=(t_ // tb,), in_specs=[row, row],
        out_specs=[row, pl.BlockSpec((8, 128), lambda i: (0, 0))],
        out_shape=[jax.ShapeDtypeStruct((t_, D), F32), jax.ShapeDtypeStruct((8, 128), F32)],
        compiler_params=_params(1))(y, target)


def _adam_math(w, g, m, v):
    bc1 = 1.0 - ADAM_B1 ** ADAM_STEP
    bc2 = 1.0 - ADAM_B2 ** ADAM_STEP
    nm = ADAM_B1 * m + (1.0 - ADAM_B1) * g
    nv = ADAM_B2 * v + (1.0 - ADAM_B2) * (g * g)
    return -ADAM_LR * ((nm / bc1) / (jnp.sqrt(nv / bc2) + ADAM_EPS) + ADAM_WD * w), nm, nv


def _adamw_layer(name, l, w, g, m, v, carry, after):
    nl, r, c = w.shape
    g2 = g.reshape(r, -1)
    gc = g2.shape[1]
    w2, m2, v2 = [a.reshape(nl * r, c) for a in (w, m, v)]
    tb = r
    while tb * gc * 4 > (1 << 20) and tb % 16 == 0:
        tb //= 2
    nb = r // tb

    def body(w_ref, g_ref, m_ref, v_ref, *rest):
        go_ref, d_ref, nm_ref, nv_ref = rest[-4:]
        gv = g_ref[...][:, :c]
        go_ref[...] = gv
        d_ref[...], nm_ref[...], nv_ref[...] = _adam_math(w_ref[...], gv, m_ref[...], v_ref[...])

    spec = pl.BlockSpec((tb, c), lambda i: (l * nb + i, 0))
    carried = list(carry) if carry is not None else []
    outs = pl.pallas_call(
        body, name=name, grid=(nb,),
        in_specs=[spec, pl.BlockSpec((tb, gc), lambda i: (i, 0)), spec, spec] + [ANY] * (len(carried) + 1),
        out_specs=[spec] * 4, out_shape=[jax.ShapeDtypeStruct((nl * r, c), F32)] * 4,
        input_output_aliases={4 + k: k for k in range(len(carried))},
        compiler_params=_params(1))(w2, g2, m2, v2, *[a.reshape(nl * r, c) for a in carried], after)
    return tuple(o.reshape(w.shape) for o in outs)


def _adamw_layer_t(name, l, w, g, m, v, carry, after):
    nl, r, c = w.shape
    g2 = g.reshape(r, -1)
    gc = g2.shape[1]
    wt, mt, vt = [jnp.swapaxes(a, 1, 2).reshape(nl * c, r) for a in (w, m, v)]

    def body(w_ref, g_ref, m_ref, v_ref, *rest):
        go_ref, d_ref, nm_ref, nv_ref = rest[-4:]
        gv = g_ref[...].T[:c]
        go_ref[...] = gv
        d_ref[...], nm_ref[...], nv_ref[...] = _adam_math(w_ref[...], gv, m_ref[...], v_ref[...])

    spec = pl.BlockSpec((c, 128), lambda j: (l, j))
    carried = list(carry) if carry is not None else []
    return tuple(pl.pallas_call(
        body, name=name, grid=(r // 128,),
        in_specs=[spec, pl.BlockSpec((128, gc), lambda j: (j, 0)), spec, spec] + [ANY] * (len(carried) + 1),
        out_specs=[spec] * 4, out_shape=[jax.ShapeDtypeStruct((nl * c, r), F32)] * 4,
        input_output_aliases={4 + k: k for k in range(len(carried))},
        compiler_params=_params(1))(wt, g2, mt, vt, *carried, after))


def _adamw(name, w, g, m, v):
    shape = w.shape
    cols = shape[-1]
    rows = w.size // cols
    w2, g2, m2, v2 = [a.reshape(rows, cols) for a in (w, g, m, v)]
    tb = rows
    while tb * cols * 4 > (1 << 20) and tb % 16 == 0:
        tb //= 2

    def body(w_ref, g_ref, m_ref, v_ref, d_ref, nm_ref, nv_ref):
        d_ref[...], nm_ref[...], nv_ref[...] = _adam_math(w_ref[...], g_ref[...], m_ref[...], v_ref[...])

    spec = pl.BlockSpec((tb, cols), lambda i: (i, 0))
    sh = jax.ShapeDtypeStruct((rows, cols), F32)
    outs = pl.pallas_call(body, name=name, grid=(rows // tb,), in_specs=[spec] * 4, out_specs=[spec] * 3,
                          out_shape=[sh] * 3, compiler_params=_params(1))(w2, g2, m2, v2)
    return [o.reshape(shape) for o in outs]


def _to_rows(bg, col0):
    t_ = bg.shape[0]
    a = bg[:, col0:col0 + 2 * DN_HEADS].reshape(t_ // CHUNK, CHUNK, 2, DN_HEADS)
    return jnp.transpose(a, (2, 0, 3, 1))


def _from_rows(db, dg):
    nch = db.shape[1]
    back = lambda a: jnp.transpose(a, (1, 3, 0, 2)).reshape(nch * CHUNK, 2 * DN_HEADS)
    return jnp.pad(jnp.concatenate([back(db), back(dg)], axis=1), ((0, 0), (0, 128 - 4 * DN_HEADS)))


def _layer_fwd(x, xm, w, rest, cos, sin):
    t_ = x.shape[0]
    nb = min(1024, t_)
    pm = _mm(xm, w["in_main"], name="mm_in", tn=1536)
    pbg = _mm(xm, w["in_bg"], name="mm_in_bg")
    w = {**w, **rest(pbg)}
    qkv = _prep_fwd(pm, w["conv"])
    bg, = _stage_fwd(_bg_f, "bg_fwd", [(pbg, 128, 0, False)], [w["arow"], w["dtrow"]], [(128, F32)], nb)
    brows, grows = _to_rows(bg, 0), _to_rows(bg, 2 * DN_HEADS)
    ares, tinv = _dn_a_fwd(qkv, grows, brows)
    o_f, o_b, st_f, st_b = _dn_b_fwd(ares, grows)

    def gnorm_all(of, ob, z, gnw):
        return jnp.concatenate([_gnorm_f(of[:, _hs(h)], ob[:, _hs(h)], z[:, _hs(h)], gnw)[0]
                                for h in range(DN_HEADS)], axis=-1)

    odn, ya = _mm_pre(gnorm_all, [(o_f, D, 0), (o_b, D, 0), (pm, D, O_Z // D)], [w["gnw"]], w["a"], name="mm_a_gnorm")
    osw, probs = _attn_fwd(pm, cos, sin, w["sinks"])
    same, first = (lambda j: j), (lambda j: 0)
    full = lambda dt: (D, D, dt, first, None)
    yb, merged = _mm_fused(
        osw, w["b"], lambda acc, ya_, ga, gb: (acc,) + _merge_f(ya_, acc, ga, gb), [full(F32), full(_MXU)],
        name="mm_b_merge", rows=[(ya, D, first), (pm, D, lambda j: O_GA // D), (pm, D, lambda j: O_GB // D)], tm=512)
    mix, x1, x1m = _mm_fused(
        merged, w["o"], lambda acc, x_, g_, b_: (acc,) + _ln_f2(x_, acc, g_, b_), [full(F32), full(F32), full(_MXU)],
        name="mm_o_ln", rows=[(x, D, first)], params=[w["ln1g"], w["ln1b"]], tm=512)
    gu, hid = _mm_fused(x1m, w["gu"], lambda acc: (acc,) + _swiglu_tile(acc),
                        [(2 * FFN, FFN, F32, same, None), (FFN, FFN // 2, _MXU, same, None)], name="mm_gu_swiglu", tn=FFN)
    ffn, x2, x2m = _mm_fused(
        hid, w["d"], lambda acc, x_, g_, b_: (acc,) + _ln_f2(x_, acc, g_, b_), [full(F32), full(F32), full(_MXU)],
        name="mm_d_ln", rows=[(x1, D, first)], params=[w["ln2g"], w["ln2b"]], tm=512, tk=FFN)
    res = dict(w=w, x=x, xm=xm, pm=pm, pbg=pbg, qkv=qkv, grows=grows, brows=brows, ares=ares, tinv=tinv, o_f=o_f, o_b=o_b, st_f=st_f,
               st_b=st_b, odn=odn, osw=osw, probs=probs, ya=ya, yb=yb, merged=merged, mix=mix, x1=x1, x1m=x1m, gu=gu, hid=hid,
               ffn=ffn)
    return x2, x2m, res


def _layer_bwd(dx2, r, w, cos, sin, mid=None, after=None):
    t_ = dx2.shape[0]
    nb = min(1024, t_)
    pm = r["pm"]
    g = {}
    dx1a, dffn, g["ln2g"], g["ln2b"] = _stage_bwd(
        _ln_f, "ln2_bwd", [(r["x1"], D, 0, False), (r["ffn"], D, 0, False)], [w["ln2g"], w["ln2b"]], [dx2],
        min(512, t_), dtypes=[F32, _MXU], after=after)
    same = lambda j: j
    dgu, = _mm_fused(dffn, w["d"], lambda dhid, gu: jax.vjp(_swiglu_tile, gu)[1]((dhid,)),
                     [(2 * FFN, FFN, _MXU, same, None)], name="mm_d_dx_swiglu", rows=[(r["gu"], FFN, same)], tb=True,
                     tn=FFN // 2)
    g["d"] = _mm(r["hid"], dffn, ta=True, name="mm_d_dw", tm=FFN // 2, out_dtype=_MXU)
    first = lambda j: 0
    full = lambda dt: (D, D, dt, first, None)

    def ln1_back(acc, dx1a_, x_, mix_, g_, b_):
        dx_, dmix_, dg_, db_ = jax.vjp(_ln_f, x_, mix_, g_, b_)[1]((acc + dx1a_,))
        return dx_, dmix_, dg_, db_

    dxa, dmix, g["ln1g"], g["ln1b"] = _mm_fused(
        dgu, w["gu"], ln1_back, [full(F32), full(_MXU)], name="mm_gu_dx_ln", tb=True, tm=256, tk=2 * FFN, n_sums=2,
        rows=[(dx1a, D, first), (r["x"], D, first), (r["mix"], D, first)], params=[w["ln1g"], w["ln1b"]])
    g["gu"] = _mm(r["x1m"], dgu, ta=True, name="mm_gu_dw", tn=FFN // 2, out_dtype=_MXU)
    tok = mid(g["gu"]) if mid is not None else None
    g["o"] = _mm(r["merged"], dmix, ta=True, name="mm_o_dw", after=tok, out_dtype=_MXU)

    def merge_back(acc, ya_, yb_, ga, gb):
        dya_, dyb_, dga, dgb = jax.vjp(_merge_f, ya_, yb_, ga, gb)[1]((acc,))
        return dya_, dyb_, jnp.concatenate([dga, dgb], axis=-1)

    dya, dyb, dpm = _mm_fused(
        dmix, w["o"], merge_back, [full(_MXU), full(_MXU), (N_MAIN, 2 * D, _MXU, lambda j: O_GA // (2 * D), None)],
        name="mm_o_dx_merge", tb=True, tm=512,
        rows=[(r["ya"], D, first), (r["yb"], D, first), (pm, D, lambda j: O_GA // D), (pm, D, lambda j: O_GB // D)])

    def gnorm_back(acc, of, ob, z, gnw):
        dos, dzs, dws = [], [], None
        for h in range(DN_HEADS):
            do_h, _, dz_h, dw_h = jax.vjp(_gnorm_f, of[:, _hs(h)], ob[:, _hs(h)], z[:, _hs(h)], gnw)[1]((acc[:, _hs(h)],))
            dos.append(do_h)
            dzs.append(dz_h)
            dws = dw_h if dws is None else dws + dw_h
        return jnp.concatenate(dos, axis=-1), jnp.concatenate(dzs, axis=-1), dws

    dof, dpm, g["gnw"] = _mm_fused(
        dya, w["a"], gnorm_back, [full(F32), (N_MAIN, D, _MXU, lambda j: O_Z // D, dpm)], name="mm_a_dx_gnorm", tb=True,
        tm=512, rows=[(r["o_f"], D, first), (r["o_b"], D, first), (pm, D, lambda j: O_Z // D)], params=[w["gnw"]],
        n_sums=1)
    g["a"] = _mm(r["odn"], dya, ta=True, name="mm_a_dw", out_dtype=_MXU)
    dosw = _mm(dyb, w["b"], tb=True, name="mm_b_dx")
    g["b"] = _mm(r["osw"], dyb, ta=True, name="mm_b_dw", out_dtype=_MXU)
    ab = _attn_bwd(pm, cos, sin, w["sinks"], r["probs"], dosw, dpm)
    dpm, g["sinks"] = ab[0], ab[7]
    dpm = _band_sum(ab[1:4], ab[4:7], dpm)
    dares, dg_b = _dn_b_bwd(r["ares"], r["grows"], r["st_f"], r["st_b"], dof)
    dqkv, dgrows, dbrows = _dn_a_bwd(r["qkv"], r["grows"], r["brows"], r["tinv"], dares, dg_b)
    dbg = _from_rows(dbrows, dgrows)
    dpbg, g["arow"], g["dtrow"] = _stage_bwd(_bg_f, "bg_bwd", [(r["pbg"], 128, 0, False)], [w["arow"], w["dtrow"]],
                                             [dbg], nb)
    dpm, g["conv"] = _prep_bwd(pm, w["conv"], dqkv, dpm)
    dx = _mm(dpm, w["in_main"], tb=True, add=dxa, name="mm_in_dx", tk=1920)
    dx = _mm(dpbg, w["in_bg"], tb=True, add=dx, name="mm_in_bg_dx")
    g["in_main"] = _mm(r["xm"], dpm, ta=True, name="mm_in_dw", tn=1536, out_dtype=_MXU)
    g["in_bg"] = _mm(r["xm"], dpbg, ta=True, name="mm_in_bg_dw", out_dtype=_MXU)
    return dx, g


def _place():
    return lax.axis_index("x"), lax.axis_index("y"), lax.axis_index("c")


def _colblock(kind, q):
    return q if kind == "col" else (q >> 1) | ((q & 1) << 1)


def _other_chips(x, y):
    return [(1 - x, y), (x, 1 - y), (1 - x, 1 - y)]


HBM = pl.BlockSpec(memory_space=pltpu.HBM)
SEM = pl.BlockSpec(memory_space=pltpu.SEMAPHORE)
DATAFLOW = pltpu.SideEffectType.DATAFLOW_SIDE_EFFECTING


def _hbm(a):
    return pltpu.HBM(a.shape, a.dtype)


def _gather_start(locs, kinds, name):
    n = len(locs)
    locs = list(locs)
    lands = [lax.empty((N_CHIPS,) + a.shape if kind == "row" else (a.shape[0], N_CHIPS * a.shape[1]), a.dtype)
             for a, kind in zip(locs, kinds)]

    def body(*refs):
        loc_refs, land_refs = refs[:n], refs[n:2 * n]
        send_sems, recv_sems, token = refs[2 * n:3 * n], refs[3 * n:4 * n], refs[-1]
        x, y, c = _place()
        myq = 2 * x + y
        for t in range(n):
            width = locs[t].shape[1]
            mine = (land_refs[t].at[myq] if kinds[t] == "row" else
                    land_refs[t].at[:, pl.ds(pl.multiple_of(_colblock(kinds[t], myq) * width, 128), width)])
            for dev in [(cx, cy, c) for cx, cy in _other_chips(x, y)] + [(x, y, 1 - c)]:
                pltpu.make_async_remote_copy(src_ref=loc_refs[t], dst_ref=mine, send_sem=send_sems[t],
                                             recv_sem=recv_sems[t], device_id=dev, device_id_type=MESH).start()
        token[...] = jnp.zeros_like(token)

    res = pl.pallas_call(
        body, name=name,
        out_shape=[pltpu.SemaphoreType.DMA(())] * (2 * n) + [_hbm(a) for a in locs + lands]
        + [jax.ShapeDtypeStruct((8, 128), F32)],
        in_specs=[HBM] * (2 * n), out_specs=[SEM] * (2 * n) + [HBM] * (2 * n) + [pl.BlockSpec(memory_space=pltpu.VMEM)],
        input_output_aliases={t: 2 * n + t for t in range(2 * n)},
        compiler_params=pltpu.CompilerParams(has_side_effects=DATAFLOW),
    )(*[pltpu.with_memory_space_constraint(a, pltpu.HBM) for a in locs + lands])
    return (res[:n], res[n:2 * n], res[2 * n:3 * n], res[3 * n:4 * n]), res[-1]


def _split_wait(handle, after, name):
    send_sems, recv_sems, srcs, lands = handle
    n = len(srcs)

    def body(*refs):
        land_refs, ssems, rsems = refs[n:2 * n], refs[2 * n:3 * n], refs[3 * n:4 * n]
        x, y, c = _place()
        for t in range(n):
            done = pltpu.make_async_remote_copy(
                src_ref=land_refs[t], dst_ref=land_refs[t], send_sem=ssems[t], recv_sem=rsems[t],
                device_id=(x, y, c), device_id_type=MESH)
            done.wait_send()
            done.wait_recv()

    res = pl.pallas_call(
        body, name=name, out_shape=[_hbm(a) for a in list(srcs) + list(lands)],
        in_specs=[HBM] * (2 * n) + [SEM] * (2 * n) + [ANY], out_specs=[HBM] * (2 * n),
        input_output_aliases={t: t for t in range(2 * n)},
        compiler_params=pltpu.CompilerParams(has_side_effects=DATAFLOW),
    )(*srcs, *lands, *send_sems, *recv_sems, after)
    return res[:n], res[n:]


RS_CHUNKS = 2


def _piece(ref, kind, q, hf, pr, pc):
    if kind == "row":
        return ref.at[pl.ds((2 * q + hf) * pr, pr), :]
    return ref.at[pl.ds(hf * pr, pr), pl.ds(pl.multiple_of(_colblock(kind, q) * pc, 128), pc)]


def _rs_sibling_start(ts, meta, name):
    n = len(ts)
    ts = list(ts)
    lands = [lax.empty((N_CHIPS, pr, pc), a.dtype) for a, (_, pr, pc) in zip(ts, meta)]

    def body(*refs):
        t_refs, land_refs = refs[:n], refs[n:2 * n]
        send_sems, recv_sems, token = refs[2 * n:3 * n], refs[3 * n:4 * n], refs[-1]
        x, y, c = _place()
        for t, (kind, pr, pc) in enumerate(meta):
            for q in range(N_CHIPS):
                pltpu.make_async_remote_copy(
                    src_ref=_piece(t_refs[t], kind, q, 1 - c, pr, pc), dst_ref=land_refs[t].at[q],
                    send_sem=send_sems[t], recv_sem=recv_sems[t], device_id=(x, y, 1 - c), device_id_type=MESH).start()
        token[...] = jnp.zeros_like(token)

    res = pl.pallas_call(
        body, name=name,
        out_shape=[pltpu.SemaphoreType.DMA(())] * (2 * n) + [_hbm(a) for a in ts + lands]
        + [jax.ShapeDtypeStruct((8, 128), F32)],
        in_specs=[HBM] * (2 * n), out_specs=[SEM] * (2 * n) + [HBM] * (2 * n) + [pl.BlockSpec(memory_space=pltpu.VMEM)],
        input_output_aliases={t: 2 * n + t for t in range(2 * n)},
        compiler_params=pltpu.CompilerParams(has_side_effects=DATAFLOW),
    )(*[pltpu.with_memory_space_constraint(a, pltpu.HBM) for a in ts + lands])
    return (res[:n], res[n:2 * n], res[2 * n:3 * n], res[3 * n:4 * n]), res[-1]


def _rs_add_sibling(ts, r1s, meta, c):
    n = len(ts)
    in_specs, out_specs, out_shape = [], [], []
    for kind, pr, pc in meta:
        rs = pr // RS_CHUNKS
        if kind == "row":
            in_specs.append(pl.BlockSpec((rs, pc), lambda q, r, c_ref: ((2 * q + c_ref[0]) * RS_CHUNKS + r, 0)))
        else:
            in_specs.append(pl.BlockSpec(
                (rs, pc), lambda q, r, c_ref, kind=kind: (c_ref[0] * RS_CHUNKS + r, _colblock(kind, q))))
    for kind, pr, pc in meta:
        sp = pl.BlockSpec((None, pr // RS_CHUNKS, pc), lambda q, r, c_ref: (q, r, 0))
        in_specs.append(sp)
        out_specs.append(sp)
        out_shape.append(jax.ShapeDtypeStruct((N_CHIPS, pr, pc), BF16))

    def body(c_ref, *refs):
        for t in range(n):
            refs[2 * n + t][...] = (refs[t][...].astype(F32) + refs[n + t][...].astype(F32)).astype(BF16)

    return pl.pallas_call(
        body, name="rs_add_sibling", out_shape=out_shape,
        grid_spec=pltpu.PrefetchScalarGridSpec(num_scalar_prefetch=1, grid=(N_CHIPS, RS_CHUNKS), in_specs=in_specs,
                                               out_specs=out_specs),
        compiler_params=_params(2))(c.reshape(1).astype(jnp.int32), *ts, *r1s)


def _rs_chips_start(ps, meta, name):
    n = len(ps)
    ps = list(ps)
    lands = [lax.empty((N_CHIPS - 1, pr, pc), p.dtype) for p, (_, pr, pc) in zip(ps, meta)]

    def body(*refs):
        p_refs, land_refs = refs[:n], refs[n:2 * n]
        send_sems, recv_sems, token = refs[2 * n:3 * n], refs[3 * n:4 * n], refs[-1]
        x, y, c = _place()
        for t in range(n):
            for j, (cx, cy) in enumerate(_other_chips(x, y)):
                pltpu.make_async_remote_copy(
                    src_ref=p_refs[t].at[2 * cx + cy], dst_ref=land_refs[t].at[j], send_sem=send_sems[t],
                    recv_sem=recv_sems[t], device_id=(cx, cy, c), device_id_type=MESH).start()
        token[...] = jnp.zeros_like(token)

    res = pl.pallas_call(
        body, name=name,
        out_shape=[pltpu.SemaphoreType.DMA(())] * (2 * n) + [_hbm(a) for a in ps + lands]
        + [jax.ShapeDtypeStruct((8, 128), F32)],
        in_specs=[HBM] * (2 * n), out_specs=[SEM] * (2 * n) + [HBM] * (2 * n) + [pl.BlockSpec(memory_space=pltpu.VMEM)],
        input_output_aliases={t: 2 * n + t for t in range(2 * n)},
        compiler_params=pltpu.CompilerParams(has_side_effects=DATAFLOW),
    )(*[pltpu.with_memory_space_constraint(a, pltpu.HBM) for a in ps + lands])
    return (res[:n], res[n:2 * n], res[2 * n:3 * n], res[3 * n:4 * n]), res[-1]


def _rs_add_chips(ps, r2s, meta, myq, c):
    n = len(ps)
    in_specs, out_specs, out_shape = [], [], []
    for _, pr, pc in meta:
        in_specs.append(pl.BlockSpec((None, pr // RS_CHUNKS, pc), lambda r, q_ref, c_ref: (q_ref[0], r, 0)))
    for _, pr, pc in meta:
        in_specs.append(pl.BlockSpec((N_CHIPS - 1, pr // RS_CHUNKS, pc), lambda r, q_ref, c_ref: (0, r, 0)))
        out_specs.append(pl.BlockSpec((None, pr // RS_CHUNKS, pc), lambda r, q_ref, c_ref: (c_ref[0], r, 0)))
        out_shape.append(jax.ShapeDtypeStruct((2, pr, pc), F32))

    def body(q_ref, c_ref, *refs):
        for t in range(n):
            r2 = refs[n + t]
            own = refs[t][...].astype(F32)
            refs[2 * n + t][...] = ((own + r2[0].astype(F32)) + r2[1].astype(F32)) + r2[2].astype(F32)

    return pl.pallas_call(
        body, name="rs_add_chips", out_shape=out_shape,
        grid_spec=pltpu.PrefetchScalarGridSpec(num_scalar_prefetch=2, grid=(RS_CHUNKS,), in_specs=in_specs,
                                               out_specs=out_specs),
        compiler_params=_params(1))(myq.reshape(1).astype(jnp.int32), c.reshape(1).astype(jnp.int32), *ps, *r2s)


def _rs_share_halves(gs):
    n = len(gs)

    def body(*refs):
        g_refs, send_sems, recv_sems = refs[n:2 * n], refs[2 * n], refs[2 * n + 1]
        x, y, c = _place()
        sib = (x, y, 1 - c)
        for t in range(n):
            pltpu.make_async_remote_copy(
                src_ref=g_refs[t].at[c], dst_ref=g_refs[t].at[c], send_sem=send_sems.at[t], recv_sem=recv_sems.at[t],
                device_id=sib, device_id_type=MESH).start()
        for t in range(n):
            cp = pltpu.make_async_remote_copy(
                src_ref=g_refs[t].at[c], dst_ref=g_refs[t].at[1 - c], send_sem=send_sems.at[t],
                recv_sem=recv_sems.at[t], device_id=sib, device_id_type=MESH)
            cp.wait_send()
            cp.wait_recv()

    return pl.pallas_call(
        body, name="rs_share_halves", in_specs=[ANY] * n, out_specs=[ANY] * n,
        out_shape=[jax.ShapeDtypeStruct(g.shape, g.dtype) for g in gs], input_output_aliases={t: t for t in range(n)},
        scratch_shapes=[pltpu.SemaphoreType.DMA((n,)), pltpu.SemaphoreType.DMA((n,))],
    )(*gs)


def _rs_middle(handle, after, meta, c, name):
    ts, r1s = _split_wait(handle, after, name + "_sib_wait")
    ps = _rs_add_sibling(ts, r1s, meta, c)
    return _rs_chips_start(ps, meta, name + "_start")


def _rs_end(handle, after, meta, c, myq, name):
    ps, r2s = _split_wait(handle, after, name + "_wait")
    return _rs_share_halves(_rs_add_chips(ps, r2s, meta, myq, c))


def _allreduce_small(buf):
    rows = buf.shape[0]
    ndev = 8

    def body(b_ref, o_ref, slots, send_sems, recv_sems):
        x, y, c = _place()
        me = 4 * x + 2 * y + c
        slots[me] = b_ref[...]
        for k in range(1, ndev):
            kx, ky, kc = (k >> 2) & 1, (k >> 1) & 1, k & 1
            peer = (x ^ kx, y ^ ky, c ^ kc)
            pltpu.make_async_remote_copy(
                src_ref=b_ref, dst_ref=slots.at[me], send_sem=send_sems.at[k - 1], recv_sem=recv_sems.at[k - 1],
                device_id=peer, device_id_type=MESH).start()
        for k in range(1, ndev):
            kx, ky, kc = (k >> 2) & 1, (k >> 1) & 1, k & 1
            cp = pltpu.make_async_remote_copy(
                src_ref=b_ref, dst_ref=slots.at[me ^ k], send_sem=send_sems.at[k - 1], recv_sem=recv_sems.at[k - 1],
                device_id=(x ^ kx, y ^ ky, c ^ kc), device_id_type=MESH)
            cp.wait_send()
            cp.wait_recv()
        acc = slots[0]
        for s in range(1, ndev):
            acc = acc + slots[s]
        o_ref[...] = acc

    vm = pl.BlockSpec(memory_space=pltpu.VMEM)
    return pl.pallas_call(
        body, name="allreduce_small", in_specs=[vm], out_specs=vm, out_shape=jax.ShapeDtypeStruct((rows, 128), F32),
        scratch_shapes=[pltpu.VMEM((ndev, rows, 128), F32), pltpu.SemaphoreType.DMA((ndev - 1,)),
                        pltpu.SemaphoreType.DMA((ndev - 1,))],
        compiler_params=pltpu.CompilerParams(vmem_limit_bytes=VMEM_LIMIT))(buf)


RS_META = [("col", D // 2, IN_PAD), ("row", D // 8, D), ("row", D // 8, D), ("row", D // 8, D),
           ("colx", D // 2, 2 * FFN // N_CHIPS), ("row", FFN // 8, D)]
SMALL_ROWS = 156


def _rope_tables(t_):
    half = SW_DIM // 2
    inv_freq = ROPE_THETA ** (-jnp.arange(half, dtype=F32) / half)
    ang = jnp.arange(t_, dtype=F32)[:, None] * inv_freq[None, :]
    reps = 128 // half
    return jnp.concatenate([jnp.cos(ang)] * reps, axis=1), jnp.concatenate([jnp.sin(ang)] * reps, axis=1)


def _orig_cols(padded, a, b):
    out = []
    for q in range(N_CHIPS):
        lo, hi = max(a, q * IN_SHARD), min(b, (q + 1) * IN_SHARD)
        if lo < hi:
            out.append(padded[:, q * IN_PAD + lo - q * IN_SHARD:q * IN_PAD + hi - q * IN_SHARD])
    return out


_ORIG_SEGMENTS = [(0, R_BG, "main", 0), (R_BG, R_SW, "bg", 0), (R_SW, R_GATES, "main", O_QS), (R_GATES, IN_COLS, "main", O_GA)]


def _to_padded_shards(main, bg):
    zeros = jnp.zeros((main.shape[0], IN_PAD - IN_SHARD), main.dtype)
    parts = []
    for q in range(N_CHIPS):
        for a, b, src, s0 in _ORIG_SEGMENTS:
            lo, hi = max(a, q * IN_SHARD), min(b, (q + 1) * IN_SHARD)
            if lo < hi:
                parts.append((main if src == "main" else bg)[:, s0 + lo - a:s0 + hi - a])
        parts.append(zeros)
    return jnp.concatenate(parts, axis=1)


def _lane_row(v16):
    return jnp.pad(v16.reshape(1, 2 * DN_HEADS), ((0, 0), (2 * DN_HEADS, 128 - 4 * DN_HEADS)))


def _pack_small(g):
    pad16 = jnp.pad(g["sinks"], ((0, 0), (0, 128 - SW_HEADS)))
    return jnp.concatenate([g["conv"].reshape(-1, 128), g["ln1g"].reshape(-1, 128), g["ln1b"].reshape(-1, 128),
                            g["ln2g"].reshape(-1, 128), g["ln2b"].reshape(-1, 128), g["gnw"], g["arow"], g["dtrow"],
                            pad16], axis=0)


def _unpack_small(buf):
    nconv = DN_CONV * 3 * D // 128
    o = nconv
    out = dict(conv=buf[:o].reshape(DN_CONV, 3 * D))
    for name in ("ln1g", "ln1b", "ln2g", "ln2b"):
        out[name] = buf[o:o + 8].reshape(D)
        o += 8
    out["gnw"] = buf[o]
    out["a_log"] = buf[o + 1, 2 * DN_HEADS:4 * DN_HEADS].reshape(2, DN_HEADS)
    out["dt_bias"] = buf[o + 2, 2 * DN_HEADS:4 * DN_HEADS].reshape(2, DN_HEADS)
    out["sinks"] = buf[o + 3, :SW_HEADS]
    return out


def kernel(x, w_in, conv_w, a_log, dt_bias, dn_norm_w, sinks, w_branch_a, w_branch_b, w_out, ln1_g, ln1_b, w_gate_up, w_down, ln2_g, ln2_b, loss_target, m_w_in, m_conv_w, m_a_log, m_dt_bias, m_dn_norm_w, m_sinks, m_w_branch_a, m_w_branch_b, m_w_out, m_ln1_g, m_ln1_b, m_w_gate_up, m_w_down, m_ln2_g, m_ln2_b, v_w_in, v_conv_w, v_a_log, v_dt_bias, v_dn_norm_w, v_sinks, v_w_branch_a, v_w_branch_b, v_w_out, v_ln1_g, v_ln1_b, v_w_gate_up, v_w_down, v_ln2_g, v_ln2_b):
    xi, yi, ci = _place()
    myq = 2 * xi + yi
    t_ = x.shape[1]
    cos, sin = _rope_tables(t_)

    loc_in = jnp.pad(w_in.astype(BF16), ((0, 0), (0, 0), (0, IN_PAD - IN_SHARD)))
    locs = [loc_in, w_branch_a.astype(BF16), w_branch_b.astype(BF16), w_out.astype(BF16), w_gate_up.astype(BF16),
            w_down.astype(BF16), conv_w]
    kinds = ["col", "row", "row", "row", "colx", "row", "col"]
    gathers = []
    for l in range(DEPTH):
        srcs = [a[l] for a in locs]
        if gathers:
            srcs[-1] = srcs[-1] + gathers[-1][1][0, 0]
        gathers.append(_gather_start(srcs, kinds, "gather_%d_start" % l))

    def in_weights(l, after):
        _, (full_in,) = _split_wait(tuple(part[:1] for part in gathers[l][0]), after, "gather_%d_wait_in" % l)
        cols = lambda a, b: _orig_cols(full_in, a, b)
        return dict(
            in_main=jnp.concatenate(cols(0, R_BG) + cols(R_GATES, IN_COLS) + cols(R_SW, R_GATES), axis=1),
            in_bg=jnp.pad(jnp.concatenate(cols(R_BG, R_SW), axis=1), ((0, 0), (0, 128 - 4 * DN_HEADS))))

    def rest_weights(l, after):
        _, (full_a, full_b, full_o, full_gu, full_d, full_conv) = _split_wait(
            tuple(part[1:] for part in gathers[l][0]), after, "gather_%d_wait_rest" % l)
        return dict(
            conv=full_conv, arow=_lane_row(a_log[l]), dtrow=_lane_row(dt_bias[l]), gnw=dn_norm_w[l][None],
            sinks=sinks[l][None], a=full_a.reshape(D, D), b=full_b.reshape(D, D), o=full_o.reshape(D, D),
            ln1g=ln1_g[l][None], ln1b=ln1_b[l][None], gu=full_gu, d=full_d.reshape(FFN, D),
            ln2g=ln2_g[l][None], ln2b=ln2_b[l][None])

    h = x[0]
    hm = h.astype(_MXU)
    residuals = []
    for l in range(DEPTH):
        win = in_weights(l, gathers[-1][1] if l == 0 else h)
        h, hm, res = _layer_fwd(h, hm, win, functools.partial(rest_weights, l), cos, sin)
        residuals.append(res)
    dh, sq = _loss_head(h, loss_target[0])
    loss = lax.psum((0.5 / D) * jnp.sum(sq), ("x", "y", "c"))

    big = [None] * DEPTH
    small = [None] * DEPTH
    hop1 = hop2 = None

    def mid(after):
        nonlocal hop1, hop2
        if hop1 is None:
            return None
        handle, tok = _rs_middle(hop1[1], after, RS_META, ci, "rs_chips_%d" % hop1[0])
        hop1, hop2 = None, (hop1[0], handle)
        return tok

    token = None
    for l in reversed(range(DEPTH)):
        dh, g = _layer_bwd(dh, residuals[l], residuals[l]["w"], cos, sin, mid, token)
        if hop2 is not None:
            big[hop2[0]] = _rs_end(hop2[1], dh, RS_META, ci, myq, "rs_chips_%d" % hop2[0])
        g_in = _to_padded_shards(g["in_main"], g["in_bg"])
        handle, token = _rs_sibling_start([g_in, g["a"], g["b"], g["o"], g["gu"], g["d"]], RS_META, "rs_sib_%d_start" % l)
        hop1 = (l, handle)
        small[l] = _pack_small(g)
    tot = _allreduce_small(jnp.concatenate(small, axis=0))
    token = mid(tot)
    pending = hop2
    sm = [_unpack_small(tot[l * SMALL_ROWS:(l + 1) * SMALL_ROWS]) for l in range(DEPTH)]
    stack = lambda name: jnp.stack([s[name] for s in sm], axis=0)
    grads = dict(
        conv_w=lax.dynamic_slice_in_dim(stack("conv"), myq * (3 * D // N_CHIPS), 3 * D // N_CHIPS, axis=2),
        a_log=stack("a_log"), dt_bias=stack("dt_bias"), dn_norm_w=stack("gnw"), sinks=stack("sinks"),
        ln1_g=stack("ln1g"), ln1_b=stack("ln1b"), ln2_g=stack("ln2g"), ln2_b=stack("ln2b"))
    weights = dict(w_in=w_in, conv_w=conv_w, a_log=a_log, dt_bias=dt_bias, dn_norm_w=dn_norm_w, sinks=sinks,
                   w_branch_a=w_branch_a, w_branch_b=w_branch_b, w_out=w_out, ln1_g=ln1_g, ln1_b=ln1_b,
                   w_gate_up=w_gate_up, w_down=w_down, ln2_g=ln2_g, ln2_b=ln2_b)
    ms = dict(w_in=m_w_in, conv_w=m_conv_w, a_log=m_a_log, dt_bias=m_dt_bias, dn_norm_w=m_dn_norm_w, sinks=m_sinks,
              w_branch_a=m_w_branch_a, w_branch_b=m_w_branch_b, w_out=m_w_out, ln1_g=m_ln1_g, ln1_b=m_ln1_b,
              w_gate_up=m_w_gate_up, w_down=m_w_down, ln2_g=m_ln2_g, ln2_b=m_ln2_b)
    vs = dict(w_in=v_w_in, conv_w=v_conv_w, a_log=v_a_log, dt_bias=v_dt_bias, dn_norm_w=v_dn_norm_w, sinks=v_sinks,
              w_branch_a=v_w_branch_a, w_branch_b=v_w_branch_b, w_out=v_w_out, ln1_g=v_ln1_g, ln1_b=v_ln1_b,
              w_gate_up=v_w_gate_up, w_down=v_w_down, ln2_g=v_ln2_g, ln2_b=v_ln2_b)
    names = list(weights)
    upd = {n: _adamw("adamw_" + n, weights[n], grads[n], ms[n], vs[n]) for n in grads}
    big_names = ["w_in", "w_branch_a", "w_branch_b", "w_out", "w_gate_up", "w_down"]
    carry = {n: None for n in big_names}

    def update_layer(l):
        for t, n in enumerate(big_names):
            step = _adamw_layer_t if weights[n].shape[-1] % 128 else _adamw_layer
            carry[n] = step("adamw_" + n, l, weights[n], big[l][t], ms[n], vs[n], carry[n], token)

    for l in range(DEPTH - 1, pending[0], -1):
        update_layer(l)
    big[pending[0]] = _rs_end(pending[1], carry[big_names[-1]][1], RS_META, ci, myq, "rs_chips_%d" % pending[0])
    update_layer(pending[0])
    for n in big_names:
        if weights[n].shape[-1] % 128:
            nl, r, c = weights[n].shape
            carry[n] = tuple(jnp.swapaxes(a.reshape(nl, c, r), 1, 2) for a in carry[n])
        grads[n], upd[n] = carry[n][0], carry[n][1:]
    return (loss, dh[None], *[grads[n] for n in names], *[upd[n][0] for n in names], *[upd[n][1] for n in names],
            *[upd[n][2] for n in names])
```

```python
import functools

import jax
import jax.numpy as jnp
from jax import lax
from jax.experimental import pallas as pl
from jax.experimental.pallas import tpu as pltpu

F32 = jnp.float32
BF16 = jnp.bfloat16
_MXU = BF16

D = 1024
DEPTH = 4
DN_HEADS = 8
DN_DIM = 128
DN_CONV = 5
CHUNK = 64
SW_HEADS = 16
SW_KV = 4
SW_DIM = 64
SW_GRP = SW_HEADS // SW_KV
SW_BLOCK = 128
ROPE_THETA = 10000.0
FFN = 2816
ALPHA = (2.0 * DEPTH) ** 0.25
LN_EPS = 1e-5
RMS_EPS = 1e-6
IN_COLS = 7712
O_Z, O_GA, O_GB, O_QS, O_KS, O_VS, N_MAIN = 3072, 4096, 5120, 6144, 7168, 7424, 7680
R_BG, R_SW, R_GATES = 4096, 4128, 5664
N_CHIPS = 4
IN_SHARD = IN_COLS // N_CHIPS
IN_PAD = 2048
ADAM_LR, ADAM_B1, ADAM_B2, ADAM_EPS, ADAM_WD, ADAM_STEP = 0.001, 0.9, 0.999, 1e-08, 0.01, 10
VMEM_LIMIT = 52 * 1024 * 1024
MESH = pl.DeviceIdType.MESH
ANY = pl.BlockSpec(memory_space=pl.ANY)


def _params(n_grid, **kw):
    return pltpu.CompilerParams(dimension_semantics=("arbitrary",) * n_grid, vmem_limit_bytes=VMEM_LIMIT, **kw)


def _full(a):
    nd = a.ndim
    return pl.BlockSpec(a.shape, lambda *_, nd=nd: (0,) * nd)


def _raw_dot(a, b, ca, cb):
    return lax.dot_general(a.astype(_MXU), b.astype(_MXU), (((ca,), (cb,)), ((), ())), preferred_element_type=F32)


@jax.custom_vjp
def _nn(a, b):
    return _raw_dot(a, b, 1, 0)


@jax.custom_vjp
def _nt(a, b):
    return _raw_dot(a, b, 1, 1)


@jax.custom_vjp
def _tn(a, b):
    if a.shape[1] > b.shape[1]:
        return _raw_dot(b, a, 0, 0).T
    return _raw_dot(a, b, 0, 0)


_nn.defvjp(lambda a, b: (_nn(a, b), (a, b)), lambda r, g: (_nt(g, r[1]), _tn(r[0], g)))
_nt.defvjp(lambda a, b: (_nt(a, b), (a, b)), lambda r, g: (_nn(g, r[1]), _tn(g, r[0])))
_tn.defvjp(lambda a, b: (_tn(a, b), (a, b)), lambda r, g: (_nt(r[1], g), _nn(r[0], g)))


def _hdot(a, b, ca=1, cb=0):
    ah, bh = a.astype(BF16), b.astype(BF16)
    al, bl = (a - ah.astype(F32)).astype(BF16), (b - bh.astype(F32)).astype(BF16)
    dot = lambda u, v: lax.dot_general(u, v, (((ca,), (cb,)), ((), ())), preferred_element_type=F32)
    return dot(ah, bh) + (dot(ah, bl) + dot(al, bh))


def _inv_impl(mats):
    n = mats[0].shape[0]
    eye = (lax.broadcasted_iota(jnp.int32, (n, n), 0) == lax.broadcasted_iota(jnp.int32, (n, n), 1)).astype(F32)
    ps = [-a for a in mats]
    ts = [eye + p for p in ps]
    for _ in range(max(1, (n - 1).bit_length()) - 1):
        ps = [_hdot(p, p) for p in ps]
        ts = [t + _hdot(t, p) for t, p in zip(ts, ps)]
    return tuple(ts)


@jax.custom_vjp
def _inv(mats):
    return _inv_impl(mats)


def _inv_fwd(mats):
    ts = _inv_impl(mats)
    return ts, ts


def _inv_bwd(ts, gs):
    xs = [_hdot(t, g, 0, 0) for t, g in zip(ts, gs)]
    return (tuple(-_hdot(x, t, 1, 1) for x, t in zip(xs, ts)),)


_inv.defvjp(_inv_fwd, _inv_bwd)


@jax.custom_vjp
def _inv_saved(mats, saved):
    return saved


_inv_saved.defvjp(lambda mats, saved: (saved, saved),
                  lambda ts, gs: (_inv_bwd(ts, gs)[0], tuple(jnp.zeros_like(t) for t in ts)))


def _tile(n, cap):
    if n <= cap:
        return n
    best = [t for t in range(128, cap + 1, 128) if n % t == 0]
    assert best, (n, cap)
    return best[-1]


def _mm(a, b, *, name, ta=False, tb=False, add=None, tm=1024, tn=1024, tk=1024, after=None, out_dtype=F32):
    if ta:
        k_, m_ = a.shape
    else:
        m_, k_ = a.shape
    n_ = b.shape[0] if tb else b.shape[1]
    tm, tn, tk = _tile(m_, tm), _tile(n_, tn), _tile(k_, tk)
    nk = k_ // tk
    has_add = add is not None

    def body(*refs):
        a_ref, b_ref = refs[:2]
        add_ref = refs[2] if has_add else None
        o_ref = refs[2 + has_add + (after is not None)]
        part = _raw_dot(a_ref[...], b_ref[...], 0 if ta else 1, 1 if tb else 0)
        if nk == 1:
            o_ref[...] = (part + add_ref[...] if has_add else part).astype(o_ref.dtype)
            return
        acc = refs[-1]
        k = pl.program_id(2)

        @pl.when(k == 0)
        def _():
            acc[...] = part

        @pl.when(jnp.logical_and(k > 0, k < nk - 1))
        def _():
            acc[...] += part

        @pl.when(k == nk - 1)
        def _():
            o_ref[...] = (acc[...] + part + add_ref[...] if has_add else acc[...] + part).astype(o_ref.dtype)

    a_spec = pl.BlockSpec((tk, tm), lambda i, j, k: (k, i)) if ta else pl.BlockSpec((tm, tk), lambda i, j, k: (i, k))
    b_spec = pl.BlockSpec((tn, tk), lambda i, j, k: (j, k)) if tb else pl.BlockSpec((tk, tn), lambda i, j, k: (k, j))
    o_spec = pl.BlockSpec((tm, tn), lambda i, j, k: (i, j))
    in_specs = [a_spec, b_spec] + ([o_spec] if has_add else []) + ([ANY] if after is not None else [])
    args = (a, b) + ((add,) if has_add else ()) + ((after,) if after is not None else ())
    return pl.pallas_call(
        body, name=name, grid=(m_ // tm, n_ // tn, nk), in_specs=in_specs, out_specs=o_spec,
        out_shape=jax.ShapeDtypeStruct((m_, n_), out_dtype),
        scratch_shapes=[pltpu.VMEM((tm, tn), F32)] if nk > 1 else [],
        compiler_params=_params(3))(*args)


def _mm_fused(a, b, post, outs, *, name, rows=(), params=(), n_sums=0, tb=False, tm=1024, tn=1024, tk=1024):
    m_, k_ = a.shape
    n_ = b.shape[0] if tb else b.shape[1]
    tm, tn, tk = _tile(m_, tm), _tile(n_, tn), _tile(k_, tk)
    nk = k_ // tk
    nr, npar, no = len(rows), len(params), len(outs)
    aliased = [o[4] for o in outs if o[4] is not None]
    nin = 2 + nr + npar

    def body(*refs):
        a_ref, b_ref = refs[:2]
        row_refs, par_refs = refs[2:2 + nr], refs[2 + nr:nin]
        out_refs = refs[nin + len(aliased):nin + len(aliased) + no + n_sums]
        part = _raw_dot(a_ref[...], b_ref[...], 1, 1 if tb else 0)
        k = pl.program_id(2)
        first = jnp.logical_and(pl.program_id(0) == 0, pl.program_id(1) == 0)
        if nk > 1:
            acc = refs[-1]

            @pl.when(k == 0)
            def _():
                acc[...] = part

            @pl.when(jnp.logical_and(k > 0, k < nk - 1))
            def _():
                acc[...] += part

        @pl.when(k == nk - 1)
        def _():
            total = acc[...] + part if nk > 1 else part
            res = post(total, *[r[...].astype(F32) for r in row_refs], *[p[...] for p in par_refs])
            for o_ref, val in zip(out_refs[:no], res[:no]):
                o_ref[...] = val.astype(o_ref.dtype)
            for s_ref, val in zip(out_refs[no:], res[no:]):
                @pl.when(first)
                def _(s_ref=s_ref):
                    s_ref[...] = jnp.zeros_like(s_ref)
                s_ref[...] += val

    b_spec = pl.BlockSpec((tn, tk), lambda i, j, k: (j, k)) if tb else pl.BlockSpec((tk, tn), lambda i, j, k: (k, j))
    in_specs = [pl.BlockSpec((tm, tk), lambda i, j, k: (i, k)), b_spec]
    in_specs += [pl.BlockSpec((tm, w), lambda i, j, k, cb=cb: (i, cb(j))) for _, w, cb in rows]
    in_specs += [_full(p) for p in params] + [ANY] * len(aliased)
    out_specs = [pl.BlockSpec((tm, w), lambda i, j, k, cb=cb: (i, cb(j))) for _, w, _, cb, _ in outs]
    out_specs += [_full(p) for p in params[:n_sums]]
    out_shape = [jax.ShapeDtypeStruct((m_, tot), dt) for tot, _, dt, _, _ in outs]
    out_shape += [jax.ShapeDtypeStruct(p.shape, F32) for p in params[:n_sums]]
    aliases, pos = {}, nin
    for oi, o in enumerate(outs):
        if o[4] is not None:
            aliases[pos] = oi
            pos += 1
    return pl.pallas_call(
        body, name=name, grid=(m_ // tm, n_ // tn, nk), in_specs=in_specs, out_specs=out_specs, out_shape=out_shape,
        scratch_shapes=[pltpu.VMEM((tm, tn), F32)] if nk > 1 else [], input_output_aliases=aliases,
        compiler_params=_params(3))(a, b, *[r[0] for r in rows], *params, *aliased)


def _mm_pre(pre, rows, params, b, *, name, tm=512):
    m_ = rows[0][0].shape[0]
    k_, n_ = b.shape
    tm = _tile(m_, tm)
    nr, npar = len(rows), len(params)

    def body(*refs):
        b_ref, a_out, o_ref = refs[nr + npar:]
        a = pre(*[r[...].astype(F32) for r in refs[:nr]], *[p[...] for p in refs[nr:nr + npar]]).astype(a_out.dtype)
        a_out[...] = a
        o_ref[...] = _raw_dot(a, b_ref[...], 1, 0)

    return pl.pallas_call(
        body, name=name, grid=(m_ // tm,),
        in_specs=[pl.BlockSpec((tm, w), lambda i, cb=cb: (i, cb)) for _, w, cb in rows] + [_full(p) for p in params]
        + [_full(b)],
        out_specs=[pl.BlockSpec((tm, k_), lambda i: (i, 0)), pl.BlockSpec((tm, n_), lambda i: (i, 0))],
        out_shape=[jax.ShapeDtypeStruct((m_, k_), _MXU), jax.ShapeDtypeStruct((m_, n_), F32)],
        compiler_params=_params(1))(*[r[0] for r in rows], *params, b)


def _row_spec(tb, w, c0, percol):
    return pl.BlockSpec((tb, w), lambda i, j, c0=c0, pc=percol: (i, c0 + (j if pc else 0)))


def _stage_fwd(f, name, rows, params, outs, tb, ncol=1):
    t_ = rows[0][0].shape[0]
    nr, npar = len(rows), len(params)

    def body(*refs):
        res = f(*[r[...].astype(F32) for r in refs[:nr + npar]])
        for o_ref, val in zip(refs[nr + npar:], res):
            o_ref[...] = val.astype(o_ref.dtype)

    return pl.pallas_call(
        body, name=name, grid=(t_ // tb, ncol),
        in_specs=[_row_spec(tb, w, c0, pc) for (_, w, c0, pc) in rows] + [_full(p) for p in params],
        out_specs=[pl.BlockSpec((tb, w), lambda i, j: (i, j)) for w, _ in outs],
        out_shape=[jax.ShapeDtypeStruct((t_, w * ncol), dt) for w, dt in outs],
        compiler_params=_params(2))(*[r[0] for r in rows], *params)


def _into(dest, tb, width, t_, ncol, dtype):
    if dest is None:
        return pl.BlockSpec((tb, width), lambda i, j: (i, j)), jax.ShapeDtypeStruct((t_, width * ncol), dtype), None
    buf, total, c0 = dest
    return (pl.BlockSpec((tb, width), lambda i, j, c0=c0: (i, c0 + j)), jax.ShapeDtypeStruct((t_, total), dtype), buf)


def _stage_bwd(f, name, rows, params, douts, tb, ncol=1, cat=None, dtypes=None, dest=None, after=None):
    t_ = rows[0][0].shape[0]
    nr, npar, nd = len(rows), len(params), len(douts)
    cat = cat if cat is not None else [[r] for r in range(nr)]
    dtypes = dtypes if dtypes is not None else [F32] * len(cat)
    dest = dest or {}
    assert ncol == 1 or all(len(g) == 1 and rows[g[0]][3] for g in cat)
    nin = nr + npar + nd
    unread = [after] if after is not None else []

    def body(*refs):
        ins = [r[...].astype(F32) for r in refs[:nr + npar]]
        dvals = tuple(r[...].astype(F32) for r in refs[nr + npar:nin])
        out_refs = refs[nin + len(aliased) + len(unread):]
        _, vjp = jax.vjp(f, *ins)
        grads = vjp(dvals)
        for o_ref, grp in zip(out_refs[:len(cat)], cat):
            val = grads[grp[0]] if len(grp) == 1 else jnp.concatenate([grads[r] for r in grp], axis=-1)
            o_ref[...] = val.astype(o_ref.dtype)
        first = jnp.logical_and(pl.program_id(0) == 0, pl.program_id(1) == 0)
        for p_ref, gp in zip(out_refs[len(cat):], grads[nr:]):
            @pl.when(first)
            def _(p_ref=p_ref):
                p_ref[...] = jnp.zeros_like(p_ref)
            p_ref[...] += gp

    gw = [sum(rows[r][1] for r in grp) for grp in cat]
    out_specs, out_shape, aliased, aliases = [], [], [], {}
    for gi, (w, dt) in enumerate(zip(gw, dtypes)):
        spec, shape, buf = _into(dest.get(gi), tb, w, t_, ncol, dt)
        out_specs.append(spec)
        out_shape.append(shape)
        if buf is not None:
            aliases[nin + len(aliased)] = gi
            aliased.append(buf)
    return pl.pallas_call(
        body, name=name, grid=(t_ // tb, ncol),
        in_specs=[_row_spec(tb, w, c0, pc) for (_, w, c0, pc) in rows] + [_full(p) for p in params]
        + [pl.BlockSpec((tb, d.shape[1] // ncol), lambda i, j: (i, j)) for d in douts]
        + [ANY] * (len(aliased) + len(unread)),
        out_specs=out_specs + [_full(p) for p in params],
        out_shape=out_shape + [jax.ShapeDtypeStruct(p.shape, F32) for p in params],
        input_output_aliases=aliases,
        compiler_params=_params(2))(*[r[0] for r in rows], *params, *douts, *aliased, *unread)


def _ln_f(x, y, g, b):
    u = ALPHA * x + y
    c = u - jnp.mean(u, axis=-1, keepdims=True)
    var = jnp.mean(c * c, axis=-1, keepdims=True)
    return (c * lax.rsqrt(var + LN_EPS) * g + b,)


def _ln_f2(x, y, g, b):
    out, = _ln_f(x, y, g, b)
    return out, out


def _swiglu_tile(gu):
    half = gu.shape[1] // 2
    return (jax.nn.silu(gu[:, :half]) * gu[:, half:],)


def _merge_f(ya, yb, ga, gb):
    return (jax.nn.sigmoid(ga) * ya + jax.nn.sigmoid(gb) * yb,)


def _gnorm_f(of, ob, z, w):
    o = of + ob
    return (o * lax.rsqrt(jnp.mean(o * o, axis=-1, keepdims=True) + RMS_EPS) * w * jax.nn.silu(z),)


def _bg_f(x, arow, dtrow):
    lane = lax.broadcasted_iota(jnp.int32, x.shape, 1)
    beta = jax.nn.sigmoid(x)
    g = -jnp.exp(arow) * jax.nn.softplus(x + dtrow)
    return (jnp.where(lane < 16, beta, jnp.where(lane < 32, g, 0.0)),)


PREP_ROWS = 512
PAD = 8


def _prep_f(part, w, *wins):
    xc = wins[0] * w[0:1, :]
    for k in range(1, DN_CONV):
        xc = xc + wins[k] * w[k:k + 1, :]
    a = jax.nn.silu(xc)
    nrm = a * lax.rsqrt(jnp.sum(a * a, axis=-1, keepdims=True) + RMS_EPS)
    return jnp.where(part == 0, nrm * (DN_DIM ** -0.5), jnp.where(part == 1, nrm, a))


def _windows(pad_ref, r0, rows):
    return [pad_ref[PAD + r0 - 2 + k:PAD + r0 - 2 + k + rows, :] for k in range(DN_CONV)]


def _prep_fwd(pm, conv):
    t_ = pm.shape[0]
    rows = min(PREP_ROWS, t_)

    def body(x_ref, w_ref, o_ref, pad_ref):
        part = pl.program_id(0) // DN_HEADS
        pad_ref[0:PAD, :] = jnp.zeros((PAD, DN_DIM), F32)
        pad_ref[PAD + t_:2 * PAD + t_, :] = jnp.zeros((PAD, DN_DIM), F32)
        pad_ref[PAD:PAD + t_, :] = x_ref[...]
        w = w_ref[...]
        for r in range(t_ // rows):
            o_ref[r * rows:(r + 1) * rows, :] = _prep_f(part, w, *_windows(pad_ref, r * rows, rows))

    ncb = 3 * DN_HEADS
    return pl.pallas_call(
        body, name="prep_fwd", grid=(ncb,),
        in_specs=[pl.BlockSpec((t_, DN_DIM), lambda j: (0, j)), pl.BlockSpec((DN_CONV, DN_DIM), lambda j: (0, j))],
        out_specs=pl.BlockSpec((t_, DN_DIM), lambda j: (0, j)),
        out_shape=jax.ShapeDtypeStruct((t_, ncb * DN_DIM), F32),
        scratch_shapes=[pltpu.VMEM((t_ + 2 * PAD, DN_DIM), F32)],
        compiler_params=_params(1))(pm, conv)


def _prep_bwd(pm, conv, dout, dpm):
    t_ = pm.shape[0]
    rows = min(PREP_ROWS, t_)

    def body(x_ref, w_ref, d_ref, _, dx_ref, dw_ref, pad_ref, dpad_ref):
        part = pl.program_id(0) // DN_HEADS
        pad_ref[0:PAD, :] = jnp.zeros((PAD, DN_DIM), F32)
        pad_ref[PAD + t_:2 * PAD + t_, :] = jnp.zeros((PAD, DN_DIM), F32)
        pad_ref[PAD:PAD + t_, :] = x_ref[...]
        dpad_ref[...] = jnp.zeros_like(dpad_ref)
        w = w_ref[...]
        dw = jnp.zeros((DN_CONV, DN_DIM), F32)
        for r in range(t_ // rows):
            r0 = r * rows
            _, vjp = jax.vjp(functools.partial(_prep_f, part), w, *_windows(pad_ref, r0, rows))
            grads = vjp(d_ref[r0:r0 + rows, :])
            dw = dw + grads[0]
            for k in range(DN_CONV):
                lo = PAD + r0 - 2 + k
                dpad_ref[lo:lo + rows, :] += grads[1 + k]
        dx_ref[...] = dpad_ref[PAD:PAD + t_, :].astype(dx_ref.dtype)
        dw_ref[...] = dw

    ncb = 3 * DN_HEADS
    col = pl.BlockSpec((t_, DN_DIM), lambda j: (0, j))
    wsp = pl.BlockSpec((DN_CONV, DN_DIM), lambda j: (0, j))
    return pl.pallas_call(
        body, name="prep_bwd", grid=(ncb,), in_specs=[col, wsp, col, ANY], out_specs=[col, wsp],
        out_shape=[jax.ShapeDtypeStruct(dpm.shape, dpm.dtype), jax.ShapeDtypeStruct((DN_CONV, ncb * DN_DIM), F32)],
        scratch_shapes=[pltpu.VMEM((t_ + 2 * PAD, DN_DIM), F32), pltpu.VMEM((t_ + 2 * PAD, DN_DIM), F32)],
        input_output_aliases={3: 0}, compiler_params=_params(1))(pm, conv, dout, dpm)


def _dn_chunk(sgns, qs, ks, vs, grows, brows, tsaved=None, with_t=False):
    c = qs[0].shape[0]
    i = lax.broadcasted_iota(jnp.int32, (c, c), 0)
    j = lax.broadcasted_iota(jnp.int32, (c, c), 1)
    eye = i == j
    incl = {s: (i - j) * int(s) >= 0 for s in set(sgns)}
    strict = {s: (i - j) * int(s) > 0 for s in set(sgns)}
    gcs = [jnp.sum(jnp.where(incl[s], g, 0.0), axis=1, keepdims=True) for s, g in zip(sgns, grows)]
    grs = [jnp.sum(jnp.where(eye, gc, 0.0), axis=0, keepdims=True) for gc in gcs]
    bcs = [jnp.sum(jnp.where(eye, b, 0.0), axis=1, keepdims=True) for b in brows]
    gls = [jnp.sum(g, axis=1, keepdims=True) for g in grows]
    decs = [jnp.exp(jnp.where(incl[s], gc - gr, -1e30)) for s, gc, gr in zip(sgns, gcs, grs)]
    kks = [_nt(k, k) for k in ks]
    amats = tuple(jnp.where(strict[s], bc * kk * dec, 0.0) for s, bc, kk, dec in zip(sgns, bcs, kks, decs))
    tinvs = _inv(amats) if tsaved is None else _inv_saved(amats, tsaved)
    egcs = [jnp.exp(gc) for gc in gcs]
    us = [_nn(t, v * bc) for t, v, bc in zip(tinvs, vs, bcs)]
    ws = [_nn(t, k * (bc * egc)) for t, k, bc, egc in zip(tinvs, ks, bcs, egcs)]
    qks = [_nt(q, k) * dec for q, k, dec in zip(qs, ks, decs)]
    qds = [q * egc for q, egc in zip(qs, egcs)]
    kds = [k * jnp.exp(gl - gc) for k, gl, gc in zip(ks, gls, gcs)]
    res = tuple(us), tuple(ws), tuple(qks), tuple(qds), tuple(kds)
    return res + (tinvs,) if with_t else res


def _dn_step(us, ws, qks, qds, kds, grows, ss):
    gls = [jnp.exp(jnp.sum(g, axis=1, keepdims=True)) for g in grows]
    wss = [_nn(w, s) for w, s in zip(ws, ss)]
    qss = [_nn(qd, s) for qd, s in zip(qds, ss)]
    vns = [u - x for u, x in zip(us, wss)]
    os_ = [a + _nn(qk, vn) for a, qk, vn in zip(qss, qks, vns)]
    s2s = [s * gl + _tn(kd, vn) for s, gl, kd, vn in zip(ss, gls, kds, vns)]
    return tuple(os_), tuple(s2s)


def _hs(h):
    return slice(h * DN_DIM, (h + 1) * DN_DIM)


_DIR_SGN = (1, -1)
_A_OUT = 5
_PROBLEMS = [(d, h) for d in range(2) for h in range(DN_HEADS)]
_SGNS = [_DIR_SGN[d] for d, _ in _PROBLEMS]


def _chunk_inputs(q_ref, k_ref, v_ref, g_ref, b_ref):
    heads = lambda ref: tuple(ref[:, _hs(h)].astype(F32) for _, h in _PROBLEMS)
    rows = lambda ref: tuple(ref[d, 0, h:h + 1, :] for d, h in _PROBLEMS)
    return heads(q_ref), heads(k_ref), heads(v_ref), rows(g_ref), rows(b_ref)


def _dn_a_fwd(qkv, grows, brows):
    t_ = qkv.shape[0]
    nch = t_ // CHUNK

    def body(q_ref, k_ref, v_ref, g_ref, b_ref, *outs):
        us, ws, qks, qds, kds, tinvs = _dn_chunk(_SGNS, *_chunk_inputs(q_ref, k_ref, v_ref, g_ref, b_ref), with_t=True)
        for p, (d, h) in enumerate(_PROBLEMS):
            u_ref, w_ref, qk_ref, qd_ref, kd_ref = outs[d * _A_OUT:(d + 1) * _A_OUT]
            u_ref[:, _hs(h)], w_ref[:, _hs(h)], qk_ref[0, h] = us[p], ws[p].astype(_MXU), qks[p].astype(_MXU)
            qd_ref[:, _hs(h)], kd_ref[:, _hs(h)] = qds[p].astype(_MXU), kds[p].astype(_MXU)
            outs[2 * _A_OUT + d][0, h] = tinvs[p]

    rspec = pl.BlockSpec((2, 1, DN_HEADS, CHUNK), lambda c: (0, c, 0, 0))
    big = pl.BlockSpec((CHUNK, D), lambda c: (c, 0))
    qks = pl.BlockSpec((1, DN_HEADS, CHUNK, CHUNK), lambda c: (c, 0, 0, 0))
    bigs = lambda dt: jax.ShapeDtypeStruct((t_, D), dt)
    qksh = lambda dt: jax.ShapeDtypeStruct((nch, DN_HEADS, CHUNK, CHUNK), dt)
    res = pl.pallas_call(
        body, name="dn_a_fwd", grid=(nch,),
        in_specs=[pl.BlockSpec((CHUNK, D), lambda c, p=p: (c, p)) for p in range(3)] + [rspec, rspec],
        out_specs=[big, big, qks, big, big] * 2 + [qks, qks],
        out_shape=[bigs(F32), bigs(_MXU), qksh(_MXU), bigs(_MXU), bigs(_MXU)] * 2 + [qksh(F32)] * 2,
        compiler_params=_params(1))(qkv, qkv, qkv, grows, brows)
    return res[:2 * _A_OUT], res[2 * _A_OUT:]


def _dn_a_bwd(qkv, grows, brows, tinv, dres, dg_b):
    t_ = qkv.shape[0]
    nch = t_ // CHUNK

    def body(q_ref, k_ref, v_ref, g_ref, b_ref, tf_ref, tb_ref, *rest):
        dins, dgb_ref, (dqkv_ref, dg_ref, db_ref) = rest[:2 * _A_OUT], rest[2 * _A_OUT], rest[2 * _A_OUT + 1:]
        tsaved = tuple((tf_ref, tb_ref)[d][0, h] for d, h in _PROBLEMS)
        _, vjp = jax.vjp(functools.partial(_dn_chunk, _SGNS, tsaved=tsaved),
                         *_chunk_inputs(q_ref, k_ref, v_ref, g_ref, b_ref))
        cots = []
        for o in range(_A_OUT):
            cots.append(tuple(dins[d * _A_OUT + o][0, h] if o == 2 else dins[d * _A_OUT + o][:, _hs(h)]
                              for d, h in _PROBLEMS))
        gq, gk, gv, gg, gb = vjp(tuple(cots))
        for p, (d, h) in enumerate(_PROBLEMS):
            dg_ref[d, 0, h:h + 1, :] = gg[p] + dgb_ref[d, 0, h:h + 1, :]
            db_ref[d, 0, h:h + 1, :] = gb[p]
        for h in range(DN_HEADS):
            dqkv_ref[:, _hs(h)] = gq[h] + gq[DN_HEADS + h]
            dqkv_ref[:, _hs(DN_HEADS + h)] = gk[h] + gk[DN_HEADS + h]
            dqkv_ref[:, _hs(2 * DN_HEADS + h)] = gv[h] + gv[DN_HEADS + h]

    rspec = pl.BlockSpec((2, 1, DN_HEADS, CHUNK), lambda c: (0, c, 0, 0))
    big = pl.BlockSpec((CHUNK, D), lambda c: (c, 0))
    qks = pl.BlockSpec((1, DN_HEADS, CHUNK, CHUNK), lambda c: (c, 0, 0, 0))
    rsh = jax.ShapeDtypeStruct(grows.shape, F32)
    return pl.pallas_call(
        body, name="dn_a_bwd", grid=(nch,),
        in_specs=[pl.BlockSpec((CHUNK, D), lambda c, p=p: (c, p)) for p in range(3)] + [rspec, rspec, qks, qks]
        + [big, big, qks, big, big] * 2 + [rspec],
        out_specs=[pl.BlockSpec((CHUNK, 3 * D), lambda c: (c, 0)), rspec, rspec],
        out_shape=[jax.ShapeDtypeStruct((t_, 3 * D), F32), rsh, rsh],
        compiler_params=_params(1))(qkv, qkv, qkv, grows, brows, *tinv, *dres, dg_b)


def _dir_specs(nch):
    def cidx(d):
        return (lambda n: n) if d == 0 else (lambda n: nch - 1 - n)
    out = []
    for d in range(2):
        ci = cidx(d)
        big = pl.BlockSpec((CHUNK, D), lambda n, ci=ci: (ci(n), 0))
        qks = pl.BlockSpec((1, DN_HEADS, CHUNK, CHUNK), lambda n, ci=ci: (ci(n), 0, 0, 0))
        row = pl.BlockSpec((1, 1, DN_HEADS, CHUNK), lambda n, ci=ci, d=d: (d, ci(n), 0, 0))
        st = pl.BlockSpec((1, DN_HEADS, DN_DIM, DN_DIM), lambda n, ci=ci: (ci(n), 0, 0, 0))
        out.append(dict(big=big, qk=qks, row=row, st=st))
    return out


def _step_inputs(ins, per_dir):
    def pick(o):
        if o == 2:
            return tuple(ins[d * per_dir + o][0, h].astype(F32) for d, h in _PROBLEMS)
        if o == 5:
            return tuple(ins[d * per_dir + o][0, 0, h:h + 1, :] for d, h in _PROBLEMS)
        return tuple(ins[d * per_dir + o][:, _hs(h)].astype(F32) for d, h in _PROBLEMS)
    return [pick(o) for o in range(6)]


def _dn_b_fwd(ares, grows):
    t_ = ares[0].shape[0]
    nch = t_ // CHUNK
    sp = _dir_specs(nch)

    def body(*refs):
        ins, outs, s_ref = refs[:12], refs[12:16], refs[16]

        @pl.when(pl.program_id(0) == 0)
        def _():
            s_ref[...] = jnp.zeros_like(s_ref)

        ss = tuple(s_ref[p] for p in range(len(_PROBLEMS)))
        os_, s2s = _dn_step(*_step_inputs(ins, 6), ss)
        for p, (d, h) in enumerate(_PROBLEMS):
            outs[2 + d][0, h] = ss[p]
            outs[d][:, _hs(h)] = os_[p]
            s_ref[p] = s2s[p]

    in_specs, args = [], []
    for d in range(2):
        in_specs += [sp[d]["big"], sp[d]["big"], sp[d]["qk"], sp[d]["big"], sp[d]["big"], sp[d]["row"]]
        args += list(ares[d * _A_OUT:(d + 1) * _A_OUT]) + [grows]
    stsh = jax.ShapeDtypeStruct((nch, DN_HEADS, DN_DIM, DN_DIM), F32)
    osh = jax.ShapeDtypeStruct((t_, D), F32)
    return pl.pallas_call(
        body, name="dn_b_fwd", grid=(nch,), in_specs=in_specs,
        out_specs=[sp[0]["big"], sp[1]["big"], sp[0]["st"], sp[1]["st"]], out_shape=[osh, osh, stsh, stsh],
        scratch_shapes=[pltpu.VMEM((2 * DN_HEADS, DN_DIM, DN_DIM), F32)],
        compiler_params=_params(1))(*args)


def _dn_b_bwd(ares, grows, st_f, st_b, do):
    t_ = ares[0].shape[0]
    nch = t_ // CHUNK
    sp = _dir_specs(nch)
    rsp = [sp[1], sp[0]]

    def body(*refs):
        ins, outs, ds_ref = refs[:16], refs[16:28], refs[28]

        @pl.when(pl.program_id(0) == 0)
        def _():
            ds_ref[...] = jnp.zeros_like(ds_ref)

        ss = tuple(ins[d * 8 + 6][0, h] for d, h in _PROBLEMS)
        _, vjp = jax.vjp(_dn_step, *_step_inputs(ins, 8), ss)
        dos = tuple(ins[d * 8 + 7][:, _hs(h)] for d, h in _PROBLEMS)
        grads = vjp((dos, tuple(ds_ref[p] for p in range(len(_PROBLEMS)))))
        for p, (d, h) in enumerate(_PROBLEMS):
            du_ref, dw_ref, dqk_ref, dqd_ref, dkd_ref, dg_ref = outs[d * 6:(d + 1) * 6]
            du_ref[:, _hs(h)], dw_ref[:, _hs(h)], dqk_ref[0, h] = grads[0][p], grads[1][p], grads[2][p]
            dqd_ref[:, _hs(h)], dkd_ref[:, _hs(h)] = grads[3][p], grads[4][p]
            dg_ref[0, 0, h:h + 1, :] = grads[5][p]
            ds_ref[p] = grads[6][p]

    in_specs, args, out_specs, out_shape = [], [], [], []
    big_sh = jax.ShapeDtypeStruct((t_, D), F32)
    qk_sh = jax.ShapeDtypeStruct((nch, DN_HEADS, CHUNK, CHUNK), F32)
    row_sh = jax.ShapeDtypeStruct((1, nch, DN_HEADS, CHUNK), F32)
    for d in range(2):
        s = rsp[d]
        row0 = pl.BlockSpec((1, 1, DN_HEADS, CHUNK), lambda m, d=d: (0, (nch - 1 - m) if d == 0 else m, 0, 0))
        rowd = pl.BlockSpec((1, 1, DN_HEADS, CHUNK), lambda m, d=d: (d, (nch - 1 - m) if d == 0 else m, 0, 0))
        in_specs += [s["big"], s["big"], s["qk"], s["big"], s["big"], rowd, s["st"], s["big"]]
        args += list(ares[d * _A_OUT:(d + 1) * _A_OUT]) + [grows, (st_f, st_b)[d], do]
        out_specs += [s["big"], s["big"], s["qk"], s["big"], s["big"], row0]
        out_shape += [big_sh, big_sh, qk_sh, big_sh, big_sh, row_sh]
    res = pl.pallas_call(
        body, name="dn_b_bwd", grid=(nch,), in_specs=in_specs, out_specs=out_specs, out_shape=out_shape,
        scratch_shapes=[pltpu.VMEM((2 * DN_HEADS, DN_DIM, DN_DIM), F32)],
        compiler_params=_params(1))(*args)
    dares = list(res[0:5]) + list(res[6:11])
    return dares, jnp.concatenate([res[5], res[11]], axis=0)


@jax.custom_vjp
def _rot_half(x):
    half, width = SW_DIM // 2, x.shape[1]
    first = lax.broadcasted_iota(jnp.int32, x.shape, 1) % SW_DIM < half
    return jnp.where(first, -pltpu.roll(x, width - half, axis=1), pltpu.roll(x, half, axis=1))


_rot_half.defvjp(lambda x: (_rot_half(x), None), lambda _, g: (-_rot_half(g),))


def _rope(x, c, s):
    reps = x.shape[1] // c.shape[1]
    return x * jnp.tile(c, (1, reps)) + _rot_half(x) * jnp.tile(s, (1, reps))


def _rope_t(g, c, s):
    reps = g.shape[1] // c.shape[1]
    return g * jnp.tile(c, (1, reps)) - _rot_half(g * jnp.tile(s, (1, reps)))


_SW_SCALE = SW_DIM ** -0.5
_KV_HEADS = [[kvh * SW_GRP + g for g in range(SW_GRP)] for kvh in range(SW_KV)]


def _by_group(x):
    return [jnp.concatenate([x[:, h * SW_DIM:(h + 1) * SW_DIM] for h in hs], axis=0) for hs in _KV_HEADS]


def _from_groups(xs):
    return jnp.concatenate([x[g * SW_BLOCK:(g + 1) * SW_BLOCK] for x in xs for g in range(SW_GRP)], axis=-1)


def _attn_probs(blk, t_, cq, sq, ck, sk, q, kall, sinks):
    qgs = _by_group(_rope(q, cq, sq))
    kr = _rope(kall, ck, sk)
    khs = [kr[:, kvh * SW_DIM:(kvh + 1) * SW_DIM] for kvh in range(SW_KV)]
    nq, nk = SW_GRP * SW_BLOCK, 3 * SW_BLOCK
    qpos = lax.broadcasted_iota(jnp.int32, (nq, nk), 0) % SW_BLOCK
    krel = lax.broadcasted_iota(jnp.int32, (nq, nk), 1) - SW_BLOCK
    kglob = krel + blk * SW_BLOCK
    valid = (jnp.abs(qpos - krel) <= SW_BLOCK) & (kglob >= 0) & (kglob < t_)
    ss = [jnp.where(valid, _nt(qg * _SW_SCALE, kh), -1e30) for qg, kh in zip(qgs, khs)]
    snks = [jnp.concatenate([jnp.broadcast_to(sinks[:, h:h + 1], (SW_BLOCK, 1)) for h in hs], axis=0) for hs in _KV_HEADS]
    ms = [jnp.maximum(jnp.max(s, axis=-1, keepdims=True), snk) for s, snk in zip(ss, snks)]
    es = [jnp.exp(s - m) for s, m in zip(ss, ms)]
    esnks = [jnp.exp(snk - m) for snk, m in zip(snks, ms)]
    invs = [1.0 / (jnp.sum(e, axis=-1, keepdims=True) + esnk) for e, esnk in zip(es, esnks)]
    ps = [e * inv for e, inv in zip(es, invs)]
    return qgs, khs, ps, [esnk * inv for esnk, inv in zip(esnks, invs)]


def _attn_f(blk, t_, cq, sq, ck, sk, q, kp, ko, kn, vp, vo, vn, sinks):
    _, _, ps, psinks = _attn_probs(blk, t_, cq, sq, ck, sk, q, jnp.concatenate([kp, ko, kn], axis=0), sinks)
    vall = jnp.concatenate([vp, vo, vn], axis=0)
    ps = [p.astype(_MXU) for p in ps]
    kept = [jnp.concatenate([p, jnp.broadcast_to(psink.astype(_MXU), (p.shape[0], 128))], axis=-1)
            for p, psink in zip(ps, psinks)]
    return _from_groups([_nn(p, vall[:, kvh * SW_DIM:(kvh + 1) * SW_DIM]) for kvh, p in enumerate(ps)]), kept


def _attn_bwd_f(cq, sq, ck, sk, q, kp, ko, kn, vp, vo, vn, do, kept):
    nk = 3 * SW_BLOCK
    ps = [x[:, :nk] for x in kept]
    psinks = [x[:, nk:nk + 1].astype(F32) for x in kept]
    qgs = _by_group(_rope(q, cq, sq))
    kr = _rope(jnp.concatenate([kp, ko, kn], axis=0), ck, sk)
    khs = [kr[:, kvh * SW_DIM:(kvh + 1) * SW_DIM] for kvh in range(SW_KV)]
    vall = jnp.concatenate([vp, vo, vn], axis=0)
    vhs = [vall[:, kvh * SW_DIM:(kvh + 1) * SW_DIM] for kvh in range(SW_KV)]
    dogs = _by_group(do)
    dvs = [_tn(p, dog) for p, dog in zip(ps, dogs)]
    ps = [p.astype(F32) for p in ps]
    dps = [_nt(dog * _SW_SCALE, vh) for dog, vh in zip(dogs, vhs)]
    deltas = [jnp.sum(p * dp, axis=-1, keepdims=True) for p, dp in zip(ps, dps)]
    dss = [p * (dp - delta) for p, dp, delta in zip(ps, dps, deltas)]
    dqr = _from_groups([_nn(ds, kh) for ds, kh in zip(dss, khs)])
    dkr = jnp.concatenate([_tn(ds, qg) for ds, qg in zip(dss, qgs)], axis=-1)
    dsnk = [-(psink * delta) * (1.0 / _SW_SCALE) for psink, delta in zip(psinks, deltas)]
    dsinks = jnp.concatenate([jnp.sum(d[g * SW_BLOCK:(g + 1) * SW_BLOCK], axis=0, keepdims=True)
                              for d in dsnk for g in range(SW_GRP)], axis=1)
    dq, dk, dv = _rope_t(dqr, cq, sq), _rope_t(dkr, ck, sk), jnp.concatenate(dvs, axis=-1)
    blocks = lambda a: [a[j * SW_BLOCK:(j + 1) * SW_BLOCK] for j in range(3)]
    return [dq] + blocks(dk) + blocks(dv) + [dsinks]


def _attn_specs(nb):
    prv = lambda i: jnp.maximum(i - 1, 0)
    nxt = lambda i: jnp.minimum(i + 1, nb - 1)
    rows = [lambda i: i, prv, lambda i: i, nxt]
    tab = [pl.BlockSpec((SW_BLOCK, 128), lambda i, r=r: (r(i), 0)) for r in rows]
    qs = pl.BlockSpec((SW_BLOCK, SW_HEADS * SW_DIM), lambda i: (i, O_QS // (SW_HEADS * SW_DIM)))
    kw = SW_KV * SW_DIM
    ks = [pl.BlockSpec((SW_BLOCK, kw), lambda i, r=r: (r(i), O_KS // kw)) for r in rows[1:]]
    vs = [pl.BlockSpec((SW_BLOCK, kw), lambda i, r=r: (r(i), O_VS // kw)) for r in rows[1:]]
    return tab, qs, ks, vs


def _attn_tables(refs):
    cq, cp, co, cn, sq, sp_, so, sn = [r[...] for r in refs]
    return cq, sq, jnp.concatenate([cp, co, cn], axis=0), jnp.concatenate([sp_, so, sn], axis=0)


_P_BLOCK = (None, SW_KV, SW_GRP * SW_BLOCK, 3 * SW_BLOCK + 128)


def _attn_fwd(pm, cos, sin, sinks):
    t_ = pm.shape[0]
    nb = t_ // SW_BLOCK
    tab, qs, ks, vs = _attn_specs(nb)

    def body(*refs):
        tabs = _attn_tables(refs[:8])
        vals = [r[...] for r in refs[8:16]]
        out, ps = _attn_f(pl.program_id(0), t_, *tabs, *vals)
        refs[16][...] = out.astype(refs[16].dtype)
        for kvh, p in enumerate(ps):
            refs[17][kvh] = p

    return pl.pallas_call(
        body, name="attn_fwd", grid=(nb,), in_specs=tab + tab + [qs] + ks + vs + [_full(sinks)],
        out_specs=[pl.BlockSpec((SW_BLOCK, D), lambda i: (i, 0)), pl.BlockSpec(_P_BLOCK, lambda i: (i, 0, 0, 0))],
        out_shape=[jax.ShapeDtypeStruct((t_, D), _MXU), jax.ShapeDtypeStruct((nb,) + _P_BLOCK[1:], _MXU)],
        compiler_params=_params(1))(*([cos] * 4), *([sin] * 4), pm, pm, pm, pm, pm, pm, pm, sinks)


def _attn_bwd(pm, cos, sin, sinks, probs, do, dpm):
    t_ = pm.shape[0]
    nb = t_ // SW_BLOCK
    tab, qs, ks, vs = _attn_specs(nb)
    kw = SW_KV * SW_DIM

    def body(*refs):
        tabs = _attn_tables(refs[:8])
        vals = [r[...] for r in refs[8:15]]
        p_ref, do_ref, outs = refs[16], refs[17], refs[19:]
        grads = _attn_bwd_f(*tabs, *vals, do_ref[...], [p_ref[kvh] for kvh in range(SW_KV)])
        for o_ref, g in zip(outs[:7], grads[:7]):
            o_ref[...] = g.astype(o_ref.dtype)

        @pl.when(pl.program_id(0) == 0)
        def _():
            outs[7][...] = jnp.zeros_like(outs[7])
        outs[7][...] += grads[7]

    own = lambda w: pl.BlockSpec((SW_BLOCK, w), lambda i: (i, 0))
    return pl.pallas_call(
        body, name="attn_bwd", grid=(nb,),
        in_specs=tab + tab + [qs] + ks + vs + [_full(sinks), pl.BlockSpec(_P_BLOCK, lambda i: (i, 0, 0, 0)), own(D), ANY],
        out_specs=[pl.BlockSpec((SW_BLOCK, D), lambda i: (i, O_QS // D))] + [own(kw)] * 6 + [_full(sinks)],
        out_shape=[jax.ShapeDtypeStruct(dpm.shape, dpm.dtype)] + [jax.ShapeDtypeStruct((t_, kw), F32)] * 6
        + [jax.ShapeDtypeStruct(sinks.shape, F32)],
        input_output_aliases={18: 0},
        compiler_params=_params(1))(*([cos] * 4), *([sin] * 4), pm, pm, pm, pm, pm, pm, pm, sinks, probs, do, dpm)


def _band_sum(kparts, vparts, dpm):
    t_, kw = kparts[1].shape
    nb = t_ // SW_BLOCK

    def body(kp, ko, kn, vp, vo, vn, _, out_ref):
        j = pl.program_id(0)
        band = lambda p, o, n: o[...] + jnp.where(j + 1 < nb, p[...], 0.0) + jnp.where(j > 0, n[...], 0.0)
        out_ref[...] = jnp.concatenate([band(kp, ko, kn), band(vp, vo, vn)], axis=-1).astype(out_ref.dtype)

    specs = [pl.BlockSpec((SW_BLOCK, kw), lambda j: (jnp.minimum(j + 1, nb - 1), 0)),
             pl.BlockSpec((SW_BLOCK, kw), lambda j: (j, 0)),
             pl.BlockSpec((SW_BLOCK, kw), lambda j: (jnp.maximum(j - 1, 0), 0))]
    return pl.pallas_call(
        body, name="band_sum", grid=(nb,), in_specs=specs * 2 + [ANY],
        out_specs=pl.BlockSpec((SW_BLOCK, 2 * kw), lambda j: (j, O_KS // (2 * kw))),
        out_shape=jax.ShapeDtypeStruct(dpm.shape, dpm.dtype), input_output_aliases={6: 0},
        compiler_params=_params(1))(*kparts, *vparts, dpm)


def _loss_head(y, target, tb=256):
    t_ = y.shape[0]

    def body(y_ref, t_ref, dy_ref, acc_ref):
        err = y_ref[...] - t_ref[...]
        dy_ref[...] = err * (1.0 / D)
        sq = (err * err).reshape(tb // 8, 8, D).sum(axis=0)
        part = sq[:, 0:128]
        for c in range(1, D // 128):
            part = part + sq[:, c * 128:(c + 1) * 128]

        @pl.when(pl.program_id(0) == 0)
        def _():
            acc_ref[...] = jnp.zeros_like(acc_ref)
        acc_ref[...] += part

    row = pl.BlockSpec((tb, D), lambda i: (i, 0))
    return pl.pallas_call(
        body, name="loss_head", grid=(t_ // tb,), in_specs=[row, row],
        out_specs=[row, pl.BlockSpec((8, 128), lambda i: (0, 0))],
        out_shape=[jax.ShapeDtypeStruct((t_, D), F32), jax.ShapeDtypeStruct((8, 128), F32)],
        compiler_params=_params(1))(y, target)


def _adam_math(w, g, m, v):
    bc1 = 1.0 - ADAM_B1 ** ADAM_STEP
    bc2 = 1.0 - ADAM_B2 ** ADAM_STEP
    nm = ADAM_B1 * m + (1.0 - ADAM_B1) * g
    nv = ADAM_B2 * v + (1.0 - ADAM_B2) * (g * g)
    return -ADAM_LR * ((nm / bc1) / (jnp.sqrt(nv / bc2) + ADAM_EPS) + ADAM_WD * w), nm, nv


def _adamw_layer(name, l, w, g, m, v, carry, after):
    nl, r, c = w.shape
    g2 = g.reshape(r, -1)
    gc = g2.shape[1]
    w2, m2, v2 = [a.reshape(nl * r, c) for a in (w, m, v)]
    tb = r
    while tb * gc * 4 > (1 << 20) and tb % 16 == 0:
        tb //= 2
    nb = r // tb

    def body(w_ref, g_ref, m_ref, v_ref, *rest):
        go_ref, d_ref, nm_ref, nv_ref = rest[-4:]
        gv = g_ref[...][:, :c]
        go_ref[...] = gv
        d_ref[...], nm_ref[...], nv_ref[...] = _adam_math(w_ref[...], gv, m_ref[...], v_ref[...])

    spec = pl.BlockSpec((tb, c), lambda i: (l * nb + i, 0))
    carried = list(carry) if carry is not None else []
    outs = pl.pallas_call(
        body, name=name, grid=(nb,),
        in_specs=[spec, pl.BlockSpec((tb, gc), lambda i: (i, 0)), spec, spec] + [ANY] * (len(carried) + 1),
        out_specs=[spec] * 4, out_shape=[jax.ShapeDtypeStruct((nl * r, c), F32)] * 4,
        input_output_aliases={4 + k: k for k in range(len(carried))},
        compiler_params=_params(1))(w2, g2, m2, v2, *[a.reshape(nl * r, c) for a in carried], after)
    return tuple(o.reshape(w.shape) for o in outs)


def _adamw_layer_t(name, l, w, g, m, v, carry, after):
    nl, r, c = w.shape
    g2 = g.reshape(r, -1)
    gc = g2.shape[1]
    wt, mt, vt = [jnp.swapaxes(a, 1, 2).reshape(nl * c, r) for a in (w, m, v)]

    def body(w_ref, g_ref, m_ref, v_ref, *rest):
        go_ref, d_ref, nm_ref, nv_ref = rest[-4:]
        gv = g_ref[...].T[:c]
        go_ref[...] = gv
        d_ref[...], nm_ref[...], nv_ref[...] = _adam_math(w_ref[...], gv, m_ref[...], v_ref[...])

    spec = pl.BlockSpec((c, 128), lambda j: (l, j))
    carried = list(carry) if carry is not None else []
    return tuple(pl.pallas_call(
        body, name=name, grid=(r // 128,),
        in_specs=[spec, pl.BlockSpec((128, gc), lambda j: (j, 0)), spec, spec] + [ANY] * (len(carried) + 1),
        out_specs=[spec] * 4, out_shape=[jax.ShapeDtypeStruct((nl * c, r), F32)] * 4,
        input_output_aliases={4 + k: k for k in range(len(carried))},
        compiler_params=_params(1))(wt, g2, mt, vt, *carried, after))


def _adamw(name, w, g, m, v):
    shape = w.shape
    cols = shape[-1]
    rows = w.size // cols
    w2, g2, m2, v2 = [a.reshape(rows, cols) for a in (w, g, m, v)]
    tb = rows
    while tb * cols * 4 > (1 << 20) and tb % 16 == 0:
        tb //= 2

    def body(w_ref, g_ref, m_ref, v_ref, d_ref, nm_ref, nv_ref):
        d_ref[...], nm_ref[...], nv_ref[...] = _adam_math(w_ref[...], g_ref[...], m_ref[...], v_ref[...])

    spec = pl.BlockSpec((tb, cols), lambda i: (i, 0))
    sh = jax.ShapeDtypeStruct((rows, cols), F32)
    outs = pl.pallas_call(body, name=name, grid=(rows // tb,), in_specs=[spec] * 4, out_specs=[spec] * 3,
                          out_shape=[sh] * 3, compiler_params=_params(1))(w2, g2, m2, v2)
    return [o.reshape(shape) for o in outs]


def _to_rows(bg, col0):
    t_ = bg.shape[0]
    a = bg[:, col0:col0 + 2 * DN_HEADS].reshape(t_ // CHUNK, CHUNK, 2, DN_HEADS)
    return jnp.transpose(a, (2, 0, 3, 1))


def _from_rows(db, dg):
    nch = db.shape[1]
    back = lambda a: jnp.transpose(a, (1, 3, 0, 2)).reshape(nch * CHUNK, 2 * DN_HEADS)
    return jnp.pad(jnp.concatenate([back(db), back(dg)], axis=1), ((0, 0), (0, 128 - 4 * DN_HEADS)))


def _layer_fwd(x, xm, w, rest, cos, sin):
    t_ = x.shape[0]
    nb = min(1024, t_)
    pm = _mm(xm, w["in_main"], name="mm_in", tn=1536)
    pbg = _mm(xm, w["in_bg"], name="mm_in_bg")
    w = {**w, **rest(pbg)}
    qkv = _prep_fwd(pm, w["conv"])
    bg, = _stage_fwd(_bg_f, "bg_fwd", [(pbg, 128, 0, False)], [w["arow"], w["dtrow"]], [(128, F32)], nb)
    brows, grows = _to_rows(bg, 0), _to_rows(bg, 2 * DN_HEADS)
    ares, tinv = _dn_a_fwd(qkv, grows, brows)
    o_f, o_b, st_f, st_b = _dn_b_fwd(ares, grows)

    def gnorm_all(of, ob, z, gnw):
        return jnp.concatenate([_gnorm_f(of[:, _hs(h)], ob[:, _hs(h)], z[:, _hs(h)], gnw)[0]
                                for h in range(DN_HEADS)], axis=-1)

    odn, ya = _mm_pre(gnorm_all, [(o_f, D, 0), (o_b, D, 0), (pm, D, O_Z // D)], [w["gnw"]], w["a"], name="mm_a_gnorm")
    osw, probs = _attn_fwd(pm, cos, sin, w["sinks"])
    same, first = (lambda j: j), (lambda j: 0)
    full = lambda dt: (D, D, dt, first, None)
    yb, merged = _mm_fused(
        osw, w["b"], lambda acc, ya_, ga, gb: (acc,) + _merge_f(ya_, acc, ga, gb), [full(F32), full(_MXU)],
        name="mm_b_merge", rows=[(ya, D, first), (pm, D, lambda j: O_GA // D), (pm, D, lambda j: O_GB // D)], tm=512)
    mix, x1, x1m = _mm_fused(
        merged, w["o"], lambda acc, x_, g_, b_: (acc,) + _ln_f2(x_, acc, g_, b_), [full(F32), full(F32), full(_MXU)],
        name="mm_o_ln", rows=[(x, D, first)], params=[w["ln1g"], w["ln1b"]], tm=512)
    gu, hid = _mm_fused(x1m, w["gu"], lambda acc: (acc,) + _swiglu_tile(acc),
                        [(2 * FFN, FFN, F32, same, None), (FFN, FFN // 2, _MXU, same, None)], name="mm_gu_swiglu", tn=FFN)
    ffn, x2, x2m = _mm_fused(
        hid, w["d"], lambda acc, x_, g_, b_: (acc,) + _ln_f2(x_, acc, g_, b_), [full(F32), full(F32), full(_MXU)],
        name="mm_d_ln", rows=[(x1, D, first)], params=[w["ln2g"], w["ln2b"]], tm=512, tk=FFN)
    res = dict(w=w, x=x, xm=xm, pm=pm, pbg=pbg, qkv=qkv, grows=grows, brows=brows, ares=ares, tinv=tinv, o_f=o_f, o_b=o_b, st_f=st_f,
               st_b=st_b, odn=odn, osw=osw, probs=probs, ya=ya, yb=yb, merged=merged, mix=mix, x1=x1, x1m=x1m, gu=gu, hid=hid,
               ffn=ffn)
    return x2, x2m, res


def _layer_bwd(dx2, r, w, cos, sin, mid=None, after=None):
    t_ = dx2.shape[0]
    nb = min(1024, t_)
    pm = r["pm"]
    g = {}
    dx1a, dffn, g["ln2g"], g["ln2b"] = _stage_bwd(
        _ln_f, "ln2_bwd", [(r["x1"], D, 0, False), (r["ffn"], D, 0, False)], [w["ln2g"], w["ln2b"]], [dx2],
        min(512, t_), dtypes=[F32, _MXU], after=after)
    same = lambda j: j
    dgu, = _mm_fused(dffn, w["d"], lambda dhid, gu: jax.vjp(_swiglu_tile, gu)[1]((dhid,)),
                     [(2 * FFN, FFN, _MXU, same, None)], name="mm_d_dx_swiglu", rows=[(r["gu"], FFN, same)], tb=True,
                     tn=FFN // 2)
    g["d"] = _mm(r["hid"], dffn, ta=True, name="mm_d_dw", tm=FFN // 2, out_dtype=_MXU)
    first = lambda j: 0
    full = lambda dt: (D, D, dt, first, None)

    def ln1_back(acc, dx1a_, x_, mix_, g_, b_):
        dx_, dmix_, dg_, db_ = jax.vjp(_ln_f, x_, mix_, g_, b_)[1]((acc + dx1a_,))
        return dx_, dmix_, dg_, db_

    dxa, dmix, g["ln1g"], g["ln1b"] = _mm_fused(
        dgu, w["gu"], ln1_back, [full(F32), full(_MXU)], name="mm_gu_dx_ln", tb=True, tm=256, tk=2 * FFN, n_sums=2,
        rows=[(dx1a, D, first), (r["x"], D, first), (r["mix"], D, first)], params=[w["ln1g"], w["ln1b"]])
    g["gu"] = _mm(r["x1m"], dgu, ta=True, name="mm_gu_dw", tn=FFN // 2, out_dtype=_MXU)
    tok = mid(g["gu"]) if mid is not None else None
    g["o"] = _mm(r["merged"], dmix, ta=True, name="mm_o_dw", after=tok, out_dtype=_MXU)

    def merge_back(acc, ya_, yb_, ga, gb):
        dya_, dyb_, dga, dgb = jax.vjp(_merge_f, ya_, yb_, ga, gb)[1]((acc,))
        return dya_, dyb_, jnp.concatenate([dga, dgb], axis=-1)

    dya, dyb, dpm = _mm_fused(
        dmix, w["o"], merge_back, [full(_MXU), full(_MXU), (N_MAIN, 2 * D, _MXU, lambda j: O_GA // (2 * D), None)],
        name="mm_o_dx_merge", tb=True, tm=512,
        rows=[(r["ya"], D, first), (r["yb"], D, first), (pm, D, lambda j: O_GA // D), (pm, D, lambda j: O_GB // D)])

    def gnorm_back(acc, of, ob, z, gnw):
        dos, dzs, dws = [], [], None
        for h in range(DN_HEADS):
            do_h, _, dz_h, dw_h = jax.vjp(_gnorm_f, of[:, _hs(h)], ob[:, _hs(h)], z[:, _hs(h)], gnw)[1]((acc[:, _hs(h)],))
            dos.append(do_h)
            dzs.append(dz_h)
            dws = dw_h if dws is None else dws + dw_h
        return jnp.concatenate(dos, axis=-1), jnp.concatenate(dzs, axis=-1), dws

    dof, dpm, g["gnw"] = _mm_fused(
        dya, w["a"], gnorm_back, [full(F32), (N_MAIN, D, _MXU, lambda j: O_Z // D, dpm)], name="mm_a_dx_gnorm", tb=True,
        tm=512, rows=[(r["o_f"], D, first), (r["o_b"], D, first), (pm, D, lambda j: O_Z // D)], params=[w["gnw"]],
        n_sums=1)
    g["a"] = _mm(r["odn"], dya, ta=True, name="mm_a_dw", out_dtype=_MXU)
    dosw = _mm(dyb, w["b"], tb=True, name="mm_b_dx")
    g["b"] = _mm(r["osw"], dyb, ta=True, name="mm_b_dw", out_dtype=_MXU)
    ab = _attn_bwd(pm, cos, sin, w["sinks"], r["probs"], dosw, dpm)
    dpm, g["sinks"] = ab[0], ab[7]
    dpm = _band_sum(ab[1:4], ab[4:7], dpm)
    dares, dg_b = _dn_b_bwd(r["ares"], r["grows"], r["st_f"], r["st_b"], dof)
    dqkv, dgrows, dbrows = _dn_a_bwd(r["qkv"], r["grows"], r["brows"], r["tinv"], dares, dg_b)
    dbg = _from_rows(dbrows, dgrows)
    dpbg, g["arow"], g["dtrow"] = _stage_bwd(_bg_f, "bg_bwd", [(r["pbg"], 128, 0, False)], [w["arow"], w["dtrow"]],
                                             [dbg], nb)
    dpm, g["conv"] = _prep_bwd(pm, w["conv"], dqkv, dpm)
    dx = _mm(dpm, w["in_main"], tb=True, add=dxa, name="mm_in_dx", tk=1920)
    dx = _mm(dpbg, w["in_bg"], tb=True, add=dx, name="mm_in_bg_dx")
    g["in_main"] = _mm(r["xm"], dpm, ta=True, name="mm_in_dw", tn=1536, out_dtype=_MXU)
    g["in_bg"] = _mm(r["xm"], dpbg, ta=True, name="mm_in_bg_dw", out_dtype=_MXU)
    return dx, g


def _place():
    return lax.axis_index("x"), lax.axis_index("y"), lax.axis_index("c")


def _colblock(kind, q):
    return q if kind == "col" else (q >> 1) | ((q & 1) << 1)


def _other_chips(x, y):
    return [(1 - x, y), (x, 1 - y), (1 - x, 1 - y)]


HBM = pl.BlockSpec(memory_space=pltpu.HBM)
SEM = pl.BlockSpec(memory_space=pltpu.SEMAPHORE)
DATAFLOW = pltpu.SideEffectType.DATAFLOW_SIDE_EFFECTING


def _hbm(a):
    return pltpu.HBM(a.shape, a.dtype)


def _gather_start(locs, kinds, name):
    n = len(locs)
    locs = list(locs)
    lands = [lax.empty((N_CHIPS,) + a.shape if kind == "row" else (a.shape[0], N_CHIPS * a.shape[1]), a.dtype)
             for a, kind in zip(locs, kinds)]

    def body(*refs):
        loc_refs, land_refs = refs[:n], refs[n:2 * n]
        send_sems, recv_sems, token = refs[2 * n:3 * n], refs[3 * n:4 * n], refs[-1]
        x, y, c = _place()
        myq = 2 * x + y
        for t in range(n):
            width = locs[t].shape[1]
            mine = (land_refs[t].at[myq] if kinds[t] == "row" else
                    land_refs[t].at[:, pl.ds(pl.multiple_of(_colblock(kinds[t], myq) * width, 128), width)])
            for dev in [(cx, cy, c) for cx, cy in _other_chips(x, y)] + [(x, y, 1 - c)]:
                pltpu.make_async_remote_copy(src_ref=loc_refs[t], dst_ref=mine, send_sem=send_sems[t],
                                             recv_sem=recv_sems[t], device_id=dev, device_id_type=MESH).start()
        token[...] = jnp.zeros_like(token)

    res = pl.pallas_call(
        body, name=name,
        out_shape=[pltpu.SemaphoreType.DMA(())] * (2 * n) + [_hbm(a) for a in locs + lands]
        + [jax.ShapeDtypeStruct((8, 128), F32)],
        in_specs=[HBM] * (2 * n), out_specs=[SEM] * (2 * n) + [HBM] * (2 * n) + [pl.BlockSpec(memory_space=pltpu.VMEM)],
        input_output_aliases={t: 2 * n + t for t in range(2 * n)},
        compiler_params=pltpu.CompilerParams(has_side_effects=DATAFLOW),
    )(*[pltpu.with_memory_space_constraint(a, pltpu.HBM) for a in locs + lands])
    return (res[:n], res[n:2 * n], res[2 * n:3 * n], res[3 * n:4 * n]), res[-1]


def _split_wait(handle, after, name):
    send_sems, recv_sems, srcs, lands = handle
    n = len(srcs)

    def body(*refs):
        land_refs, ssems, rsems = refs[n:2 * n], refs[2 * n:3 * n], refs[3 * n:4 * n]
        x, y, c = _place()
        for t in range(n):
            done = pltpu.make_async_remote_copy(
                src_ref=land_refs[t], dst_ref=land_refs[t], send_sem=ssems[t], recv_sem=rsems[t],
                device_id=(x, y, c), device_id_type=MESH)
            done.wait_send()
            done.wait_recv()

    res = pl.pallas_call(
        body, name=name, out_shape=[_hbm(a) for a in list(srcs) + list(lands)],
        in_specs=[HBM] * (2 * n) + [SEM] * (2 * n) + [ANY], out_specs=[HBM] * (2 * n),
        input_output_aliases={t: t for t in range(2 * n)},
        compiler_params=pltpu.CompilerParams(has_side_effects=DATAFLOW),
    )(*srcs, *lands, *send_sems, *recv_sems, after)
    return res[:n], res[n:]


RS_CHUNKS = 2


def _piece(ref, kind, q, hf, pr, pc):
    if kind == "row":
        return ref.at[pl.ds((2 * q + hf) * pr, pr), :]
    return ref.at[pl.ds(hf * pr, pr), pl.ds(pl.multiple_of(_colblock(kind, q) * pc, 128), pc)]


def _rs_sibling_start(ts, meta, name):
    n = len(ts)
    ts = list(ts)
    lands = [lax.empty((N_CHIPS, pr, pc), a.dtype) for a, (_, pr, pc) in zip(ts, meta)]

    def body(*refs):
        t_refs, land_refs = refs[:n], refs[n:2 * n]
        send_sems, recv_sems, token = refs[2 * n:3 * n], refs[3 * n:4 * n], refs[-1]
        x, y, c = _place()
        for t, (kind, pr, pc) in enumerate(meta):
            for q in range(N_CHIPS):
                pltpu.make_async_remote_copy(
                    src_ref=_piece(t_refs[t], kind, q, 1 - c, pr, pc), dst_ref=land_refs[t].at[q],
                    send_sem=send_sems[t], recv_sem=recv_sems[t], device_id=(x, y, 1 - c), device_id_type=MESH).start()
        token[...] = jnp.zeros_like(token)

    res = pl.pallas_call(
        body, name=name,
        out_shape=[pltpu.SemaphoreType.DMA(())] * (2 * n) + [_hbm(a) for a in ts + lands]
        + [jax.ShapeDtypeStruct((8, 128), F32)],
        in_specs=[HBM] * (2 * n), out_specs=[SEM] * (2 * n) + [HBM] * (2 * n) + [pl.BlockSpec(memory_space=pltpu.VMEM)],
        input_output_aliases={t: 2 * n + t for t in range(2 * n)},
        compiler_params=pltpu.CompilerParams(has_side_effects=DATAFLOW),
    )(*[pltpu.with_memory_space_constraint(a, pltpu.HBM) for a in ts + lands])
    return (res[:n], res[n:2 * n], res[2 * n:3 * n], res[3 * n:4 * n]), res[-1]


def _rs_add_sibling(ts, r1s, meta, c):
    n = len(ts)
    in_specs, out_specs, out_shape = [], [], []
    for kind, pr, pc in meta:
        rs = pr // RS_CHUNKS
        if kind == "row":
            in_specs.append(pl.BlockSpec((rs, pc), lambda q, r, c_ref: ((2 * q + c_ref[0]) * RS_CHUNKS + r, 0)))
        else:
            in_specs.append(pl.BlockSpec(
                (rs, pc), lambda q, r, c_ref, kind=kind: (c_ref[0] * RS_CHUNKS + r, _colblock(kind, q))))
    for kind, pr, pc in meta:
        sp = pl.BlockSpec((None, pr // RS_CHUNKS, pc), lambda q, r, c_ref: (q, r, 0))
        in_specs.append(sp)
        out_specs.append(sp)
        out_shape.append(jax.ShapeDtypeStruct((N_CHIPS, pr, pc), BF16))

    def body(c_ref, *refs):
        for t in range(n):
            refs[2 * n + t][...] = (refs[t][...].astype(F32) + refs[n + t][...].astype(F32)).astype(BF16)

    return pl.pallas_call(
        body, name="rs_add_sibling", out_shape=out_shape,
        grid_spec=pltpu.PrefetchScalarGridSpec(num_scalar_prefetch=1, grid=(N_CHIPS, RS_CHUNKS), in_specs=in_specs,
                                               out_specs=out_specs),
        compiler_params=_params(2))(c.reshape(1).astype(jnp.int32), *ts, *r1s)


def _rs_chips_start(ps, meta, name):
    n = len(ps)
    ps = list(ps)
    lands = [lax.empty((N_CHIPS - 1, pr, pc), p.dtype) for p, (_, pr, pc) in zip(ps, meta)]

    def body(*refs):
        p_refs, land_refs = refs[:n], refs[n:2 * n]
        send_sems, recv_sems, token = refs[2 * n:3 * n], refs[3 * n:4 * n], refs[-1]
        x, y, c = _place()
        for t in range(n):
            for j, (cx, cy) in enumerate(_other_chips(x, y)):
                pltpu.make_async_remote_copy(
                    src_ref=p_refs[t].at[2 * cx + cy], dst_ref=land_refs[t].at[j], send_sem=send_sems[t],
                    recv_sem=recv_sems[t], device_id=(cx, cy, c), device_id_type=MESH).start()
        token[...] = jnp.zeros_like(token)

    res = pl.pallas_call(
        body, name=name,
        out_shape=[pltpu.SemaphoreType.DMA(())] * (2 * n) + [_hbm(a) for a in ps + lands]
        + [jax.ShapeDtypeStruct((8, 128), F32)],
        in_specs=[HBM] * (2 * n), out_specs=[SEM] * (2 * n) + [HBM] * (2 * n) + [pl.BlockSpec(memory_space=pltpu.VMEM)],
        input_output_aliases={t: 2 * n + t for t in range(2 * n)},
        compiler_params=pltpu.CompilerParams(has_side_effects=DATAFLOW),
    )(*[pltpu.with_memory_space_constraint(a, pltpu.HBM) for a in ps + lands])
    return (res[:n], res[n:2 * n], res[2 * n:3 * n], res[3 * n:4 * n]), res[-1]


def _rs_add_chips(ps, r2s, meta, myq, c):
    n = len(ps)
    in_specs, out_specs, out_shape = [], [], []
    for _, pr, pc in meta:
        in_specs.append(pl.BlockSpec((None, pr // RS_CHUNKS, pc), lambda r, q_ref, c_ref: (q_ref[0], r, 0)))
    for _, pr, pc in meta:
        in_specs.append(pl.BlockSpec((N_CHIPS - 1, pr // RS_CHUNKS, pc), lambda r, q_ref, c_ref: (0, r, 0)))
        out_specs.append(pl.BlockSpec((None, pr // RS_CHUNKS, pc), lambda r, q_ref, c_ref: (c_ref[0], r, 0)))
        out_shape.append(jax.ShapeDtypeStruct((2, pr, pc), F32))

    def body(q_ref, c_ref, *refs):
        for t in range(n):
            r2 = refs[n + t]
            own = refs[t][...].astype(F32)
            refs[2 * n + t][...] = ((own + r2[0].astype(F32)) + r2[1].astype(F32)) + r2[2].astype(F32)

    return pl.pallas_call(
        body, name="rs_add_chips", out_shape=out_shape,
        grid_spec=pltpu.PrefetchScalarGridSpec(num_scalar_prefetch=2, grid=(RS_CHUNKS,), in_specs=in_specs,
                                               out_specs=out_specs),
        compiler_params=_params(1))(myq.reshape(1).astype(jnp.int32), c.reshape(1).astype(jnp.int32), *ps, *r2s)


def _rs_share_halves(gs):
    n = len(gs)

    def body(*refs):
        g_refs, send_sems, recv_sems = refs[n:2 * n], refs[2 * n], refs[2 * n + 1]
        x, y, c = _place()
        sib = (x, y, 1 - c)
        for t in range(n):
            pltpu.make_async_remote_copy(
                src_ref=g_refs[t].at[c], dst_ref=g_refs[t].at[c], send_sem=send_sems.at[t], recv_sem=recv_sems.at[t],
                device_id=sib, device_id_type=MESH).start()
        for t in range(n):
            cp = pltpu.make_async_remote_copy(
                src_ref=g_refs[t].at[c], dst_ref=g_refs[t].at[1 - c], send_sem=send_sems.at[t],
                recv_sem=recv_sems.at[t], device_id=sib, device_id_type=MESH)
            cp.wait_send()
            cp.wait_recv()

    return pl.pallas_call(
        body, name="rs_share_halves", in_specs=[ANY] * n, out_specs=[ANY] * n,
        out_shape=[jax.ShapeDtypeStruct(g.shape, g.dtype) for g in gs], input_output_aliases={t: t for t in range(n)},
        scratch_shapes=[pltpu.SemaphoreType.DMA((n,)), pltpu.SemaphoreType.DMA((n,))],
    )(*gs)


def _rs_middle(handle, after, meta, c, name):
    ts, r1s = _split_wait(handle, after, name + "_sib_wait")
    ps = _rs_add_sibling(ts, r1s, meta, c)
    return _rs_chips_start(ps, meta, name + "_start")


def _rs_end(handle, after, meta, c, myq, name):
    ps, r2s = _split_wait(handle, after, name + "_wait")
    return _rs_share_halves(_rs_add_chips(ps, r2s, meta, myq, c))


def _allreduce_small(buf):
    rows = buf.shape[0]
    ndev = 8

    def body(b_ref, o_ref, slots, send_sems, recv_sems):
        x, y, c = _place()
        me = 4 * x + 2 * y + c
        slots[me] = b_ref[...]
        for k in range(1, ndev):
            kx, ky, kc = (k >> 2) & 1, (k >> 1) & 1, k & 1
            peer = (x ^ kx, y ^ ky, c ^ kc)
            pltpu.make_async_remote_copy(
                src_ref=b_ref, dst_ref=slots.at[me], send_sem=send_sems.at[k - 1], recv_sem=recv_sems.at[k - 1],
                device_id=peer, device_id_type=MESH).start()
        for k in range(1, ndev):
            kx, ky, kc = (k >> 2) & 1, (k >> 1) & 1, k & 1
            cp = pltpu.make_async_remote_copy(
                src_ref=b_ref, dst_ref=slots.at[me ^ k], send_sem=send_sems.at[k - 1], recv_sem=recv_sems.at[k - 1],
                device_id=(x ^ kx, y ^ ky, c ^ kc), device_id_type=MESH)
            cp.wait_send()
            cp.wait_recv()
        acc = slots[0]
        for s in range(1, ndev):
            acc = acc + slots[s]
        o_ref[...] = acc

    vm = pl.BlockSpec(memory_space=pltpu.VMEM)
    return pl.pallas_call(
        body, name="allreduce_small", in_specs=[vm], out_specs=vm, out_shape=jax.ShapeDtypeStruct((rows, 128), F32),
        scratch_shapes=[pltpu.VMEM((ndev, rows, 128), F32), pltpu.SemaphoreType.DMA((ndev - 1,)),
                        pltpu.SemaphoreType.DMA((ndev - 1,))],
        compiler_params=pltpu.CompilerParams(vmem_limit_bytes=VMEM_LIMIT))(buf)


RS_META = [("col", D // 2, IN_PAD), ("row", D // 8, D), ("row", D // 8, D), ("row", D // 8, D),
           ("colx", D // 2, 2 * FFN // N_CHIPS), ("row", FFN // 8, D)]
SMALL_ROWS = 156


def _rope_tables(t_):
    half = SW_DIM // 2
    inv_freq = ROPE_THETA ** (-jnp.arange(half, dtype=F32) / half)
    ang = jnp.arange(t_, dtype=F32)[:, None] * inv_freq[None, :]
    reps = 128 // half
    return jnp.concatenate([jnp.cos(ang)] * reps, axis=1), jnp.concatenate([jnp.sin(ang)] * reps, axis=1)


def _orig_cols(padded, a, b):
    out = []
    for q in range(N_CHIPS):
        lo, hi = max(a, q * IN_SHARD), min(b, (q + 1) * IN_SHARD)
        if lo < hi:
            out.append(padded[:, q * IN_PAD + lo - q * IN_SHARD:q * IN_PAD + hi - q * IN_SHARD])
    return out


_ORIG_SEGMENTS = [(0, R_BG, "main", 0), (R_BG, R_SW, "bg", 0), (R_SW, R_GATES, "main", O_QS), (R_GATES, IN_COLS, "main", O_GA)]


def _to_padded_shards(main, bg):
    zeros = jnp.zeros((main.shape[0], IN_PAD - IN_SHARD), main.dtype)
    parts = []
    for q in range(N_CHIPS):
        for a, b, src, s0 in _ORIG_SEGMENTS:
            lo, hi = max(a, q * IN_SHARD), min(b, (q + 1) * IN_SHARD)
            if lo < hi:
                parts.append((main if src == "main" else bg)[:, s0 + lo - a:s0 + hi - a])
        parts.append(zeros)
    return jnp.concatenate(parts, axis=1)


def _lane_row(v16):
    return jnp.pad(v16.reshape(1, 2 * DN_HEADS), ((0, 0), (2 * DN_HEADS, 128 - 4 * DN_HEADS)))


def _pack_small(g):
    pad16 = jnp.pad(g["sinks"], ((0, 0), (0, 128 - SW_HEADS)))
    return jnp.concatenate([g["conv"].reshape(-1, 128), g["ln1g"].reshape(-1, 128), g["ln1b"].reshape(-1, 128),
                            g["ln2g"].reshape(-1, 128), g["ln2b"].reshape(-1, 128), g["gnw"], g["arow"], g["dtrow"],
                            pad16], axis=0)


def _unpack_small(buf):
    nconv = DN_CONV * 3 * D // 128
    o = nconv
    out = dict(conv=buf[:o].reshape(DN_CONV, 3 * D))
    for name in ("ln1g", "ln1b", "ln2g", "ln2b"):
        out[name] = buf[o:o + 8].reshape(D)
        o += 8
    out["gnw"] = buf[o]
    out["a_log"] = buf[o + 1, 2 * DN_HEADS:4 * DN_HEADS].reshape(2, DN_HEADS)
    out["dt_bias"] = buf[o + 2, 2 * DN_HEADS:4 * DN_HEADS].reshape(2, DN_HEADS)
    out["sinks"] = buf[o + 3, :SW_HEADS]
    return out


def kernel(x, w_in, conv_w, a_log, dt_bias, dn_norm_w, sinks, w_branch_a, w_branch_b, w_out, ln1_g, ln1_b, w_gate_up, w_down, ln2_g, ln2_b, loss_target, m_w_in, m_conv_w, m_a_log, m_dt_bias, m_dn_norm_w, m_sinks, m_w_branch_a, m_w_branch_b, m_w_out, m_ln1_g, m_ln1_b, m_w_gate_up, m_w_down, m_ln2_g, m_ln2_b, v_w_in, v_conv_w, v_a_log, v_dt_bias, v_dn_norm_w, v_sinks, v_w_branch_a, v_w_branch_b, v_w_out, v_ln1_g, v_ln1_b, v_w_gate_up, v_w_down, v_ln2_g, v_ln2_b):
    xi, yi, ci = _place()
    myq = 2 * xi + yi
    t_ = x.shape[1]
    cos, sin = _rope_tables(t_)

    loc_in = jnp.pad(w_in.astype(BF16), ((0, 0), (0, 0), (0, IN_PAD - IN_SHARD)))
    locs = [loc_in, w_branch_a.astype(BF16), w_branch_b.astype(BF16), w_out.astype(BF16), w_gate_up.astype(BF16),
            w_down.astype(BF16), conv_w]
    kinds = ["col", "row", "row", "row", "colx", "row", "col"]
    gathers = []
    for l in range(DEPTH):
        srcs = [a[l] for a in locs]
        if gathers:
            srcs[-1] = srcs[-1] + gathers[-1][1][0, 0]
        gathers.append(_gather_start(srcs, kinds, "gather_%d_start" % l))

    def in_weights(l, after):
        _, (full_in,) = _split_wait(tuple(part[:1] for part in gathers[l][0]), after, "gather_%d_wait_in" % l)
        cols = lambda a, b: _orig_cols(full_in, a, b)
        return dict(
            in_main=jnp.concatenate(cols(0, R_BG) + cols(R_GATES, IN_COLS) + cols(R_SW, R_GATES), axis=1),
            in_bg=jnp.pad(jnp.concatenate(cols(R_BG, R_SW), axis=1), ((0, 0), (0, 128 - 4 * DN_HEADS))))

    def rest_weights(l, after):
        _, (full_a, full_b, full_o, full_gu, full_d, full_conv) = _split_wait(
            tuple(part[1:] for part in gathers[l][0]), after, "gather_%d_wait_rest" % l)
        return dict(
            conv=full_conv, arow=_lane_row(a_log[l]), dtrow=_lane_row(dt_bias[l]), gnw=dn_norm_w[l][None],
            sinks=sinks[l][None], a=full_a.reshape(D, D), b=full_b.reshape(D, D), o=full_o.reshape(D, D),
            ln1g=ln1_g[l][None], ln1b=ln1_b[l][None], gu=full_gu, d=full_d.reshape(FFN, D),
            ln2g=ln2_g[l][None], ln2b=ln2_b[l][None])

    h = x[0]
    hm = h.astype(_MXU)
    residuals = []
    for l in range(DEPTH):
        win = in_weights(l, gathers[-1][1] if l == 0 else h)
        h, hm, res = _layer_fwd(h, hm, win, functools.partial(rest_weights, l), cos, sin)
        residuals.append(res)
    dh, sq = _loss_head(h, loss_target[0])
    loss_rows = jnp.pad(((0.5 / D) * jnp.sum(sq)).reshape(1, 1), ((0, 7), (0, 127)))

    big = [None] * DEPTH
    small = [None] * DEPTH
    hop1 = hop2 = None

    def mid(after):
        nonlocal hop1, hop2
        if hop1 is None:
            return None
        handle, tok = _rs_middle(hop1[1], after, RS_META, ci, "rs_chips_%d" % hop1[0])
        hop1, hop2 = None, (hop1[0], handle)
        return tok

    token = None
    for l in reversed(range(DEPTH)):
        dh, g = _layer_bwd(dh, residuals[l], residuals[l]["w"], cos, sin, mid, token)
        if hop2 is not None:
            big[hop2[0]] = _rs_end(hop2[1], dh, RS_META, ci, myq, "rs_chips_%d" % hop2[0])
        g_in = _to_padded_shards(g["in_main"], g["in_bg"])
        handle, token = _rs_sibling_start([g_in, g["a"], g["b"], g["o"], g["gu"], g["d"]], RS_META, "rs_sib_%d_start" % l)
        hop1 = (l, handle)
        small[l] = _pack_small(g)
    tot = _allreduce_small(jnp.concatenate(small + [loss_rows], axis=0))
    loss = tot[DEPTH * SMALL_ROWS, 0]
    token = mid(tot)
    pending = hop2
    sm = [_unpack_small(tot[l * SMALL_ROWS:(l + 1) * SMALL_ROWS]) for l in range(DEPTH)]
    stack = lambda name: jnp.stack([s[name] for s in sm], axis=0)
    grads = dict(
        conv_w=lax.dynamic_slice_in_dim(stack("conv"), myq * (3 * D // N_CHIPS), 3 * D // N_CHIPS, axis=2),
        a_log=stack("a_log"), dt_bias=stack("dt_bias"), dn_norm_w=stack("gnw"), sinks=stack("sinks"),
        ln1_g=stack("ln1g"), ln1_b=stack("ln1b"), ln2_g=stack("ln2g"), ln2_b=stack("ln2b"))
    weights = dict(w_in=w_in, conv_w=conv_w, a_log=a_log, dt_bias=dt_bias, dn_norm_w=dn_norm_w, sinks=sinks,
                   w_branch_a=w_branch_a, w_branch_b=w_branch_b, w_out=w_out, ln1_g=ln1_g, ln1_b=ln1_b,
                   w_gate_up=w_gate_up, w_down=w_down, ln2_g=ln2_g, ln2_b=ln2_b)
    ms = dict(w_in=m_w_in, conv_w=m_conv_w, a_log=m_a_log, dt_bias=m_dt_bias, dn_norm_w=m_dn_norm_w, sinks=m_sinks,
              w_branch_a=m_w_branch_a, w_branch_b=m_w_branch_b, w_out=m_w_out, ln1_g=m_ln1_g, ln1_b=m_ln1_b,
              w_gate_up=m_w_gate_up, w_down=m_w_down, ln2_g=m_ln2_g, ln2_b=m_ln2_b)
    vs = dict(w_in=v_w_in, conv_w=v_conv_w, a_log=v_a_log, dt_bias=v_dt_bias, dn_norm_w=v_dn_norm_w, sinks=v_sinks,
              w_branch_a=v_w_branch_a, w_branch_b=v_w_branch_b, w_out=v_w_out, ln1_g=v_ln1_g, ln1_b=v_ln1_b,
              w_gate_up=v_w_gate_up, w_down=v_w_down, ln2_g=v_ln2_g, ln2_b=v_ln2_b)
    names = list(weights)
    upd = {n: _adamw("adamw_" + n, weights[n], grads[n], ms[n], vs[n]) for n in grads}
    big_names = ["w_in", "w_branch_a", "w_branch_b", "w_out", "w_gate_up", "w_down"]
    carry = {n: None for n in big_names}

    def update_layer(l):
        for t, n in enumerate(big_names):
            step = _adamw_layer_t if weights[n].shape[-1] % 128 else _adamw_layer
            carry[n] = step("adamw_" + n, l, weights[n], big[l][t], ms[n], vs[n], carry[n], token)

    for l in range(DEPTH - 1, pending[0], -1):
        update_layer(l)
    big[pending[0]] = _rs_end(pending[1], carry[big_names[-1]][1], RS_META, ci, myq, "rs_chips_%d" % pending[0])
    update_layer(pending[0])
    for n in big_names:
        if weights[n].shape[-1] % 128:
            nl, r, c = weights[n].shape
            carry[n] = tuple(jnp.swapaxes(a.reshape(nl, c, r), 1, 2) for a in carry[n])
        grads[n], upd[n] = carry[n][0], carry[n][1:]
    return (loss, dh[None], *[grads[n] for n in names], *[upd[n][0] for n in names], *[upd[n][1] for n in names],
            *[upd[n][2] for n in names])
```

```python
import functools

import jax
import jax.numpy as jnp
from jax import lax
from jax.experimental import pallas as pl
from jax.experimental.pallas import tpu as pltpu

F32 = jnp.float32
BF16 = jnp.bfloat16
_MXU = BF16

D = 1024
DEPTH = 4
DN_HEADS = 8
DN_DIM = 128
DN_CONV = 5
CHUNK = 64
SW_HEADS = 16
SW_KV = 4
SW_DIM = 64
SW_GRP = SW_HEADS // SW_KV
SW_BLOCK = 128
ROPE_THETA = 10000.0
FFN = 2816
ALPHA = (2.0 * DEPTH) ** 0.25
LN_EPS = 1e-5
RMS_EPS = 1e-6
IN_COLS = 7712
O_Z, O_GA, O_GB, O_QS, O_KS, O_VS, N_MAIN = 3072, 4096, 5120, 6144, 7168, 7424, 7680
R_BG, R_SW, R_GATES = 4096, 4128, 5664
N_CHIPS = 4
IN_SHARD = IN_COLS // N_CHIPS
IN_PAD = 2048
ADAM_LR, ADAM_B1, ADAM_B2, ADAM_EPS, ADAM_WD, ADAM_STEP = 0.001, 0.9, 0.999, 1e-08, 0.01, 10
VMEM_LIMIT = 52 * 1024 * 1024
MESH = pl.DeviceIdType.MESH
ANY = pl.BlockSpec(memory_space=pl.ANY)


def _params(n_grid, **kw):
    return pltpu.CompilerParams(dimension_semantics=("arbitrary",) * n_grid, vmem_limit_bytes=VMEM_LIMIT, **kw)


def _full(a):
    nd = a.ndim
    return pl.BlockSpec(a.shape, lambda *_, nd=nd: (0,) * nd)


def _raw_dot(a, b, ca, cb):
    return lax.dot_general(a.astype(_MXU), b.astype(_MXU), (((ca,), (cb,)), ((), ())), preferred_element_type=F32)


@jax.custom_vjp
def _nn(a, b):
    return _raw_dot(a, b, 1, 0)


@jax.custom_vjp
def _nt(a, b):
    return _raw_dot(a, b, 1, 1)


@jax.custom_vjp
def _tn(a, b):
    if a.shape[1] > b.shape[1]:
        return _raw_dot(b, a, 0, 0).T
    return _raw_dot(a, b, 0, 0)


_nn.defvjp(lambda a, b: (_nn(a, b), (a, b)), lambda r, g: (_nt(g, r[1]), _tn(r[0], g)))
_nt.defvjp(lambda a, b: (_nt(a, b), (a, b)), lambda r, g: (_nn(g, r[1]), _tn(g, r[0])))
_tn.defvjp(lambda a, b: (_tn(a, b), (a, b)), lambda r, g: (_nt(r[1], g), _nn(r[0], g)))


def _hdot(a, b, ca=1, cb=0):
    ah, bh = a.astype(BF16), b.astype(BF16)
    al, bl = (a - ah.astype(F32)).astype(BF16), (b - bh.astype(F32)).astype(BF16)
    dot = lambda u, v: lax.dot_general(u, v, (((ca,), (cb,)), ((), ())), preferred_element_type=F32)
    return dot(ah, bh) + (dot(ah, bl) + dot(al, bh))


def _inv_impl(mats):
    n = mats[0].shape[0]
    eye = (lax.broadcasted_iota(jnp.int32, (n, n), 0) == lax.broadcasted_iota(jnp.int32, (n, n), 1)).astype(F32)
    ps = [-a for a in mats]
    ts = [eye + p for p in ps]
    for _ in range(max(1, (n - 1).bit_length()) - 1):
        ps = [_hdot(p, p) for p in ps]
        ts = [t + _hdot(t, p) for t, p in zip(ts, ps)]
    return tuple(ts)


@jax.custom_vjp
def _inv(mats):
    return _inv_impl(mats)


def _inv_fwd(mats):
    ts = _inv_impl(mats)
    return ts, ts


def _inv_bwd(ts, gs):
    xs = [_hdot(t, g, 0, 0) for t, g in zip(ts, gs)]
    return (tuple(-_hdot(x, t, 1, 1) for x, t in zip(xs, ts)),)


_inv.defvjp(_inv_fwd, _inv_bwd)


@jax.custom_vjp
def _inv_saved(mats, saved):
    return saved


_inv_saved.defvjp(lambda mats, saved: (saved, saved),
                  lambda ts, gs: (_inv_bwd(ts, gs)[0], tuple(jnp.zeros_like(t) for t in ts)))


def _tile(n, cap):
    if n <= cap:
        return n
    best = [t for t in range(128, cap + 1, 128) if n % t == 0]
    assert best, (n, cap)
    return best[-1]


def _mm(a, b, *, name, ta=False, tb=False, add=None, tm=1024, tn=1024, tk=1024, after=None, out_dtype=F32):
    if ta:
        k_, m_ = a.shape
    else:
        m_, k_ = a.shape
    n_ = b.shape[0] if tb else b.shape[1]
    tm, tn, tk = _tile(m_, tm), _tile(n_, tn), _tile(k_, tk)
    nk = k_ // tk
    has_add = add is not None

    def body(*refs):
        a_ref, b_ref = refs[:2]
        add_ref = refs[2] if has_add else None
        o_ref = refs[2 + has_add + (after is not None)]
        part = _raw_dot(a_ref[...], b_ref[...], 0 if ta else 1, 1 if tb else 0)
        if nk == 1:
            o_ref[...] = (part + add_ref[...] if has_add else part).astype(o_ref.dtype)
            return
        acc = refs[-1]
        k = pl.program_id(2)

        @pl.when(k == 0)
        def _():
            acc[...] = part

        @pl.when(jnp.logical_and(k > 0, k < nk - 1))
        def _():
            acc[...] += part

        @pl.when(k == nk - 1)
        def _():
            o_ref[...] = (acc[...] + part + add_ref[...] if has_add else acc[...] + part).astype(o_ref.dtype)

    a_spec = pl.BlockSpec((tk, tm), lambda i, j, k: (k, i)) if ta else pl.BlockSpec((tm, tk), lambda i, j, k: (i, k))
    b_spec = pl.BlockSpec((tn, tk), lambda i, j, k: (j, k)) if tb else pl.BlockSpec((tk, tn), lambda i, j, k: (k, j))
    o_spec = pl.BlockSpec((tm, tn), lambda i, j, k: (i, j))
    in_specs = [a_spec, b_spec] + ([o_spec] if has_add else []) + ([ANY] if after is not None else [])
    args = (a, b) + ((add,) if has_add else ()) + ((after,) if after is not None else ())
    return pl.pallas_call(
        body, name=name, grid=(m_ // tm, n_ // tn, nk), in_specs=in_specs, out_specs=o_spec,
        out_shape=jax.ShapeDtypeStruct((m_, n_), out_dtype),
        scratch_shapes=[pltpu.VMEM((tm, tn), F32)] if nk > 1 else [],
        compiler_params=_params(3))(*args)


def _mm_fused(a, b, post, outs, *, name, rows=(), params=(), n_sums=0, tb=False, tm=1024, tn=1024, tk=1024):
    m_, k_ = a.shape
    n_ = b.shape[0] if tb else b.shape[1]
    tm, tn, tk = _tile(m_, tm), _tile(n_, tn), _tile(k_, tk)
    nk = k_ // tk
    nr, npar, no = len(rows), len(params), len(outs)
    aliased = [o[4] for o in outs if o[4] is not None]
    nin = 2 + nr + npar

    def body(*refs):
        a_ref, b_ref = refs[:2]
        row_refs, par_refs = refs[2:2 + nr], refs[2 + nr:nin]
        out_refs = refs[nin + len(aliased):nin + len(aliased) + no + n_sums]
        part = _raw_dot(a_ref[...], b_ref[...], 1, 1 if tb else 0)
        k = pl.program_id(2)
        first = jnp.logical_and(pl.program_id(0) == 0, pl.program_id(1) == 0)
        if nk > 1:
            acc = refs[-1]

            @pl.when(k == 0)
            def _():
                acc[...] = part

            @pl.when(jnp.logical_and(k > 0, k < nk - 1))
            def _():
                acc[...] += part

        @pl.when(k == nk - 1)
        def _():
            total = acc[...] + part if nk > 1 else part
            res = post(total, *[r[...].astype(F32) for r in row_refs], *[p[...] for p in par_refs])
            for o_ref, val in zip(out_refs[:no], res[:no]):
                o_ref[...] = val.astype(o_ref.dtype)
            for s_ref, val in zip(out_refs[no:], res[no:]):
                @pl.when(first)
                def _(s_ref=s_ref):
                    s_ref[...] = jnp.zeros_like(s_ref)
                s_ref[...] += val

    b_spec = pl.BlockSpec((tn, tk), lambda i, j, k: (j, k)) if tb else pl.BlockSpec((tk, tn), lambda i, j, k: (k, j))
    in_specs = [pl.BlockSpec((tm, tk), lambda i, j, k: (i, k)), b_spec]
    in_specs += [pl.BlockSpec((tm, w), lambda i, j, k, cb=cb: (i, cb(j))) for _, w, cb in rows]
    in_specs += [_full(p) for p in params] + [ANY] * len(aliased)
    out_specs = [pl.BlockSpec((tm, w), lambda i, j, k, cb=cb: (i, cb(j))) for _, w, _, cb, _ in outs]
    out_specs += [_full(p) for p in params[:n_sums]]
    out_shape = [jax.ShapeDtypeStruct((m_, tot), dt) for tot, _, dt, _, _ in outs]
    out_shape += [jax.ShapeDtypeStruct(p.shape, F32) for p in params[:n_sums]]
    aliases, pos = {}, nin
    for oi, o in enumerate(outs):
        if o[4] is not None:
            aliases[pos] = oi
            pos += 1
    return pl.pallas_call(
        body, name=name, grid=(m_ // tm, n_ // tn, nk), in_specs=in_specs, out_specs=out_specs, out_shape=out_shape,
        scratch_shapes=[pltpu.VMEM((tm, tn), F32)] if nk > 1 else [], input_output_aliases=aliases,
        compiler_params=_params(3))(a, b, *[r[0] for r in rows], *params, *aliased)


def _mm_pre(pre, rows, params, b, *, name, tm=512):
    m_ = rows[0][0].shape[0]
    k_, n_ = b.shape
    tm = _tile(m_, tm)
    nr, npar = len(rows), len(params)

    def body(*refs):
        b_ref, a_out, o_ref = refs[nr + npar:]
        a = pre(*[r[...].astype(F32) for r in refs[:nr]], *[p[...] for p in refs[nr:nr + npar]]).astype(a_out.dtype)
        a_out[...] = a
        o_ref[...] = _raw_dot(a, b_ref[...], 1, 0)

    return pl.pallas_call(
        body, name=name, grid=(m_ // tm,),
        in_specs=[pl.BlockSpec((tm, w), lambda i, cb=cb: (i, cb)) for _, w, cb in rows] + [_full(p) for p in params]
        + [_full(b)],
        out_specs=[pl.BlockSpec((tm, k_), lambda i: (i, 0)), pl.BlockSpec((tm, n_), lambda i: (i, 0))],
        out_shape=[jax.ShapeDtypeStruct((m_, k_), _MXU), jax.ShapeDtypeStruct((m_, n_), F32)],
        compiler_params=_params(1))(*[r[0] for r in rows], *params, b)


def _row_spec(tb, w, c0, percol):
    return pl.BlockSpec((tb, w), lambda i, j, c0=c0, pc=percol: (i, c0 + (j if pc else 0)))


def _stage_fwd(f, name, rows, params, outs, tb, ncol=1):
    t_ = rows[0][0].shape[0]
    nr, npar = len(rows), len(params)

    def body(*refs):
        res = f(*[r[...].astype(F32) for r in refs[:nr + npar]])
        for o_ref, val in zip(refs[nr + npar:], res):
            o_ref[...] = val.astype(o_ref.dtype)

    return pl.pallas_call(
        body, name=name, grid=(t_ // tb, ncol),
        in_specs=[_row_spec(tb, w, c0, pc) for (_, w, c0, pc) in rows] + [_full(p) for p in params],
        out_specs=[pl.BlockSpec((tb, w), lambda i, j: (i, j)) for w, _ in outs],
        out_shape=[jax.ShapeDtypeStruct((t_, w * ncol), dt) for w, dt in outs],
        compiler_params=_params(2))(*[r[0] for r in rows], *params)


def _into(dest, tb, width, t_, ncol, dtype):
    if dest is None:
        return pl.BlockSpec((tb, width), lambda i, j: (i, j)), jax.ShapeDtypeStruct((t_, width * ncol), dtype), None
    buf, total, c0 = dest
    return (pl.BlockSpec((tb, width), lambda i, j, c0=c0: (i, c0 + j)), jax.ShapeDtypeStruct((t_, total), dtype), buf)


def _stage_bwd(f, name, rows, params, douts, tb, ncol=1, cat=None, dtypes=None, dest=None, after=None):
    t_ = rows[0][0].shape[0]
    nr, npar, nd = len(rows), len(params), len(douts)
    cat = cat if cat is not None else [[r] for r in range(nr)]
    dtypes = dtypes if dtypes is not None else [F32] * len(cat)
    dest = dest or {}
    assert ncol == 1 or all(len(g) == 1 and rows[g[0]][3] for g in cat)
    nin = nr + npar + nd
    unread = [after] if after is not None else []

    def body(*refs):
        ins = [r[...].astype(F32) for r in refs[:nr + npar]]
        dvals = tuple(r[...].astype(F32) for r in refs[nr + npar:nin])
        out_refs = refs[nin + len(aliased) + len(unread):]
        _, vjp = jax.vjp(f, *ins)
        grads = vjp(dvals)
        for o_ref, grp in zip(out_refs[:len(cat)], cat):
            val = grads[grp[0]] if len(grp) == 1 else jnp.concatenate([grads[r] for r in grp], axis=-1)
            o_ref[...] = val.astype(o_ref.dtype)
        first = jnp.logical_and(pl.program_id(0) == 0, pl.program_id(1) == 0)
        for p_ref, gp in zip(out_refs[len(cat):], grads[nr:]):
            @pl.when(first)
            def _(p_ref=p_ref):
                p_ref[...] = jnp.zeros_like(p_ref)
            p_ref[...] += gp

    gw = [sum(rows[r][1] for r in grp) for grp in cat]
    out_specs, out_shape, aliased, aliases = [], [], [], {}
    for gi, (w, dt) in enumerate(zip(gw, dtypes)):
        spec, shape, buf = _into(dest.get(gi), tb, w, t_, ncol, dt)
        out_specs.append(spec)
        out_shape.append(shape)
        if buf is not None:
            aliases[nin + len(aliased)] = gi
            aliased.append(buf)
    return pl.pallas_call(
        body, name=name, grid=(t_ // tb, ncol),
        in_specs=[_row_spec(tb, w, c0, pc) for (_, w, c0, pc) in rows] + [_full(p) for p in params]
        + [pl.BlockSpec((tb, d.shape[1] // ncol), lambda i, j: (i, j)) for d in douts]
        + [ANY] * (len(aliased) + len(unread)),
        out_specs=out_specs + [_full(p) for p in params],
        out_shape=out_shape + [jax.ShapeDtypeStruct(p.shape, F32) for p in params],
        input_output_aliases=aliases,
        compiler_params=_params(2))(*[r[0] for r in rows], *params, *douts, *aliased, *unread)


def _ln_f(x, y, g, b):
    u = ALPHA * x + y
    c = u - jnp.mean(u, axis=-1, keepdims=True)
    var = jnp.mean(c * c, axis=-1, keepdims=True)
    return (c * lax.rsqrt(var + LN_EPS) * g + b,)


def _ln_f2(x, y, g, b):
    out, = _ln_f(x, y, g, b)
    return out, out


def _swiglu_tile(gu):
    half = gu.shape[1] // 2
    return (jax.nn.silu(gu[:, :half]) * gu[:, half:],)


def _merge_f(ya, yb, ga, gb):
    return (jax.nn.sigmoid(ga) * ya + jax.nn.sigmoid(gb) * yb,)


def _gnorm_f(of, ob, z, w):
    o = of + ob
    return (o * lax.rsqrt(jnp.mean(o * o, axis=-1, keepdims=True) + RMS_EPS) * w * jax.nn.silu(z),)


def _bg_f(x, arow, dtrow):
    lane = lax.broadcasted_iota(jnp.int32, x.shape, 1)
    beta = jax.nn.sigmoid(x)
    g = -jnp.exp(arow) * jax.nn.softplus(x + dtrow)
    return (jnp.where(lane < 16, beta, jnp.where(lane < 32, g, 0.0)),)


PREP_ROWS = 512
PAD = 8


def _prep_f(part, w, *wins):
    xc = wins[0] * w[0:1, :]
    for k in range(1, DN_CONV):
        xc = xc + wins[k] * w[k:k + 1, :]
    a = jax.nn.silu(xc)
    nrm = a * lax.rsqrt(jnp.sum(a * a, axis=-1, keepdims=True) + RMS_EPS)
    return jnp.where(part == 0, nrm * (DN_DIM ** -0.5), jnp.where(part == 1, nrm, a))


def _windows(pad_ref, r0, rows):
    return [pad_ref[PAD + r0 - 2 + k:PAD + r0 - 2 + k + rows, :] for k in range(DN_CONV)]


def _prep_fwd(pm, conv):
    t_ = pm.shape[0]
    rows = min(PREP_ROWS, t_)

    def body(x_ref, w_ref, o_ref, pad_ref):
        part = pl.program_id(0) // DN_HEADS
        pad_ref[0:PAD, :] = jnp.zeros((PAD, DN_DIM), F32)
        pad_ref[PAD + t_:2 * PAD + t_, :] = jnp.zeros((PAD, DN_DIM), F32)
        pad_ref[PAD:PAD + t_, :] = x_ref[...]
        w = w_ref[...]
        for r in range(t_ // rows):
            o_ref[r * rows:(r + 1) * rows, :] = _prep_f(part, w, *_windows(pad_ref, r * rows, rows))

    ncb = 3 * DN_HEADS
    return pl.pallas_call(
        body, name="prep_fwd", grid=(ncb,),
        in_specs=[pl.BlockSpec((t_, DN_DIM), lambda j: (0, j)), pl.BlockSpec((DN_CONV, DN_DIM), lambda j: (0, j))],
        out_specs=pl.BlockSpec((t_, DN_DIM), lambda j: (0, j)),
        out_shape=jax.ShapeDtypeStruct((t_, ncb * DN_DIM), F32),
        scratch_shapes=[pltpu.VMEM((t_ + 2 * PAD, DN_DIM), F32)],
        compiler_params=_params(1))(pm, conv)


def _prep_bwd(pm, conv, dout, dpm):
    t_ = pm.shape[0]
    rows = min(PREP_ROWS, t_)

    def body(x_ref, w_ref, d_ref, _, dx_ref, dw_ref, pad_ref, dpad_ref):
        part = pl.program_id(0) // DN_HEADS
        pad_ref[0:PAD, :] = jnp.zeros((PAD, DN_DIM), F32)
        pad_ref[PAD + t_:2 * PAD + t_, :] = jnp.zeros((PAD, DN_DIM), F32)
        pad_ref[PAD:PAD + t_, :] = x_ref[...]
        dpad_ref[...] = jnp.zeros_like(dpad_ref)
        w = w_ref[...]
        dw = jnp.zeros((DN_CONV, DN_DIM), F32)
        for r in range(t_ // rows):
            r0 = r * rows
            _, vjp = jax.vjp(functools.partial(_prep_f, part), w, *_windows(pad_ref, r0, rows))
            grads = vjp(d_ref[r0:r0 + rows, :])
            dw = dw + grads[0]
            for k in range(DN_CONV):
                lo = PAD + r0 - 2 + k
                dpad_ref[lo:lo + rows, :] += grads[1 + k]
        dx_ref[...] = dpad_ref[PAD:PAD + t_, :].astype(dx_ref.dtype)
        dw_ref[...] = dw

    ncb = 3 * DN_HEADS
    col = pl.BlockSpec((t_, DN_DIM), lambda j: (0, j))
    wsp = pl.BlockSpec((DN_CONV, DN_DIM), lambda j: (0, j))
    return pl.pallas_call(
        body, name="prep_bwd", grid=(ncb,), in_specs=[col, wsp, col, ANY], out_specs=[col, wsp],
        out_shape=[jax.ShapeDtypeStruct(dpm.shape, dpm.dtype), jax.ShapeDtypeStruct((DN_CONV, ncb * DN_DIM), F32)],
        scratch_shapes=[pltpu.VMEM((t_ + 2 * PAD, DN_DIM), F32), pltpu.VMEM((t_ + 2 * PAD, DN_DIM), F32)],
        input_output_aliases={3: 0}, compiler_params=_params(1))(pm, conv, dout, dpm)


def _dn_chunk(sgns, qs, ks, vs, grows, brows, tsaved=None, with_t=False):
    c = qs[0].shape[0]
    i = lax.broadcasted_iota(jnp.int32, (c, c), 0)
    j = lax.broadcasted_iota(jnp.int32, (c, c), 1)
    eye = i == j
    incl = {s: (i - j) * int(s) >= 0 for s in set(sgns)}
    strict = {s: (i - j) * int(s) > 0 for s in set(sgns)}
    gcs = [jnp.sum(jnp.where(incl[s], g, 0.0), axis=1, keepdims=True) for s, g in zip(sgns, grows)]
    grs = [jnp.sum(jnp.where(eye, gc, 0.0), axis=0, keepdims=True) for gc in gcs]
    bcs = [jnp.sum(jnp.where(eye, b, 0.0), axis=1, keepdims=True) for b in brows]
    gls = [jnp.sum(g, axis=1, keepdims=True) for g in grows]
    decs = [jnp.exp(jnp.where(incl[s], gc - gr, -1e30)) for s, gc, gr in zip(sgns, gcs, grs)]
    kks = [_nt(k, k) for k in ks]
    amats = tuple(jnp.where(strict[s], bc * kk * dec, 0.0) for s, bc, kk, dec in zip(sgns, bcs, kks, decs))
    tinvs = _inv(amats) if tsaved is None else _inv_saved(amats, tsaved)
    egcs = [jnp.exp(gc) for gc in gcs]
    us = [_nn(t, v * bc) for t, v, bc in zip(tinvs, vs, bcs)]
    ws = [_nn(t, k * (bc * egc)) for t, k, bc, egc in zip(tinvs, ks, bcs, egcs)]
    qks = [_nt(q, k) * dec for q, k, dec in zip(qs, ks, decs)]
    qds = [q * egc for q, egc in zip(qs, egcs)]
    kds = [k * jnp.exp(gl - gc) for k, gl, gc in zip(ks, gls, gcs)]
    res = tuple(us), tuple(ws), tuple(qks), tuple(qds), tuple(kds)
    return res + (tinvs,) if with_t else res


def _dn_step(us, ws, qks, qds, kds, grows, ss):
    gls = [jnp.exp(jnp.sum(g, axis=1, keepdims=True)) for g in grows]
    wss = [_nn(w, s) for w, s in zip(ws, ss)]
    qss = [_nn(qd, s) for qd, s in zip(qds, ss)]
    vns = [u - x for u, x in zip(us, wss)]
    os_ = [a + _nn(qk, vn) for a, qk, vn in zip(qss, qks, vns)]
    s2s = [s * gl + _tn(kd, vn) for s, gl, kd, vn in zip(ss, gls, kds, vns)]
    return tuple(os_), tuple(s2s)


def _hs(h):
    return slice(h * DN_DIM, (h + 1) * DN_DIM)


_DIR_SGN = (1, -1)
_A_OUT = 5
_PROBLEMS = [(d, h) for d in range(2) for h in range(DN_HEADS)]
_SGNS = [_DIR_SGN[d] for d, _ in _PROBLEMS]


def _chunk_inputs(q_ref, k_ref, v_ref, g_ref, b_ref):
    heads = lambda ref: tuple(ref[:, _hs(h)].astype(F32) for _, h in _PROBLEMS)
    rows = lambda ref: tuple(ref[d, 0, h:h + 1, :] for d, h in _PROBLEMS)
    return heads(q_ref), heads(k_ref), heads(v_ref), rows(g_ref), rows(b_ref)


def _dn_a_fwd(qkv, grows, brows):
    t_ = qkv.shape[0]
    nch = t_ // CHUNK

    def body(q_ref, k_ref, v_ref, g_ref, b_ref, *outs):
        us, ws, qks, qds, kds, tinvs = _dn_chunk(_SGNS, *_chunk_inputs(q_ref, k_ref, v_ref, g_ref, b_ref), with_t=True)
        for p, (d, h) in enumerate(_PROBLEMS):
            u_ref, w_ref, qk_ref, qd_ref, kd_ref = outs[d * _A_OUT:(d + 1) * _A_OUT]
            u_ref[:, _hs(h)], w_ref[:, _hs(h)], qk_ref[0, h] = us[p], ws[p].astype(_MXU), qks[p].astype(_MXU)
            qd_ref[:, _hs(h)], kd_ref[:, _hs(h)] = qds[p].astype(_MXU), kds[p].astype(_MXU)
            outs[2 * _A_OUT + d][0, h] = tinvs[p]

    rspec = pl.BlockSpec((2, 1, DN_HEADS, CHUNK), lambda c: (0, c, 0, 0))
    big = pl.BlockSpec((CHUNK, D), lambda c: (c, 0))
    qks = pl.BlockSpec((1, DN_HEADS, CHUNK, CHUNK), lambda c: (c, 0, 0, 0))
    bigs = lambda dt: jax.ShapeDtypeStruct((t_, D), dt)
    qksh = lambda dt: jax.ShapeDtypeStruct((nch, DN_HEADS, CHUNK, CHUNK), dt)
    res = pl.pallas_call(
        body, name="dn_a_fwd", grid=(nch,),
        in_specs=[pl.BlockSpec((CHUNK, D), lambda c, p=p: (c, p)) for p in range(3)] + [rspec, rspec],
        out_specs=[big, big, qks, big, big] * 2 + [qks, qks],
        out_shape=[bigs(F32), bigs(_MXU), qksh(_MXU), bigs(_MXU), bigs(_MXU)] * 2 + [qksh(F32)] * 2,
        compiler_params=_params(1))(qkv, qkv, qkv, grows, brows)
    return res[:2 * _A_OUT], res[2 * _A_OUT:]


def _dn_a_bwd(qkv, grows, brows, tinv, dres, dg_b):
    t_ = qkv.shape[0]
    nch = t_ // CHUNK

    def body(q_ref, k_ref, v_ref, g_ref, b_ref, tf_ref, tb_ref, *rest):
        dins, dgb_ref, (dqkv_ref, dg_ref, db_ref) = rest[:2 * _A_OUT], rest[2 * _A_OUT], rest[2 * _A_OUT + 1:]
        tsaved = tuple((tf_ref, tb_ref)[d][0, h] for d, h in _PROBLEMS)
        _, vjp = jax.vjp(functools.partial(_dn_chunk, _SGNS, tsaved=tsaved),
                         *_chunk_inputs(q_ref, k_ref, v_ref, g_ref, b_ref))
        cots = []
        for o in range(_A_OUT):
            cots.append(tuple(dins[d * _A_OUT + o][0, h] if o == 2 else dins[d * _A_OUT + o][:, _hs(h)]
                              for d, h in _PROBLEMS))
        gq, gk, gv, gg, gb = vjp(tuple(cots))
        for p, (d, h) in enumerate(_PROBLEMS):
            dg_ref[d, 0, h:h + 1, :] = gg[p] + dgb_ref[d, 0, h:h + 1, :]
            db_ref[d, 0, h:h + 1, :] = gb[p]
        for h in range(DN_HEADS):
            dqkv_ref[:, _hs(h)] = gq[h] + gq[DN_HEADS + h]
            dqkv_ref[:, _hs(DN_HEADS + h)] = gk[h] + gk[DN_HEADS + h]
            dqkv_ref[:, _hs(2 * DN_HEADS + h)] = gv[h] + gv[DN_HEADS + h]

    rspec = pl.BlockSpec((2, 1, DN_HEADS, CHUNK), lambda c: (0, c, 0, 0))
    big = pl.BlockSpec((CHUNK, D), lambda c: (c, 0))
    qks = pl.BlockSpec((1, DN_HEADS, CHUNK, CHUNK), lambda c: (c, 0, 0, 0))
    rsh = jax.ShapeDtypeStruct(grows.shape, F32)
    return pl.pallas_call(
        body, name="dn_a_bwd", grid=(nch,),
        in_specs=[pl.BlockSpec((CHUNK, D), lambda c, p=p: (c, p)) for p in range(3)] + [rspec, rspec, qks, qks]
        + [big, big, qks, big, big] * 2 + [rspec],
        out_specs=[pl.BlockSpec((CHUNK, 3 * D), lambda c: (c, 0)), rspec, rspec],
        out_shape=[jax.ShapeDtypeStruct((t_, 3 * D), F32), rsh, rsh],
        compiler_params=_params(1))(qkv, qkv, qkv, grows, brows, *tinv, *dres, dg_b)


def _dir_specs(nch):
    def cidx(d):
        return (lambda n: n) if d == 0 else (lambda n: nch - 1 - n)
    out = []
    for d in range(2):
        ci = cidx(d)
        big = pl.BlockSpec((CHUNK, D), lambda n, ci=ci: (ci(n), 0))
        qks = pl.BlockSpec((1, DN_HEADS, CHUNK, CHUNK), lambda n, ci=ci: (ci(n), 0, 0, 0))
        row = pl.BlockSpec((1, 1, DN_HEADS, CHUNK), lambda n, ci=ci, d=d: (d, ci(n), 0, 0))
        st = pl.BlockSpec((1, DN_HEADS, DN_DIM, DN_DIM), lambda n, ci=ci: (ci(n), 0, 0, 0))
        out.append(dict(big=big, qk=qks, row=row, st=st))
    return out


def _step_inputs(ins, per_dir):
    def pick(o):
        if o == 2:
            return tuple(ins[d * per_dir + o][0, h].astype(F32) for d, h in _PROBLEMS)
        if o == 5:
            return tuple(ins[d * per_dir + o][0, 0, h:h + 1, :] for d, h in _PROBLEMS)
        return tuple(ins[d * per_dir + o][:, _hs(h)].astype(F32) for d, h in _PROBLEMS)
    return [pick(o) for o in range(6)]


def _dn_b_fwd(ares, grows):
    t_ = ares[0].shape[0]
    nch = t_ // CHUNK
    sp = _dir_specs(nch)

    def body(*refs):
        ins, outs, s_ref = refs[:12], refs[12:16], refs[16]

        @pl.when(pl.program_id(0) == 0)
        def _():
            s_ref[...] = jnp.zeros_like(s_ref)

        ss = tuple(s_ref[p] for p in range(len(_PROBLEMS)))
        os_, s2s = _dn_step(*_step_inputs(ins, 6), ss)
        for p, (d, h) in enumerate(_PROBLEMS):
            outs[2 + d][0, h] = ss[p]
            outs[d][:, _hs(h)] = os_[p]
            s_ref[p] = s2s[p]

    in_specs, args = [], []
    for d in range(2):
        in_specs += [sp[d]["big"], sp[d]["big"], sp[d]["qk"], sp[d]["big"], sp[d]["big"], sp[d]["row"]]
        args += list(ares[d * _A_OUT:(d + 1) * _A_OUT]) + [grows]
    stsh = jax.ShapeDtypeStruct((nch, DN_HEADS, DN_DIM, DN_DIM), F32)
    osh = jax.ShapeDtypeStruct((t_, D), F32)
    return pl.pallas_call(
        body, name="dn_b_fwd", grid=(nch,), in_specs=in_specs,
        out_specs=[sp[0]["big"], sp[1]["big"], sp[0]["st"], sp[1]["st"]], out_shape=[osh, osh, stsh, stsh],
        scratch_shapes=[pltpu.VMEM((2 * DN_HEADS, DN_DIM, DN_DIM), F32)],
        compiler_params=_params(1))(*args)


def _dn_b_bwd(ares, grows, st_f, st_b, do):
    t_ = ares[0].shape[0]
    nch = t_ // CHUNK
    sp = _dir_specs(nch)
    rsp = [sp[1], sp[0]]

    def body(*refs):
        ins, outs, ds_ref = refs[:16], refs[16:28], refs[28]

        @pl.when(pl.program_id(0) == 0)
        def _():
            ds_ref[...] = jnp.zeros_like(ds_ref)

        ss = tuple(ins[d * 8 + 6][0, h] for d, h in _PROBLEMS)
        _, vjp = jax.vjp(_dn_step, *_step_inputs(ins, 8), ss)
        dos = tuple(ins[d * 8 + 7][:, _hs(h)] for d, h in _PROBLEMS)
        grads = vjp((dos, tuple(ds_ref[p] for p in range(len(_PROBLEMS)))))
        for p, (d, h) in enumerate(_PROBLEMS):
            du_ref, dw_ref, dqk_ref, dqd_ref, dkd_ref, dg_ref = outs[d * 6:(d + 1) * 6]
            du_ref[:, _hs(h)], dw_ref[:, _hs(h)], dqk_ref[0, h] = grads[0][p], grads[1][p], grads[2][p]
            dqd_ref[:, _hs(h)], dkd_ref[:, _hs(h)] = grads[3][p], grads[4][p]
            dg_ref[0, 0, h:h + 1, :] = grads[5][p]
            ds_ref[p] = grads[6][p]

    in_specs, args, out_specs, out_shape = [], [], [], []
    big_sh = jax.ShapeDtypeStruct((t_, D), F32)
    qk_sh = jax.ShapeDtypeStruct((nch, DN_HEADS, CHUNK, CHUNK), F32)
    row_sh = jax.ShapeDtypeStruct((1, nch, DN_HEADS, CHUNK), F32)
    for d in range(2):
        s = rsp[d]
        row0 = pl.BlockSpec((1, 1, DN_HEADS, CHUNK), lambda m, d=d: (0, (nch - 1 - m) if d == 0 else m, 0, 0))
        rowd = pl.BlockSpec((1, 1, DN_HEADS, CHUNK), lambda m, d=d: (d, (nch - 1 - m) if d == 0 else m, 0, 0))
        in_specs += [s["big"], s["big"], s["qk"], s["big"], s["big"], rowd, s["st"], s["big"]]
        args += list(ares[d * _A_OUT:(d + 1) * _A_OUT]) + [grows, (st_f, st_b)[d], do]
        out_specs += [s["big"], s["big"], s["qk"], s["big"], s["big"], row0]
        out_shape += [big_sh, big_sh, qk_sh, big_sh, big_sh, row_sh]
    res = pl.pallas_call(
        body, name="dn_b_bwd", grid=(nch,), in_specs=in_specs, out_specs=out_specs, out_shape=out_shape,
        scratch_shapes=[pltpu.VMEM((2 * DN_HEADS, DN_DIM, DN_DIM), F32)],
        compiler_params=_params(1))(*args)
    dares = list(res[0:5]) + list(res[6:11])
    return dares, jnp.concatenate([res[5], res[11]], axis=0)


@jax.custom_vjp
def _rot_half(x):
    half, width = SW_DIM // 2, x.shape[1]
    first = lax.broadcasted_iota(jnp.int32, x.shape, 1) % SW_DIM < half
    return jnp.where(first, -pltpu.roll(x, width - half, axis=1), pltpu.roll(x, half, axis=1))


_rot_half.defvjp(lambda x: (_rot_half(x), None), lambda _, g: (-_rot_half(g),))


def _rope(x, c, s):
    reps = x.shape[1] // c.shape[1]
    return x * jnp.tile(c, (1, reps)) + _rot_half(x) * jnp.tile(s, (1, reps))


def _rope_t(g, c, s):
    reps = g.shape[1] // c.shape[1]
    return g * jnp.tile(c, (1, reps)) - _rot_half(g * jnp.tile(s, (1, reps)))


_SW_SCALE = SW_DIM ** -0.5
_KV_HEADS = [[kvh * SW_GRP + g for g in range(SW_GRP)] for kvh in range(SW_KV)]


def _by_group(x):
    return [jnp.concatenate([x[:, h * SW_DIM:(h + 1) * SW_DIM] for h in hs], axis=0) for hs in _KV_HEADS]


def _from_groups(xs):
    return jnp.concatenate([x[g * SW_BLOCK:(g + 1) * SW_BLOCK] for x in xs for g in range(SW_GRP)], axis=-1)


def _attn_probs(blk, t_, cq, sq, ck, sk, q, kall, sinks):
    qgs = _by_group(_rope(q, cq, sq))
    kr = _rope(kall, ck, sk)
    khs = [kr[:, kvh * SW_DIM:(kvh + 1) * SW_DIM] for kvh in range(SW_KV)]
    nq, nk = SW_GRP * SW_BLOCK, 3 * SW_BLOCK
    qpos = lax.broadcasted_iota(jnp.int32, (nq, nk), 0) % SW_BLOCK
    krel = lax.broadcasted_iota(jnp.int32, (nq, nk), 1) - SW_BLOCK
    kglob = krel + blk * SW_BLOCK
    valid = (jnp.abs(qpos - krel) <= SW_BLOCK) & (kglob >= 0) & (kglob < t_)
    ss = [jnp.where(valid, _nt(qg * _SW_SCALE, kh), -1e30) for qg, kh in zip(qgs, khs)]
    snks = [jnp.concatenate([jnp.broadcast_to(sinks[:, h:h + 1], (SW_BLOCK, 1)) for h in hs], axis=0) for hs in _KV_HEADS]
    ms = [jnp.maximum(jnp.max(s, axis=-1, keepdims=True), snk) for s, snk in zip(ss, snks)]
    es = [jnp.exp(s - m) for s, m in zip(ss, ms)]
    esnks = [jnp.exp(snk - m) for snk, m in zip(snks, ms)]
    invs = [1.0 / (jnp.sum(e, axis=-1, keepdims=True) + esnk) for e, esnk in zip(es, esnks)]
    ps = [e * inv for e, inv in zip(es, invs)]
    return qgs, khs, ps, [esnk * inv for esnk, inv in zip(esnks, invs)]


def _attn_f(blk, t_, cq, sq, ck, sk, q, kp, ko, kn, vp, vo, vn, sinks):
    _, _, ps, psinks = _attn_probs(blk, t_, cq, sq, ck, sk, q, jnp.concatenate([kp, ko, kn], axis=0), sinks)
    vall = jnp.concatenate([vp, vo, vn], axis=0)
    ps = [p.astype(_MXU) for p in ps]
    kept = [jnp.concatenate([p, jnp.broadcast_to(psink.astype(_MXU), (p.shape[0], 128))], axis=-1)
            for p, psink in zip(ps, psinks)]
    return _from_groups([_nn(p, vall[:, kvh * SW_DIM:(kvh + 1) * SW_DIM]) for kvh, p in enumerate(ps)]), kept


def _attn_bwd_f(cq, sq, ck, sk, q, kp, ko, kn, vp, vo, vn, do, kept):
    nk = 3 * SW_BLOCK
    ps = [x[:, :nk] for x in kept]
    psinks = [x[:, nk:nk + 1].astype(F32) for x in kept]
    qgs = _by_group(_rope(q, cq, sq))
    kr = _rope(jnp.concatenate([kp, ko, kn], axis=0), ck, sk)
    khs = [kr[:, kvh * SW_DIM:(kvh + 1) * SW_DIM] for kvh in range(SW_KV)]
    vall = jnp.concatenate([vp, vo, vn], axis=0)
    vhs = [vall[:, kvh * SW_DIM:(kvh + 1) * SW_DIM] for kvh in range(SW_KV)]
    dogs = _by_group(do)
    dvs = [_tn(p, dog) for p, dog in zip(ps, dogs)]
    ps = [p.astype(F32) for p in ps]
    dps = [_nt(dog * _SW_SCALE, vh) for dog, vh in zip(dogs, vhs)]
    deltas = [jnp.sum(p * dp, axis=-1, keepdims=True) for p, dp in zip(ps, dps)]
    dss = [p * (dp - delta) for p, dp, delta in zip(ps, dps, deltas)]
    dqr = _from_groups([_nn(ds, kh) for ds, kh in zip(dss, khs)])
    dkr = jnp.concatenate([_tn(ds, qg) for ds, qg in zip(dss, qgs)], axis=-1)
    dsnk = [-(psink * delta) * (1.0 / _SW_SCALE) for psink, delta in zip(psinks, deltas)]
    dsinks = jnp.concatenate([jnp.sum(d[g * SW_BLOCK:(g + 1) * SW_BLOCK], axis=0, keepdims=True)
                              for d in dsnk for g in range(SW_GRP)], axis=1)
    dq, dk, dv = _rope_t(dqr, cq, sq), _rope_t(dkr, ck, sk), jnp.concatenate(dvs, axis=-1)
    blocks = lambda a: [a[j * SW_BLOCK:(j + 1) * SW_BLOCK] for j in range(3)]
    return [dq] + blocks(dk) + blocks(dv) + [dsinks]


def _attn_specs(nb):
    prv = lambda i: jnp.maximum(i - 1, 0)
    nxt = lambda i: jnp.minimum(i + 1, nb - 1)
    rows = [lambda i: i, prv, lambda i: i, nxt]
    tab = [pl.BlockSpec((SW_BLOCK, 128), lambda i, r=r: (r(i), 0)) for r in rows]
    qs = pl.BlockSpec((SW_BLOCK, SW_HEADS * SW_DIM), lambda i: (i, O_QS // (SW_HEADS * SW_DIM)))
    kw = SW_KV * SW_DIM
    ks = [pl.BlockSpec((SW_BLOCK, kw), lambda i, r=r: (r(i), O_KS // kw)) for r in rows[1:]]
    vs = [pl.BlockSpec((SW_BLOCK, kw), lambda i, r=r: (r(i), O_VS // kw)) for r in rows[1:]]
    return tab, qs, ks, vs


def _attn_tables(refs):
    cq, cp, co, cn, sq, sp_, so, sn = [r[...] for r in refs]
    return cq, sq, jnp.concatenate([cp, co, cn], axis=0), jnp.concatenate([sp_, so, sn], axis=0)


_P_BLOCK = (None, SW_KV, SW_GRP * SW_BLOCK, 3 * SW_BLOCK + 128)


def _attn_fwd(pm, cos, sin, sinks):
    t_ = pm.shape[0]
    nb = t_ // SW_BLOCK
    tab, qs, ks, vs = _attn_specs(nb)

    def body(*refs):
        tabs = _attn_tables(refs[:8])
        vals = [r[...] for r in refs[8:16]]
        out, ps = _attn_f(pl.program_id(0), t_, *tabs, *vals)
        refs[16][...] = out.astype(refs[16].dtype)
        for kvh, p in enumerate(ps):
            refs[17][kvh] = p

    return pl.pallas_call(
        body, name="attn_fwd", grid=(nb,), in_specs=tab + tab + [qs] + ks + vs + [_full(sinks)],
        out_specs=[pl.BlockSpec((SW_BLOCK, D), lambda i: (i, 0)), pl.BlockSpec(_P_BLOCK, lambda i: (i, 0, 0, 0))],
        out_shape=[jax.ShapeDtypeStruct((t_, D), _MXU), jax.ShapeDtypeStruct((nb,) + _P_BLOCK[1:], _MXU)],
        compiler_params=_params(1))(*([cos] * 4), *([sin] * 4), pm, pm, pm, pm, pm, pm, pm, sinks)


def _attn_bwd(pm, cos, sin, sinks, probs, do, dpm):
    t_ = pm.shape[0]
    nb = t_ // SW_BLOCK
    tab, qs, ks, vs = _attn_specs(nb)
    kw = SW_KV * SW_DIM

    def body(*refs):
        tabs = _attn_tables(refs[:8])
        vals = [r[...] for r in refs[8:15]]
        p_ref, do_ref, outs = refs[16], refs[17], refs[19:]
        grads = _attn_bwd_f(*tabs, *vals, do_ref[...], [p_ref[kvh] for kvh in range(SW_KV)])
        for o_ref, g in zip(outs[:7], grads[:7]):
            o_ref[...] = g.astype(o_ref.dtype)

        @pl.when(pl.program_id(0) == 0)
        def _():
            outs[7][...] = jnp.zeros_like(outs[7])
        outs[7][...] += grads[7]

    own = lambda w: pl.BlockSpec((SW_BLOCK, w), lambda i: (i, 0))
    return pl.pallas_call(
        body, name="attn_bwd", grid=(nb,),
        in_specs=tab + tab + [qs] + ks + vs + [_full(sinks), pl.BlockSpec(_P_BLOCK, lambda i: (i, 0, 0, 0)), own(D), ANY],
        out_specs=[pl.BlockSpec((SW_BLOCK, D), lambda i: (i, O_QS // D))] + [own(kw)] * 6 + [_full(sinks)],
        out_shape=[jax.ShapeDtypeStruct(dpm.shape, dpm.dtype)] + [jax.ShapeDtypeStruct((t_, kw), F32)] * 6
        + [jax.ShapeDtypeStruct(sinks.shape, F32)],
        input_output_aliases={18: 0},
        compiler_params=_params(1))(*([cos] * 4), *([sin] * 4), pm, pm, pm, pm, pm, pm, pm, sinks, probs, do, dpm)


def _band_sum(kparts, vparts, dpm):
    t_, kw = kparts[1].shape
    nb = t_ // SW_BLOCK

    def body(kp, ko, kn, vp, vo, vn, _, out_ref):
        j = pl.program_id(0)
        band = lambda p, o, n: o[...] + jnp.where(j + 1 < nb, p[...], 0.0) + jnp.where(j > 0, n[...], 0.0)
        out_ref[...] = jnp.concatenate([band(kp, ko, kn), band(vp, vo, vn)], axis=-1).astype(out_ref.dtype)

    specs = [pl.BlockSpec((SW_BLOCK, kw), lambda j: (jnp.minimum(j + 1, nb - 1), 0)),
             pl.BlockSpec((SW_BLOCK, kw), lambda j: (j, 0)),
             pl.BlockSpec((SW_BLOCK, kw), lambda j: (jnp.maximum(j - 1, 0), 0))]
    return pl.pallas_call(
        body, name="band_sum", grid=(nb,), in_specs=specs * 2 + [ANY],
        out_specs=pl.BlockSpec((SW_BLOCK, 2 * kw), lambda j: (j, O_KS // (2 * kw))),
        out_shape=jax.ShapeDtypeStruct(dpm.shape, dpm.dtype), input_output_aliases={6: 0},
        compiler_params=_params(1))(*kparts, *vparts, dpm)


def _loss_head(y, target, tb=256):
    t_ = y.shape[0]

    def body(y_ref, t_ref, dy_ref, acc_ref):
        err = y_ref[...] - t_ref[...]
        dy_ref[...] = err * (1.0 / D)
        sq = (err * err).reshape(tb // 8, 8, D).sum(axis=0)
        part = sq[:, 0:128]
        for c in range(1, D // 128):
            part = part + sq[:, c * 128:(c + 1) * 128]

        @pl.when(pl.program_id(0) == 0)
        def _():
            acc_ref[...] = jnp.zeros_like(acc_ref)
        acc_ref[...] += part

    row = pl.BlockSpec((tb, D), lambda i: (i, 0))
    return pl.pallas_call(
        body, name="loss_head", grid=(t_ // tb,), in_specs=[row, row],
        out_specs=[row, pl.BlockSpec((8, 128), lambda i: (0, 0))],
        out_shape=[jax.ShapeDtypeStruct((t_, D), F32), jax.ShapeDtypeStruct((8, 128), F32)],
        compiler_params=_params(1))(y, target)


def _adam_math(w, g, m, v):
    bc1 = 1.0 - ADAM_B1 ** ADAM_STEP
    bc2 = 1.0 - ADAM_B2 ** ADAM_STEP
    nm = ADAM_B1 * m + (1.0 - ADAM_B1) * g
    nv = ADAM_B2 * v + (1.0 - ADAM_B2) * (g * g)
    return -ADAM_LR * ((nm / bc1) / (jnp.sqrt(nv / bc2) + ADAM_EPS) + ADAM_WD * w), nm, nv


def _adamw_layer(name, l, w, g, m, v, carry, after):
    nl, r, c = w.shape
    g2 = g.reshape(r, -1)
    gc = g2.shape[1]
    w2, m2, v2 = [a.reshape(nl * r, c) for a in (w, m, v)]
    tb = r
    while tb * gc * 4 > (1 << 20) and tb % 16 == 0:
        tb //= 2
    nb = r // tb

    def body(w_ref, g_ref, m_ref, v_ref, *rest):
        go_ref, d_ref, nm_ref, nv_ref = rest[-4:]
        gv = g_ref[...][:, :c]
        go_ref[...] = gv
        d_ref[...], nm_ref[...], nv_ref[...] = _adam_math(w_ref[...], gv, m_ref[...], v_ref[...])

    spec = pl.BlockSpec((tb, c), lambda i: (l * nb + i, 0))
    carried = list(carry) if carry is not None else []
    outs = pl.pallas_call(
        body, name=name, grid=(nb,),
        in_specs=[spec, pl.BlockSpec((tb, gc), lambda i: (i, 0)), spec, spec] + [ANY] * (len(carried) + 1),
        out_specs=[spec] * 4, out_shape=[jax.ShapeDtypeStruct((nl * r, c), F32)] * 4,
        input_output_aliases={4 + k: k for k in range(len(carried))},
        compiler_params=_params(1))(w2, g2, m2, v2, *[a.reshape(nl * r, c) for a in carried], after)
    return tuple(o.reshape(w.shape) for o in outs)


def _adamw_layer_t(name, l, w, g, m, v, carry, after):
    nl, r, c = w.shape
    g2 = g.reshape(r, -1)
    gc = g2.shape[1]
    wt, mt, vt = [jnp.swapaxes(a, 1, 2).reshape(nl * c, r) for a in (w, m, v)]

    def body(w_ref, g_ref, m_ref, v_ref, *rest):
        go_ref, d_ref, nm_ref, nv_ref = rest[-4:]
        gv = g_ref[...].T[:c]
        go_ref[...] = gv
        d_ref[...], nm_ref[...], nv_ref[...] = _adam_math(w_ref[...], gv, m_ref[...], v_ref[...])

    spec = pl.BlockSpec((c, 128), lambda j: (l, j))
    carried = list(carry) if carry is not None else []
    return tuple(pl.pallas_call(
        body, name=name, grid=(r // 128,),
        in_specs=[spec, pl.BlockSpec((128, gc), lambda j: (j, 0)), spec, spec] + [ANY] * (len(carried) + 1),
        out_specs=[spec] * 4, out_shape=[jax.ShapeDtypeStruct((nl * c, r), F32)] * 4,
        input_output_aliases={4 + k: k for k in range(len(carried))},
        compiler_params=_params(1))(wt, g2, mt, vt, *carried, after))


def _adamw(name, w, g, m, v):
    shape = w.shape
    cols = shape[-1]
    rows = w.size // cols
    w2, g2, m2, v2 = [a.reshape(rows, cols) for a in (w, g, m, v)]
    tb = rows
    while tb * cols * 4 > (1 << 20) and tb % 16 == 0:
        tb //= 2

    def body(w_ref, g_ref, m_ref, v_ref, d_ref, nm_ref, nv_ref):
        d_ref[...], nm_ref[...], nv_ref[...] = _adam_math(w_ref[...], g_ref[...], m_ref[...], v_ref[...])

    spec = pl.BlockSpec((tb, cols), lambda i: (i, 0))
    sh = jax.ShapeDtypeStruct((rows, cols), F32)
    outs = pl.pallas_call(body, name=name, grid=(rows // tb,), in_specs=[spec] * 4, out_specs=[spec] * 3,
                          out_shape=[sh] * 3, compiler_params=_params(1))(w2, g2, m2, v2)
    return [o.reshape(shape) for o in outs]


def _to_rows(bg, col0):
    t_ = bg.shape[0]
    a = bg[:, col0:col0 + 2 * DN_HEADS].reshape(t_ // CHUNK, CHUNK, 2, DN_HEADS)
    return jnp.transpose(a, (2, 0, 3, 1))


def _from_rows(db, dg):
    nch = db.shape[1]
    back = lambda a: jnp.transpose(a, (1, 3, 0, 2)).reshape(nch * CHUNK, 2 * DN_HEADS)
    return jnp.pad(jnp.concatenate([back(db), back(dg)], axis=1), ((0, 0), (0, 128 - 4 * DN_HEADS)))


def _layer_fwd(x, xm, w, rest, cos, sin):
    t_ = x.shape[0]
    nb = min(1024, t_)
    pm = _mm(xm, w["in_main"], name="mm_in", tn=1536)
    pbg = _mm(xm, w["in_bg"], name="mm_in_bg")
    w = {**w, **rest(pbg)}
    qkv = _prep_fwd(pm, w["conv"])
    bg, = _stage_fwd(_bg_f, "bg_fwd", [(pbg, 128, 0, False)], [w["arow"], w["dtrow"]], [(128, F32)], nb)
    brows, grows = _to_rows(bg, 0), _to_rows(bg, 2 * DN_HEADS)
    ares, tinv = _dn_a_fwd(qkv, grows, brows)
    o_f, o_b, st_f, st_b = _dn_b_fwd(ares, grows)

    def gnorm_all(of, ob, z, gnw):
        return jnp.concatenate([_gnorm_f(of[:, _hs(h)], ob[:, _hs(h)], z[:, _hs(h)], gnw)[0]
                                for h in range(DN_HEADS)], axis=-1)

    odn, ya = _mm_pre(gnorm_all, [(o_f, D, 0), (o_b, D, 0), (pm, D, O_Z // D)], [w["gnw"]], w["a"], name="mm_a_gnorm")
    osw, probs = _attn_fwd(pm, cos, sin, w["sinks"])
    same, first = (lambda j: j), (lambda j: 0)
    full = lambda dt: (D, D, dt, first, None)
    yb, merged = _mm_fused(
        osw, w["b"], lambda acc, ya_, ga, gb: (acc,) + _merge_f(ya_, acc, ga, gb), [full(F32), full(_MXU)],
        name="mm_b_merge", rows=[(ya, D, first), (pm, D, lambda j: O_GA // D), (pm, D, lambda j: O_GB // D)], tm=512)
    mix, x1, x1m = _mm_fused(
        merged, w["o"], lambda acc, x_, g_, b_: (acc,) + _ln_f2(x_, acc, g_, b_), [full(F32), full(F32), full(_MXU)],
        name="mm_o_ln", rows=[(x, D, first)], params=[w["ln1g"], w["ln1b"]], tm=512)
    gu, hid = _mm_fused(x1m, w["gu"], lambda acc: (acc,) + _swiglu_tile(acc),
                        [(2 * FFN, FFN, F32, same, None), (FFN, FFN // 2, _MXU, same, None)], name="mm_gu_swiglu", tn=FFN)
    ffn, x2, x2m = _mm_fused(
        hid, w["d"], lambda acc, x_, g_, b_: (acc,) + _ln_f2(x_, acc, g_, b_), [full(F32), full(F32), full(_MXU)],
        name="mm_d_ln", rows=[(x1, D, first)], params=[w["ln2g"], w["ln2b"]], tm=512, tk=FFN)
    res = dict(w=w, x=x, xm=xm, pm=pm, pbg=pbg, qkv=qkv, grows=grows, brows=brows, ares=ares, tinv=tinv, o_f=o_f, o_b=o_b, st_f=st_f,
               st_b=st_b, odn=odn, osw=osw, probs=probs, ya=ya, yb=yb, merged=merged, mix=mix, x1=x1, x1m=x1m, gu=gu, hid=hid,
               ffn=ffn)
    return x2, x2m, res


def _layer_bwd(dx2, r, w, cos, sin, mid=None, after=None):
    t_ = dx2.shape[0]
    nb = min(1024, t_)
    pm = r["pm"]
    g = {}
    dx1a, dffn, g["ln2g"], g["ln2b"] = _stage_bwd(
        _ln_f, "ln2_bwd", [(r["x1"], D, 0, False), (r["ffn"], D, 0, False)], [w["ln2g"], w["ln2b"]], [dx2],
        min(512, t_), dtypes=[F32, _MXU], after=after)
    same = lambda j: j
    dgu, = _mm_fused(dffn, w["d"], lambda dhid, gu: jax.vjp(_swiglu_tile, gu)[1]((dhid,)),
                     [(2 * FFN, FFN, _MXU, same, None)], name="mm_d_dx_swiglu", rows=[(r["gu"], FFN, same)], tb=True,
                     tn=FFN // 2)
    g["d"] = _mm(r["hid"], dffn, ta=True, name="mm_d_dw", tm=FFN // 2, out_dtype=_MXU)
    first = lambda j: 0
    full = lambda dt: (D, D, dt, first, None)

    def ln1_back(acc, dx1a_, x_, mix_, g_, b_):
        dx_, dmix_, dg_, db_ = jax.vjp(_ln_f, x_, mix_, g_, b_)[1]((acc + dx1a_,))
        return dx_, dmix_, dg_, db_

    dxa, dmix, g["ln1g"], g["ln1b"] = _mm_fused(
        dgu, w["gu"], ln1_back, [full(F32), full(_MXU)], name="mm_gu_dx_ln", tb=True, tm=256, tk=2 * FFN, n_sums=2,
        rows=[(dx1a, D, first), (r["x"], D, first), (r["mix"], D, first)], params=[w["ln1g"], w["ln1b"]])
    g["gu"] = _mm(r["x1m"], dgu, ta=True, name="mm_gu_dw", tn=FFN // 2, out_dtype=_MXU)
    tok = mid(g["gu"]) if mid is not None else None
    g["o"] = _mm(r["merged"], dmix, ta=True, name="mm_o_dw", after=tok, out_dtype=_MXU)

    def merge_back(acc, ya_, yb_, ga, gb):
        dya_, dyb_, dga, dgb = jax.vjp(_merge_f, ya_, yb_, ga, gb)[1]((acc,))
        return dya_, dyb_, jnp.concatenate([dga, dgb], axis=-1)

    dya, dyb, dpm = _mm_fused(
        dmix, w["o"], merge_back, [full(_MXU), full(_MXU), (N_MAIN, 2 * D, _MXU, lambda j: O_GA // (2 * D), None)],
        name="mm_o_dx_merge", tb=True, tm=512,
        rows=[(r["ya"], D, first), (r["yb"], D, first), (pm, D, lambda j: O_GA // D), (pm, D, lambda j: O_GB // D)])

    def gnorm_back(acc, of, ob, z, gnw):
        dos, dzs, dws = [], [], None
        for h in range(DN_HEADS):
            do_h, _, dz_h, dw_h = jax.vjp(_gnorm_f, of[:, _hs(h)], ob[:, _hs(h)], z[:, _hs(h)], gnw)[1]((acc[:, _hs(h)],))
            dos.append(do_h)
            dzs.append(dz_h)
            dws = dw_h if dws is None else dws + dw_h
        return jnp.concatenate(dos, axis=-1), jnp.concatenate(dzs, axis=-1), dws

    dof, dpm, g["gnw"] = _mm_fused(
        dya, w["a"], gnorm_back, [full(F32), (N_MAIN, D, _MXU, lambda j: O_Z // D, dpm)], name="mm_a_dx_gnorm", tb=True,
        tm=512, rows=[(r["o_f"], D, first), (r["o_b"], D, first), (pm, D, lambda j: O_Z // D)], params=[w["gnw"]],
        n_sums=1)
    g["a"] = _mm(r["odn"], dya, ta=True, name="mm_a_dw", out_dtype=_MXU)
    dosw = _mm(dyb, w["b"], tb=True, name="mm_b_dx")
    g["b"] = _mm(r["osw"], dyb, ta=True, name="mm_b_dw", out_dtype=_MXU)
    ab = _attn_bwd(pm, cos, sin, w["sinks"], r["probs"], dosw, dpm)
    dpm, g["sinks"] = ab[0], ab[7]
    dpm = _band_sum(ab[1:4], ab[4:7], dpm)
    dares, dg_b = _dn_b_bwd(r["ares"], r["grows"], r["st_f"], r["st_b"], dof)
    dqkv, dgrows, dbrows = _dn_a_bwd(r["qkv"], r["grows"], r["brows"], r["tinv"], dares, dg_b)
    dbg = _from_rows(dbrows, dgrows)
    dpbg, g["arow"], g["dtrow"] = _stage_bwd(_bg_f, "bg_bwd", [(r["pbg"], 128, 0, False)], [w["arow"], w["dtrow"]],
                                             [dbg], nb)
    dpm, g["conv"] = _prep_bwd(pm, w["conv"], dqkv, dpm)
    dx = _mm(dpm, w["in_main"], tb=True, add=dxa, name="mm_in_dx", tk=1920)
    dx = _mm(dpbg, w["in_bg"], tb=True, add=dx, name="mm_in_bg_dx")
    g["in_main"] = _mm(r["xm"], dpm, ta=True, name="mm_in_dw", tn=1536, out_dtype=_MXU)
    g["in_bg"] = _mm(r["xm"], dpbg, ta=True, name="mm_in_bg_dw", out_dtype=_MXU)
    return dx, g


def _place():
    return lax.axis_index("x"), lax.axis_index("y"), lax.axis_index("c")


def _colblock(kind, q):
    return q if kind == "col" else (q >> 1) | ((q & 1) << 1)


def _other_chips(x, y):
    return [(1 - x, y), (x, 1 - y), (1 - x, 1 - y)]


HBM = pl.BlockSpec(memory_space=pltpu.HBM)
SEM = pl.BlockSpec(memory_space=pltpu.SEMAPHORE)
DATAFLOW = pltpu.SideEffectType.DATAFLOW_SIDE_EFFECTING


def _hbm(a):
    return pltpu.HBM(a.shape, a.dtype)


def _gather_start(locs, kinds, name):
    n = len(locs)
    locs = list(locs)
    lands = [lax.empty((N_CHIPS,) + a.shape if kind == "row" else (a.shape[0], N_CHIPS * a.shape[1]), a.dtype)
             for a, kind in zip(locs, kinds)]

    def body(*refs):
        loc_refs, land_refs = refs[:n], refs[n:2 * n]
        send_sems, recv_sems, token = refs[2 * n:3 * n], refs[3 * n:4 * n], refs[-1]
        x, y, c = _place()
        myq = 2 * x + y
        for t in range(n):
            width = locs[t].shape[1]
            mine = (land_refs[t].at[myq] if kinds[t] == "row" else
                    land_refs[t].at[:, pl.ds(pl.multiple_of(_colblock(kinds[t], myq) * width, 128), width)])
            for dev in [(cx, cy, c) for cx, cy in _other_chips(x, y)] + [(x, y, 1 - c)]:
                pltpu.make_async_remote_copy(src_ref=loc_refs[t], dst_ref=mine, send_sem=send_sems[t],
                                             recv_sem=recv_sems[t], device_id=dev, device_id_type=MESH).start()
        token[...] = jnp.zeros_like(token)

    res = pl.pallas_call(
        body, name=name,
        out_shape=[pltpu.SemaphoreType.DMA(())] * (2 * n) + [_hbm(a) for a in locs + lands]
        + [jax.ShapeDtypeStruct((8, 128), F32)],
        in_specs=[HBM] * (2 * n), out_specs=[SEM] * (2 * n) + [HBM] * (2 * n) + [pl.BlockSpec(memory_space=pltpu.VMEM)],
        input_output_aliases={t: 2 * n + t for t in range(2 * n)},
        compiler_params=pltpu.CompilerParams(has_side_effects=DATAFLOW),
    )(*[pltpu.with_memory_space_constraint(a, pltpu.HBM) for a in locs + lands])
    return (res[:n], res[n:2 * n], res[2 * n:3 * n], res[3 * n:4 * n]), res[-1]


def _split_wait(handle, after, name):
    send_sems, recv_sems, srcs, lands = handle
    n = len(srcs)

    def body(*refs):
        land_refs, ssems, rsems = refs[n:2 * n], refs[2 * n:3 * n], refs[3 * n:4 * n]
        x, y, c = _place()
        for t in range(n):
            done = pltpu.make_async_remote_copy(
                src_ref=land_refs[t], dst_ref=land_refs[t], send_sem=ssems[t], recv_sem=rsems[t],
                device_id=(x, y, c), device_id_type=MESH)
            done.wait_send()
            done.wait_recv()

    res = pl.pallas_call(
        body, name=name, out_shape=[_hbm(a) for a in list(srcs) + list(lands)],
        in_specs=[HBM] * (2 * n) + [SEM] * (2 * n) + [ANY], out_specs=[HBM] * (2 * n),
        input_output_aliases={t: t for t in range(2 * n)},
        compiler_params=pltpu.CompilerParams(has_side_effects=DATAFLOW),
    )(*srcs, *lands, *send_sems, *recv_sems, after)
    return res[:n], res[n:]


RS_CHUNKS = 2


def _piece(ref, kind, q, hf, pr, pc):
    if kind == "row":
        return ref.at[pl.ds((2 * q + hf) * pr, pr), :]
    return ref.at[pl.ds(hf * pr, pr), pl.ds(pl.multiple_of(_colblock(kind, q) * pc, 128), pc)]


def _rs_sibling_start(ts, meta, name):
    n = len(ts)
    ts = list(ts)
    lands = [lax.empty((N_CHIPS, pr, pc), a.dtype) for a, (_, pr, pc) in zip(ts, meta)]

    def body(*refs):
        t_refs, land_refs = refs[:n], refs[n:2 * n]
        send_sems, recv_sems, token = refs[2 * n:3 * n], refs[3 * n:4 * n], refs[-1]
        x, y, c = _place()
        for t, (kind, pr, pc) in enumerate(meta):
            for q in range(N_CHIPS):
                pltpu.make_async_remote_copy(
                    src_ref=_piece(t_refs[t], kind, q, 1 - c, pr, pc), dst_ref=land_refs[t].at[q],
                    send_sem=send_sems[t], recv_sem=recv_sems[t], device_id=(x, y, 1 - c), device_id_type=MESH).start()
        token[...] = jnp.zeros_like(token)

    res = pl.pallas_call(
        body, name=name,
        out_shape=[pltpu.SemaphoreType.DMA(())] * (2 * n) + [_hbm(a) for a in ts + lands]
        + [jax.ShapeDtypeStruct((8, 128), F32)],
        in_specs=[HBM] * (2 * n), out_specs=[SEM] * (2 * n) + [HBM] * (2 * n) + [pl.BlockSpec(memory_space=pltpu.VMEM)],
        input_output_aliases={t: 2 * n + t for t in range(2 * n)},
        compiler_params=pltpu.CompilerParams(has_side_effects=DATAFLOW),
    )(*[pltpu.with_memory_space_constraint(a, pltpu.HBM) for a in ts + lands])
    return (res[:n], res[n:2 * n], res[2 * n:3 * n], res[3 * n:4 * n]), res[-1]


def _rs_add_sibling(ts, r1s, meta, c):
    n = len(ts)
    in_specs, out_specs, out_shape = [], [], []
    for kind, pr, pc in meta:
        rs = pr // RS_CHUNKS
        if kind == "row":
            in_specs.append(pl.BlockSpec((rs, pc), lambda q, r, c_ref: ((2 * q + c_ref[0]) * RS_CHUNKS + r, 0)))
        else:
            in_specs.append(pl.BlockSpec(
                (rs, pc), lambda q, r, c_ref, kind=kind: (c_ref[0] * RS_CHUNKS + r, _colblock(kind, q))))
    for kind, pr, pc in meta:
        sp = pl.BlockSpec((None, pr // RS_CHUNKS, pc), lambda q, r, c_ref: (q, r, 0))
        in_specs.append(sp)
        out_specs.append(sp)
        out_shape.append(jax.ShapeDtypeStruct((N_CHIPS, pr, pc), BF16))

    def body(c_ref, *refs):
        for t in range(n):
            refs[2 * n + t][...] = (refs[t][...].astype(F32) + refs[n + t][...].astype(F32)).astype(BF16)

    return pl.pallas_call(
        body, name="rs_add_sibling", out_shape=out_shape,
        grid_spec=pltpu.PrefetchScalarGridSpec(num_scalar_prefetch=1, grid=(N_CHIPS, RS_CHUNKS), in_specs=in_specs,
                                               out_specs=out_specs),
        compiler_params=_params(2))(c.reshape(1).astype(jnp.int32), *ts, *r1s)


def _rs_chips_start(ps, meta, name):
    n = len(ps)
    ps = list(ps)
    lands = [lax.empty((N_CHIPS - 1, pr, pc), p.dtype) for p, (_, pr, pc) in zip(ps, meta)]

    def body(*refs):
        p_refs, land_refs = refs[:n], refs[n:2 * n]
        send_sems, recv_sems, token = refs[2 * n:3 * n], refs[3 * n:4 * n], refs[-1]
        x, y, c = _place()
        for t in range(n):
            for j, (cx, cy) in enumerate(_other_chips(x, y)):
                pltpu.make_async_remote_copy(
                    src_ref=p_refs[t].at[2 * cx + cy], dst_ref=land_refs[t].at[j], send_sem=send_sems[t],
                    recv_sem=recv_sems[t], device_id=(cx, cy, c), device_id_type=MESH).start()
        token[...] = jnp.zeros_like(token)

    res = pl.pallas_call(
        body, name=name,
        out_shape=[pltpu.SemaphoreType.DMA(())] * (2 * n) + [_hbm(a) for a in ps + lands]
        + [jax.ShapeDtypeStruct((8, 128), F32)],
        in_specs=[HBM] * (2 * n), out_specs=[SEM] * (2 * n) + [HBM] * (2 * n) + [pl.BlockSpec(memory_space=pltpu.VMEM)],
        input_output_aliases={t: 2 * n + t for t in range(2 * n)},
        compiler_params=pltpu.CompilerParams(has_side_effects=DATAFLOW),
    )(*[pltpu.with_memory_space_constraint(a, pltpu.HBM) for a in ps + lands])
    return (res[:n], res[n:2 * n], res[2 * n:3 * n], res[3 * n:4 * n]), res[-1]


def _rs_add_chips(ps, r2s, meta, myq, c):
    n = len(ps)
    in_specs, out_specs, out_shape = [], [], []
    for _, pr, pc in meta:
        in_specs.append(pl.BlockSpec((None, pr // RS_CHUNKS, pc), lambda r, q_ref, c_ref: (q_ref[0], r, 0)))
    for _, pr, pc in meta:
        in_specs.append(pl.BlockSpec((N_CHIPS - 1, pr // RS_CHUNKS, pc), lambda r, q_ref, c_ref: (0, r, 0)))
        out_specs.append(pl.BlockSpec((None, pr // RS_CHUNKS, pc), lambda r, q_ref, c_ref: (c_ref[0], r, 0)))
        out_shape.append(jax.ShapeDtypeStruct((2, pr, pc), F32))

    def body(q_ref, c_ref, *refs):
        for t in range(n):
            r2 = refs[n + t]
            own = refs[t][...].astype(F32)
            refs[2 * n + t][...] = ((own + r2[0].astype(F32)) + r2[1].astype(F32)) + r2[2].astype(F32)

    return pl.pallas_call(
        body, name="rs_add_chips", out_shape=out_shape,
        grid_spec=pltpu.PrefetchScalarGridSpec(num_scalar_prefetch=2, grid=(RS_CHUNKS,), in_specs=in_specs,
                                               out_specs=out_specs),
        compiler_params=_params(1))(myq.reshape(1).astype(jnp.int32), c.reshape(1).astype(jnp.int32), *ps, *r2s)


def _rs_share_halves(gs):
    n = len(gs)

    def body(*refs):
        g_refs, send_sems, recv_sems = refs[n:2 * n], refs[2 * n], refs[2 * n + 1]
        x, y, c = _place()
        sib = (x, y, 1 - c)
        for t in range(n):
            pltpu.make_async_remote_copy(
                src_ref=g_refs[t].at[c], dst_ref=g_refs[t].at[c], send_sem=send_sems.at[t], recv_sem=recv_sems.at[t],
                device_id=sib, device_id_type=MESH).start()
        for t in range(n):
            cp = pltpu.make_async_remote_copy(
                src_ref=g_refs[t].at[c], dst_ref=g_refs[t].at[1 - c], send_sem=send_sems.at[t],
                recv_sem=recv_sems.at[t], device_id=sib, device_id_type=MESH)
            cp.wait_send()
            cp.wait_recv()

    return pl.pallas_call(
        body, name="rs_share_halves", in_specs=[ANY] * n, out_specs=[ANY] * n,
        out_shape=[jax.ShapeDtypeStruct(g.shape, g.dtype) for g in gs], input_output_aliases={t: t for t in range(n)},
        scratch_shapes=[pltpu.SemaphoreType.DMA((n,)), pltpu.SemaphoreType.DMA((n,))],
    )(*gs)


def _rs_middle(handle, after, meta, c, name):
    ts, r1s = _split_wait(handle, after, name + "_sib_wait")
    ps = _rs_add_sibling(ts, r1s, meta, c)
    return _rs_chips_start(ps, meta, name + "_start")


def _rs_end(handle, after, meta, c, myq, name):
    ps, r2s = _split_wait(handle, after, name + "_wait")
    return _rs_share_halves(_rs_add_chips(ps, r2s, meta, myq, c))


def _allreduce_small(buf):
    rows = buf.shape[0]
    ndev = 8

    def body(b_ref, o_ref, slots, send_sems, recv_sems):
        x, y, c = _place()
        me = 4 * x + 2 * y + c
        slots[me] = b_ref[...]
        for k in range(1, ndev):
            kx, ky, kc = (k >> 2) & 1, (k >> 1) & 1, k & 1
            peer = (x ^ kx, y ^ ky, c ^ kc)
            pltpu.make_async_remote_copy(
                src_ref=b_ref, dst_ref=slots.at[me], send_sem=send_sems.at[k - 1], recv_sem=recv_sems.at[k - 1],
                device_id=peer, device_id_type=MESH).start()
        for k in range(1, ndev):
            kx, ky, kc = (k >> 2) & 1, (k >> 1) & 1, k & 1
            cp = pltpu.make_async_remote_copy(
                src_ref=b_ref, dst_ref=slots.at[me ^ k], send_sem=send_sems.at[k - 1], recv_sem=recv_sems.at[k - 1],
                device_id=(x ^ kx, y ^ ky, c ^ kc), device_id_type=MESH)
            cp.wait_send()
            cp.wait_recv()
        acc = slots[0]
        for s in range(1, ndev):
            acc = acc + slots[s]
        o_ref[...] = acc

    vm = pl.BlockSpec(memory_space=pltpu.VMEM)
    return pl.pallas_call(
        body, name="allreduce_small", in_specs=[vm], out_specs=vm, out_shape=jax.ShapeDtypeStruct((rows, 128), F32),
        scratch_shapes=[pltpu.VMEM((ndev, rows, 128), F32), pltpu.SemaphoreType.DMA((ndev - 1,)),
                        pltpu.SemaphoreType.DMA((ndev - 1,))],
        compiler_params=pltpu.CompilerParams(vmem_limit_bytes=VMEM_LIMIT))(buf)


RS_META = [("col", D // 2, IN_PAD), ("row", D // 8, D), ("row", D // 8, D), ("row", D // 8, D),
           ("colx", D // 2, 2 * FFN // N_CHIPS), ("row", FFN // 8, D)]
SMALL_ROWS = 156


def _rope_tables(t_):
    half = SW_DIM // 2
    inv_freq = ROPE_THETA ** (-jnp.arange(half, dtype=F32) / half)
    ang = jnp.arange(t_, dtype=F32)[:, None] * inv_freq[None, :]
    reps = 128 // half
    return jnp.concatenate([jnp.cos(ang)] * reps, axis=1), jnp.concatenate([jnp.sin(ang)] * reps, axis=1)


def _orig_cols(padded, a, b):
    out = []
    for q in range(N_CHIPS):
        lo, hi = max(a, q * IN_SHARD), min(b, (q + 1) * IN_SHARD)
        if lo < hi:
            out.append(padded[:, q * IN_PAD + lo - q * IN_SHARD:q * IN_PAD + hi - q * IN_SHARD])
    return out


_ORIG_SEGMENTS = [(0, R_BG, "main", 0), (R_BG, R_SW, "bg", 0), (R_SW, R_GATES, "main", O_QS), (R_GATES, IN_COLS, "main", O_GA)]


def _to_padded_shards(main, bg):
    zeros = jnp.zeros((main.shape[0], IN_PAD - IN_SHARD), main.dtype)
    parts = []
    for q in range(N_CHIPS):
        for a, b, src, s0 in _ORIG_SEGMENTS:
            lo, hi = max(a, q * IN_SHARD), min(b, (q + 1) * IN_SHARD)
            if lo < hi:
                parts.append((main if src == "main" else bg)[:, s0 + lo - a:s0 + hi - a])
        parts.append(zeros)
    return jnp.concatenate(parts, axis=1)


def _lane_row(v16):
    return jnp.pad(v16.reshape(1, 2 * DN_HEADS), ((0, 0), (2 * DN_HEADS, 128 - 4 * DN_HEADS)))


def _pack_small(g):
    pad16 = jnp.pad(g["sinks"], ((0, 0), (0, 128 - SW_HEADS)))
    return jnp.concatenate([g["conv"].reshape(-1, 128), g["ln1g"].reshape(-1, 128), g["ln1b"].reshape(-1, 128),
                            g["ln2g"].reshape(-1, 128), g["ln2b"].reshape(-1, 128), g["gnw"], g["arow"], g["dtrow"],
                            pad16], axis=0)


def _unpack_small(buf):
    nconv = DN_CONV * 3 * D // 128
    o = nconv
    out = dict(conv=buf[:o].reshape(DN_CONV, 3 * D))
    for name in ("ln1g", "ln1b", "ln2g", "ln2b"):
        out[name] = buf[o:o + 8].reshape(D)
        o += 8
    out["gnw"] = buf[o]
    out["a_log"] = buf[o + 1, 2 * DN_HEADS:4 * DN_HEADS].reshape(2, DN_HEADS)
    out["dt_bias"] = buf[o + 2, 2 * DN_HEADS:4 * DN_HEADS].reshape(2, DN_HEADS)
    out["sinks"] = buf[o + 3, :SW_HEADS]
    return out


def kernel(x, w_in, conv_w, a_log, dt_bias, dn_norm_w, sinks, w_branch_a, w_branch_b, w_out, ln1_g, ln1_b, w_gate_up, w_down, ln2_g, ln2_b, loss_target, m_w_in, m_conv_w, m_a_log, m_dt_bias, m_dn_norm_w, m_sinks, m_w_branch_a, m_w_branch_b, m_w_out, m_ln1_g, m_ln1_b, m_w_gate_up, m_w_down, m_ln2_g, m_ln2_b, v_w_in, v_conv_w, v_a_log, v_dt_bias, v_dn_norm_w, v_sinks, v_w_branch_a, v_w_branch_b, v_w_out, v_ln1_g, v_ln1_b, v_w_gate_up, v_w_down, v_ln2_g, v_ln2_b):
    xi, yi, ci = _place()
    myq = 2 * xi + yi
    t_ = x.shape[1]
    cos, sin = _rope_tables(t_)

    kinds = ["col", "row", "row", "row", "colx", "row", "col"]
    gathers = []
    for l in range(DEPTH):
        srcs = [jnp.pad(w_in[l].astype(BF16), ((0, 0), (0, IN_PAD - IN_SHARD)))]
        srcs += [a[l].astype(BF16) for a in (w_branch_a, w_branch_b, w_out, w_gate_up, w_down)] + [conv_w[l]]
        if gathers:
            srcs[-1] = srcs[-1] + gathers[-1][1][0, 0]
        gathers.append(_gather_start(srcs, kinds, "gather_%d_start" % l))

    def in_weights(l, after):
        _, (full_in,) = _split_wait(tuple(part[:1] for part in gathers[l][0]), after, "gather_%d_wait_in" % l)
        cols = lambda a, b: _orig_cols(full_in, a, b)
        return dict(
            in_main=jnp.concatenate(cols(0, R_BG) + cols(R_GATES, IN_COLS) + cols(R_SW, R_GATES), axis=1),
            in_bg=jnp.pad(jnp.concatenate(cols(R_BG, R_SW), axis=1), ((0, 0), (0, 128 - 4 * DN_HEADS))))

    def rest_weights(l, after):
        _, (full_a, full_b, full_o, full_gu, full_d, full_conv) = _split_wait(
            tuple(part[1:] for part in gathers[l][0]), after, "gather_%d_wait_rest" % l)
        return dict(
            conv=full_conv, arow=_lane_row(a_log[l]), dtrow=_lane_row(dt_bias[l]), gnw=dn_norm_w[l][None],
            sinks=sinks[l][None], a=full_a.reshape(D, D), b=full_b.reshape(D, D), o=full_o.reshape(D, D),
            ln1g=ln1_g[l][None], ln1b=ln1_b[l][None], gu=full_gu, d=full_d.reshape(FFN, D),
            ln2g=ln2_g[l][None], ln2b=ln2_b[l][None])

    h = x[0]
    hm = h.astype(_MXU)
    residuals = []
    for l in range(DEPTH):
        win = in_weights(l, gathers[-1][1] if l == 0 else h)
        h, hm, res = _layer_fwd(h, hm, win, functools.partial(rest_weights, l), cos, sin)
        residuals.append(res)
    dh, sq = _loss_head(h, loss_target[0])
    loss_rows = jnp.pad(((0.5 / D) * jnp.sum(sq)).reshape(1, 1), ((0, 7), (0, 127)))

    big = [None] * DEPTH
    small = [None] * DEPTH
    hop1 = hop2 = None

    def mid(after):
        nonlocal hop1, hop2
        if hop1 is None:
            return None
        handle, tok = _rs_middle(hop1[1], after, RS_META, ci, "rs_chips_%d" % hop1[0])
        hop1, hop2 = None, (hop1[0], handle)
        return tok

    token = None
    for l in reversed(range(DEPTH)):
        dh, g = _layer_bwd(dh, residuals[l], residuals[l]["w"], cos, sin, mid, token)
        if hop2 is not None:
            big[hop2[0]] = _rs_end(hop2[1], dh, RS_META, ci, myq, "rs_chips_%d" % hop2[0])
        g_in = _to_padded_shards(g["in_main"], g["in_bg"])
        handle, token = _rs_sibling_start([g_in, g["a"], g["b"], g["o"], g["gu"], g["d"]], RS_META, "rs_sib_%d_start" % l)
        hop1 = (l, handle)
        small[l] = _pack_small(g)
    tot = _allreduce_small(jnp.concatenate(small + [loss_rows], axis=0))
    loss = tot[DEPTH * SMALL_ROWS, 0]
    token = mid(tot)
    pending = hop2
    sm = [_unpack_small(tot[l * SMALL_ROWS:(l + 1) * SMALL_ROWS]) for l in range(DEPTH)]
    stack = lambda name: jnp.stack([s[name] for s in sm], axis=0)
    grads = dict(
        conv_w=lax.dynamic_slice_in_dim(stack("conv"), myq * (3 * D // N_CHIPS), 3 * D // N_CHIPS, axis=2),
        a_log=stack("a_log"), dt_bias=stack("dt_bias"), dn_norm_w=stack("gnw"), sinks=stack("sinks"),
        ln1_g=stack("ln1g"), ln1_b=stack("ln1b"), ln2_g=stack("ln2g"), ln2_b=stack("ln2b"))
    weights = dict(w_in=w_in, conv_w=conv_w, a_log=a_log, dt_bias=dt_bias, dn_norm_w=dn_norm_w, sinks=sinks,
                   w_branch_a=w_branch_a, w_branch_b=w_branch_b, w_out=w_out, ln1_g=ln1_g, ln1_b=ln1_b,
                   w_gate_up=w_gate_up, w_down=w_down, ln2_g=ln2_g, ln2_b=ln2_b)
    ms = dict(w_in=m_w_in, conv_w=m_conv_w, a_log=m_a_log, dt_bias=m_dt_bias, dn_norm_w=m_dn_norm_w, sinks=m_sinks,
              w_branch_a=m_w_branch_a, w_branch_b=m_w_branch_b, w_out=m_w_out, ln1_g=m_ln1_g, ln1_b=m_ln1_b,
              w_gate_up=m_w_gate_up, w_down=m_w_down, ln2_g=m_ln2_g, ln2_b=m_ln2_b)
    vs = dict(w_in=v_w_in, conv_w=v_conv_w, a_log=v_a_log, dt_bias=v_dt_bias, dn_norm_w=v_dn_norm_w, sinks=v_sinks,
              w_branch_a=v_w_branch_a, w_branch_b=v_w_branch_b, w_out=v_w_out, ln1_g=v_ln1_g, ln1_b=v_ln1_b,
              w_gate_up=v_w_gate_up, w_down=v_w_down, ln2_g=v_ln2_g, ln2_b=v_ln2_b)
    names = list(weights)
    upd = {n: _adamw("adamw_" + n, weights[n], grads[n], ms[n], vs[n]) for n in grads}
    big_names = ["w_in", "w_branch_a", "w_branch_b", "w_out", "w_gate_up", "w_down"]
    carry = {n: None for n in big_names}

    def update_layer(l):
        for t, n in enumerate(big_names):
            step = _adamw_layer_t if weights[n].shape[-1] % 128 else _adamw_layer
            carry[n] = step("adamw_" + n, l, weights[n], big[l][t], ms[n], vs[n], carry[n], token)

    for l in range(DEPTH - 1, pending[0], -1):
        update_layer(l)
    big[pending[0]] = _rs_end(pending[1], carry[big_names[-1]][1], RS_META, ci, myq, "rs_chips_%d" % pending[0])
    update_layer(pending[0])
    for n in big_names:
        if weights[n].shape[-1] % 128:
            nl, r, c = weights[n].shape
            carry[n] = tuple(jnp.swapaxes(a.reshape(nl, c, r), 1, 2) for a in carry[n])
        grads[n], upd[n] = carry[n][0], carry[n][1:]
    return (loss, dh[None], *[grads[n] for n in names], *[upd[n][0] for n in names], *[upd[n][1] for n in names],
            *[upd[n][2] for n in names])
```

```python
import functools

import jax
import jax.numpy as jnp
from jax import lax
from jax.experimental import pallas as pl
from jax.experimental.pallas import tpu as pltpu

F32 = jnp.float32
BF16 = jnp.bfloat16
_MXU = BF16

D = 1024
DEPTH = 4
DN_HEADS = 8
DN_DIM = 128
DN_CONV = 5
CHUNK = 64
SW_HEADS = 16
SW_KV = 4
SW_DIM = 64
SW_GRP = SW_HEADS // SW_KV
SW_BLOCK = 128
ROPE_THETA = 10000.0
FFN = 2816
ALPHA = (2.0 * DEPTH) ** 0.25
LN_EPS = 1e-5
RMS_EPS = 1e-6
IN_COLS = 7712
O_Z, O_GA, O_GB, O_QS, O_KS, O_VS, N_MAIN = 3072, 4096, 5120, 6144, 7168, 7424, 7680
R_BG, R_SW, R_GATES = 4096, 4128, 5664
N_CHIPS = 4
IN_SHARD = IN_COLS // N_CHIPS
IN_PAD = 2048
ADAM_LR, ADAM_B1, ADAM_B2, ADAM_EPS, ADAM_WD, ADAM_STEP = 0.001, 0.9, 0.999, 1e-08, 0.01, 10
VMEM_LIMIT = 52 * 1024 * 1024
MESH = pl.DeviceIdType.MESH
ANY = pl.BlockSpec(memory_space=pl.ANY)


def _params(n_grid, **kw):
    return pltpu.CompilerParams(dimension_semantics=("arbitrary",) * n_grid, vmem_limit_bytes=VMEM_LIMIT, **kw)


def _full(a):
    nd = a.ndim
    return pl.BlockSpec(a.shape, lambda *_, nd=nd: (0,) * nd)


def _raw_dot(a, b, ca, cb):
    return lax.dot_general(a.astype(_MXU), b.astype(_MXU), (((ca,), (cb,)), ((), ())), preferred_element_type=F32)


@jax.custom_vjp
def _nn(a, b):
    return _raw_dot(a, b, 1, 0)


@jax.custom_vjp
def _nt(a, b):
    return _raw_dot(a, b, 1, 1)


@jax.custom_vjp
def _tn(a, b):
    if a.shape[1] > b.shape[1]:
        return _raw_dot(b, a, 0, 0).T
    return _raw_dot(a, b, 0, 0)


_nn.defvjp(lambda a, b: (_nn(a, b), (a, b)), lambda r, g: (_nt(g, r[1]), _tn(r[0], g)))
_nt.defvjp(lambda a, b: (_nt(a, b), (a, b)), lambda r, g: (_nn(g, r[1]), _tn(g, r[0])))
_tn.defvjp(lambda a, b: (_tn(a, b), (a, b)), lambda r, g: (_nt(r[1], g), _nn(r[0], g)))


def _hdot(a, b, ca=1, cb=0):
    ah, bh = a.astype(BF16), b.astype(BF16)
    al, bl = (a - ah.astype(F32)).astype(BF16), (b - bh.astype(F32)).astype(BF16)
    dot = lambda u, v: lax.dot_general(u, v, (((ca,), (cb,)), ((), ())), preferred_element_type=F32)
    return dot(ah, bh) + (dot(ah, bl) + dot(al, bh))


def _inv_impl(mats):
    n = mats[0].shape[0]
    eye = (lax.broadcasted_iota(jnp.int32, (n, n), 0) == lax.broadcasted_iota(jnp.int32, (n, n), 1)).astype(F32)
    ps = [-a for a in mats]
    ts = [eye + p for p in ps]
    for _ in range(max(1, (n - 1).bit_length()) - 1):
        ps = [_hdot(p, p) for p in ps]
        ts = [t + _hdot(t, p) for t, p in zip(ts, ps)]
    return tuple(ts)


@jax.custom_vjp
def _inv(mats):
    return _inv_impl(mats)


def _inv_fwd(mats):
    ts = _inv_impl(mats)
    return ts, ts


def _inv_bwd(ts, gs):
    xs = [_hdot(t, g, 0, 0) for t, g in zip(ts, gs)]
    return (tuple(-_hdot(x, t, 1, 1) for x, t in zip(xs, ts)),)


_inv.defvjp(_inv_fwd, _inv_bwd)


@jax.custom_vjp
def _inv_saved(mats, saved):
    return saved


_inv_saved.defvjp(lambda mats, saved: (saved, saved),
                  lambda ts, gs: (_inv_bwd(ts, gs)[0], tuple(jnp.zeros_like(t) for t in ts)))


def _tile(n, cap):
    if n <= cap:
        return n
    best = [t for t in range(128, cap + 1, 128) if n % t == 0]
    assert best, (n, cap)
    return best[-1]


def _mm(a, b, *, name, ta=False, tb=False, add=None, tm=1024, tn=1024, tk=1024, after=None, out_dtype=F32):
    if ta:
        k_, m_ = a.shape
    else:
        m_, k_ = a.shape
    n_ = b.shape[0] if tb else b.shape[1]
    tm, tn, tk = _tile(m_, tm), _tile(n_, tn), _tile(k_, tk)
    nk = k_ // tk
    has_add = add is not None

    def body(*refs):
        a_ref, b_ref = refs[:2]
        add_ref = refs[2] if has_add else None
        o_ref = refs[2 + has_add + (after is not None)]
        part = _raw_dot(a_ref[...], b_ref[...], 0 if ta else 1, 1 if tb else 0)
        if nk == 1:
            o_ref[...] = (part + add_ref[...] if has_add else part).astype(o_ref.dtype)
            return
        acc = refs[-1]
        k = pl.program_id(2)

        @pl.when(k == 0)
        def _():
            acc[...] = part

        @pl.when(jnp.logical_and(k > 0, k < nk - 1))
        def _():
            acc[...] += part

        @pl.when(k == nk - 1)
        def _():
            o_ref[...] = (acc[...] + part + add_ref[...] if has_add else acc[...] + part).astype(o_ref.dtype)

    a_spec = pl.BlockSpec((tk, tm), lambda i, j, k: (k, i)) if ta else pl.BlockSpec((tm, tk), lambda i, j, k: (i, k))
    b_spec = pl.BlockSpec((tn, tk), lambda i, j, k: (j, k)) if tb else pl.BlockSpec((tk, tn), lambda i, j, k: (k, j))
    o_spec = pl.BlockSpec((tm, tn), lambda i, j, k: (i, j))
    in_specs = [a_spec, b_spec] + ([o_spec] if has_add else []) + ([ANY] if after is not None else [])
    args = (a, b) + ((add,) if has_add else ()) + ((after,) if after is not None else ())
    return pl.pallas_call(
        body, name=name, grid=(m_ // tm, n_ // tn, nk), in_specs=in_specs, out_specs=o_spec,
        out_shape=jax.ShapeDtypeStruct((m_, n_), out_dtype),
        scratch_shapes=[pltpu.VMEM((tm, tn), F32)] if nk > 1 else [],
        compiler_params=_params(3))(*args)


def _mm_fused(a, b, post, outs, *, name, rows=(), params=(), n_sums=0, tb=False, tm=1024, tn=1024, tk=1024):
    m_, k_ = a.shape
    n_ = b.shape[0] if tb else b.shape[1]
    tm, tn, tk = _tile(m_, tm), _tile(n_, tn), _tile(k_, tk)
    nk = k_ // tk
    nr, npar, no = len(rows), len(params), len(outs)
    aliased = [o[4] for o in outs if o[4] is not None]
    nin = 2 + nr + npar

    def body(*refs):
        a_ref, b_ref = refs[:2]
        row_refs, par_refs = refs[2:2 + nr], refs[2 + nr:nin]
        out_refs = refs[nin + len(aliased):nin + len(aliased) + no + n_sums]
        part = _raw_dot(a_ref[...], b_ref[...], 1, 1 if tb else 0)
        k = pl.program_id(2)
        first = jnp.logical_and(pl.program_id(0) == 0, pl.program_id(1) == 0)
        if nk > 1:
            acc = refs[-1]

            @pl.when(k == 0)
            def _():
                acc[...] = part

            @pl.when(jnp.logical_and(k > 0, k < nk - 1))
            def _():
                acc[...] += part

        @pl.when(k == nk - 1)
        def _():
            total = acc[...] + part if nk > 1 else part
            res = post(total, *[r[...].astype(F32) for r in row_refs], *[p[...] for p in par_refs])
            for o_ref, val in zip(out_refs[:no], res[:no]):
                o_ref[...] = val.astype(o_ref.dtype)
            for s_ref, val in zip(out_refs[no:], res[no:]):
                @pl.when(first)
                def _(s_ref=s_ref):
                    s_ref[...] = jnp.zeros_like(s_ref)
                s_ref[...] += val

    b_spec = pl.BlockSpec((tn, tk), lambda i, j, k: (j, k)) if tb else pl.BlockSpec((tk, tn), lambda i, j, k: (k, j))
    in_specs = [pl.BlockSpec((tm, tk), lambda i, j, k: (i, k)), b_spec]
    in_specs += [pl.BlockSpec((tm, w), lambda i, j, k, cb=cb: (i, cb(j))) for _, w, cb in rows]
    in_specs += [_full(p) for p in params] + [ANY] * len(aliased)
    out_specs = [pl.BlockSpec((tm, w), lambda i, j, k, cb=cb: (i, cb(j))) for _, w, _, cb, _ in outs]
    out_specs += [_full(p) for p in params[:n_sums]]
    out_shape = [jax.ShapeDtypeStruct((m_, tot), dt) for tot, _, dt, _, _ in outs]
    out_shape += [jax.ShapeDtypeStruct(p.shape, F32) for p in params[:n_sums]]
    aliases, pos = {}, nin
    for oi, o in enumerate(outs):
        if o[4] is not None:
            aliases[pos] = oi
            pos += 1
    return pl.pallas_call(
        body, name=name, grid=(m_ // tm, n_ // tn, nk), in_specs=in_specs, out_specs=out_specs, out_shape=out_shape,
        scratch_shapes=[pltpu.VMEM((tm, tn), F32)] if nk > 1 else [], input_output_aliases=aliases,
        compiler_params=_params(3))(a, b, *[r[0] for r in rows], *params, *aliased)


def _mm_pre(pre, rows, params, b, *, name, tm=512):
    m_ = rows[0][0].shape[0]
    k_, n_ = b.shape
    tm = _tile(m_, tm)
    nr, npar = len(rows), len(params)

    def body(*refs):
        b_ref, a_out, o_ref = refs[nr + npar:]
        a = pre(*[r[...].astype(F32) for r in refs[:nr]], *[p[...] for p in refs[nr:nr + npar]]).astype(a_out.dtype)
        a_out[...] = a
        o_ref[...] = _raw_dot(a, b_ref[...], 1, 0)

    return pl.pallas_call(
        body, name=name, grid=(m_ // tm,),
        in_specs=[pl.BlockSpec((tm, w), lambda i, cb=cb: (i, cb)) for _, w, cb in rows] + [_full(p) for p in params]
        + [_full(b)],
        out_specs=[pl.BlockSpec((tm, k_), lambda i: (i, 0)), pl.BlockSpec((tm, n_), lambda i: (i, 0))],
        out_shape=[jax.ShapeDtypeStruct((m_, k_), _MXU), jax.ShapeDtypeStruct((m_, n_), F32)],
        compiler_params=_params(1))(*[r[0] for r in rows], *params, b)


def _row_spec(tb, w, c0, percol):
    return pl.BlockSpec((tb, w), lambda i, j, c0=c0, pc=percol: (i, c0 + (j if pc else 0)))


def _stage_fwd(f, name, rows, params, outs, tb, ncol=1):
    t_ = rows[0][0].shape[0]
    nr, npar = len(rows), len(params)

    def body(*refs):
        res = f(*[r[...].astype(F32) for r in refs[:nr + npar]])
        for o_ref, val in zip(refs[nr + npar:], res):
            o_ref[...] = val.astype(o_ref.dtype)

    return pl.pallas_call(
        body, name=name, grid=(t_ // tb, ncol),
        in_specs=[_row_spec(tb, w, c0, pc) for (_, w, c0, pc) in rows] + [_full(p) for p in params],
        out_specs=[pl.BlockSpec((tb, w), lambda i, j: (i, j)) for w, _ in outs],
        out_shape=[jax.ShapeDtypeStruct((t_, w * ncol), dt) for w, dt in outs],
        compiler_params=_params(2))(*[r[0] for r in rows], *params)


def _into(dest, tb, width, t_, ncol, dtype):
    if dest is None:
        return pl.BlockSpec((tb, width), lambda i, j: (i, j)), jax.ShapeDtypeStruct((t_, width * ncol), dtype), None
    buf, total, c0 = dest
    return (pl.BlockSpec((tb, width), lambda i, j, c0=c0: (i, c0 + j)), jax.ShapeDtypeStruct((t_, total), dtype), buf)


def _stage_bwd(f, name, rows, params, douts, tb, ncol=1, cat=None, dtypes=None, dest=None, after=None):
    t_ = rows[0][0].shape[0]
    nr, npar, nd = len(rows), len(params), len(douts)
    cat = cat if cat is not None else [[r] for r in range(nr)]
    dtypes = dtypes if dtypes is not None else [F32] * len(cat)
    dest = dest or {}
    assert ncol == 1 or all(len(g) == 1 and rows[g[0]][3] for g in cat)
    nin = nr + npar + nd
    unread = [after] if after is not None else []

    def body(*refs):
        ins = [r[...].astype(F32) for r in refs[:nr + npar]]
        dvals = tuple(r[...].astype(F32) for r in refs[nr + npar:nin])
        out_refs = refs[nin + len(aliased) + len(unread):]
        _, vjp = jax.vjp(f, *ins)
        grads = vjp(dvals)
        for o_ref, grp in zip(out_refs[:len(cat)], cat):
            val = grads[grp[0]] if len(grp) == 1 else jnp.concatenate([grads[r] for r in grp], axis=-1)
            o_ref[...] = val.astype(o_ref.dtype)
        first = jnp.logical_and(pl.program_id(0) == 0, pl.program_id(1) == 0)
        for p_ref, gp in zip(out_refs[len(cat):], grads[nr:]):
            @pl.when(first)
            def _(p_ref=p_ref):
                p_ref[...] = jnp.zeros_like(p_ref)
            p_ref[...] += gp

    gw = [sum(rows[r][1] for r in grp) for grp in cat]
    out_specs, out_shape, aliased, aliases = [], [], [], {}
    for gi, (w, dt) in enumerate(zip(gw, dtypes)):
        spec, shape, buf = _into(dest.get(gi), tb, w, t_, ncol, dt)
        out_specs.append(spec)
        out_shape.append(shape)
        if buf is not None:
            aliases[nin + len(aliased)] = gi
            aliased.append(buf)
    return pl.pallas_call(
        body, name=name, grid=(t_ // tb, ncol),
        in_specs=[_row_spec(tb, w, c0, pc) for (_, w, c0, pc) in rows] + [_full(p) for p in params]
        + [pl.BlockSpec((tb, d.shape[1] // ncol), lambda i, j: (i, j)) for d in douts]
        + [ANY] * (len(aliased) + len(unread)),
        out_specs=out_specs + [_full(p) for p in params],
        out_shape=out_shape + [jax.ShapeDtypeStruct(p.shape, F32) for p in params],
        input_output_aliases=aliases,
        compiler_params=_params(2))(*[r[0] for r in rows], *params, *douts, *aliased, *unread)


def _ln_f(x, y, g, b):
    u = ALPHA * x + y
    c = u - jnp.mean(u, axis=-1, keepdims=True)
    var = jnp.mean(c * c, axis=-1, keepdims=True)
    return (c * lax.rsqrt(var + LN_EPS) * g + b,)


def _ln_f2(x, y, g, b):
    out, = _ln_f(x, y, g, b)
    return out, out


def _swiglu_tile(gu):
    half = gu.shape[1] // 2
    return (jax.nn.silu(gu[:, :half]) * gu[:, half:],)


def _merge_f(ya, yb, ga, gb):
    return (jax.nn.sigmoid(ga) * ya + jax.nn.sigmoid(gb) * yb,)


def _gnorm_f(of, ob, z, w):
    o = of + ob
    return (o * lax.rsqrt(jnp.mean(o * o, axis=-1, keepdims=True) + RMS_EPS) * w * jax.nn.silu(z),)


def _bg_f(x, arow, dtrow):
    lane = lax.broadcasted_iota(jnp.int32, x.shape, 1)
    beta = jax.nn.sigmoid(x)
    g = -jnp.exp(arow) * jax.nn.softplus(x + dtrow)
    return (jnp.where(lane < 16, beta, jnp.where(lane < 32, g, 0.0)),)


PREP_ROWS = 512
PAD = 8


def _prep_f(part, w, *wins):
    xc = wins[0] * w[0:1, :]
    for k in range(1, DN_CONV):
        xc = xc + wins[k] * w[k:k + 1, :]
    a = jax.nn.silu(xc)
    nrm = a * lax.rsqrt(jnp.sum(a * a, axis=-1, keepdims=True) + RMS_EPS)
    return jnp.where(part == 0, nrm * (DN_DIM ** -0.5), jnp.where(part == 1, nrm, a))


def _windows(pad_ref, r0, rows):
    return [pad_ref[PAD + r0 - 2 + k:PAD + r0 - 2 + k + rows, :] for k in range(DN_CONV)]


def _prep_fwd(pm, conv):
    t_ = pm.shape[0]
    rows = min(PREP_ROWS, t_)

    def body(x_ref, w_ref, o_ref, pad_ref):
        part = pl.program_id(0) // DN_HEADS
        pad_ref[0:PAD, :] = jnp.zeros((PAD, DN_DIM), F32)
        pad_ref[PAD + t_:2 * PAD + t_, :] = jnp.zeros((PAD, DN_DIM), F32)
        pad_ref[PAD:PAD + t_, :] = x_ref[...]
        w = w_ref[...]
        for r in range(t_ // rows):
            o_ref[r * rows:(r + 1) * rows, :] = _prep_f(part, w, *_windows(pad_ref, r * rows, rows))

    ncb = 3 * DN_HEADS
    return pl.pallas_call(
        body, name="prep_fwd", grid=(ncb,),
        in_specs=[pl.BlockSpec((t_, DN_DIM), lambda j: (0, j)), pl.BlockSpec((DN_CONV, DN_DIM), lambda j: (0, j))],
        out_specs=pl.BlockSpec((t_, DN_DIM), lambda j: (0, j)),
        out_shape=jax.ShapeDtypeStruct((t_, ncb * DN_DIM), F32),
        scratch_shapes=[pltpu.VMEM((t_ + 2 * PAD, DN_DIM), F32)],
        compiler_params=_params(1))(pm, conv)


def _prep_bwd(pm, conv, dout, dpm):
    t_ = pm.shape[0]
    rows = min(PREP_ROWS, t_)

    def body(x_ref, w_ref, d_ref, _, dx_ref, dw_ref, pad_ref, dpad_ref):
        part = pl.program_id(0) // DN_HEADS
        pad_ref[0:PAD, :] = jnp.zeros((PAD, DN_DIM), F32)
        pad_ref[PAD + t_:2 * PAD + t_, :] = jnp.zeros((PAD, DN_DIM), F32)
        pad_ref[PAD:PAD + t_, :] = x_ref[...]
        dpad_ref[...] = jnp.zeros_like(dpad_ref)
        w = w_ref[...]
        dw = jnp.zeros((DN_CONV, DN_DIM), F32)
        for r in range(t_ // rows):
            r0 = r * rows
            _, vjp = jax.vjp(functools.partial(_prep_f, part), w, *_windows(pad_ref, r0, rows))
            grads = vjp(d_ref[r0:r0 + rows, :])
            dw = dw + grads[0]
            for k in range(DN_CONV):
                lo = PAD + r0 - 2 + k
                dpad_ref[lo:lo + rows, :] += grads[1 + k]
        dx_ref[...] = dpad_ref[PAD:PAD + t_, :].astype(dx_ref.dtype)
        dw_ref[...] = dw

    ncb = 3 * DN_HEADS
    col = pl.BlockSpec((t_, DN_DIM), lambda j: (0, j))
    wsp = pl.BlockSpec((DN_CONV, DN_DIM), lambda j: (0, j))
    return pl.pallas_call(
        body, name="prep_bwd", grid=(ncb,), in_specs=[col, wsp, col, ANY], out_specs=[col, wsp],
        out_shape=[jax.ShapeDtypeStruct(dpm.shape, dpm.dtype), jax.ShapeDtypeStruct((DN_CONV, ncb * DN_DIM), F32)],
        scratch_shapes=[pltpu.VMEM((t_ + 2 * PAD, DN_DIM), F32), pltpu.VMEM((t_ + 2 * PAD, DN_DIM), F32)],
        input_output_aliases={3: 0}, compiler_params=_params(1))(pm, conv, dout, dpm)


def _dn_chunk(sgns, qs, ks, vs, grows, brows, tsaved=None, with_t=False):
    c = qs[0].shape[0]
    i = lax.broadcasted_iota(jnp.int32, (c, c), 0)
    j = lax.broadcasted_iota(jnp.int32, (c, c), 1)
    eye = i == j
    incl = {s: (i - j) * int(s) >= 0 for s in set(sgns)}
    strict = {s: (i - j) * int(s) > 0 for s in set(sgns)}
    gcs = [jnp.sum(jnp.where(incl[s], g, 0.0), axis=1, keepdims=True) for s, g in zip(sgns, grows)]
    grs = [jnp.sum(jnp.where(eye, gc, 0.0), axis=0, keepdims=True) for gc in gcs]
    bcs = [jnp.sum(jnp.where(eye, b, 0.0), axis=1, keepdims=True) for b in brows]
    gls = [jnp.sum(g, axis=1, keepdims=True) for g in grows]
    decs = [jnp.exp(jnp.where(incl[s], gc - gr, -1e30)) for s, gc, gr in zip(sgns, gcs, grs)]
    kks = [_nt(k, k) for k in ks]
    amats = tuple(jnp.where(strict[s], bc * kk * dec, 0.0) for s, bc, kk, dec in zip(sgns, bcs, kks, decs))
    tinvs = _inv(amats) if tsaved is None else _inv_saved(amats, tsaved)
    egcs = [jnp.exp(gc) for gc in gcs]
    us = [_nn(t, v * bc) for t, v, bc in zip(tinvs, vs, bcs)]
    ws = [_nn(t, k * (bc * egc)) for t, k, bc, egc in zip(tinvs, ks, bcs, egcs)]
    qks = [_nt(q, k) * dec for q, k, dec in zip(qs, ks, decs)]
    qds = [q * egc for q, egc in zip(qs, egcs)]
    kds = [k * jnp.exp(gl - gc) for k, gl, gc in zip(ks, gls, gcs)]
    res = tuple(us), tuple(ws), tuple(qks), tuple(qds), tuple(kds)
    return res + (tinvs,) if with_t else res


def _dn_step(us, ws, qks, qds, kds, grows, ss):
    gls = [jnp.exp(jnp.sum(g, axis=1, keepdims=True)) for g in grows]
    wss = [_nn(w, s) for w, s in zip(ws, ss)]
    qss = [_nn(qd, s) for qd, s in zip(qds, ss)]
    vns = [u - x for u, x in zip(us, wss)]
    os_ = [a + _nn(qk, vn) for a, qk, vn in zip(qss, qks, vns)]
    s2s = [s * gl + _tn(kd, vn) for s, gl, kd, vn in zip(ss, gls, kds, vns)]
    return tuple(os_), tuple(s2s)


def _hs(h):
    return slice(h * DN_DIM, (h + 1) * DN_DIM)


_DIR_SGN = (1, -1)
_A_OUT = 5
_PROBLEMS = [(d, h) for d in range(2) for h in range(DN_HEADS)]
_SGNS = [_DIR_SGN[d] for d, _ in _PROBLEMS]


def _chunk_inputs(q_ref, k_ref, v_ref, g_ref, b_ref):
    heads = lambda ref: tuple(ref[:, _hs(h)].astype(F32) for _, h in _PROBLEMS)
    rows = lambda ref: tuple(ref[d, 0, h:h + 1, :] for d, h in _PROBLEMS)
    return heads(q_ref), heads(k_ref), heads(v_ref), rows(g_ref), rows(b_ref)


def _dn_a_fwd(qkv, grows, brows):
    t_ = qkv.shape[0]
    nch = t_ // CHUNK

    def body(q_ref, k_ref, v_ref, g_ref, b_ref, *outs):
        us, ws, qks, qds, kds, tinvs = _dn_chunk(_SGNS, *_chunk_inputs(q_ref, k_ref, v_ref, g_ref, b_ref), with_t=True)
        for p, (d, h) in enumerate(_PROBLEMS):
            u_ref, w_ref, qk_ref, qd_ref, kd_ref = outs[d * _A_OUT:(d + 1) * _A_OUT]
            u_ref[:, _hs(h)], w_ref[:, _hs(h)], qk_ref[0, h] = us[p], ws[p].astype(_MXU), qks[p].astype(_MXU)
            qd_ref[:, _hs(h)], kd_ref[:, _hs(h)] = qds[p].astype(_MXU), kds[p].astype(_MXU)
            outs[2 * _A_OUT + d][0, h] = tinvs[p]

    rspec = pl.BlockSpec((2, 1, DN_HEADS, CHUNK), lambda c: (0, c, 0, 0))
    big = pl.BlockSpec((CHUNK, D), lambda c: (c, 0))
    qks = pl.BlockSpec((1, DN_HEADS, CHUNK, CHUNK), lambda c: (c, 0, 0, 0))
    bigs = lambda dt: jax.ShapeDtypeStruct((t_, D), dt)
    qksh = lambda dt: jax.ShapeDtypeStruct((nch, DN_HEADS, CHUNK, CHUNK), dt)
    res = pl.pallas_call(
        body, name="dn_a_fwd", grid=(nch,),
        in_specs=[pl.BlockSpec((CHUNK, D), lambda c, p=p: (c, p)) for p in range(3)] + [rspec, rspec],
        out_specs=[big, big, qks, big, big] * 2 + [qks, qks],
        out_shape=[bigs(F32), bigs(_MXU), qksh(_MXU), bigs(_MXU), bigs(_MXU)] * 2 + [qksh(F32)] * 2,
        compiler_params=_params(1))(qkv, qkv, qkv, grows, brows)
    return res[:2 * _A_OUT], res[2 * _A_OUT:]


def _dn_a_bwd(qkv, grows, brows, tinv, dres, dg_b):
    t_ = qkv.shape[0]
    nch = t_ // CHUNK

    def body(q_ref, k_ref, v_ref, g_ref, b_ref, tf_ref, tb_ref, *rest):
        dins, dgb_ref, (dqkv_ref, dg_ref, db_ref) = rest[:2 * _A_OUT], rest[2 * _A_OUT], rest[2 * _A_OUT + 1:]
        tsaved = tuple((tf_ref, tb_ref)[d][0, h] for d, h in _PROBLEMS)
        _, vjp = jax.vjp(functools.partial(_dn_chunk, _SGNS, tsaved=tsaved),
                         *_chunk_inputs(q_ref, k_ref, v_ref, g_ref, b_ref))
        cots = []
        for o in range(_A_OUT):
            cots.append(tuple(dins[d * _A_OUT + o][0, h] if o == 2 else dins[d * _A_OUT + o][:, _hs(h)]
                              for d, h in _PROBLEMS))
        gq, gk, gv, gg, gb = vjp(tuple(cots))
        for p, (d, h) in enumerate(_PROBLEMS):
            dg_ref[d, 0, h:h + 1, :] = gg[p] + dgb_ref[d, 0, h:h + 1, :]
            db_ref[d, 0, h:h + 1, :] = gb[p]
        for h in range(DN_HEADS):
            dqkv_ref[:, _hs(h)] = gq[h] + gq[DN_HEADS + h]
            dqkv_ref[:, _hs(DN_HEADS + h)] = gk[h] + gk[DN_HEADS + h]
            dqkv_ref[:, _hs(2 * DN_HEADS + h)] = gv[h] + gv[DN_HEADS + h]

    rspec = pl.BlockSpec((2, 1, DN_HEADS, CHUNK), lambda c: (0, c, 0, 0))
    big = pl.BlockSpec((CHUNK, D), lambda c: (c, 0))
    qks = pl.BlockSpec((1, DN_HEADS, CHUNK, CHUNK), lambda c: (c, 0, 0, 0))
    rsh = jax.ShapeDtypeStruct(grows.shape, F32)
    return pl.pallas_call(
        body, name="dn_a_bwd", grid=(nch,),
        in_specs=[pl.BlockSpec((CHUNK, D), lambda c, p=p: (c, p)) for p in range(3)] + [rspec, rspec, qks, qks]
        + [big, big, qks, big, big] * 2 + [rspec],
        out_specs=[pl.BlockSpec((CHUNK, 3 * D), lambda c: (c, 0)), rspec, rspec],
        out_shape=[jax.ShapeDtypeStruct((t_, 3 * D), F32), rsh, rsh],
        compiler_params=_params(1))(qkv, qkv, qkv, grows, brows, *tinv, *dres, dg_b)


def _dir_specs(nch):
    def cidx(d):
        return (lambda n: n) if d == 0 else (lambda n: nch - 1 - n)
    out = []
    for d in range(2):
        ci = cidx(d)
        big = pl.BlockSpec((CHUNK, D), lambda n, ci=ci: (ci(n), 0))
        qks = pl.BlockSpec((1, DN_HEADS, CHUNK, CHUNK), lambda n, ci=ci: (ci(n), 0, 0, 0))
        row = pl.BlockSpec((1, 1, DN_HEADS, CHUNK), lambda n, ci=ci, d=d: (d, ci(n), 0, 0))
        st = pl.BlockSpec((1, DN_HEADS, DN_DIM, DN_DIM), lambda n, ci=ci: (ci(n), 0, 0, 0))
        out.append(dict(big=big, qk=qks, row=row, st=st))
    return out


def _step_inputs(ins, per_dir):
    def pick(o):
        if o == 2:
            return tuple(ins[d * per_dir + o][0, h].astype(F32) for d, h in _PROBLEMS)
        if o == 5:
            return tuple(ins[d * per_dir + o][0, 0, h:h + 1, :] for d, h in _PROBLEMS)
        return tuple(ins[d * per_dir + o][:, _hs(h)].astype(F32) for d, h in _PROBLEMS)
    return [pick(o) for o in range(6)]


def _dn_b_fwd(ares, grows):
    t_ = ares[0].shape[0]
    nch = t_ // CHUNK
    sp = _dir_specs(nch)

    def body(*refs):
        ins, outs, s_ref = refs[:12], refs[12:16], refs[16]

        @pl.when(pl.program_id(0) == 0)
        def _():
            s_ref[...] = jnp.zeros_like(s_ref)

        ss = tuple(s_ref[p] for p in range(len(_PROBLEMS)))
        os_, s2s = _dn_step(*_step_inputs(ins, 6), ss)
        for p, (d, h) in enumerate(_PROBLEMS):
            outs[2 + d][0, h] = ss[p]
            outs[d][:, _hs(h)] = os_[p]
            s_ref[p] = s2s[p]

    in_specs, args = [], []
    for d in range(2):
        in_specs += [sp[d]["big"], sp[d]["big"], sp[d]["qk"], sp[d]["big"], sp[d]["big"], sp[d]["row"]]
        args += list(ares[d * _A_OUT:(d + 1) * _A_OUT]) + [grows]
    stsh = jax.ShapeDtypeStruct((nch, DN_HEADS, DN_DIM, DN_DIM), F32)
    osh = jax.ShapeDtypeStruct((t_, D), F32)
    return pl.pallas_call(
        body, name="dn_b_fwd", grid=(nch,), in_specs=in_specs,
        out_specs=[sp[0]["big"], sp[1]["big"], sp[0]["st"], sp[1]["st"]], out_shape=[osh, osh, stsh, stsh],
        scratch_shapes=[pltpu.VMEM((2 * DN_HEADS, DN_DIM, DN_DIM), F32)],
        compiler_params=_params(1))(*args)


def _dn_b_bwd(ares, grows, st_f, st_b, do):
    t_ = ares[0].shape[0]
    nch = t_ // CHUNK
    sp = _dir_specs(nch)
    rsp = [sp[1], sp[0]]

    def body(*refs):
        ins, outs, ds_ref = refs[:16], refs[16:28], refs[28]

        @pl.when(pl.program_id(0) == 0)
        def _():
            ds_ref[...] = jnp.zeros_like(ds_ref)

        ss = tuple(ins[d * 8 + 6][0, h] for d, h in _PROBLEMS)
        _, vjp = jax.vjp(_dn_step, *_step_inputs(ins, 8), ss)
        dos = tuple(ins[d * 8 + 7][:, _hs(h)] for d, h in _PROBLEMS)
        grads = vjp((dos, tuple(ds_ref[p] for p in range(len(_PROBLEMS)))))
        for p, (d, h) in enumerate(_PROBLEMS):
            du_ref, dw_ref, dqk_ref, dqd_ref, dkd_ref, dg_ref = outs[d * 6:(d + 1) * 6]
            du_ref[:, _hs(h)], dw_ref[:, _hs(h)], dqk_ref[0, h] = grads[0][p], grads[1][p], grads[2][p]
            dqd_ref[:, _hs(h)], dkd_ref[:, _hs(h)] = grads[3][p], grads[4][p]
            dg_ref[0, 0, h:h + 1, :] = grads[5][p]
            ds_ref[p] = grads[6][p]

    in_specs, args, out_specs, out_shape = [], [], [], []
    big_sh = jax.ShapeDtypeStruct((t_, D), F32)
    qk_sh = jax.ShapeDtypeStruct((nch, DN_HEADS, CHUNK, CHUNK), F32)
    row_sh = jax.ShapeDtypeStruct((1, nch, DN_HEADS, CHUNK), F32)
    for d in range(2):
        s = rsp[d]
        row0 = pl.BlockSpec((1, 1, DN_HEADS, CHUNK), lambda m, d=d: (0, (nch - 1 - m) if d == 0 else m, 0, 0))
        rowd = pl.BlockSpec((1, 1, DN_HEADS, CHUNK), lambda m, d=d: (d, (nch - 1 - m) if d == 0 else m, 0, 0))
        in_specs += [s["big"], s["big"], s["qk"], s["big"], s["big"], rowd, s["st"], s["big"]]
        args += list(ares[d * _A_OUT:(d + 1) * _A_OUT]) + [grows, (st_f, st_b)[d], do]
        out_specs += [s["big"], s["big"], s["qk"], s["big"], s["big"], row0]
        out_shape += [big_sh, big_sh, qk_sh, big_sh, big_sh, row_sh]
    res = pl.pallas_call(
        body, name="dn_b_bwd", grid=(nch,), in_specs=in_specs, out_specs=out_specs, out_shape=out_shape,
        scratch_shapes=[pltpu.VMEM((2 * DN_HEADS, DN_DIM, DN_DIM), F32)],
        compiler_params=_params(1))(*args)
    dares = list(res[0:5]) + list(res[6:11])
    return dares, jnp.concatenate([res[5], res[11]], axis=0)


@jax.custom_vjp
def _rot_half(x):
    half, width = SW_DIM // 2, x.shape[1]
    first = lax.broadcasted_iota(jnp.int32, x.shape, 1) % SW_DIM < half
    return jnp.where(first, -pltpu.roll(x, width - half, axis=1), pltpu.roll(x, half, axis=1))


_rot_half.defvjp(lambda x: (_rot_half(x), None), lambda _, g: (-_rot_half(g),))


def _rope(x, c, s):
    reps = x.shape[1] // c.shape[1]
    return x * jnp.tile(c, (1, reps)) + _rot_half(x) * jnp.tile(s, (1, reps))


def _rope_t(g, c, s):
    reps = g.shape[1] // c.shape[1]
    return g * jnp.tile(c, (1, reps)) - _rot_half(g * jnp.tile(s, (1, reps)))


_SW_SCALE = SW_DIM ** -0.5
_KV_HEADS = [[kvh * SW_GRP + g for g in range(SW_GRP)] for kvh in range(SW_KV)]


def _by_group(x):
    return [jnp.concatenate([x[:, h * SW_DIM:(h + 1) * SW_DIM] for h in hs], axis=0) for hs in _KV_HEADS]


def _from_groups(xs):
    return jnp.concatenate([x[g * SW_BLOCK:(g + 1) * SW_BLOCK] for x in xs for g in range(SW_GRP)], axis=-1)


def _attn_probs(blk, t_, cq, sq, ck, sk, q, kall, sinks):
    qgs = _by_group(_rope(q, cq, sq))
    kr = _rope(kall, ck, sk)
    khs = [kr[:, kvh * SW_DIM:(kvh + 1) * SW_DIM] for kvh in range(SW_KV)]
    nq, nk = SW_GRP * SW_BLOCK, 3 * SW_BLOCK
    qpos = lax.broadcasted_iota(jnp.int32, (nq, nk), 0) % SW_BLOCK
    krel = lax.broadcasted_iota(jnp.int32, (nq, nk), 1) - SW_BLOCK
    kglob = krel + blk * SW_BLOCK
    valid = (jnp.abs(qpos - krel) <= SW_BLOCK) & (kglob >= 0) & (kglob < t_)
    ss = [jnp.where(valid, _nt(qg * _SW_SCALE, kh), -1e30) for qg, kh in zip(qgs, khs)]
    snks = [jnp.concatenate([jnp.broadcast_to(sinks[:, h:h + 1], (SW_BLOCK, 1)) for h in hs], axis=0) for hs in _KV_HEADS]
    ms = [jnp.maximum(jnp.max(s, axis=-1, keepdims=True), snk) for s, snk in zip(ss, snks)]
    es = [jnp.exp(s - m) for s, m in zip(ss, ms)]
    esnks = [jnp.exp(snk - m) for snk, m in zip(snks, ms)]
    invs = [1.0 / (jnp.sum(e, axis=-1, keepdims=True) + esnk) for e, esnk in zip(es, esnks)]
    ps = [e * inv for e, inv in zip(es, invs)]
    return qgs, khs, ps, [esnk * inv for esnk, inv in zip(esnks, invs)]


def _attn_f(blk, t_, cq, sq, ck, sk, q, kp, ko, kn, vp, vo, vn, sinks):
    _, _, ps, psinks = _attn_probs(blk, t_, cq, sq, ck, sk, q, jnp.concatenate([kp, ko, kn], axis=0), sinks)
    vall = jnp.concatenate([vp, vo, vn], axis=0)
    ps = [p.astype(_MXU) for p in ps]
    kept = [jnp.concatenate([p, jnp.broadcast_to(psink.astype(_MXU), (p.shape[0], 128))], axis=-1)
            for p, psink in zip(ps, psinks)]
    return _from_groups([_nn(p, vall[:, kvh * SW_DIM:(kvh + 1) * SW_DIM]) for kvh, p in enumerate(ps)]), kept


def _attn_bwd_f(cq, sq, ck, sk, q, kp, ko, kn, vp, vo, vn, do, kept):
    nk = 3 * SW_BLOCK
    ps = [x[:, :nk] for x in kept]
    psinks = [x[:, nk:nk + 1].astype(F32) for x in kept]
    qgs = _by_group(_rope(q, cq, sq))
    kr = _rope(jnp.concatenate([kp, ko, kn], axis=0), ck, sk)
    khs = [kr[:, kvh * SW_DIM:(kvh + 1) * SW_DIM] for kvh in range(SW_KV)]
    vall = jnp.concatenate([vp, vo, vn], axis=0)
    vhs = [vall[:, kvh * SW_DIM:(kvh + 1) * SW_DIM] for kvh in range(SW_KV)]
    dogs = _by_group(do)
    dvs = [_tn(p, dog) for p, dog in zip(ps, dogs)]
    ps = [p.astype(F32) for p in ps]
    dps = [_nt(dog * _SW_SCALE, vh) for dog, vh in zip(dogs, vhs)]
    deltas = [jnp.sum(p * dp, axis=-1, keepdims=True) for p, dp in zip(ps, dps)]
    dss = [p * (dp - delta) for p, dp, delta in zip(ps, dps, deltas)]
    dqr = _from_groups([_nn(ds, kh) for ds, kh in zip(dss, khs)])
    dkr = jnp.concatenate([_tn(ds, qg) for ds, qg in zip(dss, qgs)], axis=-1)
    dsnk = [-(psink * delta) * (1.0 / _SW_SCALE) for psink, delta in zip(psinks, deltas)]
    dsinks = jnp.concatenate([jnp.sum(d[g * SW_BLOCK:(g + 1) * SW_BLOCK], axis=0, keepdims=True)
                              for d in dsnk for g in range(SW_GRP)], axis=1)
    dq, dk, dv = _rope_t(dqr, cq, sq), _rope_t(dkr, ck, sk), jnp.concatenate(dvs, axis=-1)
    blocks = lambda a: [a[j * SW_BLOCK:(j + 1) * SW_BLOCK] for j in range(3)]
    return [dq] + blocks(dk) + blocks(dv) + [dsinks]


def _attn_specs(nb):
    prv = lambda i: jnp.maximum(i - 1, 0)
    nxt = lambda i: jnp.minimum(i + 1, nb - 1)
    rows = [lambda i: i, prv, lambda i: i, nxt]
    tab = [pl.BlockSpec((SW_BLOCK, 128), lambda i, r=r: (r(i), 0)) for r in rows]
    qs = pl.BlockSpec((SW_BLOCK, SW_HEADS * SW_DIM), lambda i: (i, O_QS // (SW_HEADS * SW_DIM)))
    kw = SW_KV * SW_DIM
    ks = [pl.BlockSpec((SW_BLOCK, kw), lambda i, r=r: (r(i), O_KS // kw)) for r in rows[1:]]
    vs = [pl.BlockSpec((SW_BLOCK, kw), lambda i, r=r: (r(i), O_VS // kw)) for r in rows[1:]]
    return tab, qs, ks, vs


def _attn_tables(refs):
    cq, cp, co, cn, sq, sp_, so, sn = [r[...] for r in refs]
    return cq, sq, jnp.concatenate([cp, co, cn], axis=0), jnp.concatenate([sp_, so, sn], axis=0)


_P_BLOCK = (None, SW_KV, SW_GRP * SW_BLOCK, 3 * SW_BLOCK + 128)


def _attn_fwd(pm, cos, sin, sinks):
    t_ = pm.shape[0]
    nb = t_ // SW_BLOCK
    tab, qs, ks, vs = _attn_specs(nb)

    def body(*refs):
        tabs = _attn_tables(refs[:8])
        vals = [r[...] for r in refs[8:16]]
        out, ps = _attn_f(pl.program_id(0), t_, *tabs, *vals)
        refs[16][...] = out.astype(refs[16].dtype)
        for kvh, p in enumerate(ps):
            refs[17][kvh] = p

    return pl.pallas_call(
        body, name="attn_fwd", grid=(nb,), in_specs=tab + tab + [qs] + ks + vs + [_full(sinks)],
        out_specs=[pl.BlockSpec((SW_BLOCK, D), lambda i: (i, 0)), pl.BlockSpec(_P_BLOCK, lambda i: (i, 0, 0, 0))],
        out_shape=[jax.ShapeDtypeStruct((t_, D), _MXU), jax.ShapeDtypeStruct((nb,) + _P_BLOCK[1:], _MXU)],
        compiler_params=_params(1))(*([cos] * 4), *([sin] * 4), pm, pm, pm, pm, pm, pm, pm, sinks)


def _attn_bwd(pm, cos, sin, sinks, probs, do, dpm):
    t_ = pm.shape[0]
    nb = t_ // SW_BLOCK
    tab, qs, ks, vs = _attn_specs(nb)
    kw = SW_KV * SW_DIM

    def body(*refs):
        tabs = _attn_tables(refs[:8])
        vals = [r[...] for r in refs[8:15]]
        p_ref, do_ref, outs = refs[16], refs[17], refs[19:]
        grads = _attn_bwd_f(*tabs, *vals, do_ref[...], [p_ref[kvh] for kvh in range(SW_KV)])
        for o_ref, g in zip(outs[:7], grads[:7]):
            o_ref[...] = g.astype(o_ref.dtype)

        @pl.when(pl.program_id(0) == 0)
        def _():
            outs[7][...] = jnp.zeros_like(outs[7])
        outs[7][...] += grads[7]

    own = lambda w: pl.BlockSpec((SW_BLOCK, w), lambda i: (i, 0))
    return pl.pallas_call(
        body, name="attn_bwd", grid=(nb,),
        in_specs=tab + tab + [qs] + ks + vs + [_full(sinks), pl.BlockSpec(_P_BLOCK, lambda i: (i, 0, 0, 0)), own(D), ANY],
        out_specs=[pl.BlockSpec((SW_BLOCK, D), lambda i: (i, O_QS // D))] + [own(kw)] * 6 + [_full(sinks)],
        out_shape=[jax.ShapeDtypeStruct(dpm.shape, dpm.dtype)] + [jax.ShapeDtypeStruct((t_, kw), F32)] * 6
        + [jax.ShapeDtypeStruct(sinks.shape, F32)],
        input_output_aliases={18: 0},
        compiler_params=_params(1))(*([cos] * 4), *([sin] * 4), pm, pm, pm, pm, pm, pm, pm, sinks, probs, do, dpm)


def _band_sum(kparts, vparts, dpm):
    t_, kw = kparts[1].shape
    nb = t_ // SW_BLOCK

    def body(kp, ko, kn, vp, vo, vn, _, out_ref):
        j = pl.program_id(0)
        band = lambda p, o, n: o[...] + jnp.where(j + 1 < nb, p[...], 0.0) + jnp.where(j > 0, n[...], 0.0)
        out_ref[...] = jnp.concatenate([band(kp, ko, kn), band(vp, vo, vn)], axis=-1).astype(out_ref.dtype)

    specs = [pl.BlockSpec((SW_BLOCK, kw), lambda j: (jnp.minimum(j + 1, nb - 1), 0)),
             pl.BlockSpec((SW_BLOCK, kw), lambda j: (j, 0)),
             pl.BlockSpec((SW_BLOCK, kw), lambda j: (jnp.maximum(j - 1, 0), 0))]
    return pl.pallas_call(
        body, name="band_sum", grid=(nb,), in_specs=specs * 2 + [ANY],
        out_specs=pl.BlockSpec((SW_BLOCK, 2 * kw), lambda j: (j, O_KS // (2 * kw))),
        out_shape=jax.ShapeDtypeStruct(dpm.shape, dpm.dtype), input_output_aliases={6: 0},
        compiler_params=_params(1))(*kparts, *vparts, dpm)


def _loss_head(y, target, tb=256):
    t_ = y.shape[0]

    def body(y_ref, t_ref, dy_ref, acc_ref):
        err = y_ref[...] - t_ref[...]
        dy_ref[...] = err * (1.0 / D)
        sq = (err * err).reshape(tb // 8, 8, D).sum(axis=0)
        part = sq[:, 0:128]
        for c in range(1, D // 128):
            part = part + sq[:, c * 128:(c + 1) * 128]

        @pl.when(pl.program_id(0) == 0)
        def _():
            acc_ref[...] = jnp.zeros_like(acc_ref)
        acc_ref[...] += part

    row = pl.BlockSpec((tb, D), lambda i: (i, 0))
    return pl.pallas_call(
        body, name="loss_head", grid=(t_ // tb,), in_specs=[row, row],
        out_specs=[row, pl.BlockSpec((8, 128), lambda i: (0, 0))],
        out_shape=[jax.ShapeDtypeStruct((t_, D), F32), jax.ShapeDtypeStruct((8, 128), F32)],
        compiler_params=_params(1))(y, target)


def _adam_math(w, g, m, v):
    bc1 = 1.0 - ADAM_B1 ** ADAM_STEP
    bc2 = 1.0 - ADAM_B2 ** ADAM_STEP
    nm = ADAM_B1 * m + (1.0 - ADAM_B1) * g
    nv = ADAM_B2 * v + (1.0 - ADAM_B2) * (g * g)
    return -ADAM_LR * ((nm / bc1) / (jnp.sqrt(nv / bc2) + ADAM_EPS) + ADAM_WD * w), nm, nv


def _adamw_layer(name, l, w, g, m, v, carry, after):
    nl, r, c = w.shape
    g2 = g.reshape(r, -1)
    gc = g2.shape[1]
    w2, m2, v2 = [a.reshape(nl * r, c) for a in (w, m, v)]
    tb = r
    while tb * gc * 4 > (1 << 20) and tb % 16 == 0:
        tb //= 2
    nb = r // tb

    def body(w_ref, g_ref, m_ref, v_ref, *rest):
        go_ref, d_ref, nm_ref, nv_ref = rest[-4:]
        gv = g_ref[...][:, :c]
        go_ref[...] = gv
        d_ref[...], nm_ref[...], nv_ref[...] = _adam_math(w_ref[...], gv, m_ref[...], v_ref[...])

    spec = pl.BlockSpec((tb, c), lambda i: (l * nb + i, 0))
    carried = list(carry) if carry is not None else []
    outs = pl.pallas_call(
        body, name=name, grid=(nb,),
        in_specs=[spec, pl.BlockSpec((tb, gc), lambda i: (i, 0)), spec, spec] + [ANY] * (len(carried) + 1),
        out_specs=[spec] * 4, out_shape=[jax.ShapeDtypeStruct((nl * r, c), F32)] * 4,
        input_output_aliases={4 + k: k for k in range(len(carried))},
        compiler_params=_params(1))(w2, g2, m2, v2, *[a.reshape(nl * r, c) for a in carried], after)
    return tuple(o.reshape(w.shape) for o in outs)


def _adamw_layer_t(name, l, w, g, m, v, carry, after):
    nl, r, c = w.shape
    g2 = g.reshape(r, -1)
    gc = g2.shape[1]
    wt, mt, vt = [jnp.swapaxes(a, 1, 2).reshape(nl * c, r) for a in (w, m, v)]

    def body(w_ref, g_ref, m_ref, v_ref, *rest):
        go_ref, d_ref, nm_ref, nv_ref = rest[-4:]
        gv = g_ref[...].T[:c]
        go_ref[...] = gv
        d_ref[...], nm_ref[...], nv_ref[...] = _adam_math(w_ref[...], gv, m_ref[...], v_ref[...])

    spec = pl.BlockSpec((c, 128), lambda j: (l, j))
    carried = list(carry) if carry is not None else []
    return tuple(pl.pallas_call(
        body, name=name, grid=(r // 128,),
        in_specs=[spec, pl.BlockSpec((128, gc), lambda j: (j, 0)), spec, spec] + [ANY] * (len(carried) + 1),
        out_specs=[spec] * 4, out_shape=[jax.ShapeDtypeStruct((nl * c, r), F32)] * 4,
        input_output_aliases={4 + k: k for k in range(len(carried))},
        compiler_params=_params(1))(wt, g2, mt, vt, *carried, after))


def _adamw(name, w, g, m, v):
    shape = w.shape
    cols = shape[-1]
    rows = w.size // cols
    w2, g2, m2, v2 = [a.reshape(rows, cols) for a in (w, g, m, v)]
    tb = rows
    while tb * cols * 4 > (1 << 20) and tb % 16 == 0:
        tb //= 2

    def body(w_ref, g_ref, m_ref, v_ref, d_ref, nm_ref, nv_ref):
        d_ref[...], nm_ref[...], nv_ref[...] = _adam_math(w_ref[...], g_ref[...], m_ref[...], v_ref[...])

    spec = pl.BlockSpec((tb, cols), lambda i: (i, 0))
    sh = jax.ShapeDtypeStruct((rows, cols), F32)
    outs = pl.pallas_call(body, name=name, grid=(rows // tb,), in_specs=[spec] * 4, out_specs=[spec] * 3,
                          out_shape=[sh] * 3, compiler_params=_params(1))(w2, g2, m2, v2)
    return [o.reshape(shape) for o in outs]


def _to_rows(bg, col0):
    t_ = bg.shape[0]
    a = bg[:, col0:col0 + 2 * DN_HEADS].reshape(t_ // CHUNK, CHUNK, 2, DN_HEADS)
    return jnp.transpose(a, (2, 0, 3, 1))


def _from_rows(db, dg):
    nch = db.shape[1]
    back = lambda a: jnp.transpose(a, (1, 3, 0, 2)).reshape(nch * CHUNK, 2 * DN_HEADS)
    return jnp.pad(jnp.concatenate([back(db), back(dg)], axis=1), ((0, 0), (0, 128 - 4 * DN_HEADS)))


def _layer_fwd(x, xm, w, rest, late, cos, sin):
    t_ = x.shape[0]
    nb = min(1024, t_)
    pm = _mm(xm, w["in_main"], name="mm_in", tn=1536)
    pbg = _mm(xm, w["in_bg"], name="mm_in_bg")
    w = {**w, **rest(pbg)}
    qkv = _prep_fwd(pm, w["conv"])
    bg, = _stage_fwd(_bg_f, "bg_fwd", [(pbg, 128, 0, False)], [w["arow"], w["dtrow"]], [(128, F32)], nb)
    brows, grows = _to_rows(bg, 0), _to_rows(bg, 2 * DN_HEADS)
    ares, tinv = _dn_a_fwd(qkv, grows, brows)
    o_f, o_b, st_f, st_b = _dn_b_fwd(ares, grows)
    w = {**w, **late(o_f)}

    def gnorm_all(of, ob, z, gnw):
        return jnp.concatenate([_gnorm_f(of[:, _hs(h)], ob[:, _hs(h)], z[:, _hs(h)], gnw)[0]
                                for h in range(DN_HEADS)], axis=-1)

    odn, ya = _mm_pre(gnorm_all, [(o_f, D, 0), (o_b, D, 0), (pm, D, O_Z // D)], [w["gnw"]], w["a"], name="mm_a_gnorm")
    osw, probs = _attn_fwd(pm, cos, sin, w["sinks"])
    same, first = (lambda j: j), (lambda j: 0)
    full = lambda dt: (D, D, dt, first, None)
    yb, merged = _mm_fused(
        osw, w["b"], lambda acc, ya_, ga, gb: (acc,) + _merge_f(ya_, acc, ga, gb), [full(F32), full(_MXU)],
        name="mm_b_merge", rows=[(ya, D, first), (pm, D, lambda j: O_GA // D), (pm, D, lambda j: O_GB // D)], tm=512)
    mix, x1, x1m = _mm_fused(
        merged, w["o"], lambda acc, x_, g_, b_: (acc,) + _ln_f2(x_, acc, g_, b_), [full(F32), full(F32), full(_MXU)],
        name="mm_o_ln", rows=[(x, D, first)], params=[w["ln1g"], w["ln1b"]], tm=512)
    gu, hid = _mm_fused(x1m, w["gu"], lambda acc: (acc,) + _swiglu_tile(acc),
                        [(2 * FFN, FFN, F32, same, None), (FFN, FFN // 2, _MXU, same, None)], name="mm_gu_swiglu", tn=FFN)
    ffn, x2, x2m = _mm_fused(
        hid, w["d"], lambda acc, x_, g_, b_: (acc,) + _ln_f2(x_, acc, g_, b_), [full(F32), full(F32), full(_MXU)],
        name="mm_d_ln", rows=[(x1, D, first)], params=[w["ln2g"], w["ln2b"]], tm=512, tk=FFN)
    res = dict(w=w, x=x, xm=xm, pm=pm, pbg=pbg, qkv=qkv, grows=grows, brows=brows, ares=ares, tinv=tinv, o_f=o_f, o_b=o_b, st_f=st_f,
               st_b=st_b, odn=odn, osw=osw, probs=probs, ya=ya, yb=yb, merged=merged, mix=mix, x1=x1, x1m=x1m, gu=gu, hid=hid,
               ffn=ffn)
    return x2, x2m, res


def _layer_bwd(dx2, r, w, cos, sin, mid=None, after=None):
    t_ = dx2.shape[0]
    nb = min(1024, t_)
    pm = r["pm"]
    g = {}
    dx1a, dffn, g["ln2g"], g["ln2b"] = _stage_bwd(
        _ln_f, "ln2_bwd", [(r["x1"], D, 0, False), (r["ffn"], D, 0, False)], [w["ln2g"], w["ln2b"]], [dx2],
        min(512, t_), dtypes=[F32, _MXU], after=after)
    same = lambda j: j
    dgu, = _mm_fused(dffn, w["d"], lambda dhid, gu: jax.vjp(_swiglu_tile, gu)[1]((dhid,)),
                     [(2 * FFN, FFN, _MXU, same, None)], name="mm_d_dx_swiglu", rows=[(r["gu"], FFN, same)], tb=True,
                     tn=FFN // 2)
    g["d"] = _mm(r["hid"], dffn, ta=True, name="mm_d_dw", tm=FFN // 2, out_dtype=_MXU)
    first = lambda j: 0
    full = lambda dt: (D, D, dt, first, None)

    def ln1_back(acc, dx1a_, x_, mix_, g_, b_):
        dx_, dmix_, dg_, db_ = jax.vjp(_ln_f, x_, mix_, g_, b_)[1]((acc + dx1a_,))
        return dx_, dmix_, dg_, db_

    dxa, dmix, g["ln1g"], g["ln1b"] = _mm_fused(
        dgu, w["gu"], ln1_back, [full(F32), full(_MXU)], name="mm_gu_dx_ln", tb=True, tm=256, tk=2 * FFN, n_sums=2,
        rows=[(dx1a, D, first), (r["x"], D, first), (r["mix"], D, first)], params=[w["ln1g"], w["ln1b"]])
    g["gu"] = _mm(r["x1m"], dgu, ta=True, name="mm_gu_dw", tn=FFN // 2, out_dtype=_MXU)
    tok = mid(g["gu"]) if mid is not None else None
    g["o"] = _mm(r["merged"], dmix, ta=True, name="mm_o_dw", after=tok, out_dtype=_MXU)

    def merge_back(acc, ya_, yb_, ga, gb):
        dya_, dyb_, dga, dgb = jax.vjp(_merge_f, ya_, yb_, ga, gb)[1]((acc,))
        return dya_, dyb_, jnp.concatenate([dga, dgb], axis=-1)

    dya, dyb, dpm = _mm_fused(
        dmix, w["o"], merge_back, [full(_MXU), full(_MXU), (N_MAIN, 2 * D, _MXU, lambda j: O_GA // (2 * D), None)],
        name="mm_o_dx_merge", tb=True, tm=512,
        rows=[(r["ya"], D, first), (r["yb"], D, first), (pm, D, lambda j: O_GA // D), (pm, D, lambda j: O_GB // D)])

    def gnorm_back(acc, of, ob, z, gnw):
        dos, dzs, dws = [], [], None
        for h in range(DN_HEADS):
            do_h, _, dz_h, dw_h = jax.vjp(_gnorm_f, of[:, _hs(h)], ob[:, _hs(h)], z[:, _hs(h)], gnw)[1]((acc[:, _hs(h)],))
            dos.append(do_h)
            dzs.append(dz_h)
            dws = dw_h if dws is None else dws + dw_h
        return jnp.concatenate(dos, axis=-1), jnp.concatenate(dzs, axis=-1), dws

    dof, dpm, g["gnw"] = _mm_fused(
        dya, w["a"], gnorm_back, [full(F32), (N_MAIN, D, _MXU, lambda j: O_Z // D, dpm)], name="mm_a_dx_gnorm", tb=True,
        tm=512, rows=[(r["o_f"], D, first), (r["o_b"], D, first), (pm, D, lambda j: O_Z // D)], params=[w["gnw"]],
        n_sums=1)
    g["a"] = _mm(r["odn"], dya, ta=True, name="mm_a_dw", out_dtype=_MXU)
    dosw = _mm(dyb, w["b"], tb=True, name="mm_b_dx")
    g["b"] = _mm(r["osw"], dyb, ta=True, name="mm_b_dw", out_dtype=_MXU)
    ab = _attn_bwd(pm, cos, sin, w["sinks"], r["probs"], dosw, dpm)
    dpm, g["sinks"] = ab[0], ab[7]
    dpm = _band_sum(ab[1:4], ab[4:7], dpm)
    dares, dg_b = _dn_b_bwd(r["ares"], r["grows"], r["st_f"], r["st_b"], dof)
    dqkv, dgrows, dbrows = _dn_a_bwd(r["qkv"], r["grows"], r["brows"], r["tinv"], dares, dg_b)
    dbg = _from_rows(dbrows, dgrows)
    dpbg, g["arow"], g["dtrow"] = _stage_bwd(_bg_f, "bg_bwd", [(r["pbg"], 128, 0, False)], [w["arow"], w["dtrow"]],
                                             [dbg], nb)
    dpm, g["conv"] = _prep_bwd(pm, w["conv"], dqkv, dpm)
    dx = _mm(dpm, w["in_main"], tb=True, add=dxa, name="mm_in_dx", tk=1920)
    dx = _mm(dpbg, w["in_bg"], tb=True, add=dx, name="mm_in_bg_dx")
    g["in_main"] = _mm(r["xm"], dpm, ta=True, name="mm_in_dw", tn=1536, out_dtype=_MXU)
    g["in_bg"] = _mm(r["xm"], dpbg, ta=True, name="mm_in_bg_dw", out_dtype=_MXU)
    return dx, g


def _place():
    return lax.axis_index("x"), lax.axis_index("y"), lax.axis_index("c")


def _colblock(kind, q):
    return q if kind == "col" else (q >> 1) | ((q & 1) << 1)


def _other_chips(x, y):
    return [(1 - x, y), (x, 1 - y), (1 - x, 1 - y)]


HBM = pl.BlockSpec(memory_space=pltpu.HBM)
SEM = pl.BlockSpec(memory_space=pltpu.SEMAPHORE)
DATAFLOW = pltpu.SideEffectType.DATAFLOW_SIDE_EFFECTING


def _hbm(a):
    return pltpu.HBM(a.shape, a.dtype)


def _gather_start(locs, kinds, name):
    n = len(locs)
    locs = list(locs)
    lands = [lax.empty((N_CHIPS,) + a.shape if kind == "row" else (a.shape[0], N_CHIPS * a.shape[1]), a.dtype)
             for a, kind in zip(locs, kinds)]

    def body(*refs):
        loc_refs, land_refs = refs[:n], refs[n:2 * n]
        send_sems, recv_sems, token = refs[2 * n:3 * n], refs[3 * n:4 * n], refs[-1]
        x, y, c = _place()
        myq = 2 * x + y
        for t in range(n):
            width = locs[t].shape[1]
            mine = (land_refs[t].at[myq] if kinds[t] == "row" else
                    land_refs[t].at[:, pl.ds(pl.multiple_of(_colblock(kinds[t], myq) * width, 128), width)])
            for dev in [(cx, cy, c) for cx, cy in _other_chips(x, y)] + [(x, y, 1 - c)]:
                pltpu.make_async_remote_copy(src_ref=loc_refs[t], dst_ref=mine, send_sem=send_sems[t],
                                             recv_sem=recv_sems[t], device_id=dev, device_id_type=MESH).start()
        token[...] = jnp.zeros_like(token)

    res = pl.pallas_call(
        body, name=name,
        out_shape=[pltpu.SemaphoreType.DMA(())] * (2 * n) + [_hbm(a) for a in locs + lands]
        + [jax.ShapeDtypeStruct((8, 128), F32)],
        in_specs=[HBM] * (2 * n), out_specs=[SEM] * (2 * n) + [HBM] * (2 * n) + [pl.BlockSpec(memory_space=pltpu.VMEM)],
        input_output_aliases={t: 2 * n + t for t in range(2 * n)},
        compiler_params=pltpu.CompilerParams(has_side_effects=DATAFLOW),
    )(*[pltpu.with_memory_space_constraint(a, pltpu.HBM) for a in locs + lands])
    return (res[:n], res[n:2 * n], res[2 * n:3 * n], res[3 * n:4 * n]), res[-1]


def _split_wait(handle, after, name):
    send_sems, recv_sems, srcs, lands = handle
    n = len(srcs)

    def body(*refs):
        land_refs, ssems, rsems = refs[n:2 * n], refs[2 * n:3 * n], refs[3 * n:4 * n]
        x, y, c = _place()
        for t in range(n):
            done = pltpu.make_async_remote_copy(
                src_ref=land_refs[t], dst_ref=land_refs[t], send_sem=ssems[t], recv_sem=rsems[t],
                device_id=(x, y, c), device_id_type=MESH)
            done.wait_send()
            done.wait_recv()

    res = pl.pallas_call(
        body, name=name, out_shape=[_hbm(a) for a in list(srcs) + list(lands)],
        in_specs=[HBM] * (2 * n) + [SEM] * (2 * n) + [ANY], out_specs=[HBM] * (2 * n),
        input_output_aliases={t: t for t in range(2 * n)},
        compiler_params=pltpu.CompilerParams(has_side_effects=DATAFLOW),
    )(*srcs, *lands, *send_sems, *recv_sems, after)
    return res[:n], res[n:]


RS_CHUNKS = 2


def _piece(ref, kind, q, hf, pr, pc):
    if kind == "row":
        return ref.at[pl.ds((2 * q + hf) * pr, pr), :]
    return ref.at[pl.ds(hf * pr, pr), pl.ds(pl.multiple_of(_colblock(kind, q) * pc, 128), pc)]


def _rs_sibling_start(ts, meta, name):
    n = len(ts)
    ts = list(ts)
    lands = [lax.empty((N_CHIPS, pr, pc), a.dtype) for a, (_, pr, pc) in zip(ts, meta)]

    def body(*refs):
        t_refs, land_refs = refs[:n], refs[n:2 * n]
        send_sems, recv_sems, token = refs[2 * n:3 * n], refs[3 * n:4 * n], refs[-1]
        x, y, c = _place()
        for t, (kind, pr, pc) in enumerate(meta):
            for q in range(N_CHIPS):
                pltpu.make_async_remote_copy(
                    src_ref=_piece(t_refs[t], kind, q, 1 - c, pr, pc), dst_ref=land_refs[t].at[q],
                    send_sem=send_sems[t], recv_sem=recv_sems[t], device_id=(x, y, 1 - c), device_id_type=MESH).start()
        token[...] = jnp.zeros_like(token)

    res = pl.pallas_call(
        body, name=name,
        out_shape=[pltpu.SemaphoreType.DMA(())] * (2 * n) + [_hbm(a) for a in ts + lands]
        + [jax.ShapeDtypeStruct((8, 128), F32)],
        in_specs=[HBM] * (2 * n), out_specs=[SEM] * (2 * n) + [HBM] * (2 * n) + [pl.BlockSpec(memory_space=pltpu.VMEM)],
        input_output_aliases={t: 2 * n + t for t in range(2 * n)},
        compiler_params=pltpu.CompilerParams(has_side_effects=DATAFLOW),
    )(*[pltpu.with_memory_space_constraint(a, pltpu.HBM) for a in ts + lands])
    return (res[:n], res[n:2 * n], res[2 * n:3 * n], res[3 * n:4 * n]), res[-1]


def _rs_add_sibling(ts, r1s, meta, c):
    n = len(ts)
    in_specs, out_specs, out_shape = [], [], []
    for kind, pr, pc in meta:
        rs = pr // RS_CHUNKS
        if kind == "row":
            in_specs.append(pl.BlockSpec((rs, pc), lambda q, r, c_ref: ((2 * q + c_ref[0]) * RS_CHUNKS + r, 0)))
        else:
            in_specs.append(pl.BlockSpec(
                (rs, pc), lambda q, r, c_ref, kind=kind: (c_ref[0] * RS_CHUNKS + r, _colblock(kind, q))))
    for kind, pr, pc in meta:
        sp = pl.BlockSpec((None, pr // RS_CHUNKS, pc), lambda q, r, c_ref: (q, r, 0))
        in_specs.append(sp)
        out_specs.append(sp)
        out_shape.append(jax.ShapeDtypeStruct((N_CHIPS, pr, pc), BF16))

    def body(c_ref, *refs):
        for t in range(n):
            refs[2 * n + t][...] = (refs[t][...].astype(F32) + refs[n + t][...].astype(F32)).astype(BF16)

    return pl.pallas_call(
        body, name="rs_add_sibling", out_shape=out_shape,
        grid_spec=pltpu.PrefetchScalarGridSpec(num_scalar_prefetch=1, grid=(N_CHIPS, RS_CHUNKS), in_specs=in_specs,
                                               out_specs=out_specs),
        compiler_params=_params(2))(c.reshape(1).astype(jnp.int32), *ts, *r1s)


def _rs_chips_start(ps, meta, name):
    n = len(ps)
    ps = list(ps)
    lands = [lax.empty((N_CHIPS - 1, pr, pc), p.dtype) for p, (_, pr, pc) in zip(ps, meta)]

    def body(*refs):
        p_refs, land_refs = refs[:n], refs[n:2 * n]
        send_sems, recv_sems, token = refs[2 * n:3 * n], refs[3 * n:4 * n], refs[-1]
        x, y, c = _place()
        for t in range(n):
            for j, (cx, cy) in enumerate(_other_chips(x, y)):
                pltpu.make_async_remote_copy(
                    src_ref=p_refs[t].at[2 * cx + cy], dst_ref=land_refs[t].at[j], send_sem=send_sems[t],
                    recv_sem=recv_sems[t], device_id=(cx, cy, c), device_id_type=MESH).start()
        token[...] = jnp.zeros_like(token)

    res = pl.pallas_call(
        body, name=name,
        out_shape=[pltpu.SemaphoreType.DMA(())] * (2 * n) + [_hbm(a) for a in ps + lands]
        + [jax.ShapeDtypeStruct((8, 128), F32)],
        in_specs=[HBM] * (2 * n), out_specs=[SEM] * (2 * n) + [HBM] * (2 * n) + [pl.BlockSpec(memory_space=pltpu.VMEM)],
        input_output_aliases={t: 2 * n + t for t in range(2 * n)},
        compiler_params=pltpu.CompilerParams(has_side_effects=DATAFLOW),
    )(*[pltpu.with_memory_space_constraint(a, pltpu.HBM) for a in ps + lands])
    return (res[:n], res[n:2 * n], res[2 * n:3 * n], res[3 * n:4 * n]), res[-1]


def _rs_add_chips(ps, r2s, meta, myq, c):
    n = len(ps)
    in_specs, out_specs, out_shape = [], [], []
    for _, pr, pc in meta:
        in_specs.append(pl.BlockSpec((None, pr // RS_CHUNKS, pc), lambda r, q_ref, c_ref: (q_ref[0], r, 0)))
    for _, pr, pc in meta:
        in_specs.append(pl.BlockSpec((N_CHIPS - 1, pr // RS_CHUNKS, pc), lambda r, q_ref, c_ref: (0, r, 0)))
        out_specs.append(pl.BlockSpec((None, pr // RS_CHUNKS, pc), lambda r, q_ref, c_ref: (c_ref[0], r, 0)))
        out_shape.append(jax.ShapeDtypeStruct((2, pr, pc), F32))

    def body(q_ref, c_ref, *refs):
        for t in range(n):
            r2 = refs[n + t]
            own = refs[t][...].astype(F32)
            refs[2 * n + t][...] = ((own + r2[0].astype(F32)) + r2[1].astype(F32)) + r2[2].astype(F32)

    return pl.pallas_call(
        body, name="rs_add_chips", out_shape=out_shape,
        grid_spec=pltpu.PrefetchScalarGridSpec(num_scalar_prefetch=2, grid=(RS_CHUNKS,), in_specs=in_specs,
                                               out_specs=out_specs),
        compiler_params=_params(1))(myq.reshape(1).astype(jnp.int32), c.reshape(1).astype(jnp.int32), *ps, *r2s)


def _rs_share_halves(gs):
    n = len(gs)

    def body(*refs):
        g_refs, send_sems, recv_sems = refs[n:2 * n], refs[2 * n], refs[2 * n + 1]
        x, y, c = _place()
        sib = (x, y, 1 - c)
        for t in range(n):
            pltpu.make_async_remote_copy(
                src_ref=g_refs[t].at[c], dst_ref=g_refs[t].at[c], send_sem=send_sems.at[t], recv_sem=recv_sems.at[t],
                device_id=sib, device_id_type=MESH).start()
        for t in range(n):
            cp = pltpu.make_async_remote_copy(
                src_ref=g_refs[t].at[c], dst_ref=g_refs[t].at[1 - c], send_sem=send_sems.at[t],
                recv_sem=recv_sems.at[t], device_id=sib, device_id_type=MESH)
            cp.wait_send()
            cp.wait_recv()

    return pl.pallas_call(
        body, name="rs_share_halves", in_specs=[ANY] * n, out_specs=[ANY] * n,
        out_shape=[jax.ShapeDtypeStruct(g.shape, g.dtype) for g in gs], input_output_aliases={t: t for t in range(n)},
        scratch_shapes=[pltpu.SemaphoreType.DMA((n,)), pltpu.SemaphoreType.DMA((n,))],
    )(*gs)


def _rs_middle(handle, after, meta, c, name):
    ts, r1s = _split_wait(handle, after, name + "_sib_wait")
    ps = _rs_add_sibling(ts, r1s, meta, c)
    return _rs_chips_start(ps, meta, name + "_start")


def _rs_end(handle, after, meta, c, myq, name):
    ps, r2s = _split_wait(handle, after, name + "_wait")
    return _rs_share_halves(_rs_add_chips(ps, r2s, meta, myq, c))


def _allreduce_small(buf):
    rows = buf.shape[0]
    ndev = 8

    def body(b_ref, o_ref, slots, send_sems, recv_sems):
        x, y, c = _place()
        me = 4 * x + 2 * y + c
        slots[me] = b_ref[...]
        for k in range(1, ndev):
            kx, ky, kc = (k >> 2) & 1, (k >> 1) & 1, k & 1
            peer = (x ^ kx, y ^ ky, c ^ kc)
            pltpu.make_async_remote_copy(
                src_ref=b_ref, dst_ref=slots.at[me], send_sem=send_sems.at[k - 1], recv_sem=recv_sems.at[k - 1],
                device_id=peer, device_id_type=MESH).start()
        for k in range(1, ndev):
            kx, ky, kc = (k >> 2) & 1, (k >> 1) & 1, k & 1
            cp = pltpu.make_async_remote_copy(
                src_ref=b_ref, dst_ref=slots.at[me ^ k], send_sem=send_sems.at[k - 1], recv_sem=recv_sems.at[k - 1],
                device_id=(x ^ kx, y ^ ky, c ^ kc), device_id_type=MESH)
            cp.wait_send()
            cp.wait_recv()
        acc = slots[0]
        for s in range(1, ndev):
            acc = acc + slots[s]
        o_ref[...] = acc

    vm = pl.BlockSpec(memory_space=pltpu.VMEM)
    return pl.pallas_call(
        body, name="allreduce_small", in_specs=[vm], out_specs=vm, out_shape=jax.ShapeDtypeStruct((rows, 128), F32),
        scratch_shapes=[pltpu.VMEM((ndev, rows, 128), F32), pltpu.SemaphoreType.DMA((ndev - 1,)),
                        pltpu.SemaphoreType.DMA((ndev - 1,))],
        compiler_params=pltpu.CompilerParams(vmem_limit_bytes=VMEM_LIMIT))(buf)


RS_META = [("col", D // 2, IN_PAD), ("row", D // 8, D), ("row", D // 8, D), ("row", D // 8, D),
           ("colx", D // 2, 2 * FFN // N_CHIPS), ("row", FFN // 8, D)]
SMALL_ROWS = 156


def _rope_tables(t_):
    half = SW_DIM // 2
    inv_freq = ROPE_THETA ** (-jnp.arange(half, dtype=F32) / half)
    ang = jnp.arange(t_, dtype=F32)[:, None] * inv_freq[None, :]
    reps = 128 // half
    return jnp.concatenate([jnp.cos(ang)] * reps, axis=1), jnp.concatenate([jnp.sin(ang)] * reps, axis=1)


def _orig_cols(padded, a, b):
    out = []
    for q in range(N_CHIPS):
        lo, hi = max(a, q * IN_SHARD), min(b, (q + 1) * IN_SHARD)
        if lo < hi:
            out.append(padded[:, q * IN_PAD + lo - q * IN_SHARD:q * IN_PAD + hi - q * IN_SHARD])
    return out


_ORIG_SEGMENTS = [(0, R_BG, "main", 0), (R_BG, R_SW, "bg", 0), (R_SW, R_GATES, "main", O_QS), (R_GATES, IN_COLS, "main", O_GA)]


def _to_padded_shards(main, bg):
    zeros = jnp.zeros((main.shape[0], IN_PAD - IN_SHARD), main.dtype)
    parts = []
    for q in range(N_CHIPS):
        for a, b, src, s0 in _ORIG_SEGMENTS:
            lo, hi = max(a, q * IN_SHARD), min(b, (q + 1) * IN_SHARD)
            if lo < hi:
                parts.append((main if src == "main" else bg)[:, s0 + lo - a:s0 + hi - a])
        parts.append(zeros)
    return jnp.concatenate(parts, axis=1)


def _lane_row(v16):
    return jnp.pad(v16.reshape(1, 2 * DN_HEADS), ((0, 0), (2 * DN_HEADS, 128 - 4 * DN_HEADS)))


def _pack_small(g):
    pad16 = jnp.pad(g["sinks"], ((0, 0), (0, 128 - SW_HEADS)))
    return jnp.concatenate([g["conv"].reshape(-1, 128), g["ln1g"].reshape(-1, 128), g["ln1b"].reshape(-1, 128),
                            g["ln2g"].reshape(-1, 128), g["ln2b"].reshape(-1, 128), g["gnw"], g["arow"], g["dtrow"],
                            pad16], axis=0)


def _unpack_small(buf):
    nconv = DN_CONV * 3 * D // 128
    o = nconv
    out = dict(conv=buf[:o].reshape(DN_CONV, 3 * D))
    for name in ("ln1g", "ln1b", "ln2g", "ln2b"):
        out[name] = buf[o:o + 8].reshape(D)
        o += 8
    out["gnw"] = buf[o]
    out["a_log"] = buf[o + 1, 2 * DN_HEADS:4 * DN_HEADS].reshape(2, DN_HEADS)
    out["dt_bias"] = buf[o + 2, 2 * DN_HEADS:4 * DN_HEADS].reshape(2, DN_HEADS)
    out["sinks"] = buf[o + 3, :SW_HEADS]
    return out


def kernel(x, w_in, conv_w, a_log, dt_bias, dn_norm_w, sinks, w_branch_a, w_branch_b, w_out, ln1_g, ln1_b, w_gate_up, w_down, ln2_g, ln2_b, loss_target, m_w_in, m_conv_w, m_a_log, m_dt_bias, m_dn_norm_w, m_sinks, m_w_branch_a, m_w_branch_b, m_w_out, m_ln1_g, m_ln1_b, m_w_gate_up, m_w_down, m_ln2_g, m_ln2_b, v_w_in, v_conv_w, v_a_log, v_dt_bias, v_dn_norm_w, v_sinks, v_w_branch_a, v_w_branch_b, v_w_out, v_ln1_g, v_ln1_b, v_w_gate_up, v_w_down, v_ln2_g, v_ln2_b):
    xi, yi, ci = _place()
    myq = 2 * xi + yi
    t_ = x.shape[1]
    cos, sin = _rope_tables(t_)

    kinds = ["col", "col", "row", "row", "row", "colx", "row"]
    gathers = []
    for l in range(DEPTH):
        srcs = [jnp.pad(w_in[l].astype(BF16), ((0, 0), (0, IN_PAD - IN_SHARD))), conv_w[l]]
        srcs += [a[l].astype(BF16) for a in (w_branch_a, w_branch_b, w_out, w_gate_up, w_down)]
        if gathers:
            srcs[1] = srcs[1] + gathers[-1][1][0, 0]
        gathers.append(_gather_start(srcs, kinds, "gather_%d_start" % l))

    def in_weights(l, after):
        _, (full_in,) = _split_wait(tuple(part[:1] for part in gathers[l][0]), after, "gather_%d_wait_in" % l)
        cols = lambda a, b: _orig_cols(full_in, a, b)
        return dict(
            in_main=jnp.concatenate(cols(0, R_BG) + cols(R_GATES, IN_COLS) + cols(R_SW, R_GATES), axis=1),
            in_bg=jnp.pad(jnp.concatenate(cols(R_BG, R_SW), axis=1), ((0, 0), (0, 128 - 4 * DN_HEADS))))

    def rest_weights(l, after):
        _, (full_conv,) = _split_wait(tuple(part[1:2] for part in gathers[l][0]), after, "gather_%d_wait_conv" % l)
        return dict(
            conv=full_conv, arow=_lane_row(a_log[l]), dtrow=_lane_row(dt_bias[l]), gnw=dn_norm_w[l][None],
            sinks=sinks[l][None], ln1g=ln1_g[l][None], ln1b=ln1_b[l][None], ln2g=ln2_g[l][None], ln2b=ln2_b[l][None])

    def late_weights(l, after):
        _, (full_a, full_b, full_o, full_gu, full_d) = _split_wait(
            tuple(part[2:] for part in gathers[l][0]), after, "gather_%d_wait_rest" % l)
        return dict(a=full_a.reshape(D, D), b=full_b.reshape(D, D), o=full_o.reshape(D, D), gu=full_gu,
                    d=full_d.reshape(FFN, D))

    h = x[0]
    hm = h.astype(_MXU)
    residuals = []
    for l in range(DEPTH):
        win = in_weights(l, gathers[-1][1] if l == 0 else h)
        h, hm, res = _layer_fwd(h, hm, win, functools.partial(rest_weights, l), functools.partial(late_weights, l),
                                cos, sin)
        residuals.append(res)
    dh, sq = _loss_head(h, loss_target[0])
    loss_rows = jnp.pad(((0.5 / D) * jnp.sum(sq)).reshape(1, 1), ((0, 7), (0, 127)))

    big = [None] * DEPTH
    small = [None] * DEPTH
    hop1 = hop2 = None

    def mid(after):
        nonlocal hop1, hop2
        if hop1 is None:
            return None
        handle, tok = _rs_middle(hop1[1], after, RS_META, ci, "rs_chips_%d" % hop1[0])
        hop1, hop2 = None, (hop1[0], handle)
        return tok

    token = None
    for l in reversed(range(DEPTH)):
        dh, g = _layer_bwd(dh, residuals[l], residuals[l]["w"], cos, sin, mid, token)
        if hop2 is not None:
            big[hop2[0]] = _rs_end(hop2[1], dh, RS_META, ci, myq, "rs_chips_%d" % hop2[0])
        g_in = _to_padded_shards(g["in_main"], g["in_bg"])
        handle, token = _rs_sibling_start([g_in, g["a"], g["b"], g["o"], g["gu"], g["d"]], RS_META, "rs_sib_%d_start" % l)
        hop1 = (l, handle)
        small[l] = _pack_small(g)
    tot = _allreduce_small(jnp.concatenate(small + [loss_rows], axis=0))
    loss = tot[DEPTH * SMALL_ROWS, 0]
    token = mid(tot)
    pending = hop2
    sm = [_unpack_small(tot[l * SMALL_ROWS:(l + 1) * SMALL_ROWS]) for l in range(DEPTH)]
    stack = lambda name: jnp.stack([s[name] for s in sm], axis=0)
    grads = dict(
        conv_w=lax.dynamic_slice_in_dim(stack("conv"), myq * (3 * D // N_CHIPS), 3 * D // N_CHIPS, axis=2),
        a_log=stack("a_log"), dt_bias=stack("dt_bias"), dn_norm_w=stack("gnw"), sinks=stack("sinks"),
        ln1_g=stack("ln1g"), ln1_b=stack("ln1b"), ln2_g=stack("ln2g"), ln2_b=stack("ln2b"))
    weights = dict(w_in=w_in, conv_w=conv_w, a_log=a_log, dt_bias=dt_bias, dn_norm_w=dn_norm_w, sinks=sinks,
                   w_branch_a=w_branch_a, w_branch_b=w_branch_b, w_out=w_out, ln1_g=ln1_g, ln1_b=ln1_b,
                   w_gate_up=w_gate_up, w_down=w_down, ln2_g=ln2_g, ln2_b=ln2_b)
    ms = dict(w_in=m_w_in, conv_w=m_conv_w, a_log=m_a_log, dt_bias=m_dt_bias, dn_norm_w=m_dn_norm_w, sinks=m_sinks,
              w_branch_a=m_w_branch_a, w_branch_b=m_w_branch_b, w_out=m_w_out, ln1_g=m_ln1_g, ln1_b=m_ln1_b,
              w_gate_up=m_w_gate_up, w_down=m_w_down, ln2_g=m_ln2_g, ln2_b=m_ln2_b)
    vs = dict(w_in=v_w_in, conv_w=v_conv_w, a_log=v_a_log, dt_bias=v_dt_bias, dn_norm_w=v_dn_norm_w, sinks=v_sinks,
              w_branch_a=v_w_branch_a, w_branch_b=v_w_branch_b, w_out=v_w_out, ln1_g=v_ln1_g, ln1_b=v_ln1_b,
              w_gate_up=v_w_gate_up, w_down=v_w_down, ln2_g=v_ln2_g, ln2_b=v_ln2_b)
    names = list(weights)
    upd = {n: _adamw("adamw_" + n, weights[n], grads[n], ms[n], vs[n]) for n in grads}
    big_names = ["w_in", "w_branch_a", "w_branch_b", "w_out", "w_gate_up", "w_down"]
    carry = {n: None for n in big_names}

    def update_layer(l):
        for t, n in enumerate(big_names):
            step = _adamw_layer_t if weights[n].shape[-1] % 128 else _adamw_layer
            carry[n] = step("adamw_" + n, l, weights[n], big[l][t], ms[n], vs[n], carry[n], token)

    for l in range(DEPTH - 1, pending[0], -1):
        update_layer(l)
    big[pending[0]] = _rs_end(pending[1], carry[big_names[-1]][1], RS_META, ci, myq, "rs_chips_%d" % pending[0])
    update_layer(pending[0])
    for n in big_names:
        if weights[n].shape[-1] % 128:
            nl, r, c = weights[n].shape
            carry[n] = tuple(jnp.swapaxes(a.reshape(nl, c, r), 1, 2) for a in carry[n])
        grads[n], upd[n] = carry[n][0], carry[n][1:]
    return (loss, dh[None], *[grads[n] for n in names], *[upd[n][0] for n in names], *[upd[n][1] for n in names],
            *[upd[n][2] for n in names])
```

```python
import functools

import jax
import jax.numpy as jnp
from jax import lax
from jax.experimental import pallas as pl
from jax.experimental.pallas import tpu as pltpu

F32 = jnp.float32
BF16 = jnp.bfloat16
_MXU = BF16

D = 1024
DEPTH = 4
DN_HEADS = 8
DN_DIM = 128
DN_CONV = 5
CHUNK = 64
SW_HEADS = 16
SW_KV = 4
SW_DIM = 64
SW_GRP = SW_HEADS // SW_KV
SW_BLOCK = 128
ROPE_THETA = 10000.0
FFN = 2816
ALPHA = (2.0 * DEPTH) ** 0.25
LN_EPS = 1e-5
RMS_EPS = 1e-6
IN_COLS = 7712
O_Z, O_GA, O_GB, O_QS, O_KS, O_VS, N_MAIN = 3072, 4096, 5120, 6144, 7168, 7424, 7680
R_BG, R_SW, R_GATES = 4096, 4128, 5664
N_CHIPS = 4
IN_SHARD = IN_COLS // N_CHIPS
IN_PAD = 2048
ADAM_LR, ADAM_B1, ADAM_B2, ADAM_EPS, ADAM_WD, ADAM_STEP = 0.001, 0.9, 0.999, 1e-08, 0.01, 10
VMEM_LIMIT = 52 * 1024 * 1024
MESH = pl.DeviceIdType.MESH
ANY = pl.BlockSpec(memory_space=pl.ANY)


def _params(n_grid, **kw):
    return pltpu.CompilerParams(dimension_semantics=("arbitrary",) * n_grid, vmem_limit_bytes=VMEM_LIMIT, **kw)


def _full(a):
    nd = a.ndim
    return pl.BlockSpec(a.shape, lambda *_, nd=nd: (0,) * nd)


def _raw_dot(a, b, ca, cb):
    return lax.dot_general(a.astype(_MXU), b.astype(_MXU), (((ca,), (cb,)), ((), ())), preferred_element_type=F32)


@jax.custom_vjp
def _nn(a, b):
    return _raw_dot(a, b, 1, 0)


@jax.custom_vjp
def _nt(a, b):
    return _raw_dot(a, b, 1, 1)


@jax.custom_vjp
def _tn(a, b):
    if a.shape[1] > b.shape[1]:
        return _raw_dot(b, a, 0, 0).T
    return _raw_dot(a, b, 0, 0)


_nn.defvjp(lambda a, b: (_nn(a, b), (a, b)), lambda r, g: (_nt(g, r[1]), _tn(r[0], g)))
_nt.defvjp(lambda a, b: (_nt(a, b), (a, b)), lambda r, g: (_nn(g, r[1]), _tn(g, r[0])))
_tn.defvjp(lambda a, b: (_tn(a, b), (a, b)), lambda r, g: (_nt(r[1], g), _nn(r[0], g)))


def _hdot(a, b, ca=1, cb=0):
    ah, bh = a.astype(BF16), b.astype(BF16)
    al, bl = (a - ah.astype(F32)).astype(BF16), (b - bh.astype(F32)).astype(BF16)
    dot = lambda u, v: lax.dot_general(u, v, (((ca,), (cb,)), ((), ())), preferred_element_type=F32)
    return dot(ah, bh) + (dot(ah, bl) + dot(al, bh))


def _inv_impl(mats):
    n = mats[0].shape[0]
    eye = (lax.broadcasted_iota(jnp.int32, (n, n), 0) == lax.broadcasted_iota(jnp.int32, (n, n), 1)).astype(F32)
    ps = [-a for a in mats]
    ts = [eye + p for p in ps]
    for _ in range(max(1, (n - 1).bit_length()) - 1):
        ps = [_hdot(p, p) for p in ps]
        ts = [t + _hdot(t, p) for t, p in zip(ts, ps)]
    return tuple(ts)


@jax.custom_vjp
def _inv(mats):
    return _inv_impl(mats)


def _inv_fwd(mats):
    ts = _inv_impl(mats)
    return ts, ts


def _inv_bwd(ts, gs):
    xs = [_hdot(t, g, 0, 0) for t, g in zip(ts, gs)]
    return (tuple(-_hdot(x, t, 1, 1) for x, t in zip(xs, ts)),)


_inv.defvjp(_inv_fwd, _inv_bwd)


@jax.custom_vjp
def _inv_saved(mats, saved):
    return saved


_inv_saved.defvjp(lambda mats, saved: (saved, saved),
                  lambda ts, gs: (_inv_bwd(ts, gs)[0], tuple(jnp.zeros_like(t) for t in ts)))


def _tile(n, cap):
    if n <= cap:
        return n
    best = [t for t in range(128, cap + 1, 128) if n % t == 0]
    assert best, (n, cap)
    return best[-1]


def _mm(a, b, *, name, ta=False, tb=False, add=None, tm=1024, tn=1024, tk=1024, after=None, out_dtype=F32):
    if ta:
        k_, m_ = a.shape
    else:
        m_, k_ = a.shape
    n_ = b.shape[0] if tb else b.shape[1]
    tm, tn, tk = _tile(m_, tm), _tile(n_, tn), _tile(k_, tk)
    nk = k_ // tk
    has_add = add is not None

    def body(*refs):
        a_ref, b_ref = refs[:2]
        add_ref = refs[2] if has_add else None
        o_ref = refs[2 + has_add + (after is not None)]
        part = _raw_dot(a_ref[...], b_ref[...], 0 if ta else 1, 1 if tb else 0)
        if nk == 1:
            o_ref[...] = (part + add_ref[...] if has_add else part).astype(o_ref.dtype)
            return
        acc = refs[-1]
        k = pl.program_id(2)

        @pl.when(k == 0)
        def _():
            acc[...] = part

        @pl.when(jnp.logical_and(k > 0, k < nk - 1))
        def _():
            acc[...] += part

        @pl.when(k == nk - 1)
        def _():
            o_ref[...] = (acc[...] + part + add_ref[...] if has_add else acc[...] + part).astype(o_ref.dtype)

    a_spec = pl.BlockSpec((tk, tm), lambda i, j, k: (k, i)) if ta else pl.BlockSpec((tm, tk), lambda i, j, k: (i, k))
    b_spec = pl.BlockSpec((tn, tk), lambda i, j, k: (j, k)) if tb else pl.BlockSpec((tk, tn), lambda i, j, k: (k, j))
    o_spec = pl.BlockSpec((tm, tn), lambda i, j, k: (i, j))
    in_specs = [a_spec, b_spec] + ([o_spec] if has_add else []) + ([ANY] if after is not None else [])
    args = (a, b) + ((add,) if has_add else ()) + ((after,) if after is not None else ())
    return pl.pallas_call(
        body, name=name, grid=(m_ // tm, n_ // tn, nk), in_specs=in_specs, out_specs=o_spec,
        out_shape=jax.ShapeDtypeStruct((m_, n_), out_dtype),
        scratch_shapes=[pltpu.VMEM((tm, tn), F32)] if nk > 1 else [],
        compiler_params=_params(3))(*args)


def _mm_fused(a, b, post, outs, *, name, rows=(), params=(), n_sums=0, tb=False, tm=1024, tn=1024, tk=1024):
    m_, k_ = a.shape
    n_ = b.shape[0] if tb else b.shape[1]
    tm, tn, tk = _tile(m_, tm), _tile(n_, tn), _tile(k_, tk)
    nk = k_ // tk
    nr, npar, no = len(rows), len(params), len(outs)
    aliased = [o[4] for o in outs if o[4] is not None]
    nin = 2 + nr + npar

    def body(*refs):
        a_ref, b_ref = refs[:2]
        row_refs, par_refs = refs[2:2 + nr], refs[2 + nr:nin]
        out_refs = refs[nin + len(aliased):nin + len(aliased) + no + n_sums]
        part = _raw_dot(a_ref[...], b_ref[...], 1, 1 if tb else 0)
        k = pl.program_id(2)
        first = jnp.logical_and(pl.program_id(0) == 0, pl.program_id(1) == 0)
        if nk > 1:
            acc = refs[-1]

            @pl.when(k == 0)
            def _():
                acc[...] = part

            @pl.when(jnp.logical_and(k > 0, k < nk - 1))
            def _():
                acc[...] += part

        @pl.when(k == nk - 1)
        def _():
            total = acc[...] + part if nk > 1 else part
            res = post(total, *[r[...].astype(F32) for r in row_refs], *[p[...] for p in par_refs])
            for o_ref, val in zip(out_refs[:no], res[:no]):
                o_ref[...] = val.astype(o_ref.dtype)
            for s_ref, val in zip(out_refs[no:], res[no:]):
                @pl.when(first)
                def _(s_ref=s_ref):
                    s_ref[...] = jnp.zeros_like(s_ref)
                s_ref[...] += val

    b_spec = pl.BlockSpec((tn, tk), lambda i, j, k: (j, k)) if tb else pl.BlockSpec((tk, tn), lambda i, j, k: (k, j))
    in_specs = [pl.BlockSpec((tm, tk), lambda i, j, k: (i, k)), b_spec]
    in_specs += [pl.BlockSpec((tm, w), lambda i, j, k, cb=cb: (i, cb(j))) for _, w, cb in rows]
    in_specs += [_full(p) for p in params] + [ANY] * len(aliased)
    out_specs = [pl.BlockSpec((tm, w), lambda i, j, k, cb=cb: (i, cb(j))) for _, w, _, cb, _ in outs]
    out_specs += [_full(p) for p in params[:n_sums]]
    out_shape = [jax.ShapeDtypeStruct((m_, tot), dt) for tot, _, dt, _, _ in outs]
    out_shape += [jax.ShapeDtypeStruct(p.shape, F32) for p in params[:n_sums]]
    aliases, pos = {}, nin
    for oi, o in enumerate(outs):
        if o[4] is not None:
            aliases[pos] = oi
            pos += 1
    return pl.pallas_call(
        body, name=name, grid=(m_ // tm, n_ // tn, nk), in_specs=in_specs, out_specs=out_specs, out_shape=out_shape,
        scratch_shapes=[pltpu.VMEM((tm, tn), F32)] if nk > 1 else [], input_output_aliases=aliases,
        compiler_params=_params(3))(a, b, *[r[0] for r in rows], *params, *aliased)


def _mm_pre(pre, rows, params, b, *, name, tm=512):
    m_ = rows[0][0].shape[0]
    k_, n_ = b.shape
    tm = _tile(m_, tm)
    nr, npar = len(rows), len(params)

    def body(*refs):
        b_ref, a_out, o_ref = refs[nr + npar:]
        a = pre(*[r[...].astype(F32) for r in refs[:nr]], *[p[...] for p in refs[nr:nr + npar]]).astype(a_out.dtype)
        a_out[...] = a
        o_ref[...] = _raw_dot(a, b_ref[...], 1, 0)

    return pl.pallas_call(
        body, name=name, grid=(m_ // tm,),
        in_specs=[pl.BlockSpec((tm, w), lambda i, cb=cb: (i, cb)) for _, w, cb in rows] + [_full(p) for p in params]
        + [_full(b)],
        out_specs=[pl.BlockSpec((tm, k_), lambda i: (i, 0)), pl.BlockSpec((tm, n_), lambda i: (i, 0))],
        out_shape=[jax.ShapeDtypeStruct((m_, k_), _MXU), jax.ShapeDtypeStruct((m_, n_), F32)],
        compiler_params=_params(1))(*[r[0] for r in rows], *params, b)


def _row_spec(tb, w, c0, percol):
    return pl.BlockSpec((tb, w), lambda i, j, c0=c0, pc=percol: (i, c0 + (j if pc else 0)))


def _stage_fwd(f, name, rows, params, outs, tb, ncol=1):
    t_ = rows[0][0].shape[0]
    nr, npar = len(rows), len(params)

    def body(*refs):
        res = f(*[r[...].astype(F32) for r in refs[:nr + npar]])
        for o_ref, val in zip(refs[nr + npar:], res):
            o_ref[...] = val.astype(o_ref.dtype)

    return pl.pallas_call(
        body, name=name, grid=(t_ // tb, ncol),
        in_specs=[_row_spec(tb, w, c0, pc) for (_, w, c0, pc) in rows] + [_full(p) for p in params],
        out_specs=[pl.BlockSpec((tb, w), lambda i, j: (i, j)) for w, _ in outs],
        out_shape=[jax.ShapeDtypeStruct((t_, w * ncol), dt) for w, dt in outs],
        compiler_params=_params(2))(*[r[0] for r in rows], *params)


def _into(dest, tb, width, t_, ncol, dtype):
    if dest is None:
        return pl.BlockSpec((tb, width), lambda i, j: (i, j)), jax.ShapeDtypeStruct((t_, width * ncol), dtype), None
    buf, total, c0 = dest
    return (pl.BlockSpec((tb, width), lambda i, j, c0=c0: (i, c0 + j)), jax.ShapeDtypeStruct((t_, total), dtype), buf)


def _stage_bwd(f, name, rows, params, douts, tb, ncol=1, cat=None, dtypes=None, dest=None, after=None):
    t_ = rows[0][0].shape[0]
    nr, npar, nd = len(rows), len(params), len(douts)
    cat = cat if cat is not None else [[r] for r in range(nr)]
    dtypes = dtypes if dtypes is not None else [F32] * len(cat)
    dest = dest or {}
    assert ncol == 1 or all(len(g) == 1 and rows[g[0]][3] for g in cat)
    nin = nr + npar + nd
    unread = [after] if after is not None else []

    def body(*refs):
        ins = [r[...].astype(F32) for r in refs[:nr + npar]]
        dvals = tuple(r[...].astype(F32) for r in refs[nr + npar:nin])
        out_refs = refs[nin + len(aliased) + len(unread):]
        _, vjp = jax.vjp(f, *ins)
        grads = vjp(dvals)
        for o_ref, grp in zip(out_refs[:len(cat)], cat):
            val = grads[grp[0]] if len(grp) == 1 else jnp.concatenate([grads[r] for r in grp], axis=-1)
            o_ref[...] = val.astype(o_ref.dtype)
        first = jnp.logical_and(pl.program_id(0) == 0, pl.program_id(1) == 0)
        for p_ref, gp in zip(out_refs[len(cat):], grads[nr:]):
            @pl.when(first)
            def _(p_ref=p_ref):
                p_ref[...] = jnp.zeros_like(p_ref)
            p_ref[...] += gp

    gw = [sum(rows[r][1] for r in grp) for grp in cat]
    out_specs, out_shape, aliased, aliases = [], [], [], {}
    for gi, (w, dt) in enumerate(zip(gw, dtypes)):
        spec, shape, buf = _into(dest.get(gi), tb, w, t_, ncol, dt)
        out_specs.append(spec)
        out_shape.append(shape)
        if buf is not None:
            aliases[nin + len(aliased)] = gi
            aliased.append(buf)
    return pl.pallas_call(
        body, name=name, grid=(t_ // tb, ncol),
        in_specs=[_row_spec(tb, w, c0, pc) for (_, w, c0, pc) in rows] + [_full(p) for p in params]
        + [pl.BlockSpec((tb, d.shape[1] // ncol), lambda i, j: (i, j)) for d in douts]
        + [ANY] * (len(aliased) + len(unread)),
        out_specs=out_specs + [_full(p) for p in params],
        out_shape=out_shape + [jax.ShapeDtypeStruct(p.shape, F32) for p in params],
        input_output_aliases=aliases,
        compiler_params=_params(2))(*[r[0] for r in rows], *params, *douts, *aliased, *unread)


def _ln_f(x, y, g, b):
    u = ALPHA * x + y
    c = u - jnp.mean(u, axis=-1, keepdims=True)
    var = jnp.mean(c * c, axis=-1, keepdims=True)
    return (c * lax.rsqrt(var + LN_EPS) * g + b,)


def _ln_f2(x, y, g, b):
    out, = _ln_f(x, y, g, b)
    return out, out


def _swiglu_tile(gu):
    half = gu.shape[1] // 2
    return (jax.nn.silu(gu[:, :half]) * gu[:, half:],)


def _merge_f(ya, yb, ga, gb):
    return (jax.nn.sigmoid(ga) * ya + jax.nn.sigmoid(gb) * yb,)


def _gnorm_f(of, ob, z, w):
    o = of + ob
    return (o * lax.rsqrt(jnp.mean(o * o, axis=-1, keepdims=True) + RMS_EPS) * w * jax.nn.silu(z),)


def _bg_f(x, arow, dtrow):
    lane = lax.broadcasted_iota(jnp.int32, x.shape, 1)
    beta = jax.nn.sigmoid(x)
    g = -jnp.exp(arow) * jax.nn.softplus(x + dtrow)
    return (jnp.where(lane < 16, beta, jnp.where(lane < 32, g, 0.0)),)


PREP_ROWS = 512
PAD = 8


def _prep_f(part, w, *wins):
    xc = wins[0] * w[0:1, :]
    for k in range(1, DN_CONV):
        xc = xc + wins[k] * w[k:k + 1, :]
    a = jax.nn.silu(xc)
    nrm = a * lax.rsqrt(jnp.sum(a * a, axis=-1, keepdims=True) + RMS_EPS)
    return jnp.where(part == 0, nrm * (DN_DIM ** -0.5), jnp.where(part == 1, nrm, a))


def _windows(pad_ref, r0, rows):
    return [pad_ref[PAD + r0 - 2 + k:PAD + r0 - 2 + k + rows, :] for k in range(DN_CONV)]


def _prep_fwd(pm, conv):
    t_ = pm.shape[0]
    rows = min(PREP_ROWS, t_)

    def body(x_ref, w_ref, o_ref, pad_ref):
        part = pl.program_id(0) // DN_HEADS
        pad_ref[0:PAD, :] = jnp.zeros((PAD, DN_DIM), F32)
        pad_ref[PAD + t_:2 * PAD + t_, :] = jnp.zeros((PAD, DN_DIM), F32)
        pad_ref[PAD:PAD + t_, :] = x_ref[...]
        w = w_ref[...]
        for r in range(t_ // rows):
            o_ref[r * rows:(r + 1) * rows, :] = _prep_f(part, w, *_windows(pad_ref, r * rows, rows))

    ncb = 3 * DN_HEADS
    return pl.pallas_call(
        body, name="prep_fwd", grid=(ncb,),
        in_specs=[pl.BlockSpec((t_, DN_DIM), lambda j: (0, j)), pl.BlockSpec((DN_CONV, DN_DIM), lambda j: (0, j))],
        out_specs=pl.BlockSpec((t_, DN_DIM), lambda j: (0, j)),
        out_shape=jax.ShapeDtypeStruct((t_, ncb * DN_DIM), F32),
        scratch_shapes=[pltpu.VMEM((t_ + 2 * PAD, DN_DIM), F32)],
        compiler_params=_params(1))(pm, conv)


def _prep_bwd(pm, conv, dout, dpm):
    t_ = pm.shape[0]
    rows = min(PREP_ROWS, t_)

    def body(x_ref, w_ref, d_ref, _, dx_ref, dw_ref, pad_ref, dpad_ref):
        part = pl.program_id(0) // DN_HEADS
        pad_ref[0:PAD, :] = jnp.zeros((PAD, DN_DIM), F32)
        pad_ref[PAD + t_:2 * PAD + t_, :] = jnp.zeros((PAD, DN_DIM), F32)
        pad_ref[PAD:PAD + t_, :] = x_ref[...]
        dpad_ref[...] = jnp.zeros_like(dpad_ref)
        w = w_ref[...]
        dw = jnp.zeros((DN_CONV, DN_DIM), F32)
        for r in range(t_ // rows):
            r0 = r * rows
            _, vjp = jax.vjp(functools.partial(_prep_f, part), w, *_windows(pad_ref, r0, rows))
            grads = vjp(d_ref[r0:r0 + rows, :])
            dw = dw + grads[0]
            for k in range(DN_CONV):
                lo = PAD + r0 - 2 + k
                dpad_ref[lo:lo + rows, :] += grads[1 + k]
        dx_ref[...] = dpad_ref[PAD:PAD + t_, :].astype(dx_ref.dtype)
        dw_ref[...] = dw

    ncb = 3 * DN_HEADS
    col = pl.BlockSpec((t_, DN_DIM), lambda j: (0, j))
    wsp = pl.BlockSpec((DN_CONV, DN_DIM), lambda j: (0, j))
    return pl.pallas_call(
        body, name="prep_bwd", grid=(ncb,), in_specs=[col, wsp, col, ANY], out_specs=[col, wsp],
        out_shape=[jax.ShapeDtypeStruct(dpm.shape, dpm.dtype), jax.ShapeDtypeStruct((DN_CONV, ncb * DN_DIM), F32)],
        scratch_shapes=[pltpu.VMEM((t_ + 2 * PAD, DN_DIM), F32), pltpu.VMEM((t_ + 2 * PAD, DN_DIM), F32)],
        input_output_aliases={3: 0}, compiler_params=_params(1))(pm, conv, dout, dpm)


def _dn_chunk(sgns, qs, ks, vs, grows, brows, tsaved=None, with_t=False):
    c = qs[0].shape[0]
    i = lax.broadcasted_iota(jnp.int32, (c, c), 0)
    j = lax.broadcasted_iota(jnp.int32, (c, c), 1)
    eye = i == j
    incl = {s: (i - j) * int(s) >= 0 for s in set(sgns)}
    strict = {s: (i - j) * int(s) > 0 for s in set(sgns)}
    gcs = [jnp.sum(jnp.where(incl[s], g, 0.0), axis=1, keepdims=True) for s, g in zip(sgns, grows)]
    grs = [jnp.sum(jnp.where(eye, gc, 0.0), axis=0, keepdims=True) for gc in gcs]
    bcs = [jnp.sum(jnp.where(eye, b, 0.0), axis=1, keepdims=True) for b in brows]
    gls = [jnp.sum(g, axis=1, keepdims=True) for g in grows]
    decs = [jnp.exp(jnp.where(incl[s], gc - gr, -1e30)) for s, gc, gr in zip(sgns, gcs, grs)]
    kks = [_nt(k, k) for k in ks]
    amats = tuple(jnp.where(strict[s], bc * kk * dec, 0.0) for s, bc, kk, dec in zip(sgns, bcs, kks, decs))
    tinvs = _inv(amats) if tsaved is None else _inv_saved(amats, tsaved)
    egcs = [jnp.exp(gc) for gc in gcs]
    us = [_nn(t, v * bc) for t, v, bc in zip(tinvs, vs, bcs)]
    ws = [_nn(t, k * (bc * egc)) for t, k, bc, egc in zip(tinvs, ks, bcs, egcs)]
    qks = [_nt(q, k) * dec for q, k, dec in zip(qs, ks, decs)]
    qds = [q * egc for q, egc in zip(qs, egcs)]
    kds = [k * jnp.exp(gl - gc) for k, gl, gc in zip(ks, gls, gcs)]
    res = tuple(us), tuple(ws), tuple(qks), tuple(qds), tuple(kds)
    return res + (tinvs,) if with_t else res


def _dn_step(us, ws, qks, qds, kds, grows, ss):
    gls = [jnp.exp(jnp.sum(g, axis=1, keepdims=True)) for g in grows]
    c = ws[0].shape[0]
    both = [_nn(jnp.concatenate([w, qd], axis=0), s) for w, qd, s in zip(ws, qds, ss)]
    wss = [b[:c] for b in both]
    qss = [b[c:] for b in both]
    vns = [u - x for u, x in zip(us, wss)]
    os_ = [a + _nn(qk, vn) for a, qk, vn in zip(qss, qks, vns)]
    s2s = [s * gl + _tn(kd, vn) for s, gl, kd, vn in zip(ss, gls, kds, vns)]
    return tuple(os_), tuple(s2s)


def _hs(h):
    return slice(h * DN_DIM, (h + 1) * DN_DIM)


_DIR_SGN = (1, -1)
_A_OUT = 5
_PROBLEMS = [(d, h) for d in range(2) for h in range(DN_HEADS)]
_SGNS = [_DIR_SGN[d] for d, _ in _PROBLEMS]


def _chunk_inputs(q_ref, k_ref, v_ref, g_ref, b_ref):
    heads = lambda ref: tuple(ref[:, _hs(h)].astype(F32) for _, h in _PROBLEMS)
    rows = lambda ref: tuple(ref[d, 0, h:h + 1, :] for d, h in _PROBLEMS)
    return heads(q_ref), heads(k_ref), heads(v_ref), rows(g_ref), rows(b_ref)


def _dn_a_fwd(qkv, grows, brows):
    t_ = qkv.shape[0]
    nch = t_ // CHUNK

    def body(q_ref, k_ref, v_ref, g_ref, b_ref, *outs):
        us, ws, qks, qds, kds, tinvs = _dn_chunk(_SGNS, *_chunk_inputs(q_ref, k_ref, v_ref, g_ref, b_ref), with_t=True)
        for p, (d, h) in enumerate(_PROBLEMS):
            u_ref, w_ref, qk_ref, qd_ref, kd_ref = outs[d * _A_OUT:(d + 1) * _A_OUT]
            u_ref[:, _hs(h)], w_ref[:, _hs(h)], qk_ref[0, h] = us[p], ws[p].astype(_MXU), qks[p].astype(_MXU)
            qd_ref[:, _hs(h)], kd_ref[:, _hs(h)] = qds[p].astype(_MXU), kds[p].astype(_MXU)
            outs[2 * _A_OUT + d][0, h] = tinvs[p]

    rspec = pl.BlockSpec((2, 1, DN_HEADS, CHUNK), lambda c: (0, c, 0, 0))
    big = pl.BlockSpec((CHUNK, D), lambda c: (c, 0))
    qks = pl.BlockSpec((1, DN_HEADS, CHUNK, CHUNK), lambda c: (c, 0, 0, 0))
    bigs = lambda dt: jax.ShapeDtypeStruct((t_, D), dt)
    qksh = lambda dt: jax.ShapeDtypeStruct((nch, DN_HEADS, CHUNK, CHUNK), dt)
    res = pl.pallas_call(
        body, name="dn_a_fwd", grid=(nch,),
        in_specs=[pl.BlockSpec((CHUNK, D), lambda c, p=p: (c, p)) for p in range(3)] + [rspec, rspec],
        out_specs=[big, big, qks, big, big] * 2 + [qks, qks],
        out_shape=[bigs(F32), bigs(_MXU), qksh(_MXU), bigs(_MXU), bigs(_MXU)] * 2 + [qksh(F32)] * 2,
        compiler_params=_params(1))(qkv, qkv, qkv, grows, brows)
    return res[:2 * _A_OUT], res[2 * _A_OUT:]


def _dn_a_bwd(qkv, grows, brows, tinv, dres, dg_b):
    t_ = qkv.shape[0]
    nch = t_ // CHUNK

    def body(q_ref, k_ref, v_ref, g_ref, b_ref, tf_ref, tb_ref, *rest):
        dins, dgb_ref, (dqkv_ref, dg_ref, db_ref) = rest[:2 * _A_OUT], rest[2 * _A_OUT], rest[2 * _A_OUT + 1:]
        tsaved = tuple((tf_ref, tb_ref)[d][0, h] for d, h in _PROBLEMS)
        _, vjp = jax.vjp(functools.partial(_dn_chunk, _SGNS, tsaved=tsaved),
                         *_chunk_inputs(q_ref, k_ref, v_ref, g_ref, b_ref))
        cots = []
        for o in range(_A_OUT):
            cots.append(tuple(dins[d * _A_OUT + o][0, h] if o == 2 else dins[d * _A_OUT + o][:, _hs(h)]
                              for d, h in _PROBLEMS))
        gq, gk, gv, gg, gb = vjp(tuple(cots))
        for p, (d, h) in enumerate(_PROBLEMS):
            dg_ref[d, 0, h:h + 1, :] = gg[p] + dgb_ref[d, 0, h:h + 1, :]
            db_ref[d, 0, h:h + 1, :] = gb[p]
        for h in range(DN_HEADS):
            dqkv_ref[:, _hs(h)] = gq[h] + gq[DN_HEADS + h]
            dqkv_ref[:, _hs(DN_HEADS + h)] = gk[h] + gk[DN_HEADS + h]
            dqkv_ref[:, _hs(2 * DN_HEADS + h)] = gv[h] + gv[DN_HEADS + h]

    rspec = pl.BlockSpec((2, 1, DN_HEADS, CHUNK), lambda c: (0, c, 0, 0))
    big = pl.BlockSpec((CHUNK, D), lambda c: (c, 0))
    qks = pl.BlockSpec((1, DN_HEADS, CHUNK, CHUNK), lambda c: (c, 0, 0, 0))
    rsh = jax.ShapeDtypeStruct(grows.shape, F32)
    return pl.pallas_call(
        body, name="dn_a_bwd", grid=(nch,),
        in_specs=[pl.BlockSpec((CHUNK, D), lambda c, p=p: (c, p)) for p in range(3)] + [rspec, rspec, qks, qks]
        + [big, big, qks, big, big] * 2 + [rspec],
        out_specs=[pl.BlockSpec((CHUNK, 3 * D), lambda c: (c, 0)), rspec, rspec],
        out_shape=[jax.ShapeDtypeStruct((t_, 3 * D), F32), rsh, rsh],
        compiler_params=_params(1))(qkv, qkv, qkv, grows, brows, *tinv, *dres, dg_b)


def _dir_specs(nch):
    def cidx(d):
        return (lambda n: n) if d == 0 else (lambda n: nch - 1 - n)
    out = []
    for d in range(2):
        ci = cidx(d)
        big = pl.BlockSpec((CHUNK, D), lambda n, ci=ci: (ci(n), 0))
        qks = pl.BlockSpec((1, DN_HEADS, CHUNK, CHUNK), lambda n, ci=ci: (ci(n), 0, 0, 0))
        row = pl.BlockSpec((1, 1, DN_HEADS, CHUNK), lambda n, ci=ci, d=d: (d, ci(n), 0, 0))
        st = pl.BlockSpec((1, DN_HEADS, DN_DIM, DN_DIM), lambda n, ci=ci: (ci(n), 0, 0, 0))
        out.append(dict(big=big, qk=qks, row=row, st=st))
    return out


def _step_inputs(ins, per_dir):
    def pick(o):
        if o == 2:
            return tuple(ins[d * per_dir + o][0, h].astype(F32) for d, h in _PROBLEMS)
        if o == 5:
            return tuple(ins[d * per_dir + o][0, 0, h:h + 1, :] for d, h in _PROBLEMS)
        return tuple(ins[d * per_dir + o][:, _hs(h)].astype(F32) for d, h in _PROBLEMS)
    return [pick(o) for o in range(6)]


def _dn_b_fwd(ares, grows):
    t_ = ares[0].shape[0]
    nch = t_ // CHUNK
    sp = _dir_specs(nch)

    def body(*refs):
        ins, outs, s_ref = refs[:12], refs[12:16], refs[16]

        @pl.when(pl.program_id(0) == 0)
        def _():
            s_ref[...] = jnp.zeros_like(s_ref)

        ss = tuple(s_ref[p] for p in range(len(_PROBLEMS)))
        os_, s2s = _dn_step(*_step_inputs(ins, 6), ss)
        for p, (d, h) in enumerate(_PROBLEMS):
            outs[2 + d][0, h] = ss[p]
            outs[d][:, _hs(h)] = os_[p]
            s_ref[p] = s2s[p]

    in_specs, args = [], []
    for d in range(2):
        in_specs += [sp[d]["big"], sp[d]["big"], sp[d]["qk"], sp[d]["big"], sp[d]["big"], sp[d]["row"]]
        args += list(ares[d * _A_OUT:(d + 1) * _A_OUT]) + [grows]
    stsh = jax.ShapeDtypeStruct((nch, DN_HEADS, DN_DIM, DN_DIM), F32)
    osh = jax.ShapeDtypeStruct((t_, D), F32)
    return pl.pallas_call(
        body, name="dn_b_fwd", grid=(nch,), in_specs=in_specs,
        out_specs=[sp[0]["big"], sp[1]["big"], sp[0]["st"], sp[1]["st"]], out_shape=[osh, osh, stsh, stsh],
        scratch_shapes=[pltpu.VMEM((2 * DN_HEADS, DN_DIM, DN_DIM), F32)],
        compiler_params=_params(1))(*args)


def _dn_b_bwd(ares, grows, st_f, st_b, do):
    t_ = ares[0].shape[0]
    nch = t_ // CHUNK
    sp = _dir_specs(nch)
    rsp = [sp[1], sp[0]]

    def body(*refs):
        ins, outs, ds_ref = refs[:16], refs[16:28], refs[28]

        @pl.when(pl.program_id(0) == 0)
        def _():
            ds_ref[...] = jnp.zeros_like(ds_ref)

        ss = tuple(ins[d * 8 + 6][0, h] for d, h in _PROBLEMS)
        _, vjp = jax.vjp(_dn_step, *_step_inputs(ins, 8), ss)
        dos = tuple(ins[d * 8 + 7][:, _hs(h)] for d, h in _PROBLEMS)
        grads = vjp((dos, tuple(ds_ref[p] for p in range(len(_PROBLEMS)))))
        for p, (d, h) in enumerate(_PROBLEMS):
            du_ref, dw_ref, dqk_ref, dqd_ref, dkd_ref, dg_ref = outs[d * 6:(d + 1) * 6]
            du_ref[:, _hs(h)], dw_ref[:, _hs(h)], dqk_ref[0, h] = grads[0][p], grads[1][p], grads[2][p]
            dqd_ref[:, _hs(h)], dkd_ref[:, _hs(h)] = grads[3][p], grads[4][p]
            dg_ref[0, 0, h:h + 1, :] = grads[5][p]
            ds_ref[p] = grads[6][p]

    in_specs, args, out_specs, out_shape = [], [], [], []
    big_sh = jax.ShapeDtypeStruct((t_, D), F32)
    qk_sh = jax.ShapeDtypeStruct((nch, DN_HEADS, CHUNK, CHUNK), F32)
    row_sh = jax.ShapeDtypeStruct((1, nch, DN_HEADS, CHUNK), F32)
    for d in range(2):
        s = rsp[d]
        row0 = pl.BlockSpec((1, 1, DN_HEADS, CHUNK), lambda m, d=d: (0, (nch - 1 - m) if d == 0 else m, 0, 0))
        rowd = pl.BlockSpec((1, 1, DN_HEADS, CHUNK), lambda m, d=d: (d, (nch - 1 - m) if d == 0 else m, 0, 0))
        in_specs += [s["big"], s["big"], s["qk"], s["big"], s["big"], rowd, s["st"], s["big"]]
        args += list(ares[d * _A_OUT:(d + 1) * _A_OUT]) + [grows, (st_f, st_b)[d], do]
        out_specs += [s["big"], s["big"], s["qk"], s["big"], s["big"], row0]
        out_shape += [big_sh, big_sh, qk_sh, big_sh, big_sh, row_sh]
    res = pl.pallas_call(
        body, name="dn_b_bwd", grid=(nch,), in_specs=in_specs, out_specs=out_specs, out_shape=out_shape,
        scratch_shapes=[pltpu.VMEM((2 * DN_HEADS, DN_DIM, DN_DIM), F32)],
        compiler_params=_params(1))(*args)
    dares = list(res[0:5]) + list(res[6:11])
    return dares, jnp.concatenate([res[5], res[11]], axis=0)


@jax.custom_vjp
def _rot_half(x):
    half, width = SW_DIM // 2, x.shape[1]
    first = lax.broadcasted_iota(jnp.int32, x.shape, 1) % SW_DIM < half
    return jnp.where(first, -pltpu.roll(x, width - half, axis=1), pltpu.roll(x, half, axis=1))


_rot_half.defvjp(lambda x: (_rot_half(x), None), lambda _, g: (-_rot_half(g),))


def _rope(x, c, s):
    reps = x.shape[1] // c.shape[1]
    return x * jnp.tile(c, (1, reps)) + _rot_half(x) * jnp.tile(s, (1, reps))


def _rope_t(g, c, s):
    reps = g.shape[1] // c.shape[1]
    return g * jnp.tile(c, (1, reps)) - _rot_half(g * jnp.tile(s, (1, reps)))


_SW_SCALE = SW_DIM ** -0.5
_KV_HEADS = [[kvh * SW_GRP + g for g in range(SW_GRP)] for kvh in range(SW_KV)]


def _by_group(x):
    return [jnp.concatenate([x[:, h * SW_DIM:(h + 1) * SW_DIM] for h in hs], axis=0) for hs in _KV_HEADS]


def _from_groups(xs):
    return jnp.concatenate([x[g * SW_BLOCK:(g + 1) * SW_BLOCK] for x in xs for g in range(SW_GRP)], axis=-1)


def _attn_probs(blk, t_, cq, sq, ck, sk, q, kall, sinks):
    qgs = _by_group(_rope(q, cq, sq))
    kr = _rope(kall, ck, sk)
    khs = [kr[:, kvh * SW_DIM:(kvh + 1) * SW_DIM] for kvh in range(SW_KV)]
    nq, nk = SW_GRP * SW_BLOCK, 3 * SW_BLOCK
    qpos = lax.broadcasted_iota(jnp.int32, (nq, nk), 0) % SW_BLOCK
    krel = lax.broadcasted_iota(jnp.int32, (nq, nk), 1) - SW_BLOCK
    kglob = krel + blk * SW_BLOCK
    valid = (jnp.abs(qpos - krel) <= SW_BLOCK) & (kglob >= 0) & (kglob < t_)
    ss = [jnp.where(valid, _nt(qg * _SW_SCALE, kh), -1e30) for qg, kh in zip(qgs, khs)]
    snks = [jnp.concatenate([jnp.broadcast_to(sinks[:, h:h + 1], (SW_BLOCK, 1)) for h in hs], axis=0) for hs in _KV_HEADS]
    ms = [jnp.maximum(jnp.max(s, axis=-1, keepdims=True), snk) for s, snk in zip(ss, snks)]
    es = [jnp.exp(s - m) for s, m in zip(ss, ms)]
    esnks = [jnp.exp(snk - m) for snk, m in zip(snks, ms)]
    invs = [1.0 / (jnp.sum(e, axis=-1, keepdims=True) + esnk) for e, esnk in zip(es, esnks)]
    ps = [e * inv for e, inv in zip(es, invs)]
    return qgs, khs, ps, [esnk * inv for esnk, inv in zip(esnks, invs)]


def _attn_f(blk, t_, cq, sq, ck, sk, q, kp, ko, kn, vp, vo, vn, sinks):
    _, _, ps, psinks = _attn_probs(blk, t_, cq, sq, ck, sk, q, jnp.concatenate([kp, ko, kn], axis=0), sinks)
    vall = jnp.concatenate([vp, vo, vn], axis=0)
    ps = [p.astype(_MXU) for p in ps]
    kept = [jnp.concatenate([p, jnp.broadcast_to(psink.astype(_MXU), (p.shape[0], 128))], axis=-1)
            for p, psink in zip(ps, psinks)]
    return _from_groups([_nn(p, vall[:, kvh * SW_DIM:(kvh + 1) * SW_DIM]) for kvh, p in enumerate(ps)]), kept


def _attn_bwd_f(cq, sq, ck, sk, q, kp, ko, kn, vp, vo, vn, do, kept):
    nk = 3 * SW_BLOCK
    ps = [x[:, :nk] for x in kept]
    psinks = [x[:, nk:nk + 1].astype(F32) for x in kept]
    qgs = _by_group(_rope(q, cq, sq))
    kr = _rope(jnp.concatenate([kp, ko, kn], axis=0), ck, sk)
    khs = [kr[:, kvh * SW_DIM:(kvh + 1) * SW_DIM] for kvh in range(SW_KV)]
    vall = jnp.concatenate([vp, vo, vn], axis=0)
    vhs = [vall[:, kvh * SW_DIM:(kvh + 1) * SW_DIM] for kvh in range(SW_KV)]
    dogs = _by_group(do)
    dvs = [_tn(p, dog) for p, dog in zip(ps, dogs)]
    ps = [p.astype(F32) for p in ps]
    dps = [_nt(dog * _SW_SCALE, vh) for dog, vh in zip(dogs, vhs)]
    deltas = [jnp.sum(p * dp, axis=-1, keepdims=True) for p, dp in zip(ps, dps)]
    dss = [p * (dp - delta) for p, dp, delta in zip(ps, dps, deltas)]
    dqr = _from_groups([_nn(ds, kh) for ds, kh in zip(dss, khs)])
    dkr = jnp.concatenate([_tn(ds, qg) for ds, qg in zip(dss, qgs)], axis=-1)
    dsnk = [-(psink * delta) * (1.0 / _SW_SCALE) for psink, delta in zip(psinks, deltas)]
    dsinks = jnp.concatenate([jnp.sum(d[g * SW_BLOCK:(g + 1) * SW_BLOCK], axis=0, keepdims=True)
                              for d in dsnk for g in range(SW_GRP)], axis=1)
    dq, dk, dv = _rope_t(dqr, cq, sq), _rope_t(dkr, ck, sk), jnp.concatenate(dvs, axis=-1)
    blocks = lambda a: [a[j * SW_BLOCK:(j + 1) * SW_BLOCK] for j in range(3)]
    return [dq] + blocks(dk) + blocks(dv) + [dsinks]


def _attn_specs(nb):
    prv = lambda i: jnp.maximum(i - 1, 0)
    nxt = lambda i: jnp.minimum(i + 1, nb - 1)
    rows = [lambda i: i, prv, lambda i: i, nxt]
    tab = [pl.BlockSpec((SW_BLOCK, 128), lambda i, r=r: (r(i), 0)) for r in rows]
    qs = pl.BlockSpec((SW_BLOCK, SW_HEADS * SW_DIM), lambda i: (i, O_QS // (SW_HEADS * SW_DIM)))
    kw = SW_KV * SW_DIM
    ks = [pl.BlockSpec((SW_BLOCK, kw), lambda i, r=r: (r(i), O_KS // kw)) for r in rows[1:]]
    vs = [pl.BlockSpec((SW_BLOCK, kw), lambda i, r=r: (r(i), O_VS // kw)) for r in rows[1:]]
    return tab, qs, ks, vs


def _attn_tables(refs):
    cq, cp, co, cn, sq, sp_, so, sn = [r[...] for r in refs]
    return cq, sq, jnp.concatenate([cp, co, cn], axis=0), jnp.concatenate([sp_, so, sn], axis=0)


_P_BLOCK = (None, SW_KV, SW_GRP * SW_BLOCK, 3 * SW_BLOCK + 128)


def _attn_fwd(pm, cos, sin, sinks):
    t_ = pm.shape[0]
    nb = t_ // SW_BLOCK
    tab, qs, ks, vs = _attn_specs(nb)

    def body(*refs):
        tabs = _attn_tables(refs[:8])
        vals = [r[...] for r in refs[8:16]]
        out, ps = _attn_f(pl.program_id(0), t_, *tabs, *vals)
        refs[16][...] = out.astype(refs[16].dtype)
        for kvh, p in enumerate(ps):
            refs[17][kvh] = p

    return pl.pallas_call(
        body, name="attn_fwd", grid=(nb,), in_specs=tab + tab + [qs] + ks + vs + [_full(sinks)],
        out_specs=[pl.BlockSpec((SW_BLOCK, D), lambda i: (i, 0)), pl.BlockSpec(_P_BLOCK, lambda i: (i, 0, 0, 0))],
        out_shape=[jax.ShapeDtypeStruct((t_, D), _MXU), jax.ShapeDtypeStruct((nb,) + _P_BLOCK[1:], _MXU)],
        compiler_params=_params(1))(*([cos] * 4), *([sin] * 4), pm, pm, pm, pm, pm, pm, pm, sinks)


def _attn_bwd(pm, cos, sin, sinks, probs, do, dpm):
    t_ = pm.shape[0]
    nb = t_ // SW_BLOCK
    tab, qs, ks, vs = _attn_specs(nb)
    kw = SW_KV * SW_DIM

    def body(*refs):
        tabs = _attn_tables(refs[:8])
        vals = [r[...] for r in refs[8:15]]
        p_ref, do_ref, outs = refs[16], refs[17], refs[19:]
        grads = _attn_bwd_f(*tabs, *vals, do_ref[...], [p_ref[kvh] for kvh in range(SW_KV)])
        for o_ref, g in zip(outs[:7], grads[:7]):
            o_ref[...] = g.astype(o_ref.dtype)

        @pl.when(pl.program_id(0) == 0)
        def _():
            outs[7][...] = jnp.zeros_like(outs[7])
        outs[7][...] += grads[7]

    own = lambda w: pl.BlockSpec((SW_BLOCK, w), lambda i: (i, 0))
    return pl.pallas_call(
        body, name="attn_bwd", grid=(nb,),
        in_specs=tab + tab + [qs] + ks + vs + [_full(sinks), pl.BlockSpec(_P_BLOCK, lambda i: (i, 0, 0, 0)), own(D), ANY],
        out_specs=[pl.BlockSpec((SW_BLOCK, D), lambda i: (i, O_QS // D))] + [own(kw)] * 6 + [_full(sinks)],
        out_shape=[jax.ShapeDtypeStruct(dpm.shape, dpm.dtype)] + [jax.ShapeDtypeStruct((t_, kw), F32)] * 6
        + [jax.ShapeDtypeStruct(sinks.shape, F32)],
        input_output_aliases={18: 0},
        compiler_params=_params(1))(*([cos] * 4), *([sin] * 4), pm, pm, pm, pm, pm, pm, pm, sinks, probs, do, dpm)


def _band_sum(kparts, vparts, dpm):
    t_, kw = kparts[1].shape
    nb = t_ // SW_BLOCK

    def body(kp, ko, kn, vp, vo, vn, _, out_ref):
        j = pl.program_id(0)
        band = lambda p, o, n: o[...] + jnp.where(j + 1 < nb, p[...], 0.0) + jnp.where(j > 0, n[...], 0.0)
        out_ref[...] = jnp.concatenate([band(kp, ko, kn), band(vp, vo, vn)], axis=-1).astype(out_ref.dtype)

    specs = [pl.BlockSpec((SW_BLOCK, kw), lambda j: (jnp.minimum(j + 1, nb - 1), 0)),
             pl.BlockSpec((SW_BLOCK, kw), lambda j: (j, 0)),
             pl.BlockSpec((SW_BLOCK, kw), lambda j: (jnp.maximum(j - 1, 0), 0))]
    return pl.pallas_call(
        body, name="band_sum", grid=(nb,), in_specs=specs * 2 + [ANY],
        out_specs=pl.BlockSpec((SW_BLOCK, 2 * kw), lambda j: (j, O_KS // (2 * kw))),
        out_shape=jax.ShapeDtypeStruct(dpm.shape, dpm.dtype), input_output_aliases={6: 0},
        compiler_params=_params(1))(*kparts, *vparts, dpm)


def _loss_head(y, target, tb=256):
    t_ = y.shape[0]

    def body(y_ref, t_ref, dy_ref, acc_ref):
        err = y_ref[...] - t_ref[...]
        dy_ref[...] = err * (1.0 / D)
        sq = (err * err).reshape(tb // 8, 8, D).sum(axis=0)
        part = sq[:, 0:128]
        for c in range(1, D // 128):
            part = part + sq[:, c * 128:(c + 1) * 128]

        @pl.when(pl.program_id(0) == 0)
        def _():
            acc_ref[...] = jnp.zeros_like(acc_ref)
        acc_ref[...] += part

    row = pl.BlockSpec((tb, D), lambda i: (i, 0))
    return pl.pallas_call(
        body, name="loss_head", grid=(t_ // tb,), in_specs=[row, row],
        out_specs=[row, pl.BlockSpec((8, 128), lambda i: (0, 0))],
        out_shape=[jax.ShapeDtypeStruct((t_, D), F32), jax.ShapeDtypeStruct((8, 128), F32)],
        compiler_params=_params(1))(y, target)


def _adam_math(w, g, m, v):
    bc1 = 1.0 - ADAM_B1 ** ADAM_STEP
    bc2 = 1.0 - ADAM_B2 ** ADAM_STEP
    nm = ADAM_B1 * m + (1.0 - ADAM_B1) * g
    nv = ADAM_B2 * v + (1.0 - ADAM_B2) * (g * g)
    return -ADAM_LR * ((nm / bc1) / (jnp.sqrt(nv / bc2) + ADAM_EPS) + ADAM_WD * w), nm, nv


def _adamw_layer(name, l, w, g, m, v, carry, after):
    nl, r, c = w.shape
    g2 = g.reshape(r, -1)
    gc = g2.shape[1]
    w2, m2, v2 = [a.reshape(nl * r, c) for a in (w, m, v)]
    tb = r
    while tb * gc * 4 > (1 << 20) and tb % 16 == 0:
        tb //= 2
    nb = r // tb

    def body(w_ref, g_ref, m_ref, v_ref, *rest):
        go_ref, d_ref, nm_ref, nv_ref = rest[-4:]
        gv = g_ref[...][:, :c]
        go_ref[...] = gv
        d_ref[...], nm_ref[...], nv_ref[...] = _adam_math(w_ref[...], gv, m_ref[...], v_ref[...])

    spec = pl.BlockSpec((tb, c), lambda i: (l * nb + i, 0))
    carried = list(carry) if carry is not None else []
    outs = pl.pallas_call(
        body, name=name, grid=(nb,),
        in_specs=[spec, pl.BlockSpec((tb, gc), lambda i: (i, 0)), spec, spec] + [ANY] * (len(carried) + 1),
        out_specs=[spec] * 4, out_shape=[jax.ShapeDtypeStruct((nl * r, c), F32)] * 4,
        input_output_aliases={4 + k: k for k in range(len(carried))},
        compiler_params=_params(1))(w2, g2, m2, v2, *[a.reshape(nl * r, c) for a in carried], after)
    return tuple(o.reshape(w.shape) for o in outs)


def _adamw_layer_t(name, l, w, g, m, v, carry, after):
    nl, r, c = w.shape
    g2 = g.reshape(r, -1)
    gc = g2.shape[1]
    wt, mt, vt = [jnp.swapaxes(a, 1, 2).reshape(nl * c, r) for a in (w, m, v)]

    def body(w_ref, g_ref, m_ref, v_ref, *rest):
        go_ref, d_ref, nm_ref, nv_ref = rest[-4:]
        gv = g_ref[...].T[:c]
        go_ref[...] = gv
        d_ref[...], nm_ref[...], nv_ref[...] = _adam_math(w_ref[...], gv, m_ref[...], v_ref[...])

    spec = pl.BlockSpec((c, 128), lambda j: (l, j))
    carried = list(carry) if carry is not None else []
    return tuple(pl.pallas_call(
        body, name=name, grid=(r // 128,),
        in_specs=[spec, pl.BlockSpec((128, gc), lambda j: (j, 0)), spec, spec] + [ANY] * (len(carried) + 1),
        out_specs=[spec] * 4, out_shape=[jax.ShapeDtypeStruct((nl * c, r), F32)] * 4,
        input_output_aliases={4 + k: k for k in range(len(carried))},
        compiler_params=_params(1))(wt, g2, mt, vt, *carried, after))


def _adamw(name, w, g, m, v):
    shape = w.shape
    cols = shape[-1]
    rows = w.size // cols
    w2, g2, m2, v2 = [a.reshape(rows, cols) for a in (w, g, m, v)]
    tb = rows
    while tb * cols * 4 > (1 << 20) and tb % 16 == 0:
        tb //= 2

    def body(w_ref, g_ref, m_ref, v_ref, d_ref, nm_ref, nv_ref):
        d_ref[...], nm_ref[...], nv_ref[...] = _adam_math(w_ref[...], g_ref[...], m_ref[...], v_ref[...])

    spec = pl.BlockSpec((tb, cols), lambda i: (i, 0))
    sh = jax.ShapeDtypeStruct((rows, cols), F32)
    outs = pl.pallas_call(body, name=name, grid=(rows // tb,), in_specs=[spec] * 4, out_specs=[spec] * 3,
                          out_shape=[sh] * 3, compiler_params=_params(1))(w2, g2, m2, v2)
    return [o.reshape(shape) for o in outs]


def _to_rows(bg, col0):
    t_ = bg.shape[0]
    a = bg[:, col0:col0 + 2 * DN_HEADS].reshape(t_ // CHUNK, CHUNK, 2, DN_HEADS)
    return jnp.transpose(a, (2, 0, 3, 1))


def _from_rows(db, dg):
    nch = db.shape[1]
    back = lambda a: jnp.transpose(a, (1, 3, 0, 2)).reshape(nch * CHUNK, 2 * DN_HEADS)
    return jnp.pad(jnp.concatenate([back(db), back(dg)], axis=1), ((0, 0), (0, 128 - 4 * DN_HEADS)))


def _layer_fwd(x, xm, w, rest, late, cos, sin):
    t_ = x.shape[0]
    nb = min(1024, t_)
    pm = _mm(xm, w["in_main"], name="mm_in", tn=1536)
    pbg = _mm(xm, w["in_bg"], name="mm_in_bg")
    w = {**w, **rest(pbg)}
    qkv = _prep_fwd(pm, w["conv"])
    bg, = _stage_fwd(_bg_f, "bg_fwd", [(pbg, 128, 0, False)], [w["arow"], w["dtrow"]], [(128, F32)], nb)
    brows, grows = _to_rows(bg, 0), _to_rows(bg, 2 * DN_HEADS)
    ares, tinv = _dn_a_fwd(qkv, grows, brows)
    o_f, o_b, st_f, st_b = _dn_b_fwd(ares, grows)
    w = {**w, **late(o_f)}

    def gnorm_all(of, ob, z, gnw):
        return jnp.concatenate([_gnorm_f(of[:, _hs(h)], ob[:, _hs(h)], z[:, _hs(h)], gnw)[0]
                                for h in range(DN_HEADS)], axis=-1)

    odn, ya = _mm_pre(gnorm_all, [(o_f, D, 0), (o_b, D, 0), (pm, D, O_Z // D)], [w["gnw"]], w["a"], name="mm_a_gnorm")
    osw, probs = _attn_fwd(pm, cos, sin, w["sinks"])
    same, first = (lambda j: j), (lambda j: 0)
    full = lambda dt: (D, D, dt, first, None)
    yb, merged = _mm_fused(
        osw, w["b"], lambda acc, ya_, ga, gb: (acc,) + _merge_f(ya_, acc, ga, gb), [full(F32), full(_MXU)],
        name="mm_b_merge", rows=[(ya, D, first), (pm, D, lambda j: O_GA // D), (pm, D, lambda j: O_GB // D)], tm=512)
    mix, x1, x1m = _mm_fused(
        merged, w["o"], lambda acc, x_, g_, b_: (acc,) + _ln_f2(x_, acc, g_, b_), [full(F32), full(F32), full(_MXU)],
        name="mm_o_ln", rows=[(x, D, first)], params=[w["ln1g"], w["ln1b"]], tm=512)
    gu, hid = _mm_fused(x1m, w["gu"], lambda acc: (acc,) + _swiglu_tile(acc),
                        [(2 * FFN, FFN, F32, same, None), (FFN, FFN // 2, _MXU, same, None)], name="mm_gu_swiglu", tn=FFN)
    ffn, x2, x2m = _mm_fused(
        hid, w["d"], lambda acc, x_, g_, b_: (acc,) + _ln_f2(x_, acc, g_, b_), [full(F32), full(F32), full(_MXU)],
        name="mm_d_ln", rows=[(x1, D, first)], params=[w["ln2g"], w["ln2b"]], tm=512, tk=FFN)
    res = dict(w=w, x=x, xm=xm, pm=pm, pbg=pbg, qkv=qkv, grows=grows, brows=brows, ares=ares, tinv=tinv, o_f=o_f, o_b=o_b, st_f=st_f,
               st_b=st_b, odn=odn, osw=osw, probs=probs, ya=ya, yb=yb, merged=merged, mix=mix, x1=x1, x1m=x1m, gu=gu, hid=hid,
               ffn=ffn)
    return x2, x2m, res


def _layer_bwd(dx2, r, w, cos, sin, mid=None, after=None):
    t_ = dx2.shape[0]
    nb = min(1024, t_)
    pm = r["pm"]
    g = {}
    dx1a, dffn, g["ln2g"], g["ln2b"] = _stage_bwd(
        _ln_f, "ln2_bwd", [(r["x1"], D, 0, False), (r["ffn"], D, 0, False)], [w["ln2g"], w["ln2b"]], [dx2],
        min(512, t_), dtypes=[F32, _MXU], after=after)
    same = lambda j: j
    dgu, = _mm_fused(dffn, w["d"], lambda dhid, gu: jax.vjp(_swiglu_tile, gu)[1]((dhid,)),
                     [(2 * FFN, FFN, _MXU, same, None)], name="mm_d_dx_swiglu", rows=[(r["gu"], FFN, same)], tb=True,
                     tn=FFN // 2)
    g["d"] = _mm(r["hid"], dffn, ta=True, name="mm_d_dw", tm=FFN // 2, out_dtype=_MXU)
    first = lambda j: 0
    full = lambda dt: (D, D, dt, first, None)

    def ln1_back(acc, dx1a_, x_, mix_, g_, b_):
        dx_, dmix_, dg_, db_ = jax.vjp(_ln_f, x_, mix_, g_, b_)[1]((acc + dx1a_,))
        return dx_, dmix_, dg_, db_

    dxa, dmix, g["ln1g"], g["ln1b"] = _mm_fused(
        dgu, w["gu"], ln1_back, [full(F32), full(_MXU)], name="mm_gu_dx_ln", tb=True, tm=256, tk=2 * FFN, n_sums=2,
        rows=[(dx1a, D, first), (r["x"], D, first), (r["mix"], D, first)], params=[w["ln1g"], w["ln1b"]])
    g["gu"] = _mm(r["x1m"], dgu, ta=True, name="mm_gu_dw", tn=FFN // 2, out_dtype=_MXU)
    tok = mid(g["gu"]) if mid is not None else None
    g["o"] = _mm(r["merged"], dmix, ta=True, name="mm_o_dw", after=tok, out_dtype=_MXU)

    def merge_back(acc, ya_, yb_, ga, gb):
        dya_, dyb_, dga, dgb = jax.vjp(_merge_f, ya_, yb_, ga, gb)[1]((acc,))
        return dya_, dyb_, jnp.concatenate([dga, dgb], axis=-1)

    dya, dyb, dpm = _mm_fused(
        dmix, w["o"], merge_back, [full(_MXU), full(_MXU), (N_MAIN, 2 * D, _MXU, lambda j: O_GA // (2 * D), None)],
        name="mm_o_dx_merge", tb=True, tm=512,
        rows=[(r["ya"], D, first), (r["yb"], D, first), (pm, D, lambda j: O_GA // D), (pm, D, lambda j: O_GB // D)])

    def gnorm_back(acc, of, ob, z, gnw):
        dos, dzs, dws = [], [], None
        for h in range(DN_HEADS):
            do_h, _, dz_h, dw_h = jax.vjp(_gnorm_f, of[:, _hs(h)], ob[:, _hs(h)], z[:, _hs(h)], gnw)[1]((acc[:, _hs(h)],))
            dos.append(do_h)
            dzs.append(dz_h)
            dws = dw_h if dws is None else dws + dw_h
        return jnp.concatenate(dos, axis=-1), jnp.concatenate(dzs, axis=-1), dws

    dof, dpm, g["gnw"] = _mm_fused(
        dya, w["a"], gnorm_back, [full(F32), (N_MAIN, D, _MXU, lambda j: O_Z // D, dpm)], name="mm_a_dx_gnorm", tb=True,
        tm=512, rows=[(r["o_f"], D, first), (r["o_b"], D, first), (pm, D, lambda j: O_Z // D)], params=[w["gnw"]],
        n_sums=1)
    g["a"] = _mm(r["odn"], dya, ta=True, name="mm_a_dw", out_dtype=_MXU)
    dosw = _mm(dyb, w["b"], tb=True, name="mm_b_dx")
    g["b"] = _mm(r["osw"], dyb, ta=True, name="mm_b_dw", out_dtype=_MXU)
    ab = _attn_bwd(pm, cos, sin, w["sinks"], r["probs"], dosw, dpm)
    dpm, g["sinks"] = ab[0], ab[7]
    dpm = _band_sum(ab[1:4], ab[4:7], dpm)
    dares, dg_b = _dn_b_bwd(r["ares"], r["grows"], r["st_f"], r["st_b"], dof)
    dqkv, dgrows, dbrows = _dn_a_bwd(r["qkv"], r["grows"], r["brows"], r["tinv"], dares, dg_b)
    dbg = _from_rows(dbrows, dgrows)
    dpbg, g["arow"], g["dtrow"] = _stage_bwd(_bg_f, "bg_bwd", [(r["pbg"], 128, 0, False)], [w["arow"], w["dtrow"]],
                                             [dbg], nb)
    dpm, g["conv"] = _prep_bwd(pm, w["conv"], dqkv, dpm)
    dx = _mm(dpm, w["in_main"], tb=True, add=dxa, name="mm_in_dx", tk=1920)
    dx = _mm(dpbg, w["in_bg"], tb=True, add=dx, name="mm_in_bg_dx")
    g["in_main"] = _mm(r["xm"], dpm, ta=True, name="mm_in_dw", tn=1536, out_dtype=_MXU)
    g["in_bg"] = _mm(r["xm"], dpbg, ta=True, name="mm_in_bg_dw", out_dtype=_MXU)
    return dx, g


def _place():
    return lax.axis_index("x"), lax.axis_index("y"), lax.axis_index("c")


def _colblock(kind, q):
    return q if kind == "col" else (q >> 1) | ((q & 1) << 1)


def _other_chips(x, y):
    return [(1 - x, y), (x, 1 - y), (1 - x, 1 - y)]


HBM = pl.BlockSpec(memory_space=pltpu.HBM)
SEM = pl.BlockSpec(memory_space=pltpu.SEMAPHORE)
DATAFLOW = pltpu.SideEffectType.DATAFLOW_SIDE_EFFECTING


def _hbm(a):
    return pltpu.HBM(a.shape, a.dtype)


def _gather_start(locs, kinds, name):
    n = len(locs)
    locs = list(locs)
    lands = [lax.empty((N_CHIPS,) + a.shape if kind == "row" else (a.shape[0], N_CHIPS * a.shape[1]), a.dtype)
             for a, kind in zip(locs, kinds)]

    def body(*refs):
        loc_refs, land_refs = refs[:n], refs[n:2 * n]
        send_sems, recv_sems, token = refs[2 * n:3 * n], refs[3 * n:4 * n], refs[-1]
        x, y, c = _place()
        myq = 2 * x + y
        for t in range(n):
            width = locs[t].shape[1]
            mine = (land_refs[t].at[myq] if kinds[t] == "row" else
                    land_refs[t].at[:, pl.ds(pl.multiple_of(_colblock(kinds[t], myq) * width, 128), width)])
            for dev in [(cx, cy, c) for cx, cy in _other_chips(x, y)] + [(x, y, 1 - c)]:
                pltpu.make_async_remote_copy(src_ref=loc_refs[t], dst_ref=mine, send_sem=send_sems[t],
                                             recv_sem=recv_sems[t], device_id=dev, device_id_type=MESH).start()
        token[...] = jnp.zeros_like(token)

    res = pl.pallas_call(
        body, name=name,
        out_shape=[pltpu.SemaphoreType.DMA(())] * (2 * n) + [_hbm(a) for a in locs + lands]
        + [jax.ShapeDtypeStruct((8, 128), F32)],
        in_specs=[HBM] * (2 * n), out_specs=[SEM] * (2 * n) + [HBM] * (2 * n) + [pl.BlockSpec(memory_space=pltpu.VMEM)],
        input_output_aliases={t: 2 * n + t for t in range(2 * n)},
        compiler_params=pltpu.CompilerParams(has_side_effects=DATAFLOW),
    )(*[pltpu.with_memory_space_constraint(a, pltpu.HBM) for a in locs + lands])
    return (res[:n], res[n:2 * n], res[2 * n:3 * n], res[3 * n:4 * n]), res[-1]


def _split_wait(handle, after, name):
    send_sems, recv_sems, srcs, lands = handle
    n = len(srcs)

    def body(*refs):
        land_refs, ssems, rsems = refs[n:2 * n], refs[2 * n:3 * n], refs[3 * n:4 * n]
        x, y, c = _place()
        for t in range(n):
            done = pltpu.make_async_remote_copy(
                src_ref=land_refs[t], dst_ref=land_refs[t], send_sem=ssems[t], recv_sem=rsems[t],
                device_id=(x, y, c), device_id_type=MESH)
            done.wait_send()
            done.wait_recv()

    res = pl.pallas_call(
        body, name=name, out_shape=[_hbm(a) for a in list(srcs) + list(lands)],
        in_specs=[HBM] * (2 * n) + [SEM] * (2 * n) + [ANY], out_specs=[HBM] * (2 * n),
        input_output_aliases={t: t for t in range(2 * n)},
        compiler_params=pltpu.CompilerParams(has_side_effects=DATAFLOW),
    )(*srcs, *lands, *send_sems, *recv_sems, after)
    return res[:n], res[n:]


RS_CHUNKS = 2


def _piece(ref, kind, q, hf, pr, pc):
    if kind == "row":
        return ref.at[pl.ds((2 * q + hf) * pr, pr), :]
    return ref.at[pl.ds(hf * pr, pr), pl.ds(pl.multiple_of(_colblock(kind, q) * pc, 128), pc)]


def _rs_sibling_start(ts, meta, name):
    n = len(ts)
    ts = list(ts)
    lands = [lax.empty((N_CHIPS, pr, pc), a.dtype) for a, (_, pr, pc) in zip(ts, meta)]

    def body(*refs):
        t_refs, land_refs = refs[:n], refs[n:2 * n]
        send_sems, recv_sems, token = refs[2 * n:3 * n], refs[3 * n:4 * n], refs[-1]
        x, y, c = _place()
        for t, (kind, pr, pc) in enumerate(meta):
            for q in range(N_CHIPS):
                pltpu.make_async_remote_copy(
                    src_ref=_piece(t_refs[t], kind, q, 1 - c, pr, pc), dst_ref=land_refs[t].at[q],
                    send_sem=send_sems[t], recv_sem=recv_sems[t], device_id=(x, y, 1 - c), device_id_type=MESH).start()
        token[...] = jnp.zeros_like(token)

    res = pl.pallas_call(
        body, name=name,
        out_shape=[pltpu.SemaphoreType.DMA(())] * (2 * n) + [_hbm(a) for a in ts + lands]
        + [jax.ShapeDtypeStruct((8, 128), F32)],
        in_specs=[HBM] * (2 * n), out_specs=[SEM] * (2 * n) + [HBM] * (2 * n) + [pl.BlockSpec(memory_space=pltpu.VMEM)],
        input_output_aliases={t: 2 * n + t for t in range(2 * n)},
        compiler_params=pltpu.CompilerParams(has_side_effects=DATAFLOW),
    )(*[pltpu.with_memory_space_constraint(a, pltpu.HBM) for a in ts + lands])
    return (res[:n], res[n:2 * n], res[2 * n:3 * n], res[3 * n:4 * n]), res[-1]


def _rs_add_sibling(ts, r1s, meta, c):
    n = len(ts)
    in_specs, out_specs, out_shape = [], [], []
    for kind, pr, pc in meta:
        rs = pr // RS_CHUNKS
        if kind == "row":
            in_specs.append(pl.BlockSpec((rs, pc), lambda q, r, c_ref: ((2 * q + c_ref[0]) * RS_CHUNKS + r, 0)))
        else:
            in_specs.append(pl.BlockSpec(
                (rs, pc), lambda q, r, c_ref, kind=kind: (c_ref[0] * RS_CHUNKS + r, _colblock(kind, q))))
    for kind, pr, pc in meta:
        sp = pl.BlockSpec((None, pr // RS_CHUNKS, pc), lambda q, r, c_ref: (q, r, 0))
        in_specs.append(sp)
        out_specs.append(sp)
        out_shape.append(jax.ShapeDtypeStruct((N_CHIPS, pr, pc), BF16))

    def body(c_ref, *refs):
        for t in range(n):
            refs[2 * n + t][...] = (refs[t][...].astype(F32) + refs[n + t][...].astype(F32)).astype(BF16)

    return pl.pallas_call(
        body, name="rs_add_sibling", out_shape=out_shape,
        grid_spec=pltpu.PrefetchScalarGridSpec(num_scalar_prefetch=1, grid=(N_CHIPS, RS_CHUNKS), in_specs=in_specs,
                                               out_specs=out_specs),
        compiler_params=_params(2))(c.reshape(1).astype(jnp.int32), *ts, *r1s)


def _rs_chips_start(ps, meta, name):
    n = len(ps)
    ps = list(ps)
    lands = [lax.empty((N_CHIPS - 1, pr, pc), p.dtype) for p, (_, pr, pc) in zip(ps, meta)]

    def body(*refs):
        p_refs, land_refs = refs[:n], refs[n:2 * n]
        send_sems, recv_sems, token = refs[2 * n:3 * n], refs[3 * n:4 * n], refs[-1]
        x, y, c = _place()
        for t in range(n):
            for j, (cx, cy) in enumerate(_other_chips(x, y)):
                pltpu.make_async_remote_copy(
                    src_ref=p_refs[t].at[2 * cx + cy], dst_ref=land_refs[t].at[j], send_sem=send_sems[t],
                    recv_sem=recv_sems[t], device_id=(cx, cy, c), device_id_type=MESH).start()
        token[...] = jnp.zeros_like(token)

    res = pl.pallas_call(
        body, name=name,
        out_shape=[pltpu.SemaphoreType.DMA(())] * (2 * n) + [_hbm(a) for a in ps + lands]
        + [jax.ShapeDtypeStruct((8, 128), F32)],
        in_specs=[HBM] * (2 * n), out_specs=[SEM] * (2 * n) + [HBM] * (2 * n) + [pl.BlockSpec(memory_space=pltpu.VMEM)],
        input_output_aliases={t: 2 * n + t for t in range(2 * n)},
        compiler_params=pltpu.CompilerParams(has_side_effects=DATAFLOW),
    )(*[pltpu.with_memory_space_constraint(a, pltpu.HBM) for a in ps + lands])
    return (res[:n], res[n:2 * n], res[2 * n:3 * n], res[3 * n:4 * n]), res[-1]


def _rs_add_chips(ps, r2s, meta, myq, c):
    n = len(ps)
    in_specs, out_specs, out_shape = [], [], []
    for _, pr, pc in meta:
        in_specs.append(pl.BlockSpec((None, pr // RS_CHUNKS, pc), lambda r, q_ref, c_ref: (q_ref[0], r, 0)))
    for _, pr, pc in meta:
        in_specs.append(pl.BlockSpec((N_CHIPS - 1, pr // RS_CHUNKS, pc), lambda r, q_ref, c_ref: (0, r, 0)))
        out_specs.append(pl.BlockSpec((None, pr // RS_CHUNKS, pc), lambda r, q_ref, c_ref: (c_ref[0], r, 0)))
        out_shape.append(jax.ShapeDtypeStruct((2, pr, pc), F32))

    def body(q_ref, c_ref, *refs):
        for t in range(n):
            r2 = refs[n + t]
            own = refs[t][...].astype(F32)
            refs[2 * n + t][...] = ((own + r2[0].astype(F32)) + r2[1].astype(F32)) + r2[2].astype(F32)

    return pl.pallas_call(
        body, name="rs_add_chips", out_shape=out_shape,
        grid_spec=pltpu.PrefetchScalarGridSpec(num_scalar_prefetch=2, grid=(RS_CHUNKS,), in_specs=in_specs,
                                               out_specs=out_specs),
        compiler_params=_params(1))(myq.reshape(1).astype(jnp.int32), c.reshape(1).astype(jnp.int32), *ps, *r2s)


def _rs_share_halves(gs):
    n = len(gs)

    def body(*refs):
        g_refs, send_sems, recv_sems = refs[n:2 * n], refs[2 * n], refs[2 * n + 1]
        x, y, c = _place()
        sib = (x, y, 1 - c)
        for t in range(n):
            pltpu.make_async_remote_copy(
                src_ref=g_refs[t].at[c], dst_ref=g_refs[t].at[c], send_sem=send_sems.at[t], recv_sem=recv_sems.at[t],
                device_id=sib, device_id_type=MESH).start()
        for t in range(n):
            cp = pltpu.make_async_remote_copy(
                src_ref=g_refs[t].at[c], dst_ref=g_refs[t].at[1 - c], send_sem=send_sems.at[t],
                recv_sem=recv_sems.at[t], device_id=sib, device_id_type=MESH)
            cp.wait_send()
            cp.wait_recv()

    return pl.pallas_call(
        body, name="rs_share_halves", in_specs=[ANY] * n, out_specs=[ANY] * n,
        out_shape=[jax.ShapeDtypeStruct(g.shape, g.dtype) for g in gs], input_output_aliases={t: t for t in range(n)},
        scratch_shapes=[pltpu.SemaphoreType.DMA((n,)), pltpu.SemaphoreType.DMA((n,))],
    )(*gs)


def _rs_middle(handle, after, meta, c, name):
    ts, r1s = _split_wait(handle, after, name + "_sib_wait")
    ps = _rs_add_sibling(ts, r1s, meta, c)
    return _rs_chips_start(ps, meta, name + "_start")


def _rs_end(handle, after, meta, c, myq, name):
    ps, r2s = _split_wait(handle, after, name + "_wait")
    return _rs_share_halves(_rs_add_chips(ps, r2s, meta, myq, c))


def _allreduce_small(buf):
    rows = buf.shape[0]
    ndev = 8

    def body(b_ref, o_ref, slots, send_sems, recv_sems):
        x, y, c = _place()
        me = 4 * x + 2 * y + c
        slots[me] = b_ref[...]
        for k in range(1, ndev):
            kx, ky, kc = (k >> 2) & 1, (k >> 1) & 1, k & 1
            peer = (x ^ kx, y ^ ky, c ^ kc)
            pltpu.make_async_remote_copy(
                src_ref=b_ref, dst_ref=slots.at[me], send_sem=send_sems.at[k - 1], recv_sem=recv_sems.at[k - 1],
                device_id=peer, device_id_type=MESH).start()
        for k in range(1, ndev):
            kx, ky, kc = (k >> 2) & 1, (k >> 1) & 1, k & 1
            cp = pltpu.make_async_remote_copy(
                src_ref=b_ref, dst_ref=slots.at[me ^ k], send_sem=send_sems.at[k - 1], recv_sem=recv_sems.at[k - 1],
                device_id=(x ^ kx, y ^ ky, c ^ kc), device_id_type=MESH)
            cp.wait_send()
            cp.wait_recv()
        acc = slots[0]
        for s in range(1, ndev):
            acc = acc + slots[s]
        o_ref[...] = acc

    vm = pl.BlockSpec(memory_space=pltpu.VMEM)
    return pl.pallas_call(
        body, name="allreduce_small", in_specs=[vm], out_specs=vm, out_shape=jax.ShapeDtypeStruct((rows, 128), F32),
        scratch_shapes=[pltpu.VMEM((ndev, rows, 128), F32), pltpu.SemaphoreType.DMA((ndev - 1,)),
                        pltpu.SemaphoreType.DMA((ndev - 1,))],
        compiler_params=pltpu.CompilerParams(vmem_limit_bytes=VMEM_LIMIT))(buf)


RS_META = [("col", D // 2, IN_PAD), ("row", D // 8, D), ("row", D // 8, D), ("row", D // 8, D),
           ("colx", D // 2, 2 * FFN // N_CHIPS), ("row", FFN // 8, D)]
SMALL_ROWS = 156


def _rope_tables(t_):
    half = SW_DIM // 2
    inv_freq = ROPE_THETA ** (-jnp.arange(half, dtype=F32) / half)
    ang = jnp.arange(t_, dtype=F32)[:, None] * inv_freq[None, :]
    reps = 128 // half
    return jnp.concatenate([jnp.cos(ang)] * reps, axis=1), jnp.concatenate([jnp.sin(ang)] * reps, axis=1)


def _orig_cols(padded, a, b):
    out = []
    for q in range(N_CHIPS):
        lo, hi = max(a, q * IN_SHARD), min(b, (q + 1) * IN_SHARD)
        if lo < hi:
            out.append(padded[:, q * IN_PAD + lo - q * IN_SHARD:q * IN_PAD + hi - q * IN_SHARD])
    return out


_ORIG_SEGMENTS = [(0, R_BG, "main", 0), (R_BG, R_SW, "bg", 0), (R_SW, R_GATES, "main", O_QS), (R_GATES, IN_COLS, "main", O_GA)]


def _to_padded_shards(main, bg):
    zeros = jnp.zeros((main.shape[0], IN_PAD - IN_SHARD), main.dtype)
    parts = []
    for q in range(N_CHIPS):
        for a, b, src, s0 in _ORIG_SEGMENTS:
            lo, hi = max(a, q * IN_SHARD), min(b, (q + 1) * IN_SHARD)
            if lo < hi:
                parts.append((main if src == "main" else bg)[:, s0 + lo - a:s0 + hi - a])
        parts.append(zeros)
    return jnp.concatenate(parts, axis=1)


def _lane_row(v16):
    return jnp.pad(v16.reshape(1, 2 * DN_HEADS), ((0, 0), (2 * DN_HEADS, 128 - 4 * DN_HEADS)))


def _pack_small(g):
    pad16 = jnp.pad(g["sinks"], ((0, 0), (0, 128 - SW_HEADS)))
    return jnp.concatenate([g["conv"].reshape(-1, 128), g["ln1g"].reshape(-1, 128), g["ln1b"].reshape(-1, 128),
                            g["ln2g"].reshape(-1, 128), g["ln2b"].reshape(-1, 128), g["gnw"], g["arow"], g["dtrow"],
                            pad16], axis=0)


def _unpack_small(buf):
    nconv = DN_CONV * 3 * D // 128
    o = nconv
    out = dict(conv=buf[:o].reshape(DN_CONV, 3 * D))
    for name in ("ln1g", "ln1b", "ln2g", "ln2b"):
        out[name] = buf[o:o + 8].reshape(D)
        o += 8
    out["gnw"] = buf[o]
    out["a_log"] = buf[o + 1, 2 * DN_HEADS:4 * DN_HEADS].reshape(2, DN_HEADS)
    out["dt_bias"] = buf[o + 2, 2 * DN_HEADS:4 * DN_HEADS].reshape(2, DN_HEADS)
    out["sinks"] = buf[o + 3, :SW_HEADS]
    return out


def kernel(x, w_in, conv_w, a_log, dt_bias, dn_norm_w, sinks, w_branch_a, w_branch_b, w_out, ln1_g, ln1_b, w_gate_up, w_down, ln2_g, ln2_b, loss_target, m_w_in, m_conv_w, m_a_log, m_dt_bias, m_dn_norm_w, m_sinks, m_w_branch_a, m_w_branch_b, m_w_out, m_ln1_g, m_ln1_b, m_w_gate_up, m_w_down, m_ln2_g, m_ln2_b, v_w_in, v_conv_w, v_a_log, v_dt_bias, v_dn_norm_w, v_sinks, v_w_branch_a, v_w_branch_b, v_w_out, v_ln1_g, v_ln1_b, v_w_gate_up, v_w_down, v_ln2_g, v_ln2_b):
    xi, yi, ci = _place()
    myq = 2 * xi + yi
    t_ = x.shape[1]
    cos, sin = _rope_tables(t_)

    kinds = ["col", "col", "row", "row", "row", "colx", "row"]
    gathers = []
    for l in range(DEPTH):
        srcs = [jnp.pad(w_in[l].astype(BF16), ((0, 0), (0, IN_PAD - IN_SHARD))), conv_w[l]]
        srcs += [a[l].astype(BF16) for a in (w_branch_a, w_branch_b, w_out, w_gate_up, w_down)]
        if gathers:
            srcs[1] = srcs[1] + gathers[-1][1][0, 0]
        gathers.append(_gather_start(srcs, kinds, "gather_%d_start" % l))

    def in_weights(l, after):
        _, (full_in,) = _split_wait(tuple(part[:1] for part in gathers[l][0]), after, "gather_%d_wait_in" % l)
        cols = lambda a, b: _orig_cols(full_in, a, b)
        return dict(
            in_main=jnp.concatenate(cols(0, R_BG) + cols(R_GATES, IN_COLS) + cols(R_SW, R_GATES), axis=1),
            in_bg=jnp.pad(jnp.concatenate(cols(R_BG, R_SW), axis=1), ((0, 0), (0, 128 - 4 * DN_HEADS))))

    def rest_weights(l, after):
        _, (full_conv,) = _split_wait(tuple(part[1:2] for part in gathers[l][0]), after, "gather_%d_wait_conv" % l)
        return dict(
            conv=full_conv, arow=_lane_row(a_log[l]), dtrow=_lane_row(dt_bias[l]), gnw=dn_norm_w[l][None],
            sinks=sinks[l][None], ln1g=ln1_g[l][None], ln1b=ln1_b[l][None], ln2g=ln2_g[l][None], ln2b=ln2_b[l][None])

    def late_weights(l, after):
        _, (full_a, full_b, full_o, full_gu, full_d) = _split_wait(
            tuple(part[2:] for part in gathers[l][0]), after, "gather_%d_wait_rest" % l)
        return dict(a=full_a.reshape(D, D), b=full_b.reshape(D, D), o=full_o.reshape(D, D), gu=full_gu,
                    d=full_d.reshape(FFN, D))

    h = x[0]
    hm = h.astype(_MXU)
    residuals = []
    for l in range(DEPTH):
        win = in_weights(l, gathers[-1][1] if l == 0 else h)
        h, hm, res = _layer_fwd(h, hm, win, functools.partial(rest_weights, l), functools.partial(late_weights, l),
                                cos, sin)
        residuals.append(res)
    dh, sq = _loss_head(h, loss_target[0])
    loss_rows = jnp.pad(((0.5 / D) * jnp.sum(sq)).reshape(1, 1), ((0, 7), (0, 127)))

    big = [None] * DEPTH
    small = [None] * DEPTH
    hop1 = hop2 = None

    def mid(after):
        nonlocal hop1, hop2
        if hop1 is None:
            return None
        handle, tok = _rs_middle(hop1[1], after, RS_META, ci, "rs_chips_%d" % hop1[0])
        hop1, hop2 = None, (hop1[0], handle)
        return tok

    token = None
    for l in reversed(range(DEPTH)):
        dh, g = _layer_bwd(dh, residuals[l], residuals[l]["w"], cos, sin, mid, token)
        if hop2 is not None:
            big[hop2[0]] = _rs_end(hop2[1], dh, RS_META, ci, myq, "rs_chips_%d" % hop2[0])
        g_in = _to_padded_shards(g["in_main"], g["in_bg"])
        handle, token = _rs_sibling_start([g_in, g["a"], g["b"], g["o"], g["gu"], g["d"]], RS_META, "rs_sib_%d_start" % l)
        hop1 = (l, handle)
        small[l] = _pack_small(g)
    tot = _allreduce_small(jnp.concatenate(small + [loss_rows], axis=0))
    loss = tot[DEPTH * SMALL_ROWS, 0]
    token = mid(tot)
    pending = hop2
    sm = [_unpack_small(tot[l * SMALL_ROWS:(l + 1) * SMALL_ROWS]) for l in range(DEPTH)]
    stack = lambda name: jnp.stack([s[name] for s in sm], axis=0)
    grads = dict(
        conv_w=lax.dynamic_slice_in_dim(stack("conv"), myq * (3 * D // N_CHIPS), 3 * D // N_CHIPS, axis=2),
        a_log=stack("a_log"), dt_bias=stack("dt_bias"), dn_norm_w=stack("gnw"), sinks=stack("sinks"),
        ln1_g=stack("ln1g"), ln1_b=stack("ln1b"), ln2_g=stack("ln2g"), ln2_b=stack("ln2b"))
    weights = dict(w_in=w_in, conv_w=conv_w, a_log=a_log, dt_bias=dt_bias, dn_norm_w=dn_norm_w, sinks=sinks,
                   w_branch_a=w_branch_a, w_branch_b=w_branch_b, w_out=w_out, ln1_g=ln1_g, ln1_b=ln1_b,
                   w_gate_up=w_gate_up, w_down=w_down, ln2_g=ln2_g, ln2_b=ln2_b)
    ms = dict(w_in=m_w_in, conv_w=m_conv_w, a_log=m_a_log, dt_bias=m_dt_bias, dn_norm_w=m_dn_norm_w, sinks=m_sinks,
              w_branch_a=m_w_branch_a, w_branch_b=m_w_branch_b, w_out=m_w_out, ln1_g=m_ln1_g, ln1_b=m_ln1_b,
              w_gate_up=m_w_gate_up, w_down=m_w_down, ln2_g=m_ln2_g, ln2_b=m_ln2_b)
    vs = dict(w_in=v_w_in, conv_w=v_conv_w, a_log=v_a_log, dt_bias=v_dt_bias, dn_norm_w=v_dn_norm_w, sinks=v_sinks,
              w_branch_a=v_w_branch_a, w_branch_b=v_w_branch_b, w_out=v_w_out, ln1_g=v_ln1_g, ln1_b=v_ln1_b,
              w_gate_up=v_w_gate_up, w_down=v_w_down, ln2_g=v_ln2_g, ln2_b=v_ln2_b)
    names = list(weights)
    upd = {n: _adamw("adamw_" + n, weights[n], grads[n], ms[n], vs[n]) for n in grads}
    big_names = ["w_in", "w_branch_a", "w_branch_b", "w_out", "w_gate_up", "w_down"]
    carry = {n: None for n in big_names}

    def update_layer(l):
        for t, n in enumerate(big_names):
            step = _adamw_layer_t if weights[n].shape[-1] % 128 else _adamw_layer
            carry[n] = step("adamw_" + n, l, weights[n], big[l][t], ms[n], vs[n], carry[n], token)

    for l in range(DEPTH - 1, pending[0], -1):
        update_layer(l)
    big[pending[0]] = _rs_end(pending[1], carry[big_names[-1]][1], RS_META, ci, myq, "rs_chips_%d" % pending[0])
    update_layer(pending[0])
    for n in big_names:
        if weights[n].shape[-1] % 128:
            nl, r, c = weights[n].shape
            carry[n] = tuple(jnp.swapaxes(a.reshape(nl, c, r), 1, 2) for a in carry[n])
        grads[n], upd[n] = carry[n][0], carry[n][1:]
    return (loss, dh[None], *[grads[n] for n in names], *[upd[n][0] for n in names], *[upd[n][1] for n in names],
            *[upd[n][2] for n in names])
```

```python
import functools

import jax
import jax.numpy as jnp
from jax import lax
from jax.experimental import pallas as pl
from jax.experimental.pallas import tpu as pltpu

F32 = jnp.float32
BF16 = jnp.bfloat16
_MXU = BF16

D = 1024
DEPTH = 4
DN_HEADS = 8
DN_DIM = 128
DN_CONV = 5
CHUNK = 64
SW_HEADS = 16
SW_KV = 4
SW_DIM = 64
SW_GRP = SW_HEADS // SW_KV
SW_BLOCK = 128
ROPE_THETA = 10000.0
FFN = 2816
ALPHA = (2.0 * DEPTH) ** 0.25
LN_EPS = 1e-5
RMS_EPS = 1e-6
IN_COLS = 7712
O_Z, O_GA, O_GB, O_QS, O_KS, O_VS, N_MAIN = 3072, 4096, 5120, 6144, 7168, 7424, 7680
R_BG, R_SW, R_GATES = 4096, 4128, 5664
N_CHIPS = 4
IN_SHARD = IN_COLS // N_CHIPS
IN_PAD = 2048
ADAM_LR, ADAM_B1, ADAM_B2, ADAM_EPS, ADAM_WD, ADAM_STEP = 0.001, 0.9, 0.999, 1e-08, 0.01, 10
VMEM_LIMIT = 52 * 1024 * 1024
MESH = pl.DeviceIdType.MESH
ANY = pl.BlockSpec(memory_space=pl.ANY)


def _params(n_grid, **kw):
    return pltpu.CompilerParams(dimension_semantics=("arbitrary",) * n_grid, vmem_limit_bytes=VMEM_LIMIT, **kw)


def _full(a):
    nd = a.ndim
    return pl.BlockSpec(a.shape, lambda *_, nd=nd: (0,) * nd)


def _raw_dot(a, b, ca, cb):
    return lax.dot_general(a.astype(_MXU), b.astype(_MXU), (((ca,), (cb,)), ((), ())), preferred_element_type=F32)


@jax.custom_vjp
def _nn(a, b):
    return _raw_dot(a, b, 1, 0)


@jax.custom_vjp
def _nt(a, b):
    return _raw_dot(a, b, 1, 1)


@jax.custom_vjp
def _tn(a, b):
    if a.shape[1] > b.shape[1]:
        return _raw_dot(b, a, 0, 0).T
    return _raw_dot(a, b, 0, 0)


_nn.defvjp(lambda a, b: (_nn(a, b), (a, b)), lambda r, g: (_nt(g, r[1]), _tn(r[0], g)))
_nt.defvjp(lambda a, b: (_nt(a, b), (a, b)), lambda r, g: (_nn(g, r[1]), _tn(g, r[0])))
_tn.defvjp(lambda a, b: (_tn(a, b), (a, b)), lambda r, g: (_nt(r[1], g), _nn(r[0], g)))


def _hdot(a, b, ca=1, cb=0):
    ah, bh = a.astype(BF16), b.astype(BF16)
    al, bl = (a - ah.astype(F32)).astype(BF16), (b - bh.astype(F32)).astype(BF16)
    dot = lambda u, v: lax.dot_general(u, v, (((ca,), (cb,)), ((), ())), preferred_element_type=F32)
    return dot(ah, bh) + (dot(ah, bl) + dot(al, bh))


def _inv_impl(mats):
    n = mats[0].shape[0]
    eye = (lax.broadcasted_iota(jnp.int32, (n, n), 0) == lax.broadcasted_iota(jnp.int32, (n, n), 1)).astype(F32)
    ps = [-a for a in mats]
    ts = [eye + p for p in ps]
    for _ in range(max(1, (n - 1).bit_length()) - 1):
        ps = [_hdot(p, p) for p in ps]
        ts = [t + _hdot(t, p) for t, p in zip(ts, ps)]
    return tuple(ts)


@jax.custom_vjp
def _inv(mats):
    return _inv_impl(mats)


def _inv_fwd(mats):
    ts = _inv_impl(mats)
    return ts, ts


def _inv_bwd(ts, gs):
    xs = [_hdot(t, g, 0, 0) for t, g in zip(ts, gs)]
    return (tuple(-_hdot(x, t, 1, 1) for x, t in zip(xs, ts)),)


_inv.defvjp(_inv_fwd, _inv_bwd)


@jax.custom_vjp
def _inv_saved(mats, saved):
    return saved


_inv_saved.defvjp(lambda mats, saved: (saved, saved),
                  lambda ts, gs: (_inv_bwd(ts, gs)[0], tuple(jnp.zeros_like(t) for t in ts)))


def _tile(n, cap):
    if n <= cap:
        return n
    best = [t for t in range(128, cap + 1, 128) if n % t == 0]
    assert best, (n, cap)
    return best[-1]


def _mm(a, b, *, name, ta=False, tb=False, add=None, tm=1024, tn=1024, tk=1024, after=None, out_dtype=F32):
    if ta:
        k_, m_ = a.shape
    else:
        m_, k_ = a.shape
    n_ = b.shape[0] if tb else b.shape[1]
    tm, tn, tk = _tile(m_, tm), _tile(n_, tn), _tile(k_, tk)
    nk = k_ // tk
    has_add = add is not None

    def body(*refs):
        a_ref, b_ref = refs[:2]
        add_ref = refs[2] if has_add else None
        o_ref = refs[2 + has_add + (after is not None)]
        part = _raw_dot(a_ref[...], b_ref[...], 0 if ta else 1, 1 if tb else 0)
        if nk == 1:
            o_ref[...] = (part + add_ref[...] if has_add else part).astype(o_ref.dtype)
            return
        acc = refs[-1]
        k = pl.program_id(2)

        @pl.when(k == 0)
        def _():
            acc[...] = part

        @pl.when(jnp.logical_and(k > 0, k < nk - 1))
        def _():
            acc[...] += part

        @pl.when(k == nk - 1)
        def _():
            o_ref[...] = (acc[...] + part + add_ref[...] if has_add else acc[...] + part).astype(o_ref.dtype)

    a_spec = pl.BlockSpec((tk, tm), lambda i, j, k: (k, i)) if ta else pl.BlockSpec((tm, tk), lambda i, j, k: (i, k))
    b_spec = pl.BlockSpec((tn, tk), lambda i, j, k: (j, k)) if tb else pl.BlockSpec((tk, tn), lambda i, j, k: (k, j))
    o_spec = pl.BlockSpec((tm, tn), lambda i, j, k: (i, j))
    in_specs = [a_spec, b_spec] + ([o_spec] if has_add else []) + ([ANY] if after is not None else [])
    args = (a, b) + ((add,) if has_add else ()) + ((after,) if after is not None else ())
    return pl.pallas_call(
        body, name=name, grid=(m_ // tm, n_ // tn, nk), in_specs=in_specs, out_specs=o_spec,
        out_shape=jax.ShapeDtypeStruct((m_, n_), out_dtype),
        scratch_shapes=[pltpu.VMEM((tm, tn), F32)] if nk > 1 else [],
        compiler_params=_params(3))(*args)


def _mm_fused(a, b, post, outs, *, name, rows=(), params=(), n_sums=0, tb=False, tm=1024, tn=1024, tk=1024):
    m_, k_ = a.shape
    n_ = b.shape[0] if tb else b.shape[1]
    tm, tn, tk = _tile(m_, tm), _tile(n_, tn), _tile(k_, tk)
    nk = k_ // tk
    nr, npar, no = len(rows), len(params), len(outs)
    aliased = [o[4] for o in outs if o[4] is not None]
    nin = 2 + nr + npar

    def body(*refs):
        a_ref, b_ref = refs[:2]
        row_refs, par_refs = refs[2:2 + nr], refs[2 + nr:nin]
        out_refs = refs[nin + len(aliased):nin + len(aliased) + no + n_sums]
        part = _raw_dot(a_ref[...], b_ref[...], 1, 1 if tb else 0)
        k = pl.program_id(2)
        first = jnp.logical_and(pl.program_id(0) == 0, pl.program_id(1) == 0)
        if nk > 1:
            acc = refs[-1]

            @pl.when(k == 0)
            def _():
                acc[...] = part

            @pl.when(jnp.logical_and(k > 0, k < nk - 1))
            def _():
                acc[...] += part

        @pl.when(k == nk - 1)
        def _():
            total = acc[...] + part if nk > 1 else part
            res = post(total, *[r[...].astype(F32) for r in row_refs], *[p[...] for p in par_refs])
            for o_ref, val in zip(out_refs[:no], res[:no]):
                o_ref[...] = val.astype(o_ref.dtype)
            for s_ref, val in zip(out_refs[no:], res[no:]):
                @pl.when(first)
                def _(s_ref=s_ref):
                    s_ref[...] = jnp.zeros_like(s_ref)
                s_ref[...] += val

    b_spec = pl.BlockSpec((tn, tk), lambda i, j, k: (j, k)) if tb else pl.BlockSpec((tk, tn), lambda i, j, k: (k, j))
    in_specs = [pl.BlockSpec((tm, tk), lambda i, j, k: (i, k)), b_spec]
    in_specs += [pl.BlockSpec((tm, w), lambda i, j, k, cb=cb: (i, cb(j))) for _, w, cb in rows]
    in_specs += [_full(p) for p in params] + [ANY] * len(aliased)
    out_specs = [pl.BlockSpec((tm, w), lambda i, j, k, cb=cb: (i, cb(j))) for _, w, _, cb, _ in outs]
    out_specs += [_full(p) for p in params[:n_sums]]
    out_shape = [jax.ShapeDtypeStruct((m_, tot), dt) for tot, _, dt, _, _ in outs]
    out_shape += [jax.ShapeDtypeStruct(p.shape, F32) for p in params[:n_sums]]
    aliases, pos = {}, nin
    for oi, o in enumerate(outs):
        if o[4] is not None:
            aliases[pos] = oi
            pos += 1
    return pl.pallas_call(
        body, name=name, grid=(m_ // tm, n_ // tn, nk), in_specs=in_specs, out_specs=out_specs, out_shape=out_shape,
        scratch_shapes=[pltpu.VMEM((tm, tn), F32)] if nk > 1 else [], input_output_aliases=aliases,
        compiler_params=_params(3))(a, b, *[r[0] for r in rows], *params, *aliased)


def _mm_pre(pre, rows, params, b, *, name, tm=512):
    m_ = rows[0][0].shape[0]
    k_, n_ = b.shape
    tm = _tile(m_, tm)
    nr, npar = len(rows), len(params)

    def body(*refs):
        b_ref, a_out, o_ref = refs[nr + npar:]
        a = pre(*[r[...].astype(F32) for r in refs[:nr]], *[p[...] for p in refs[nr:nr + npar]]).astype(a_out.dtype)
        a_out[...] = a
        o_ref[...] = _raw_dot(a, b_ref[...], 1, 0)

    return pl.pallas_call(
        body, name=name, grid=(m_ // tm,),
        in_specs=[pl.BlockSpec((tm, w), lambda i, cb=cb: (i, cb)) for _, w, cb in rows] + [_full(p) for p in params]
        + [_full(b)],
        out_specs=[pl.BlockSpec((tm, k_), lambda i: (i, 0)), pl.BlockSpec((tm, n_), lambda i: (i, 0))],
        out_shape=[jax.ShapeDtypeStruct((m_, k_), _MXU), jax.ShapeDtypeStruct((m_, n_), F32)],
        compiler_params=_params(1))(*[r[0] for r in rows], *params, b)


def _row_spec(tb, w, c0, percol):
    return pl.BlockSpec((tb, w), lambda i, j, c0=c0, pc=percol: (i, c0 + (j if pc else 0)))


def _stage_fwd(f, name, rows, params, outs, tb, ncol=1):
    t_ = rows[0][0].shape[0]
    nr, npar = len(rows), len(params)

    def body(*refs):
        res = f(*[r[...].astype(F32) for r in refs[:nr + npar]])
        for o_ref, val in zip(refs[nr + npar:], res):
            o_ref[...] = val.astype(o_ref.dtype)

    return pl.pallas_call(
        body, name=name, grid=(t_ // tb, ncol),
        in_specs=[_row_spec(tb, w, c0, pc) for (_, w, c0, pc) in rows] + [_full(p) for p in params],
        out_specs=[pl.BlockSpec((tb, w), lambda i, j: (i, j)) for w, _ in outs],
        out_shape=[jax.ShapeDtypeStruct((t_, w * ncol), dt) for w, dt in outs],
        compiler_params=_params(2))(*[r[0] for r in rows], *params)


def _into(dest, tb, width, t_, ncol, dtype):
    if dest is None:
        return pl.BlockSpec((tb, width), lambda i, j: (i, j)), jax.ShapeDtypeStruct((t_, width * ncol), dtype), None
    buf, total, c0 = dest
    return (pl.BlockSpec((tb, width), lambda i, j, c0=c0: (i, c0 + j)), jax.ShapeDtypeStruct((t_, total), dtype), buf)


def _stage_bwd(f, name, rows, params, douts, tb, ncol=1, cat=None, dtypes=None, dest=None, after=None):
    t_ = rows[0][0].shape[0]
    nr, npar, nd = len(rows), len(params), len(douts)
    cat = cat if cat is not None else [[r] for r in range(nr)]
    dtypes = dtypes if dtypes is not None else [F32] * len(cat)
    dest = dest or {}
    assert ncol == 1 or all(len(g) == 1 and rows[g[0]][3] for g in cat)
    nin = nr + npar + nd
    unread = [after] if after is not None else []

    def body(*refs):
        ins = [r[...].astype(F32) for r in refs[:nr + npar]]
        dvals = tuple(r[...].astype(F32) for r in refs[nr + npar:nin])
        out_refs = refs[nin + len(aliased) + len(unread):]
        _, vjp = jax.vjp(f, *ins)
        grads = vjp(dvals)
        for o_ref, grp in zip(out_refs[:len(cat)], cat):
            val = grads[grp[0]] if len(grp) == 1 else jnp.concatenate([grads[r] for r in grp], axis=-1)
            o_ref[...] = val.astype(o_ref.dtype)
        first = jnp.logical_and(pl.program_id(0) == 0, pl.program_id(1) == 0)
        for p_ref, gp in zip(out_refs[len(cat):], grads[nr:]):
            @pl.when(first)
            def _(p_ref=p_ref):
                p_ref[...] = jnp.zeros_like(p_ref)
            p_ref[...] += gp

    gw = [sum(rows[r][1] for r in grp) for grp in cat]
    out_specs, out_shape, aliased, aliases = [], [], [], {}
    for gi, (w, dt) in enumerate(zip(gw, dtypes)):
        spec, shape, buf = _into(dest.get(gi), tb, w, t_, ncol, dt)
        out_specs.append(spec)
        out_shape.append(shape)
        if buf is not None:
            aliases[nin + len(aliased)] = gi
            aliased.append(buf)
    return pl.pallas_call(
        body, name=name, grid=(t_ // tb, ncol),
        in_specs=[_row_spec(tb, w, c0, pc) for (_, w, c0, pc) in rows] + [_full(p) for p in params]
        + [pl.BlockSpec((tb, d.shape[1] // ncol), lambda i, j: (i, j)) for d in douts]
        + [ANY] * (len(aliased) + len(unread)),
        out_specs=out_specs + [_full(p) for p in params],
        out_shape=out_shape + [jax.ShapeDtypeStruct(p.shape, F32) for p in params],
        input_output_aliases=aliases,
        compiler_params=_params(2))(*[r[0] for r in rows], *params, *douts, *aliased, *unread)


def _ln_f(x, y, g, b):
    u = ALPHA * x + y
    c = u - jnp.mean(u, axis=-1, keepdims=True)
    var = jnp.mean(c * c, axis=-1, keepdims=True)
    return (c * lax.rsqrt(var + LN_EPS) * g + b,)


def _ln_f2(x, y, g, b):
    out, = _ln_f(x, y, g, b)
    return out, out


def _swiglu_tile(gu):
    half = gu.shape[1] // 2
    return (jax.nn.silu(gu[:, :half]) * gu[:, half:],)


def _merge_f(ya, yb, ga, gb):
    return (jax.nn.sigmoid(ga) * ya + jax.nn.sigmoid(gb) * yb,)


def _gnorm_f(of, ob, z, w):
    o = of + ob
    return (o * lax.rsqrt(jnp.mean(o * o, axis=-1, keepdims=True) + RMS_EPS) * w * jax.nn.silu(z),)


def _bg_f(x, arow, dtrow):
    lane = lax.broadcasted_iota(jnp.int32, x.shape, 1)
    beta = jax.nn.sigmoid(x)
    g = -jnp.exp(arow) * jax.nn.softplus(x + dtrow)
    return (jnp.where(lane < 16, beta, jnp.where(lane < 32, g, 0.0)),)


PREP_ROWS = 512
PAD = 8


def _prep_f(part, w, *wins):
    xc = wins[0] * w[0:1, :]
    for k in range(1, DN_CONV):
        xc = xc + wins[k] * w[k:k + 1, :]
    a = jax.nn.silu(xc)
    nrm = a * lax.rsqrt(jnp.sum(a * a, axis=-1, keepdims=True) + RMS_EPS)
    return jnp.where(part == 0, nrm * (DN_DIM ** -0.5), jnp.where(part == 1, nrm, a))


def _windows(pad_ref, r0, rows):
    return [pad_ref[PAD + r0 - 2 + k:PAD + r0 - 2 + k + rows, :] for k in range(DN_CONV)]


def _prep_fwd(pm, conv):
    t_ = pm.shape[0]
    rows = min(PREP_ROWS, t_)

    def body(x_ref, w_ref, o_ref, pad_ref):
        part = pl.program_id(0) // DN_HEADS
        pad_ref[0:PAD, :] = jnp.zeros((PAD, DN_DIM), F32)
        pad_ref[PAD + t_:2 * PAD + t_, :] = jnp.zeros((PAD, DN_DIM), F32)
        pad_ref[PAD:PAD + t_, :] = x_ref[...]
        w = w_ref[...]
        for r in range(t_ // rows):
            o_ref[r * rows:(r + 1) * rows, :] = _prep_f(part, w, *_windows(pad_ref, r * rows, rows))

    ncb = 3 * DN_HEADS
    return pl.pallas_call(
        body, name="prep_fwd", grid=(ncb,),
        in_specs=[pl.BlockSpec((t_, DN_DIM), lambda j: (0, j)), pl.BlockSpec((DN_CONV, DN_DIM), lambda j: (0, j))],
        out_specs=pl.BlockSpec((t_, DN_DIM), lambda j: (0, j)),
        out_shape=jax.ShapeDtypeStruct((t_, ncb * DN_DIM), F32),
        scratch_shapes=[pltpu.VMEM((t_ + 2 * PAD, DN_DIM), F32)],
        compiler_params=_params(1))(pm, conv)


def _prep_bwd(pm, conv, dout, dpm):
    t_ = pm.shape[0]
    rows = min(PREP_ROWS, t_)

    def body(x_ref, w_ref, d_ref, _, dx_ref, dw_ref, pad_ref, dpad_ref):
        part = pl.program_id(0) // DN_HEADS
        pad_ref[0:PAD, :] = jnp.zeros((PAD, DN_DIM), F32)
        pad_ref[PAD + t_:2 * PAD + t_, :] = jnp.zeros((PAD, DN_DIM), F32)
        pad_ref[PAD:PAD + t_, :] = x_ref[...]
        dpad_ref[...] = jnp.zeros_like(dpad_ref)
        w = w_ref[...]
        dw = jnp.zeros((DN_CONV, DN_DIM), F32)
        for r in range(t_ // rows):
            r0 = r * rows
            _, vjp = jax.vjp(functools.partial(_prep_f, part), w, *_windows(pad_ref, r0, rows))
            grads = vjp(d_ref[r0:r0 + rows, :])
            dw = dw + grads[0]
            for k in range(DN_CONV):
                lo = PAD + r0 - 2 + k
                dpad_ref[lo:lo + rows, :] += grads[1 + k]
        dx_ref[...] = dpad_ref[PAD:PAD + t_, :].astype(dx_ref.dtype)
        dw_ref[...] = dw

    ncb = 3 * DN_HEADS
    col = pl.BlockSpec((t_, DN_DIM), lambda j: (0, j))
    wsp = pl.BlockSpec((DN_CONV, DN_DIM), lambda j: (0, j))
    return pl.pallas_call(
        body, name="prep_bwd", grid=(ncb,), in_specs=[col, wsp, col, ANY], out_specs=[col, wsp],
        out_shape=[jax.ShapeDtypeStruct(dpm.shape, dpm.dtype), jax.ShapeDtypeStruct((DN_CONV, ncb * DN_DIM), F32)],
        scratch_shapes=[pltpu.VMEM((t_ + 2 * PAD, DN_DIM), F32), pltpu.VMEM((t_ + 2 * PAD, DN_DIM), F32)],
        input_output_aliases={3: 0}, compiler_params=_params(1))(pm, conv, dout, dpm)


def _dn_chunk(sgns, qs, ks, vs, grows, brows, tsaved=None, with_t=False):
    c = qs[0].shape[0]
    i = lax.broadcasted_iota(jnp.int32, (c, c), 0)
    j = lax.broadcasted_iota(jnp.int32, (c, c), 1)
    eye = i == j
    incl = {s: (i - j) * int(s) >= 0 for s in set(sgns)}
    strict = {s: (i - j) * int(s) > 0 for s in set(sgns)}
    gcs = [jnp.sum(jnp.where(incl[s], g, 0.0), axis=1, keepdims=True) for s, g in zip(sgns, grows)]
    grs = [jnp.sum(jnp.where(eye, gc, 0.0), axis=0, keepdims=True) for gc in gcs]
    bcs = [jnp.sum(jnp.where(eye, b, 0.0), axis=1, keepdims=True) for b in brows]
    gls = [jnp.sum(g, axis=1, keepdims=True) for g in grows]
    decs = [jnp.exp(jnp.where(incl[s], gc - gr, -1e30)) for s, gc, gr in zip(sgns, gcs, grs)]
    kks = [_nt(k, k) for k in ks]
    amats = tuple(jnp.where(strict[s], bc * kk * dec, 0.0) for s, bc, kk, dec in zip(sgns, bcs, kks, decs))
    tinvs = _inv(amats) if tsaved is None else _inv_saved(amats, tsaved)
    egcs = [jnp.exp(gc) for gc in gcs]
    us = [_nn(t, v * bc) for t, v, bc in zip(tinvs, vs, bcs)]
    ws = [_nn(t, k * (bc * egc)) for t, k, bc, egc in zip(tinvs, ks, bcs, egcs)]
    qks = [_nt(q, k) * dec for q, k, dec in zip(qs, ks, decs)]
    qds = [q * egc for q, egc in zip(qs, egcs)]
    kds = [k * jnp.exp(gl - gc) for k, gl, gc in zip(ks, gls, gcs)]
    res = tuple(us), tuple(ws), tuple(qks), tuple(qds), tuple(kds)
    return res + (tinvs,) if with_t else res


def _dn_step(us, ws, qks, qds, kds, grows, ss):
    gls = [jnp.exp(jnp.sum(g, axis=1, keepdims=True)) for g in grows]
    c = ws[0].shape[0]
    both = [_nn(jnp.concatenate([w, qd], axis=0), s) for w, qd, s in zip(ws, qds, ss)]
    wss = [b[:c] for b in both]
    qss = [b[c:] for b in both]
    vns = [u - x for u, x in zip(us, wss)]
    outer = [_nn(jnp.concatenate([qk, kd.T], axis=0), vn) for qk, kd, vn in zip(qks, kds, vns)]
    os_ = [a + b[:c] for a, b in zip(qss, outer)]
    s2s = [s * gl + b[c:] for s, gl, b in zip(ss, gls, outer)]
    return tuple(os_), tuple(s2s)


def _hs(h):
    return slice(h * DN_DIM, (h + 1) * DN_DIM)


_DIR_SGN = (1, -1)
_A_OUT = 5
_PROBLEMS = [(d, h) for d in range(2) for h in range(DN_HEADS)]
_SGNS = [_DIR_SGN[d] for d, _ in _PROBLEMS]


def _chunk_inputs(q_ref, k_ref, v_ref, g_ref, b_ref):
    heads = lambda ref: tuple(ref[:, _hs(h)].astype(F32) for _, h in _PROBLEMS)
    rows = lambda ref: tuple(ref[d, 0, h:h + 1, :] for d, h in _PROBLEMS)
    return heads(q_ref), heads(k_ref), heads(v_ref), rows(g_ref), rows(b_ref)


def _dn_a_fwd(qkv, grows, brows):
    t_ = qkv.shape[0]
    nch = t_ // CHUNK

    def body(q_ref, k_ref, v_ref, g_ref, b_ref, *outs):
        us, ws, qks, qds, kds, tinvs = _dn_chunk(_SGNS, *_chunk_inputs(q_ref, k_ref, v_ref, g_ref, b_ref), with_t=True)
        for p, (d, h) in enumerate(_PROBLEMS):
            u_ref, w_ref, qk_ref, qd_ref, kd_ref = outs[d * _A_OUT:(d + 1) * _A_OUT]
            u_ref[:, _hs(h)], w_ref[:, _hs(h)], qk_ref[0, h] = us[p], ws[p].astype(_MXU), qks[p].astype(_MXU)
            qd_ref[:, _hs(h)], kd_ref[:, _hs(h)] = qds[p].astype(_MXU), kds[p].astype(_MXU)
            outs[2 * _A_OUT + d][0, h] = tinvs[p]

    rspec = pl.BlockSpec((2, 1, DN_HEADS, CHUNK), lambda c: (0, c, 0, 0))
    big = pl.BlockSpec((CHUNK, D), lambda c: (c, 0))
    qks = pl.BlockSpec((1, DN_HEADS, CHUNK, CHUNK), lambda c: (c, 0, 0, 0))
    bigs = lambda dt: jax.ShapeDtypeStruct((t_, D), dt)
    qksh = lambda dt: jax.ShapeDtypeStruct((nch, DN_HEADS, CHUNK, CHUNK), dt)
    res = pl.pallas_call(
        body, name="dn_a_fwd", grid=(nch,),
        in_specs=[pl.BlockSpec((CHUNK, D), lambda c, p=p: (c, p)) for p in range(3)] + [rspec, rspec],
        out_specs=[big, big, qks, big, big] * 2 + [qks, qks],
        out_shape=[bigs(F32), bigs(_MXU), qksh(_MXU), bigs(_MXU), bigs(_MXU)] * 2 + [qksh(F32)] * 2,
        compiler_params=_params(1))(qkv, qkv, qkv, grows, brows)
    return res[:2 * _A_OUT], res[2 * _A_OUT:]


def _dn_a_bwd(qkv, grows, brows, tinv, dres, dg_b):
    t_ = qkv.shape[0]
    nch = t_ // CHUNK

    def body(q_ref, k_ref, v_ref, g_ref, b_ref, tf_ref, tb_ref, *rest):
        dins, dgb_ref, (dqkv_ref, dg_ref, db_ref) = rest[:2 * _A_OUT], rest[2 * _A_OUT], rest[2 * _A_OUT + 1:]
        tsaved = tuple((tf_ref, tb_ref)[d][0, h] for d, h in _PROBLEMS)
        _, vjp = jax.vjp(functools.partial(_dn_chunk, _SGNS, tsaved=tsaved),
                         *_chunk_inputs(q_ref, k_ref, v_ref, g_ref, b_ref))
        cots = []
        for o in range(_A_OUT):
            cots.append(tuple(dins[d * _A_OUT + o][0, h] if o == 2 else dins[d * _A_OUT + o][:, _hs(h)]
                              for d, h in _PROBLEMS))
        gq, gk, gv, gg, gb = vjp(tuple(cots))
        for p, (d, h) in enumerate(_PROBLEMS):
            dg_ref[d, 0, h:h + 1, :] = gg[p] + dgb_ref[d, 0, h:h + 1, :]
            db_ref[d, 0, h:h + 1, :] = gb[p]
        for h in range(DN_HEADS):
            dqkv_ref[:, _hs(h)] = gq[h] + gq[DN_HEADS + h]
            dqkv_ref[:, _hs(DN_HEADS + h)] = gk[h] + gk[DN_HEADS + h]
            dqkv_ref[:, _hs(2 * DN_HEADS + h)] = gv[h] + gv[DN_HEADS + h]

    rspec = pl.BlockSpec((2, 1, DN_HEADS, CHUNK), lambda c: (0, c, 0, 0))
    big = pl.BlockSpec((CHUNK, D), lambda c: (c, 0))
    qks = pl.BlockSpec((1, DN_HEADS, CHUNK, CHUNK), lambda c: (c, 0, 0, 0))
    rsh = jax.ShapeDtypeStruct(grows.shape, F32)
    return pl.pallas_call(
        body, name="dn_a_bwd", grid=(nch,),
        in_specs=[pl.BlockSpec((CHUNK, D), lambda c, p=p: (c, p)) for p in range(3)] + [rspec, rspec, qks, qks]
        + [big, big, qks, big, big] * 2 + [rspec],
        out_specs=[pl.BlockSpec((CHUNK, 3 * D), lambda c: (c, 0)), rspec, rspec],
        out_shape=[jax.ShapeDtypeStruct((t_, 3 * D), F32), rsh, rsh],
        compiler_params=_params(1))(qkv, qkv, qkv, grows, brows, *tinv, *dres, dg_b)


def _dir_specs(nch):
    def cidx(d):
        return (lambda n: n) if d == 0 else (lambda n: nch - 1 - n)
    out = []
    for d in range(2):
        ci = cidx(d)
        big = pl.BlockSpec((CHUNK, D), lambda n, ci=ci: (ci(n), 0))
        qks = pl.BlockSpec((1, DN_HEADS, CHUNK, CHUNK), lambda n, ci=ci: (ci(n), 0, 0, 0))
        row = pl.BlockSpec((1, 1, DN_HEADS, CHUNK), lambda n, ci=ci, d=d: (d, ci(n), 0, 0))
        st = pl.BlockSpec((1, DN_HEADS, DN_DIM, DN_DIM), lambda n, ci=ci: (ci(n), 0, 0, 0))
        out.append(dict(big=big, qk=qks, row=row, st=st))
    return out


def _step_inputs(ins, per_dir):
    def pick(o):
        if o == 2:
            return tuple(ins[d * per_dir + o][0, h].astype(F32) for d, h in _PROBLEMS)
        if o == 5:
            return tuple(ins[d * per_dir + o][0, 0, h:h + 1, :] for d, h in _PROBLEMS)
        return tuple(ins[d * per_dir + o][:, _hs(h)].astype(F32) for d, h in _PROBLEMS)
    return [pick(o) for o in range(6)]


def _dn_b_fwd(ares, grows):
    t_ = ares[0].shape[0]
    nch = t_ // CHUNK
    sp = _dir_specs(nch)

    def body(*refs):
        ins, outs, s_ref = refs[:12], refs[12:16], refs[16]

        @pl.when(pl.program_id(0) == 0)
        def _():
            s_ref[...] = jnp.zeros_like(s_ref)

        ss = tuple(s_ref[p] for p in range(len(_PROBLEMS)))
        os_, s2s = _dn_step(*_step_inputs(ins, 6), ss)
        for p, (d, h) in enumerate(_PROBLEMS):
            outs[2 + d][0, h] = ss[p]
            outs[d][:, _hs(h)] = os_[p]
            s_ref[p] = s2s[p]

    in_specs, args = [], []
    for d in range(2):
        in_specs += [sp[d]["big"], sp[d]["big"], sp[d]["qk"], sp[d]["big"], sp[d]["big"], sp[d]["row"]]
        args += list(ares[d * _A_OUT:(d + 1) * _A_OUT]) + [grows]
    stsh = jax.ShapeDtypeStruct((nch, DN_HEADS, DN_DIM, DN_DIM), F32)
    osh = jax.ShapeDtypeStruct((t_, D), F32)
    return pl.pallas_call(
        body, name="dn_b_fwd", grid=(nch,), in_specs=in_specs,
        out_specs=[sp[0]["big"], sp[1]["big"], sp[0]["st"], sp[1]["st"]], out_shape=[osh, osh, stsh, stsh],
        scratch_shapes=[pltpu.VMEM((2 * DN_HEADS, DN_DIM, DN_DIM), F32)],
        compiler_params=_params(1))(*args)


def _dn_b_bwd(ares, grows, st_f, st_b, do):
    t_ = ares[0].shape[0]
    nch = t_ // CHUNK
    sp = _dir_specs(nch)
    rsp = [sp[1], sp[0]]

    def body(*refs):
        ins, outs, ds_ref = refs[:16], refs[16:28], refs[28]

        @pl.when(pl.program_id(0) == 0)
        def _():
            ds_ref[...] = jnp.zeros_like(ds_ref)

        ss = tuple(ins[d * 8 + 6][0, h] for d, h in _PROBLEMS)
        _, vjp = jax.vjp(_dn_step, *_step_inputs(ins, 8), ss)
        dos = tuple(ins[d * 8 + 7][:, _hs(h)] for d, h in _PROBLEMS)
        grads = vjp((dos, tuple(ds_ref[p] for p in range(len(_PROBLEMS)))))
        for p, (d, h) in enumerate(_PROBLEMS):
            du_ref, dw_ref, dqk_ref, dqd_ref, dkd_ref, dg_ref = outs[d * 6:(d + 1) * 6]
            du_ref[:, _hs(h)], dw_ref[:, _hs(h)], dqk_ref[0, h] = grads[0][p], grads[1][p], grads[2][p]
            dqd_ref[:, _hs(h)], dkd_ref[:, _hs(h)] = grads[3][p], grads[4][p]
            dg_ref[0, 0, h:h + 1, :] = grads[5][p]
            ds_ref[p] = grads[6][p]

    in_specs, args, out_specs, out_shape = [], [], [], []
    big_sh = jax.ShapeDtypeStruct((t_, D), F32)
    qk_sh = jax.ShapeDtypeStruct((nch, DN_HEADS, CHUNK, CHUNK), F32)
    row_sh = jax.ShapeDtypeStruct((1, nch, DN_HEADS, CHUNK), F32)
    for d in range(2):
        s = rsp[d]
        row0 = pl.BlockSpec((1, 1, DN_HEADS, CHUNK), lambda m, d=d: (0, (nch - 1 - m) if d == 0 else m, 0, 0))
        rowd = pl.BlockSpec((1, 1, DN_HEADS, CHUNK), lambda m, d=d: (d, (nch - 1 - m) if d == 0 else m, 0, 0))
        in_specs += [s["big"], s["big"], s["qk"], s["big"], s["big"], rowd, s["st"], s["big"]]
        args += list(ares[d * _A_OUT:(d + 1) * _A_OUT]) + [grows, (st_f, st_b)[d], do]
        out_specs += [s["big"], s["big"], s["qk"], s["big"], s["big"], row0]
        out_shape += [big_sh, big_sh, qk_sh, big_sh, big_sh, row_sh]
    res = pl.pallas_call(
        body, name="dn_b_bwd", grid=(nch,), in_specs=in_specs, out_specs=out_specs, out_shape=out_shape,
        scratch_shapes=[pltpu.VMEM((2 * DN_HEADS, DN_DIM, DN_DIM), F32)],
        compiler_params=_params(1))(*args)
    dares = list(res[0:5]) + list(res[6:11])
    return dares, jnp.concatenate([res[5], res[11]], axis=0)


@jax.custom_vjp
def _rot_half(x):
    half, width = SW_DIM // 2, x.shape[1]
    first = lax.broadcasted_iota(jnp.int32, x.shape, 1) % SW_DIM < half
    return jnp.where(first, -pltpu.roll(x, width - half, axis=1), pltpu.roll(x, half, axis=1))


_rot_half.defvjp(lambda x: (_rot_half(x), None), lambda _, g: (-_rot_half(g),))


def _rope(x, c, s):
    reps = x.shape[1] // c.shape[1]
    return x * jnp.tile(c, (1, reps)) + _rot_half(x) * jnp.tile(s, (1, reps))


def _rope_t(g, c, s):
    reps = g.shape[1] // c.shape[1]
    return g * jnp.tile(c, (1, reps)) - _rot_half(g * jnp.tile(s, (1, reps)))


_SW_SCALE = SW_DIM ** -0.5
_KV_HEADS = [[kvh * SW_GRP + g for g in range(SW_GRP)] for kvh in range(SW_KV)]


def _by_group(x):
    return [jnp.concatenate([x[:, h * SW_DIM:(h + 1) * SW_DIM] for h in hs], axis=0) for hs in _KV_HEADS]


def _from_groups(xs):
    return jnp.concatenate([x[g * SW_BLOCK:(g + 1) * SW_BLOCK] for x in xs for g in range(SW_GRP)], axis=-1)


def _attn_probs(blk, t_, cq, sq, ck, sk, q, kall, sinks):
    qgs = _by_group(_rope(q, cq, sq))
    kr = _rope(kall, ck, sk)
    khs = [kr[:, kvh * SW_DIM:(kvh + 1) * SW_DIM] for kvh in range(SW_KV)]
    nq, nk = SW_GRP * SW_BLOCK, 3 * SW_BLOCK
    qpos = lax.broadcasted_iota(jnp.int32, (nq, nk), 0) % SW_BLOCK
    krel = lax.broadcasted_iota(jnp.int32, (nq, nk), 1) - SW_BLOCK
    kglob = krel + blk * SW_BLOCK
    valid = (jnp.abs(qpos - krel) <= SW_BLOCK) & (kglob >= 0) & (kglob < t_)
    ss = [jnp.where(valid, _nt(qg * _SW_SCALE, kh), -1e30) for qg, kh in zip(qgs, khs)]
    snks = [jnp.concatenate([jnp.broadcast_to(sinks[:, h:h + 1], (SW_BLOCK, 1)) for h in hs], axis=0) for hs in _KV_HEADS]
    ms = [jnp.maximum(jnp.max(s, axis=-1, keepdims=True), snk) for s, snk in zip(ss, snks)]
    es = [jnp.exp(s - m) for s, m in zip(ss, ms)]
    esnks = [jnp.exp(snk - m) for snk, m in zip(snks, ms)]
    invs = [1.0 / (jnp.sum(e, axis=-1, keepdims=True) + esnk) for e, esnk in zip(es, esnks)]
    ps = [e * inv for e, inv in zip(es, invs)]
    return qgs, khs, ps, [esnk * inv for esnk, inv in zip(esnks, invs)]


def _attn_f(blk, t_, cq, sq, ck, sk, q, kp, ko, kn, vp, vo, vn, sinks):
    _, _, ps, psinks = _attn_probs(blk, t_, cq, sq, ck, sk, q, jnp.concatenate([kp, ko, kn], axis=0), sinks)
    vall = jnp.concatenate([vp, vo, vn], axis=0)
    ps = [p.astype(_MXU) for p in ps]
    kept = [jnp.concatenate([p, jnp.broadcast_to(psink.astype(_MXU), (p.shape[0], 128))], axis=-1)
            for p, psink in zip(ps, psinks)]
    return _from_groups([_nn(p, vall[:, kvh * SW_DIM:(kvh + 1) * SW_DIM]) for kvh, p in enumerate(ps)]), kept


def _attn_bwd_f(cq, sq, ck, sk, q, kp, ko, kn, vp, vo, vn, do, kept):
    nk = 3 * SW_BLOCK
    ps = [x[:, :nk] for x in kept]
    psinks = [x[:, nk:nk + 1].astype(F32) for x in kept]
    qgs = _by_group(_rope(q, cq, sq))
    kr = _rope(jnp.concatenate([kp, ko, kn], axis=0), ck, sk)
    khs = [kr[:, kvh * SW_DIM:(kvh + 1) * SW_DIM] for kvh in range(SW_KV)]
    vall = jnp.concatenate([vp, vo, vn], axis=0)
    vhs = [vall[:, kvh * SW_DIM:(kvh + 1) * SW_DIM] for kvh in range(SW_KV)]
    dogs = _by_group(do)
    dvs = [_tn(p, dog) for p, dog in zip(ps, dogs)]
    ps = [p.astype(F32) for p in ps]
    dps = [_nt(dog * _SW_SCALE, vh) for dog, vh in zip(dogs, vhs)]
    deltas = [jnp.sum(p * dp, axis=-1, keepdims=True) for p, dp in zip(ps, dps)]
    dss = [p * (dp - delta) for p, dp, delta in zip(ps, dps, deltas)]
    dqr = _from_groups([_nn(ds, kh) for ds, kh in zip(dss, khs)])
    dkr = jnp.concatenate([_tn(ds, qg) for ds, qg in zip(dss, qgs)], axis=-1)
    dsnk = [-(psink * delta) * (1.0 / _SW_SCALE) for psink, delta in zip(psinks, deltas)]
    dsinks = jnp.concatenate([jnp.sum(d[g * SW_BLOCK:(g + 1) * SW_BLOCK], axis=0, keepdims=True)
                              for d in dsnk for g in range(SW_GRP)], axis=1)
    dq, dk, dv = _rope_t(dqr, cq, sq), _rope_t(dkr, ck, sk), jnp.concatenate(dvs, axis=-1)
    blocks = lambda a: [a[j * SW_BLOCK:(j + 1) * SW_BLOCK] for j in range(3)]
    return [dq] + blocks(dk) + blocks(dv) + [dsinks]


def _attn_specs(nb):
    prv = lambda i: jnp.maximum(i - 1, 0)
    nxt = lambda i: jnp.minimum(i + 1, nb - 1)
    rows = [lambda i: i, prv, lambda i: i, nxt]
    tab = [pl.BlockSpec((SW_BLOCK, 128), lambda i, r=r: (r(i), 0)) for r in rows]
    qs = pl.BlockSpec((SW_BLOCK, SW_HEADS * SW_DIM), lambda i: (i, O_QS // (SW_HEADS * SW_DIM)))
    kw = SW_KV * SW_DIM
    ks = [pl.BlockSpec((SW_BLOCK, kw), lambda i, r=r: (r(i), O_KS // kw)) for r in rows[1:]]
    vs = [pl.BlockSpec((SW_BLOCK, kw), lambda i, r=r: (r(i), O_VS // kw)) for r in rows[1:]]
    return tab, qs, ks, vs


def _attn_tables(refs):
    cq, cp, co, cn, sq, sp_, so, sn = [r[...] for r in refs]
    return cq, sq, jnp.concatenate([cp, co, cn], axis=0), jnp.concatenate([sp_, so, sn], axis=0)


_P_BLOCK = (None, SW_KV, SW_GRP * SW_BLOCK, 3 * SW_BLOCK + 128)


def _attn_fwd(pm, cos, sin, sinks):
    t_ = pm.shape[0]
    nb = t_ // SW_BLOCK
    tab, qs, ks, vs = _attn_specs(nb)

    def body(*refs):
        tabs = _attn_tables(refs[:8])
        vals = [r[...] for r in refs[8:16]]
        out, ps = _attn_f(pl.program_id(0), t_, *tabs, *vals)
        refs[16][...] = out.astype(refs[16].dtype)
        for kvh, p in enumerate(ps):
            refs[17][kvh] = p

    return pl.pallas_call(
        body, name="attn_fwd", grid=(nb,), in_specs=tab + tab + [qs] + ks + vs + [_full(sinks)],
        out_specs=[pl.BlockSpec((SW_BLOCK, D), lambda i: (i, 0)), pl.BlockSpec(_P_BLOCK, lambda i: (i, 0, 0, 0))],
        out_shape=[jax.ShapeDtypeStruct((t_, D), _MXU), jax.ShapeDtypeStruct((nb,) + _P_BLOCK[1:], _MXU)],
        compiler_params=_params(1))(*([cos] * 4), *([sin] * 4), pm, pm, pm, pm, pm, pm, pm, sinks)


def _attn_bwd(pm, cos, sin, sinks, probs, do, dpm):
    t_ = pm.shape[0]
    nb = t_ // SW_BLOCK
    tab, qs, ks, vs = _attn_specs(nb)
    kw = SW_KV * SW_DIM

    def body(*refs):
        tabs = _attn_tables(refs[:8])
        vals = [r[...] for r in refs[8:15]]
        p_ref, do_ref, outs = refs[16], refs[17], refs[19:]
        grads = _attn_bwd_f(*tabs, *vals, do_ref[...], [p_ref[kvh] for kvh in range(SW_KV)])
        for o_ref, g in zip(outs[:7], grads[:7]):
            o_ref[...] = g.astype(o_ref.dtype)

        @pl.when(pl.program_id(0) == 0)
        def _():
            outs[7][...] = jnp.zeros_like(outs[7])
        outs[7][...] += grads[7]

    own = lambda w: pl.BlockSpec((SW_BLOCK, w), lambda i: (i, 0))
    return pl.pallas_call(
        body, name="attn_bwd", grid=(nb,),
        in_specs=tab + tab + [qs] + ks + vs + [_full(sinks), pl.BlockSpec(_P_BLOCK, lambda i: (i, 0, 0, 0)), own(D), ANY],
        out_specs=[pl.BlockSpec((SW_BLOCK, D), lambda i: (i, O_QS // D))] + [own(kw)] * 6 + [_full(sinks)],
        out_shape=[jax.ShapeDtypeStruct(dpm.shape, dpm.dtype)] + [jax.ShapeDtypeStruct((t_, kw), F32)] * 6
        + [jax.ShapeDtypeStruct(sinks.shape, F32)],
        input_output_aliases={18: 0},
        compiler_params=_params(1))(*([cos] * 4), *([sin] * 4), pm, pm, pm, pm, pm, pm, pm, sinks, probs, do, dpm)


def _band_sum(kparts, vparts, dpm):
    t_, kw = kparts[1].shape
    nb = t_ // SW_BLOCK

    def body(kp, ko, kn, vp, vo, vn, _, out_ref):
        j = pl.program_id(0)
        band = lambda p, o, n: o[...] + jnp.where(j + 1 < nb, p[...], 0.0) + jnp.where(j > 0, n[...], 0.0)
        out_ref[...] = jnp.concatenate([band(kp, ko, kn), band(vp, vo, vn)], axis=-1).astype(out_ref.dtype)

    specs = [pl.BlockSpec((SW_BLOCK, kw), lambda j: (jnp.minimum(j + 1, nb - 1), 0)),
             pl.BlockSpec((SW_BLOCK, kw), lambda j: (j, 0)),
             pl.BlockSpec((SW_BLOCK, kw), lambda j: (jnp.maximum(j - 1, 0), 0))]
    return pl.pallas_call(
        body, name="band_sum", grid=(nb,), in_specs=specs * 2 + [ANY],
        out_specs=pl.BlockSpec((SW_BLOCK, 2 * kw), lambda j: (j, O_KS // (2 * kw))),
        out_shape=jax.ShapeDtypeStruct(dpm.shape, dpm.dtype), input_output_aliases={6: 0},
        compiler_params=_params(1))(*kparts, *vparts, dpm)


def _loss_head(y, target, tb=256):
    t_ = y.shape[0]

    def body(y_ref, t_ref, dy_ref, acc_ref):
        err = y_ref[...] - t_ref[...]
        dy_ref[...] = err * (1.0 / D)
        sq = (err * err).reshape(tb // 8, 8, D).sum(axis=0)
        part = sq[:, 0:128]
        for c in range(1, D // 128):
            part = part + sq[:, c * 128:(c + 1) * 128]

        @pl.when(pl.program_id(0) == 0)
        def _():
            acc_ref[...] = jnp.zeros_like(acc_ref)
        acc_ref[...] += part

    row = pl.BlockSpec((tb, D), lambda i: (i, 0))
    return pl.pallas_call(
        body, name="loss_head", grid=(t_ // tb,), in_specs=[row, row],
        out_specs=[row, pl.BlockSpec((8, 128), lambda i: (0, 0))],
        out_shape=[jax.ShapeDtypeStruct((t_, D), F32), jax.ShapeDtypeStruct((8, 128), F32)],
        compiler_params=_params(1))(y, target)


def _adam_math(w, g, m, v):
    bc1 = 1.0 - ADAM_B1 ** ADAM_STEP
    bc2 = 1.0 - ADAM_B2 ** ADAM_STEP
    nm = ADAM_B1 * m + (1.0 - ADAM_B1) * g
    nv = ADAM_B2 * v + (1.0 - ADAM_B2) * (g * g)
    return -ADAM_LR * ((nm / bc1) / (jnp.sqrt(nv / bc2) + ADAM_EPS) + ADAM_WD * w), nm, nv


def _adamw_layer(name, l, w, g, m, v, carry, after):
    nl, r, c = w.shape
    g2 = g.reshape(r, -1)
    gc = g2.shape[1]
    w2, m2, v2 = [a.reshape(nl * r, c) for a in (w, m, v)]
    tb = r
    while tb * gc * 4 > (1 << 20) and tb % 16 == 0:
        tb //= 2
    nb = r // tb

    def body(w_ref, g_ref, m_ref, v_ref, *rest):
        go_ref, d_ref, nm_ref, nv_ref = rest[-4:]
        gv = g_ref[...][:, :c]
        go_ref[...] = gv
        d_ref[...], nm_ref[...], nv_ref[...] = _adam_math(w_ref[...], gv, m_ref[...], v_ref[...])

    spec = pl.BlockSpec((tb, c), lambda i: (l * nb + i, 0))
    carried = list(carry) if carry is not None else []
    outs = pl.pallas_call(
        body, name=name, grid=(nb,),
        in_specs=[spec, pl.BlockSpec((tb, gc), lambda i: (i, 0)), spec, spec] + [ANY] * (len(carried) + 1),
        out_specs=[spec] * 4, out_shape=[jax.ShapeDtypeStruct((nl * r, c), F32)] * 4,
        input_output_aliases={4 + k: k for k in range(len(carried))},
        compiler_params=_params(1))(w2, g2, m2, v2, *[a.reshape(nl * r, c) for a in carried], after)
    return tuple(o.reshape(w.shape) for o in outs)


def _adamw_layer_t(name, l, w, g, m, v, carry, after):
    nl, r, c = w.shape
    g2 = g.reshape(r, -1)
    gc = g2.shape[1]
    wt, mt, vt = [jnp.swapaxes(a, 1, 2).reshape(nl * c, r) for a in (w, m, v)]

    def body(w_ref, g_ref, m_ref, v_ref, *rest):
        go_ref, d_ref, nm_ref, nv_ref = rest[-4:]
        gv = g_ref[...].T[:c]
        go_ref[...] = gv
        d_ref[...], nm_ref[...], nv_ref[...] = _adam_math(w_ref[...], gv, m_ref[...], v_ref[...])

    spec = pl.BlockSpec((c, 128), lambda j: (l, j))
    carried = list(carry) if carry is not None else []
    return tuple(pl.pallas_call(
        body, name=name, grid=(r // 128,),
        in_specs=[spec, pl.BlockSpec((128, gc), lambda j: (j, 0)), spec, spec] + [ANY] * (len(carried) + 1),
        out_specs=[spec] * 4, out_shape=[jax.ShapeDtypeStruct((nl * c, r), F32)] * 4,
        input_output_aliases={4 + k: k for k in range(len(carried))},
        compiler_params=_params(1))(wt, g2, mt, vt, *carried, after))


def _adamw(name, w, g, m, v):
    shape = w.shape
    cols = shape[-1]
    rows = w.size // cols
    w2, g2, m2, v2 = [a.reshape(rows, cols) for a in (w, g, m, v)]
    tb = rows
    while tb * cols * 4 > (1 << 20) and tb % 16 == 0:
        tb //= 2

    def body(w_ref, g_ref, m_ref, v_ref, d_ref, nm_ref, nv_ref):
        d_ref[...], nm_ref[...], nv_ref[...] = _adam_math(w_ref[...], g_ref[...], m_ref[...], v_ref[...])

    spec = pl.BlockSpec((tb, cols), lambda i: (i, 0))
    sh = jax.ShapeDtypeStruct((rows, cols), F32)
    outs = pl.pallas_call(body, name=name, grid=(rows // tb,), in_specs=[spec] * 4, out_specs=[spec] * 3,
                          out_shape=[sh] * 3, compiler_params=_params(1))(w2, g2, m2, v2)
    return [o.reshape(shape) for o in outs]


def _to_rows(bg, col0):
    t_ = bg.shape[0]
    a = bg[:, col0:col0 + 2 * DN_HEADS].reshape(t_ // CHUNK, CHUNK, 2, DN_HEADS)
    return jnp.transpose(a, (2, 0, 3, 1))


def _from_rows(db, dg):
    nch = db.shape[1]
    back = lambda a: jnp.transpose(a, (1, 3, 0, 2)).reshape(nch * CHUNK, 2 * DN_HEADS)
    return jnp.pad(jnp.concatenate([back(db), back(dg)], axis=1), ((0, 0), (0, 128 - 4 * DN_HEADS)))


def _layer_fwd(x, xm, w, rest, late, cos, sin):
    t_ = x.shape[0]
    nb = min(1024, t_)
    pm = _mm(xm, w["in_main"], name="mm_in", tn=1536)
    pbg = _mm(xm, w["in_bg"], name="mm_in_bg")
    w = {**w, **rest(pbg)}
    qkv = _prep_fwd(pm, w["conv"])
    bg, = _stage_fwd(_bg_f, "bg_fwd", [(pbg, 128, 0, False)], [w["arow"], w["dtrow"]], [(128, F32)], nb)
    brows, grows = _to_rows(bg, 0), _to_rows(bg, 2 * DN_HEADS)
    ares, tinv = _dn_a_fwd(qkv, grows, brows)
    o_f, o_b, st_f, st_b = _dn_b_fwd(ares, grows)
    w = {**w, **late(o_f)}

    def gnorm_all(of, ob, z, gnw):
        return jnp.concatenate([_gnorm_f(of[:, _hs(h)], ob[:, _hs(h)], z[:, _hs(h)], gnw)[0]
                                for h in range(DN_HEADS)], axis=-1)

    odn, ya = _mm_pre(gnorm_all, [(o_f, D, 0), (o_b, D, 0), (pm, D, O_Z // D)], [w["gnw"]], w["a"], name="mm_a_gnorm")
    osw, probs = _attn_fwd(pm, cos, sin, w["sinks"])
    same, first = (lambda j: j), (lambda j: 0)
    full = lambda dt: (D, D, dt, first, None)
    yb, merged = _mm_fused(
        osw, w["b"], lambda acc, ya_, ga, gb: (acc,) + _merge_f(ya_, acc, ga, gb), [full(F32), full(_MXU)],
        name="mm_b_merge", rows=[(ya, D, first), (pm, D, lambda j: O_GA // D), (pm, D, lambda j: O_GB // D)], tm=512)
    mix, x1, x1m = _mm_fused(
        merged, w["o"], lambda acc, x_, g_, b_: (acc,) + _ln_f2(x_, acc, g_, b_), [full(F32), full(F32), full(_MXU)],
        name="mm_o_ln", rows=[(x, D, first)], params=[w["ln1g"], w["ln1b"]], tm=512)
    gu, hid = _mm_fused(x1m, w["gu"], lambda acc: (acc,) + _swiglu_tile(acc),
                        [(2 * FFN, FFN, F32, same, None), (FFN, FFN // 2, _MXU, same, None)], name="mm_gu_swiglu", tn=FFN)
    ffn, x2, x2m = _mm_fused(
        hid, w["d"], lambda acc, x_, g_, b_: (acc,) + _ln_f2(x_, acc, g_, b_), [full(F32), full(F32), full(_MXU)],
        name="mm_d_ln", rows=[(x1, D, first)], params=[w["ln2g"], w["ln2b"]], tm=512, tk=FFN)
    res = dict(w=w, x=x, xm=xm, pm=pm, pbg=pbg, qkv=qkv, grows=grows, brows=brows, ares=ares, tinv=tinv, o_f=o_f, o_b=o_b, st_f=st_f,
               st_b=st_b, odn=odn, osw=osw, probs=probs, ya=ya, yb=yb, merged=merged, mix=mix, x1=x1, x1m=x1m, gu=gu, hid=hid,
               ffn=ffn)
    return x2, x2m, res


def _layer_bwd(dx2, r, w, cos, sin, mid=None, after=None):
    t_ = dx2.shape[0]
    nb = min(1024, t_)
    pm = r["pm"]
    g = {}
    dx1a, dffn, g["ln2g"], g["ln2b"] = _stage_bwd(
        _ln_f, "ln2_bwd", [(r["x1"], D, 0, False), (r["ffn"], D, 0, False)], [w["ln2g"], w["ln2b"]], [dx2],
        min(512, t_), dtypes=[F32, _MXU], after=after)
    same = lambda j: j
    dgu, = _mm_fused(dffn, w["d"], lambda dhid, gu: jax.vjp(_swiglu_tile, gu)[1]((dhid,)),
                     [(2 * FFN, FFN, _MXU, same, None)], name="mm_d_dx_swiglu", rows=[(r["gu"], FFN, same)], tb=True,
                     tn=FFN // 2)
    g["d"] = _mm(r["hid"], dffn, ta=True, name="mm_d_dw", tm=FFN // 2, out_dtype=_MXU)
    first = lambda j: 0
    full = lambda dt: (D, D, dt, first, None)

    def ln1_back(acc, dx1a_, x_, mix_, g_, b_):
        dx_, dmix_, dg_, db_ = jax.vjp(_ln_f, x_, mix_, g_, b_)[1]((acc + dx1a_,))
        return dx_, dmix_, dg_, db_

    dxa, dmix, g["ln1g"], g["ln1b"] = _mm_fused(
        dgu, w["gu"], ln1_back, [full(F32), full(_MXU)], name="mm_gu_dx_ln", tb=True, tm=256, tk=2 * FFN, n_sums=2,
        rows=[(dx1a, D, first), (r["x"], D, first), (r["mix"], D, first)], params=[w["ln1g"], w["ln1b"]])
    g["gu"] = _mm(r["x1m"], dgu, ta=True, name="mm_gu_dw", tn=FFN // 2, out_dtype=_MXU)
    tok = mid(g["gu"]) if mid is not None else None
    g["o"] = _mm(r["merged"], dmix, ta=True, name="mm_o_dw", after=tok, out_dtype=_MXU)

    def merge_back(acc, ya_, yb_, ga, gb):
        dya_, dyb_, dga, dgb = jax.vjp(_merge_f, ya_, yb_, ga, gb)[1]((acc,))
        return dya_, dyb_, jnp.concatenate([dga, dgb], axis=-1)

    dya, dyb, dpm = _mm_fused(
        dmix, w["o"], merge_back, [full(_MXU), full(_MXU), (N_MAIN, 2 * D, _MXU, lambda j: O_GA // (2 * D), None)],
        name="mm_o_dx_merge", tb=True, tm=512,
        rows=[(r["ya"], D, first), (r["yb"], D, first), (pm, D, lambda j: O_GA // D), (pm, D, lambda j: O_GB // D)])

    def gnorm_back(acc, of, ob, z, gnw):
        dos, dzs, dws = [], [], None
        for h in range(DN_HEADS):
            do_h, _, dz_h, dw_h = jax.vjp(_gnorm_f, of[:, _hs(h)], ob[:, _hs(h)], z[:, _hs(h)], gnw)[1]((acc[:, _hs(h)],))
            dos.append(do_h)
            dzs.append(dz_h)
            dws = dw_h if dws is None else dws + dw_h
        return jnp.concatenate(dos, axis=-1), jnp.concatenate(dzs, axis=-1), dws

    dof, dpm, g["gnw"] = _mm_fused(
        dya, w["a"], gnorm_back, [full(F32), (N_MAIN, D, _MXU, lambda j: O_Z // D, dpm)], name="mm_a_dx_gnorm", tb=True,
        tm=512, rows=[(r["o_f"], D, first), (r["o_b"], D, first), (pm, D, lambda j: O_Z // D)], params=[w["gnw"]],
        n_sums=1)
    g["a"] = _mm(r["odn"], dya, ta=True, name="mm_a_dw", out_dtype=_MXU)
    dosw = _mm(dyb, w["b"], tb=True, name="mm_b_dx")
    g["b"] = _mm(r["osw"], dyb, ta=True, name="mm_b_dw", out_dtype=_MXU)
    ab = _attn_bwd(pm, cos, sin, w["sinks"], r["probs"], dosw, dpm)
    dpm, g["sinks"] = ab[0], ab[7]
    dpm = _band_sum(ab[1:4], ab[4:7], dpm)
    dares, dg_b = _dn_b_bwd(r["ares"], r["grows"], r["st_f"], r["st_b"], dof)
    dqkv, dgrows, dbrows = _dn_a_bwd(r["qkv"], r["grows"], r["brows"], r["tinv"], dares, dg_b)
    dbg = _from_rows(dbrows, dgrows)
    dpbg, g["arow"], g["dtrow"] = _stage_bwd(_bg_f, "bg_bwd", [(r["pbg"], 128, 0, False)], [w["arow"], w["dtrow"]],
                                             [dbg], nb)
    dpm, g["conv"] = _prep_bwd(pm, w["conv"], dqkv, dpm)
    dx = _mm(dpm, w["in_main"], tb=True, add=dxa, name="mm_in_dx", tk=1920)
    dx = _mm(dpbg, w["in_bg"], tb=True, add=dx, name="mm_in_bg_dx")
    g["in_main"] = _mm(r["xm"], dpm, ta=True, name="mm_in_dw", tn=1536, out_dtype=_MXU)
    g["in_bg"] = _mm(r["xm"], dpbg, ta=True, name="mm_in_bg_dw", out_dtype=_MXU)
    return dx, g


def _place():
    return lax.axis_index("x"), lax.axis_index("y"), lax.axis_index("c")


def _colblock(kind, q):
    return q if kind == "col" else (q >> 1) | ((q & 1) << 1)


def _other_chips(x, y):
    return [(1 - x, y), (x, 1 - y), (1 - x, 1 - y)]


HBM = pl.BlockSpec(memory_space=pltpu.HBM)
SEM = pl.BlockSpec(memory_space=pltpu.SEMAPHORE)
DATAFLOW = pltpu.SideEffectType.DATAFLOW_SIDE_EFFECTING


def _hbm(a):
    return pltpu.HBM(a.shape, a.dtype)


def _gather_start(locs, kinds, name):
    n = len(locs)
    locs = list(locs)
    lands = [lax.empty((N_CHIPS,) + a.shape if kind == "row" else (a.shape[0], N_CHIPS * a.shape[1]), a.dtype)
             for a, kind in zip(locs, kinds)]

    def body(*refs):
        loc_refs, land_refs = refs[:n], refs[n:2 * n]
        send_sems, recv_sems, token = refs[2 * n:3 * n], refs[3 * n:4 * n], refs[-1]
        x, y, c = _place()
        myq = 2 * x + y
        for t in range(n):
            width = locs[t].shape[1]
            mine = (land_refs[t].at[myq] if kinds[t] == "row" else
                    land_refs[t].at[:, pl.ds(pl.multiple_of(_colblock(kinds[t], myq) * width, 128), width)])
            for dev in [(cx, cy, c) for cx, cy in _other_chips(x, y)] + [(x, y, 1 - c)]:
                pltpu.make_async_remote_copy(src_ref=loc_refs[t], dst_ref=mine, send_sem=send_sems[t],
                                             recv_sem=recv_sems[t], device_id=dev, device_id_type=MESH).start()
        token[...] = jnp.zeros_like(token)

    res = pl.pallas_call(
        body, name=name,
        out_shape=[pltpu.SemaphoreType.DMA(())] * (2 * n) + [_hbm(a) for a in locs + lands]
        + [jax.ShapeDtypeStruct((8, 128), F32)],
        in_specs=[HBM] * (2 * n), out_specs=[SEM] * (2 * n) + [HBM] * (2 * n) + [pl.BlockSpec(memory_space=pltpu.VMEM)],
        input_output_aliases={t: 2 * n + t for t in range(2 * n)},
        compiler_params=pltpu.CompilerParams(has_side_effects=DATAFLOW),
    )(*[pltpu.with_memory_space_constraint(a, pltpu.HBM) for a in locs + lands])
    return (res[:n], res[n:2 * n], res[2 * n:3 * n], res[3 * n:4 * n]), res[-1]


def _split_wait(handle, after, name):
    send_sems, recv_sems, srcs, lands = handle
    n = len(srcs)

    def body(*refs):
        land_refs, ssems, rsems = refs[n:2 * n], refs[2 * n:3 * n], refs[3 * n:4 * n]
        x, y, c = _place()
        for t in range(n):
            done = pltpu.make_async_remote_copy(
                src_ref=land_refs[t], dst_ref=land_refs[t], send_sem=ssems[t], recv_sem=rsems[t],
                device_id=(x, y, c), device_id_type=MESH)
            done.wait_send()
            done.wait_recv()

    res = pl.pallas_call(
        body, name=name, out_shape=[_hbm(a) for a in list(srcs) + list(lands)],
        in_specs=[HBM] * (2 * n) + [SEM] * (2 * n) + [ANY], out_specs=[HBM] * (2 * n),
        input_output_aliases={t: t for t in range(2 * n)},
        compiler_params=pltpu.CompilerParams(has_side_effects=DATAFLOW),
    )(*srcs, *lands, *send_sems, *recv_sems, after)
    return res[:n], res[n:]


RS_CHUNKS = 2


def _piece(ref, kind, q, hf, pr, pc):
    if kind == "row":
        return ref.at[pl.ds((2 * q + hf) * pr, pr), :]
    return ref.at[pl.ds(hf * pr, pr), pl.ds(pl.multiple_of(_colblock(kind, q) * pc, 128), pc)]


def _rs_sibling_start(ts, meta, name):
    n = len(ts)
    ts = list(ts)
    lands = [lax.empty((N_CHIPS, pr, pc), a.dtype) for a, (_, pr, pc) in zip(ts, meta)]

    def body(*refs):
        t_refs, land_refs = refs[:n], refs[n:2 * n]
        send_sems, recv_sems, token = refs[2 * n:3 * n], refs[3 * n:4 * n], refs[-1]
        x, y, c = _place()
        for t, (kind, pr, pc) in enumerate(meta):
            for q in range(N_CHIPS):
                pltpu.make_async_remote_copy(
                    src_ref=_piece(t_refs[t], kind, q, 1 - c, pr, pc), dst_ref=land_refs[t].at[q],
                    send_sem=send_sems[t], recv_sem=recv_sems[t], device_id=(x, y, 1 - c), device_id_type=MESH).start()
        token[...] = jnp.zeros_like(token)

    res = pl.pallas_call(
        body, name=name,
        out_shape=[pltpu.SemaphoreType.DMA(())] * (2 * n) + [_hbm(a) for a in ts + lands]
        + [jax.ShapeDtypeStruct((8, 128), F32)],
        in_specs=[HBM] * (2 * n), out_specs=[SEM] * (2 * n) + [HBM] * (2 * n) + [pl.BlockSpec(memory_space=pltpu.VMEM)],
        input_output_aliases={t: 2 * n + t for t in range(2 * n)},
        compiler_params=pltpu.CompilerParams(has_side_effects=DATAFLOW),
    )(*[pltpu.with_memory_space_constraint(a, pltpu.HBM) for a in ts + lands])
    return (res[:n], res[n:2 * n], res[2 * n:3 * n], res[3 * n:4 * n]), res[-1]


def _rs_add_sibling(ts, r1s, meta, c):
    n = len(ts)
    in_specs, out_specs, out_shape = [], [], []
    for kind, pr, pc in meta:
        rs = pr // RS_CHUNKS
        if kind == "row":
            in_specs.append(pl.BlockSpec((rs, pc), lambda q, r, c_ref: ((2 * q + c_ref[0]) * RS_CHUNKS + r, 0)))
        else:
            in_specs.append(pl.BlockSpec(
                (rs, pc), lambda q, r, c_ref, kind=kind: (c_ref[0] * RS_CHUNKS + r, _colblock(kind, q))))
    for kind, pr, pc in meta:
        sp = pl.BlockSpec((None, pr // RS_CHUNKS, pc), lambda q, r, c_ref: (q, r, 0))
        in_specs.append(sp)
        out_specs.append(sp)
        out_shape.append(jax.ShapeDtypeStruct((N_CHIPS, pr, pc), BF16))

    def body(c_ref, *refs):
        for t in range(n):
            refs[2 * n + t][...] = (refs[t][...].astype(F32) + refs[n + t][...].astype(F32)).astype(BF16)

    return pl.pallas_call(
        body, name="rs_add_sibling", out_shape=out_shape,
        grid_spec=pltpu.PrefetchScalarGridSpec(num_scalar_prefetch=1, grid=(N_CHIPS, RS_CHUNKS), in_specs=in_specs,
                                               out_specs=out_specs),
        compiler_params=_params(2))(c.reshape(1).astype(jnp.int32), *ts, *r1s)


def _rs_chips_start(ps, meta, name):
    n = len(ps)
    ps = list(ps)
    lands = [lax.empty((N_CHIPS - 1, pr, pc), p.dtype) for p, (_, pr, pc) in zip(ps, meta)]

    def body(*refs):
        p_refs, land_refs = refs[:n], refs[n:2 * n]
        send_sems, recv_sems, token = refs[2 * n:3 * n], refs[3 * n:4 * n], refs[-1]
        x, y, c = _place()
        for t in range(n):
            for j, (cx, cy) in enumerate(_other_chips(x, y)):
                pltpu.make_async_remote_copy(
                    src_ref=p_refs[t].at[2 * cx + cy], dst_ref=land_refs[t].at[j], send_sem=send_sems[t],
                    recv_sem=recv_sems[t], device_id=(cx, cy, c), device_id_type=MESH).start()
        token[...] = jnp.zeros_like(token)

    res = pl.pallas_call(
        body, name=name,
        out_shape=[pltpu.SemaphoreType.DMA(())] * (2 * n) + [_hbm(a) for a in ps + lands]
        + [jax.ShapeDtypeStruct((8, 128), F32)],
        in_specs=[HBM] * (2 * n), out_specs=[SEM] * (2 * n) + [HBM] * (2 * n) + [pl.BlockSpec(memory_space=pltpu.VMEM)],
        input_output_aliases={t: 2 * n + t for t in range(2 * n)},
        compiler_params=pltpu.CompilerParams(has_side_effects=DATAFLOW),
    )(*[pltpu.with_memory_space_constraint(a, pltpu.HBM) for a in ps + lands])
    return (res[:n], res[n:2 * n], res[2 * n:3 * n], res[3 * n:4 * n]), res[-1]


def _rs_add_chips(ps, r2s, meta, myq, c):
    n = len(ps)
    in_specs, out_specs, out_shape = [], [], []
    for _, pr, pc in meta:
        in_specs.append(pl.BlockSpec((None, pr // RS_CHUNKS, pc), lambda r, q_ref, c_ref: (q_ref[0], r, 0)))
    for _, pr, pc in meta:
        in_specs.append(pl.BlockSpec((N_CHIPS - 1, pr // RS_CHUNKS, pc), lambda r, q_ref, c_ref: (0, r, 0)))
        out_specs.append(pl.BlockSpec((None, pr // RS_CHUNKS, pc), lambda r, q_ref, c_ref: (c_ref[0], r, 0)))
        out_shape.append(jax.ShapeDtypeStruct((2, pr, pc), F32))

    def body(q_ref, c_ref, *refs):
        for t in range(n):
            r2 = refs[n + t]
            own = refs[t][...].astype(F32)
            refs[2 * n + t][...] = ((own + r2[0].astype(F32)) + r2[1].astype(F32)) + r2[2].astype(F32)

    return pl.pallas_call(
        body, name="rs_add_chips", out_shape=out_shape,
        grid_spec=pltpu.PrefetchScalarGridSpec(num_scalar_prefetch=2, grid=(RS_CHUNKS,), in_specs=in_specs,
                                               out_specs=out_specs),
        compiler_params=_params(1))(myq.reshape(1).astype(jnp.int32), c.reshape(1).astype(jnp.int32), *ps, *r2s)


def _rs_share_halves(gs):
    n = len(gs)

    def body(*refs):
        g_refs, send_sems, recv_sems = refs[n:2 * n], refs[2 * n], refs[2 * n + 1]
        x, y, c = _place()
        sib = (x, y, 1 - c)
        for t in range(n):
            pltpu.make_async_remote_copy(
                src_ref=g_refs[t].at[c], dst_ref=g_refs[t].at[c], send_sem=send_sems.at[t], recv_sem=recv_sems.at[t],
                device_id=sib, device_id_type=MESH).start()
        for t in range(n):
            cp = pltpu.make_async_remote_copy(
                src_ref=g_refs[t].at[c], dst_ref=g_refs[t].at[1 - c], send_sem=send_sems.at[t],
                recv_sem=recv_sems.at[t], device_id=sib, device_id_type=MESH)
            cp.wait_send()
            cp.wait_recv()

    return pl.pallas_call(
        body, name="rs_share_halves", in_specs=[ANY] * n, out_specs=[ANY] * n,
        out_shape=[jax.ShapeDtypeStruct(g.shape, g.dtype) for g in gs], input_output_aliases={t: t for t in range(n)},
        scratch_shapes=[pltpu.SemaphoreType.DMA((n,)), pltpu.SemaphoreType.DMA((n,))],
    )(*gs)


def _rs_middle(handle, after, meta, c, name):
    ts, r1s = _split_wait(handle, after, name + "_sib_wait")
    ps = _rs_add_sibling(ts, r1s, meta, c)
    return _rs_chips_start(ps, meta, name + "_start")


def _rs_end(handle, after, meta, c, myq, name):
    ps, r2s = _split_wait(handle, after, name + "_wait")
    return _rs_share_halves(_rs_add_chips(ps, r2s, meta, myq, c))


def _allreduce_small(buf):
    rows = buf.shape[0]
    ndev = 8

    def body(b_ref, o_ref, slots, send_sems, recv_sems):
        x, y, c = _place()
        me = 4 * x + 2 * y + c
        slots[me] = b_ref[...]
        for k in range(1, ndev):
            kx, ky, kc = (k >> 2) & 1, (k >> 1) & 1, k & 1
            peer = (x ^ kx, y ^ ky, c ^ kc)
            pltpu.make_async_remote_copy(
                src_ref=b_ref, dst_ref=slots.at[me], send_sem=send_sems.at[k - 1], recv_sem=recv_sems.at[k - 1],
                device_id=peer, device_id_type=MESH).start()
        for k in range(1, ndev):
            kx, ky, kc = (k >> 2) & 1, (k >> 1) & 1, k & 1
            cp = pltpu.make_async_remote_copy(
                src_ref=b_ref, dst_ref=slots.at[me ^ k], send_sem=send_sems.at[k - 1], recv_sem=recv_sems.at[k - 1],
                device_id=(x ^ kx, y ^ ky, c ^ kc), device_id_type=MESH)
            cp.wait_send()
            cp.wait_recv()
        acc = slots[0]
        for s in range(1, ndev):
            acc = acc + slots[s]
        o_ref[...] = acc

    vm = pl.BlockSpec(memory_space=pltpu.VMEM)
    return pl.pallas_call(
        body, name="allreduce_small", in_specs=[vm], out_specs=vm, out_shape=jax.ShapeDtypeStruct((rows, 128), F32),
        scratch_shapes=[pltpu.VMEM((ndev, rows, 128), F32), pltpu.SemaphoreType.DMA((ndev - 1,)),
                        pltpu.SemaphoreType.DMA((ndev - 1,))],
        compiler_params=pltpu.CompilerParams(vmem_limit_bytes=VMEM_LIMIT))(buf)


RS_META = [("col", D // 2, IN_PAD), ("row", D // 8, D), ("row", D // 8, D), ("row", D // 8, D),
           ("colx", D // 2, 2 * FFN // N_CHIPS), ("row", FFN // 8, D)]
SMALL_ROWS = 156


def _rope_tables(t_):
    half = SW_DIM // 2
    inv_freq = ROPE_THETA ** (-jnp.arange(half, dtype=F32) / half)
    ang = jnp.arange(t_, dtype=F32)[:, None] * inv_freq[None, :]
    reps = 128 // half
    return jnp.concatenate([jnp.cos(ang)] * reps, axis=1), jnp.concatenate([jnp.sin(ang)] * reps, axis=1)


def _orig_cols(padded, a, b):
    out = []
    for q in range(N_CHIPS):
        lo, hi = max(a, q * IN_SHARD), min(b, (q + 1) * IN_SHARD)
        if lo < hi:
            out.append(padded[:, q * IN_PAD + lo - q * IN_SHARD:q * IN_PAD + hi - q * IN_SHARD])
    return out


_ORIG_SEGMENTS = [(0, R_BG, "main", 0), (R_BG, R_SW, "bg", 0), (R_SW, R_GATES, "main", O_QS), (R_GATES, IN_COLS, "main", O_GA)]


def _to_padded_shards(main, bg):
    zeros = jnp.zeros((main.shape[0], IN_PAD - IN_SHARD), main.dtype)
    parts = []
    for q in range(N_CHIPS):
        for a, b, src, s0 in _ORIG_SEGMENTS:
            lo, hi = max(a, q * IN_SHARD), min(b, (q + 1) * IN_SHARD)
            if lo < hi:
                parts.append((main if src == "main" else bg)[:, s0 + lo - a:s0 + hi - a])
        parts.append(zeros)
    return jnp.concatenate(parts, axis=1)


def _lane_row(v16):
    return jnp.pad(v16.reshape(1, 2 * DN_HEADS), ((0, 0), (2 * DN_HEADS, 128 - 4 * DN_HEADS)))


def _pack_small(g):
    pad16 = jnp.pad(g["sinks"], ((0, 0), (0, 128 - SW_HEADS)))
    return jnp.concatenate([g["conv"].reshape(-1, 128), g["ln1g"].reshape(-1, 128), g["ln1b"].reshape(-1, 128),
                            g["ln2g"].reshape(-1, 128), g["ln2b"].reshape(-1, 128), g["gnw"], g["arow"], g["dtrow"],
                            pad16], axis=0)


def _unpack_small(buf):
    nconv = DN_CONV * 3 * D // 128
    o = nconv
    out = dict(conv=buf[:o].reshape(DN_CONV, 3 * D))
    for name in ("ln1g", "ln1b", "ln2g", "ln2b"):
        out[name] = buf[o:o + 8].reshape(D)
        o += 8
    out["gnw"] = buf[o]
    out["a_log"] = buf[o + 1, 2 * DN_HEADS:4 * DN_HEADS].reshape(2, DN_HEADS)
    out["dt_bias"] = buf[o + 2, 2 * DN_HEADS:4 * DN_HEADS].reshape(2, DN_HEADS)
    out["sinks"] = buf[o + 3, :SW_HEADS]
    return out


def kernel(x, w_in, conv_w, a_log, dt_bias, dn_norm_w, sinks, w_branch_a, w_branch_b, w_out, ln1_g, ln1_b, w_gate_up, w_down, ln2_g, ln2_b, loss_target, m_w_in, m_conv_w, m_a_log, m_dt_bias, m_dn_norm_w, m_sinks, m_w_branch_a, m_w_branch_b, m_w_out, m_ln1_g, m_ln1_b, m_w_gate_up, m_w_down, m_ln2_g, m_ln2_b, v_w_in, v_conv_w, v_a_log, v_dt_bias, v_dn_norm_w, v_sinks, v_w_branch_a, v_w_branch_b, v_w_out, v_ln1_g, v_ln1_b, v_w_gate_up, v_w_down, v_ln2_g, v_ln2_b):
    xi, yi, ci = _place()
    myq = 2 * xi + yi
    t_ = x.shape[1]
    cos, sin = _rope_tables(t_)

    kinds = ["col", "col", "row", "row", "row", "colx", "row"]
    gathers = []
    for l in range(DEPTH):
        srcs = [jnp.pad(w_in[l].astype(BF16), ((0, 0), (0, IN_PAD - IN_SHARD))), conv_w[l]]
        srcs += [a[l].astype(BF16) for a in (w_branch_a, w_branch_b, w_out, w_gate_up, w_down)]
        if gathers:
            srcs[1] = srcs[1] + gathers[-1][1][0, 0]
        gathers.append(_gather_start(srcs, kinds, "gather_%d_start" % l))

    def in_weights(l, after):
        _, (full_in,) = _split_wait(tuple(part[:1] for part in gathers[l][0]), after, "gather_%d_wait_in" % l)
        cols = lambda a, b: _orig_cols(full_in, a, b)
        return dict(
            in_main=jnp.concatenate(cols(0, R_BG) + cols(R_GATES, IN_COLS) + cols(R_SW, R_GATES), axis=1),
            in_bg=jnp.pad(jnp.concatenate(cols(R_BG, R_SW), axis=1), ((0, 0), (0, 128 - 4 * DN_HEADS))))

    def rest_weights(l, after):
        _, (full_conv,) = _split_wait(tuple(part[1:2] for part in gathers[l][0]), after, "gather_%d_wait_conv" % l)
        return dict(
            conv=full_conv, arow=_lane_row(a_log[l]), dtrow=_lane_row(dt_bias[l]), gnw=dn_norm_w[l][None],
            sinks=sinks[l][None], ln1g=ln1_g[l][None], ln1b=ln1_b[l][None], ln2g=ln2_g[l][None], ln2b=ln2_b[l][None])

    def late_weights(l, after):
        _, (full_a, full_b, full_o, full_gu, full_d) = _split_wait(
            tuple(part[2:] for part in gathers[l][0]), after, "gather_%d_wait_rest" % l)
        return dict(a=full_a.reshape(D, D), b=full_b.reshape(D, D), o=full_o.reshape(D, D), gu=full_gu,
                    d=full_d.reshape(FFN, D))

    h = x[0]
    hm = h.astype(_MXU)
    residuals = []
    for l in range(DEPTH):
        win = in_weights(l, gathers[-1][1] if l == 0 else h)
        h, hm, res = _layer_fwd(h, hm, win, functools.partial(rest_weights, l), functools.partial(late_weights, l),
                                cos, sin)
        residuals.append(res)
    dh, sq = _loss_head(h, loss_target[0])
    loss_rows = jnp.pad(((0.5 / D) * jnp.sum(sq)).reshape(1, 1), ((0, 7), (0, 127)))

    big = [None] * DEPTH
    small = [None] * DEPTH
    hop1 = hop2 = None

    def mid(after):
        nonlocal hop1, hop2
        if hop1 is None:
            return None
        handle, tok = _rs_middle(hop1[1], after, RS_META, ci, "rs_chips_%d" % hop1[0])
        hop1, hop2 = None, (hop1[0], handle)
        return tok

    token = None
    for l in reversed(range(DEPTH)):
        dh, g = _layer_bwd(dh, residuals[l], residuals[l]["w"], cos, sin, mid, token)
        if hop2 is not None:
            big[hop2[0]] = _rs_end(hop2[1], dh, RS_META, ci, myq, "rs_chips_%d" % hop2[0])
        g_in = _to_padded_shards(g["in_main"], g["in_bg"])
        handle, token = _rs_sibling_start([g_in, g["a"], g["b"], g["o"], g["gu"], g["d"]], RS_META, "rs_sib_%d_start" % l)
        hop1 = (l, handle)
        small[l] = _pack_small(g)
    tot = _allreduce_small(jnp.concatenate(small + [loss_rows], axis=0))
    loss = tot[DEPTH * SMALL_ROWS, 0]
    token = mid(tot)
    pending = hop2
    sm = [_unpack_small(tot[l * SMALL_ROWS:(l + 1) * SMALL_ROWS]) for l in range(DEPTH)]
    stack = lambda name: jnp.stack([s[name] for s in sm], axis=0)
    grads = dict(
        conv_w=lax.dynamic_slice_in_dim(stack("conv"), myq * (3 * D // N_CHIPS), 3 * D // N_CHIPS, axis=2),
        a_log=stack("a_log"), dt_bias=stack("dt_bias"), dn_norm_w=stack("gnw"), sinks=stack("sinks"),
        ln1_g=stack("ln1g"), ln1_b=stack("ln1b"), ln2_g=stack("ln2g"), ln2_b=stack("ln2b"))
    weights = dict(w_in=w_in, conv_w=conv_w, a_log=a_log, dt_bias=dt_bias, dn_norm_w=dn_norm_w, sinks=sinks,
                   w_branch_a=w_branch_a, w_branch_b=w_branch_b, w_out=w_out, ln1_g=ln1_g, ln1_b=ln1_b,
                   w_gate_up=w_gate_up, w_down=w_down, ln2_g=ln2_g, ln2_b=ln2_b)
    ms = dict(w_in=m_w_in, conv_w=m_conv_w, a_log=m_a_log, dt_bias=m_dt_bias, dn_norm_w=m_dn_norm_w, sinks=m_sinks,
              w_branch_a=m_w_branch_a, w_branch_b=m_w_branch_b, w_out=m_w_out, ln1_g=m_ln1_g, ln1_b=m_ln1_b,
              w_gate_up=m_w_gate_up, w_down=m_w_down, ln2_g=m_ln2_g, ln2_b=m_ln2_b)
    vs = dict(w_in=v_w_in, conv_w=v_conv_w, a_log=v_a_log, dt_bias=v_dt_bias, dn_norm_w=v_dn_norm_w, sinks=v_sinks,
              w_branch_a=v_w_branch_a, w_branch_b=v_w_branch_b, w_out=v_w_out, ln1_g=v_ln1_g, ln1_b=v_ln1_b,
              w_gate_up=v_w_gate_up, w_down=v_w_down, ln2_g=v_ln2_g, ln2_b=v_ln2_b)
    names = list(weights)
    upd = {n: _adamw("adamw_" + n, weights[n], grads[n], ms[n], vs[n]) for n in grads}
    big_names = ["w_in", "w_branch_a", "w_branch_b", "w_out", "w_gate_up", "w_down"]
    carry = {n: None for n in big_names}

    def update_layer(l):
        for t, n in enumerate(big_names):
            step = _adamw_layer_t if weights[n].shape[-1] % 128 else _adamw_layer
            carry[n] = step("adamw_" + n, l, weights[n], big[l][t], ms[n], vs[n], carry[n], token)

    for l in range(DEPTH - 1, pending[0], -1):
        update_layer(l)
    big[pending[0]] = _rs_end(pending[1], carry[big_names[-1]][1], RS_META, ci, myq, "rs_chips_%d" % pending[0])
    update_layer(pending[0])
    for n in big_names:
        if weights[n].shape[-1] % 128:
            nl, r, c = weights[n].shape
            carry[n] = tuple(jnp.swapaxes(a.reshape(nl, c, r), 1, 2) for a in carry[n])
        grads[n], upd[n] = carry[n][0], carry[n][1:]
    return (loss, dh[None], *[grads[n] for n in names], *[upd[n][0] for n in names], *[upd[n][1] for n in names],
            *[upd[n][2] for n in names])
```
